```python
import jax, jax.numpy as jnp
from jax import lax
import numpy as np

D_MODEL = 1024
BATCH = 8
SEQ = 8192
DEPTH = 4

GRID_W = 64
QBLOCK = 128
NORM_EPS = 1e-6
ROPE_THETA = 10000.0
NEG_INF = -1e30

MLA_HEADS = 8
MLA_Q_RANK = 384
MLA_KV_RANK = 256
MLA_NOPE = 64
MLA_ROPE = 32
MLA_V = 64

DIL_PAIRS = ((128, 1), (512, 4), (2048, 16))
DIL_HALF = 64
DIL_SLOTS = 4
DIL_GROUPS = len(DIL_PAIRS)
DIL_HEADS = DIL_SLOTS * DIL_GROUPS
DIL_HEAD_DIM = 64

GQA_HEADS = 16
GQA_KV_HEADS = 4
GQA_HEAD_DIM = 64

FFN_HIDDEN = -(-8 * D_MODEL // (3 * 256)) * 256

IN_A = MLA_Q_RANK + MLA_KV_RANK + MLA_ROPE
IN_B = 3 * DIL_HEADS * DIL_HEAD_DIM
MIX_IN = IN_A + IN_B
MIX_OUT = MLA_HEADS * MLA_V + DIL_SLOTS * DIL_HEAD_DIM
N_EVEN = (DEPTH + 1) // 2
N_ODD = DEPTH // 2

kernel_name = "hybrid_mla_dilated_axial_gqa_encoder"


def rmsnorm(x, g):
    xf = x.astype(jnp.float32)
    y = xf * lax.rsqrt(jnp.mean(xf * xf, axis=-1, keepdims=True) + NORM_EPS)
    return (y * g.astype(jnp.float32)).astype(x.dtype)


def rope_angles(pos, dim):
    freqs = ROPE_THETA ** (-jnp.arange(0, dim, 2, dtype=jnp.float32) / dim)
    ang = pos.astype(jnp.float32)[:, None] * freqs[None, :]
    return jnp.cos(ang), jnp.sin(ang)


def apply_rope(x, cos, sin):
    xf = x.astype(jnp.float32)
    x1, x2 = jnp.split(xf, 2, axis=-1)
    return jnp.concatenate([x1 * cos - x2 * sin, x1 * sin + x2 * cos], axis=-1).astype(x.dtype)


def blocked_attention(q, k, v, scale):
    b, h, s, dk = q.shape
    g = k.shape[1]
    r = h // g
    nq = s // QBLOCK
    qb = q.reshape(b, g, r, nq, QBLOCK, dk).transpose(3, 0, 1, 2, 4, 5)

    def one_block(qblk):
        sc = jnp.einsum('bgrqd,bgkd->bgrqk', qblk, k, preferred_element_type=jnp.float32) * scale
        p = jax.nn.softmax(sc, axis=-1)
        return jnp.einsum('bgrqk,bgkd->bgrqd', p.astype(v.dtype), v)

    out = lax.map(one_block, qb)
    return out.transpose(1, 2, 3, 0, 4, 5).reshape(b, h, s, -1)


def mla_mixer(h_a, q_norm_g, kv_norm_g, w_uq, w_ukv, cos, sin):
    b, s, _ = h_a.shape
    cq, ckv, k_rope = jnp.split(h_a, [MLA_Q_RANK, MLA_Q_RANK + MLA_KV_RANK], axis=-1)
    cq = rmsnorm(cq, q_norm_g)
    ckv = rmsnorm(ckv, kv_norm_g)
    q = jnp.einsum('bsr,rhd->bhsd', cq, w_uq)
    kv = jnp.einsum('bsr,rhd->bhsd', ckv, w_ukv)
    q_nope, q_rope = q[..., :MLA_NOPE], q[..., MLA_NOPE:]
    k_nope, v = kv[..., :MLA_NOPE], kv[..., MLA_NOPE:]
    q_rope = apply_rope(q_rope, cos, sin)
    k_rope = apply_rope(k_rope, cos, sin)[:, None]
    k_rope = jnp.broadcast_to(k_rope, (b, MLA_HEADS, s, MLA_ROPE))
    qh = jnp.concatenate([q_nope, q_rope], axis=-1)
    kh = jnp.concatenate([k_nope, k_rope], axis=-1)
    o = blocked_attention(qh, kh, v, (MLA_NOPE + MLA_ROPE) ** -0.5)
    return o.transpose(0, 2, 1, 3).reshape(b, s, MLA_HEADS * MLA_V)


def dilated_group_attention(q, k, v, dilation, slopes):
    b, hg, s, dh = q.shape
    P = DIL_HALF
    L = s // dilation
    nb = -(-L // P)
    Lp = nb * P

    def to_residue(t):
        return t.reshape(b, hg, L, dilation, dh).transpose(0, 1, 3, 2, 4)

    qr, kr, vr = to_residue(q), to_residue(k), to_residue(v)
    qb = jnp.pad(qr, ((0, 0), (0, 0), (0, 0), (0, Lp - L), (0, 0))).reshape(b, hg, dilation, nb, P, dh)

    def band(t):
        tb = jnp.pad(t, ((0, 0), (0, 0), (0, 0), (P, Lp - L + P), (0, 0))).reshape(b, hg, dilation, nb + 2, P, dh)
        return jnp.concatenate([tb[:, :, :, :-2], tb[:, :, :, 1:-1], tb[:, :, :, 2:]], axis=4)

    kb, vb = band(kr), band(vr)
    sc = jnp.einsum('bhrnqd,bhrnkd->bhrnqk', qb, kb, preferred_element_type=jnp.float32) * dh ** -0.5

    i = jnp.arange(P)[:, None]
    c = jnp.arange(3 * P)[None, :]
    rel = c - P - i
    uk = jnp.arange(nb)[:, None, None] * P + c - P
    valid = (jnp.abs(rel) <= DIL_HALF)[None] & (uk >= 0) & (uk < L)
    dist = (dilation * jnp.abs(rel)).astype(jnp.float32)
    bias = -slopes.astype(jnp.float32)[:, None, None, None, None] * dist

    sc = jnp.where(valid, sc + bias, NEG_INF)
    m = jnp.max(sc, axis=-1, keepdims=True)
    e = jnp.exp(sc - m)
    den = jnp.sum(e, axis=-1, keepdims=True)
    o = jnp.einsum('bhrnqk,bhrnkd->bhrnqd', (e / den).astype(v.dtype), vb)
    lse = (m + jnp.log(den))[..., 0]

    o = o.reshape(b, hg, dilation, Lp, dh)[:, :, :, :L].transpose(0, 1, 3, 2, 4).reshape(b, hg, s, dh)
    lse = lse.reshape(b, hg, dilation, Lp)[:, :, :, :L].transpose(0, 1, 3, 2).reshape(b, hg, s)
    return o, lse


def dilated_mixer(h_b, slopes):
    b, s, _ = h_b.shape

    def heads(t):
        return t.reshape(b, s, DIL_GROUPS, DIL_SLOTS, DIL_HEAD_DIM).transpose(2, 0, 3, 1, 4)

    q, k, v = (heads(t) for t in jnp.split(h_b, 3, axis=-1))
    outs, lses = [], []
    for g, (_, dil) in enumerate(DIL_PAIRS):
        o, lse = dilated_group_attention(q[g], k[g], v[g], dil, slopes[g * DIL_SLOTS:(g + 1) * DIL_SLOTS])
        outs.append(o)
        lses.append(lse)
    outs = jnp.stack(outs, axis=0)
    wts = jax.nn.softmax(jnp.stack(lses, axis=0), axis=0)
    comb = jnp.sum(wts[..., None] * outs.astype(jnp.float32), axis=0).astype(h_b.dtype)
    return comb.transpose(0, 2, 1, 3).reshape(b, s, DIL_SLOTS * DIL_HEAD_DIM)


def gqa_axial_mixer(hn, w_q, w_kv, q_gain, k_gain, w_o, cos_r, sin_r, cos_c, sin_c):
    b, s, _ = hn.shape
    q = (hn @ w_q).reshape(b, s, GQA_HEADS, GQA_HEAD_DIM).transpose(0, 2, 1, 3)
    kv = (hn @ w_kv).reshape(b, s, 2, GQA_KV_HEADS, GQA_HEAD_DIM)
    k = kv[:, :, 0].transpose(0, 2, 1, 3)
    v = kv[:, :, 1].transpose(0, 2, 1, 3)
    q = rmsnorm(q, q_gain)
    k = rmsnorm(k, k_gain)
    half = GQA_HEAD_DIM // 2

    def axial(t):
        return jnp.concatenate([apply_rope(t[..., :half], cos_r, sin_r),
                                apply_rope(t[..., half:], cos_c, sin_c)], axis=-1)

    o = blocked_attention(axial(q), axial(k), v, GQA_HEAD_DIM ** -0.5)
    return o.transpose(0, 2, 1, 3).reshape(b, s, GQA_HEADS * GQA_HEAD_DIM) @ w_o


def swiglu(hn, w_in, w_out):
    gate, up = jnp.split(hn @ w_in, 2, axis=-1)
    return (jax.nn.silu(gate) * up) @ w_out


def _fwd_setup_inputs(seed: int = 0) -> dict:
    key = jax.random.key(seed)
    ks = jax.random.split(key, 18)
    f32 = jnp.float32

    def w(k, shape, fan_in):
        return jax.random.normal(k, shape, f32) * fan_in ** -0.5

    def gain(k, shape):
        return 1.0 + 0.02 * jax.random.normal(k, shape, f32)

    ne, no = N_EVEN, N_ODD
    return {
        "x": jax.random.normal(ks[0], (BATCH, SEQ, D_MODEL), f32),
        "mix_norm_ab": gain(ks[1], (ne, D_MODEL)),
        "w_in_ab": w(ks[2], (ne, D_MODEL, MIX_IN), D_MODEL),
        "mla_q_norm": gain(ks[3], (ne, MLA_Q_RANK)),
        "mla_kv_norm": gain(ks[4], (ne, MLA_KV_RANK)),
        "mla_w_uq": w(ks[5], (ne, MLA_Q_RANK, MLA_HEADS, MLA_NOPE + MLA_ROPE), MLA_Q_RANK),
        "mla_w_ukv": w(ks[6], (ne, MLA_KV_RANK, MLA_HEADS, MLA_NOPE + MLA_V), MLA_KV_RANK),
        "w_out_ab": w(ks[7], (ne, MIX_OUT, D_MODEL), MIX_OUT),
        "mix_norm_c": gain(ks[8], (no, D_MODEL)),
        "gqa_w_q": w(ks[9], (no, D_MODEL, GQA_HEADS * GQA_HEAD_DIM), D_MODEL),
        "gqa_w_kv": w(ks[10], (no, D_MODEL, 2 * GQA_KV_HEADS * GQA_HEAD_DIM), D_MODEL),
        "gqa_q_norm": gain(ks[11], (no, GQA_HEAD_DIM)),
        "gqa_k_norm": gain(ks[12], (no, GQA_HEAD_DIM)),
        "gqa_w_o": w(ks[13], (no, GQA_HEADS * GQA_HEAD_DIM, D_MODEL), GQA_HEADS * GQA_HEAD_DIM),
        "ffn_norm": gain(ks[14], (DEPTH, D_MODEL)),
        "ffn_w_in": w(ks[15], (DEPTH, D_MODEL, 2 * FFN_HIDDEN), D_MODEL),
        "ffn_w_out": w(ks[16], (DEPTH, FFN_HIDDEN, D_MODEL), FFN_HIDDEN),
        "final_norm": gain(ks[17], (D_MODEL,)),
    }


def _fwd_reference(x, mix_norm_ab, w_in_ab, mla_q_norm, mla_kv_norm, mla_w_uq, mla_w_ukv, w_out_ab,
              mix_norm_c, gqa_w_q, gqa_w_kv, gqa_q_norm, gqa_k_norm, gqa_w_o,
              ffn_norm, ffn_w_in, ffn_w_out, final_norm):
    s = x.shape[1]
    rows = s // GRID_W
    pos = jnp.arange(s)
    cos_t, sin_t = rope_angles(pos, MLA_ROPE)
    row_idx = jnp.broadcast_to(jnp.arange(rows)[:, None], (rows, GRID_W)).reshape(-1)
    col_idx = jnp.broadcast_to(jnp.arange(GRID_W)[None, :], (rows, GRID_W)).reshape(-1)
    cos_r, sin_r = rope_angles(row_idx, GQA_HEAD_DIM // 2)
    cos_c, sin_c = rope_angles(col_idx, GQA_HEAD_DIM // 2)
    slopes = jnp.exp2(-8.0 * jnp.arange(1, DIL_HEADS + 1, dtype=jnp.float32) / DIL_HEADS)

    for layer in range(DEPTH):
        i = layer // 2
        if layer % 2 == 0:
            z = rmsnorm(x, mix_norm_ab[i]) @ w_in_ab[i]
            o_a = mla_mixer(z[..., :IN_A], mla_q_norm[i], mla_kv_norm[i], mla_w_uq[i], mla_w_ukv[i],
                            cos_t, sin_t)
            o_b = dilated_mixer(z[..., IN_A:], slopes)
            x = x + jnp.concatenate([o_a, o_b], axis=-1) @ w_out_ab[i]
        else:
            x = x + gqa_axial_mixer(rmsnorm(x, mix_norm_c[i]), gqa_w_q[i], gqa_w_kv[i], gqa_q_norm[i],
                                    gqa_k_norm[i], gqa_w_o[i], cos_r, sin_r, cos_c, sin_c)
        x = x + swiglu(rmsnorm(x, ffn_norm[layer]), ffn_w_in[layer], ffn_w_out[layer])
    return rmsnorm(x, final_norm)


import jax as _jax
import jax.numpy as _jnp

TWIN_FORMAT = 'train_step'
FWD_PARAMS = ['x', 'mix_norm_ab', 'w_in_ab', 'mla_q_norm', 'mla_kv_norm', 'mla_w_uq', 'mla_w_ukv', 'w_out_ab', 'mix_norm_c', 'gqa_w_q', 'gqa_w_kv', 'gqa_q_norm', 'gqa_k_norm', 'gqa_w_o', 'ffn_norm', 'ffn_w_in', 'ffn_w_out', 'final_norm']
TWIN_WEIGHTS = ['mix_norm_ab', 'w_in_ab', 'mla_q_norm', 'mla_kv_norm', 'mla_w_uq', 'mla_w_ukv', 'w_out_ab', 'mix_norm_c', 'gqa_w_q', 'gqa_w_kv', 'gqa_q_norm', 'gqa_k_norm', 'gqa_w_o', 'ffn_norm', 'ffn_w_in', 'ffn_w_out', 'final_norm']
TWIN_DIFF_INPUT = 'x'
TWIN_INPUTS = ['x', 'mix_norm_ab', 'w_in_ab', 'mla_q_norm', 'mla_kv_norm', 'mla_w_uq', 'mla_w_ukv', 'w_out_ab', 'mix_norm_c', 'gqa_w_q', 'gqa_w_kv', 'gqa_q_norm', 'gqa_k_norm', 'gqa_w_o', 'ffn_norm', 'ffn_w_in', 'ffn_w_out', 'final_norm', 'loss_target', 'm_mix_norm_ab', 'm_w_in_ab', 'm_mla_q_norm', 'm_mla_kv_norm', 'm_mla_w_uq', 'm_mla_w_ukv', 'm_w_out_ab', 'm_mix_norm_c', 'm_gqa_w_q', 'm_gqa_w_kv', 'm_gqa_q_norm', 'm_gqa_k_norm', 'm_gqa_w_o', 'm_ffn_norm', 'm_ffn_w_in', 'm_ffn_w_out', 'm_final_norm', 'v_mix_norm_ab', 'v_w_in_ab', 'v_mla_q_norm', 'v_mla_kv_norm', 'v_mla_w_uq', 'v_mla_w_ukv', 'v_w_out_ab', 'v_mix_norm_c', 'v_gqa_w_q', 'v_gqa_w_kv', 'v_gqa_q_norm', 'v_gqa_k_norm', 'v_gqa_w_o', 'v_ffn_norm', 'v_ffn_w_in', 'v_ffn_w_out', 'v_final_norm']
TWIN_OUTPUTS = ['loss', 'grad_x', 'grad_mix_norm_ab', 'grad_w_in_ab', 'grad_mla_q_norm', 'grad_mla_kv_norm', 'grad_mla_w_uq', 'grad_mla_w_ukv', 'grad_w_out_ab', 'grad_mix_norm_c', 'grad_gqa_w_q', 'grad_gqa_w_kv', 'grad_gqa_q_norm', 'grad_gqa_k_norm', 'grad_gqa_w_o', 'grad_ffn_norm', 'grad_ffn_w_in', 'grad_ffn_w_out', 'grad_final_norm', 'delta_mix_norm_ab', 'delta_w_in_ab', 'delta_mla_q_norm', 'delta_mla_kv_norm', 'delta_mla_w_uq', 'delta_mla_w_ukv', 'delta_w_out_ab', 'delta_mix_norm_c', 'delta_gqa_w_q', 'delta_gqa_w_kv', 'delta_gqa_q_norm', 'delta_gqa_k_norm', 'delta_gqa_w_o', 'delta_ffn_norm', 'delta_ffn_w_in', 'delta_ffn_w_out', 'delta_final_norm', 'new_m_mix_norm_ab', 'new_m_w_in_ab', 'new_m_mla_q_norm', 'new_m_mla_kv_norm', 'new_m_mla_w_uq', 'new_m_mla_w_ukv', 'new_m_w_out_ab', 'new_m_mix_norm_c', 'new_m_gqa_w_q', 'new_m_gqa_w_kv', 'new_m_gqa_q_norm', 'new_m_gqa_k_norm', 'new_m_gqa_w_o', 'new_m_ffn_norm', 'new_m_ffn_w_in', 'new_m_ffn_w_out', 'new_m_final_norm', 'new_v_mix_norm_ab', 'new_v_w_in_ab', 'new_v_mla_q_norm', 'new_v_mla_kv_norm', 'new_v_mla_w_uq', 'new_v_mla_w_ukv', 'new_v_w_out_ab', 'new_v_mix_norm_c', 'new_v_gqa_w_q', 'new_v_gqa_w_kv', 'new_v_gqa_q_norm', 'new_v_gqa_k_norm', 'new_v_gqa_w_o', 'new_v_ffn_norm', 'new_v_ffn_w_in', 'new_v_ffn_w_out', 'new_v_final_norm']
TWIN_LEAF_KINDS = {'loss': 'loss', 'grad_x': 'grad_x', 'grad_mix_norm_ab': 'grad_w', 'grad_w_in_ab': 'grad_w', 'grad_mla_q_norm': 'grad_w', 'grad_mla_kv_norm': 'grad_w', 'grad_mla_w_uq': 'grad_w', 'grad_mla_w_ukv': 'grad_w', 'grad_w_out_ab': 'grad_w', 'grad_mix_norm_c': 'grad_w', 'grad_gqa_w_q': 'grad_w', 'grad_gqa_w_kv': 'grad_w', 'grad_gqa_q_norm': 'grad_w', 'grad_gqa_k_norm': 'grad_w', 'grad_gqa_w_o': 'grad_w', 'grad_ffn_norm': 'grad_w', 'grad_ffn_w_in': 'grad_w', 'grad_ffn_w_out': 'grad_w', 'grad_final_norm': 'grad_w', 'delta_mix_norm_ab': 'delta_w', 'delta_w_in_ab': 'delta_w', 'delta_mla_q_norm': 'delta_w', 'delta_mla_kv_norm': 'delta_w', 'delta_mla_w_uq': 'delta_w', 'delta_mla_w_ukv': 'delta_w', 'delta_w_out_ab': 'delta_w', 'delta_mix_norm_c': 'delta_w', 'delta_gqa_w_q': 'delta_w', 'delta_gqa_w_kv': 'delta_w', 'delta_gqa_q_norm': 'delta_w', 'delta_gqa_k_norm': 'delta_w', 'delta_gqa_w_o': 'delta_w', 'delta_ffn_norm': 'delta_w', 'delta_ffn_w_in': 'delta_w', 'delta_ffn_w_out': 'delta_w', 'delta_final_norm': 'delta_w', 'new_m_mix_norm_ab': 'new_m', 'new_m_w_in_ab': 'new_m', 'new_m_mla_q_norm': 'new_m', 'new_m_mla_kv_norm': 'new_m', 'new_m_mla_w_uq': 'new_m', 'new_m_mla_w_ukv': 'new_m', 'new_m_w_out_ab': 'new_m', 'new_m_mix_norm_c': 'new_m', 'new_m_gqa_w_q': 'new_m', 'new_m_gqa_w_kv': 'new_m', 'new_m_gqa_q_norm': 'new_m', 'new_m_gqa_k_norm': 'new_m', 'new_m_gqa_w_o': 'new_m', 'new_m_ffn_norm': 'new_m', 'new_m_ffn_w_in': 'new_m', 'new_m_ffn_w_out': 'new_m', 'new_m_final_norm': 'new_m', 'new_v_mix_norm_ab': 'new_v', 'new_v_w_in_ab': 'new_v', 'new_v_mla_q_norm': 'new_v', 'new_v_mla_kv_norm': 'new_v', 'new_v_mla_w_uq': 'new_v', 'new_v_mla_w_ukv': 'new_v', 'new_v_w_out_ab': 'new_v', 'new_v_mix_norm_c': 'new_v', 'new_v_gqa_w_q': 'new_v', 'new_v_gqa_w_kv': 'new_v', 'new_v_gqa_q_norm': 'new_v', 'new_v_gqa_k_norm': 'new_v', 'new_v_gqa_w_o': 'new_v', 'new_v_ffn_norm': 'new_v', 'new_v_ffn_w_in': 'new_v', 'new_v_ffn_w_out': 'new_v', 'new_v_final_norm': 'new_v'}


def _forward(args):
    return _fwd_reference(*[args[k] for k in FWD_PARAMS])


def _output_shape():
    def fwd():
        inp = _fwd_setup_inputs(0)
        return _fwd_reference(*[inp[k] for k in FWD_PARAMS])
    out = _jax.eval_shape(fwd)
    return out.shape, out.dtype

N_MICROBATCH = 1
ADAM_LR = 0.001
ADAM_B1 = 0.9
ADAM_B2 = 0.999
ADAM_EPS = 1e-08
ADAM_WD = 0.01
ADAM_STEP = 10
PER_EXAMPLE_BATCH_AXIS = {'x': 0, 'loss_target': 0}
SHARED_INPUTS = []
_WEIGHT_DTYPES = {'mix_norm_ab': _jnp.float32, 'w_in_ab': _jnp.float32, 'mla_q_norm': _jnp.float32, 'mla_kv_norm': _jnp.float32, 'mla_w_uq': _jnp.float32, 'mla_w_ukv': _jnp.float32, 'w_out_ab': _jnp.float32, 'mix_norm_c': _jnp.float32, 'gqa_w_q': _jnp.float32, 'gqa_w_kv': _jnp.float32, 'gqa_q_norm': _jnp.float32, 'gqa_k_norm': _jnp.float32, 'gqa_w_o': _jnp.float32, 'ffn_norm': _jnp.float32, 'ffn_w_in': _jnp.float32, 'ffn_w_out': _jnp.float32, 'final_norm': _jnp.float32}
MOMENT_SCALE = {'mix_norm_ab': 8.499278e-02, 'w_in_ab': 4.948367e-02, 'mla_q_norm': 4.465288e-02, 'mla_kv_norm': 8.772074e-02, 'mla_w_uq': 3.227788e-02, 'mla_w_ukv': 4.108646e-02, 'w_out_ab': 5.416833e-02, 'mix_norm_c': 4.557014e-02, 'gqa_w_q': 2.338369e-02, 'gqa_w_kv': 5.505040e-02, 'gqa_q_norm': 9.330288e-02, 'gqa_k_norm': 9.552231e-02, 'gqa_w_o': 3.109390e-02, 'ffn_norm': 1.924528e-01, 'ffn_w_in': 7.931901e-02, 'ffn_w_out': 1.295003e-01, 'final_norm': 6.401959e+01}


def _to_microbatches(a, axis):
    t = _jnp.moveaxis(a, axis, 0)
    t = t.reshape((N_MICROBATCH, t.shape[0] // N_MICROBATCH) + t.shape[1:])
    return _jnp.moveaxis(t, 1, axis + 1)


def setup_inputs(seed: int = 0) -> dict:
    inp = _fwd_setup_inputs(seed)
    key = _jax.random.fold_in(_jax.random.key(seed), 7919)
    shape, _ = _output_shape()
    out = dict(inp)
    out["loss_target"] = _jax.random.normal(_jax.random.fold_in(key, 0), shape, _jnp.float32)
    for i, name in enumerate(TWIN_WEIGHTS):
        w = inp[name].astype(_jnp.float32)
        if MOMENT_SCALE is None:
            s = _jnp.sqrt(_jnp.mean(_jnp.square(w)) + 1e-30)
        else:
            s = MOMENT_SCALE[name]
        km, kv = _jax.random.split(_jax.random.fold_in(key, i + 1))
        out[name] = w
        out["m_" + name] = s * _jax.random.normal(km, w.shape, _jnp.float32)
        out["v_" + name] = (s * s) * _jax.random.uniform(kv, w.shape, _jnp.float32, 0.5, 1.5)
    if N_MICROBATCH > 1:
        for name, axis in PER_EXAMPLE_BATCH_AXIS.items():
            out[name] = _to_microbatches(out[name], axis)
    return {'x': out['x'], 'mix_norm_ab': out['mix_norm_ab'], 'w_in_ab': out['w_in_ab'], 'mla_q_norm': out['mla_q_norm'], 'mla_kv_norm': out['mla_kv_norm'], 'mla_w_uq': out['mla_w_uq'], 'mla_w_ukv': out['mla_w_ukv'], 'w_out_ab': out['w_out_ab'], 'mix_norm_c': out['mix_norm_c'], 'gqa_w_q': out['gqa_w_q'], 'gqa_w_kv': out['gqa_w_kv'], 'gqa_q_norm': out['gqa_q_norm'], 'gqa_k_norm': out['gqa_k_norm'], 'gqa_w_o': out['gqa_w_o'], 'ffn_norm': out['ffn_norm'], 'ffn_w_in': out['ffn_w_in'], 'ffn_w_out': out['ffn_w_out'], 'final_norm': out['final_norm'], 'loss_target': out['loss_target'], 'm_mix_norm_ab': out['m_mix_norm_ab'], 'm_w_in_ab': out['m_w_in_ab'], 'm_mla_q_norm': out['m_mla_q_norm'], 'm_mla_kv_norm': out['m_mla_kv_norm'], 'm_mla_w_uq': out['m_mla_w_uq'], 'm_mla_w_ukv': out['m_mla_w_ukv'], 'm_w_out_ab': out['m_w_out_ab'], 'm_mix_norm_c': out['m_mix_norm_c'], 'm_gqa_w_q': out['m_gqa_w_q'], 'm_gqa_w_kv': out['m_gqa_w_kv'], 'm_gqa_q_norm': out['m_gqa_q_norm'], 'm_gqa_k_norm': out['m_gqa_k_norm'], 'm_gqa_w_o': out['m_gqa_w_o'], 'm_ffn_norm': out['m_ffn_norm'], 'm_ffn_w_in': out['m_ffn_w_in'], 'm_ffn_w_out': out['m_ffn_w_out'], 'm_final_norm': out['m_final_norm'], 'v_mix_norm_ab': out['v_mix_norm_ab'], 'v_w_in_ab': out['v_w_in_ab'], 'v_mla_q_norm': out['v_mla_q_norm'], 'v_mla_kv_norm': out['v_mla_kv_norm'], 'v_mla_w_uq': out['v_mla_w_uq'], 'v_mla_w_ukv': out['v_mla_w_ukv'], 'v_w_out_ab': out['v_w_out_ab'], 'v_mix_norm_c': out['v_mix_norm_c'], 'v_gqa_w_q': out['v_gqa_w_q'], 'v_gqa_w_kv': out['v_gqa_w_kv'], 'v_gqa_q_norm': out['v_gqa_q_norm'], 'v_gqa_k_norm': out['v_gqa_k_norm'], 'v_gqa_w_o': out['v_gqa_w_o'], 'v_ffn_norm': out['v_ffn_norm'], 'v_ffn_w_in': out['v_ffn_w_in'], 'v_ffn_w_out': out['v_ffn_w_out'], 'v_final_norm': out['v_final_norm']}


def _loss(weights, diff, rest, loss_target):
    with _jax.named_scope("forward"):
        args = {**rest, TWIN_DIFF_INPUT: diff, **{k: w.astype(_WEIGHT_DTYPES[k]) for k, w in weights.items()}}
        y = _forward(args)
    with _jax.named_scope("loss_head"):
        err = _jnp.square(y.astype(_jnp.float32) - loss_target)
        return 0.5 * _jnp.sum(_jnp.mean(err, axis=-1)) if err.ndim else 0.5 * err


def _adamw(w, g, m, v):
    m = ADAM_B1 * m + (1.0 - ADAM_B1) * g
    v = ADAM_B2 * v + (1.0 - ADAM_B2) * _jnp.square(g)
    m_hat = m / (1.0 - ADAM_B1 ** ADAM_STEP)
    v_hat = v / (1.0 - ADAM_B2 ** ADAM_STEP)
    delta = -ADAM_LR * (m_hat / (_jnp.sqrt(v_hat) + ADAM_EPS) + ADAM_WD * w)
    return delta, m, v


def reference(x, mix_norm_ab, w_in_ab, mla_q_norm, mla_kv_norm, mla_w_uq, mla_w_ukv, w_out_ab, mix_norm_c, gqa_w_q, gqa_w_kv, gqa_q_norm, gqa_k_norm, gqa_w_o, ffn_norm, ffn_w_in, ffn_w_out, final_norm, loss_target, m_mix_norm_ab, m_w_in_ab, m_mla_q_norm, m_mla_kv_norm, m_mla_w_uq, m_mla_w_ukv, m_w_out_ab, m_mix_norm_c, m_gqa_w_q, m_gqa_w_kv, m_gqa_q_norm, m_gqa_k_norm, m_gqa_w_o, m_ffn_norm, m_ffn_w_in, m_ffn_w_out, m_final_norm, v_mix_norm_ab, v_w_in_ab, v_mla_q_norm, v_mla_kv_norm, v_mla_w_uq, v_mla_w_ukv, v_w_out_ab, v_mix_norm_c, v_gqa_w_q, v_gqa_w_kv, v_gqa_q_norm, v_gqa_k_norm, v_gqa_w_o, v_ffn_norm, v_ffn_w_in, v_ffn_w_out, v_final_norm):
    given = dict(x=x, mix_norm_ab=mix_norm_ab, w_in_ab=w_in_ab, mla_q_norm=mla_q_norm, mla_kv_norm=mla_kv_norm, mla_w_uq=mla_w_uq, mla_w_ukv=mla_w_ukv, w_out_ab=w_out_ab, mix_norm_c=mix_norm_c, gqa_w_q=gqa_w_q, gqa_w_kv=gqa_w_kv, gqa_q_norm=gqa_q_norm, gqa_k_norm=gqa_k_norm, gqa_w_o=gqa_w_o, ffn_norm=ffn_norm, ffn_w_in=ffn_w_in, ffn_w_out=ffn_w_out, final_norm=final_norm, loss_target=loss_target, m_mix_norm_ab=m_mix_norm_ab, m_w_in_ab=m_w_in_ab, m_mla_q_norm=m_mla_q_norm, m_mla_kv_norm=m_mla_kv_norm, m_mla_w_uq=m_mla_w_uq, m_mla_w_ukv=m_mla_w_ukv, m_w_out_ab=m_w_out_ab, m_mix_norm_c=m_mix_norm_c, m_gqa_w_q=m_gqa_w_q, m_gqa_w_kv=m_gqa_w_kv, m_gqa_q_norm=m_gqa_q_norm, m_gqa_k_norm=m_gqa_k_norm, m_gqa_w_o=m_gqa_w_o, m_ffn_norm=m_ffn_norm, m_ffn_w_in=m_ffn_w_in, m_ffn_w_out=m_ffn_w_out, m_final_norm=m_final_norm, v_mix_norm_ab=v_mix_norm_ab, v_w_in_ab=v_w_in_ab, v_mla_q_norm=v_mla_q_norm, v_mla_kv_norm=v_mla_kv_norm, v_mla_w_uq=v_mla_w_uq, v_mla_w_ukv=v_mla_w_ukv, v_w_out_ab=v_w_out_ab, v_mix_norm_c=v_mix_norm_c, v_gqa_w_q=v_gqa_w_q, v_gqa_w_kv=v_gqa_w_kv, v_gqa_q_norm=v_gqa_q_norm, v_gqa_k_norm=v_gqa_k_norm, v_gqa_w_o=v_gqa_w_o, v_ffn_norm=v_ffn_norm, v_ffn_w_in=v_ffn_w_in, v_ffn_w_out=v_ffn_w_out, v_final_norm=v_final_norm)
    weights = {n: given[n] for n in TWIN_WEIGHTS}
    shared = {n: given[n] for n in SHARED_INPUTS}
    per_example = {n: given[n] for n in ['x']}
    grad_fn = _jax.value_and_grad(_loss, argnums=(0, 1))

    def one_microbatch(ex, loss_target):
        ex = dict(ex)
        diff = ex.pop(TWIN_DIFF_INPUT)
        return grad_fn(weights, diff, {**shared, **ex}, loss_target)

    if N_MICROBATCH == 1:
        loss, (grad_w, grad_x) = one_microbatch(per_example, given["loss_target"])
    else:
        def body(carry, xs):
            loss_sum, grad_sum = carry
            l_k, (gw_k, gx_k) = one_microbatch(xs[0], xs[1])
            with _jax.named_scope("update"):
                return (loss_sum + l_k, _jax.tree.map(_jnp.add, grad_sum, gw_k)), gx_k

        init = (_jnp.zeros((), _jnp.float32), _jax.tree.map(_jnp.zeros_like, weights))
        (loss, grad_w), grad_x = _jax.lax.scan(body, init, (per_example, given["loss_target"]))
    with _jax.named_scope("update"):
        delta_w, new_m, new_v = {}, {}, {}
        for n in TWIN_WEIGHTS:
            delta_w[n], new_m[n], new_v[n] = _adamw(weights[n], grad_w[n], given["m_" + n], given["v_" + n])
    return (loss, grad_x, *[grad_w[n] for n in TWIN_WEIGHTS], *[delta_w[n] for n in TWIN_WEIGHTS],
            *[new_m[n] for n in TWIN_WEIGHTS], *[new_v[n] for n in TWIN_WEIGHTS])
```

```python
import math

import numpy as np
import jax
import jax.numpy as jnp
from jax import lax
from jax.experimental import pallas as pl
from jax.experimental.pallas import tpu as pltpu

F32 = jnp.float32
BF16 = jnp.bfloat16
MESH = pl.DeviceIdType.MESH

VMEM_LIMIT_BYTES = 56 * 1024 * 1024
LANES = 128

D_MODEL = 1024
NORM_EPS = 1e-6
ROPE_THETA = 10000.0
NEG_INF = -1e30
GRID_W = 64

MLA_HEADS, MLA_Q_RANK, MLA_KV_RANK, MLA_NOPE, MLA_ROPE, MLA_V = 8, 384, 256, 64, 32, 64
MLA_DK = 128
DIL_PAIRS = ((128, 1), (512, 4), (2048, 16))
DIL_HALF, DIL_SLOTS, DIL_GROUPS, DIL_HEAD_DIM = 64, 4, 3, 64
DIL_HEADS = DIL_SLOTS * DIL_GROUPS
DIL_W = DIL_SLOTS * DIL_HEAD_DIM
GQA_HEADS, GQA_KV_HEADS, GQA_HEAD_DIM = 16, 4, 64
FFN_HIDDEN = 2816
IN_A = MLA_Q_RANK + MLA_KV_RANK + MLA_ROPE
IN_A_PAD = 768
IN_B = 3 * DIL_HEADS * DIL_HEAD_DIM

ADAM_LR, ADAM_B1, ADAM_B2, ADAM_EPS, ADAM_WD, ADAM_STEP = 0.001, 0.9, 0.999, 1e-08, 0.01, 10

WEIGHTS = ['mix_norm_ab', 'w_in_ab', 'mla_q_norm', 'mla_kv_norm', 'mla_w_uq', 'mla_w_ukv', 'w_out_ab', 'mix_norm_c',
           'gqa_w_q', 'gqa_w_kv', 'gqa_q_norm', 'gqa_k_norm', 'gqa_w_o', 'ffn_norm', 'ffn_w_in', 'ffn_w_out',
           'final_norm']
BIG = (('w_in_ab', (2, 1024, 744), 2), ('mla_w_uq', (2, 96, 8, 96), 1), ('mla_w_ukv', (2, 64, 8, 128), 1),
       ('w_out_ab', (2, 768, 256), 2), ('gqa_w_q', (2, 256, 1024), 1), ('gqa_w_kv', (2, 256, 512), 1),
       ('gqa_w_o', (2, 256, 1024), 1), ('ffn_w_in', (4, 1024, 1408), 2), ('ffn_w_out', (4, 704, 1024), 1))
PACK_W = 1024
PACK_ROWS = sum(math.prod(s) for _, s, _ in BIG) // PACK_W
HALF_ROWS = PACK_ROWS // 2
SMALL = (('mix_norm_ab', (2, 1024)), ('mla_q_norm', (2, 384)), ('mla_kv_norm', (2, 256)), ('gqa_q_norm', (2, 64)),
         ('gqa_k_norm', (2, 64)), ('ffn_norm', (4, 1024)), ('final_norm', (1024,)), ('mix_norm_c', (2, 1024)))
PACKET_ROWS = 88


def _cparams(sem=None):
    return pltpu.CompilerParams(dimension_semantics=sem, vmem_limit_bytes=VMEM_LIMIT_BYTES)


def _pick(n, pref, mult=LANES):
    if n <= pref:
        return n
    for d in range(pref - pref % mult, 0, -mult):
        if n % d == 0:
            return d
    return n


_DIMS = {"nn": (((1,), (0,)), ((), ())), "nt": (((1,), (1,)), ((), ())), "tn": (((0,), (0,)), ((), ()))}


def _mm(a, b, *, mode="nn", add=None, out_dtype=F32, name="mm"):
    if mode == "nn":
        (m, k), (k2, n) = a.shape, b.shape
    elif mode == "nt":
        (m, k), (n, k2) = a.shape, b.shape
    else:
        (k, m), (k2, n) = a.shape, b.shape
    assert k == k2, (a.shape, b.shape, mode)
    bm = _pick(m, 512, LANES if mode == "tn" else 16)
    bn = _pick(n, 512)
    bk = _pick(k, 1408 if mode != "tn" else 1024, LANES if mode != "tn" else 16)
    nk = k // bk
    assert m % bm == 0 and n % bn == 0 and k % bk == 0
    has_add = add is not None

    def body(*refs):
        a_ref, b_ref = refs[0], refs[1]
        add_ref = refs[2] if has_add else None
        o_ref = refs[3 if has_add else 2]
        part = lax.dot_general(a_ref[...].astype(BF16), b_ref[...].astype(BF16), _DIMS[mode],
                               preferred_element_type=F32)

        def finish(r):
            if has_add:
                r = r + add_ref[...]
            o_ref[...] = r.astype(o_ref.dtype)

        if nk == 1:
            finish(part)
        else:
            acc_ref = refs[-1]
            kk = pl.program_id(2)

            @pl.when(kk == 0)
            def _():
                acc_ref[...] = part

            @pl.when(kk > 0)
            def _():
                acc_ref[...] += part

            @pl.when(kk == nk - 1)
            def _():
                finish(acc_ref[...])

    if mode == "nn":
        a_spec = pl.BlockSpec((bm, bk), lambda i, j, kk: (i, kk))
        b_spec = pl.BlockSpec((bk, bn), lambda i, j, kk: (kk, j))
    elif mode == "nt":
        a_spec = pl.BlockSpec((bm, bk), lambda i, j, kk: (i, kk))
        b_spec = pl.BlockSpec((bn, bk), lambda i, j, kk: (j, kk))
    else:
        a_spec = pl.BlockSpec((bk, bm), lambda i, j, kk: (kk, i))
        b_spec = pl.BlockSpec((bk, bn), lambda i, j, kk: (kk, j))
    o_spec = pl.BlockSpec((bm, bn), lambda i, j, kk: (i, j))
    in_specs = [a_spec, b_spec] + ([o_spec] if has_add else [])
    args = (a, b) + ((add,) if has_add else ())
    return pl.pallas_call(
        body, name=name, grid=(m // bm, n // bn, nk), in_specs=in_specs, out_specs=o_spec,
        out_shape=jax.ShapeDtypeStruct((m, n), out_dtype),
        scratch_shapes=[pltpu.VMEM((bm, bn), F32)] if nk > 1 else [],
        compiler_params=_cparams(("parallel", "parallel", "arbitrary")),
    )(*args)


def _rows_call(fn, rows, consts, out_rows, out_accs=(), *, bs=256, name):
    s = rows[0].shape[0]
    bs = min(bs, s)
    assert s % bs == 0
    nr, nc, no, na = len(rows), len(consts), len(out_rows), len(out_accs)

    def body(*refs):
        vals = [r[...] for r in refs[:nr + nc]]
        outs = refs[nr + nc:]
        res = fn(*vals)
        if not isinstance(res, (tuple, list)):
            res = (res,)
        assert len(res) == no + na, (len(res), no, na)
        for r, v in zip(outs[:no], res[:no]):
            r[...] = v.astype(r.dtype)
        if na:
            i = pl.program_id(0)
            for r, v in zip(outs[no:], res[no:]):
                @pl.when(i == 0)
                def _(r=r, v=v):
                    r[...] = v

                @pl.when(i > 0)
                def _(r=r, v=v):
                    r[...] += v

    in_specs = [pl.BlockSpec((bs, a.shape[1]), lambda i: (i, 0)) for a in rows]
    in_specs += [pl.BlockSpec(c.shape, lambda i: (0, 0)) for c in consts]
    out_specs = [pl.BlockSpec((bs, c), lambda i: (i, 0)) for c, _ in out_rows]
    out_specs += [pl.BlockSpec(tuple(sh), lambda i: (0, 0)) for sh in out_accs]
    out_shape = [jax.ShapeDtypeStruct((s, c), dt) for c, dt in out_rows]
    out_shape += [jax.ShapeDtypeStruct(tuple(sh), F32) for sh in out_accs]
    res = pl.pallas_call(
        body, name=name, grid=(s // bs,), in_specs=in_specs, out_specs=out_specs, out_shape=out_shape,
        compiler_params=_cparams(("arbitrary",) if na else ("parallel",)),
    )(*rows, *consts)
    return res


def _rms(x, g):
    return x * lax.rsqrt(jnp.mean(x * x, axis=-1, keepdims=True) + NORM_EPS) * g


def _rms_bwd_math(x, g, dy):
    r = lax.rsqrt(jnp.mean(x * x, axis=-1, keepdims=True) + NORM_EPS)
    u = dy * g
    dx = r * u - x * (r * r * r) * jnp.mean(u * x, axis=-1, keepdims=True)
    dg = jnp.sum(dy * x * r, axis=0, keepdims=True)
    return dx, dg


def _rms_fwd(x, g, name):
    return _rows_call(lambda xv, gv: _rms(xv, gv), [x], [g], [(x.shape[1], BF16)], name=name)[0]


def _rms_bwd(x, g, dy, dres, name):
    def fn(xv, dyv, dresv, gv):
        dx, dg = _rms_bwd_math(xv, gv, dyv.astype(F32))
        return dx + dresv, dg
    return _rows_call(fn, [x, dy, dres], [g], [(x.shape[1], F32)], [(1, x.shape[1])], name=name)


def _chunkdot(x, m):
    outs = [jnp.dot(x[:, c:c + LANES], m, precision=lax.Precision.HIGHEST, preferred_element_type=F32)
            for c in range(0, x.shape[1], LANES)]
    return outs[0] if len(outs) == 1 else jnp.concatenate(outs, axis=1)


def _lanes(t, width):
    n = width // LANES
    return t if n == 1 else jnp.concatenate([t] * n, axis=1)


def _rope(x, cs, swap):
    w = x.shape[1]
    return x * _lanes(cs[:, :LANES], w) + _chunkdot(x, swap) * _lanes(cs[:, LANES:], w)


def _rope_t(dy, cs, swap):
    w = dy.shape[1]
    return dy * _lanes(cs[:, :LANES], w) + _chunkdot(dy * _lanes(cs[:, LANES:], w), swap)


def _swap_matrix():
    m = np.zeros((LANES, LANES), np.float32)
    for j in range(LANES):
        src = j + 16 if (j % 32) < 16 else j - 16
        m[src, j] = 1.0
    return jnp.asarray(m)


def _seg_matrix(seg):
    idx = np.arange(LANES) // seg
    return jnp.asarray((idx[:, None] == idx[None, :]).astype(np.float32))


def _rope_tables(s):
    pos = jnp.arange(s)

    def angles(p, dim):
        freqs = ROPE_THETA ** (-jnp.arange(0, dim, 2, dtype=F32) / dim)
        ang = p.astype(F32)[:, None] * freqs[None, :]
        return jnp.cos(ang), jnp.sin(ang)

    cos_t, sin_t = angles(pos, MLA_ROPE)
    one, zero = jnp.ones((s, 64), F32), jnp.zeros((s, 64), F32)
    mla = jnp.concatenate([one, cos_t, cos_t, one[:, :32], zero, -sin_t, sin_t, zero[:, :32]], axis=1)
    cos_r, sin_r = angles(pos // GRID_W, GQA_HEAD_DIM // 2)
    cos_c, sin_c = angles(pos % GRID_W, GQA_HEAD_DIM // 2)
    c64 = jnp.concatenate([cos_r, cos_r, cos_c, cos_c], axis=1)
    s64 = jnp.concatenate([-sin_r, sin_r, -sin_c, sin_c], axis=1)
    gqa = jnp.concatenate([c64, c64, s64, s64], axis=1)
    return mla, gqa


def _col_to_row(col):
    return jnp.transpose(jnp.broadcast_to(col, (col.shape[0], LANES)))[0:1, :]


def _stack_heads(ref, heads, d, dtype=None):
    parts = [ref[:, hd * d:(hd + 1) * d] for hd in heads]
    out = parts[0] if len(parts) == 1 else jnp.concatenate(parts, axis=0)
    return out if dtype is None else out.astype(dtype)


def _flash_fwd(q, k, v, *, R, dk, dv, hb, scale, bq, bk, name):
    s = q.shape[0]
    g = k.shape[1] // dk
    ng = g // hb
    bq, bk = min(bq, s), min(bk, s)
    nq, nkb = s // bq, s // bk
    rb = R * bq

    def body(q_ref, k_ref, v_ref, o_ref, lse_ref):
        for h in range(hb):
            heads = [h * R + r for r in range(R)]
            qs = _stack_heads(q_ref, heads, dk)

            def step(j, carry, h=h, qs=qs):
                m, l, acc = carry
                r0 = pl.multiple_of(j * bk, bk)
                kj = k_ref[pl.ds(r0, bk), h * dk:(h + 1) * dk]
                vj = v_ref[pl.ds(r0, bk), h * dv:(h + 1) * dv]
                sc = lax.dot_general(qs, kj, _DIMS["nt"], preferred_element_type=F32) * scale
                m2 = jnp.maximum(m, jnp.max(sc, axis=1, keepdims=True))
                p = jnp.exp(sc - m2)
                a = jnp.exp(m - m2)
                l2 = a * l + jnp.sum(p, axis=1, keepdims=True)
                acc2 = a * acc + jnp.dot(p.astype(BF16), vj, preferred_element_type=F32)
                return m2, l2, acc2

            init = (jnp.full((rb, 1), NEG_INF, F32), jnp.zeros((rb, 1), F32), jnp.zeros((rb, dv), F32))
            m, l, acc = lax.fori_loop(0, nkb, step, init)
            o = acc / l
            row = _col_to_row(m + jnp.log(l))
            for r, hd in enumerate(heads):
                o_ref[:, hd * dv:(hd + 1) * dv] = o[r * bq:(r + 1) * bq].astype(o_ref.dtype)
                lse_ref[0, hd:hd + 1, :] = row[:, r * bq:(r + 1) * bq]

    return pl.pallas_call(
        body, name=name, grid=(ng, nq),
        in_specs=[pl.BlockSpec((bq, hb * R * dk), lambda gi, i: (i, gi)),
                  pl.BlockSpec((s, hb * dk), lambda gi, i: (0, gi)),
                  pl.BlockSpec((s, hb * dv), lambda gi, i: (0, gi))],
        out_specs=[pl.BlockSpec((bq, hb * R * dv), lambda gi, i: (i, gi)),
                   pl.BlockSpec((1, hb * R, bq), lambda gi, i: (gi, 0, i))],
        out_shape=[jax.ShapeDtypeStruct((s, g * R * dv), BF16), jax.ShapeDtypeStruct((ng, hb * R, s), F32)],
        compiler_params=_cparams(("parallel", "parallel")),
    )(q, k, v)


def _flash_bwd(q, k, v, o, do, lse, *, R, dk, dv, hb, scale, bq, bk, name):
    s = q.shape[0]
    g = k.shape[1] // dk
    ng = g // hb
    bq, bk = min(bq, s), min(bk, s)
    nq, nkb = s // bq, s // bk
    rb = R * bq

    def body(q_ref, k_ref, v_ref, o_ref, do_ref, lse_ref, dq_ref, dk_ref, dv_ref):
        @pl.when(pl.program_id(1) == 0)
        def _():
            dk_ref[...] = jnp.zeros(dk_ref.shape, F32)
            dv_ref[...] = jnp.zeros(dv_ref.shape, F32)

        for h in range(hb):
            heads = [h * R + r for r in range(R)]
            qs = _stack_heads(q_ref, heads, dk)
            dos = _stack_heads(do_ref, heads, dv, BF16)
            os_ = _stack_heads(o_ref, heads, dv)
            drow = _col_to_row(jnp.sum(dos.astype(F32) * os_.astype(F32), axis=1, keepdims=True))
            lrows = [lse_ref[0, hd:hd + 1, :] for hd in heads]
            lrow = lrows[0] if R == 1 else jnp.concatenate(lrows, axis=1)

            def step(j, dq, h=h, qs=qs, dos=dos, drow=drow, lrow=lrow):
                r0 = pl.multiple_of(j * bk, bk)
                kj = k_ref[pl.ds(r0, bk), h * dk:(h + 1) * dk]
                vj = v_ref[pl.ds(r0, bk), h * dv:(h + 1) * dv]
                st = lax.dot_general(kj, qs, _DIMS["nt"], preferred_element_type=F32) * scale
                pt = jnp.exp(st - lrow)
                dv_ref[pl.ds(r0, bk), h * dv:(h + 1) * dv] += jnp.dot(pt.astype(BF16), dos,
                                                                      preferred_element_type=F32)
                dpt = lax.dot_general(vj, dos, _DIMS["nt"], preferred_element_type=F32)
                dst = (pt * (dpt - drow) * scale).astype(BF16)
                dk_ref[pl.ds(r0, bk), h * dk:(h + 1) * dk] += jnp.dot(dst, qs, preferred_element_type=F32)
                return dq + lax.dot_general(dst, kj, _DIMS["tn"], preferred_element_type=F32)

            dq = lax.fori_loop(0, nkb, step, jnp.zeros((rb, dk), F32))
            for r, hd in enumerate(heads):
                dq_ref[:, hd * dk:(hd + 1) * dk] = dq[r * bq:(r + 1) * bq]

    qspec = pl.BlockSpec((bq, hb * R * dk), lambda gi, i: (i, gi))
    ospec = pl.BlockSpec((bq, hb * R * dv), lambda gi, i: (i, gi))
    kspec = pl.BlockSpec((s, hb * dk), lambda gi, i: (0, gi))
    vspec = pl.BlockSpec((s, hb * dv), lambda gi, i: (0, gi))
    return pl.pallas_call(
        body, name=name, grid=(ng, nq),
        in_specs=[qspec, kspec, vspec, ospec, ospec, pl.BlockSpec((1, hb * R, bq), lambda gi, i: (gi, 0, i))],
        out_specs=[qspec, kspec, vspec],
        out_shape=[jax.ShapeDtypeStruct((s, g * R * dk), F32), jax.ShapeDtypeStruct((s, g * dk), F32),
                   jax.ShapeDtypeStruct((s, g * dv), F32)],
        compiler_params=_cparams(("parallel", "arbitrary")),
    )(q, k, v, o, do, lse)


DIL_BQ = 128
DIL_NCOL = IN_B // DIL_W


def _alibi_slope(head):
    return float(2.0 ** (-8.0 * (head + 1) / DIL_HEADS))


def _band(i, d, length, queries_first):
    if queries_first:
        shape = (3 * DIL_BQ, DIL_BQ)
        wide = (i - 1) * DIL_BQ + lax.broadcasted_iota(jnp.int32, shape, 0)
        narrow = i * DIL_BQ + lax.broadcasted_iota(jnp.int32, shape, 1)
    else:
        shape = (DIL_BQ, 3 * DIL_BQ)
        narrow = i * DIL_BQ + lax.broadcasted_iota(jnp.int32, shape, 0)
        wide = (i - 1) * DIL_BQ + lax.broadcasted_iota(jnp.int32, shape, 1)
    rel = jnp.abs(wide - narrow)
    valid = (rel <= DIL_HALF) & (wide >= 0) & (wide < length)
    return valid, rel.astype(F32) * float(d)


def _nbr_specs(col_of, nb):
    return [pl.BlockSpec((DIL_BQ, DIL_W), lambda r, i: (jnp.maximum(i - 1, 0), col_of(r))),
            pl.BlockSpec((DIL_BQ, DIL_W), lambda r, i: (i, col_of(r))),
            pl.BlockSpec((DIL_BQ, DIL_W), lambda r, i: (jnp.minimum(i + 1, nb - 1), col_of(r)))]


def _cat3(refs, sl, dtype=None):
    out = jnp.concatenate([r[:, sl] for r in refs], axis=0)
    return out if dtype is None else out.astype(dtype)


def _dil_fwd(zb, grp, name):
    s = zb.shape[0]
    d = DIL_PAIRS[grp][1]
    length = s // d
    nb = length // DIL_BQ
    zv = zb.reshape(length, d * IN_B)
    scale = DIL_HEAD_DIM ** -0.5

    def body(q_ref, kp, kc, kn, vp, vc, vn, o_ref, lse_ref):
        valid, dist = _band(pl.program_id(1), d, length, False)
        outs, lses = [], []
        for sl_i in range(DIL_SLOTS):
            sl = slice(sl_i * DIL_HEAD_DIM, (sl_i + 1) * DIL_HEAD_DIM)
            kcat, vcat = _cat3((kp, kc, kn), sl), _cat3((vp, vc, vn), sl)
            sc = lax.dot_general(q_ref[:, sl], kcat, _DIMS["nt"], preferred_element_type=F32) * scale
            sc = jnp.where(valid, sc - _alibi_slope(grp * DIL_SLOTS + sl_i) * dist, NEG_INF)
            m = jnp.max(sc, axis=1, keepdims=True)
            e = jnp.exp(sc - m)
            den = jnp.sum(e, axis=1, keepdims=True)
            outs.append(jnp.dot((e / den).astype(BF16), vcat, preferred_element_type=F32))
            lses.append(jnp.broadcast_to(m + jnp.log(den), (DIL_BQ, DIL_HEAD_DIM)))
        o_ref[...] = jnp.concatenate(outs, axis=1)
        lse_ref[...] = jnp.concatenate(lses, axis=1)

    cur = pl.BlockSpec((DIL_BQ, DIL_W), lambda r, i: (i, r * DIL_NCOL + grp))
    out_spec = pl.BlockSpec((DIL_BQ, DIL_W), lambda r, i: (i, r))
    o, lse = pl.pallas_call(
        body, name=name, grid=(d, nb),
        in_specs=[cur] + _nbr_specs(lambda r: r * DIL_NCOL + 3 + grp, nb)
        + _nbr_specs(lambda r: r * DIL_NCOL + 6 + grp, nb),
        out_specs=[out_spec, out_spec],
        out_shape=[jax.ShapeDtypeStruct((length, d * DIL_W), F32)] * 2,
        compiler_params=_cparams(("parallel", "parallel")),
    )(zv, zv, zv, zv, zv, zv, zv)
    return o.reshape(s, DIL_W), lse.reshape(s, DIL_W)


def _dil_bwd(zb, do, lse, dl, grp, name):
    s = zb.shape[0]
    d = DIL_PAIRS[grp][1]
    length = s // d
    nb = length // DIL_BQ
    zv = zb.reshape(length, d * IN_B)
    dov, lsev, dlv = (t.reshape(length, d * DIL_W) for t in (do, lse, dl))
    scale = DIL_HEAD_DIM ** -0.5

    def probs(sc, valid, dist, sl_i, lse_col):
        sc = jnp.where(valid, sc * scale - _alibi_slope(grp * DIL_SLOTS + sl_i) * dist, NEG_INF)
        return jnp.exp(sc - lse_col)

    def dq_body(q_ref, kp, kc, kn, vp, vc, vn, do_ref, lse_ref, dl_ref, dq_ref):
        valid, dist = _band(pl.program_id(1), d, length, False)
        outs = []
        for sl_i in range(DIL_SLOTS):
            sl = slice(sl_i * DIL_HEAD_DIM, (sl_i + 1) * DIL_HEAD_DIM)
            c0 = sl_i * DIL_HEAD_DIM
            kcat, vcat = _cat3((kp, kc, kn), sl), _cat3((vp, vc, vn), sl)
            sc = lax.dot_general(q_ref[:, sl], kcat, _DIMS["nt"], preferred_element_type=F32)
            p = probs(sc, valid, dist, sl_i, lse_ref[:, c0:c0 + 1])
            dp = lax.dot_general(do_ref[:, sl], vcat, _DIMS["nt"], preferred_element_type=F32)
            ds = (p * (dp - dl_ref[:, c0:c0 + 1]) * scale).astype(BF16)
            outs.append(jnp.dot(ds, kcat, preferred_element_type=F32))
        dq_ref[...] = jnp.concatenate(outs, axis=1).astype(dq_ref.dtype)

    def dkv_body(k_ref, v_ref, qp, qc, qn, dop, doc, don, lp, lc, ln, dlp, dlc, dln, dk_ref, dv_ref):
        valid, dist = _band(pl.program_id(1), d, length, True)
        dks, dvs = [], []
        for sl_i in range(DIL_SLOTS):
            sl = slice(sl_i * DIL_HEAD_DIM, (sl_i + 1) * DIL_HEAD_DIM)
            one = slice(sl_i * DIL_HEAD_DIM, sl_i * DIL_HEAD_DIM + 1)
            qcat, docat = _cat3((qp, qc, qn), sl), _cat3((dop, doc, don), sl)
            sc = lax.dot_general(qcat, k_ref[:, sl], _DIMS["nt"], preferred_element_type=F32)
            p = probs(sc, valid, dist, sl_i, _cat3((lp, lc, ln), one))
            dvs.append(lax.dot_general(p.astype(BF16), docat, _DIMS["tn"], preferred_element_type=F32))
            dp = lax.dot_general(docat, v_ref[:, sl], _DIMS["nt"], preferred_element_type=F32)
            ds = (p * (dp - _cat3((dlp, dlc, dln), one)) * scale).astype(BF16)
            dks.append(lax.dot_general(ds, qcat, _DIMS["tn"], preferred_element_type=F32))
        dk_ref[...] = jnp.concatenate(dks, axis=1).astype(dk_ref.dtype)
        dv_ref[...] = jnp.concatenate(dvs, axis=1).astype(dv_ref.dtype)

    def zcur(c):
        return pl.BlockSpec((DIL_BQ, DIL_W), lambda r, i: (i, r * DIL_NCOL + c + grp))

    own = pl.BlockSpec((DIL_BQ, DIL_W), lambda r, i: (i, r))
    view = jax.ShapeDtypeStruct((length, d * DIL_W), BF16)
    dq = pl.pallas_call(
        dq_body, name=name + "_dq", grid=(d, nb),
        in_specs=[zcur(0)] + _nbr_specs(lambda r: r * DIL_NCOL + 3 + grp, nb)
        + _nbr_specs(lambda r: r * DIL_NCOL + 6 + grp, nb) + [own, own, own],
        out_specs=own, out_shape=view, compiler_params=_cparams(("parallel", "parallel")),
    )(zv, zv, zv, zv, zv, zv, zv, dov, lsev, dlv)
    own3 = _nbr_specs(lambda r: r, nb)
    dk, dv = pl.pallas_call(
        dkv_body, name=name + "_dkv", grid=(d, nb),
        in_specs=[zcur(3), zcur(6)] + _nbr_specs(lambda r: r * DIL_NCOL + grp, nb) + own3 + own3 + own3,
        out_specs=[own, own], out_shape=[view, view], compiler_params=_cparams(("parallel", "parallel")),
    )(zv, zv, zv, zv, zv, dov, dov, dov, lsev, lsev, lsev, dlv, dlv, dlv)
    return dq.reshape(s, DIL_W), dk.reshape(s, DIL_W), dv.reshape(s, DIL_W)


def _dil_combine(os_, ls_, name):
    def fn(o0, o1, o2, l0, l1, l2):
        m = jnp.maximum(jnp.maximum(l0, l1), l2)
        e0, e1, e2 = jnp.exp(l0 - m), jnp.exp(l1 - m), jnp.exp(l2 - m)
        den = e0 + e1 + e2
        comb = (e0 / den) * o0 + (e1 / den) * o1 + (e2 / den) * o2
        return comb, m + jnp.log(den)
    return _rows_call(fn, list(os_) + list(ls_), [], [(DIL_W, BF16), (DIL_W, F32)], name=name)


def _dil_combine_bwd(dcomb, os_, ls_, lt, seg64, name):
    def fn(dc, o0, o1, o2, l0, l1, l2, ltv, seg):
        w = [jnp.exp(l - ltv) for l in (l0, l1, l2)]
        comb = w[0] * o0 + w[1] * o1 + w[2] * o2
        t = _chunkdot(dc * comb, seg)
        return [wg * dc for wg in w] + [wg * t for wg in w]
    return _rows_call(fn, [dcomb] + list(os_) + list(ls_) + [lt], [seg64],
                      [(DIL_W, BF16)] * 3 + [(DIL_W, F32)] * 3, name=name)


def _mla_prep(za, gq, gkv, cs, swap, name):
    def fn(z, csv, gqv, gkvv, sw):
        return (_rms(z[:, :MLA_Q_RANK], gqv), _rms(z[:, MLA_Q_RANK:640], gkvv), _rope(z[:, 640:], csv, sw))
    return _rows_call(fn, [za, cs], [gq, gkv, swap], [(MLA_Q_RANK, BF16), (MLA_KV_RANK, BF16), (LANES, F32)],
                      name=name)


def _mla_prep_bwd(za, cs, dcq, dckv, dkr, gq, gkv, swap, name):
    def fn(z, csv, dcqv, dckvv, dkrv, gqv, gkvv, sw):
        d1, dg1 = _rms_bwd_math(z[:, :MLA_Q_RANK], gqv, dcqv)
        d2, dg2 = _rms_bwd_math(z[:, MLA_Q_RANK:640], gkvv, dckvv)
        d3 = _rope_t(dkrv, csv, sw)
        return jnp.concatenate([d1, d2, d3], axis=1), dg1, dg2
    return _rows_call(fn, [za, cs, dcq, dckv, dkr], [gq, gkv, swap], [(IN_A_PAD, BF16)],
                      [(1, MLA_Q_RANK), (1, MLA_KV_RANK)], name=name)


def _mla_qk(q_raw, k_pad, krr, cs, swap, name):
    w = MLA_HEADS * MLA_DK

    def fn(qv, kv, krv, csv, sw):
        return _rope(qv, csv, sw), kv + _lanes(krv, w)
    return _rows_call(fn, [q_raw, k_pad, krr, cs], [swap], [(w, BF16), (w, BF16)], name=name)


def _mla_qk_bwd(dqh, dkh, cs, swap, name):
    def fn(dq, dk, csv, sw):
        acc = dk[:, :LANES]
        for h in range(1, MLA_HEADS):
            acc = acc + dk[:, h * LANES:(h + 1) * LANES]
        lane = lax.broadcasted_iota(jnp.int32, acc.shape, 1)
        acc = jnp.where((lane >= MLA_NOPE) & (lane < MLA_NOPE + MLA_ROPE), acc, 0.0)
        return _rope_t(dq, csv, sw), acc
    return _rows_call(fn, [dqh, dkh, cs], [swap], [(MLA_HEADS * MLA_DK, BF16), (LANES, F32)], name=name)


def _head_norm(t, g2, seg):
    r = lax.rsqrt(_chunkdot(t * t, seg) * (1.0 / GQA_HEAD_DIM) + NORM_EPS)
    return t * r * _lanes(g2, t.shape[1]), r


def _head_norm_bwd(t, g2, seg, dn):
    w = t.shape[1]
    r = lax.rsqrt(_chunkdot(t * t, seg) * (1.0 / GQA_HEAD_DIM) + NORM_EPS)
    u = dn * _lanes(g2, w)
    dt = r * u - t * (r * r * r) * (_chunkdot(u * t, seg) * (1.0 / GQA_HEAD_DIM))
    dgw = jnp.sum(dn * t * r, axis=0, keepdims=True)
    dg = dgw[:, :LANES]
    for c in range(LANES, w, LANES):
        dg = dg + dgw[:, c:c + LANES]
    return dt, dg


def _gqa_prep(q_raw, kv_raw, cs, gq2, gk2, seg, swap, name):
    kw = GQA_KV_HEADS * GQA_HEAD_DIM

    def fn(qv, kvv, csv, gqv, gkv, sg, sw):
        qn, _ = _head_norm(qv, gqv, sg)
        kn, _ = _head_norm(kvv[:, :kw], gkv, sg)
        return _rope(qn, csv, sw), _rope(kn, csv, sw), kvv[:, kw:]
    return _rows_call(fn, [q_raw, kv_raw, cs], [gq2, gk2, seg, swap],
                      [(GQA_HEADS * GQA_HEAD_DIM, BF16), (kw, BF16), (kw, BF16)], name=name)


def _gqa_prep_bwd(q_raw, kv_raw, cs, dqh, dkh, dv, gq2, gk2, seg, swap, name):
    kw = GQA_KV_HEADS * GQA_HEAD_DIM

    def fn(qv, kvv, csv, dq, dk, dvv, gqv, gkv, sg, sw):
        dqr, dgq = _head_norm_bwd(qv, gqv, sg, _rope_t(dq, csv, sw))
        dkr, dgk = _head_norm_bwd(kvv[:, :kw], gkv, sg, _rope_t(dk, csv, sw))
        return dqr, jnp.concatenate([dkr, dvv], axis=1), dgq, dgk
    return _rows_call(fn, [q_raw, kv_raw, cs, dqh, dkh, dv], [gq2, gk2, seg, swap],
                      [(GQA_HEADS * GQA_HEAD_DIM, BF16), (2 * kw, BF16)], [(1, LANES), (1, LANES)], name=name)


def _swiglu(h, name):
    def fn(hv):
        gate, up = hv[:, :FFN_HIDDEN].astype(F32), hv[:, FFN_HIDDEN:].astype(F32)
        return gate / (1.0 + jnp.exp(-gate)) * up
    return _rows_call(fn, [h], [], [(FFN_HIDDEN, BF16)], name=name)[0]


def _swiglu_bwd(h, da, name):
    def fn(hv, dav):
        gate, up = hv[:, :FFN_HIDDEN].astype(F32), hv[:, FFN_HIDDEN:].astype(F32)
        dav = dav.astype(F32)
        sig = 1.0 / (1.0 + jnp.exp(-gate))
        silu = gate * sig
        dgate = dav * up * (sig + silu * (1.0 - sig))
        return jnp.concatenate([dgate, dav * silu], axis=1)
    return _rows_call(fn, [h, da], [], [(2 * FFN_HIDDEN, BF16)], name=name)[0]


def _loss_head(x, target, g, name):
    dm = x.shape[1]

    def fn(xv, tv, gv):
        err = _rms(xv, gv) - tv
        loss = 0.5 * jnp.sum(err * err) / dm
        dx, dg = _rms_bwd_math(xv, gv, err * (1.0 / dm))
        return dx, jnp.zeros((1, LANES), F32) + loss, dg
    return _rows_call(fn, [x, target], [g], [(dm, F32)], [(1, LANES), (1, dm)], name=name)


def _adamw(w, g, m, v, name):
    def fn(wv, gv, mv, vv):
        m2 = ADAM_B1 * mv + (1.0 - ADAM_B1) * gv
        v2 = ADAM_B2 * vv + (1.0 - ADAM_B2) * (gv * gv)
        m_hat = m2 / (1.0 - ADAM_B1 ** ADAM_STEP)
        v_hat = v2 / (1.0 - ADAM_B2 ** ADAM_STEP)
        return -ADAM_LR * (m_hat / (jnp.sqrt(v_hat) + ADAM_EPS) + ADAM_WD * wv), m2, v2
    c = w.shape[1]
    return _rows_call(fn, [w, g, m, v], [], [(c, F32)] * 3, bs=_pick(w.shape[0], 256, 8), name=name)


HBM_SPEC = pl.BlockSpec(memory_space=pltpu.HBM)
VMEM_SPEC = pl.BlockSpec(memory_space=pltpu.VMEM)


def _position():
    return lax.axis_index("x"), lax.axis_index("y"), lax.axis_index("c")


def _other_chips(x, y):
    return [(1 - x, y), (x, 1 - y), (1 - x, 1 - y)]


def _all_gather_weights(packed):
    rows = packed.shape[0]
    hr = rows // 2

    def body(p_ref, g_ref, send_sems, recv_sems, local_sem):
        x, y, c = _position()
        r0 = c * hr
        chips = _other_chips(x, y)

        def half(chip, hc):
            return g_ref.at[2 * chip[0] + chip[1], pl.ds(hc * hr, hr), :]

        def copy(j, src, dst, to):
            return pltpu.make_async_remote_copy(src_ref=src, dst_ref=dst, send_sem=send_sems.at[j],
                                                recv_sem=recv_sems.at[j], device_id=to, device_id_type=MESH)

        mine = pltpu.make_async_copy(p_ref, g_ref.at[2 * x + y], local_sem)
        mine.start()
        first = [copy(j, p_ref.at[pl.ds(r0, hr), :], half((x, y), c), (*chip, c)) for j, chip in enumerate(chips)]
        for cp in first:
            cp.start()
        passed = [copy(3 + j, half(chip, c), half(chip, c), (x, y, 1 - c)) for j, chip in enumerate(chips)]
        for j, chip in enumerate(chips):
            copy(j, half(chip, c), half(chip, c), (x, y, c)).wait_recv()
            passed[j].start()
        for j, chip in enumerate(chips):
            copy(3 + j, half(chip, 1 - c), half(chip, 1 - c), (x, y, c)).wait_recv()
        for cp in first + passed:
            cp.wait_send()
        mine.wait()

    return pl.pallas_call(
        body, name="all_gather_weights", in_specs=[HBM_SPEC], out_specs=HBM_SPEC,
        out_shape=jax.ShapeDtypeStruct((4, rows, packed.shape[1]), packed.dtype),
        scratch_shapes=[pltpu.SemaphoreType.DMA((6,)), pltpu.SemaphoreType.DMA((6,)), pltpu.SemaphoreType.DMA],
    )(packed)


def _sibling_swap_halves(grads):
    hr = grads.shape[1] // 2

    def body(g_ref, a_ref, send_sem, recv_sem):
        x, y, c = _position()
        cp = pltpu.make_async_remote_copy(src_ref=g_ref.at[:, pl.ds((1 - c) * hr, hr), :], dst_ref=a_ref,
                                          send_sem=send_sem, recv_sem=recv_sem, device_id=(x, y, 1 - c),
                                          device_id_type=MESH)
        cp.start()
        cp.wait()

    return pl.pallas_call(
        body, name="grad_swap_cores", in_specs=[HBM_SPEC], out_specs=HBM_SPEC,
        out_shape=jax.ShapeDtypeStruct((4, hr, grads.shape[2]), grads.dtype),
        scratch_shapes=[pltpu.SemaphoreType.DMA, pltpu.SemaphoreType.DMA],
    )(grads)


RS_BLOCK = 848


def _chip_sum(grads, other, c):
    hr = other.shape[1]
    nblk = hr // RS_BLOCK
    width = grads.shape[2]

    def body(c_ref, g_ref, a_ref, o_ref):
        o_ref[...] = g_ref[...] + a_ref[...]

    blk = (1, RS_BLOCK, width)
    return pl.pallas_call(
        body, name="grad_chip_sum",
        grid_spec=pltpu.PrefetchScalarGridSpec(
            num_scalar_prefetch=1, grid=(4, nblk),
            in_specs=[pl.BlockSpec(blk, lambda k, i, c_ref: (k, c_ref[0] * nblk + i, 0)),
                      pl.BlockSpec(blk, lambda k, i, c_ref: (k, i, 0))],
            out_specs=pl.BlockSpec(blk, lambda k, i, c_ref: (k, i, 0))),
        out_shape=jax.ShapeDtypeStruct(other.shape, F32),
        compiler_params=_cparams(("parallel", "parallel")),
    )(jnp.reshape(c, (1,)).astype(jnp.int32), grads, other)


def _send_chip_sums(sums):
    hr, width = sums.shape[1], sums.shape[2]

    def body(t_ref, b_ref, send_sems, recv_sems):
        x, y, c = _position()
        copies = [pltpu.make_async_remote_copy(src_ref=t_ref.at[2 * chip[0] + chip[1]], dst_ref=b_ref.at[j],
                                               send_sem=send_sems.at[j], recv_sem=recv_sems.at[j],
                                               device_id=(*chip, c), device_id_type=MESH)
                  for j, chip in enumerate(_other_chips(x, y))]
        for cp in copies:
            cp.start()
        for cp in copies:
            cp.wait()

    return pl.pallas_call(
        body, name="grad_send_chips", in_specs=[HBM_SPEC], out_specs=HBM_SPEC,
        out_shape=jax.ShapeDtypeStruct((3, hr, width), sums.dtype),
        scratch_shapes=[pltpu.SemaphoreType.DMA((3,)), pltpu.SemaphoreType.DMA((3,))],
    )(sums)


def _final_sum(sums, recv, k):
    hr, width = sums.shape[1], sums.shape[2]

    def body(k_ref, t_ref, b_ref, o_ref):
        o_ref[...] = ((t_ref[0] + b_ref[0]) + b_ref[1]) + b_ref[2]

    return pl.pallas_call(
        body, name="grad_final_sum",
        grid_spec=pltpu.PrefetchScalarGridSpec(
            num_scalar_prefetch=1, grid=(hr // RS_BLOCK,),
            in_specs=[pl.BlockSpec((1, RS_BLOCK, width), lambda i, k_ref: (k_ref[0], i, 0)),
                      pl.BlockSpec((3, RS_BLOCK, width), lambda i, k_ref: (0, i, 0))],
            out_specs=pl.BlockSpec((RS_BLOCK, width), lambda i, k_ref: (i, 0))),
        out_shape=jax.ShapeDtypeStruct((hr, width), F32),
        compiler_params=_cparams(("parallel",)),
    )(jnp.reshape(k, (1,)).astype(jnp.int32), sums, recv)


def _join_halves(half):
    hr, width = half.shape

    def body(h_ref, o_ref, send_sem, recv_sem, local_sem):
        x, y, c = _position()
        mine = pltpu.make_async_copy(h_ref, o_ref.at[pl.ds(c * hr, hr), :], local_sem)
        mine.start()
        cp = pltpu.make_async_remote_copy(src_ref=h_ref, dst_ref=o_ref.at[pl.ds(c * hr, hr), :], send_sem=send_sem,
                                          recv_sem=recv_sem, device_id=(x, y, 1 - c), device_id_type=MESH)
        cp.start()
        pltpu.make_async_remote_copy(src_ref=h_ref, dst_ref=o_ref.at[pl.ds((1 - c) * hr, hr), :], send_sem=send_sem,
                                     recv_sem=recv_sem, device_id=(x, y, 1 - c), device_id_type=MESH).wait_recv()
        cp.wait_send()
        mine.wait()

    return pl.pallas_call(
        body, name="grad_join_cores", in_specs=[HBM_SPEC], out_specs=HBM_SPEC,
        out_shape=jax.ShapeDtypeStruct((2 * hr, width), half.dtype),
        scratch_shapes=[pltpu.SemaphoreType.DMA, pltpu.SemaphoreType.DMA, pltpu.SemaphoreType.DMA],
    )(half)


def _all_reduce_packet(packet):
    rows = packet.shape[0]

    def body(p_ref, o_ref, buf, send_sems, recv_sems):
        x, y, c = _position()
        me = 4 * x + 2 * y + c
        buf[me] = p_ref[...]

        def flip(v, bit):
            return 1 - v if bit else v

        for p in range(1, 8):
            peer = (flip(x, p & 4), flip(y, p & 2), flip(c, p & 1))
            pltpu.make_async_remote_copy(src_ref=p_ref, dst_ref=buf.at[me], send_sem=send_sems.at[p - 1],
                                         recv_sem=recv_sems.at[p - 1], device_id=peer, device_id_type=MESH).start()
        for p in range(1, 8):
            peer = (flip(x, p & 4), flip(y, p & 2), flip(c, p & 1))
            slot = 4 * peer[0] + 2 * peer[1] + peer[2]
            cp = pltpu.make_async_remote_copy(src_ref=p_ref, dst_ref=buf.at[slot], send_sem=send_sems.at[p - 1],
                                              recv_sem=recv_sems.at[p - 1], device_id=peer, device_id_type=MESH)
            cp.wait_recv()
            cp.wait_send()
        acc = buf[0]
        for dev in range(1, 8):
            acc = acc + buf[dev]
        o_ref[...] = acc

    return pl.pallas_call(
        body, name="all_reduce_packet", in_specs=[VMEM_SPEC], out_specs=VMEM_SPEC,
        out_shape=jax.ShapeDtypeStruct(packet.shape, F32),
        scratch_shapes=[pltpu.VMEM((8, rows, LANES), F32), pltpu.SemaphoreType.DMA((7,)),
                        pltpu.SemaphoreType.DMA((7,))],
    )(packet)


def _pack_blocks(blocks, dtype):
    return jnp.concatenate([blocks[n].astype(dtype).reshape(-1, PACK_W) for n, _, _ in BIG], axis=0)


def _unpack_blocks(packed):
    out, off = {}, 0
    for n, shape, _ in BIG:
        r = math.prod(shape) // PACK_W
        out[n] = packed[off:off + r].reshape(shape)
        off += r
    return out


def _unpack_gathered(gathered):
    per_chip = [_unpack_blocks(gathered[k]) for k in range(4)]
    return {n: jnp.concatenate([per_chip[k][n] for k in range(4)], axis=ax) for n, _, ax in BIG}


def _pack_full(full, dtype):
    chips = []
    for k in range(4):
        blocks = {}
        for n, shape, ax in BIG:
            blocks[n] = lax.slice_in_dim(full[n], k * shape[ax], (k + 1) * shape[ax], axis=ax)
        chips.append(_pack_blocks(blocks, dtype))
    return jnp.stack(chips, axis=0)


def _pack_small(vals, loss_row):
    rows = [loss_row.reshape(1, LANES)]
    for n, shape in SMALL:
        v = vals.get(n)
        v = jnp.zeros(shape, F32) if v is None else v
        rows.append(v.astype(F32).reshape(-1, LANES))
    packet = jnp.concatenate(rows, axis=0)
    return jnp.pad(packet, ((0, PACKET_ROWS - packet.shape[0]), (0, 0)))


def _unpack_small(packet):
    out, off = {}, 1
    for n, shape in SMALL:
        r = math.prod(shape) // LANES
        out[n] = packet[off:off + r].reshape(shape)
        off += r
    return packet[0, 0], out


_MLA = dict(R=1, dk=MLA_DK, dv=MLA_V, hb=2, scale=(MLA_NOPE + MLA_ROPE) ** -0.5, bq=512, bk=512)
_GQA = dict(R=GQA_HEADS // GQA_KV_HEADS, dk=GQA_HEAD_DIM, dv=GQA_HEAD_DIM, hb=2, scale=GQA_HEAD_DIM ** -0.5,
            bq=256, bk=512)


def _layer_weights(full, gains):
    layers = []
    for layer in range(4):
        i = layer // 2
        p = dict(ffn_norm=gains['ffn_norm'][layer][None], ffn_w_in=full['ffn_w_in'][layer],
                 ffn_w_out=full['ffn_w_out'][layer])
        if layer % 2 == 0:
            w_in = full['w_in_ab'][i]
            zeros = jnp.zeros((D_MODEL, 32), w_in.dtype)
            p['w_a'] = jnp.concatenate([w_in[:, :640], zeros, zeros, w_in[:, 640:IN_A], zeros], axis=1)
            p['w_b'] = w_in[:, IN_A:]
            p['w_uq'] = jnp.pad(full['mla_w_uq'][i], ((0, 0), (0, 0), (0, MLA_DK - 96))).reshape(MLA_Q_RANK, -1)
            ukv = full['mla_w_ukv'][i]
            p['w_uk'] = jnp.pad(ukv[:, :, :MLA_NOPE], ((0, 0), (0, 0), (0, MLA_DK - MLA_NOPE))).reshape(MLA_KV_RANK, -1)
            p['w_uv'] = ukv[:, :, MLA_NOPE:].reshape(MLA_KV_RANK, -1)
            p['w_out'] = full['w_out_ab'][i]
            p['mix_norm'] = gains['mix_norm_ab'][i][None]
            p['q_norm'] = gains['mla_q_norm'][i][None]
            p['kv_norm'] = gains['mla_kv_norm'][i][None]
        else:
            p['w_q'], p['w_kv'], p['w_o'] = full['gqa_w_q'][i], full['gqa_w_kv'][i], full['gqa_w_o'][i]
            p['mix_norm'] = gains['mix_norm_c'][i][None]
            p['q_norm'] = jnp.tile(gains['gqa_q_norm'][i][None], (1, 2))
            p['k_norm'] = jnp.tile(gains['gqa_k_norm'][i][None], (1, 2))
        layers.append(p)
    return layers


def _even_fwd(x, p, cs, swap, tag):
    xn = _rms_fwd(x, p['mix_norm'], tag + "_norm")
    za = _mm(xn, p['w_a'], name=tag + "_in_a")
    zb = _mm(xn, p['w_b'], out_dtype=BF16, name=tag + "_in_b")
    cq, ckv, krr = _mla_prep(za, p['q_norm'], p['kv_norm'], cs, swap, tag + "_mla_prep")
    q_raw = _mm(cq, p['w_uq'], name=tag + "_uq")
    k_pad = _mm(ckv, p['w_uk'], name=tag + "_uk")
    v = _mm(ckv, p['w_uv'], out_dtype=BF16, name=tag + "_uv")
    qh, kh = _mla_qk(q_raw, k_pad, krr, cs, swap, tag + "_mla_qk")
    o_a, lse_a = _flash_fwd(qh, kh, v, name=tag + "_mla_attn", **_MLA)
    og, lg = [], []
    for grp in range(DIL_GROUPS):
        o, l = _dil_fwd(zb, grp, f"{tag}_dil{grp}")
        og.append(o)
        lg.append(l)
    o_b, lt = _dil_combine(og, lg, tag + "_dil_merge")
    ocat = jnp.concatenate([o_a, o_b], axis=1)
    x1 = _mm(ocat, p['w_out'], add=x, name=tag + "_out")
    saved = dict(x=x, xn=xn, za=za, zb=zb, cq=cq, ckv=ckv, qh=qh, kh=kh, v=v, lse_a=lse_a, og=og, lg=lg, lt=lt,
                 ocat=ocat)
    return x1, saved


def _even_bwd(dx1, p, sv, cs, swap, seg64, tag):
    docat = _mm(dx1, p['w_out'], mode="nt", name=tag + "_out_dx")
    d_w_out = _mm(sv['ocat'], dx1, mode="tn", name=tag + "_out_dw")
    n_a = MLA_HEADS * MLA_V
    do_a = docat[:, :n_a].astype(BF16)
    res = _dil_combine_bwd(docat[:, n_a:], sv['og'], sv['lg'], sv['lt'], seg64, tag + "_dil_merge_bwd")
    dqs, dks, dvs = [], [], []
    for grp in range(DIL_GROUPS):
        dq, dk, dv = _dil_bwd(sv['zb'], res[grp], sv['lg'][grp], res[3 + grp], grp, f"{tag}_dil{grp}_bwd")
        dqs.append(dq)
        dks.append(dk)
        dvs.append(dv)
    dzb = jnp.concatenate(dqs + dks + dvs, axis=1)
    dqh, dkh, dv = _flash_bwd(sv['qh'], sv['kh'], sv['v'], sv['ocat'][:, :n_a], do_a, sv['lse_a'],
                              name=tag + "_mla_attn_bwd", **_MLA)
    dq_raw, dkrr = _mla_qk_bwd(dqh, dkh, cs, swap, tag + "_mla_qk_bwd")
    dcq = _mm(dq_raw, p['w_uq'], mode="nt", name=tag + "_uq_dx")
    d_w_uq = _mm(sv['cq'], dq_raw, mode="tn", name=tag + "_uq_dw")
    dckv = _mm(dkh, p['w_uk'], mode="nt", name=tag + "_uk_dx")
    dckv = _mm(dv, p['w_uv'], mode="nt", add=dckv, name=tag + "_uv_dx")
    d_w_uk = _mm(sv['ckv'], dkh, mode="tn", name=tag + "_uk_dw")
    d_w_uv = _mm(sv['ckv'], dv, mode="tn", name=tag + "_uv_dw")
    dza, d_gq, d_gkv = _mla_prep_bwd(sv['za'], cs, dcq, dckv, dkrr, p['q_norm'], p['kv_norm'], swap,
                                     tag + "_mla_prep_bwd")
    dxn = _mm(dza, p['w_a'], mode="nt", name=tag + "_in_a_dx")
    dxn = _mm(dzb, p['w_b'], mode="nt", add=dxn, name=tag + "_in_b_dx")
    d_w_a = _mm(sv['xn'], dza, mode="tn", name=tag + "_in_a_dw")
    d_w_b = _mm(sv['xn'], dzb, mode="tn", name=tag + "_in_b_dw")
    dx, d_g = _rms_bwd(sv['x'], p['mix_norm'], dxn, dx1, tag + "_norm_bwd")
    d_w_in = jnp.concatenate([d_w_a[:, :640], d_w_a[:, 704:736], d_w_b], axis=1)
    d_uq = d_w_uq.reshape(MLA_Q_RANK, MLA_HEADS, MLA_DK)[:, :, :MLA_NOPE + MLA_ROPE]
    d_ukv = jnp.concatenate([d_w_uk.reshape(MLA_KV_RANK, MLA_HEADS, MLA_DK)[:, :, :MLA_NOPE],
                             d_w_uv.reshape(MLA_KV_RANK, MLA_HEADS, MLA_V)], axis=2)
    grads = dict(w_in_ab=d_w_in, mla_w_uq=d_uq, mla_w_ukv=d_ukv, w_out_ab=d_w_out, mix_norm_ab=d_g[0],
                 mla_q_norm=d_gq[0], mla_kv_norm=d_gkv[0])
    return dx, grads


def _odd_fwd(x, p, cs, seg64, swap, tag):
    xn = _rms_fwd(x, p['mix_norm'], tag + "_norm")
    q_raw = _mm(xn, p['w_q'], name=tag + "_q")
    kv_raw = _mm(xn, p['w_kv'], name=tag + "_kv")
    qh, kh, v = _gqa_prep(q_raw, kv_raw, cs, p['q_norm'], p['k_norm'], seg64, swap, tag + "_gqa_prep")
    o, lse = _flash_fwd(qh, kh, v, name=tag + "_gqa_attn", **_GQA)
    x1 = _mm(o, p['w_o'], add=x, name=tag + "_o")
    return x1, dict(x=x, xn=xn, q_raw=q_raw, kv_raw=kv_raw, qh=qh, kh=kh, v=v, o=o, lse=lse)


def _odd_bwd(dx1, p, sv, cs, seg64, swap, tag):
    do = _mm(dx1, p['w_o'], mode="nt", out_dtype=BF16, name=tag + "_o_dx")
    d_w_o = _mm(sv['o'], dx1, mode="tn", name=tag + "_o_dw")
    dqh, dkh, dv = _flash_bwd(sv['qh'], sv['kh'], sv['v'], sv['o'], do, sv['lse'], name=tag + "_gqa_attn_bwd",
                              **_GQA)
    dq_raw, dkv_raw, d_gq, d_gk = _gqa_prep_bwd(sv['q_raw'], sv['kv_raw'], cs, dqh, dkh, dv, p['q_norm'],
                                                p['k_norm'], seg64, swap, tag + "_gqa_prep_bwd")
    dxn = _mm(dq_raw, p['w_q'], mode="nt", name=tag + "_q_dx")
    dxn = _mm(dkv_raw, p['w_kv'], mode="nt", add=dxn, name=tag + "_kv_dx")
    d_w_q = _mm(sv['xn'], dq_raw, mode="tn", name=tag + "_q_dw")
    d_w_kv = _mm(sv['xn'], dkv_raw, mode="tn", name=tag + "_kv_dw")
    dx, d_g = _rms_bwd(sv['x'], p['mix_norm'], dxn, dx1, tag + "_norm_bwd")
    grads = dict(gqa_w_q=d_w_q, gqa_w_kv=d_w_kv, gqa_w_o=d_w_o, mix_norm_c=d_g[0],
                 gqa_q_norm=d_gq[0, :GQA_HEAD_DIM] + d_gq[0, GQA_HEAD_DIM:],
                 gqa_k_norm=d_gk[0, :GQA_HEAD_DIM] + d_gk[0, GQA_HEAD_DIM:])
    return dx, grads


def _ffn_fwd(x, p, tag):
    xn = _rms_fwd(x, p['ffn_norm'], tag + "_ffn_norm")
    h = _mm(xn, p['ffn_w_in'], out_dtype=BF16, name=tag + "_ffn_in")
    a = _swiglu(h, tag + "_swiglu")
    x2 = _mm(a, p['ffn_w_out'], add=x, name=tag + "_ffn_out")
    return x2, dict(x=x, xn=xn, h=h, a=a)


def _ffn_bwd(dx2, p, sv, tag):
    da = _mm(dx2, p['ffn_w_out'], mode="nt", out_dtype=BF16, name=tag + "_ffn_out_dx")
    d_w_out = _mm(sv['a'], dx2, mode="tn", name=tag + "_ffn_out_dw")
    dh = _swiglu_bwd(sv['h'], da, tag + "_swiglu_bwd")
    d_w_in = _mm(sv['xn'], dh, mode="tn", name=tag + "_ffn_in_dw")
    dxn = _mm(dh, p['ffn_w_in'], mode="nt", name=tag + "_ffn_in_dx")
    dx, d_g = _rms_bwd(sv['x'], p['ffn_norm'], dxn, dx2, tag + "_ffn_norm_bwd")
    return dx, d_w_in, d_w_out, d_g[0]


def _local_step(x, target, full, gains):
    s = x.shape[0]
    cs_mla, cs_gqa = _rope_tables(s)
    swap, seg64 = _swap_matrix(), _seg_matrix(GQA_HEAD_DIM)
    layers = _layer_weights(full, gains)
    saved = []
    for layer, p in enumerate(layers):
        tag = f"l{layer}"
        if layer % 2 == 0:
            x, sv = _even_fwd(x, p, cs_mla, swap, tag)
        else:
            x, sv = _odd_fwd(x, p, cs_gqa, seg64, swap, tag)
        x, sv_f = _ffn_fwd(x, p, tag)
        saved.append((sv, sv_f))
    dx, loss_row, d_final = _loss_head(x, target, gains['final_norm'][None], "loss_head")

    per_layer = []
    for layer in reversed(range(4)):
        p, (sv, sv_f), tag = layers[layer], saved[layer], f"l{layer}"
        dx, d_ffn_in, d_ffn_out, d_ffn_g = _ffn_bwd(dx, p, sv_f, tag)
        if layer % 2 == 0:
            dx, g = _even_bwd(dx, p, sv, cs_mla, swap, seg64, tag)
        else:
            dx, g = _odd_bwd(dx, p, sv, cs_gqa, seg64, swap, tag)
        g.update(ffn_w_in=d_ffn_in, ffn_w_out=d_ffn_out, ffn_norm=d_ffn_g)
        per_layer.append(g)
    per_layer = per_layer[::-1]

    grads = {'final_norm': d_final[0]}
    for n in ('ffn_norm', 'ffn_w_in', 'ffn_w_out'):
        grads[n] = jnp.stack([per_layer[layer][n] for layer in range(4)], axis=0)
    for n in ('mix_norm_ab', 'w_in_ab', 'mla_q_norm', 'mla_kv_norm', 'mla_w_uq', 'mla_w_ukv', 'w_out_ab'):
        grads[n] = jnp.stack([per_layer[layer][n] for layer in (0, 2)], axis=0)
    for n in ('mix_norm_c', 'gqa_w_q', 'gqa_w_kv', 'gqa_q_norm', 'gqa_k_norm', 'gqa_w_o'):
        grads[n] = jnp.stack([per_layer[layer][n] for layer in (1, 3)], axis=0)
    return loss_row, dx, grads


def _step(x, target, w, m, v):
    big_names = [n for n, _, _ in BIG]
    chip = 2 * lax.axis_index("x") + lax.axis_index("y")
    core = lax.axis_index("c")

    gathered = _all_gather_weights(_pack_blocks(w, BF16))
    full = _unpack_gathered(gathered)
    gains = {n: w[n] for n, _ in SMALL if n != 'mix_norm_c'}
    c_cols = w['mix_norm_c'].shape[1]
    own_c = lax.dynamic_update_slice(jnp.zeros((2, 4 * c_cols), F32), w['mix_norm_c'], (0, chip * c_cols))
    gains['mix_norm_c'] = _unpack_small(_all_reduce_packet(_pack_small(
        {'mix_norm_c': own_c * 0.5}, jnp.zeros((LANES,), F32))))[1]['mix_norm_c']

    loss_row, dx, grads = _local_step(x[0], target[0], full, gains)

    packed = _pack_full(grads, F32)
    other = _sibling_swap_halves(packed)
    sums = _chip_sum(packed, other, core)
    recv = _send_chip_sums(sums)
    reduced = _join_halves(_final_sum(sums, recv, chip))
    g_blocks = _unpack_blocks(reduced)

    loss, g_small = _unpack_small(_all_reduce_packet(_pack_small({n: grads[n] for n, _ in SMALL}, loss_row[0])))
    g_small['mix_norm_c'] = lax.dynamic_slice(g_small['mix_norm_c'], (0, chip * c_cols), (2, c_cols))

    out_g, out_d, out_m, out_v = {}, {}, {}, {}
    for n in big_names:
        shape = w[n].shape
        cols = shape[-1]
        d_, m_, v_ = _adamw(w[n].reshape(-1, cols), g_blocks[n].reshape(-1, cols), m[n].reshape(-1, cols),
                            v[n].reshape(-1, cols), "adamw_" + n)
        out_g[n], out_d[n], out_m[n], out_v[n] = g_blocks[n], d_.reshape(shape), m_.reshape(shape), v_.reshape(shape)
    for n, _ in SMALL:
        shape = w[n].shape
        as2d = (lambda t: t.reshape(1, -1)) if len(shape) == 1 else (lambda t: t)
        d_, m_, v_ = _adamw(as2d(w[n]), as2d(g_small[n]), as2d(m[n]), as2d(v[n]), "adamw_" + n)
        out_g[n], out_d[n], out_m[n], out_v[n] = g_small[n], d_.reshape(shape), m_.reshape(shape), v_.reshape(shape)
    return (loss, dx[None], *[out_g[n] for n in WEIGHTS], *[out_d[n] for n in WEIGHTS],
            *[out_m[n] for n in WEIGHTS], *[out_v[n] for n in WEIGHTS])


def kernel(x, mix_norm_ab, w_in_ab, mla_q_norm, mla_kv_norm, mla_w_uq, mla_w_ukv, w_out_ab, mix_norm_c, gqa_w_q, gqa_w_kv, gqa_q_norm, gqa_k_norm, gqa_w_o, ffn_norm, ffn_w_in, ffn_w_out, final_norm, loss_target, m_mix_norm_ab, m_w_in_ab, m_mla_q_norm, m_mla_kv_norm, m_mla_w_uq, m_mla_w_ukv, m_w_out_ab, m_mix_norm_c, m_gqa_w_q, m_gqa_w_kv, m_gqa_q_norm, m_gqa_k_norm, m_gqa_w_o, m_ffn_norm, m_ffn_w_in, m_ffn_w_out, m_final_norm, v_mix_norm_ab, v_w_in_ab, v_mla_q_norm, v_mla_kv_norm, v_mla_w_uq, v_mla_w_ukv, v_w_out_ab, v_mix_norm_c, v_gqa_w_q, v_gqa_w_kv, v_gqa_q_norm, v_gqa_k_norm, v_gqa_w_o, v_ffn_norm, v_ffn_w_in, v_ffn_w_out, v_final_norm):
    w = dict(zip(WEIGHTS, (mix_norm_ab, w_in_ab, mla_q_norm, mla_kv_norm, mla_w_uq, mla_w_ukv, w_out_ab, mix_norm_c,
                           gqa_w_q, gqa_w_kv, gqa_q_norm, gqa_k_norm, gqa_w_o, ffn_norm, ffn_w_in, ffn_w_out,
                           final_norm)))
    m = dict(zip(WEIGHTS, (m_mix_norm_ab, m_w_in_ab, m_mla_q_norm, m_mla_kv_norm, m_mla_w_uq, m_mla_w_ukv,
                           m_w_out_ab, m_mix_norm_c, m_gqa_w_q, m_gqa_w_kv, m_gqa_q_norm, m_gqa_k_norm, m_gqa_w_o,
                           m_ffn_norm, m_ffn_w_in, m_ffn_w_out, m_final_norm)))
    v = dict(zip(WEIGHTS, (v_mix_norm_ab, v_w_in_ab, v_mla_q_norm, v_mla_kv_norm, v_mla_w_uq, v_mla_w_ukv,
                           v_w_out_ab, v_mix_norm_c, v_gqa_w_q, v_gqa_w_kv, v_gqa_q_norm, v_gqa_k_norm, v_gqa_w_o,
                           v_ffn_norm, v_ffn_w_in, v_ffn_w_out, v_final_norm)))
    return _step(x, loss_target, w, m, v)
```

```python
import math

import numpy as np
import jax
import jax.numpy as jnp
from jax import lax
from jax.experimental import pallas as pl
from jax.experimental.pallas import tpu as pltpu

F32 = jnp.float32
BF16 = jnp.bfloat16
MESH = pl.DeviceIdType.MESH

VMEM_LIMIT_BYTES = 56 * 1024 * 1024
LANES = 128

D_MODEL = 1024
NORM_EPS = 1e-6
ROPE_THETA = 10000.0
NEG_INF = -1e30
GRID_W = 64

MLA_HEADS, MLA_Q_RANK, MLA_KV_RANK, MLA_NOPE, MLA_ROPE, MLA_V = 8, 384, 256, 64, 32, 64
MLA_DK = 128
DIL_PAIRS = ((128, 1), (512, 4), (2048, 16))
DIL_HALF, DIL_SLOTS, DIL_GROUPS, DIL_HEAD_DIM = 64, 4, 3, 64
DIL_HEADS = DIL_SLOTS * DIL_GROUPS
DIL_W = DIL_SLOTS * DIL_HEAD_DIM
GQA_HEADS, GQA_KV_HEADS, GQA_HEAD_DIM = 16, 4, 64
FFN_HIDDEN = 2816
IN_A = MLA_Q_RANK + MLA_KV_RANK + MLA_ROPE
IN_A_PAD = 768
IN_B = 3 * DIL_HEADS * DIL_HEAD_DIM

ADAM_LR, ADAM_B1, ADAM_B2, ADAM_EPS, ADAM_WD, ADAM_STEP = 0.001, 0.9, 0.999, 1e-08, 0.01, 10

LOG2E, LN2 = math.log2(math.e), math.log(2.0)
MLA_SCALE = (MLA_NOPE + MLA_ROPE) ** -0.5
GQA_SCALE = GQA_HEAD_DIM ** -0.5

WEIGHTS = ['mix_norm_ab', 'w_in_ab', 'mla_q_norm', 'mla_kv_norm', 'mla_w_uq', 'mla_w_ukv', 'w_out_ab', 'mix_norm_c',
           'gqa_w_q', 'gqa_w_kv', 'gqa_q_norm', 'gqa_k_norm', 'gqa_w_o', 'ffn_norm', 'ffn_w_in', 'ffn_w_out',
           'final_norm']
BIG = (('w_in_ab', (2, 1024, 744), 2), ('mla_w_uq', (2, 96, 8, 96), 1), ('mla_w_ukv', (2, 64, 8, 128), 1),
       ('w_out_ab', (2, 768, 256), 2), ('gqa_w_q', (2, 256, 1024), 1), ('gqa_w_kv', (2, 256, 512), 1),
       ('gqa_w_o', (2, 256, 1024), 1), ('ffn_w_in', (4, 1024, 1408), 2), ('ffn_w_out', (4, 704, 1024), 1))
PACK_W = 1024
PACK_ROWS = sum(math.prod(s) for _, s, _ in BIG) // PACK_W
HALF_ROWS = PACK_ROWS // 2
SMALL = (('mix_norm_ab', (2, 1024)), ('mla_q_norm', (2, 384)), ('mla_kv_norm', (2, 256)), ('gqa_q_norm', (2, 64)),
         ('gqa_k_norm', (2, 64)), ('ffn_norm', (4, 1024)), ('final_norm', (1024,)), ('mix_norm_c', (2, 1024)))
PACKET_ROWS = 88


def _cparams(sem=None):
    return pltpu.CompilerParams(dimension_semantics=sem, vmem_limit_bytes=VMEM_LIMIT_BYTES)


def _pick(n, pref, mult=LANES):
    if n <= pref:
        return n
    for d in range(pref - pref % mult, 0, -mult):
        if n % d == 0:
            return d
    return n


_DIMS = {"nn": (((1,), (0,)), ((), ())), "nt": (((1,), (1,)), ((), ())), "tn": (((0,), (0,)), ((), ()))}


def _mm(a, b, *, mode="nn", add=None, out_dtype=F32, name="mm"):
    if mode == "nn":
        (m, k), (k2, n) = a.shape, b.shape
    elif mode == "nt":
        (m, k), (n, k2) = a.shape, b.shape
    else:
        (k, m), (k2, n) = a.shape, b.shape
    assert k == k2, (a.shape, b.shape, mode)
    if mode == "tn":
        bm, bn, bk = _pick(m, 1408), _pick(n, 1024), _pick(k, 1024, 16)
    else:
        bm, bn, bk = _pick(m, 512, 16), _pick(n, 1408), _pick(k, 1408)
    nk = k // bk
    assert m % bm == 0 and n % bn == 0 and k % bk == 0
    has_add = add is not None

    def body(*refs):
        a_ref, b_ref = refs[0], refs[1]
        add_ref = refs[2] if has_add else None
        o_ref = refs[3 if has_add else 2]
        part = lax.dot_general(a_ref[...].astype(BF16), b_ref[...].astype(BF16), _DIMS[mode],
                               preferred_element_type=F32)

        def finish(r):
            if has_add:
                r = r + add_ref[...]
            o_ref[...] = r.astype(o_ref.dtype)

        if nk == 1:
            finish(part)
        else:
            acc_ref = refs[-1]
            kk = pl.program_id(2)

            @pl.when(kk == 0)
            def _():
                acc_ref[...] = part

            @pl.when(kk > 0)
            def _():
                acc_ref[...] += part

            @pl.when(kk == nk - 1)
            def _():
                finish(acc_ref[...])

    if mode == "nn":
        a_spec = pl.BlockSpec((bm, bk), lambda i, j, kk: (i, kk))
        b_spec = pl.BlockSpec((bk, bn), lambda i, j, kk: (kk, j))
    elif mode == "nt":
        a_spec = pl.BlockSpec((bm, bk), lambda i, j, kk: (i, kk))
        b_spec = pl.BlockSpec((bn, bk), lambda i, j, kk: (j, kk))
    else:
        a_spec = pl.BlockSpec((bk, bm), lambda i, j, kk: (kk, i))
        b_spec = pl.BlockSpec((bk, bn), lambda i, j, kk: (kk, j))
    o_spec = pl.BlockSpec((bm, bn), lambda i, j, kk: (i, j))
    in_specs = [a_spec, b_spec] + ([o_spec] if has_add else [])
    args = (a, b) + ((add,) if has_add else ())
    return pl.pallas_call(
        body, name=name, grid=(m // bm, n // bn, nk), in_specs=in_specs, out_specs=o_spec,
        out_shape=jax.ShapeDtypeStruct((m, n), out_dtype),
        scratch_shapes=[pltpu.VMEM((bm, bn), F32)] if nk > 1 else [],
        compiler_params=_cparams(("parallel", "parallel", "arbitrary")),
    )(*args)


def _rows_call(fn, rows, consts, out_rows, out_accs=(), *, bs=256, name):
    s = rows[0].shape[0]
    bs = min(bs, s)
    assert s % bs == 0
    nr, nc, no, na = len(rows), len(consts), len(out_rows), len(out_accs)

    def body(*refs):
        vals = [r[...] for r in refs[:nr + nc]]
        outs = refs[nr + nc:]
        res = fn(*vals)
        if not isinstance(res, (tuple, list)):
            res = (res,)
        assert len(res) == no + na, (len(res), no, na)
        for r, v in zip(outs[:no], res[:no]):
            r[...] = v.astype(r.dtype)
        if na:
            i = pl.program_id(0)
            for r, v in zip(outs[no:], res[no:]):
                @pl.when(i == 0)
                def _(r=r, v=v):
                    r[...] = v

                @pl.when(i > 0)
                def _(r=r, v=v):
                    r[...] += v

    in_specs = [pl.BlockSpec((bs, a.shape[1]), lambda i: (i, 0)) for a in rows]
    in_specs += [pl.BlockSpec(c.shape, lambda i: (0, 0)) for c in consts]
    out_specs = [pl.BlockSpec((bs, c), lambda i: (i, 0)) for c, _ in out_rows]
    out_specs += [pl.BlockSpec(tuple(sh), lambda i: (0, 0)) for sh in out_accs]
    out_shape = [jax.ShapeDtypeStruct((s, c), dt) for c, dt in out_rows]
    out_shape += [jax.ShapeDtypeStruct(tuple(sh), F32) for sh in out_accs]
    res = pl.pallas_call(
        body, name=name, grid=(s // bs,), in_specs=in_specs, out_specs=out_specs, out_shape=out_shape,
        compiler_params=_cparams(("arbitrary",) if na else ("parallel",)),
    )(*rows, *consts)
    return res


def _rms(x, g):
    return x * lax.rsqrt(jnp.mean(x * x, axis=-1, keepdims=True) + NORM_EPS) * g


def _rms_bwd_math(x, g, dy):
    r = lax.rsqrt(jnp.mean(x * x, axis=-1, keepdims=True) + NORM_EPS)
    u = dy * g
    dx = r * u - x * (r * r * r) * jnp.mean(u * x, axis=-1, keepdims=True)
    dg = jnp.sum(dy * x * r, axis=0, keepdims=True)
    return dx, dg


def _rms_fwd(x, g, name):
    return _rows_call(lambda xv, gv: _rms(xv, gv), [x], [g], [(x.shape[1], BF16)], name=name)[0]


def _rms_bwd(x, g, dy, dres, name):
    def fn(xv, dyv, dresv, gv):
        dx, dg = _rms_bwd_math(xv, gv, dyv.astype(F32))
        return dx + dresv, dg
    return _rows_call(fn, [x, dy, dres], [g], [(x.shape[1], F32)], [(1, x.shape[1])], name=name)


def _chunkdot(x, m):
    outs = [jnp.dot(x[:, c:c + LANES], m, precision=lax.Precision.HIGHEST, preferred_element_type=F32)
            for c in range(0, x.shape[1], LANES)]
    return outs[0] if len(outs) == 1 else jnp.concatenate(outs, axis=1)


def _lanes(t, width):
    n = width // LANES
    return t if n == 1 else jnp.concatenate([t] * n, axis=1)


def _rope(x, cs, swap):
    w = x.shape[1]
    return x * _lanes(cs[:, :LANES], w) + _chunkdot(x, swap) * _lanes(cs[:, LANES:], w)


def _rope_t(dy, cs, swap):
    w = dy.shape[1]
    return dy * _lanes(cs[:, :LANES], w) + _chunkdot(dy * _lanes(cs[:, LANES:], w), swap)


def _swap_matrix():
    m = np.zeros((LANES, LANES), np.float32)
    for j in range(LANES):
        src = j + 16 if (j % 32) < 16 else j - 16
        m[src, j] = 1.0
    return jnp.asarray(m)


def _seg_matrix(seg):
    idx = np.arange(LANES) // seg
    return jnp.asarray((idx[:, None] == idx[None, :]).astype(np.float32))


def _rope_tables(s):
    pos = jnp.arange(s)

    def angles(p, dim):
        freqs = ROPE_THETA ** (-jnp.arange(0, dim, 2, dtype=F32) / dim)
        ang = p.astype(F32)[:, None] * freqs[None, :]
        return jnp.cos(ang), jnp.sin(ang)

    cos_t, sin_t = angles(pos, MLA_ROPE)
    one, zero = jnp.ones((s, 64), F32), jnp.zeros((s, 64), F32)
    mla = jnp.concatenate([one, cos_t, cos_t, one[:, :32], zero, -sin_t, sin_t, zero[:, :32]], axis=1)
    cos_r, sin_r = angles(pos // GRID_W, GQA_HEAD_DIM // 2)
    cos_c, sin_c = angles(pos % GRID_W, GQA_HEAD_DIM // 2)
    c64 = jnp.concatenate([cos_r, cos_r, cos_c, cos_c], axis=1)
    s64 = jnp.concatenate([-sin_r, sin_r, -sin_c, sin_c], axis=1)
    gqa = jnp.concatenate([c64, c64, s64, s64], axis=1)
    return mla, gqa


def _col_to_row(col):
    return jnp.transpose(jnp.broadcast_to(col, (col.shape[0], LANES)))[0:1, :]


def _stack_heads(ref, heads, d, dtype=None):
    parts = [ref[:, hd * d:(hd + 1) * d] for hd in heads]
    out = parts[0] if len(parts) == 1 else jnp.concatenate(parts, axis=0)
    return out if dtype is None else out.astype(dtype)


def _fill_v_ones(v_ref, va_ref, hb, dv):
    @pl.when(pl.program_id(1) == 0)
    def _():
        ones = jnp.ones((v_ref.shape[0], dv), BF16)
        for h in range(hb):
            va_ref[:, 2 * h * dv:(2 * h + 1) * dv] = v_ref[:, h * dv:(h + 1) * dv]
            va_ref[:, (2 * h + 1) * dv:(2 * h + 2) * dv] = ones


def _flash_fwd(q, k, v, *, R, dk, dv, hb, bq, bk, name):
    s = q.shape[0]
    g = k.shape[1] // dk
    ng = g // hb
    bq, bk = min(bq, s), min(bk, s)
    nq, nkb = s // bq, s // bk
    rb = R * bq

    def body(q_ref, k_ref, v_ref, o_ref, lse_ref, va_ref):
        _fill_v_ones(v_ref, va_ref, hb, dv)
        head_sets = [[h * R + r for r in range(R)] for h in range(hb)]
        qss = [_stack_heads(q_ref, heads, dk) for heads in head_sets]

        def step(jj, carry):
            carry = list(carry)
            for u in range(unroll):
                r0 = pl.multiple_of((jj * unroll + u) * bk, bk)
                for h in range(hb):
                    m, acc = carry[h]
                    kj = k_ref[pl.ds(r0, bk), h * dk:(h + 1) * dk]
                    va = va_ref[pl.ds(r0, bk), 2 * h * dv:2 * (h + 1) * dv]
                    sc = lax.dot_general(qss[h], kj, _DIMS["nt"], preferred_element_type=F32)
                    m2 = jnp.maximum(m, jnp.max(sc, axis=1, keepdims=True))
                    p = jnp.exp2(sc - m2).astype(BF16)
                    carry[h] = (m2, jnp.exp2(m - m2) * acc + jnp.dot(p, va, preferred_element_type=F32))
            return tuple(carry)

        unroll = 2 if nkb % 2 == 0 else 1
        init = tuple((jnp.full((rb, 1), NEG_INF, F32), jnp.zeros((rb, 2 * dv), F32)) for _ in range(hb))
        final = lax.fori_loop(0, nkb // unroll, step, init)
        for h, heads in enumerate(head_sets):
            m, acc = final[h]
            l = acc[:, dv:dv + 1]
            o = acc[:, :dv] / l
            row = _col_to_row(m + jnp.log2(l))
            for r, hd in enumerate(heads):
                o_ref[:, hd * dv:(hd + 1) * dv] = o[r * bq:(r + 1) * bq].astype(o_ref.dtype)
                lse_ref[0, hd:hd + 1, :] = row[:, r * bq:(r + 1) * bq]

    return pl.pallas_call(
        body, name=name, grid=(ng, nq),
        in_specs=[pl.BlockSpec((bq, hb * R * dk), lambda gi, i: (i, gi)),
                  pl.BlockSpec((s, hb * dk), lambda gi, i: (0, gi)),
                  pl.BlockSpec((s, hb * dv), lambda gi, i: (0, gi))],
        out_specs=[pl.BlockSpec((bq, hb * R * dv), lambda gi, i: (i, gi)),
                   pl.BlockSpec((1, hb * R, bq), lambda gi, i: (gi, 0, i))],
        out_shape=[jax.ShapeDtypeStruct((s, g * R * dv), BF16), jax.ShapeDtypeStruct((ng, hb * R, s), F32)],
        scratch_shapes=[pltpu.VMEM((s, 2 * hb * dv), BF16)],
        compiler_params=_cparams(("parallel", "arbitrary")),
    )(q, k, v)


def _flash_bwd(q, k, v, o, do, lse, *, R, dk, dv, hb, bq, bk, name):
    s = q.shape[0]
    g = k.shape[1] // dk
    ng = g // hb
    bq, bk = min(bq, s), min(bk, s)
    nq, nkb = s // bq, s // bk
    rb = R * bq

    def body(q_ref, k_ref, v_ref, o_ref, do_ref, lse_ref, dq_ref, dk_ref, dv_ref, va_ref):
        @pl.when(pl.program_id(1) == 0)
        def _():
            dk_ref[...] = jnp.zeros(dk_ref.shape, F32)
            dv_ref[...] = jnp.zeros(dv_ref.shape, F32)

        _fill_v_ones(v_ref, va_ref, hb, dv)
        lane = lax.broadcasted_iota(jnp.int32, (rb, dv), 1)
        head_sets = [[h * R + r for r in range(R)] for h in range(hb)]
        qss, doss, dosas, lrows = [], [], [], []
        for heads in head_sets:
            dos = _stack_heads(do_ref, heads, dv, BF16)
            delta = jnp.sum(dos.astype(F32) * _stack_heads(o_ref, heads, dv, F32), axis=1, keepdims=True)
            hi = delta.astype(BF16).astype(F32)
            lo = delta - hi
            qss.append(_stack_heads(q_ref, heads, dk))
            doss.append(dos)
            dosas.append(jnp.concatenate(
                [dos, jnp.where(lane == 0, -hi, jnp.where(lane == 1, -lo, 0.0)).astype(BF16)], axis=1))
            rows = [lse_ref[0, hd:hd + 1, :] for hd in heads]
            lrows.append(rows[0] if R == 1 else jnp.concatenate(rows, axis=1))

        def step(j, dqs):
            r0 = pl.multiple_of(j * bk, bk)
            out = []
            for h in range(hb):
                kj = k_ref[pl.ds(r0, bk), h * dk:(h + 1) * dk]
                va = va_ref[pl.ds(r0, bk), 2 * h * dv:2 * (h + 1) * dv]
                st = lax.dot_general(kj, qss[h], _DIMS["nt"], preferred_element_type=F32)
                pt = jnp.exp2(st - lrows[h])
                dv_ref[pl.ds(r0, bk), h * dv:(h + 1) * dv] += jnp.dot(pt.astype(BF16), doss[h],
                                                                      preferred_element_type=F32)
                dst = (pt * lax.dot_general(va, dosas[h], _DIMS["nt"], preferred_element_type=F32)).astype(BF16)
                dk_ref[pl.ds(r0, bk), h * dk:(h + 1) * dk] += jnp.dot(dst, qss[h], preferred_element_type=F32)
                out.append(dqs[h] + lax.dot_general(dst, kj, _DIMS["tn"], preferred_element_type=F32))
            return tuple(out)

        dqs = lax.fori_loop(0, nkb, step, tuple(jnp.zeros((rb, dk), F32) for _ in range(hb)))
        for h, heads in enumerate(head_sets):
            for r, hd in enumerate(heads):
                dq_ref[:, hd * dk:(hd + 1) * dk] = dqs[h][r * bq:(r + 1) * bq]

    qspec = pl.BlockSpec((bq, hb * R * dk), lambda gi, i: (i, gi))
    ospec = pl.BlockSpec((bq, hb * R * dv), lambda gi, i: (i, gi))
    kspec = pl.BlockSpec((s, hb * dk), lambda gi, i: (0, gi))
    vspec = pl.BlockSpec((s, hb * dv), lambda gi, i: (0, gi))
    return pl.pallas_call(
        body, name=name, grid=(ng, nq),
        in_specs=[qspec, kspec, vspec, ospec, ospec, pl.BlockSpec((1, hb * R, bq), lambda gi, i: (gi, 0, i))],
        out_specs=[qspec, kspec, vspec],
        out_shape=[jax.ShapeDtypeStruct((s, g * R * dk), F32), jax.ShapeDtypeStruct((s, g * dk), F32),
                   jax.ShapeDtypeStruct((s, g * dv), F32)],
        scratch_shapes=[pltpu.VMEM((s, 2 * hb * dv), BF16)],
        compiler_params=_cparams(("parallel", "arbitrary")),
    )(q, k, v, o, do, lse)


DIL_BQ = 128
DIL_NCOL = IN_B // DIL_W


def _alibi_slope(head):
    return float(2.0 ** (-8.0 * (head + 1) / DIL_HEADS))


def _band(i, d, length, queries_first):
    if queries_first:
        shape = (3 * DIL_BQ, DIL_BQ)
        wide = (i - 1) * DIL_BQ + lax.broadcasted_iota(jnp.int32, shape, 0)
        narrow = i * DIL_BQ + lax.broadcasted_iota(jnp.int32, shape, 1)
    else:
        shape = (DIL_BQ, 3 * DIL_BQ)
        narrow = i * DIL_BQ + lax.broadcasted_iota(jnp.int32, shape, 0)
        wide = (i - 1) * DIL_BQ + lax.broadcasted_iota(jnp.int32, shape, 1)
    rel = jnp.abs(wide - narrow)
    valid = (rel <= DIL_HALF) & (wide >= 0) & (wide < length)
    return valid, rel.astype(F32) * float(d)


def _nbr_specs(col_of, nb):
    return [pl.BlockSpec((DIL_BQ, DIL_W), lambda r, i: (jnp.maximum(i - 1, 0), col_of(r))),
            pl.BlockSpec((DIL_BQ, DIL_W), lambda r, i: (i, col_of(r))),
            pl.BlockSpec((DIL_BQ, DIL_W), lambda r, i: (jnp.minimum(i + 1, nb - 1), col_of(r)))]


def _cat3(refs, sl, dtype=None):
    out = jnp.concatenate([r[:, sl] for r in refs], axis=0)
    return out if dtype is None else out.astype(dtype)


def _dil_fwd(zb, grp, name):
    s = zb.shape[0]
    d = DIL_PAIRS[grp][1]
    length = s // d
    nb = length // DIL_BQ
    zv = zb.reshape(length, d * IN_B)
    scale = DIL_HEAD_DIM ** -0.5

    def body(q_ref, kp, kc, kn, vp, vc, vn, o_ref, lse_ref):
        valid, dist = _band(pl.program_id(1), d, length, False)
        outs, lses = [], []
        for sl_i in range(DIL_SLOTS):
            sl = slice(sl_i * DIL_HEAD_DIM, (sl_i + 1) * DIL_HEAD_DIM)
            kcat, vcat = _cat3((kp, kc, kn), sl), _cat3((vp, vc, vn), sl)
            sc = lax.dot_general(q_ref[:, sl], kcat, _DIMS["nt"], preferred_element_type=F32) * scale
            sc = jnp.where(valid, sc - _alibi_slope(grp * DIL_SLOTS + sl_i) * dist, NEG_INF)
            m = jnp.max(sc, axis=1, keepdims=True)
            e = jnp.exp(sc - m)
            den = jnp.sum(e, axis=1, keepdims=True)
            outs.append(jnp.dot((e / den).astype(BF16), vcat, preferred_element_type=F32))
            lses.append(jnp.broadcast_to(m + jnp.log(den), (DIL_BQ, DIL_HEAD_DIM)))
        o_ref[...] = jnp.concatenate(outs, axis=1)
        lse_ref[...] = jnp.concatenate(lses, axis=1)

    cur = pl.BlockSpec((DIL_BQ, DIL_W), lambda r, i: (i, r * DIL_NCOL + grp))
    out_spec = pl.BlockSpec((DIL_BQ, DIL_W), lambda r, i: (i, r))
    o, lse = pl.pallas_call(
        body, name=name, grid=(d, nb),
        in_specs=[cur] + _nbr_specs(lambda r: r * DIL_NCOL + 3 + grp, nb)
        + _nbr_specs(lambda r: r * DIL_NCOL + 6 + grp, nb),
        out_specs=[out_spec, out_spec],
        out_shape=[jax.ShapeDtypeStruct((length, d * DIL_W), F32)] * 2,
        compiler_params=_cparams(("parallel", "parallel")),
    )(zv, zv, zv, zv, zv, zv, zv)
    return o.reshape(s, DIL_W), lse.reshape(s, DIL_W)


def _dil_bwd(zb, do, lse, dl, grp, name):
    s = zb.shape[0]
    d = DIL_PAIRS[grp][1]
    length = s // d
    nb = length // DIL_BQ
    zv = zb.reshape(length, d * IN_B)
    dov, lsev, dlv = (t.reshape(length, d * DIL_W) for t in (do, lse, dl))
    scale = DIL_HEAD_DIM ** -0.5

    def probs(sc, valid, dist, sl_i, lse_col):
        sc = jnp.where(valid, sc * scale - _alibi_slope(grp * DIL_SLOTS + sl_i) * dist, NEG_INF)
        return jnp.exp(sc - lse_col)

    def dq_body(q_ref, kp, kc, kn, vp, vc, vn, do_ref, lse_ref, dl_ref, dq_ref):
        valid, dist = _band(pl.program_id(1), d, length, False)
        outs = []
        for sl_i in range(DIL_SLOTS):
            sl = slice(sl_i * DIL_HEAD_DIM, (sl_i + 1) * DIL_HEAD_DIM)
            c0 = sl_i * DIL_HEAD_DIM
            kcat, vcat = _cat3((kp, kc, kn), sl), _cat3((vp, vc, vn), sl)
            sc = lax.dot_general(q_ref[:, sl], kcat, _DIMS["nt"], preferred_element_type=F32)
            p = probs(sc, valid, dist, sl_i, lse_ref[:, c0:c0 + 1])
            dp = lax.dot_general(do_ref[:, sl], vcat, _DIMS["nt"], preferred_element_type=F32)
            ds = (p * (dp - dl_ref[:, c0:c0 + 1]) * scale).astype(BF16)
            outs.append(jnp.dot(ds, kcat, preferred_element_type=F32))
        dq_ref[...] = jnp.concatenate(outs, axis=1).astype(dq_ref.dtype)

    def dkv_body(k_ref, v_ref, qp, qc, qn, dop, doc, don, lp, lc, ln, dlp, dlc, dln, dk_ref, dv_ref):
        valid, dist = _band(pl.program_id(1), d, length, True)
        dks, dvs = [], []
        for sl_i in range(DIL_SLOTS):
            sl = slice(sl_i * DIL_HEAD_DIM, (sl_i + 1) * DIL_HEAD_DIM)
            one = slice(sl_i * DIL_HEAD_DIM, sl_i * DIL_HEAD_DIM + 1)
            qcat, docat = _cat3((qp, qc, qn), sl), _cat3((dop, doc, don), sl)
            sc = lax.dot_general(qcat, k_ref[:, sl], _DIMS["nt"], preferred_element_type=F32)
            p = probs(sc, valid, dist, sl_i, _cat3((lp, lc, ln), one))
            dvs.append(lax.dot_general(p.astype(BF16), docat, _DIMS["tn"], preferred_element_type=F32))
            dp = lax.dot_general(docat, v_ref[:, sl], _DIMS["nt"], preferred_element_type=F32)
            ds = (p * (dp - _cat3((dlp, dlc, dln), one)) * scale).astype(BF16)
            dks.append(lax.dot_general(ds, qcat, _DIMS["tn"], preferred_element_type=F32))
        dk_ref[...] = jnp.concatenate(dks, axis=1).astype(dk_ref.dtype)
        dv_ref[...] = jnp.concatenate(dvs, axis=1).astype(dv_ref.dtype)

    def zcur(c):
        return pl.BlockSpec((DIL_BQ, DIL_W), lambda r, i: (i, r * DIL_NCOL + c + grp))

    own = pl.BlockSpec((DIL_BQ, DIL_W), lambda r, i: (i, r))
    view = jax.ShapeDtypeStruct((length, d * DIL_W), BF16)
    dq = pl.pallas_call(
        dq_body, name=name + "_dq", grid=(d, nb),
        in_specs=[zcur(0)] + _nbr_specs(lambda r: r * DIL_NCOL + 3 + grp, nb)
        + _nbr_specs(lambda r: r * DIL_NCOL + 6 + grp, nb) + [own, own, own],
        out_specs=own, out_shape=view, compiler_params=_cparams(("parallel", "parallel")),
    )(zv, zv, zv, zv, zv, zv, zv, dov, lsev, dlv)
    own3 = _nbr_specs(lambda r: r, nb)
    dk, dv = pl.pallas_call(
        dkv_body, name=name + "_dkv", grid=(d, nb),
        in_specs=[zcur(3), zcur(6)] + _nbr_specs(lambda r: r * DIL_NCOL + grp, nb) + own3 + own3 + own3,
        out_specs=[own, own], out_shape=[view, view], compiler_params=_cparams(("parallel", "parallel")),
    )(zv, zv, zv, zv, zv, dov, dov, dov, lsev, lsev, lsev, dlv, dlv, dlv)
    return dq.reshape(s, DIL_W), dk.reshape(s, DIL_W), dv.reshape(s, DIL_W)


def _dil_combine(os_, ls_, name):
    def fn(o0, o1, o2, l0, l1, l2):
        m = jnp.maximum(jnp.maximum(l0, l1), l2)
        e0, e1, e2 = jnp.exp(l0 - m), jnp.exp(l1 - m), jnp.exp(l2 - m)
        den = e0 + e1 + e2
        comb = (e0 / den) * o0 + (e1 / den) * o1 + (e2 / den) * o2
        return comb, m + jnp.log(den)
    return _rows_call(fn, list(os_) + list(ls_), [], [(DIL_W, BF16), (DIL_W, F32)], name=name)


def _dil_combine_bwd(dcomb, os_, ls_, lt, seg64, name):
    def fn(dc, o0, o1, o2, l0, l1, l2, ltv, seg):
        w = [jnp.exp(l - ltv) for l in (l0, l1, l2)]
        comb = w[0] * o0 + w[1] * o1 + w[2] * o2
        t = _chunkdot(dc * comb, seg)
        return [wg * dc for wg in w] + [wg * t for wg in w]
    return _rows_call(fn, [dcomb] + list(os_) + list(ls_) + [lt], [seg64],
                      [(DIL_W, BF16)] * 3 + [(DIL_W, F32)] * 3, name=name)


def _mla_prep(za, gq, gkv, cs, swap, name):
    def fn(z, csv, gqv, gkvv, sw):
        return (_rms(z[:, :MLA_Q_RANK], gqv), _rms(z[:, MLA_Q_RANK:640], gkvv), _rope(z[:, 640:], csv, sw))
    return _rows_call(fn, [za, cs], [gq, gkv, swap], [(MLA_Q_RANK, BF16), (MLA_KV_RANK, BF16), (LANES, F32)],
                      name=name)


def _mla_prep_bwd(za, cs, dcq, dckv, dkr, gq, gkv, swap, name):
    def fn(z, csv, dcqv, dckvv, dkrv, gqv, gkvv, sw):
        d1, dg1 = _rms_bwd_math(z[:, :MLA_Q_RANK], gqv, dcqv)
        d2, dg2 = _rms_bwd_math(z[:, MLA_Q_RANK:640], gkvv, dckvv)
        d3 = _rope_t(dkrv, csv, sw)
        return jnp.concatenate([d1, d2, d3], axis=1), dg1, dg2
    return _rows_call(fn, [za, cs, dcq, dckv, dkr], [gq, gkv, swap], [(IN_A_PAD, BF16)],
                      [(1, MLA_Q_RANK), (1, MLA_KV_RANK)], name=name)


def _mla_qk(q_raw, k_pad, krr, cs, swap, name):
    w = MLA_HEADS * MLA_DK

    def fn(qv, kv, krv, csv, sw):
        return _rope(qv, csv, sw) * (MLA_SCALE * LOG2E), kv + _lanes(krv, w)
    return _rows_call(fn, [q_raw, k_pad, krr, cs], [swap], [(w, BF16), (w, BF16)], name=name)


def _mla_qk_bwd(dqh, dkh, cs, swap, name):
    w = MLA_HEADS * MLA_DK

    def fn(dq, dk, csv, sw):
        dk = dk * LN2
        acc = dk[:, :LANES]
        for h in range(1, MLA_HEADS):
            acc = acc + dk[:, h * LANES:(h + 1) * LANES]
        lane = lax.broadcasted_iota(jnp.int32, acc.shape, 1)
        acc = jnp.where((lane >= MLA_NOPE) & (lane < MLA_NOPE + MLA_ROPE), acc, 0.0)
        return _rope_t(dq * MLA_SCALE, csv, sw), dk, acc
    return _rows_call(fn, [dqh, dkh, cs], [swap], [(w, BF16), (w, BF16), (LANES, F32)], name=name)


def _head_norm(t, g2, seg):
    r = lax.rsqrt(_chunkdot(t * t, seg) * (1.0 / GQA_HEAD_DIM) + NORM_EPS)
    return t * r * _lanes(g2, t.shape[1]), r


def _head_norm_bwd(t, g2, seg, dn):
    w = t.shape[1]
    r = lax.rsqrt(_chunkdot(t * t, seg) * (1.0 / GQA_HEAD_DIM) + NORM_EPS)
    u = dn * _lanes(g2, w)
    dt = r * u - t * (r * r * r) * (_chunkdot(u * t, seg) * (1.0 / GQA_HEAD_DIM))
    dgw = jnp.sum(dn * t * r, axis=0, keepdims=True)
    dg = dgw[:, :LANES]
    for c in range(LANES, w, LANES):
        dg = dg + dgw[:, c:c + LANES]
    return dt, dg


def _gqa_prep(q_raw, kv_raw, cs, gq2, gk2, seg, swap, name):
    kw = GQA_KV_HEADS * GQA_HEAD_DIM

    def fn(qv, kvv, csv, gqv, gkv, sg, sw):
        qn, _ = _head_norm(qv, gqv, sg)
        kn, _ = _head_norm(kvv[:, :kw], gkv, sg)
        return _rope(qn, csv, sw) * (GQA_SCALE * LOG2E), _rope(kn, csv, sw), kvv[:, kw:]
    return _rows_call(fn, [q_raw, kv_raw, cs], [gq2, gk2, seg, swap],
                      [(GQA_HEADS * GQA_HEAD_DIM, BF16), (kw, BF16), (kw, BF16)], name=name)


def _gqa_prep_bwd(q_raw, kv_raw, cs, dqh, dkh, dv, gq2, gk2, seg, swap, name):
    kw = GQA_KV_HEADS * GQA_HEAD_DIM

    def fn(qv, kvv, csv, dq, dk, dvv, gqv, gkv, sg, sw):
        dqr, dgq = _head_norm_bwd(qv, gqv, sg, _rope_t(dq * GQA_SCALE, csv, sw))
        dkr, dgk = _head_norm_bwd(kvv[:, :kw], gkv, sg, _rope_t(dk * LN2, csv, sw))
        return dqr, jnp.concatenate([dkr, dvv], axis=1), dgq, dgk
    return _rows_call(fn, [q_raw, kv_raw, cs, dqh, dkh, dv], [gq2, gk2, seg, swap],
                      [(GQA_HEADS * GQA_HEAD_DIM, BF16), (2 * kw, BF16)], [(1, LANES), (1, LANES)], name=name)


def _swiglu(h, name):
    def fn(hv):
        gate, up = hv[:, :FFN_HIDDEN].astype(F32), hv[:, FFN_HIDDEN:].astype(F32)
        return gate / (1.0 + jnp.exp(-gate)) * up
    return _rows_call(fn, [h], [], [(FFN_HIDDEN, BF16)], name=name)[0]


def _swiglu_bwd(h, da, name):
    def fn(hv, dav):
        gate, up = hv[:, :FFN_HIDDEN].astype(F32), hv[:, FFN_HIDDEN:].astype(F32)
        dav = dav.astype(F32)
        sig = 1.0 / (1.0 + jnp.exp(-gate))
        silu = gate * sig
        dgate = dav * up * (sig + silu * (1.0 - sig))
        return jnp.concatenate([dgate, dav * silu], axis=1)
    return _rows_call(fn, [h, da], [], [(2 * FFN_HIDDEN, BF16)], name=name)[0]


def _loss_head(x, target, g, name):
    dm = x.shape[1]

    def fn(xv, tv, gv):
        err = _rms(xv, gv) - tv
        loss = 0.5 * jnp.sum(err * err) / dm
        dx, dg = _rms_bwd_math(xv, gv, err * (1.0 / dm))
        return dx, jnp.zeros((1, LANES), F32) + loss, dg
    return _rows_call(fn, [x, target], [g], [(dm, F32)], [(1, LANES), (1, dm)], name=name)


def _adamw(w, g, m, v, name):
    def fn(wv, gv, mv, vv):
        m2 = ADAM_B1 * mv + (1.0 - ADAM_B1) * gv
        v2 = ADAM_B2 * vv + (1.0 - ADAM_B2) * (gv * gv)
        m_hat = m2 / (1.0 - ADAM_B1 ** ADAM_STEP)
        v_hat = v2 / (1.0 - ADAM_B2 ** ADAM_STEP)
        return -ADAM_LR * (m_hat / (jnp.sqrt(v_hat) + ADAM_EPS) + ADAM_WD * wv), m2, v2
    c = w.shape[1]
    return _rows_call(fn, [w, g, m, v], [], [(c, F32)] * 3, bs=_pick(w.shape[0], 256, 8), name=name)


HBM_SPEC = pl.BlockSpec(memory_space=pltpu.HBM)
VMEM_SPEC = pl.BlockSpec(memory_space=pltpu.VMEM)


def _position():
    return lax.axis_index("x"), lax.axis_index("y"), lax.axis_index("c")


def _other_chips(x, y):
    return [(1 - x, y), (x, 1 - y), (1 - x, 1 - y)]


def _all_gather_weights(packed):
    rows = packed.shape[0]
    hr = rows // 2

    def body(p_ref, g_ref, send_sems, recv_sems, local_sem):
        x, y, c = _position()
        r0 = c * hr
        chips = _other_chips(x, y)

        def half(chip, hc):
            return g_ref.at[2 * chip[0] + chip[1], pl.ds(hc * hr, hr), :]

        def copy(j, src, dst, to):
            return pltpu.make_async_remote_copy(src_ref=src, dst_ref=dst, send_sem=send_sems.at[j],
                                                recv_sem=recv_sems.at[j], device_id=to, device_id_type=MESH)

        mine = pltpu.make_async_copy(p_ref, g_ref.at[2 * x + y], local_sem)
        mine.start()
        first = [copy(j, p_ref.at[pl.ds(r0, hr), :], half((x, y), c), (*chip, c)) for j, chip in enumerate(chips)]
        for cp in first:
            cp.start()
        passed = [copy(3 + j, half(chip, c), half(chip, c), (x, y, 1 - c)) for j, chip in enumerate(chips)]
        for j, chip in enumerate(chips):
            copy(j, half(chip, c), half(chip, c), (x, y, c)).wait_recv()
            passed[j].start()
        for j, chip in enumerate(chips):
            copy(3 + j, half(chip, 1 - c), half(chip, 1 - c), (x, y, c)).wait_recv()
        for cp in first + passed:
            cp.wait_send()
        mine.wait()

    return pl.pallas_call(
        body, name="all_gather_weights", in_specs=[HBM_SPEC], out_specs=HBM_SPEC,
        out_shape=jax.ShapeDtypeStruct((4, rows, packed.shape[1]), packed.dtype),
        scratch_shapes=[pltpu.SemaphoreType.DMA((6,)), pltpu.SemaphoreType.DMA((6,)), pltpu.SemaphoreType.DMA],
    )(packed)


def _sibling_swap_halves(grads):
    hr = grads.shape[1] // 2

    def body(g_ref, a_ref, send_sem, recv_sem):
        x, y, c = _position()
        cp = pltpu.make_async_remote_copy(src_ref=g_ref.at[:, pl.ds((1 - c) * hr, hr), :], dst_ref=a_ref,
                                          send_sem=send_sem, recv_sem=recv_sem, device_id=(x, y, 1 - c),
                                          device_id_type=MESH)
        cp.start()
        cp.wait()

    return pl.pallas_call(
        body, name="grad_swap_cores", in_specs=[HBM_SPEC], out_specs=HBM_SPEC,
        out_shape=jax.ShapeDtypeStruct((4, hr, grads.shape[2]), grads.dtype),
        scratch_shapes=[pltpu.SemaphoreType.DMA, pltpu.SemaphoreType.DMA],
    )(grads)


RS_BLOCK = 848


def _chip_sum(grads, other, c):
    hr = other.shape[1]
    nblk = hr // RS_BLOCK
    width = grads.shape[2]

    def body(c_ref, g_ref, a_ref, o_ref):
        o_ref[...] = g_ref[...] + a_ref[...]

    blk = (1, RS_BLOCK, width)
    return pl.pallas_call(
        body, name="grad_chip_sum",
        grid_spec=pltpu.PrefetchScalarGridSpec(
            num_scalar_prefetch=1, grid=(4, nblk),
            in_specs=[pl.BlockSpec(blk, lambda k, i, c_ref: (k, c_ref[0] * nblk + i, 0)),
                      pl.BlockSpec(blk, lambda k, i, c_ref: (k, i, 0))],
            out_specs=pl.BlockSpec(blk, lambda k, i, c_ref: (k, i, 0))),
        out_shape=jax.ShapeDtypeStruct(other.shape, F32),
        compiler_params=_cparams(("parallel", "parallel")),
    )(jnp.reshape(c, (1,)).astype(jnp.int32), grads, other)


def _send_chip_sums(sums):
    hr, width = sums.shape[1], sums.shape[2]

    def body(t_ref, b_ref, send_sems, recv_sems):
        x, y, c = _position()
        copies = [pltpu.make_async_remote_copy(src_ref=t_ref.at[2 * chip[0] + chip[1]], dst_ref=b_ref.at[j],
                                               send_sem=send_sems.at[j], recv_sem=recv_sems.at[j],
                                               device_id=(*chip, c), device_id_type=MESH)
                  for j, chip in enumerate(_other_chips(x, y))]
        for cp in copies:
            cp.start()
        for cp in copies:
            cp.wait()

    return pl.pallas_call(
        body, name="grad_send_chips", in_specs=[HBM_SPEC], out_specs=HBM_SPEC,
        out_shape=jax.ShapeDtypeStruct((3, hr, width), sums.dtype),
        scratch_shapes=[pltpu.SemaphoreType.DMA((3,)), pltpu.SemaphoreType.DMA((3,))],
    )(sums)


def _final_sum(sums, recv, k):
    hr, width = sums.shape[1], sums.shape[2]

    def body(k_ref, t_ref, b_ref, o_ref):
        o_ref[...] = ((t_ref[0] + b_ref[0]) + b_ref[1]) + b_ref[2]

    return pl.pallas_call(
        body, name="grad_final_sum",
        grid_spec=pltpu.PrefetchScalarGridSpec(
            num_scalar_prefetch=1, grid=(hr // RS_BLOCK,),
            in_specs=[pl.BlockSpec((1, RS_BLOCK, width), lambda i, k_ref: (k_ref[0], i, 0)),
                      pl.BlockSpec((3, RS_BLOCK, width), lambda i, k_ref: (0, i, 0))],
            out_specs=pl.BlockSpec((RS_BLOCK, width), lambda i, k_ref: (i, 0))),
        out_shape=jax.ShapeDtypeStruct((hr, width), F32),
        compiler_params=_cparams(("parallel",)),
    )(jnp.reshape(k, (1,)).astype(jnp.int32), sums, recv)


def _join_halves(half):
    hr, width = half.shape

    def body(h_ref, o_ref, send_sem, recv_sem, local_sem):
        x, y, c = _position()
        mine = pltpu.make_async_copy(h_ref, o_ref.at[pl.ds(c * hr, hr), :], local_sem)
        mine.start()
        cp = pltpu.make_async_remote_copy(src_ref=h_ref, dst_ref=o_ref.at[pl.ds(c * hr, hr), :], send_sem=send_sem,
                                          recv_sem=recv_sem, device_id=(x, y, 1 - c), device_id_type=MESH)
        cp.start()
        pltpu.make_async_remote_copy(src_ref=h_ref, dst_ref=o_ref.at[pl.ds((1 - c) * hr, hr), :], send_sem=send_sem,
                                     recv_sem=recv_sem, device_id=(x, y, 1 - c), device_id_type=MESH).wait_recv()
        cp.wait_send()
        mine.wait()

    return pl.pallas_call(
        body, name="grad_join_cores", in_specs=[HBM_SPEC], out_specs=HBM_SPEC,
        out_shape=jax.ShapeDtypeStruct((2 * hr, width), half.dtype),
        scratch_shapes=[pltpu.SemaphoreType.DMA, pltpu.SemaphoreType.DMA, pltpu.SemaphoreType.DMA],
    )(half)


def _all_reduce_packet(packet):
    rows = packet.shape[0]

    def body(p_ref, o_ref, buf, send_sems, recv_sems):
        x, y, c = _position()
        me = 4 * x + 2 * y + c
        buf[me] = p_ref[...]

        def flip(v, bit):
            return 1 - v if bit else v

        for p in range(1, 8):
            peer = (flip(x, p & 4), flip(y, p & 2), flip(c, p & 1))
            pltpu.make_async_remote_copy(src_ref=p_ref, dst_ref=buf.at[me], send_sem=send_sems.at[p - 1],
                                         recv_sem=recv_sems.at[p - 1], device_id=peer, device_id_type=MESH).start()
        for p in range(1, 8):
            peer = (flip(x, p & 4), flip(y, p & 2), flip(c, p & 1))
            slot = 4 * peer[0] + 2 * peer[1] + peer[2]
            cp = pltpu.make_async_remote_copy(src_ref=p_ref, dst_ref=buf.at[slot], send_sem=send_sems.at[p - 1],
                                              recv_sem=recv_sems.at[p - 1], device_id=peer, device_id_type=MESH)
            cp.wait_recv()
            cp.wait_send()
        acc = buf[0]
        for dev in range(1, 8):
            acc = acc + buf[dev]
        o_ref[...] = acc

    return pl.pallas_call(
        body, name="all_reduce_packet", in_specs=[VMEM_SPEC], out_specs=VMEM_SPEC,
        out_shape=jax.ShapeDtypeStruct(packet.shape, F32),
        scratch_shapes=[pltpu.VMEM((8, rows, LANES), F32), pltpu.SemaphoreType.DMA((7,)),
                        pltpu.SemaphoreType.DMA((7,))],
    )(packet)


def _pack_blocks(blocks, dtype):
    return jnp.concatenate([blocks[n].astype(dtype).reshape(-1, PACK_W) for n, _, _ in BIG], axis=0)


def _unpack_blocks(packed):
    out, off = {}, 0
    for n, shape, _ in BIG:
        r = math.prod(shape) // PACK_W
        out[n] = packed[off:off + r].reshape(shape)
        off += r
    return out


def _unpack_gathered(gathered):
    per_chip = [_unpack_blocks(gathered[k]) for k in range(4)]
    return {n: jnp.concatenate([per_chip[k][n] for k in range(4)], axis=ax) for n, _, ax in BIG}


def _pack_full(full, dtype):
    chips = []
    for k in range(4):
        blocks = {}
        for n, shape, ax in BIG:
            blocks[n] = lax.slice_in_dim(full[n], k * shape[ax], (k + 1) * shape[ax], axis=ax)
        chips.append(_pack_blocks(blocks, dtype))
    return jnp.stack(chips, axis=0)


def _pack_small(vals, loss_row):
    rows = [loss_row.reshape(1, LANES)]
    for n, shape in SMALL:
        v = vals.get(n)
        v = jnp.zeros(shape, F32) if v is None else v
        rows.append(v.astype(F32).reshape(-1, LANES))
    packet = jnp.concatenate(rows, axis=0)
    return jnp.pad(packet, ((0, PACKET_ROWS - packet.shape[0]), (0, 0)))


def _unpack_small(packet):
    out, off = {}, 1
    for n, shape in SMALL:
        r = math.prod(shape) // LANES
        out[n] = packet[off:off + r].reshape(shape)
        off += r
    return packet[0, 0], out


_MLA = dict(R=1, dk=MLA_DK, dv=MLA_V, hb=2, bq=512, bk=512)
_GQA = dict(R=GQA_HEADS // GQA_KV_HEADS, dk=GQA_HEAD_DIM, dv=GQA_HEAD_DIM, hb=2, bq=256, bk=512)


def _layer_weights(full, gains):
    layers = []
    for layer in range(4):
        i = layer // 2
        p = dict(ffn_norm=gains['ffn_norm'][layer][None], ffn_w_in=full['ffn_w_in'][layer],
                 ffn_w_out=full['ffn_w_out'][layer])
        if layer % 2 == 0:
            w_in = full['w_in_ab'][i]
            zeros = jnp.zeros((D_MODEL, 32), w_in.dtype)
            p['w_a'] = jnp.concatenate([w_in[:, :640], zeros, zeros, w_in[:, 640:IN_A], zeros], axis=1)
            p['w_b'] = w_in[:, IN_A:]
            p['w_uq'] = jnp.pad(full['mla_w_uq'][i], ((0, 0), (0, 0), (0, MLA_DK - 96))).reshape(MLA_Q_RANK, -1)
            ukv = full['mla_w_ukv'][i]
            p['w_uk'] = jnp.pad(ukv[:, :, :MLA_NOPE], ((0, 0), (0, 0), (0, MLA_DK - MLA_NOPE))).reshape(MLA_KV_RANK, -1)
            p['w_uv'] = ukv[:, :, MLA_NOPE:].reshape(MLA_KV_RANK, -1)
            p['w_out'] = full['w_out_ab'][i]
            p['mix_norm'] = gains['mix_norm_ab'][i][None]
            p['q_norm'] = gains['mla_q_norm'][i][None]
            p['kv_norm'] = gains['mla_kv_norm'][i][None]
        else:
            p['w_q'], p['w_kv'], p['w_o'] = full['gqa_w_q'][i], full['gqa_w_kv'][i], full['gqa_w_o'][i]
            p['mix_norm'] = gains['mix_norm_c'][i][None]
            p['q_norm'] = jnp.tile(gains['gqa_q_norm'][i][None], (1, 2))
            p['k_norm'] = jnp.tile(gains['gqa_k_norm'][i][None], (1, 2))
        layers.append(p)
    return layers


def _even_fwd(x, p, cs, swap, tag):
    xn = _rms_fwd(x, p['mix_norm'], tag + "_norm")
    za = _mm(xn, p['w_a'], name=tag + "_in_a")
    zb = _mm(xn, p['w_b'], out_dtype=BF16, name=tag + "_in_b")
    cq, ckv, krr = _mla_prep(za, p['q_norm'], p['kv_norm'], cs, swap, tag + "_mla_prep")
    q_raw = _mm(cq, p['w_uq'], name=tag + "_uq")
    k_pad = _mm(ckv, p['w_uk'], name=tag + "_uk")
    v = _mm(ckv, p['w_uv'], out_dtype=BF16, name=tag + "_uv")
    qh, kh = _mla_qk(q_raw, k_pad, krr, cs, swap, tag + "_mla_qk")
    o_a, lse_a = _flash_fwd(qh, kh, v, name=tag + "_mla_attn", **_MLA)
    og, lg = [], []
    for grp in range(DIL_GROUPS):
        o, l = _dil_fwd(zb, grp, f"{tag}_dil{grp}")
        og.append(o)
        lg.append(l)
    o_b, lt = _dil_combine(og, lg, tag + "_dil_merge")
    ocat = jnp.concatenate([o_a, o_b], axis=1)
    x1 = _mm(ocat, p['w_out'], add=x, name=tag + "_out")
    saved = dict(x=x, xn=xn, za=za, zb=zb, cq=cq, ckv=ckv, qh=qh, kh=kh, v=v, lse_a=lse_a, og=og, lg=lg, lt=lt,
                 ocat=ocat)
    return x1, saved


def _even_bwd(dx1, p, sv, cs, swap, seg64, tag):
    docat = _mm(dx1, p['w_out'], mode="nt", name=tag + "_out_dx")
    d_w_out = _mm(sv['ocat'], dx1, mode="tn", name=tag + "_out_dw")
    n_a = MLA_HEADS * MLA_V
    do_a = docat[:, :n_a].astype(BF16)
    res = _dil_combine_bwd(docat[:, n_a:], sv['og'], sv['lg'], sv['lt'], seg64, tag + "_dil_merge_bwd")
    dqs, dks, dvs = [], [], []
    for grp in range(DIL_GROUPS):
        dq, dk, dv = _dil_bwd(sv['zb'], res[grp], sv['lg'][grp], res[3 + grp], grp, f"{tag}_dil{grp}_bwd")
        dqs.append(dq)
        dks.append(dk)
        dvs.append(dv)
    dzb = jnp.concatenate(dqs + dks + dvs, axis=1)
    dqh, dkh, dv = _flash_bwd(sv['qh'], sv['kh'], sv['v'], sv['ocat'][:, :n_a], do_a, sv['lse_a'],
                              name=tag + "_mla_attn_bwd", **_MLA)
    dq_raw, dkh, dkrr = _mla_qk_bwd(dqh, dkh, cs, swap, tag + "_mla_qk_bwd")
    dcq = _mm(dq_raw, p['w_uq'], mode="nt", name=tag + "_uq_dx")
    d_w_uq = _mm(sv['cq'], dq_raw, mode="tn", name=tag + "_uq_dw")
    dckv = _mm(dkh, p['w_uk'], mode="nt", name=tag + "_uk_dx")
    dckv = _mm(dv, p['w_uv'], mode="nt", add=dckv, name=tag + "_uv_dx")
    d_w_uk = _mm(sv['ckv'], dkh, mode="tn", name=tag + "_uk_dw")
    d_w_uv = _mm(sv['ckv'], dv, mode="tn", name=tag + "_uv_dw")
    dza, d_gq, d_gkv = _mla_prep_bwd(sv['za'], cs, dcq, dckv, dkrr, p['q_norm'], p['kv_norm'], swap,
                                     tag + "_mla_prep_bwd")
    dxn = _mm(dza, p['w_a'], mode="nt", name=tag + "_in_a_dx")
    dxn = _mm(dzb, p['w_b'], mode="nt", add=dxn, name=tag + "_in_b_dx")
    d_w_a = _mm(sv['xn'], dza, mode="tn", name=tag + "_in_a_dw")
    d_w_b = _mm(sv['xn'], dzb, mode="tn", name=tag + "_in_b_dw")
    dx, d_g = _rms_bwd(sv['x'], p['mix_norm'], dxn, dx1, tag + "_norm_bwd")
    d_w_in = jnp.concatenate([d_w_a[:, :640], d_w_a[:, 704:736], d_w_b], axis=1)
    d_uq = d_w_uq.reshape(MLA_Q_RANK, MLA_HEADS, MLA_DK)[:, :, :MLA_NOPE + MLA_ROPE]
    d_ukv = jnp.concatenate([d_w_uk.reshape(MLA_KV_RANK, MLA_HEADS, MLA_DK)[:, :, :MLA_NOPE],
                             d_w_uv.reshape(MLA_KV_RANK, MLA_HEADS, MLA_V)], axis=2)
    grads = dict(w_in_ab=d_w_in, mla_w_uq=d_uq, mla_w_ukv=d_ukv, w_out_ab=d_w_out, mix_norm_ab=d_g[0],
                 mla_q_norm=d_gq[0], mla_kv_norm=d_gkv[0])
    return dx, grads


def _odd_fwd(x, p, cs, seg64, swap, tag):
    xn = _rms_fwd(x, p['mix_norm'], tag + "_norm")
    q_raw = _mm(xn, p['w_q'], name=tag + "_q")
    kv_raw = _mm(xn, p['w_kv'], name=tag + "_kv")
    qh, kh, v = _gqa_prep(q_raw, kv_raw, cs, p['q_norm'], p['k_norm'], seg64, swap, tag + "_gqa_prep")
    o, lse = _flash_fwd(qh, kh, v, name=tag + "_gqa_attn", **_GQA)
    x1 = _mm(o, p['w_o'], add=x, name=tag + "_o")
    return x1, dict(x=x, xn=xn, q_raw=q_raw, kv_raw=kv_raw, qh=qh, kh=kh, v=v, o=o, lse=lse)


def _odd_bwd(dx1, p, sv, cs, seg64, swap, tag):
    do = _mm(dx1, p['w_o'], mode="nt", out_dtype=BF16, name=tag + "_o_dx")
    d_w_o = _mm(sv['o'], dx1, mode="tn", name=tag + "_o_dw")
    dqh, dkh, dv = _flash_bwd(sv['qh'], sv['kh'], sv['v'], sv['o'], do, sv['lse'], name=tag + "_gqa_attn_bwd",
                              **_GQA)
    dq_raw, dkv_raw, d_gq, d_gk = _gqa_prep_bwd(sv['q_raw'], sv['kv_raw'], cs, dqh, dkh, dv, p['q_norm'],
                                                p['k_norm'], seg64, swap, tag + "_gqa_prep_bwd")
    dxn = _mm(dq_raw, p['w_q'], mode="nt", name=tag + "_q_dx")
    dxn = _mm(dkv_raw, p['w_kv'], mode="nt", add=dxn, name=tag + "_kv_dx")
    d_w_q = _mm(sv['xn'], dq_raw, mode="tn", name=tag + "_q_dw")
    d_w_kv = _mm(sv['xn'], dkv_raw, mode="tn", name=tag + "_kv_dw")
    dx, d_g = _rms_bwd(sv['x'], p['mix_norm'], dxn, dx1, tag + "_norm_bwd")
    grads = dict(gqa_w_q=d_w_q, gqa_w_kv=d_w_kv, gqa_w_o=d_w_o, mix_norm_c=d_g[0],
                 gqa_q_norm=d_gq[0, :GQA_HEAD_DIM] + d_gq[0, GQA_HEAD_DIM:],
                 gqa_k_norm=d_gk[0, :GQA_HEAD_DIM] + d_gk[0, GQA_HEAD_DIM:])
    return dx, grads


def _ffn_fwd(x, p, tag):
    xn = _rms_fwd(x, p['ffn_norm'], tag + "_ffn_norm")
    h = _mm(xn, p['ffn_w_in'], out_dtype=BF16, name=tag + "_ffn_in")
    a = _swiglu(h, tag + "_swiglu")
    x2 = _mm(a, p['ffn_w_out'], add=x, name=tag + "_ffn_out")
    return x2, dict(x=x, xn=xn, h=h, a=a)


def _ffn_bwd(dx2, p, sv, tag):
    da = _mm(dx2, p['ffn_w_out'], mode="nt", out_dtype=BF16, name=tag + "_ffn_out_dx")
    d_w_out = _mm(sv['a'], dx2, mode="tn", name=tag + "_ffn_out_dw")
    dh = _swiglu_bwd(sv['h'], da, tag + "_swiglu_bwd")
    d_w_in = _mm(sv['xn'], dh, mode="tn", name=tag + "_ffn_in_dw")
    dxn = _mm(dh, p['ffn_w_in'], mode="nt", name=tag + "_ffn_in_dx")
    dx, d_g = _rms_bwd(sv['x'], p['ffn_norm'], dxn, dx2, tag + "_ffn_norm_bwd")
    return dx, d_w_in, d_w_out, d_g[0]


def _local_step(x, target, full, gains):
    s = x.shape[0]
    cs_mla, cs_gqa = _rope_tables(s)
    swap, seg64 = _swap_matrix(), _seg_matrix(GQA_HEAD_DIM)
    layers = _layer_weights(full, gains)
    saved = []
    for layer, p in enumerate(layers):
        tag = f"l{layer}"
        if layer % 2 == 0:
            x, sv = _even_fwd(x, p, cs_mla, swap, tag)
        else:
            x, sv = _odd_fwd(x, p, cs_gqa, seg64, swap, tag)
        x, sv_f = _ffn_fwd(x, p, tag)
        saved.append((sv, sv_f))
    dx, loss_row, d_final = _loss_head(x, target, gains['final_norm'][None], "loss_head")

    per_layer = []
    for layer in reversed(range(4)):
        p, (sv, sv_f), tag = layers[layer], saved[layer], f"l{layer}"
        dx, d_ffn_in, d_ffn_out, d_ffn_g = _ffn_bwd(dx, p, sv_f, tag)
        if layer % 2 == 0:
            dx, g = _even_bwd(dx, p, sv, cs_mla, swap, seg64, tag)
        else:
            dx, g = _odd_bwd(dx, p, sv, cs_gqa, seg64, swap, tag)
        g.update(ffn_w_in=d_ffn_in, ffn_w_out=d_ffn_out, ffn_norm=d_ffn_g)
        per_layer.append(g)
    per_layer = per_layer[::-1]

    grads = {'final_norm': d_final[0]}
    for n in ('ffn_norm', 'ffn_w_in', 'ffn_w_out'):
        grads[n] = jnp.stack([per_layer[layer][n] for layer in range(4)], axis=0)
    for n in ('mix_norm_ab', 'w_in_ab', 'mla_q_norm', 'mla_kv_norm', 'mla_w_uq', 'mla_w_ukv', 'w_out_ab'):
        grads[n] = jnp.stack([per_layer[layer][n] for layer in (0, 2)], axis=0)
    for n in ('mix_norm_c', 'gqa_w_q', 'gqa_w_kv', 'gqa_q_norm', 'gqa_k_norm', 'gqa_w_o'):
        grads[n] = jnp.stack([per_layer[layer][n] for layer in (1, 3)], axis=0)
    return loss_row, dx, grads


def _step(x, target, w, m, v):
    big_names = [n for n, _, _ in BIG]
    chip = 2 * lax.axis_index("x") + lax.axis_index("y")
    core = lax.axis_index("c")

    gathered = _all_gather_weights(_pack_blocks(w, BF16))
    full = _unpack_gathered(gathered)
    gains = {n: w[n] for n, _ in SMALL if n != 'mix_norm_c'}
    c_cols = w['mix_norm_c'].shape[1]
    own_c = lax.dynamic_update_slice(jnp.zeros((2, 4 * c_cols), F32), w['mix_norm_c'], (0, chip * c_cols))
    gains['mix_norm_c'] = _unpack_small(_all_reduce_packet(_pack_small(
        {'mix_norm_c': own_c * 0.5}, jnp.zeros((LANES,), F32))))[1]['mix_norm_c']

    loss_row, dx, grads = _local_step(x[0], target[0], full, gains)

    packed = _pack_full(grads, F32)
    other = _sibling_swap_halves(packed)
    sums = _chip_sum(packed, other, core)
    recv = _send_chip_sums(sums)
    reduced = _join_halves(_final_sum(sums, recv, chip))
    g_blocks = _unpack_blocks(reduced)

    loss, g_small = _unpack_small(_all_reduce_packet(_pack_small({n: grads[n] for n, _ in SMALL}, loss_row[0])))
    g_small['mix_norm_c'] = lax.dynamic_slice(g_small['mix_norm_c'], (0, chip * c_cols), (2, c_cols))

    out_g, out_d, out_m, out_v = {}, {}, {}, {}
    for n in big_names:
        shape = w[n].shape
        cols = shape[-1]
        d_, m_, v_ = _adamw(w[n].reshape(-1, cols), g_blocks[n].reshape(-1, cols), m[n].reshape(-1, cols),
                            v[n].reshape(-1, cols), "adamw_" + n)
        out_g[n], out_d[n], out_m[n], out_v[n] = g_blocks[n], d_.reshape(shape), m_.reshape(shape), v_.reshape(shape)
    for n, _ in SMALL:
        shape = w[n].shape
        as2d = (lambda t: t.reshape(1, -1)) if len(shape) == 1 else (lambda t: t)
        d_, m_, v_ = _adamw(as2d(w[n]), as2d(g_small[n]), as2d(m[n]), as2d(v[n]), "adamw_" + n)
        out_g[n], out_d[n], out_m[n], out_v[n] = g_small[n], d_.reshape(shape), m_.reshape(shape), v_.reshape(shape)
    return (loss, dx[None], *[out_g[n] for n in WEIGHTS], *[out_d[n] for n in WEIGHTS],
            *[out_m[n] for n in WEIGHTS], *[out_v[n] for n in WEIGHTS])


def kernel(x, mix_norm_ab, w_in_ab, mla_q_norm, mla_kv_norm, mla_w_uq, mla_w_ukv, w_out_ab, mix_norm_c, gqa_w_q, gqa_w_kv, gqa_q_norm, gqa_k_norm, gqa_w_o, ffn_norm, ffn_w_in, ffn_w_out, final_norm, loss_target, m_mix_norm_ab, m_w_in_ab, m_mla_q_norm, m_mla_kv_norm, m_mla_w_uq, m_mla_w_ukv, m_w_out_ab, m_mix_norm_c, m_gqa_w_q, m_gqa_w_kv, m_gqa_q_norm, m_gqa_k_norm, m_gqa_w_o, m_ffn_norm, m_ffn_w_in, m_ffn_w_out, m_final_norm, v_mix_norm_ab, v_w_in_ab, v_mla_q_norm, v_mla_kv_norm, v_mla_w_uq, v_mla_w_ukv, v_w_out_ab, v_mix_norm_c, v_gqa_w_q, v_gqa_w_kv, v_gqa_q_norm, v_gqa_k_norm, v_gqa_w_o, v_ffn_norm, v_ffn_w_in, v_ffn_w_out, v_final_norm):
    w = dict(zip(WEIGHTS, (mix_norm_ab, w_in_ab, mla_q_norm, mla_kv_norm, mla_w_uq, mla_w_ukv, w_out_ab, mix_norm_c,
                           gqa_w_q, gqa_w_kv, gqa_q_norm, gqa_k_norm, gqa_w_o, ffn_norm, ffn_w_in, ffn_w_out,
                           final_norm)))
    m = dict(zip(WEIGHTS, (m_mix_norm_ab, m_w_in_ab, m_mla_q_norm, m_mla_kv_norm, m_mla_w_uq, m_mla_w_ukv,
                           m_w_out_ab, m_mix_norm_c, m_gqa_w_q, m_gqa_w_kv, m_gqa_q_norm, m_gqa_k_norm, m_gqa_w_o,
                           m_ffn_norm, m_ffn_w_in, m_ffn_w_out, m_final_norm)))
    v = dict(zip(WEIGHTS, (v_mix_norm_ab, v_w_in_ab, v_mla_q_norm, v_mla_kv_norm, v_mla_w_uq, v_mla_w_ukv,
                           v_w_out_ab, v_mix_norm_c, v_gqa_w_q, v_gqa_w_kv, v_gqa_q_norm, v_gqa_k_norm, v_gqa_w_o,
                           v_ffn_norm, v_ffn_w_in, v_ffn_w_out, v_final_norm)))
    return _step(x, loss_target, w, m, v)
```

```python
import math

import numpy as np
import jax
import jax.numpy as jnp
from jax import lax
from jax.experimental import pallas as pl
from jax.experimental.pallas import tpu as pltpu

F32 = jnp.float32
BF16 = jnp.bfloat16
MESH = pl.DeviceIdType.MESH

VMEM_LIMIT_BYTES = 56 * 1024 * 1024
LANES = 128

D_MODEL = 1024
NORM_EPS = 1e-6
ROPE_THETA = 10000.0
NEG_INF = -1e30
GRID_W = 64

MLA_HEADS, MLA_Q_RANK, MLA_KV_RANK, MLA_NOPE, MLA_ROPE, MLA_V = 8, 384, 256, 64, 32, 64
MLA_DK = 128
DIL_PAIRS = ((128, 1), (512, 4), (2048, 16))
DIL_HALF, DIL_SLOTS, DIL_GROUPS, DIL_HEAD_DIM = 64, 4, 3, 64
DIL_HEADS = DIL_SLOTS * DIL_GROUPS
DIL_W = DIL_SLOTS * DIL_HEAD_DIM
GQA_HEADS, GQA_KV_HEADS, GQA_HEAD_DIM = 16, 4, 64
FFN_HIDDEN = 2816
IN_A = MLA_Q_RANK + MLA_KV_RANK + MLA_ROPE
IN_A_PAD = 768
IN_B = 3 * DIL_HEADS * DIL_HEAD_DIM

ADAM_LR, ADAM_B1, ADAM_B2, ADAM_EPS, ADAM_WD, ADAM_STEP = 0.001, 0.9, 0.999, 1e-08, 0.01, 10

LOG2E, LN2 = math.log2(math.e), math.log(2.0)
MLA_SCALE = (MLA_NOPE + MLA_ROPE) ** -0.5
GQA_SCALE = GQA_HEAD_DIM ** -0.5

WEIGHTS = ['mix_norm_ab', 'w_in_ab', 'mla_q_norm', 'mla_kv_norm', 'mla_w_uq', 'mla_w_ukv', 'w_out_ab', 'mix_norm_c',
           'gqa_w_q', 'gqa_w_kv', 'gqa_q_norm', 'gqa_k_norm', 'gqa_w_o', 'ffn_norm', 'ffn_w_in', 'ffn_w_out',
           'final_norm']
BIG = (('w_in_ab', (2, 1024, 744), 2), ('mla_w_uq', (2, 96, 8, 96), 1), ('mla_w_ukv', (2, 64, 8, 128), 1),
       ('w_out_ab', (2, 768, 256), 2), ('gqa_w_q', (2, 256, 1024), 1), ('gqa_w_kv', (2, 256, 512), 1),
       ('gqa_w_o', (2, 256, 1024), 1), ('ffn_w_in', (4, 1024, 1408), 2), ('ffn_w_out', (4, 704, 1024), 1))
PACK_W = 1024
SMALL = (('mix_norm_ab', (2, 1024)), ('mla_q_norm', (2, 384)), ('mla_kv_norm', (2, 256)), ('gqa_q_norm', (2, 64)),
         ('gqa_k_norm', (2, 64)), ('ffn_norm', (4, 1024)), ('final_norm', (1024,)), ('mix_norm_c', (2, 1024)))
PACKET_ROWS = 88


def _cparams(sem=None):
    return pltpu.CompilerParams(dimension_semantics=sem, vmem_limit_bytes=VMEM_LIMIT_BYTES)


def _pick(n, pref, mult=LANES):
    if n <= pref:
        return n
    for d in range(pref - pref % mult, 0, -mult):
        if n % d == 0:
            return d
    return n


_DIMS = {"nn": (((1,), (0,)), ((), ())), "nt": (((1,), (1,)), ((), ())), "tn": (((0,), (0,)), ((), ()))}


def _mm(a, b, *, mode="nn", add=None, out_dtype=F32, name="mm"):
    if mode == "nn":
        (m, k), (k2, n) = a.shape, b.shape
    elif mode == "nt":
        (m, k), (n, k2) = a.shape, b.shape
    else:
        (k, m), (k2, n) = a.shape, b.shape
    assert k == k2, (a.shape, b.shape, mode)
    if mode == "tn":
        bm, bn, bk = _pick(m, 1408), _pick(n, 1024), _pick(k, 1024, 16)
    else:
        bm, bn, bk = _pick(m, 512, 16), _pick(n, 1408), _pick(k, 1408)
    nk = k // bk
    assert m % bm == 0 and n % bn == 0 and k % bk == 0
    has_add = add is not None

    def body(*refs):
        a_ref, b_ref = refs[0], refs[1]
        add_ref = refs[2] if has_add else None
        o_ref = refs[3 if has_add else 2]
        part = lax.dot_general(a_ref[...].astype(BF16), b_ref[...].astype(BF16), _DIMS[mode],
                               preferred_element_type=F32)

        def finish(r):
            if has_add:
                r = r + add_ref[...]
            o_ref[...] = r.astype(o_ref.dtype)

        if nk == 1:
            finish(part)
        else:
            acc_ref = refs[-1]
            kk = pl.program_id(2)

            @pl.when(kk == 0)
            def _():
                acc_ref[...] = part

            @pl.when(kk > 0)
            def _():
                acc_ref[...] += part

            @pl.when(kk == nk - 1)
            def _():
                finish(acc_ref[...])

    if mode == "nn":
        a_spec = pl.BlockSpec((bm, bk), lambda i, j, kk: (i, kk))
        b_spec = pl.BlockSpec((bk, bn), lambda i, j, kk: (kk, j))
    elif mode == "nt":
        a_spec = pl.BlockSpec((bm, bk), lambda i, j, kk: (i, kk))
        b_spec = pl.BlockSpec((bn, bk), lambda i, j, kk: (j, kk))
    else:
        a_spec = pl.BlockSpec((bk, bm), lambda i, j, kk: (kk, i))
        b_spec = pl.BlockSpec((bk, bn), lambda i, j, kk: (kk, j))
    o_spec = pl.BlockSpec((bm, bn), lambda i, j, kk: (i, j))
    in_specs = [a_spec, b_spec] + ([o_spec] if has_add else [])
    args = (a, b) + ((add,) if has_add else ())
    return pl.pallas_call(
        body, name=name, grid=(m // bm, n // bn, nk), in_specs=in_specs, out_specs=o_spec,
        out_shape=jax.ShapeDtypeStruct((m, n), out_dtype),
        scratch_shapes=[pltpu.VMEM((bm, bn), F32)] if nk > 1 else [],
        compiler_params=_cparams(("parallel", "parallel", "arbitrary")),
    )(*args)


def _rows_call(fn, rows, consts, out_rows, out_accs=(), *, bs=256, name):
    s = rows[0].shape[0]
    bs = min(bs, s)
    assert s % bs == 0
    nr, nc, no, na = len(rows), len(consts), len(out_rows), len(out_accs)

    def body(*refs):
        vals = [r[...] for r in refs[:nr + nc]]
        outs = refs[nr + nc:]
        res = fn(*vals)
        if not isinstance(res, (tuple, list)):
            res = (res,)
        assert len(res) == no + na, (len(res), no, na)
        for r, v in zip(outs[:no], res[:no]):
            r[...] = v.astype(r.dtype)
        if na:
            i = pl.program_id(0)
            for r, v in zip(outs[no:], res[no:]):
                @pl.when(i == 0)
                def _(r=r, v=v):
                    r[...] = v

                @pl.when(i > 0)
                def _(r=r, v=v):
                    r[...] += v

    in_specs = [pl.BlockSpec((bs, a.shape[1]), lambda i: (i, 0)) for a in rows]
    in_specs += [pl.BlockSpec(c.shape, lambda i: (0, 0)) for c in consts]
    out_specs = [pl.BlockSpec((bs, c), lambda i: (i, 0)) for c, _ in out_rows]
    out_specs += [pl.BlockSpec(tuple(sh), lambda i: (0, 0)) for sh in out_accs]
    out_shape = [jax.ShapeDtypeStruct((s, c), dt) for c, dt in out_rows]
    out_shape += [jax.ShapeDtypeStruct(tuple(sh), F32) for sh in out_accs]
    res = pl.pallas_call(
        body, name=name, grid=(s // bs,), in_specs=in_specs, out_specs=out_specs, out_shape=out_shape,
        compiler_params=_cparams(("arbitrary",) if na else ("parallel",)),
    )(*rows, *consts)
    return res


def _rms(x, g):
    return x * lax.rsqrt(jnp.mean(x * x, axis=-1, keepdims=True) + NORM_EPS) * g


def _rms_bwd_math(x, g, dy):
    r = lax.rsqrt(jnp.mean(x * x, axis=-1, keepdims=True) + NORM_EPS)
    u = dy * g
    dx = r * u - x * (r * r * r) * jnp.mean(u * x, axis=-1, keepdims=True)
    dg = jnp.sum(dy * x * r, axis=0, keepdims=True)
    return dx, dg


def _rms_fwd(x, g, name):
    return _rows_call(lambda xv, gv: _rms(xv, gv), [x], [g], [(x.shape[1], BF16)], name=name)[0]


def _rms_bwd(x, g, dy, dres, name):
    def fn(xv, dyv, dresv, gv):
        dx, dg = _rms_bwd_math(xv, gv, dyv.astype(F32))
        return dx + dresv, dg
    return _rows_call(fn, [x, dy, dres], [g], [(x.shape[1], F32)], [(1, x.shape[1])], name=name)


def _chunkdot(x, m):
    outs = [jnp.dot(x[:, c:c + LANES], m, precision=lax.Precision.HIGHEST, preferred_element_type=F32)
            for c in range(0, x.shape[1], LANES)]
    return outs[0] if len(outs) == 1 else jnp.concatenate(outs, axis=1)


def _lanes(t, width):
    n = width // LANES
    return t if n == 1 else jnp.concatenate([t] * n, axis=1)


def _rope(x, cs, swap):
    w = x.shape[1]
    return x * _lanes(cs[:, :LANES], w) + _chunkdot(x, swap) * _lanes(cs[:, LANES:], w)


def _rope_t(dy, cs, swap):
    w = dy.shape[1]
    return dy * _lanes(cs[:, :LANES], w) + _chunkdot(dy * _lanes(cs[:, LANES:], w), swap)


def _swap_matrix():
    m = np.zeros((LANES, LANES), np.float32)
    for j in range(LANES):
        src = j + 16 if (j % 32) < 16 else j - 16
        m[src, j] = 1.0
    return jnp.asarray(m)


def _seg_matrix(seg):
    idx = np.arange(LANES) // seg
    return jnp.asarray((idx[:, None] == idx[None, :]).astype(np.float32))


def _rope_tables(s):
    pos = jnp.arange(s)

    def angles(p, dim):
        freqs = ROPE_THETA ** (-jnp.arange(0, dim, 2, dtype=F32) / dim)
        ang = p.astype(F32)[:, None] * freqs[None, :]
        return jnp.cos(ang), jnp.sin(ang)

    cos_t, sin_t = angles(pos, MLA_ROPE)
    one, zero = jnp.ones((s, 64), F32), jnp.zeros((s, 64), F32)
    mla = jnp.concatenate([one, cos_t, cos_t, one[:, :32], zero, -sin_t, sin_t, zero[:, :32]], axis=1)
    cos_r, sin_r = angles(pos // GRID_W, GQA_HEAD_DIM // 2)
    cos_c, sin_c = angles(pos % GRID_W, GQA_HEAD_DIM // 2)
    c64 = jnp.concatenate([cos_r, cos_r, cos_c, cos_c], axis=1)
    s64 = jnp.concatenate([-sin_r, sin_r, -sin_c, sin_c], axis=1)
    gqa = jnp.concatenate([c64, c64, s64, s64], axis=1)
    return mla, gqa


def _col_to_row(col):
    return jnp.transpose(jnp.broadcast_to(col, (col.shape[0], LANES)))[0:1, :]


def _stack_heads(ref, heads, d, dtype=None):
    parts = [ref[:, hd * d:(hd + 1) * d] for hd in heads]
    out = parts[0] if len(parts) == 1 else jnp.concatenate(parts, axis=0)
    return out if dtype is None else out.astype(dtype)


def _fill_v_ones(v_ref, va_ref, hb, dv):
    @pl.when(pl.program_id(1) == 0)
    def _():
        ones = jnp.ones((v_ref.shape[0], dv), BF16)
        for h in range(hb):
            va_ref[:, 2 * h * dv:(2 * h + 1) * dv] = v_ref[:, h * dv:(h + 1) * dv]
            va_ref[:, (2 * h + 1) * dv:(2 * h + 2) * dv] = ones


def _flash_fwd(q, k, v, *, R, dk, dv, hb, bq, bk, name):
    s = q.shape[0]
    g = k.shape[1] // dk
    ng = g // hb
    bq, bk = min(bq, s), min(bk, s)
    nq, nkb = s // bq, s // bk
    rb = R * bq

    def body(q_ref, k_ref, v_ref, o_ref, lse_ref, va_ref):
        _fill_v_ones(v_ref, va_ref, hb, dv)
        head_sets = [[h * R + r for r in range(R)] for h in range(hb)]
        qss = [_stack_heads(q_ref, heads, dk) for heads in head_sets]

        def step(jj, carry):
            carry = list(carry)
            for u in range(unroll):
                r0 = pl.multiple_of((jj * unroll + u) * bk, bk)
                for h in range(hb):
                    m, acc = carry[h]
                    kj = k_ref[pl.ds(r0, bk), h * dk:(h + 1) * dk]
                    va = va_ref[pl.ds(r0, bk), 2 * h * dv:2 * (h + 1) * dv]
                    sc = lax.dot_general(qss[h], kj, _DIMS["nt"], preferred_element_type=F32)
                    m2 = jnp.maximum(m, jnp.max(sc, axis=1, keepdims=True))
                    p = jnp.exp2(sc - m2).astype(BF16)
                    carry[h] = (m2, jnp.exp2(m - m2) * acc + jnp.dot(p, va, preferred_element_type=F32))
            return tuple(carry)

        unroll = 2 if nkb % 2 == 0 else 1
        init = tuple((jnp.full((rb, 1), NEG_INF, F32), jnp.zeros((rb, 2 * dv), F32)) for _ in range(hb))
        final = lax.fori_loop(0, nkb // unroll, step, init)
        for h, heads in enumerate(head_sets):
            m, acc = final[h]
            l = acc[:, dv:dv + 1]
            o = acc[:, :dv] / l
            row = _col_to_row(m + jnp.log2(l))
            for r, hd in enumerate(heads):
                o_ref[:, hd * dv:(hd + 1) * dv] = o[r * bq:(r + 1) * bq].astype(o_ref.dtype)
                lse_ref[0, hd:hd + 1, :] = row[:, r * bq:(r + 1) * bq]

    return pl.pallas_call(
        body, name=name, grid=(ng, nq),
        in_specs=[pl.BlockSpec((bq, hb * R * dk), lambda gi, i: (i, gi)),
                  pl.BlockSpec((s, hb * dk), lambda gi, i: (0, gi)),
                  pl.BlockSpec((s, hb * dv), lambda gi, i: (0, gi))],
        out_specs=[pl.BlockSpec((bq, hb * R * dv), lambda gi, i: (i, gi)),
                   pl.BlockSpec((1, hb * R, bq), lambda gi, i: (gi, 0, i))],
        out_shape=[jax.ShapeDtypeStruct((s, g * R * dv), BF16), jax.ShapeDtypeStruct((ng, hb * R, s), F32)],
        scratch_shapes=[pltpu.VMEM((s, 2 * hb * dv), BF16)],
        compiler_params=_cparams(("parallel", "arbitrary")),
    )(q, k, v)


def _flash_bwd(q, k, v, o, do, lse, *, R, dk, dv, hb, bq, bk, name):
    s = q.shape[0]
    g = k.shape[1] // dk
    ng = g // hb
    bq, bk = min(bq, s), min(bk, s)
    nq, nkb = s // bq, s // bk
    rb = R * bq

    def body(q_ref, k_ref, v_ref, o_ref, do_ref, lse_ref, dq_ref, dk_ref, dv_ref, va_ref):
        @pl.when(pl.program_id(1) == 0)
        def _():
            dk_ref[...] = jnp.zeros(dk_ref.shape, F32)
            dv_ref[...] = jnp.zeros(dv_ref.shape, F32)

        _fill_v_ones(v_ref, va_ref, hb, dv)
        lane = lax.broadcasted_iota(jnp.int32, (rb, dv), 1)
        head_sets = [[h * R + r for r in range(R)] for h in range(hb)]
        qss, doss, dosas, lrows = [], [], [], []
        for heads in head_sets:
            dos = _stack_heads(do_ref, heads, dv, BF16)
            delta = jnp.sum(dos.astype(F32) * _stack_heads(o_ref, heads, dv, F32), axis=1, keepdims=True)
            hi = delta.astype(BF16).astype(F32)
            lo = delta - hi
            qss.append(_stack_heads(q_ref, heads, dk))
            doss.append(dos)
            dosas.append(jnp.concatenate(
                [dos, jnp.where(lane == 0, -hi, jnp.where(lane == 1, -lo, 0.0)).astype(BF16)], axis=1))
            rows = [lse_ref[0, hd:hd + 1, :] for hd in heads]
            lrows.append(rows[0] if R == 1 else jnp.concatenate(rows, axis=1))

        def step(j, dqs):
            r0 = pl.multiple_of(j * bk, bk)
            out = []
            for h in range(hb):
                kj = k_ref[pl.ds(r0, bk), h * dk:(h + 1) * dk]
                va = va_ref[pl.ds(r0, bk), 2 * h * dv:2 * (h + 1) * dv]
                st = lax.dot_general(kj, qss[h], _DIMS["nt"], preferred_element_type=F32)
                pt = jnp.exp2(st - lrows[h])
                dv_ref[pl.ds(r0, bk), h * dv:(h + 1) * dv] += jnp.dot(pt.astype(BF16), doss[h],
                                                                      preferred_element_type=F32)
                dst = (pt * lax.dot_general(va, dosas[h], _DIMS["nt"], preferred_element_type=F32)).astype(BF16)
                dk_ref[pl.ds(r0, bk), h * dk:(h + 1) * dk] += jnp.dot(dst, qss[h], preferred_element_type=F32)
                out.append(dqs[h] + lax.dot_general(dst, kj, _DIMS["tn"], preferred_element_type=F32))
            return tuple(out)

        dqs = lax.fori_loop(0, nkb, step, tuple(jnp.zeros((rb, dk), F32) for _ in range(hb)))
        for h, heads in enumerate(head_sets):
            for r, hd in enumerate(heads):
                dq_ref[:, hd * dk:(hd + 1) * dk] = dqs[h][r * bq:(r + 1) * bq]

    qspec = pl.BlockSpec((bq, hb * R * dk), lambda gi, i: (i, gi))
    ospec = pl.BlockSpec((bq, hb * R * dv), lambda gi, i: (i, gi))
    kspec = pl.BlockSpec((s, hb * dk), lambda gi, i: (0, gi))
    vspec = pl.BlockSpec((s, hb * dv), lambda gi, i: (0, gi))
    return pl.pallas_call(
        body, name=name, grid=(ng, nq),
        in_specs=[qspec, kspec, vspec, ospec, ospec, pl.BlockSpec((1, hb * R, bq), lambda gi, i: (gi, 0, i))],
        out_specs=[qspec, kspec, vspec],
        out_shape=[jax.ShapeDtypeStruct((s, g * R * dk), F32), jax.ShapeDtypeStruct((s, g * dk), F32),
                   jax.ShapeDtypeStruct((s, g * dv), F32)],
        scratch_shapes=[pltpu.VMEM((s, 2 * hb * dv), BF16)],
        compiler_params=_cparams(("parallel", "arbitrary")),
    )(q, k, v, o, do, lse)


DIL_BQ = 128
DIL_NCOL = IN_B // DIL_W


def _alibi_slope(head):
    return float(2.0 ** (-8.0 * (head + 1) / DIL_HEADS))


def _band(i, d, length, queries_first):
    if queries_first:
        shape = (3 * DIL_BQ, DIL_BQ)
        wide = (i - 1) * DIL_BQ + lax.broadcasted_iota(jnp.int32, shape, 0)
        narrow = i * DIL_BQ + lax.broadcasted_iota(jnp.int32, shape, 1)
    else:
        shape = (DIL_BQ, 3 * DIL_BQ)
        narrow = i * DIL_BQ + lax.broadcasted_iota(jnp.int32, shape, 0)
        wide = (i - 1) * DIL_BQ + lax.broadcasted_iota(jnp.int32, shape, 1)
    rel = jnp.abs(wide - narrow)
    valid = (rel <= DIL_HALF) & (wide >= 0) & (wide < length)
    return valid, rel.astype(F32) * float(d)


def _nbr_specs(col_of, nb):
    return [pl.BlockSpec((DIL_BQ, DIL_W), lambda r, i: (jnp.maximum(i - 1, 0), col_of(r))),
            pl.BlockSpec((DIL_BQ, DIL_W), lambda r, i: (i, col_of(r))),
            pl.BlockSpec((DIL_BQ, DIL_W), lambda r, i: (jnp.minimum(i + 1, nb - 1), col_of(r)))]


def _cat3(refs, sl, dtype=None):
    out = jnp.concatenate([r[:, sl] for r in refs], axis=0)
    return out if dtype is None else out.astype(dtype)


def _dil_fwd(zb, grp, name):
    s = zb.shape[0]
    d = DIL_PAIRS[grp][1]
    length = s // d
    nb = length // DIL_BQ
    zv = zb.reshape(length, d * IN_B)
    scale = DIL_HEAD_DIM ** -0.5

    def body(q_ref, kp, kc, kn, vp, vc, vn, o_ref, lse_ref):
        valid, dist = _band(pl.program_id(1), d, length, False)
        outs, lses = [], []
        for sl_i in range(DIL_SLOTS):
            sl = slice(sl_i * DIL_HEAD_DIM, (sl_i + 1) * DIL_HEAD_DIM)
            kcat, vcat = _cat3((kp, kc, kn), sl), _cat3((vp, vc, vn), sl)
            sc = lax.dot_general(q_ref[:, sl], kcat, _DIMS["nt"], preferred_element_type=F32) * scale
            sc = jnp.where(valid, sc - _alibi_slope(grp * DIL_SLOTS + sl_i) * dist, NEG_INF)
            m = jnp.max(sc, axis=1, keepdims=True)
            e = jnp.exp(sc - m)
            den = jnp.sum(e, axis=1, keepdims=True)
            outs.append(jnp.dot((e / den).astype(BF16), vcat, preferred_element_type=F32))
            lses.append(jnp.broadcast_to(m + jnp.log(den), (DIL_BQ, DIL_HEAD_DIM)))
        o_ref[...] = jnp.concatenate(outs, axis=1)
        lse_ref[...] = jnp.concatenate(lses, axis=1)

    cur = pl.BlockSpec((DIL_BQ, DIL_W), lambda r, i: (i, r * DIL_NCOL + grp))
    out_spec = pl.BlockSpec((DIL_BQ, DIL_W), lambda r, i: (i, r))
    o, lse = pl.pallas_call(
        body, name=name, grid=(d, nb),
        in_specs=[cur] + _nbr_specs(lambda r: r * DIL_NCOL + 3 + grp, nb)
        + _nbr_specs(lambda r: r * DIL_NCOL + 6 + grp, nb),
        out_specs=[out_spec, out_spec],
        out_shape=[jax.ShapeDtypeStruct((length, d * DIL_W), F32)] * 2,
        compiler_params=_cparams(("parallel", "parallel")),
    )(zv, zv, zv, zv, zv, zv, zv)
    return o.reshape(s, DIL_W), lse.reshape(s, DIL_W)


def _dil_bwd(zb, do, lse, dl, grp, name):
    s = zb.shape[0]
    d = DIL_PAIRS[grp][1]
    length = s // d
    nb = length // DIL_BQ
    zv = zb.reshape(length, d * IN_B)
    dov, lsev, dlv = (t.reshape(length, d * DIL_W) for t in (do, lse, dl))
    scale = DIL_HEAD_DIM ** -0.5

    def probs(sc, valid, dist, sl_i, lse_col):
        sc = jnp.where(valid, sc * scale - _alibi_slope(grp * DIL_SLOTS + sl_i) * dist, NEG_INF)
        return jnp.exp(sc - lse_col)

    def dq_body(q_ref, kp, kc, kn, vp, vc, vn, do_ref, lse_ref, dl_ref, dq_ref):
        valid, dist = _band(pl.program_id(1), d, length, False)
        outs = []
        for sl_i in range(DIL_SLOTS):
            sl = slice(sl_i * DIL_HEAD_DIM, (sl_i + 1) * DIL_HEAD_DIM)
            c0 = sl_i * DIL_HEAD_DIM
            kcat, vcat = _cat3((kp, kc, kn), sl), _cat3((vp, vc, vn), sl)
            sc = lax.dot_general(q_ref[:, sl], kcat, _DIMS["nt"], preferred_element_type=F32)
            p = probs(sc, valid, dist, sl_i, lse_ref[:, c0:c0 + 1])
            dp = lax.dot_general(do_ref[:, sl], vcat, _DIMS["nt"], preferred_element_type=F32)
            ds = (p * (dp - dl_ref[:, c0:c0 + 1]) * scale).astype(BF16)
            outs.append(jnp.dot(ds, kcat, preferred_element_type=F32))
        dq_ref[...] = jnp.concatenate(outs, axis=1).astype(dq_ref.dtype)

    def dkv_body(k_ref, v_ref, qp, qc, qn, dop, doc, don, lp, lc, ln, dlp, dlc, dln, dk_ref, dv_ref):
        valid, dist = _band(pl.program_id(1), d, length, True)
        dks, dvs = [], []
        for sl_i in range(DIL_SLOTS):
            sl = slice(sl_i * DIL_HEAD_DIM, (sl_i + 1) * DIL_HEAD_DIM)
            one = slice(sl_i * DIL_HEAD_DIM, sl_i * DIL_HEAD_DIM + 1)
            qcat, docat = _cat3((qp, qc, qn), sl), _cat3((dop, doc, don), sl)
            sc = lax.dot_general(qcat, k_ref[:, sl], _DIMS["nt"], preferred_element_type=F32)
            p = probs(sc, valid, dist, sl_i, _cat3((lp, lc, ln), one))
            dvs.append(lax.dot_general(p.astype(BF16), docat, _DIMS["tn"], preferred_element_type=F32))
            dp = lax.dot_general(docat, v_ref[:, sl], _DIMS["nt"], preferred_element_type=F32)
            ds = (p * (dp - _cat3((dlp, dlc, dln), one)) * scale).astype(BF16)
            dks.append(lax.dot_general(ds, qcat, _DIMS["tn"], preferred_element_type=F32))
        dk_ref[...] = jnp.concatenate(dks, axis=1).astype(dk_ref.dtype)
        dv_ref[...] = jnp.concatenate(dvs, axis=1).astype(dv_ref.dtype)

    def zcur(c):
        return pl.BlockSpec((DIL_BQ, DIL_W), lambda r, i: (i, r * DIL_NCOL + c + grp))

    own = pl.BlockSpec((DIL_BQ, DIL_W), lambda r, i: (i, r))
    view = jax.ShapeDtypeStruct((length, d * DIL_W), BF16)
    dq = pl.pallas_call(
        dq_body, name=name + "_dq", grid=(d, nb),
        in_specs=[zcur(0)] + _nbr_specs(lambda r: r * DIL_NCOL + 3 + grp, nb)
        + _nbr_specs(lambda r: r * DIL_NCOL + 6 + grp, nb) + [own, own, own],
        out_specs=own, out_shape=view, compiler_params=_cparams(("parallel", "parallel")),
    )(zv, zv, zv, zv, zv, zv, zv, dov, lsev, dlv)
    own3 = _nbr_specs(lambda r: r, nb)
    dk, dv = pl.pallas_call(
        dkv_body, name=name + "_dkv", grid=(d, nb),
        in_specs=[zcur(3), zcur(6)] + _nbr_specs(lambda r: r * DIL_NCOL + grp, nb) + own3 + own3 + own3,
        out_specs=[own, own], out_shape=[view, view], compiler_params=_cparams(("parallel", "parallel")),
    )(zv, zv, zv, zv, zv, dov, dov, dov, lsev, lsev, lsev, dlv, dlv, dlv)
    return dq.reshape(s, DIL_W), dk.reshape(s, DIL_W), dv.reshape(s, DIL_W)


def _dil_combine(os_, ls_, name):
    def fn(o0, o1, o2, l0, l1, l2):
        m = jnp.maximum(jnp.maximum(l0, l1), l2)
        e0, e1, e2 = jnp.exp(l0 - m), jnp.exp(l1 - m), jnp.exp(l2 - m)
        den = e0 + e1 + e2
        comb = (e0 / den) * o0 + (e1 / den) * o1 + (e2 / den) * o2
        return comb, m + jnp.log(den)
    return _rows_call(fn, list(os_) + list(ls_), [], [(DIL_W, BF16), (DIL_W, F32)], name=name)


def _dil_combine_bwd(dcomb, os_, ls_, lt, seg64, name):
    def fn(dc, o0, o1, o2, l0, l1, l2, ltv, seg):
        w = [jnp.exp(l - ltv) for l in (l0, l1, l2)]
        comb = w[0] * o0 + w[1] * o1 + w[2] * o2
        t = _chunkdot(dc * comb, seg)
        return [wg * dc for wg in w] + [wg * t for wg in w]
    return _rows_call(fn, [dcomb] + list(os_) + list(ls_) + [lt], [seg64],
                      [(DIL_W, BF16)] * 3 + [(DIL_W, F32)] * 3, name=name)


def _mla_prep(za, gq, gkv, cs, swap, name):
    def fn(z, csv, gqv, gkvv, sw):
        return (_rms(z[:, :MLA_Q_RANK], gqv), _rms(z[:, MLA_Q_RANK:640], gkvv), _rope(z[:, 640:], csv, sw))
    return _rows_call(fn, [za, cs], [gq, gkv, swap], [(MLA_Q_RANK, BF16), (MLA_KV_RANK, BF16), (LANES, F32)],
                      name=name)


def _mla_prep_bwd(za, cs, dcq, dckv, dkr, gq, gkv, swap, name):
    def fn(z, csv, dcqv, dckvv, dkrv, gqv, gkvv, sw):
        d1, dg1 = _rms_bwd_math(z[:, :MLA_Q_RANK], gqv, dcqv)
        d2, dg2 = _rms_bwd_math(z[:, MLA_Q_RANK:640], gkvv, dckvv)
        d3 = _rope_t(dkrv, csv, sw)
        return jnp.concatenate([d1, d2, d3], axis=1), dg1, dg2
    return _rows_call(fn, [za, cs, dcq, dckv, dkr], [gq, gkv, swap], [(IN_A_PAD, BF16)],
                      [(1, MLA_Q_RANK), (1, MLA_KV_RANK)], name=name)


def _mla_qk(q_raw, k_pad, krr, cs, swap, name):
    w = MLA_HEADS * MLA_DK

    def fn(qv, kv, krv, csv, sw):
        return _rope(qv, csv, sw) * (MLA_SCALE * LOG2E), kv + _lanes(krv, w)
    return _rows_call(fn, [q_raw, k_pad, krr, cs], [swap], [(w, BF16), (w, BF16)], name=name)


def _mla_qk_bwd(dqh, dkh, cs, swap, name):
    w = MLA_HEADS * MLA_DK

    def fn(dq, dk, csv, sw):
        dk = dk * LN2
        acc = dk[:, :LANES]
        for h in range(1, MLA_HEADS):
            acc = acc + dk[:, h * LANES:(h + 1) * LANES]
        lane = lax.broadcasted_iota(jnp.int32, acc.shape, 1)
        acc = jnp.where((lane >= MLA_NOPE) & (lane < MLA_NOPE + MLA_ROPE), acc, 0.0)
        return _rope_t(dq * MLA_SCALE, csv, sw), dk, acc
    return _rows_call(fn, [dqh, dkh, cs], [swap], [(w, BF16), (w, BF16), (LANES, F32)], name=name)


def _head_norm(t, g2, seg):
    r = lax.rsqrt(_chunkdot(t * t, seg) * (1.0 / GQA_HEAD_DIM) + NORM_EPS)
    return t * r * _lanes(g2, t.shape[1]), r


def _head_norm_bwd(t, g2, seg, dn):
    w = t.shape[1]
    r = lax.rsqrt(_chunkdot(t * t, seg) * (1.0 / GQA_HEAD_DIM) + NORM_EPS)
    u = dn * _lanes(g2, w)
    dt = r * u - t * (r * r * r) * (_chunkdot(u * t, seg) * (1.0 / GQA_HEAD_DIM))
    dgw = jnp.sum(dn * t * r, axis=0, keepdims=True)
    dg = dgw[:, :LANES]
    for c in range(LANES, w, LANES):
        dg = dg + dgw[:, c:c + LANES]
    return dt, dg


def _gqa_prep(q_raw, kv_raw, cs, gq2, gk2, seg, swap, name):
    kw = GQA_KV_HEADS * GQA_HEAD_DIM

    def fn(qv, kvv, csv, gqv, gkv, sg, sw):
        qn, _ = _head_norm(qv, gqv, sg)
        kn, _ = _head_norm(kvv[:, :kw], gkv, sg)
        return _rope(qn, csv, sw) * (GQA_SCALE * LOG2E), _rope(kn, csv, sw), kvv[:, kw:]
    return _rows_call(fn, [q_raw, kv_raw, cs], [gq2, gk2, seg, swap],
                      [(GQA_HEADS * GQA_HEAD_DIM, BF16), (kw, BF16), (kw, BF16)], name=name)


def _gqa_prep_bwd(q_raw, kv_raw, cs, dqh, dkh, dv, gq2, gk2, seg, swap, name):
    kw = GQA_KV_HEADS * GQA_HEAD_DIM

    def fn(qv, kvv, csv, dq, dk, dvv, gqv, gkv, sg, sw):
        dqr, dgq = _head_norm_bwd(qv, gqv, sg, _rope_t(dq * GQA_SCALE, csv, sw))
        dkr, dgk = _head_norm_bwd(kvv[:, :kw], gkv, sg, _rope_t(dk * LN2, csv, sw))
        return dqr, jnp.concatenate([dkr, dvv], axis=1), dgq, dgk
    return _rows_call(fn, [q_raw, kv_raw, cs, dqh, dkh, dv], [gq2, gk2, seg, swap],
                      [(GQA_HEADS * GQA_HEAD_DIM, BF16), (2 * kw, BF16)], [(1, LANES), (1, LANES)], name=name)


def _swiglu(h, name):
    def fn(hv):
        gate, up = hv[:, :FFN_HIDDEN].astype(F32), hv[:, FFN_HIDDEN:].astype(F32)
        return gate / (1.0 + jnp.exp(-gate)) * up
    return _rows_call(fn, [h], [], [(FFN_HIDDEN, BF16)], name=name)[0]


def _swiglu_bwd(h, da, name):
    def fn(hv, dav):
        gate, up = hv[:, :FFN_HIDDEN].astype(F32), hv[:, FFN_HIDDEN:].astype(F32)
        dav = dav.astype(F32)
        sig = 1.0 / (1.0 + jnp.exp(-gate))
        silu = gate * sig
        dgate = dav * up * (sig + silu * (1.0 - sig))
        return jnp.concatenate([dgate, dav * silu], axis=1)
    return _rows_call(fn, [h, da], [], [(2 * FFN_HIDDEN, BF16)], name=name)[0]


def _loss_head(x, target, g, name):
    dm = x.shape[1]

    def fn(xv, tv, gv):
        err = _rms(xv, gv) - tv
        loss = 0.5 * jnp.sum(err * err) / dm
        dx, dg = _rms_bwd_math(xv, gv, err * (1.0 / dm))
        return dx, jnp.zeros((1, LANES), F32) + loss, dg
    return _rows_call(fn, [x, target], [g], [(dm, F32)], [(1, LANES), (1, dm)], name=name)


def _adamw(w, g, m, v, name):
    def fn(wv, gv, mv, vv):
        m2 = ADAM_B1 * mv + (1.0 - ADAM_B1) * gv
        v2 = ADAM_B2 * vv + (1.0 - ADAM_B2) * (gv * gv)
        m_hat = m2 / (1.0 - ADAM_B1 ** ADAM_STEP)
        v_hat = v2 / (1.0 - ADAM_B2 ** ADAM_STEP)
        return -ADAM_LR * (m_hat / (jnp.sqrt(v_hat) + ADAM_EPS) + ADAM_WD * wv), m2, v2
    c = w.shape[1]
    return _rows_call(fn, [w, g, m, v], [], [(c, F32)] * 3, bs=_pick(w.shape[0], 256, 8), name=name)


HBM_SPEC = pl.BlockSpec(memory_space=pltpu.HBM)
VMEM_SPEC = pl.BlockSpec(memory_space=pltpu.VMEM)


def _position():
    return lax.axis_index("x"), lax.axis_index("y"), lax.axis_index("c")


def _other_chips(x, y):
    return [(1 - x, y), (x, 1 - y), (1 - x, 1 - y)]


HALF_W = PACK_W // 2


def _cols(c):
    return pl.ds(pl.multiple_of(c * HALF_W, HALF_W), HALF_W)


def _all_gather_weights(packed):
    rows = packed.shape[0]

    def body(p_ref, g_ref, send_sems, recv_sems, local_sem):
        x, y, c = _position()
        chips = _other_chips(x, y)

        def half(chip, hc):
            return g_ref.at[2 * chip[0] + chip[1], :, _cols(hc)]

        def copy(j, src, dst, to):
            return pltpu.make_async_remote_copy(src_ref=src, dst_ref=dst, send_sem=send_sems.at[j],
                                                recv_sem=recv_sems.at[j], device_id=to, device_id_type=MESH)

        mine = pltpu.make_async_copy(p_ref, g_ref.at[2 * x + y], local_sem)
        mine.start()
        first = [copy(j, p_ref.at[:, _cols(c)], half((x, y), c), (*chip, c)) for j, chip in enumerate(chips)]
        for cp in first:
            cp.start()
        passed = [copy(3 + j, half(chip, c), half(chip, c), (x, y, 1 - c)) for j, chip in enumerate(chips)]
        for j, chip in enumerate(chips):
            copy(j, half(chip, c), half(chip, c), (x, y, c)).wait_recv()
            passed[j].start()
        for j, chip in enumerate(chips):
            copy(3 + j, half(chip, 1 - c), half(chip, 1 - c), (x, y, c)).wait_recv()
        for cp in first + passed:
            cp.wait_send()
        mine.wait()

    return pl.pallas_call(
        body, name="all_gather_weights", in_specs=[HBM_SPEC], out_specs=HBM_SPEC,
        out_shape=jax.ShapeDtypeStruct((4, rows, packed.shape[1]), packed.dtype),
        scratch_shapes=[pltpu.SemaphoreType.DMA((6,)), pltpu.SemaphoreType.DMA((6,)), pltpu.SemaphoreType.DMA],
    )(packed)


SEM_SPEC = pl.BlockSpec(memory_space=pltpu.SEMAPHORE)
ANY_SPEC = pl.BlockSpec(memory_space=pl.ANY)
DATAFLOW = pltpu.SideEffectType.DATAFLOW_SIDE_EFFECTING


def _hbm(a):
    return pltpu.with_memory_space_constraint(a, pltpu.HBM)


def _gather_start(packed, tag):
    rows = packed.shape[0]

    def body(p_ref, g_ref, send_sems, recv_sems, p_thru, g_thru, token):
        x, y, c = _position()
        for j, chip in enumerate(_other_chips(x, y)):
            for s in range(2):
                pltpu.make_async_remote_copy(
                    src_ref=p_ref.at[:, _cols(c)], dst_ref=g_ref.at[2 * x + y, :, _cols(c)],
                    send_sem=send_sems.at[2 * j + s], recv_sem=recv_sems.at[2 * j + s],
                    device_id=(*chip, 1 - c if s else c), device_id_type=MESH).start()
        token[...] = jnp.zeros_like(token)

    return pl.pallas_call(
        body, name=tag + "_start",
        out_shape=(pltpu.SemaphoreType.DMA((6,)), pltpu.SemaphoreType.DMA((6,)), pltpu.HBM(packed.shape, packed.dtype),
                   pltpu.HBM((4, rows, PACK_W), packed.dtype), jax.ShapeDtypeStruct((8, LANES), F32)),
        in_specs=(HBM_SPEC, HBM_SPEC), out_specs=(SEM_SPEC, SEM_SPEC, HBM_SPEC, HBM_SPEC, VMEM_SPEC),
        input_output_aliases={0: 2, 1: 3},
        compiler_params=pltpu.CompilerParams(has_side_effects=DATAFLOW),
    )(_hbm(packed), _hbm(lax.empty((4, rows, PACK_W), packed.dtype)))


def _gather_wait(send_sems, recv_sems, p_thru, g_thru, after, tag):
    def body(p_ref, g_ref, send_sems, recv_sems, after_ref, p_dead, got_ref):
        x, y, c = _position()
        for j, chip in enumerate(_other_chips(x, y)):
            for s in range(2):
                cp = pltpu.make_async_remote_copy(
                    src_ref=p_ref.at[:, _cols(c)], dst_ref=g_ref.at[2 * chip[0] + chip[1], :, _cols(1 - c if s else c)],
                    send_sem=send_sems.at[2 * j + s], recv_sem=recv_sems.at[2 * j + s],
                    device_id=(x, y, c), device_id_type=MESH)
                cp.wait_send()
                cp.wait_recv()

    return pl.pallas_call(
        body, name=tag + "_wait",
        out_shape=(pltpu.HBM(p_thru.shape, p_thru.dtype), pltpu.HBM(g_thru.shape, g_thru.dtype)),
        in_specs=(HBM_SPEC, HBM_SPEC, SEM_SPEC, SEM_SPEC, ANY_SPEC), out_specs=(HBM_SPEC, HBM_SPEC),
        input_output_aliases={0: 0, 1: 1},
        compiler_params=pltpu.CompilerParams(has_side_effects=DATAFLOW),
    )(p_thru, g_thru, send_sems, recv_sems, after)[1]


def _sibling_swap_halves(grads, tag):
    rows = grads.shape[1]

    def body(g_ref, a_ref, send_sem, recv_sem):
        x, y, c = _position()
        cp = pltpu.make_async_remote_copy(src_ref=g_ref.at[:, :, _cols(1 - c)], dst_ref=a_ref,
                                          send_sem=send_sem, recv_sem=recv_sem, device_id=(x, y, 1 - c),
                                          device_id_type=MESH)
        cp.start()
        cp.wait()

    return pl.pallas_call(
        body, name=tag + "_swap_cores", in_specs=[HBM_SPEC], out_specs=HBM_SPEC,
        out_shape=jax.ShapeDtypeStruct((4, rows, HALF_W), grads.dtype),
        scratch_shapes=[pltpu.SemaphoreType.DMA, pltpu.SemaphoreType.DMA],
    )(grads)


def _rs_block(rows):
    return max(d for d in range(16, 1601, 16) if rows % d == 0)


def _chip_sum(grads, other, c, tag):
    rows = other.shape[1]
    rb = _rs_block(rows)

    def body(c_ref, g_ref, a_ref, o_ref):
        o_ref[...] = (g_ref[...] + a_ref[...]).astype(o_ref.dtype)

    blk = (1, rb, HALF_W)
    return pl.pallas_call(
        body, name=tag + "_chip_sum",
        grid_spec=pltpu.PrefetchScalarGridSpec(
            num_scalar_prefetch=1, grid=(4, rows // rb),
            in_specs=[pl.BlockSpec(blk, lambda k, i, c_ref: (k, i, c_ref[0])),
                      pl.BlockSpec(blk, lambda k, i, c_ref: (k, i, 0))],
            out_specs=pl.BlockSpec(blk, lambda k, i, c_ref: (k, i, 0))),
        out_shape=jax.ShapeDtypeStruct(other.shape, BF16),
        compiler_params=_cparams(("parallel", "parallel")),
    )(jnp.reshape(c, (1,)).astype(jnp.int32), grads, other)


def _chip_copies(t_ref, b_ref, send_sems, recv_sems):
    x, y, c = _position()
    return [pltpu.make_async_remote_copy(src_ref=t_ref.at[2 * chip[0] + chip[1]], dst_ref=b_ref.at[j],
                                         send_sem=send_sems.at[j], recv_sem=recv_sems.at[j],
                                         device_id=(*chip, c), device_id_type=MESH)
            for j, chip in enumerate(_other_chips(x, y))]


def _send_chip_sums(sums, tag):
    def body(t_ref, b_ref, send_sems, recv_sems):
        copies = _chip_copies(t_ref, b_ref, send_sems, recv_sems)
        for cp in copies:
            cp.start()
        for cp in copies:
            cp.wait()

    return pl.pallas_call(
        body, name=tag + "_send_chips", in_specs=[HBM_SPEC], out_specs=HBM_SPEC,
        out_shape=jax.ShapeDtypeStruct((3,) + sums.shape[1:], sums.dtype),
        scratch_shapes=[pltpu.SemaphoreType.DMA((3,)), pltpu.SemaphoreType.DMA((3,))],
    )(sums)


def _send_chip_sums_start(sums, tag):
    land = (3,) + sums.shape[1:]

    def body(t_ref, b_ref, send_sems, recv_sems, t_thru, b_thru, token):
        for cp in _chip_copies(t_ref, b_ref, send_sems, recv_sems):
            cp.start()
        token[...] = jnp.zeros_like(token)

    return pl.pallas_call(
        body, name=tag + "_send_chips_start",
        out_shape=(pltpu.SemaphoreType.DMA((3,)), pltpu.SemaphoreType.DMA((3,)), pltpu.HBM(sums.shape, sums.dtype),
                   pltpu.HBM(land, sums.dtype), jax.ShapeDtypeStruct((8, LANES), F32)),
        in_specs=(HBM_SPEC, HBM_SPEC), out_specs=(SEM_SPEC, SEM_SPEC, HBM_SPEC, HBM_SPEC, VMEM_SPEC),
        input_output_aliases={0: 2, 1: 3},
        compiler_params=pltpu.CompilerParams(has_side_effects=DATAFLOW),
    )(_hbm(sums), _hbm(lax.empty(land, sums.dtype)))


def _send_chip_sums_wait(send_sems, recv_sems, t_thru, b_thru, after, tag):
    def body(t_ref, b_ref, send_sems, recv_sems, after_ref, t_dead, got_ref):
        for cp in _chip_copies(t_ref, b_ref, send_sems, recv_sems):
            cp.wait_send()
            cp.wait_recv()

    return pl.pallas_call(
        body, name=tag + "_send_chips_wait",
        out_shape=(pltpu.HBM(t_thru.shape, t_thru.dtype), pltpu.HBM(b_thru.shape, b_thru.dtype)),
        in_specs=(HBM_SPEC, HBM_SPEC, SEM_SPEC, SEM_SPEC, ANY_SPEC), out_specs=(HBM_SPEC, HBM_SPEC),
        input_output_aliases={0: 0, 1: 1},
        compiler_params=pltpu.CompilerParams(has_side_effects=DATAFLOW),
    )(t_thru, b_thru, send_sems, recv_sems, after)[1]


def _final_sum(grads, other, recv, k, c, tag):
    rows = other.shape[1]
    rb = _rs_block(rows)

    def body(k_ref, c_ref, g_ref, a_ref, b_ref, o_ref):
        own = g_ref[0] + a_ref[0]
        o_ref[...] = ((own + b_ref[0].astype(F32)) + b_ref[1].astype(F32)) + b_ref[2].astype(F32)

    return pl.pallas_call(
        body, name=tag + "_final_sum",
        grid_spec=pltpu.PrefetchScalarGridSpec(
            num_scalar_prefetch=2, grid=(rows // rb,),
            in_specs=[pl.BlockSpec((1, rb, HALF_W), lambda i, k_ref, c_ref: (k_ref[0], i, c_ref[0])),
                      pl.BlockSpec((1, rb, HALF_W), lambda i, k_ref, c_ref: (k_ref[0], i, 0)),
                      pl.BlockSpec((3, rb, HALF_W), lambda i, k_ref, c_ref: (0, i, 0))],
            out_specs=pl.BlockSpec((rb, HALF_W), lambda i, k_ref, c_ref: (i, 0))),
        out_shape=jax.ShapeDtypeStruct((rows, HALF_W), F32),
        compiler_params=_cparams(("parallel",)),
    )(jnp.reshape(k, (1,)).astype(jnp.int32), jnp.reshape(c, (1,)).astype(jnp.int32), grads, other, recv)


def _join_halves(half, tag):
    rows = half.shape[0]

    def body(h_ref, o_ref, send_sem, recv_sem, local_sem):
        x, y, c = _position()
        mine = pltpu.make_async_copy(h_ref, o_ref.at[:, _cols(c)], local_sem)
        mine.start()
        cp = pltpu.make_async_remote_copy(src_ref=h_ref, dst_ref=o_ref.at[:, _cols(c)], send_sem=send_sem,
                                          recv_sem=recv_sem, device_id=(x, y, 1 - c), device_id_type=MESH)
        cp.start()
        pltpu.make_async_remote_copy(src_ref=h_ref, dst_ref=o_ref.at[:, _cols(1 - c)], send_sem=send_sem,
                                     recv_sem=recv_sem, device_id=(x, y, 1 - c), device_id_type=MESH).wait_recv()
        cp.wait_send()
        mine.wait()

    return pl.pallas_call(
        body, name=tag + "_join_cores", in_specs=[HBM_SPEC], out_specs=HBM_SPEC,
        out_shape=jax.ShapeDtypeStruct((rows, PACK_W), half.dtype),
        scratch_shapes=[pltpu.SemaphoreType.DMA, pltpu.SemaphoreType.DMA, pltpu.SemaphoreType.DMA],
    )(half)


def _all_reduce_packet(packet):
    rows = packet.shape[0]

    def body(p_ref, o_ref, buf, send_sems, recv_sems):
        x, y, c = _position()
        me = 4 * x + 2 * y + c
        buf[me] = p_ref[...]

        def flip(v, bit):
            return 1 - v if bit else v

        for p in range(1, 8):
            peer = (flip(x, p & 4), flip(y, p & 2), flip(c, p & 1))
            pltpu.make_async_remote_copy(src_ref=p_ref, dst_ref=buf.at[me], send_sem=send_sems.at[p - 1],
                                         recv_sem=recv_sems.at[p - 1], device_id=peer, device_id_type=MESH).start()
        for p in range(1, 8):
            peer = (flip(x, p & 4), flip(y, p & 2), flip(c, p & 1))
            slot = 4 * peer[0] + 2 * peer[1] + peer[2]
            cp = pltpu.make_async_remote_copy(src_ref=p_ref, dst_ref=buf.at[slot], send_sem=send_sems.at[p - 1],
                                              recv_sem=recv_sems.at[p - 1], device_id=peer, device_id_type=MESH)
            cp.wait_recv()
            cp.wait_send()
        acc = buf[0]
        for dev in range(1, 8):
            acc = acc + buf[dev]
        o_ref[...] = acc

    return pl.pallas_call(
        body, name="all_reduce_packet", in_specs=[VMEM_SPEC], out_specs=VMEM_SPEC,
        out_shape=jax.ShapeDtypeStruct(packet.shape, F32),
        scratch_shapes=[pltpu.VMEM((8, rows, LANES), F32), pltpu.SemaphoreType.DMA((7,)),
                        pltpu.SemaphoreType.DMA((7,))],
    )(packet)


def _stack_range(name, n_stack, pack):
    if name.startswith('gqa'):
        return (0, 0) if pack == 0 else (0, n_stack)
    return (0, 1) if pack == 0 else (1, n_stack)


def _pack_members(pack):
    out = []
    for n, shape, ax in BIG:
        lo, hi = _stack_range(n, shape[0], pack)
        if hi > lo:
            out.append((n, (hi - lo,) + shape[1:], ax, (lo, hi)))
    return out


PACK_ROW_MULTIPLE = 512


def _pad_rows(parts, dtype):
    rows = sum(p.shape[0] for p in parts)
    pad = -rows % PACK_ROW_MULTIPLE
    return jnp.concatenate(parts + ([jnp.zeros((pad, PACK_W), dtype)] if pad else []), axis=0)


def _pack_blocks(blocks, dtype, pack):
    return _pad_rows([blocks[n][lo:hi].astype(dtype).reshape(-1, PACK_W)
                      for n, _, _, (lo, hi) in _pack_members(pack)], dtype)


def _unpack_blocks(packed, pack):
    out, off = {}, 0
    for n, shape, _, _ in _pack_members(pack):
        r = math.prod(shape) // PACK_W
        out[n] = packed[off:off + r].reshape(shape)
        off += r
    return out


def _unpack_gathered(gathered, pack):
    per_chip = [_unpack_blocks(gathered[k], pack) for k in range(4)]
    return {n: jnp.concatenate([per_chip[k][n] for k in range(4)], axis=ax) for n, _, ax, _ in _pack_members(pack)}


def _pack_full(full, dtype, pack):
    chips = []
    for k in range(4):
        parts = []
        for n, shape, ax, _ in _pack_members(pack):
            blk = lax.slice_in_dim(full[n], k * shape[ax], (k + 1) * shape[ax], axis=ax)
            parts.append(blk.astype(dtype).reshape(-1, PACK_W))
        chips.append(_pad_rows(parts, dtype))
    return jnp.stack(chips, axis=0)


def _pack_small(vals, loss_row):
    rows = [loss_row.reshape(1, LANES)]
    for n, shape in SMALL:
        v = vals.get(n)
        v = jnp.zeros(shape, F32) if v is None else v
        rows.append(v.astype(F32).reshape(-1, LANES))
    packet = jnp.concatenate(rows, axis=0)
    return jnp.pad(packet, ((0, PACKET_ROWS - packet.shape[0]), (0, 0)))


def _unpack_small(packet):
    out, off = {}, 1
    for n, shape in SMALL:
        r = math.prod(shape) // LANES
        out[n] = packet[off:off + r].reshape(shape)
        off += r
    return packet[0, 0], out


_MLA = dict(R=1, dk=MLA_DK, dv=MLA_V, hb=2, bq=512, bk=512)
_MLA_FWD_BQ = 1024
_GQA = dict(R=GQA_HEADS // GQA_KV_HEADS, dk=GQA_HEAD_DIM, dv=GQA_HEAD_DIM, hb=2, bq=256, bk=512)


def _layer_params(layer, full, gains):
    pack = 0 if layer == 0 else 1
    i = layer // 2

    def mat(name):
        lo, _ = _stack_range(name, 4 if name.startswith('ffn') else 2, pack)
        return full[name][(layer if name.startswith('ffn') else i) - lo]

    p = dict(ffn_norm=gains['ffn_norm'][layer][None], ffn_w_in=mat('ffn_w_in'), ffn_w_out=mat('ffn_w_out'))
    if layer % 2 == 0:
        w_in = mat('w_in_ab')
        zeros = jnp.zeros((D_MODEL, 32), w_in.dtype)
        p['w_a'] = jnp.concatenate([w_in[:, :640], zeros, zeros, w_in[:, 640:IN_A], zeros], axis=1)
        p['w_b'] = w_in[:, IN_A:]
        p['w_uq'] = jnp.pad(mat('mla_w_uq'), ((0, 0), (0, 0), (0, MLA_DK - 96))).reshape(MLA_Q_RANK, -1)
        ukv = mat('mla_w_ukv')
        p['w_uk'] = jnp.pad(ukv[:, :, :MLA_NOPE], ((0, 0), (0, 0), (0, MLA_DK - MLA_NOPE))).reshape(MLA_KV_RANK, -1)
        p['w_uv'] = ukv[:, :, MLA_NOPE:].reshape(MLA_KV_RANK, -1)
        p['w_out'] = mat('w_out_ab')
        p['mix_norm'] = gains['mix_norm_ab'][i][None]
        p['q_norm'] = gains['mla_q_norm'][i][None]
        p['kv_norm'] = gains['mla_kv_norm'][i][None]
    else:
        p['w_q'], p['w_kv'], p['w_o'] = mat('gqa_w_q'), mat('gqa_w_kv'), mat('gqa_w_o')
        p['mix_norm'] = gains['mix_norm_c'][i][None]
        p['q_norm'] = jnp.tile(gains['gqa_q_norm'][i][None], (1, 2))
        p['k_norm'] = jnp.tile(gains['gqa_k_norm'][i][None], (1, 2))
    return p


def _even_fwd(x, p, cs, swap, tag):
    xn = _rms_fwd(x, p['mix_norm'], tag + "_norm")
    za = _mm(xn, p['w_a'], name=tag + "_in_a")
    zb = _mm(xn, p['w_b'], out_dtype=BF16, name=tag + "_in_b")
    cq, ckv, krr = _mla_prep(za, p['q_norm'], p['kv_norm'], cs, swap, tag + "_mla_prep")
    q_raw = _mm(cq, p['w_uq'], name=tag + "_uq")
    k_pad = _mm(ckv, p['w_uk'], name=tag + "_uk")
    v = _mm(ckv, p['w_uv'], out_dtype=BF16, name=tag + "_uv")
    qh, kh = _mla_qk(q_raw, k_pad, krr, cs, swap, tag + "_mla_qk")
    o_a, lse_a = _flash_fwd(qh, kh, v, name=tag + "_mla_attn", **dict(_MLA, bq=_MLA_FWD_BQ))
    og, lg = [], []
    for grp in range(DIL_GROUPS):
        o, l = _dil_fwd(zb, grp, f"{tag}_dil{grp}")
        og.append(o)
        lg.append(l)
    o_b, lt = _dil_combine(og, lg, tag + "_dil_merge")
    ocat = jnp.concatenate([o_a, o_b], axis=1)
    x1 = _mm(ocat, p['w_out'], add=x, name=tag + "_out")
    saved = dict(x=x, xn=xn, za=za, zb=zb, cq=cq, ckv=ckv, qh=qh, kh=kh, v=v, lse_a=lse_a, og=og, lg=lg, lt=lt,
                 ocat=ocat)
    return x1, saved


def _even_bwd(dx1, p, sv, cs, swap, seg64, tag):
    docat = _mm(dx1, p['w_out'], mode="nt", name=tag + "_out_dx")
    d_w_out = _mm(sv['ocat'], dx1, mode="tn", name=tag + "_out_dw")
    n_a = MLA_HEADS * MLA_V
    do_a = docat[:, :n_a].astype(BF16)
    res = _dil_combine_bwd(docat[:, n_a:], sv['og'], sv['lg'], sv['lt'], seg64, tag + "_dil_merge_bwd")
    dqs, dks, dvs = [], [], []
    for grp in range(DIL_GROUPS):
        dq, dk, dv = _dil_bwd(sv['zb'], res[grp], sv['lg'][grp], res[3 + grp], grp, f"{tag}_dil{grp}_bwd")
        dqs.append(dq)
        dks.append(dk)
        dvs.append(dv)
    dzb = jnp.concatenate(dqs + dks + dvs, axis=1)
    dqh, dkh, dv = _flash_bwd(sv['qh'], sv['kh'], sv['v'], sv['ocat'][:, :n_a], do_a, sv['lse_a'],
                              name=tag + "_mla_attn_bwd", **_MLA)
    dq_raw, dkh, dkrr = _mla_qk_bwd(dqh, dkh, cs, swap, tag + "_mla_qk_bwd")
    dcq = _mm(dq_raw, p['w_uq'], mode="nt", name=tag + "_uq_dx")
    d_w_uq = _mm(sv['cq'], dq_raw, mode="tn", name=tag + "_uq_dw")
    dckv = _mm(dkh, p['w_uk'], mode="nt", name=tag + "_uk_dx")
    dckv = _mm(dv, p['w_uv'], mode="nt", add=dckv, name=tag + "_uv_dx")
    d_w_uk = _mm(sv['ckv'], dkh, mode="tn", name=tag + "_uk_dw")
    d_w_uv = _mm(sv['ckv'], dv, mode="tn", name=tag + "_uv_dw")
    dza, d_gq, d_gkv = _mla_prep_bwd(sv['za'], cs, dcq, dckv, dkrr, p['q_norm'], p['kv_norm'], swap,
                                     tag + "_mla_prep_bwd")
    dxn = _mm(dza, p['w_a'], mode="nt", name=tag + "_in_a_dx")
    dxn = _mm(dzb, p['w_b'], mode="nt", add=dxn, name=tag + "_in_b_dx")
    d_w_a = _mm(sv['xn'], dza, mode="tn", name=tag + "_in_a_dw")
    d_w_b = _mm(sv['xn'], dzb, mode="tn", name=tag + "_in_b_dw")
    dx, d_g = _rms_bwd(sv['x'], p['mix_norm'], dxn, dx1, tag + "_norm_bwd")
    d_w_in = jnp.concatenate([d_w_a[:, :640], d_w_a[:, 704:736], d_w_b], axis=1)
    d_uq = d_w_uq.reshape(MLA_Q_RANK, MLA_HEADS, MLA_DK)[:, :, :MLA_NOPE + MLA_ROPE]
    d_ukv = jnp.concatenate([d_w_uk.reshape(MLA_KV_RANK, MLA_HEADS, MLA_DK)[:, :, :MLA_NOPE],
                             d_w_uv.reshape(MLA_KV_RANK, MLA_HEADS, MLA_V)], axis=2)
    grads = dict(w_in_ab=d_w_in, mla_w_uq=d_uq, mla_w_ukv=d_ukv, w_out_ab=d_w_out, mix_norm_ab=d_g[0],
                 mla_q_norm=d_gq[0], mla_kv_norm=d_gkv[0])
    return dx, grads


def _odd_fwd(x, p, cs, seg64, swap, tag):
    xn = _rms_fwd(x, p['mix_norm'], tag + "_norm")
    q_raw = _mm(xn, p['w_q'], name=tag + "_q")
    kv_raw = _mm(xn, p['w_kv'], name=tag + "_kv")
    qh, kh, v = _gqa_prep(q_raw, kv_raw, cs, p['q_norm'], p['k_norm'], seg64, swap, tag + "_gqa_prep")
    o, lse = _flash_fwd(qh, kh, v, name=tag + "_gqa_attn", **_GQA)
    x1 = _mm(o, p['w_o'], add=x, name=tag + "_o")
    return x1, dict(x=x, xn=xn, q_raw=q_raw, kv_raw=kv_raw, qh=qh, kh=kh, v=v, o=o, lse=lse)


def _odd_bwd(dx1, p, sv, cs, seg64, swap, tag):
    do = _mm(dx1, p['w_o'], mode="nt", out_dtype=BF16, name=tag + "_o_dx")
    d_w_o = _mm(sv['o'], dx1, mode="tn", name=tag + "_o_dw")
    dqh, dkh, dv = _flash_bwd(sv['qh'], sv['kh'], sv['v'], sv['o'], do, sv['lse'], name=tag + "_gqa_attn_bwd",
                              **_GQA)
    dq_raw, dkv_raw, d_gq, d_gk = _gqa_prep_bwd(sv['q_raw'], sv['kv_raw'], cs, dqh, dkh, dv, p['q_norm'],
                                                p['k_norm'], seg64, swap, tag + "_gqa_prep_bwd")
    dxn = _mm(dq_raw, p['w_q'], mode="nt", name=tag + "_q_dx")
    dxn = _mm(dkv_raw, p['w_kv'], mode="nt", add=dxn, name=tag + "_kv_dx")
    d_w_q = _mm(sv['xn'], dq_raw, mode="tn", name=tag + "_q_dw")
    d_w_kv = _mm(sv['xn'], dkv_raw, mode="tn", name=tag + "_kv_dw")
    dx, d_g = _rms_bwd(sv['x'], p['mix_norm'], dxn, dx1, tag + "_norm_bwd")
    grads = dict(gqa_w_q=d_w_q, gqa_w_kv=d_w_kv, gqa_w_o=d_w_o, mix_norm_c=d_g[0],
                 gqa_q_norm=d_gq[0, :GQA_HEAD_DIM] + d_gq[0, GQA_HEAD_DIM:],
                 gqa_k_norm=d_gk[0, :GQA_HEAD_DIM] + d_gk[0, GQA_HEAD_DIM:])
    return dx, grads


def _ffn_fwd(x, p, tag):
    xn = _rms_fwd(x, p['ffn_norm'], tag + "_ffn_norm")
    h = _mm(xn, p['ffn_w_in'], out_dtype=BF16, name=tag + "_ffn_in")
    a = _swiglu(h, tag + "_swiglu")
    x2 = _mm(a, p['ffn_w_out'], add=x, name=tag + "_ffn_out")
    return x2, dict(x=x, xn=xn, h=h, a=a)


def _ffn_bwd(dx2, p, sv, tag):
    da = _mm(dx2, p['ffn_w_out'], mode="nt", out_dtype=BF16, name=tag + "_ffn_out_dx")
    d_w_out = _mm(sv['a'], dx2, mode="tn", name=tag + "_ffn_out_dw")
    dh = _swiglu_bwd(sv['h'], da, tag + "_swiglu_bwd")
    d_w_in = _mm(sv['xn'], dh, mode="tn", name=tag + "_ffn_in_dw")
    dxn = _mm(dh, p['ffn_w_in'], mode="nt", name=tag + "_ffn_in_dx")
    dx, d_g = _rms_bwd(sv['x'], p['ffn_norm'], dxn, dx2, tag + "_ffn_norm_bwd")
    return dx, d_w_in, d_w_out, d_g[0]


EVEN_MATS = ('w_in_ab', 'mla_w_uq', 'mla_w_ukv', 'w_out_ab')
ODD_MATS = ('gqa_w_q', 'gqa_w_kv', 'gqa_w_o')
FFN_MATS = ('ffn_w_in', 'ffn_w_out')


def _schedule(x, target, gains, full_of_pack, rest_grads_ready):
    s = x.shape[0]
    cs_mla, cs_gqa = _rope_tables(s)
    swap, seg64 = _swap_matrix(), _seg_matrix(GQA_HEAD_DIM)
    params, saved, full = [], [], None
    for layer in range(4):
        tag = f"l{layer}"
        if layer < 2:
            full = full_of_pack(layer, x)
        p = _layer_params(layer, full, gains)
        if layer % 2 == 0:
            x, sv = _even_fwd(x, p, cs_mla, swap, tag)
        else:
            x, sv = _odd_fwd(x, p, cs_gqa, seg64, swap, tag)
        x, sv_f = _ffn_fwd(x, p, tag)
        params.append(p)
        saved.append((sv, sv_f))
    dx, loss_row, d_final = _loss_head(x, target, gains['final_norm'][None], "loss_head")

    per_layer, rest = {}, None
    for layer in reversed(range(4)):
        p, (sv, sv_f), tag = params[layer], saved[layer], f"l{layer}"
        if layer == 0:
            rest = {n: per_layer[2][n][None] for n in EVEN_MATS}
            rest.update({n: jnp.stack([per_layer[1][n], per_layer[3][n]], axis=0) for n in ODD_MATS})
            rest.update({n: jnp.stack([per_layer[l][n] for l in (1, 2, 3)], axis=0) for n in FFN_MATS})
            token = rest_grads_ready(rest)
            if token is not None:
                p = dict(p, ffn_w_out=p['ffn_w_out'] + token[0, 0].astype(p['ffn_w_out'].dtype))
        dx, d_ffn_in, d_ffn_out, d_ffn_g = _ffn_bwd(dx, p, sv_f, tag)
        if layer % 2 == 0:
            dx, g = _even_bwd(dx, p, sv, cs_mla, swap, seg64, tag)
        else:
            dx, g = _odd_bwd(dx, p, sv, cs_gqa, seg64, swap, tag)
        g.update(ffn_w_in=d_ffn_in, ffn_w_out=d_ffn_out, ffn_norm=d_ffn_g)
        per_layer[layer] = g

    first = {n: per_layer[0][n][None] for n in EVEN_MATS + FFN_MATS}
    small = {'final_norm': d_final[0], 'ffn_norm': jnp.stack([per_layer[l]['ffn_norm'] for l in range(4)], axis=0)}
    for n in ('mix_norm_ab', 'mla_q_norm', 'mla_kv_norm'):
        small[n] = jnp.stack([per_layer[0][n], per_layer[2][n]], axis=0)
    for n in ('mix_norm_c', 'gqa_q_norm', 'gqa_k_norm'):
        small[n] = jnp.stack([per_layer[1][n], per_layer[3][n]], axis=0)
    return loss_row, dx, first, rest, small


def _core_sums(grads, pack, core, tag):
    packed = _pack_full(grads, F32, pack)
    other = _sibling_swap_halves(packed, tag)
    return packed, other, _chip_sum(packed, other, core, tag)


def _finish_reduce_scatter(packed, other, recv, pack, chip, core, tag):
    return _unpack_blocks(_join_halves(_final_sum(packed, other, recv, chip, core, tag), tag), pack)


def _step(x, target, w, m, v):
    big_names = [n for n, _, _ in BIG]
    chip = 2 * lax.axis_index("x") + lax.axis_index("y")
    core = lax.axis_index("c")

    gains = {n: w[n] for n, _ in SMALL if n != 'mix_norm_c'}
    c_cols = w['mix_norm_c'].shape[1]
    own_c = lax.dynamic_update_slice(jnp.zeros((2, 4 * c_cols), F32), w['mix_norm_c'], (0, chip * c_cols))
    gains['mix_norm_c'] = _unpack_small(_all_reduce_packet(_pack_small(
        {'mix_norm_c': own_c * 0.5}, jnp.zeros((LANES,), F32))))[1]['mix_norm_c']

    gathered0 = _all_gather_weights(_pack_blocks(w, BF16, 0))
    packed1, gathered0 = lax.optimization_barrier((_pack_blocks(w, BF16, 1), gathered0))
    ag_send, ag_recv, p_thru, g_thru, ag_token = _gather_start(packed1, "gather_rest")
    gains['mix_norm_ab'] = gains['mix_norm_ab'] + ag_token[0, 0]
    full0 = _unpack_gathered(gathered0, 0)

    def full_of_pack(pack, after):
        if pack == 0:
            return full0
        landed = _gather_wait(ag_send, ag_recv, p_thru, g_thru, after, "gather_rest")
        return _unpack_gathered(lax.dynamic_update_slice(landed, packed1[None], (chip, 0, 0)), 1)

    rs = {}

    def rest_grads_ready(rest):
        rs['packed'], rs['other'], sums = _core_sums(rest, 1, core, "grad_rest")
        rs['send'], rs['recv'], rs['t'], rs['b'], token = _send_chip_sums_start(sums, "grad_rest")
        return token

    loss_row, dx, first, rest, small = _schedule(x[0], target[0], gains, full_of_pack, rest_grads_ready)
    recv1 = _send_chip_sums_wait(rs['send'], rs['recv'], rs['t'], rs['b'], dx, "grad_rest")
    g_rest = _finish_reduce_scatter(rs['packed'], rs['other'], recv1, 1, chip, core, "grad_rest")
    packed0, other0, sums0 = _core_sums(first, 0, core, "grad_first")
    g_first = _finish_reduce_scatter(packed0, other0, _send_chip_sums(sums0, "grad_first"), 0, chip, core,
                                     "grad_first")
    g_blocks = {n: (jnp.concatenate([g_first[n], g_rest[n]], axis=0) if n in g_first else g_rest[n])
                for n in big_names}

    loss, g_small = _unpack_small(_all_reduce_packet(_pack_small(small, loss_row[0])))
    g_small['mix_norm_c'] = lax.dynamic_slice(g_small['mix_norm_c'], (0, chip * c_cols), (2, c_cols))

    out_g, out_d, out_m, out_v = {}, {}, {}, {}
    for n in big_names:
        shape = w[n].shape
        cols = shape[-1]
        d_, m_, v_ = _adamw(w[n].reshape(-1, cols), g_blocks[n].reshape(-1, cols), m[n].reshape(-1, cols),
                            v[n].reshape(-1, cols), "adamw_" + n)
        out_g[n], out_d[n], out_m[n], out_v[n] = g_blocks[n], d_.reshape(shape), m_.reshape(shape), v_.reshape(shape)
    for n, _ in SMALL:
        shape = w[n].shape
        as2d = (lambda t: t.reshape(1, -1)) if len(shape) == 1 else (lambda t: t)
        d_, m_, v_ = _adamw(as2d(w[n]), as2d(g_small[n]), as2d(m[n]), as2d(v[n]), "adamw_" + n)
        out_g[n], out_d[n], out_m[n], out_v[n] = g_small[n], d_.reshape(shape), m_.reshape(shape), v_.reshape(shape)
    return (loss, dx[None], *[out_g[n] for n in WEIGHTS], *[out_d[n] for n in WEIGHTS],
            *[out_m[n] for n in WEIGHTS], *[out_v[n] for n in WEIGHTS])


def kernel(x, mix_norm_ab, w_in_ab, mla_q_norm, mla_kv_norm, mla_w_uq, mla_w_ukv, w_out_ab, mix_norm_c, gqa_w_q, gqa_w_kv, gqa_q_norm, gqa_k_norm, gqa_w_o, ffn_norm, ffn_w_in, ffn_w_out, final_norm, loss_target, m_mix_norm_ab, m_w_in_ab, m_mla_q_norm, m_mla_kv_norm, m_mla_w_uq, m_mla_w_ukv, m_w_out_ab, m_mix_norm_c, m_gqa_w_q, m_gqa_w_kv, m_gqa_q_norm, m_gqa_k_norm, m_gqa_w_o, m_ffn_norm, m_ffn_w_in, m_ffn_w_out, m_final_norm, v_mix_norm_ab, v_w_in_ab, v_mla_q_norm, v_mla_kv_norm, v_mla_w_uq, v_mla_w_ukv, v_w_out_ab, v_mix_norm_c, v_gqa_w_q, v_gqa_w_kv, v_gqa_q_norm, v_gqa_k_norm, v_gqa_w_o, v_ffn_norm, v_ffn_w_in, v_ffn_w_out, v_final_norm):
    w = dict(zip(WEIGHTS, (mix_norm_ab, w_in_ab, mla_q_norm, mla_kv_norm, mla_w_uq, mla_w_ukv, w_out_ab, mix_norm_c,
                           gqa_w_q, gqa_w_kv, gqa_q_norm, gqa_k_norm, gqa_w_o, ffn_norm, ffn_w_in, ffn_w_out,
                           final_norm)))
    m = dict(zip(WEIGHTS, (m_mix_norm_ab, m_w_in_ab, m_mla_q_norm, m_mla_kv_norm, m_mla_w_uq, m_mla_w_ukv,
                           m_w_out_ab, m_mix_norm_c, m_gqa_w_q, m_gqa_w_kv, m_gqa_q_norm, m_gqa_k_norm, m_gqa_w_o,
                           m_ffn_norm, m_ffn_w_in, m_ffn_w_out, m_final_norm)))
    v = dict(zip(WEIGHTS, (v_mix_norm_ab, v_w_in_ab, v_mla_q_norm, v_mla_kv_norm, v_mla_w_uq, v_mla_w_ukv,
                           v_w_out_ab, v_mix_norm_c, v_gqa_w_q, v_gqa_w_kv, v_gqa_q_norm, v_gqa_k_norm, v_gqa_w_o,
                           v_ffn_norm, v_ffn_w_in, v_ffn_w_out, v_final_norm)))
    return _step(x, loss_target, w, m, v)
```

```python
import math

import numpy as np
import jax
import jax.numpy as jnp
from jax import lax
from jax.experimental import pallas as pl
from jax.experimental.pallas import tpu as pltpu

F32 = jnp.float32
BF16 = jnp.bfloat16
MESH = pl.DeviceIdType.MESH

VMEM_LIMIT_BYTES = 56 * 1024 * 1024
LANES = 128

D_MODEL = 1024
NORM_EPS = 1e-6
ROPE_THETA = 10000.0
NEG_INF = -1e30
GRID_W = 64

MLA_HEADS, MLA_Q_RANK, MLA_KV_RANK, MLA_NOPE, MLA_ROPE, MLA_V = 8, 384, 256, 64, 32, 64
MLA_DK = 128
DIL_PAIRS = ((128, 1), (512, 4), (2048, 16))
DIL_HALF, DIL_SLOTS, DIL_GROUPS, DIL_HEAD_DIM = 64, 4, 3, 64
DIL_HEADS = DIL_SLOTS * DIL_GROUPS
DIL_W = DIL_SLOTS * DIL_HEAD_DIM
GQA_HEADS, GQA_KV_HEADS, GQA_HEAD_DIM = 16, 4, 64
FFN_HIDDEN = 2816
IN_A = MLA_Q_RANK + MLA_KV_RANK + MLA_ROPE
IN_A_PAD = 768
IN_B = 3 * DIL_HEADS * DIL_HEAD_DIM

ADAM_LR, ADAM_B1, ADAM_B2, ADAM_EPS, ADAM_WD, ADAM_STEP = 0.001, 0.9, 0.999, 1e-08, 0.01, 10

LOG2E, LN2 = math.log2(math.e), math.log(2.0)
MLA_SCALE = (MLA_NOPE + MLA_ROPE) ** -0.5
GQA_SCALE = GQA_HEAD_DIM ** -0.5

WEIGHTS = ['mix_norm_ab', 'w_in_ab', 'mla_q_norm', 'mla_kv_norm', 'mla_w_uq', 'mla_w_ukv', 'w_out_ab', 'mix_norm_c',
           'gqa_w_q', 'gqa_w_kv', 'gqa_q_norm', 'gqa_k_norm', 'gqa_w_o', 'ffn_norm', 'ffn_w_in', 'ffn_w_out',
           'final_norm']
BIG = (('w_in_ab', (2, 1024, 744), 2), ('mla_w_uq', (2, 96, 8, 96), 1), ('mla_w_ukv', (2, 64, 8, 128), 1),
       ('w_out_ab', (2, 768, 256), 2), ('gqa_w_q', (2, 256, 1024), 1), ('gqa_w_kv', (2, 256, 512), 1),
       ('gqa_w_o', (2, 256, 1024), 1), ('ffn_w_in', (4, 1024, 1408), 2), ('ffn_w_out', (4, 704, 1024), 1))
PACK_W = 1024
SMALL = (('mix_norm_ab', (2, 1024)), ('mla_q_norm', (2, 384)), ('mla_kv_norm', (2, 256)), ('gqa_q_norm', (2, 64)),
         ('gqa_k_norm', (2, 64)), ('ffn_norm', (4, 1024)), ('final_norm', (1024,)), ('mix_norm_c', (2, 1024)))
PACKET_ROWS = 88


def _cparams(sem=None):
    return pltpu.CompilerParams(dimension_semantics=sem, vmem_limit_bytes=VMEM_LIMIT_BYTES)


def _pick(n, pref, mult=LANES):
    if n <= pref:
        return n
    for d in range(pref - pref % mult, 0, -mult):
        if n % d == 0:
            return d
    return n


_DIMS = {"nn": (((1,), (0,)), ((), ())), "nt": (((1,), (1,)), ((), ())), "tn": (((0,), (0,)), ((), ()))}


def _mm(a, b, *, mode="nn", add=None, out_dtype=F32, name="mm"):
    if mode == "nn":
        (m, k), (k2, n) = a.shape, b.shape
    elif mode == "nt":
        (m, k), (n, k2) = a.shape, b.shape
    else:
        (k, m), (k2, n) = a.shape, b.shape
    assert k == k2, (a.shape, b.shape, mode)
    if mode == "tn":
        bm, bn, bk = _pick(m, 1408), _pick(n, 1024), _pick(k, 1024, 16)
    else:
        bm, bn, bk = _pick(m, 512, 16), _pick(n, 1408), _pick(k, 1408)
    nk = k // bk
    assert m % bm == 0 and n % bn == 0 and k % bk == 0
    has_add = add is not None

    def body(*refs):
        a_ref, b_ref = refs[0], refs[1]
        add_ref = refs[2] if has_add else None
        o_ref = refs[3 if has_add else 2]
        part = lax.dot_general(a_ref[...].astype(BF16), b_ref[...].astype(BF16), _DIMS[mode],
                               preferred_element_type=F32)

        def finish(r):
            if has_add:
                r = r + add_ref[...]
            o_ref[...] = r.astype(o_ref.dtype)

        if nk == 1:
            finish(part)
        else:
            acc_ref = refs[-1]
            kk = pl.program_id(2)

            @pl.when(kk == 0)
            def _():
                acc_ref[...] = part

            @pl.when(kk > 0)
            def _():
                acc_ref[...] += part

            @pl.when(kk == nk - 1)
            def _():
                finish(acc_ref[...])

    a_bytes, b_bytes = a.size * a.dtype.itemsize, b.size * b.dtype.itemsize
    n_outer = nk == 1 and (n // bn) * a_bytes + b_bytes < a_bytes + (m // bm) * b_bytes

    def at(f):
        return (lambda j, i, kk: f(i, j, kk)) if n_outer else f

    if mode == "nn":
        a_spec = pl.BlockSpec((bm, bk), at(lambda i, j, kk: (i, kk)))
        b_spec = pl.BlockSpec((bk, bn), at(lambda i, j, kk: (kk, j)))
    elif mode == "nt":
        a_spec = pl.BlockSpec((bm, bk), at(lambda i, j, kk: (i, kk)))
        b_spec = pl.BlockSpec((bn, bk), at(lambda i, j, kk: (j, kk)))
    else:
        a_spec = pl.BlockSpec((bk, bm), at(lambda i, j, kk: (kk, i)))
        b_spec = pl.BlockSpec((bk, bn), at(lambda i, j, kk: (kk, j)))
    o_spec = pl.BlockSpec((bm, bn), at(lambda i, j, kk: (i, j)))
    in_specs = [a_spec, b_spec] + ([o_spec] if has_add else [])
    args = (a, b) + ((add,) if has_add else ())
    grid = (n // bn, m // bm, nk) if n_outer else (m // bm, n // bn, nk)
    return pl.pallas_call(
        body, name=name, grid=grid, in_specs=in_specs, out_specs=o_spec,
        out_shape=jax.ShapeDtypeStruct((m, n), out_dtype),
        scratch_shapes=[pltpu.VMEM((bm, bn), F32)] if nk > 1 else [],
        compiler_params=_cparams(("parallel", "parallel", "arbitrary")),
    )(*args)


def _rows_call(fn, rows, consts, out_rows, out_accs=(), *, bs=256, name):
    s = rows[0].shape[0]
    bs = min(bs, s)
    assert s % bs == 0
    nr, nc, no, na = len(rows), len(consts), len(out_rows), len(out_accs)

    def body(*refs):
        vals = [r[...] for r in refs[:nr + nc]]
        outs = refs[nr + nc:]
        res = fn(*vals)
        if not isinstance(res, (tuple, list)):
            res = (res,)
        assert len(res) == no + na, (len(res), no, na)
        for r, v in zip(outs[:no], res[:no]):
            r[...] = v.astype(r.dtype)
        if na:
            i = pl.program_id(0)
            for r, v in zip(outs[no:], res[no:]):
                @pl.when(i == 0)
                def _(r=r, v=v):
                    r[...] = v

                @pl.when(i > 0)
                def _(r=r, v=v):
                    r[...] += v

    in_specs = [pl.BlockSpec((bs, a.shape[1]), lambda i: (i, 0)) for a in rows]
    in_specs += [pl.BlockSpec(c.shape, lambda i: (0, 0)) for c in consts]
    out_specs = [pl.BlockSpec((bs, c), lambda i: (i, 0)) for c, _ in out_rows]
    out_specs += [pl.BlockSpec(tuple(sh), lambda i: (0, 0)) for sh in out_accs]
    out_shape = [jax.ShapeDtypeStruct((s, c), dt) for c, dt in out_rows]
    out_shape += [jax.ShapeDtypeStruct(tuple(sh), F32) for sh in out_accs]
    res = pl.pallas_call(
        body, name=name, grid=(s // bs,), in_specs=in_specs, out_specs=out_specs, out_shape=out_shape,
        compiler_params=_cparams(("arbitrary",) if na else ("parallel",)),
    )(*rows, *consts)
    return res


def _rms(x, g):
    return x * lax.rsqrt(jnp.mean(x * x, axis=-1, keepdims=True) + NORM_EPS) * g


def _rms_bwd_math(x, g, dy):
    r = lax.rsqrt(jnp.mean(x * x, axis=-1, keepdims=True) + NORM_EPS)
    u = dy * g
    dx = r * u - x * (r * r * r) * jnp.mean(u * x, axis=-1, keepdims=True)
    dg = jnp.sum(dy * x * r, axis=0, keepdims=True)
    return dx, dg


def _rms_fwd(x, g, name):
    return _rows_call(lambda xv, gv: _rms(xv, gv), [x], [g], [(x.shape[1], BF16)], name=name)[0]


def _rms_bwd(x, g, dy, dres, name):
    def fn(xv, dyv, dresv, gv):
        dx, dg = _rms_bwd_math(xv, gv, dyv.astype(F32))
        return dx + dresv, dg
    return _rows_call(fn, [x, dy, dres], [g], [(x.shape[1], F32)], [(1, x.shape[1])], name=name)


def _chunkdot(x, m):
    outs = [jnp.dot(x[:, c:c + LANES], m, precision=lax.Precision.HIGHEST, preferred_element_type=F32)
            for c in range(0, x.shape[1], LANES)]
    return outs[0] if len(outs) == 1 else jnp.concatenate(outs, axis=1)


def _lanes(t, width):
    n = width // LANES
    return t if n == 1 else jnp.concatenate([t] * n, axis=1)


def _rope(x, cs, swap):
    w = x.shape[1]
    return x * _lanes(cs[:, :LANES], w) + _chunkdot(x, swap) * _lanes(cs[:, LANES:], w)


def _rope_t(dy, cs, swap):
    w = dy.shape[1]
    return dy * _lanes(cs[:, :LANES], w) + _chunkdot(dy * _lanes(cs[:, LANES:], w), swap)


def _swap_matrix():
    m = np.zeros((LANES, LANES), np.float32)
    for j in range(LANES):
        src = j + 16 if (j % 32) < 16 else j - 16
        m[src, j] = 1.0
    return jnp.asarray(m)


def _seg_matrix(seg):
    idx = np.arange(LANES) // seg
    return jnp.asarray((idx[:, None] == idx[None, :]).astype(np.float32))


def _rope_tables(s):
    pos = jnp.arange(s)

    def angles(p, dim):
        freqs = ROPE_THETA ** (-jnp.arange(0, dim, 2, dtype=F32) / dim)
        ang = p.astype(F32)[:, None] * freqs[None, :]
        return jnp.cos(ang), jnp.sin(ang)

    cos_t, sin_t = angles(pos, MLA_ROPE)
    one, zero = jnp.ones((s, 64), F32), jnp.zeros((s, 64), F32)
    mla = jnp.concatenate([one, cos_t, cos_t, one[:, :32], zero, -sin_t, sin_t, zero[:, :32]], axis=1)
    cos_r, sin_r = angles(pos // GRID_W, GQA_HEAD_DIM // 2)
    cos_c, sin_c = angles(pos % GRID_W, GQA_HEAD_DIM // 2)
    c64 = jnp.concatenate([cos_r, cos_r, cos_c, cos_c], axis=1)
    s64 = jnp.concatenate([-sin_r, sin_r, -sin_c, sin_c], axis=1)
    gqa = jnp.concatenate([c64, c64, s64, s64], axis=1)
    return mla, gqa


def _col_to_row(col):
    return jnp.transpose(jnp.broadcast_to(col, (col.shape[0], LANES)))[0:1, :]


def _stack_heads(ref, heads, d, dtype=None):
    parts = [ref[:, hd * d:(hd + 1) * d] for hd in heads]
    out = parts[0] if len(parts) == 1 else jnp.concatenate(parts, axis=0)
    return out if dtype is None else out.astype(dtype)


def _fill_v_ones(v_ref, va_ref, hb, dv):
    @pl.when(pl.program_id(1) == 0)
    def _():
        ones = jnp.ones((v_ref.shape[0], dv), BF16)
        for h in range(hb):
            va_ref[:, 2 * h * dv:(2 * h + 1) * dv] = v_ref[:, h * dv:(h + 1) * dv]
            va_ref[:, (2 * h + 1) * dv:(2 * h + 2) * dv] = ones


def _flash_fwd(q, k, v, *, R, dk, dv, hb, bq, bk, name):
    s = q.shape[0]
    g = k.shape[1] // dk
    ng = g // hb
    bq, bk = min(bq, s), min(bk, s)
    nq, nkb = s // bq, s // bk
    rb = R * bq

    def body(q_ref, k_ref, v_ref, o_ref, lse_ref, va_ref):
        _fill_v_ones(v_ref, va_ref, hb, dv)
        head_sets = [[h * R + r for r in range(R)] for h in range(hb)]
        qss = [_stack_heads(q_ref, heads, dk) for heads in head_sets]

        def step(jj, carry):
            carry = list(carry)
            for u in range(unroll):
                r0 = pl.multiple_of((jj * unroll + u) * bk, bk)
                for h in range(hb):
                    m, acc = carry[h]
                    kj = k_ref[pl.ds(r0, bk), h * dk:(h + 1) * dk]
                    va = va_ref[pl.ds(r0, bk), 2 * h * dv:2 * (h + 1) * dv]
                    sc = lax.dot_general(qss[h], kj, _DIMS["nt"], preferred_element_type=F32)
                    m2 = jnp.maximum(m, jnp.max(sc, axis=1, keepdims=True))
                    p = jnp.exp2(sc - m2).astype(BF16)
                    carry[h] = (m2, jnp.exp2(m - m2) * acc + jnp.dot(p, va, preferred_element_type=F32))
            return tuple(carry)

        unroll = 2 if nkb % 2 == 0 else 1
        init = tuple((jnp.full((rb, 1), NEG_INF, F32), jnp.zeros((rb, 2 * dv), F32)) for _ in range(hb))
        final = lax.fori_loop(0, nkb // unroll, step, init)
        for h, heads in enumerate(head_sets):
            m, acc = final[h]
            l = acc[:, dv:dv + 1]
            o = acc[:, :dv] / l
            row = _col_to_row(m + jnp.log2(l))
            for r, hd in enumerate(heads):
                o_ref[:, hd * dv:(hd + 1) * dv] = o[r * bq:(r + 1) * bq].astype(o_ref.dtype)
                lse_ref[0, hd:hd + 1, :] = row[:, r * bq:(r + 1) * bq]

    return pl.pallas_call(
        body, name=name, grid=(ng, nq),
        in_specs=[pl.BlockSpec((bq, hb * R * dk), lambda gi, i: (i, gi)),
                  pl.BlockSpec((s, hb * dk), lambda gi, i: (0, gi)),
                  pl.BlockSpec((s, hb * dv), lambda gi, i: (0, gi))],
        out_specs=[pl.BlockSpec((bq, hb * R * dv), lambda gi, i: (i, gi)),
                   pl.BlockSpec((1, hb * R, bq), lambda gi, i: (gi, 0, i))],
        out_shape=[jax.ShapeDtypeStruct((s, g * R * dv), BF16), jax.ShapeDtypeStruct((ng, hb * R, s), F32)],
        scratch_shapes=[pltpu.VMEM((s, 2 * hb * dv), BF16)],
        compiler_params=_cparams(("parallel", "arbitrary")),
    )(q, k, v)


def _flash_bwd(q, k, v, o, do, lse, *, R, dk, dv, hb, bq, bk, name):
    s = q.shape[0]
    g = k.shape[1] // dk
    ng = g // hb
    bq, bk = min(bq, s), min(bk, s)
    nq, nkb = s // bq, s // bk
    rb = R * bq

    def body(q_ref, k_ref, v_ref, o_ref, do_ref, lse_ref, dq_ref, dk_ref, dv_ref, va_ref):
        @pl.when(pl.program_id(1) == 0)
        def _():
            dk_ref[...] = jnp.zeros(dk_ref.shape, F32)
            dv_ref[...] = jnp.zeros(dv_ref.shape, F32)

        _fill_v_ones(v_ref, va_ref, hb, dv)
        lane = lax.broadcasted_iota(jnp.int32, (rb, dv), 1)
        head_sets = [[h * R + r for r in range(R)] for h in range(hb)]
        qss, doss, dosas, lrows = [], [], [], []
        for heads in head_sets:
            dos = _stack_heads(do_ref, heads, dv, BF16)
            delta = jnp.sum(dos.astype(F32) * _stack_heads(o_ref, heads, dv, F32), axis=1, keepdims=True)
            hi = delta.astype(BF16).astype(F32)
            lo = delta - hi
            qss.append(_stack_heads(q_ref, heads, dk))
            doss.append(dos)
            dosas.append(jnp.concatenate(
                [dos, jnp.where(lane == 0, -hi, jnp.where(lane == 1, -lo, 0.0)).astype(BF16)], axis=1))
            rows = [lse_ref[0, hd:hd + 1, :] for hd in heads]
            lrows.append(rows[0] if R == 1 else jnp.concatenate(rows, axis=1))

        def step(j, dqs):
            r0 = pl.multiple_of(j * bk, bk)
            out = []
            for h in range(hb):
                kj = k_ref[pl.ds(r0, bk), h * dk:(h + 1) * dk]
                va = va_ref[pl.ds(r0, bk), 2 * h * dv:2 * (h + 1) * dv]
                st = lax.dot_general(kj, qss[h], _DIMS["nt"], preferred_element_type=F32)
                pt = jnp.exp2(st - lrows[h])
                dv_ref[pl.ds(r0, bk), h * dv:(h + 1) * dv] += jnp.dot(pt.astype(BF16), doss[h],
                                                                      preferred_element_type=F32)
                dst = (pt * lax.dot_general(va, dosas[h], _DIMS["nt"], preferred_element_type=F32)).astype(BF16)
                dk_ref[pl.ds(r0, bk), h * dk:(h + 1) * dk] += jnp.dot(dst, qss[h], preferred_element_type=F32)
                out.append(dqs[h] + lax.dot_general(dst, kj, _DIMS["tn"], preferred_element_type=F32))
            return tuple(out)

        dqs = lax.fori_loop(0, nkb, step, tuple(jnp.zeros((rb, dk), F32) for _ in range(hb)))
        for h, heads in enumerate(head_sets):
            for r, hd in enumerate(heads):
                dq_ref[:, hd * dk:(hd + 1) * dk] = dqs[h][r * bq:(r + 1) * bq]

    qspec = pl.BlockSpec((bq, hb * R * dk), lambda gi, i: (i, gi))
    ospec = pl.BlockSpec((bq, hb * R * dv), lambda gi, i: (i, gi))
    kspec = pl.BlockSpec((s, hb * dk), lambda gi, i: (0, gi))
    vspec = pl.BlockSpec((s, hb * dv), lambda gi, i: (0, gi))
    return pl.pallas_call(
        body, name=name, grid=(ng, nq),
        in_specs=[qspec, kspec, vspec, ospec, ospec, pl.BlockSpec((1, hb * R, bq), lambda gi, i: (gi, 0, i))],
        out_specs=[qspec, kspec, vspec],
        out_shape=[jax.ShapeDtypeStruct((s, g * R * dk), F32), jax.ShapeDtypeStruct((s, g * dk), F32),
                   jax.ShapeDtypeStruct((s, g * dv), F32)],
        scratch_shapes=[pltpu.VMEM((s, 2 * hb * dv), BF16)],
        compiler_params=_cparams(("parallel", "arbitrary")),
    )(q, k, v, o, do, lse)


DIL_T = 1024
DIL_P = DIL_HALF
DIL_NCOL = IN_B // DIL_W


DIL_BATCH = 4


def _alibi_slope(head):
    return float(2.0 ** (-8.0 * (head + 1) / DIL_HEADS))


def _slot(sl_i):
    return slice(sl_i * DIL_HEAD_DIM, (sl_i + 1) * DIL_HEAD_DIM)


def _halo_specs(d, col, s, t):
    h = DIL_P * d
    per, last = t // h, s // h - 1
    return [pl.BlockSpec((h, DIL_W), lambda c: (jnp.maximum(c * per - 1, 0), col)),
            pl.BlockSpec((t, DIL_W), lambda c: (c, col)),
            pl.BlockSpec((h, DIL_W), lambda c: (jnp.minimum((c + 1) * per, last), col))]


def _staging(rows):
    return tuple(pltpu.VMEM((rows, LANES), F32) for _ in range(DIL_W // LANES))


def _stage(buf, refs):
    off = 0
    for r in refs:
        val = r[...].astype(F32)
        for j in range(DIL_W // LANES):
            buf[j][off:off + r.shape[0], :] = val[:, j * LANES:(j + 1) * LANES]
        off += r.shape[0]


def _unstage(buf, ref):
    ref[...] = jnp.concatenate([half[...] for half in buf], axis=1).astype(ref.dtype)


def _sub_tiles(d, t):
    return [(b * DIL_P * d + r, b * DIL_P) for b in range(t // (DIL_P * d)) for r in range(d)]


def _rows(start, size, d):
    return pl.ds(start, size, stride=d) if d > 1 else pl.ds(start, size)


def _strided(buf, start, size, d):
    return jnp.concatenate([half[_rows(start, size, d), :] for half in buf], axis=1)


def _put_strided(buf, start, d, val):
    for j in range(DIL_W // LANES):
        buf[j][_rows(start, val.shape[0], d), :] = val[:, j * LANES:(j + 1) * LANES]


def _band(u0, length, d, queries_wide):
    if queries_wide:
        shape = (3 * DIL_P, DIL_P)
        wide = u0 - DIL_P + lax.broadcasted_iota(jnp.int32, shape, 0)
        narrow = u0 + lax.broadcasted_iota(jnp.int32, shape, 1)
    else:
        shape = (DIL_P, 3 * DIL_P)
        narrow = u0 + lax.broadcasted_iota(jnp.int32, shape, 0)
        wide = u0 - DIL_P + lax.broadcasted_iota(jnp.int32, shape, 1)
    rel = jnp.abs(wide - narrow)
    valid = (rel <= DIL_HALF) & (wide >= 0) & (wide < length)
    return valid, rel.astype(F32) * float(d)


def _dil_fwd(zb, grp, name):
    s = zb.shape[0]
    d = DIL_PAIRS[grp][1]
    t = min(DIL_T, s)
    h = DIL_P * d
    scale = DIL_HEAD_DIM ** -0.5

    def body(q_ref, kp, kc, kn, vp, vc, vn, o_ref, lse_ref, qbuf, kbuf, vbuf, obuf, lbuf):
        _stage(qbuf, (q_ref,))
        _stage(kbuf, (kp, kc, kn))
        _stage(vbuf, (vp, vc, vn))
        u_step = pl.program_id(0) * (t // d)
        tiles = _sub_tiles(d, t)
        for g0 in range(0, len(tiles), DIL_BATCH):
            batch = tiles[g0:g0 + DIL_BATCH]
            masks = [_band(u_step + u, s // d, d, False) for _, u in batch]
            qs = [_strided(qbuf, row, DIL_P, d).astype(BF16) for row, _ in batch]
            ks = [_strided(kbuf, row, 3 * DIL_P, d).astype(BF16) for row, _ in batch]
            vs = [_strided(vbuf, row, 3 * DIL_P, d).astype(BF16) for row, _ in batch]
            chains = [(i, sl_i) for i in range(len(batch)) for sl_i in range(DIL_SLOTS)]
            scs = [lax.dot_general(qs[i][:, _slot(sl_i)], ks[i][:, _slot(sl_i)], _DIMS["nt"],
                                   preferred_element_type=F32) for i, sl_i in chains]
            scs = [jnp.where(masks[i][0], sc * scale - _alibi_slope(grp * DIL_SLOTS + sl_i) * masks[i][1], NEG_INF)
                   for (i, sl_i), sc in zip(chains, scs)]
            ms = [jnp.max(sc, axis=1, keepdims=True) for sc in scs]
            es = [jnp.exp(sc - m) for sc, m in zip(scs, ms)]
            dens = [jnp.sum(e, axis=1, keepdims=True) for e in es]
            outs = [jnp.dot((e / den).astype(BF16), vs[i][:, _slot(sl_i)], preferred_element_type=F32)
                    for (i, sl_i), e, den in zip(chains, es, dens)]
            lses = [jnp.broadcast_to(m + jnp.log(den), (DIL_P, DIL_HEAD_DIM)) for m, den in zip(ms, dens)]
            for i, (row, _) in enumerate(batch):
                pick = slice(i * DIL_SLOTS, (i + 1) * DIL_SLOTS)
                _put_strided(obuf, row, d, jnp.concatenate(outs[pick], axis=1))
                _put_strided(lbuf, row, d, jnp.concatenate(lses[pick], axis=1))
        _unstage(obuf, o_ref)
        _unstage(lbuf, lse_ref)

    own = pl.BlockSpec((t, DIL_W), lambda c: (c, 0))
    return pl.pallas_call(
        body, name=name, grid=(s // t,),
        in_specs=[pl.BlockSpec((t, DIL_W), lambda c: (c, grp))] + _halo_specs(d, 3 + grp, s, t)
        + _halo_specs(d, 6 + grp, s, t),
        out_specs=[own, own], out_shape=[jax.ShapeDtypeStruct((s, DIL_W), F32)] * 2,
        scratch_shapes=[_staging(t), _staging(t + 2 * h), _staging(t + 2 * h), _staging(t), _staging(t)],
        compiler_params=_cparams(("parallel",)),
    )(zb, zb, zb, zb, zb, zb, zb)


def _dil_bwd(zb, do, lse, dl, grp, name):
    s = zb.shape[0]
    d = DIL_PAIRS[grp][1]
    t = min(DIL_T, s)
    h = DIL_P * d
    scale = DIL_HEAD_DIM ** -0.5

    def chain_grads(qs, ks, vs, dos, lses, dls, masks):
        chains = [(i, sl_i) for i in range(len(qs)) for sl_i in range(DIL_SLOTS)]
        scs = [lax.dot_general(qs[i][:, _slot(sl_i)], ks[i][:, _slot(sl_i)], _DIMS["nt"],
                               preferred_element_type=F32) for i, sl_i in chains]
        dps = [lax.dot_general(dos[i][:, _slot(sl_i)], vs[i][:, _slot(sl_i)], _DIMS["nt"],
                               preferred_element_type=F32) for i, sl_i in chains]
        ps = [jnp.exp(jnp.where(masks[i][0], sc * scale - _alibi_slope(grp * DIL_SLOTS + sl_i) * masks[i][1],
                                NEG_INF) - lses[i][:, sl_i * DIL_HEAD_DIM:sl_i * DIL_HEAD_DIM + 1])
              for (i, sl_i), sc in zip(chains, scs)]
        dss = [(p * (dp - dls[i][:, sl_i * DIL_HEAD_DIM:sl_i * DIL_HEAD_DIM + 1]) * scale).astype(BF16)
               for (i, sl_i), p, dp in zip(chains, ps, dps)]
        return chains, ps, dss

    def dq_body(q_ref, kp, kc, kn, vp, vc, vn, do_ref, lse_ref, dl_ref, dq_ref, qbuf, kbuf, vbuf, dobuf, lsebuf,
                dlbuf, obuf):
        _stage(qbuf, (q_ref,))
        _stage(dobuf, (do_ref,))
        _stage(lsebuf, (lse_ref,))
        _stage(dlbuf, (dl_ref,))
        _stage(kbuf, (kp, kc, kn))
        _stage(vbuf, (vp, vc, vn))
        u_step = pl.program_id(0) * (t // d)
        tiles = _sub_tiles(d, t)
        for g0 in range(0, len(tiles), DIL_BATCH):
            batch = tiles[g0:g0 + DIL_BATCH]
            masks = [_band(u_step + u, s // d, d, False) for _, u in batch]
            narrow = [[_strided(b, row, DIL_P, d) for row, _ in batch] for b in (qbuf, dobuf, lsebuf, dlbuf)]
            ks = [_strided(kbuf, row, 3 * DIL_P, d).astype(BF16) for row, _ in batch]
            vs = [_strided(vbuf, row, 3 * DIL_P, d).astype(BF16) for row, _ in batch]
            chains, _, dss = chain_grads([a.astype(BF16) for a in narrow[0]], ks, vs,
                                         [a.astype(BF16) for a in narrow[1]], narrow[2], narrow[3], masks)
            outs = [jnp.dot(ds, ks[i][:, _slot(sl_i)], preferred_element_type=F32)
                    for (i, sl_i), ds in zip(chains, dss)]
            for i, (row, _) in enumerate(batch):
                _put_strided(obuf, row, d, jnp.concatenate(outs[i * DIL_SLOTS:(i + 1) * DIL_SLOTS], axis=1))
        _unstage(obuf, dq_ref)

    def dkv_body(k_ref, v_ref, qp, qc, qn, dop, doc, don, lp, lc, ln, dlp, dlc, dln, dk_ref, dv_ref,
                 kbuf, vbuf, qbuf, dobuf, lsebuf, dlbuf, dkbuf, dvbuf):
        _stage(kbuf, (k_ref,))
        _stage(vbuf, (v_ref,))
        _stage(qbuf, (qp, qc, qn))
        _stage(dobuf, (dop, doc, don))
        _stage(lsebuf, (lp, lc, ln))
        _stage(dlbuf, (dlp, dlc, dln))
        u_step = pl.program_id(0) * (t // d)
        tiles = _sub_tiles(d, t)
        for g0 in range(0, len(tiles), DIL_BATCH):
            batch = tiles[g0:g0 + DIL_BATCH]
            masks = [_band(u_step + u, s // d, d, True) for _, u in batch]
            ks = [_strided(kbuf, row, DIL_P, d).astype(BF16) for row, _ in batch]
            vs = [_strided(vbuf, row, DIL_P, d).astype(BF16) for row, _ in batch]
            wide_ = [[_strided(b, row, 3 * DIL_P, d) for row, _ in batch] for b in (qbuf, dobuf, lsebuf, dlbuf)]
            qs, dos = [a.astype(BF16) for a in wide_[0]], [a.astype(BF16) for a in wide_[1]]
            chains, ps, dss = chain_grads(qs, ks, vs, dos, wide_[2], wide_[3], masks)
            dvs = [lax.dot_general(p.astype(BF16), dos[i][:, _slot(sl_i)], _DIMS["tn"], preferred_element_type=F32)
                   for (i, sl_i), p in zip(chains, ps)]
            dks = [lax.dot_general(ds, qs[i][:, _slot(sl_i)], _DIMS["tn"], preferred_element_type=F32)
                   for (i, sl_i), ds in zip(chains, dss)]
            for i, (row, _) in enumerate(batch):
                pick = slice(i * DIL_SLOTS, (i + 1) * DIL_SLOTS)
                _put_strided(dkbuf, row, d, jnp.concatenate(dks[pick], axis=1))
                _put_strided(dvbuf, row, d, jnp.concatenate(dvs[pick], axis=1))
        _unstage(dkbuf, dk_ref)
        _unstage(dvbuf, dv_ref)

    def zcur(col):
        return pl.BlockSpec((t, DIL_W), lambda c: (c, col))

    own = pl.BlockSpec((t, DIL_W), lambda c: (c, 0))
    out = jax.ShapeDtypeStruct((s, DIL_W), F32)
    tile, wide = _staging(t), _staging(t + 2 * h)
    dq = pl.pallas_call(
        dq_body, name=name + "_dq", grid=(s // t,),
        in_specs=[zcur(grp)] + _halo_specs(d, 3 + grp, s, t) + _halo_specs(d, 6 + grp, s, t) + [own, own, own],
        out_specs=own, out_shape=out, scratch_shapes=[tile, wide, wide, tile, tile, tile, tile],
        compiler_params=_cparams(("parallel",)),
    )(zb, zb, zb, zb, zb, zb, zb, do, lse, dl)
    own3 = _halo_specs(d, 0, s, t)
    dk, dv = pl.pallas_call(
        dkv_body, name=name + "_dkv", grid=(s // t,),
        in_specs=[zcur(3 + grp), zcur(6 + grp)] + _halo_specs(d, grp, s, t) + own3 + own3 + own3,
        out_specs=[own, own], out_shape=[out, out],
        scratch_shapes=[tile, tile, wide, wide, wide, wide, tile, tile],
        compiler_params=_cparams(("parallel",)),
    )(zb, zb, zb, zb, zb, do, do, do, lse, lse, lse, dl, dl, dl)
    return dq, dk, dv


def _dil_combine(os_, ls_, name):
    def fn(o0, o1, o2, l0, l1, l2):
        m = jnp.maximum(jnp.maximum(l0, l1), l2)
        e0, e1, e2 = jnp.exp(l0 - m), jnp.exp(l1 - m), jnp.exp(l2 - m)
        den = e0 + e1 + e2
        comb = (e0 / den) * o0 + (e1 / den) * o1 + (e2 / den) * o2
        return comb, m + jnp.log(den)
    return _rows_call(fn, list(os_) + list(ls_), [], [(DIL_W, BF16), (DIL_W, F32)], name=name)


def _dil_combine_bwd(dcomb, os_, ls_, lt, seg64, name):
    def fn(dc, o0, o1, o2, l0, l1, l2, ltv, seg):
        w = [jnp.exp(l - ltv) for l in (l0, l1, l2)]
        comb = w[0] * o0 + w[1] * o1 + w[2] * o2
        t = _chunkdot(dc * comb, seg)
        return [wg * dc for wg in w] + [wg * t for wg in w]
    return _rows_call(fn, [dcomb] + list(os_) + list(ls_) + [lt], [seg64],
                      [(DIL_W, BF16)] * 3 + [(DIL_W, F32)] * 3, name=name)


def _mla_prep(za, gq, gkv, cs, swap, name):
    def fn(z, csv, gqv, gkvv, sw):
        return (_rms(z[:, :MLA_Q_RANK], gqv), _rms(z[:, MLA_Q_RANK:640], gkvv), _rope(z[:, 640:], csv, sw))
    return _rows_call(fn, [za, cs], [gq, gkv, swap], [(MLA_Q_RANK, BF16), (MLA_KV_RANK, BF16), (LANES, F32)],
                      name=name)


def _mla_prep_bwd(za, cs, dcq, dckv, dkr, gq, gkv, swap, name):
    def fn(z, csv, dcqv, dckvv, dkrv, gqv, gkvv, sw):
        d1, dg1 = _rms_bwd_math(z[:, :MLA_Q_RANK], gqv, dcqv)
        d2, dg2 = _rms_bwd_math(z[:, MLA_Q_RANK:640], gkvv, dckvv)
        d3 = _rope_t(dkrv, csv, sw)
        return jnp.concatenate([d1, d2, d3], axis=1), dg1, dg2
    return _rows_call(fn, [za, cs, dcq, dckv, dkr], [gq, gkv, swap], [(IN_A_PAD, BF16)],
                      [(1, MLA_Q_RANK), (1, MLA_KV_RANK)], name=name)


def _mla_qk(q_raw, k_pad, krr, cs, swap, name):
    w = MLA_HEADS * MLA_DK

    def fn(qv, kv, krv, csv, sw):
        return _rope(qv, csv, sw) * (MLA_SCALE * LOG2E), kv + _lanes(krv, w)
    return _rows_call(fn, [q_raw, k_pad, krr, cs], [swap], [(w, BF16), (w, BF16)], name=name)


def _mla_qk_bwd(dqh, dkh, cs, swap, name):
    w = MLA_HEADS * MLA_DK

    def fn(dq, dk, csv, sw):
        dk = dk * LN2
        acc = dk[:, :LANES]
        for h in range(1, MLA_HEADS):
            acc = acc + dk[:, h * LANES:(h + 1) * LANES]
        lane = lax.broadcasted_iota(jnp.int32, acc.shape, 1)
        acc = jnp.where((lane >= MLA_NOPE) & (lane < MLA_NOPE + MLA_ROPE), acc, 0.0)
        return _rope_t(dq * MLA_SCALE, csv, sw), dk, acc
    return _rows_call(fn, [dqh, dkh, cs], [swap], [(w, BF16), (w, BF16), (LANES, F32)], name=name)


def _head_norm(t, g2, seg):
    r = lax.rsqrt(_chunkdot(t * t, seg) * (1.0 / GQA_HEAD_DIM) + NORM_EPS)
    return t * r * _lanes(g2, t.shape[1]), r


def _head_norm_bwd(t, g2, seg, dn):
    w = t.shape[1]
    r = lax.rsqrt(_chunkdot(t * t, seg) * (1.0 / GQA_HEAD_DIM) + NORM_EPS)
    u = dn * _lanes(g2, w)
    dt = r * u - t * (r * r * r) * (_chunkdot(u * t, seg) * (1.0 / GQA_HEAD_DIM))
    dgw = jnp.sum(dn * t * r, axis=0, keepdims=True)
    dg = dgw[:, :LANES]
    for c in range(LANES, w, LANES):
        dg = dg + dgw[:, c:c + LANES]
    return dt, dg


def _gqa_prep(q_raw, kv_raw, cs, gq2, gk2, seg, swap, name):
    kw = GQA_KV_HEADS * GQA_HEAD_DIM

    def fn(qv, kvv, csv, gqv, gkv, sg, sw):
        qn, _ = _head_norm(qv, gqv, sg)
        kn, _ = _head_norm(kvv[:, :kw], gkv, sg)
        return _rope(qn, csv, sw) * (GQA_SCALE * LOG2E), _rope(kn, csv, sw), kvv[:, kw:]
    return _rows_call(fn, [q_raw, kv_raw, cs], [gq2, gk2, seg, swap],
                      [(GQA_HEADS * GQA_HEAD_DIM, BF16), (kw, BF16), (kw, BF16)], name=name)


def _gqa_prep_bwd(q_raw, kv_raw, cs, dqh, dkh, dv, gq2, gk2, seg, swap, name):
    kw = GQA_KV_HEADS * GQA_HEAD_DIM

    def fn(qv, kvv, csv, dq, dk, dvv, gqv, gkv, sg, sw):
        dqr, dgq = _head_norm_bwd(qv, gqv, sg, _rope_t(dq * GQA_SCALE, csv, sw))
        dkr, dgk = _head_norm_bwd(kvv[:, :kw], gkv, sg, _rope_t(dk * LN2, csv, sw))
        return dqr, jnp.concatenate([dkr, dvv], axis=1), dgq, dgk
    return _rows_call(fn, [q_raw, kv_raw, cs, dqh, dkh, dv], [gq2, gk2, seg, swap],
                      [(GQA_HEADS * GQA_HEAD_DIM, BF16), (2 * kw, BF16)], [(1, LANES), (1, LANES)], name=name)


def _swiglu(h, name):
    def fn(hv):
        gate, up = hv[:, :FFN_HIDDEN].astype(F32), hv[:, FFN_HIDDEN:].astype(F32)
        return gate / (1.0 + jnp.exp(-gate)) * up
    return _rows_call(fn, [h], [], [(FFN_HIDDEN, BF16)], name=name)[0]


def _swiglu_bwd(h, da, name):
    def fn(hv, dav):
        gate, up = hv[:, :FFN_HIDDEN].astype(F32), hv[:, FFN_HIDDEN:].astype(F32)
        dav = dav.astype(F32)
        sig = 1.0 / (1.0 + jnp.exp(-gate))
        silu = gate * sig
        dgate = dav * up * (sig + silu * (1.0 - sig))
        return jnp.concatenate([dgate, dav * silu], axis=1)
    return _rows_call(fn, [h, da], [], [(2 * FFN_HIDDEN, BF16)], name=name)[0]


def _loss_head(x, target, g, name):
    dm = x.shape[1]

    def fn(xv, tv, gv):
        err = _rms(xv, gv) - tv
        loss = 0.5 * jnp.sum(err * err) / dm
        dx, dg = _rms_bwd_math(xv, gv, err * (1.0 / dm))
        return dx, jnp.zeros((1, LANES), F32) + loss, dg
    return _rows_call(fn, [x, target], [g], [(dm, F32)], [(1, LANES), (1, dm)], name=name)


def _adamw(w, g, m, v, name):
    def fn(wv, gv, mv, vv):
        m2 = ADAM_B1 * mv + (1.0 - ADAM_B1) * gv
        v2 = ADAM_B2 * vv + (1.0 - ADAM_B2) * (gv * gv)
        m_hat = m2 / (1.0 - ADAM_B1 ** ADAM_STEP)
        v_hat = v2 / (1.0 - ADAM_B2 ** ADAM_STEP)
        return -ADAM_LR * (m_hat / (jnp.sqrt(v_hat) + ADAM_EPS) + ADAM_WD * wv), m2, v2
    c = w.shape[1]
    return _rows_call(fn, [w, g, m, v], [], [(c, F32)] * 3, bs=_pick(w.shape[0], 256, 8), name=name)


HBM_SPEC = pl.BlockSpec(memory_space=pltpu.HBM)
VMEM_SPEC = pl.BlockSpec(memory_space=pltpu.VMEM)


def _position():
    return lax.axis_index("x"), lax.axis_index("y"), lax.axis_index("c")


def _other_chips(x, y):
    return [(1 - x, y), (x, 1 - y), (1 - x, 1 - y)]


HALF_W = PACK_W // 2


def _cols(c):
    return pl.ds(pl.multiple_of(c * HALF_W, HALF_W), HALF_W)


def _all_gather_weights(packed):
    rows = packed.shape[0]

    def body(p_ref, g_ref, send_sems, recv_sems, local_sem):
        x, y, c = _position()
        chips = _other_chips(x, y)

        def half(chip, hc):
            return g_ref.at[2 * chip[0] + chip[1], :, _cols(hc)]

        def copy(j, src, dst, to):
            return pltpu.make_async_remote_copy(src_ref=src, dst_ref=dst, send_sem=send_sems.at[j],
                                                recv_sem=recv_sems.at[j], device_id=to, device_id_type=MESH)

        mine = pltpu.make_async_copy(p_ref, g_ref.at[2 * x + y], local_sem)
        mine.start()
        first = [copy(j, p_ref.at[:, _cols(c)], half((x, y), c), (*chip, c)) for j, chip in enumerate(chips)]
        for cp in first:
            cp.start()
        passed = [copy(3 + j, half(chip, c), half(chip, c), (x, y, 1 - c)) for j, chip in enumerate(chips)]
        for j, chip in enumerate(chips):
            copy(j, half(chip, c), half(chip, c), (x, y, c)).wait_recv()
            passed[j].start()
        for j, chip in enumerate(chips):
            copy(3 + j, half(chip, 1 - c), half(chip, 1 - c), (x, y, c)).wait_recv()
        for cp in first + passed:
            cp.wait_send()
        mine.wait()

    return pl.pallas_call(
        body, name="all_gather_weights", in_specs=[HBM_SPEC], out_specs=HBM_SPEC,
        out_shape=jax.ShapeDtypeStruct((4, rows, packed.shape[1]), packed.dtype),
        scratch_shapes=[pltpu.SemaphoreType.DMA((6,)), pltpu.SemaphoreType.DMA((6,)), pltpu.SemaphoreType.DMA],
    )(packed)


SEM_SPEC = pl.BlockSpec(memory_space=pltpu.SEMAPHORE)
ANY_SPEC = pl.BlockSpec(memory_space=pl.ANY)
DATAFLOW = pltpu.SideEffectType.DATAFLOW_SIDE_EFFECTING


def _hbm(a):
    return pltpu.with_memory_space_constraint(a, pltpu.HBM)


def _gather_start(packed, tag):
    rows = packed.shape[0]

    def body(p_ref, g_ref, send_sems, recv_sems, p_thru, g_thru, token):
        x, y, c = _position()
        for j, chip in enumerate(_other_chips(x, y)):
            for s in range(2):
                pltpu.make_async_remote_copy(
                    src_ref=p_ref.at[:, _cols(c)], dst_ref=g_ref.at[2 * x + y, :, _cols(c)],
                    send_sem=send_sems.at[2 * j + s], recv_sem=recv_sems.at[2 * j + s],
                    device_id=(*chip, 1 - c if s else c), device_id_type=MESH).start()
        token[...] = jnp.zeros_like(token)

    return pl.pallas_call(
        body, name=tag + "_start",
        out_shape=(pltpu.SemaphoreType.DMA((6,)), pltpu.SemaphoreType.DMA((6,)), pltpu.HBM(packed.shape, packed.dtype),
                   pltpu.HBM((4, rows, PACK_W), packed.dtype), jax.ShapeDtypeStruct((8, LANES), F32)),
        in_specs=(HBM_SPEC, HBM_SPEC), out_specs=(SEM_SPEC, SEM_SPEC, HBM_SPEC, HBM_SPEC, VMEM_SPEC),
        input_output_aliases={0: 2, 1: 3},
        compiler_params=pltpu.CompilerParams(has_side_effects=DATAFLOW),
    )(_hbm(packed), _hbm(lax.empty((4, rows, PACK_W), packed.dtype)))


def _gather_wait(send_sems, recv_sems, p_thru, g_thru, after, tag):
    def body(p_ref, g_ref, send_sems, recv_sems, after_ref, p_dead, got_ref):
        x, y, c = _position()
        for j, chip in enumerate(_other_chips(x, y)):
            for s in range(2):
                cp = pltpu.make_async_remote_copy(
                    src_ref=p_ref.at[:, _cols(c)], dst_ref=g_ref.at[2 * chip[0] + chip[1], :, _cols(1 - c if s else c)],
                    send_sem=send_sems.at[2 * j + s], recv_sem=recv_sems.at[2 * j + s],
                    device_id=(x, y, c), device_id_type=MESH)
                cp.wait_send()
                cp.wait_recv()

    return pl.pallas_call(
        body, name=tag + "_wait",
        out_shape=(pltpu.HBM(p_thru.shape, p_thru.dtype), pltpu.HBM(g_thru.shape, g_thru.dtype)),
        in_specs=(HBM_SPEC, HBM_SPEC, SEM_SPEC, SEM_SPEC, ANY_SPEC), out_specs=(HBM_SPEC, HBM_SPEC),
        input_output_aliases={0: 0, 1: 1},
        compiler_params=pltpu.CompilerParams(has_side_effects=DATAFLOW),
    )(p_thru, g_thru, send_sems, recv_sems, after)[1]


def _sibling_swap_halves(grads, tag):
    rows = grads.shape[1]

    def body(g_ref, a_ref, send_sem, recv_sem):
        x, y, c = _position()
        cp = pltpu.make_async_remote_copy(src_ref=g_ref.at[:, :, _cols(1 - c)], dst_ref=a_ref,
                                          send_sem=send_sem, recv_sem=recv_sem, device_id=(x, y, 1 - c),
                                          device_id_type=MESH)
        cp.start()
        cp.wait()

    return pl.pallas_call(
        body, name=tag + "_swap_cores", in_specs=[HBM_SPEC], out_specs=HBM_SPEC,
        out_shape=jax.ShapeDtypeStruct((4, rows, HALF_W), grads.dtype),
        scratch_shapes=[pltpu.SemaphoreType.DMA, pltpu.SemaphoreType.DMA],
    )(grads)


def _rs_block(rows):
    return max(d for d in range(16, 1601, 16) if rows % d == 0)


def _chip_sum(grads, other, c, tag):
    rows = other.shape[1]
    rb = _rs_block(rows)

    def body(c_ref, g_ref, a_ref, o_ref):
        o_ref[...] = (g_ref[...] + a_ref[...]).astype(o_ref.dtype)

    blk = (1, rb, HALF_W)
    return pl.pallas_call(
        body, name=tag + "_chip_sum",
        grid_spec=pltpu.PrefetchScalarGridSpec(
            num_scalar_prefetch=1, grid=(4, rows // rb),
            in_specs=[pl.BlockSpec(blk, lambda k, i, c_ref: (k, i, c_ref[0])),
                      pl.BlockSpec(blk, lambda k, i, c_ref: (k, i, 0))],
            out_specs=pl.BlockSpec(blk, lambda k, i, c_ref: (k, i, 0))),
        out_shape=jax.ShapeDtypeStruct(other.shape, BF16),
        compiler_params=_cparams(("parallel", "parallel")),
    )(jnp.reshape(c, (1,)).astype(jnp.int32), grads, other)


def _chip_copies(t_ref, b_ref, send_sems, recv_sems):
    x, y, c = _position()
    return [pltpu.make_async_remote_copy(src_ref=t_ref.at[2 * chip[0] + chip[1]], dst_ref=b_ref.at[j],
                                         send_sem=send_sems.at[j], recv_sem=recv_sems.at[j],
                                         device_id=(*chip, c), device_id_type=MESH)
            for j, chip in enumerate(_other_chips(x, y))]


def _send_chip_sums(sums, tag):
    def body(t_ref, b_ref, send_sems, recv_sems):
        copies = _chip_copies(t_ref, b_ref, send_sems, recv_sems)
        for cp in copies:
            cp.start()
        for cp in copies:
            cp.wait()

    return pl.pallas_call(
        body, name=tag + "_send_chips", in_specs=[HBM_SPEC], out_specs=HBM_SPEC,
        out_shape=jax.ShapeDtypeStruct((3,) + sums.shape[1:], sums.dtype),
        scratch_shapes=[pltpu.SemaphoreType.DMA((3,)), pltpu.SemaphoreType.DMA((3,))],
    )(sums)


def _send_chip_sums_start(sums, tag):
    land = (3,) + sums.shape[1:]

    def body(t_ref, b_ref, send_sems, recv_sems, t_thru, b_thru, token):
        for cp in _chip_copies(t_ref, b_ref, send_sems, recv_sems):
            cp.start()
        token[...] = jnp.zeros_like(token)

    return pl.pallas_call(
        body, name=tag + "_send_chips_start",
        out_shape=(pltpu.SemaphoreType.DMA((3,)), pltpu.SemaphoreType.DMA((3,)), pltpu.HBM(sums.shape, sums.dtype),
                   pltpu.HBM(land, sums.dtype), jax.ShapeDtypeStruct((8, LANES), F32)),
        in_specs=(HBM_SPEC, HBM_SPEC), out_specs=(SEM_SPEC, SEM_SPEC, HBM_SPEC, HBM_SPEC, VMEM_SPEC),
        input_output_aliases={0: 2, 1: 3},
        compiler_params=pltpu.CompilerParams(has_side_effects=DATAFLOW),
    )(_hbm(sums), _hbm(lax.empty(land, sums.dtype)))


def _send_chip_sums_wait(send_sems, recv_sems, t_thru, b_thru, after, tag):
    def body(t_ref, b_ref, send_sems, recv_sems, after_ref, t_dead, got_ref):
        for cp in _chip_copies(t_ref, b_ref, send_sems, recv_sems):
            cp.wait_send()
            cp.wait_recv()

    return pl.pallas_call(
        body, name=tag + "_send_chips_wait",
        out_shape=(pltpu.HBM(t_thru.shape, t_thru.dtype), pltpu.HBM(b_thru.shape, b_thru.dtype)),
        in_specs=(HBM_SPEC, HBM_SPEC, SEM_SPEC, SEM_SPEC, ANY_SPEC), out_specs=(HBM_SPEC, HBM_SPEC),
        input_output_aliases={0: 0, 1: 1},
        compiler_params=pltpu.CompilerParams(has_side_effects=DATAFLOW),
    )(t_thru, b_thru, send_sems, recv_sems, after)[1]


def _final_sum(grads, other, recv, k, c, tag):
    rows = other.shape[1]
    rb = _rs_block(rows)

    def body(k_ref, c_ref, g_ref, a_ref, b_ref, o_ref):
        own = g_ref[0] + a_ref[0]
        o_ref[...] = ((own + b_ref[0].astype(F32)) + b_ref[1].astype(F32)) + b_ref[2].astype(F32)

    return pl.pallas_call(
        body, name=tag + "_final_sum",
        grid_spec=pltpu.PrefetchScalarGridSpec(
            num_scalar_prefetch=2, grid=(rows // rb,),
            in_specs=[pl.BlockSpec((1, rb, HALF_W), lambda i, k_ref, c_ref: (k_ref[0], i, c_ref[0])),
                      pl.BlockSpec((1, rb, HALF_W), lambda i, k_ref, c_ref: (k_ref[0], i, 0)),
                      pl.BlockSpec((3, rb, HALF_W), lambda i, k_ref, c_ref: (0, i, 0))],
            out_specs=pl.BlockSpec((rb, HALF_W), lambda i, k_ref, c_ref: (i, 0))),
        out_shape=jax.ShapeDtypeStruct((rows, HALF_W), F32),
        compiler_params=_cparams(("parallel",)),
    )(jnp.reshape(k, (1,)).astype(jnp.int32), jnp.reshape(c, (1,)).astype(jnp.int32), grads, other, recv)


def _join_halves(half, core, tag):
    def body(h_ref, o_ref, send_sem, recv_sem):
        x, y, c = _position()
        cp = pltpu.make_async_remote_copy(src_ref=h_ref, dst_ref=o_ref, send_sem=send_sem, recv_sem=recv_sem,
                                          device_id=(x, y, 1 - c), device_id_type=MESH)
        cp.start()
        cp.wait()

    other = pl.pallas_call(
        body, name=tag + "_join_cores", in_specs=[HBM_SPEC], out_specs=HBM_SPEC,
        out_shape=jax.ShapeDtypeStruct(half.shape, half.dtype),
        scratch_shapes=[pltpu.SemaphoreType.DMA, pltpu.SemaphoreType.DMA],
    )(half)
    first = core == 0
    return jnp.concatenate([jnp.where(first, half, other), jnp.where(first, other, half)], axis=1)


def _all_reduce_packet(packet):
    rows = packet.shape[0]

    def body(p_ref, o_ref, buf, send_sems, recv_sems):
        x, y, c = _position()
        me = 4 * x + 2 * y + c
        buf[me] = p_ref[...]

        def flip(v, bit):
            return 1 - v if bit else v

        for p in range(1, 8):
            peer = (flip(x, p & 4), flip(y, p & 2), flip(c, p & 1))
            pltpu.make_async_remote_copy(src_ref=p_ref, dst_ref=buf.at[me], send_sem=send_sems.at[p - 1],
                                         recv_sem=recv_sems.at[p - 1], device_id=peer, device_id_type=MESH).start()
        for p in range(1, 8):
            peer = (flip(x, p & 4), flip(y, p & 2), flip(c, p & 1))
            slot = 4 * peer[0] + 2 * peer[1] + peer[2]
            cp = pltpu.make_async_remote_copy(src_ref=p_ref, dst_ref=buf.at[slot], send_sem=send_sems.at[p - 1],
                                              recv_sem=recv_sems.at[p - 1], device_id=peer, device_id_type=MESH)
            cp.wait_recv()
            cp.wait_send()
        acc = buf[0]
        for dev in range(1, 8):
            acc = acc + buf[dev]
        o_ref[...] = acc

    return pl.pallas_call(
        body, name="all_reduce_packet", in_specs=[VMEM_SPEC], out_specs=VMEM_SPEC,
        out_shape=jax.ShapeDtypeStruct(packet.shape, F32),
        scratch_shapes=[pltpu.VMEM((8, rows, LANES), F32), pltpu.SemaphoreType.DMA((7,)),
                        pltpu.SemaphoreType.DMA((7,))],
    )(packet)


def _stack_range(name, n_stack, pack):
    if name.startswith('gqa'):
        return (0, 0) if pack == 0 else (0, n_stack)
    return (0, 1) if pack == 0 else (1, n_stack)


def _pack_members(pack):
    out = []
    for n, shape, ax in BIG:
        lo, hi = _stack_range(n, shape[0], pack)
        if hi > lo:
            out.append((n, (hi - lo,) + shape[1:], ax, (lo, hi)))
    return out


PACK_ROW_MULTIPLE = 512


def _pad_rows(parts, dtype):
    rows = sum(p.shape[0] for p in parts)
    pad = -rows % PACK_ROW_MULTIPLE
    return jnp.concatenate(parts + ([jnp.zeros((pad, PACK_W), dtype)] if pad else []), axis=0)


def _pack_blocks(blocks, dtype, pack):
    return _pad_rows([blocks[n][lo:hi].astype(dtype).reshape(-1, PACK_W)
                      for n, _, _, (lo, hi) in _pack_members(pack)], dtype)


def _unpack_blocks(packed, pack):
    out, off = {}, 0
    for n, shape, _, _ in _pack_members(pack):
        r = math.prod(shape) // PACK_W
        out[n] = packed[off:off + r].reshape(shape)
        off += r
    return out


def _unpack_gathered(gathered, pack):
    per_chip = [_unpack_blocks(gathered[k], pack) for k in range(4)]
    return {n: jnp.concatenate([per_chip[k][n] for k in range(4)], axis=ax) for n, _, ax, _ in _pack_members(pack)}


def _pack_full(full, dtype, pack):
    chips = []
    for k in range(4):
        parts = []
        for n, shape, ax, _ in _pack_members(pack):
            blk = lax.slice_in_dim(full[n], k * shape[ax], (k + 1) * shape[ax], axis=ax)
            parts.append(blk.astype(dtype).reshape(-1, PACK_W))
        chips.append(_pad_rows(parts, dtype))
    return jnp.stack(chips, axis=0)


def _pack_small(vals, loss_row):
    rows = [loss_row.reshape(1, LANES)]
    for n, shape in SMALL:
        v = vals.get(n)
        v = jnp.zeros(shape, F32) if v is None else v
        rows.append(v.astype(F32).reshape(-1, LANES))
    packet = jnp.concatenate(rows, axis=0)
    return jnp.pad(packet, ((0, PACKET_ROWS - packet.shape[0]), (0, 0)))


def _unpack_small(packet):
    out, off = {}, 1
    for n, shape in SMALL:
        r = math.prod(shape) // LANES
        out[n] = packet[off:off + r].reshape(shape)
        off += r
    return packet[0, 0], out


_MLA = dict(R=1, dk=MLA_DK, dv=MLA_V, hb=2, bq=512, bk=512)
_MLA_FWD_BQ = 1024
_GQA = dict(R=GQA_HEADS // GQA_KV_HEADS, dk=GQA_HEAD_DIM, dv=GQA_HEAD_DIM, hb=2, bq=256, bk=512)


def _layer_params(layer, full, gains):
    pack = 0 if layer == 0 else 1
    i = layer // 2

    def mat(name):
        lo, _ = _stack_range(name, 4 if name.startswith('ffn') else 2, pack)
        return full[name][(layer if name.startswith('ffn') else i) - lo]

    p = dict(ffn_norm=gains['ffn_norm'][layer][None], ffn_w_in=mat('ffn_w_in'), ffn_w_out=mat('ffn_w_out'))
    if layer % 2 == 0:
        w_in = mat('w_in_ab')
        zeros = jnp.zeros((D_MODEL, 32), w_in.dtype)
        p['w_a'] = jnp.concatenate([w_in[:, :640], zeros, zeros, w_in[:, 640:IN_A], zeros], axis=1)
        p['w_b'] = w_in[:, IN_A:]
        p['w_uq'] = jnp.pad(mat('mla_w_uq'), ((0, 0), (0, 0), (0, MLA_DK - 96))).reshape(MLA_Q_RANK, -1)
        ukv = mat('mla_w_ukv')
        p['w_uk'] = jnp.pad(ukv[:, :, :MLA_NOPE], ((0, 0), (0, 0), (0, MLA_DK - MLA_NOPE))).reshape(MLA_KV_RANK, -1)
        p['w_uv'] = ukv[:, :, MLA_NOPE:].reshape(MLA_KV_RANK, -1)
        p['w_out'] = mat('w_out_ab')
        p['mix_norm'] = gains['mix_norm_ab'][i][None]
        p['q_norm'] = gains['mla_q_norm'][i][None]
        p['kv_norm'] = gains['mla_kv_norm'][i][None]
    else:
        p['w_q'], p['w_kv'], p['w_o'] = mat('gqa_w_q'), mat('gqa_w_kv'), mat('gqa_w_o')
        p['mix_norm'] = gains['mix_norm_c'][i][None]
        p['q_norm'] = jnp.tile(gains['gqa_q_norm'][i][None], (1, 2))
        p['k_norm'] = jnp.tile(gains['gqa_k_norm'][i][None], (1, 2))
    return p


def _even_fwd(x, p, cs, swap, tag):
    xn = _rms_fwd(x, p['mix_norm'], tag + "_norm")
    za = _mm(xn, p['w_a'], name=tag + "_in_a")
    zb = _mm(xn, p['w_b'], out_dtype=BF16, name=tag + "_in_b")
    cq, ckv, krr = _mla_prep(za, p['q_norm'], p['kv_norm'], cs, swap, tag + "_mla_prep")
    q_raw = _mm(cq, p['w_uq'], name=tag + "_uq")
    k_pad = _mm(ckv, p['w_uk'], name=tag + "_uk")
    v = _mm(ckv, p['w_uv'], out_dtype=BF16, name=tag + "_uv")
    qh, kh = _mla_qk(q_raw, k_pad, krr, cs, swap, tag + "_mla_qk")
    o_a, lse_a = _flash_fwd(qh, kh, v, name=tag + "_mla_attn", **dict(_MLA, bq=_MLA_FWD_BQ))
    og, lg = [], []
    for grp in range(DIL_GROUPS):
        o, l = _dil_fwd(zb, grp, f"{tag}_dil{grp}")
        og.append(o)
        lg.append(l)
    o_b, lt = _dil_combine(og, lg, tag + "_dil_merge")
    ocat = jnp.concatenate([o_a, o_b], axis=1)
    x1 = _mm(ocat, p['w_out'], add=x, name=tag + "_out")
    saved = dict(x=x, xn=xn, za=za, zb=zb, cq=cq, ckv=ckv, qh=qh, kh=kh, v=v, lse_a=lse_a, og=og, lg=lg, lt=lt,
                 ocat=ocat)
    return x1, saved


def _even_bwd(dx1, p, sv, cs, swap, seg64, tag):
    docat = _mm(dx1, p['w_out'], mode="nt", name=tag + "_out_dx")
    d_w_out = _mm(sv['ocat'], dx1, mode="tn", name=tag + "_out_dw")
    n_a = MLA_HEADS * MLA_V
    do_a = docat[:, :n_a].astype(BF16)
    res = _dil_combine_bwd(docat[:, n_a:], sv['og'], sv['lg'], sv['lt'], seg64, tag + "_dil_merge_bwd")
    dqs, dks, dvs = [], [], []
    for grp in range(DIL_GROUPS):
        dq, dk, dv = _dil_bwd(sv['zb'], res[grp], sv['lg'][grp], res[3 + grp], grp, f"{tag}_dil{grp}_bwd")
        dqs.append(dq)
        dks.append(dk)
        dvs.append(dv)
    dzb = jnp.concatenate(dqs + dks + dvs, axis=1).astype(BF16)
    dqh, dkh, dv = _flash_bwd(sv['qh'], sv['kh'], sv['v'], sv['ocat'][:, :n_a], do_a, sv['lse_a'],
                              name=tag + "_mla_attn_bwd", **_MLA)
    dq_raw, dkh, dkrr = _mla_qk_bwd(dqh, dkh, cs, swap, tag + "_mla_qk_bwd")
    dcq = _mm(dq_raw, p['w_uq'], mode="nt", name=tag + "_uq_dx")
    d_w_uq = _mm(sv['cq'], dq_raw, mode="tn", name=tag + "_uq_dw")
    dckv = _mm(dkh, p['w_uk'], mode="nt", name=tag + "_uk_dx")
    dckv = _mm(dv, p['w_uv'], mode="nt", add=dckv, name=tag + "_uv_dx")
    d_w_uk = _mm(sv['ckv'], dkh, mode="tn", name=tag + "_uk_dw")
    d_w_uv = _mm(sv['ckv'], dv, mode="tn", name=tag + "_uv_dw")
    dza, d_gq, d_gkv = _mla_prep_bwd(sv['za'], cs, dcq, dckv, dkrr, p['q_norm'], p['kv_norm'], swap,
                                     tag + "_mla_prep_bwd")
    dxn = _mm(dza, p['w_a'], mode="nt", name=tag + "_in_a_dx")
    dxn = _mm(dzb, p['w_b'], mode="nt", add=dxn, name=tag + "_in_b_dx")
    d_w_a = _mm(sv['xn'], dza, mode="tn", name=tag + "_in_a_dw")
    d_w_b = _mm(sv['xn'], dzb, mode="tn", name=tag + "_in_b_dw")
    dx, d_g = _rms_bwd(sv['x'], p['mix_norm'], dxn, dx1, tag + "_norm_bwd")
    d_w_in = jnp.concatenate([d_w_a[:, :640], d_w_a[:, 704:736], d_w_b], axis=1)
    d_uq = d_w_uq.reshape(MLA_Q_RANK, MLA_HEADS, MLA_DK)[:, :, :MLA_NOPE + MLA_ROPE]
    d_ukv = jnp.concatenate([d_w_uk.reshape(MLA_KV_RANK, MLA_HEADS, MLA_DK)[:, :, :MLA_NOPE],
                             d_w_uv.reshape(MLA_KV_RANK, MLA_HEADS, MLA_V)], axis=2)
    grads = dict(w_in_ab=d_w_in, mla_w_uq=d_uq, mla_w_ukv=d_ukv, w_out_ab=d_w_out, mix_norm_ab=d_g[0],
                 mla_q_norm=d_gq[0], mla_kv_norm=d_gkv[0])
    return dx, grads


def _odd_fwd(x, p, cs, seg64, swap, tag):
    xn = _rms_fwd(x, p['mix_norm'], tag + "_norm")
    q_raw = _mm(xn, p['w_q'], name=tag + "_q")
    kv_raw = _mm(xn, p['w_kv'], name=tag + "_kv")
    qh, kh, v = _gqa_prep(q_raw, kv_raw, cs, p['q_norm'], p['k_norm'], seg64, swap, tag + "_gqa_prep")
    o, lse = _flash_fwd(qh, kh, v, name=tag + "_gqa_attn", **_GQA)
    x1 = _mm(o, p['w_o'], add=x, name=tag + "_o")
    return x1, dict(x=x, xn=xn, q_raw=q_raw, kv_raw=kv_raw, qh=qh, kh=kh, v=v, o=o, lse=lse)


def _odd_bwd(dx1, p, sv, cs, seg64, swap, tag):
    do = _mm(dx1, p['w_o'], mode="nt", out_dtype=BF16, name=tag + "_o_dx")
    d_w_o = _mm(sv['o'], dx1, mode="tn", name=tag + "_o_dw")
    dqh, dkh, dv = _flash_bwd(sv['qh'], sv['kh'], sv['v'], sv['o'], do, sv['lse'], name=tag + "_gqa_attn_bwd",
                              **_GQA)
    dq_raw, dkv_raw, d_gq, d_gk = _gqa_prep_bwd(sv['q_raw'], sv['kv_raw'], cs, dqh, dkh, dv, p['q_norm'],
                                                p['k_norm'], seg64, swap, tag + "_gqa_prep_bwd")
    dxn = _mm(dq_raw, p['w_q'], mode="nt", name=tag + "_q_dx")
    dxn = _mm(dkv_raw, p['w_kv'], mode="nt", add=dxn, name=tag + "_kv_dx")
    d_w_q = _mm(sv['xn'], dq_raw, mode="tn", name=tag + "_q_dw")
    d_w_kv = _mm(sv['xn'], dkv_raw, mode="tn", name=tag + "_kv_dw")
    dx, d_g = _rms_bwd(sv['x'], p['mix_norm'], dxn, dx1, tag + "_norm_bwd")
    grads = dict(gqa_w_q=d_w_q, gqa_w_kv=d_w_kv, gqa_w_o=d_w_o, mix_norm_c=d_g[0],
                 gqa_q_norm=d_gq[0, :GQA_HEAD_DIM] + d_gq[0, GQA_HEAD_DIM:],
                 gqa_k_norm=d_gk[0, :GQA_HEAD_DIM] + d_gk[0, GQA_HEAD_DIM:])
    return dx, grads


def _ffn_fwd(x, p, tag):
    xn = _rms_fwd(x, p['ffn_norm'], tag + "_ffn_norm")
    h = _mm(xn, p['ffn_w_in'], out_dtype=BF16, name=tag + "_ffn_in")
    a = _swiglu(h, tag + "_swiglu")
    x2 = _mm(a, p['ffn_w_out'], add=x, name=tag + "_ffn_out")
    return x2, dict(x=x, xn=xn, h=h, a=a)


def _ffn_bwd(dx2, p, sv, tag):
    da = _mm(dx2, p['ffn_w_out'], mode="nt", out_dtype=BF16, name=tag + "_ffn_out_dx")
    d_w_out = _mm(sv['a'], dx2, mode="tn", name=tag + "_ffn_out_dw")
    dh = _swiglu_bwd(sv['h'], da, tag + "_swiglu_bwd")
    d_w_in = _mm(sv['xn'], dh, mode="tn", name=tag + "_ffn_in_dw")
    dxn = _mm(dh, p['ffn_w_in'], mode="nt", name=tag + "_ffn_in_dx")
    dx, d_g = _rms_bwd(sv['x'], p['ffn_norm'], dxn, dx2, tag + "_ffn_norm_bwd")
    return dx, d_w_in, d_w_out, d_g[0]


EVEN_MATS = ('w_in_ab', 'mla_w_uq', 'mla_w_ukv', 'w_out_ab')
ODD_MATS = ('gqa_w_q', 'gqa_w_kv', 'gqa_w_o')
FFN_MATS = ('ffn_w_in', 'ffn_w_out')


def _schedule(x, target, gains, full_of_pack, rest_grads_ready):
    s = x.shape[0]
    cs_mla, cs_gqa = _rope_tables(s)
    swap, seg64 = _swap_matrix(), _seg_matrix(GQA_HEAD_DIM)
    params, saved, full = [], [], None
    for layer in range(4):
        tag = f"l{layer}"
        if layer < 2:
            full = full_of_pack(layer, x)
        p = _layer_params(layer, full, gains)
        if layer % 2 == 0:
            x, sv = _even_fwd(x, p, cs_mla, swap, tag)
        else:
            x, sv = _odd_fwd(x, p, cs_gqa, seg64, swap, tag)
        x, sv_f = _ffn_fwd(x, p, tag)
        params.append(p)
        saved.append((sv, sv_f))
    dx, loss_row, d_final = _loss_head(x, target, gains['final_norm'][None], "loss_head")

    per_layer, rest = {}, None
    for layer in reversed(range(4)):
        p, (sv, sv_f), tag = params[layer], saved[layer], f"l{layer}"
        if layer == 0:
            rest = {n: per_layer[2][n][None] for n in EVEN_MATS}
            rest.update({n: jnp.stack([per_layer[1][n], per_layer[3][n]], axis=0) for n in ODD_MATS})
            rest.update({n: jnp.stack([per_layer[l][n] for l in (1, 2, 3)], axis=0) for n in FFN_MATS})
            token = rest_grads_ready(rest)
            if token is not None:
                p = dict(p, ffn_w_out=p['ffn_w_out'] + token[0, 0].astype(p['ffn_w_out'].dtype))
        dx, d_ffn_in, d_ffn_out, d_ffn_g = _ffn_bwd(dx, p, sv_f, tag)
        if layer % 2 == 0:
            dx, g = _even_bwd(dx, p, sv, cs_mla, swap, seg64, tag)
        else:
            dx, g = _odd_bwd(dx, p, sv, cs_gqa, seg64, swap, tag)
        g.update(ffn_w_in=d_ffn_in, ffn_w_out=d_ffn_out, ffn_norm=d_ffn_g)
        per_layer[layer] = g

    first = {n: per_layer[0][n][None] for n in EVEN_MATS + FFN_MATS}
    small = {'final_norm': d_final[0], 'ffn_norm': jnp.stack([per_layer[l]['ffn_norm'] for l in range(4)], axis=0)}
    for n in ('mix_norm_ab', 'mla_q_norm', 'mla_kv_norm'):
        small[n] = jnp.stack([per_layer[0][n], per_layer[2][n]], axis=0)
    for n in ('mix_norm_c', 'gqa_q_norm', 'gqa_k_norm'):
        small[n] = jnp.stack([per_layer[1][n], per_layer[3][n]], axis=0)
    return loss_row, dx, first, rest, small


def _core_sums(grads, pack, core, tag):
    packed = _pack_full(grads, F32, pack)
    other = _sibling_swap_halves(packed, tag)
    return packed, other, _chip_sum(packed, other, core, tag)


def _finish_reduce_scatter(packed, other, recv, pack, chip, core, tag):
    return _unpack_blocks(_join_halves(_final_sum(packed, other, recv, chip, core, tag), core, tag), pack)


def _step(x, target, w, m, v):
    big_names = [n for n, _, _ in BIG]
    chip = 2 * lax.axis_index("x") + lax.axis_index("y")
    core = lax.axis_index("c")

    gains = {n: w[n] for n, _ in SMALL if n != 'mix_norm_c'}
    c_cols = w['mix_norm_c'].shape[1]
    own_c = lax.dynamic_update_slice(jnp.zeros((2, 4 * c_cols), F32), w['mix_norm_c'], (0, chip * c_cols))
    gains['mix_norm_c'] = _unpack_small(_all_reduce_packet(_pack_small(
        {'mix_norm_c': own_c * 0.5}, jnp.zeros((LANES,), F32))))[1]['mix_norm_c']

    gathered0 = _all_gather_weights(_pack_blocks(w, BF16, 0))
    packed1, gathered0 = lax.optimization_barrier((_pack_blocks(w, BF16, 1), gathered0))
    ag_send, ag_recv, p_thru, g_thru, ag_token = _gather_start(packed1, "gather_rest")
    gains['mix_norm_ab'] = gains['mix_norm_ab'] + ag_token[0, 0]
    full0 = _unpack_gathered(gathered0, 0)

    def full_of_pack(pack, after):
        if pack == 0:
            return full0
        landed = _gather_wait(ag_send, ag_recv, p_thru, g_thru, after, "gather_rest")
        return _unpack_gathered(lax.dynamic_update_slice(landed, packed1[None], (chip, 0, 0)), 1)

    rs = {}

    def rest_grads_ready(rest):
        rs['packed'], rs['other'], sums = _core_sums(rest, 1, core, "grad_rest")
        rs['send'], rs['recv'], rs['t'], rs['b'], token = _send_chip_sums_start(sums, "grad_rest")
        return token

    loss_row, dx, first, rest, small = _schedule(x[0], target[0], gains, full_of_pack, rest_grads_ready)
    recv1 = _send_chip_sums_wait(rs['send'], rs['recv'], rs['t'], rs['b'], dx, "grad_rest")
    g_rest = _finish_reduce_scatter(rs['packed'], rs['other'], recv1, 1, chip, core, "grad_rest")
    packed0, other0, sums0 = _core_sums(first, 0, core, "grad_first")
    g_first = _finish_reduce_scatter(packed0, other0, _send_chip_sums(sums0, "grad_first"), 0, chip, core,
                                     "grad_first")
    g_blocks = {n: (jnp.concatenate([g_first[n], g_rest[n]], axis=0) if n in g_first else g_rest[n])
                for n in big_names}

    loss, g_small = _unpack_small(_all_reduce_packet(_pack_small(small, loss_row[0])))
    g_small['mix_norm_c'] = lax.dynamic_slice(g_small['mix_norm_c'], (0, chip * c_cols), (2, c_cols))

    out_g, out_d, out_m, out_v = {}, {}, {}, {}
    for n in big_names:
        shape = w[n].shape
        cols = shape[-1]
        d_, m_, v_ = _adamw(w[n].reshape(-1, cols), g_blocks[n].reshape(-1, cols), m[n].reshape(-1, cols),
                            v[n].reshape(-1, cols), "adamw_" + n)
        out_g[n], out_d[n], out_m[n], out_v[n] = g_blocks[n], d_.reshape(shape), m_.reshape(shape), v_.reshape(shape)
    for n, _ in SMALL:
        shape = w[n].shape
        as2d = (lambda t: t.reshape(1, -1)) if len(shape) == 1 else (lambda t: t)
        d_, m_, v_ = _adamw(as2d(w[n]), as2d(g_small[n]), as2d(m[n]), as2d(v[n]), "adamw_" + n)
        out_g[n], out_d[n], out_m[n], out_v[n] = g_small[n], d_.reshape(shape), m_.reshape(shape), v_.reshape(shape)
    return (loss, dx[None], *[out_g[n] for n in WEIGHTS], *[out_d[n] for n in WEIGHTS],
            *[out_m[n] for n in WEIGHTS], *[out_v[n] for n in WEIGHTS])


def kernel(x, mix_norm_ab, w_in_ab, mla_q_norm, mla_kv_norm, mla_w_uq, mla_w_ukv, w_out_ab, mix_norm_c, gqa_w_q, gqa_w_kv, gqa_q_norm, gqa_k_norm, gqa_w_o, ffn_norm, ffn_w_in, ffn_w_out, final_norm, loss_target, m_mix_norm_ab, m_w_in_ab, m_mla_q_norm, m_mla_kv_norm, m_mla_w_uq, m_mla_w_ukv, m_w_out_ab, m_mix_norm_c, m_gqa_w_q, m_gqa_w_kv, m_gqa_q_norm, m_gqa_k_norm, m_gqa_w_o, m_ffn_norm, m_ffn_w_in, m_ffn_w_out, m_final_norm, v_mix_norm_ab, v_w_in_ab, v_mla_q_norm, v_mla_kv_norm, v_mla_w_uq, v_mla_w_ukv, v_w_out_ab, v_mix_norm_c, v_gqa_w_q, v_gqa_w_kv, v_gqa_q_norm, v_gqa_k_norm, v_gqa_w_o, v_ffn_norm, v_ffn_w_in, v_ffn_w_out, v_final_norm):
    w = dict(zip(WEIGHTS, (mix_norm_ab, w_in_ab, mla_q_norm, mla_kv_norm, mla_w_uq, mla_w_ukv, w_out_ab, mix_norm_c,
                           gqa_w_q, gqa_w_kv, gqa_q_norm, gqa_k_norm, gqa_w_o, ffn_norm, ffn_w_in, ffn_w_out,
                           final_norm)))
    m = dict(zip(WEIGHTS, (m_mix_norm_ab, m_w_in_ab, m_mla_q_norm, m_mla_kv_norm, m_mla_w_uq, m_mla_w_ukv,
                           m_w_out_ab, m_mix_norm_c, m_gqa_w_q, m_gqa_w_kv, m_gqa_q_norm, m_gqa_k_norm, m_gqa_w_o,
                           m_ffn_norm, m_ffn_w_in, m_ffn_w_out, m_final_norm)))
    v = dict(zip(WEIGHTS, (v_mix_norm_ab, v_w_in_ab, v_mla_q_norm, v_mla_kv_norm, v_mla_w_uq, v_mla_w_ukv,
                           v_w_out_ab, v_mix_norm_c, v_gqa_w_q, v_gqa_w_kv, v_gqa_q_norm, v_gqa_k_norm, v_gqa_w_o,
                           v_ffn_norm, v_ffn_w_in, v_ffn_w_out, v_final_norm)))
    return _step(x, loss_target, w, m, v)
```

```python
import math

import numpy as np
import jax
import jax.numpy as jnp
from jax import lax
from jax.experimental import pallas as pl
from jax.experimental.pallas import tpu as pltpu

F32 = jnp.float32
BF16 = jnp.bfloat16
MESH = pl.DeviceIdType.MESH

VMEM_LIMIT_BYTES = 56 * 1024 * 1024
LANES = 128

D_MODEL = 1024
NORM_EPS = 1e-6
ROPE_THETA = 10000.0
NEG_INF = -1e30
GRID_W = 64

MLA_HEADS, MLA_Q_RANK, MLA_KV_RANK, MLA_NOPE, MLA_ROPE, MLA_V = 8, 384, 256, 64, 32, 64
MLA_DK = 128
DIL_PAIRS = ((128, 1), (512, 4), (2048, 16))
DIL_HALF, DIL_SLOTS, DIL_GROUPS, DIL_HEAD_DIM = 64, 4, 3, 64
DIL_HEADS = DIL_SLOTS * DIL_GROUPS
DIL_W = DIL_SLOTS * DIL_HEAD_DIM
GQA_HEADS, GQA_KV_HEADS, GQA_HEAD_DIM = 16, 4, 64
FFN_HIDDEN = 2816
IN_A = MLA_Q_RANK + MLA_KV_RANK + MLA_ROPE
IN_A_PAD = 768
IN_B = 3 * DIL_HEADS * DIL_HEAD_DIM

ADAM_LR, ADAM_B1, ADAM_B2, ADAM_EPS, ADAM_WD, ADAM_STEP = 0.001, 0.9, 0.999, 1e-08, 0.01, 10

LOG2E, LN2 = math.log2(math.e), math.log(2.0)
MLA_SCALE = (MLA_NOPE + MLA_ROPE) ** -0.5
GQA_SCALE = GQA_HEAD_DIM ** -0.5

WEIGHTS = ['mix_norm_ab', 'w_in_ab', 'mla_q_norm', 'mla_kv_norm', 'mla_w_uq', 'mla_w_ukv', 'w_out_ab', 'mix_norm_c',
           'gqa_w_q', 'gqa_w_kv', 'gqa_q_norm', 'gqa_k_norm', 'gqa_w_o', 'ffn_norm', 'ffn_w_in', 'ffn_w_out',
           'final_norm']
BIG = (('w_in_ab', (2, 1024, 744), 2), ('mla_w_uq', (2, 96, 8, 96), 1), ('mla_w_ukv', (2, 64, 8, 128), 1),
       ('w_out_ab', (2, 768, 256), 2), ('gqa_w_q', (2, 256, 1024), 1), ('gqa_w_kv', (2, 256, 512), 1),
       ('gqa_w_o', (2, 256, 1024), 1), ('ffn_w_in', (4, 1024, 1408), 2), ('ffn_w_out', (4, 704, 1024), 1))
PACK_W = 1024
SMALL = (('mix_norm_ab', (2, 1024)), ('mla_q_norm', (2, 384)), ('mla_kv_norm', (2, 256)), ('gqa_q_norm', (2, 64)),
         ('gqa_k_norm', (2, 64)), ('ffn_norm', (4, 1024)), ('final_norm', (1024,)), ('mix_norm_c', (2, 1024)))
PACKET_ROWS = 88


def _cparams(sem=None):
    return pltpu.CompilerParams(dimension_semantics=sem, vmem_limit_bytes=VMEM_LIMIT_BYTES)


def _pick(n, pref, mult=LANES):
    if n <= pref:
        return n
    for d in range(pref - pref % mult, 0, -mult):
        if n % d == 0:
            return d
    return n


_DIMS = {"nn": (((1,), (0,)), ((), ())), "nt": (((1,), (1,)), ((), ())), "tn": (((0,), (0,)), ((), ()))}


def _mm(a, b, *, mode="nn", add=None, out_dtype=F32, name="mm"):
    if mode == "nn":
        (m, k), (k2, n) = a.shape, b.shape
    elif mode == "nt":
        (m, k), (n, k2) = a.shape, b.shape
    else:
        (k, m), (k2, n) = a.shape, b.shape
    assert k == k2, (a.shape, b.shape, mode)
    if mode == "tn":
        deep = a.dtype == BF16 and b.dtype == BF16
        bm, bn, bk = _pick(m, 1408), _pick(n, 1024), _pick(k, 2048 if deep else 1024, 16)
    else:
        bm, bn, bk = _pick(m, 512, 16), _pick(n, 1408), _pick(k, 2816)
    nk = k // bk
    assert m % bm == 0 and n % bn == 0 and k % bk == 0
    has_add = add is not None

    def body(*refs):
        a_ref, b_ref = refs[0], refs[1]
        add_ref = refs[2] if has_add else None
        o_ref = refs[3 if has_add else 2]
        part = lax.dot_general(a_ref[...].astype(BF16), b_ref[...].astype(BF16), _DIMS[mode],
                               preferred_element_type=F32)

        def finish(r):
            if has_add:
                r = r + add_ref[...]
            o_ref[...] = r.astype(o_ref.dtype)

        if nk == 1:
            finish(part)
        else:
            acc_ref = refs[-1]
            kk = pl.program_id(2)

            @pl.when(kk == 0)
            def _():
                acc_ref[...] = part

            @pl.when(kk > 0)
            def _():
                acc_ref[...] += part

            @pl.when(kk == nk - 1)
            def _():
                finish(acc_ref[...])

    a_bytes, b_bytes = a.size * a.dtype.itemsize, b.size * b.dtype.itemsize
    n_outer = nk == 1 and (n // bn) * a_bytes + b_bytes < a_bytes + (m // bm) * b_bytes

    def at(f):
        return (lambda j, i, kk: f(i, j, kk)) if n_outer else f

    if mode == "nn":
        a_spec = pl.BlockSpec((bm, bk), at(lambda i, j, kk: (i, kk)))
        b_spec = pl.BlockSpec((bk, bn), at(lambda i, j, kk: (kk, j)))
    elif mode == "nt":
        a_spec = pl.BlockSpec((bm, bk), at(lambda i, j, kk: (i, kk)))
        b_spec = pl.BlockSpec((bn, bk), at(lambda i, j, kk: (j, kk)))
    else:
        a_spec = pl.BlockSpec((bk, bm), at(lambda i, j, kk: (kk, i)))
        b_spec = pl.BlockSpec((bk, bn), at(lambda i, j, kk: (kk, j)))
    o_spec = pl.BlockSpec((bm, bn), at(lambda i, j, kk: (i, j)))
    in_specs = [a_spec, b_spec] + ([o_spec] if has_add else [])
    args = (a, b) + ((add,) if has_add else ())
    grid = (n // bn, m // bm, nk) if n_outer else (m // bm, n // bn, nk)
    return pl.pallas_call(
        body, name=name, grid=grid, in_specs=in_specs, out_specs=o_spec,
        out_shape=jax.ShapeDtypeStruct((m, n), out_dtype),
        scratch_shapes=[pltpu.VMEM((bm, bn), F32)] if nk > 1 else [],
        compiler_params=_cparams(("parallel", "parallel", "arbitrary")),
    )(*args)


def _rows_call(fn, rows, consts, out_rows, out_accs=(), *, bs=256, name):
    s = rows[0].shape[0]
    bs = min(bs, s)
    assert s % bs == 0
    nr, nc, no, na = len(rows), len(consts), len(out_rows), len(out_accs)

    def body(*refs):
        vals = [r[...] for r in refs[:nr + nc]]
        outs = refs[nr + nc:]
        res = fn(*vals)
        if not isinstance(res, (tuple, list)):
            res = (res,)
        assert len(res) == no + na, (len(res), no, na)
        for r, v in zip(outs[:no], res[:no]):
            r[...] = v.astype(r.dtype)
        if na:
            i = pl.program_id(0)
            for r, v in zip(outs[no:], res[no:]):
                @pl.when(i == 0)
                def _(r=r, v=v):
                    r[...] = v

                @pl.when(i > 0)
                def _(r=r, v=v):
                    r[...] += v

    in_specs = [pl.BlockSpec((bs, a.shape[1]), lambda i: (i, 0)) for a in rows]
    in_specs += [pl.BlockSpec(c.shape, lambda i: (0, 0)) for c in consts]
    out_specs = [pl.BlockSpec((bs, c), lambda i: (i, 0)) for c, _ in out_rows]
    out_specs += [pl.BlockSpec(tuple(sh), lambda i: (0, 0)) for sh in out_accs]
    out_shape = [jax.ShapeDtypeStruct((s, c), dt) for c, dt in out_rows]
    out_shape += [jax.ShapeDtypeStruct(tuple(sh), F32) for sh in out_accs]
    res = pl.pallas_call(
        body, name=name, grid=(s // bs,), in_specs=in_specs, out_specs=out_specs, out_shape=out_shape,
        compiler_params=_cparams(("arbitrary",) if na else ("parallel",)),
    )(*rows, *consts)
    return res


def _rms(x, g):
    return x * lax.rsqrt(jnp.mean(x * x, axis=-1, keepdims=True) + NORM_EPS) * g


def _rms_bwd_math(x, g, dy):
    r = lax.rsqrt(jnp.mean(x * x, axis=-1, keepdims=True) + NORM_EPS)
    u = dy * g
    dx = r * u - x * (r * r * r) * jnp.mean(u * x, axis=-1, keepdims=True)
    dg = jnp.sum(dy * x * r, axis=0, keepdims=True)
    return dx, dg


def _rms_fwd(x, g, name):
    return _rows_call(lambda xv, gv: _rms(xv, gv), [x], [g], [(x.shape[1], BF16)], name=name)[0]


def _rms_bwd(x, g, dy, dres, name):
    def fn(xv, dyv, dresv, gv):
        dx, dg = _rms_bwd_math(xv, gv, dyv.astype(F32))
        return dx + dresv, dg
    return _rows_call(fn, [x, dy, dres], [g], [(x.shape[1], F32)], [(1, x.shape[1])], name=name)


def _chunkdot(x, m):
    outs = [jnp.dot(x[:, c:c + LANES], m, precision=lax.Precision.HIGHEST, preferred_element_type=F32)
            for c in range(0, x.shape[1], LANES)]
    return outs[0] if len(outs) == 1 else jnp.concatenate(outs, axis=1)


def _lanes(t, width):
    n = width // LANES
    return t if n == 1 else jnp.concatenate([t] * n, axis=1)


def _rope(x, cs, swap):
    w = x.shape[1]
    return x * _lanes(cs[:, :LANES], w) + _chunkdot(x, swap) * _lanes(cs[:, LANES:], w)


def _rope_t(dy, cs, swap):
    w = dy.shape[1]
    return dy * _lanes(cs[:, :LANES], w) + _chunkdot(dy * _lanes(cs[:, LANES:], w), swap)


def _swap_matrix():
    m = np.zeros((LANES, LANES), np.float32)
    for j in range(LANES):
        src = j + 16 if (j % 32) < 16 else j - 16
        m[src, j] = 1.0
    return jnp.asarray(m)


def _seg_matrix(seg):
    idx = np.arange(LANES) // seg
    return jnp.asarray((idx[:, None] == idx[None, :]).astype(np.float32))


def _rope_tables(s):
    pos = jnp.arange(s)

    def angles(p, dim):
        freqs = ROPE_THETA ** (-jnp.arange(0, dim, 2, dtype=F32) / dim)
        ang = p.astype(F32)[:, None] * freqs[None, :]
        return jnp.cos(ang), jnp.sin(ang)

    cos_t, sin_t = angles(pos, MLA_ROPE)
    one, zero = jnp.ones((s, 64), F32), jnp.zeros((s, 64), F32)
    mla = jnp.concatenate([one, cos_t, cos_t, one[:, :32], zero, -sin_t, sin_t, zero[:, :32]], axis=1)
    cos_r, sin_r = angles(pos // GRID_W, GQA_HEAD_DIM // 2)
    cos_c, sin_c = angles(pos % GRID_W, GQA_HEAD_DIM // 2)
    c64 = jnp.concatenate([cos_r, cos_r, cos_c, cos_c], axis=1)
    s64 = jnp.concatenate([-sin_r, sin_r, -sin_c, sin_c], axis=1)
    gqa = jnp.concatenate([c64, c64, s64, s64], axis=1)
    return mla, gqa


def _col_to_row(col):
    return jnp.transpose(jnp.broadcast_to(col, (col.shape[0], LANES)))[0:1, :]


def _stack_heads(ref, heads, d, dtype=None):
    parts = [ref[:, hd * d:(hd + 1) * d] for hd in heads]
    out = parts[0] if len(parts) == 1 else jnp.concatenate(parts, axis=0)
    return out if dtype is None else out.astype(dtype)


def _fill_v_ones(v_ref, va_ref, hb, dv):
    @pl.when(pl.program_id(1) == 0)
    def _():
        ones = jnp.ones((v_ref.shape[0], dv), BF16)
        for h in range(hb):
            va_ref[:, 2 * h * dv:(2 * h + 1) * dv] = v_ref[:, h * dv:(h + 1) * dv]
            va_ref[:, (2 * h + 1) * dv:(2 * h + 2) * dv] = ones


def _flash_fwd(q, k, v, *, R, dk, dv, hb, bq, bk, name):
    s = q.shape[0]
    g = k.shape[1] // dk
    ng = g // hb
    bq, bk = min(bq, s), min(bk, s)
    nq, nkb = s // bq, s // bk
    rb = R * bq

    def body(q_ref, k_ref, v_ref, o_ref, lse_ref, va_ref):
        _fill_v_ones(v_ref, va_ref, hb, dv)
        head_sets = [[h * R + r for r in range(R)] for h in range(hb)]
        qss = [_stack_heads(q_ref, heads, dk) for heads in head_sets]

        def step(jj, carry):
            carry = list(carry)
            rows = [pl.ds(pl.multiple_of((jj * unroll + u) * bk, bk), bk) for u in range(unroll)]
            scs = [[lax.dot_general(qss[h], k_ref[rows[u], h * dk:(h + 1) * dk], _DIMS["nt"],
                                    preferred_element_type=F32) for h in range(hb)] for u in range(unroll)]
            for u in range(unroll):
                m2s = [jnp.maximum(carry[h][0], jnp.max(scs[u][h], axis=1, keepdims=True)) for h in range(hb)]
                ps = [jnp.exp2(scs[u][h] - m2s[h]).astype(BF16) for h in range(hb)]
                pvs = [jnp.dot(ps[h], va_ref[rows[u], 2 * h * dv:2 * (h + 1) * dv], preferred_element_type=F32)
                       for h in range(hb)]
                carry = [(m2s[h], jnp.exp2(carry[h][0] - m2s[h]) * carry[h][1] + pvs[h]) for h in range(hb)]
            return tuple(carry)

        unroll = 2 if nkb % 2 == 0 else 1
        init = tuple((jnp.full((rb, 1), NEG_INF, F32), jnp.zeros((rb, 2 * dv), F32)) for _ in range(hb))
        final = lax.fori_loop(0, nkb // unroll, step, init)
        for h, heads in enumerate(head_sets):
            m, acc = final[h]
            l = acc[:, dv:dv + 1]
            o = acc[:, :dv] / l
            row = _col_to_row(m + jnp.log2(l))
            for r, hd in enumerate(heads):
                o_ref[:, hd * dv:(hd + 1) * dv] = o[r * bq:(r + 1) * bq].astype(o_ref.dtype)
                lse_ref[0, hd:hd + 1, :] = row[:, r * bq:(r + 1) * bq]

    return pl.pallas_call(
        body, name=name, grid=(ng, nq),
        in_specs=[pl.BlockSpec((bq, hb * R * dk), lambda gi, i: (i, gi)),
                  pl.BlockSpec((s, hb * dk), lambda gi, i: (0, gi)),
                  pl.BlockSpec((s, hb * dv), lambda gi, i: (0, gi))],
        out_specs=[pl.BlockSpec((bq, hb * R * dv), lambda gi, i: (i, gi)),
                   pl.BlockSpec((1, hb * R, bq), lambda gi, i: (gi, 0, i))],
        out_shape=[jax.ShapeDtypeStruct((s, g * R * dv), BF16), jax.ShapeDtypeStruct((ng, hb * R, s), F32)],
        scratch_shapes=[pltpu.VMEM((s, 2 * hb * dv), BF16)],
        compiler_params=_cparams(("parallel", "arbitrary")),
    )(q, k, v)


def _flash_bwd(q, k, v, o, do, lse, *, R, dk, dv, hb, bq, bk, name):
    s = q.shape[0]
    g = k.shape[1] // dk
    ng = g // hb
    bq, bk = min(bq, s), min(bk, s)
    nq, nkb = s // bq, s // bk
    rb = R * bq

    def body(q_ref, k_ref, v_ref, o_ref, do_ref, lse_ref, dq_ref, dk_ref, dv_ref, va_ref):
        @pl.when(pl.program_id(1) == 0)
        def _():
            dk_ref[...] = jnp.zeros(dk_ref.shape, F32)
            dv_ref[...] = jnp.zeros(dv_ref.shape, F32)

        _fill_v_ones(v_ref, va_ref, hb, dv)
        lane = lax.broadcasted_iota(jnp.int32, (rb, dv), 1)
        head_sets = [[h * R + r for r in range(R)] for h in range(hb)]
        qss, doss, dosas, lrows = [], [], [], []
        for heads in head_sets:
            dos = _stack_heads(do_ref, heads, dv, BF16)
            delta = jnp.sum(dos.astype(F32) * _stack_heads(o_ref, heads, dv, F32), axis=1, keepdims=True)
            hi = delta.astype(BF16).astype(F32)
            lo = delta - hi
            qss.append(_stack_heads(q_ref, heads, dk))
            doss.append(dos)
            dosas.append(jnp.concatenate(
                [dos, jnp.where(lane == 0, -hi, jnp.where(lane == 1, -lo, 0.0)).astype(BF16)], axis=1))
            rows = [lse_ref[0, hd:hd + 1, :] for hd in heads]
            lrows.append(rows[0] if R == 1 else jnp.concatenate(rows, axis=1))

        def step(j, dqs):
            r0 = pl.multiple_of(j * bk, bk)
            hs = range(hb)
            kjs = [k_ref[pl.ds(r0, bk), h * dk:(h + 1) * dk] for h in hs]
            vas = [va_ref[pl.ds(r0, bk), 2 * h * dv:2 * (h + 1) * dv] for h in hs]
            sts = [lax.dot_general(kjs[h], qss[h], _DIMS["nt"], preferred_element_type=F32) for h in hs]
            dpts = [lax.dot_general(vas[h], dosas[h], _DIMS["nt"], preferred_element_type=F32) for h in hs]
            pts = [jnp.exp2(sts[h] - lrows[h]) for h in hs]
            pbs = [pts[h].astype(BF16) for h in hs]
            dsts = [(pts[h] * dpts[h]).astype(BF16) for h in hs]
            for h in hs:
                dv_ref[pl.ds(r0, bk), h * dv:(h + 1) * dv] += jnp.dot(pbs[h], doss[h], preferred_element_type=F32)
            for h in hs:
                dk_ref[pl.ds(r0, bk), h * dk:(h + 1) * dk] += jnp.dot(dsts[h], qss[h], preferred_element_type=F32)
            return tuple(dqs[h] + lax.dot_general(dsts[h], kjs[h], _DIMS["tn"], preferred_element_type=F32)
                         for h in hs)

        dqs = lax.fori_loop(0, nkb, step, tuple(jnp.zeros((rb, dk), F32) for _ in range(hb)))
        for h, heads in enumerate(head_sets):
            for r, hd in enumerate(heads):
                dq_ref[:, hd * dk:(hd + 1) * dk] = dqs[h][r * bq:(r + 1) * bq]

    qspec = pl.BlockSpec((bq, hb * R * dk), lambda gi, i: (i, gi))
    ospec = pl.BlockSpec((bq, hb * R * dv), lambda gi, i: (i, gi))
    kspec = pl.BlockSpec((s, hb * dk), lambda gi, i: (0, gi))
    vspec = pl.BlockSpec((s, hb * dv), lambda gi, i: (0, gi))
    return pl.pallas_call(
        body, name=name, grid=(ng, nq),
        in_specs=[qspec, kspec, vspec, ospec, ospec, pl.BlockSpec((1, hb * R, bq), lambda gi, i: (gi, 0, i))],
        out_specs=[qspec, kspec, vspec],
        out_shape=[jax.ShapeDtypeStruct((s, g * R * dk), F32), jax.ShapeDtypeStruct((s, g * dk), F32),
                   jax.ShapeDtypeStruct((s, g * dv), F32)],
        scratch_shapes=[pltpu.VMEM((s, 2 * hb * dv), BF16)],
        compiler_params=_cparams(("parallel", "arbitrary")),
    )(q, k, v, o, do, lse)


DIL_T = 1024
DIL_P = DIL_HALF
DIL_NCOL = IN_B // DIL_W


DIL_BATCH = 4


def _alibi_slope(head):
    return float(2.0 ** (-8.0 * (head + 1) / DIL_HEADS))


def _slot(sl_i):
    return slice(sl_i * DIL_HEAD_DIM, (sl_i + 1) * DIL_HEAD_DIM)


def _halo_specs(d, col, s, t):
    h = DIL_P * d
    per, last = t // h, s // h - 1
    return [pl.BlockSpec((h, DIL_W), lambda c: (jnp.maximum(c * per - 1, 0), col)),
            pl.BlockSpec((t, DIL_W), lambda c: (c, col)),
            pl.BlockSpec((h, DIL_W), lambda c: (jnp.minimum((c + 1) * per, last), col))]


def _staging(rows):
    return tuple(pltpu.VMEM((rows, LANES), F32) for _ in range(DIL_W // LANES))


def _stage(buf, refs):
    off = 0
    for r in refs:
        val = r[...].astype(F32)
        for j in range(DIL_W // LANES):
            buf[j][off:off + r.shape[0], :] = val[:, j * LANES:(j + 1) * LANES]
        off += r.shape[0]


def _unstage(buf, ref):
    ref[...] = jnp.concatenate([half[...] for half in buf], axis=1).astype(ref.dtype)


def _sub_tiles(d, t):
    return [(b * DIL_P * d + r, b * DIL_P) for b in range(t // (DIL_P * d)) for r in range(d)]


def _rows(start, size, d):
    return pl.ds(start, size, stride=d) if d > 1 else pl.ds(start, size)


def _strided(buf, start, size, d):
    return jnp.concatenate([half[_rows(start, size, d), :] for half in buf], axis=1)


def _put_strided(buf, start, d, val):
    for j in range(DIL_W // LANES):
        buf[j][_rows(start, val.shape[0], d), :] = val[:, j * LANES:(j + 1) * LANES]


def _band(u0, length, d, queries_wide):
    if queries_wide:
        shape = (3 * DIL_P, DIL_P)
        wide = u0 - DIL_P + lax.broadcasted_iota(jnp.int32, shape, 0)
        narrow = u0 + lax.broadcasted_iota(jnp.int32, shape, 1)
    else:
        shape = (DIL_P, 3 * DIL_P)
        narrow = u0 + lax.broadcasted_iota(jnp.int32, shape, 0)
        wide = u0 - DIL_P + lax.broadcasted_iota(jnp.int32, shape, 1)
    rel = jnp.abs(wide - narrow)
    valid = (rel <= DIL_HALF) & (wide >= 0) & (wide < length)
    return valid, rel.astype(F32) * float(d)


def _dil_fwd(zb, grp, name):
    s = zb.shape[0]
    d = DIL_PAIRS[grp][1]
    t = min(DIL_T, s)
    h = DIL_P * d
    scale = DIL_HEAD_DIM ** -0.5

    def body(q_ref, kp, kc, kn, vp, vc, vn, o_ref, lse_ref, qbuf, kbuf, vbuf, obuf, lbuf):
        _stage(qbuf, (q_ref,))
        _stage(kbuf, (kp, kc, kn))
        _stage(vbuf, (vp, vc, vn))
        u_step = pl.program_id(0) * (t // d)
        tiles = _sub_tiles(d, t)
        for g0 in range(0, len(tiles), DIL_BATCH):
            batch = tiles[g0:g0 + DIL_BATCH]
            masks = [_band(u_step + u, s // d, d, False) for _, u in batch]
            qs = [_strided(qbuf, row, DIL_P, d).astype(BF16) for row, _ in batch]
            ks = [_strided(kbuf, row, 3 * DIL_P, d).astype(BF16) for row, _ in batch]
            vs = [_strided(vbuf, row, 3 * DIL_P, d).astype(BF16) for row, _ in batch]
            chains = [(i, sl_i) for i in range(len(batch)) for sl_i in range(DIL_SLOTS)]
            scs = [lax.dot_general(qs[i][:, _slot(sl_i)], ks[i][:, _slot(sl_i)], _DIMS["nt"],
                                   preferred_element_type=F32) for i, sl_i in chains]
            scs = [jnp.where(masks[i][0], sc * scale - _alibi_slope(grp * DIL_SLOTS + sl_i) * masks[i][1], NEG_INF)
                   for (i, sl_i), sc in zip(chains, scs)]
            ms = [jnp.max(sc, axis=1, keepdims=True) for sc in scs]
            es = [jnp.exp(sc - m) for sc, m in zip(scs, ms)]
            dens = [jnp.sum(e, axis=1, keepdims=True) for e in es]
            outs = [jnp.dot((e / den).astype(BF16), vs[i][:, _slot(sl_i)], preferred_element_type=F32)
                    for (i, sl_i), e, den in zip(chains, es, dens)]
            lses = [jnp.broadcast_to(m + jnp.log(den), (DIL_P, DIL_HEAD_DIM)) for m, den in zip(ms, dens)]
            for i, (row, _) in enumerate(batch):
                pick = slice(i * DIL_SLOTS, (i + 1) * DIL_SLOTS)
                _put_strided(obuf, row, d, jnp.concatenate(outs[pick], axis=1))
                _put_strided(lbuf, row, d, jnp.concatenate(lses[pick], axis=1))
        _unstage(obuf, o_ref)
        _unstage(lbuf, lse_ref)

    own = pl.BlockSpec((t, DIL_W), lambda c: (c, 0))
    return pl.pallas_call(
        body, name=name, grid=(s // t,),
        in_specs=[pl.BlockSpec((t, DIL_W), lambda c: (c, grp))] + _halo_specs(d, 3 + grp, s, t)
        + _halo_specs(d, 6 + grp, s, t),
        out_specs=[own, own], out_shape=[jax.ShapeDtypeStruct((s, DIL_W), F32)] * 2,
        scratch_shapes=[_staging(t), _staging(t + 2 * h), _staging(t + 2 * h), _staging(t), _staging(t)],
        compiler_params=_cparams(("parallel",)),
    )(zb, zb, zb, zb, zb, zb, zb)


def _dil_bwd(zb, do, lse, dl, grp, name):
    s = zb.shape[0]
    d = DIL_PAIRS[grp][1]
    t = min(DIL_T, s)
    h = DIL_P * d
    scale = DIL_HEAD_DIM ** -0.5

    def chain_grads(qs, ks, vs, dos, lses, dls, masks):
        chains = [(i, sl_i) for i in range(len(qs)) for sl_i in range(DIL_SLOTS)]
        scs = [lax.dot_general(qs[i][:, _slot(sl_i)], ks[i][:, _slot(sl_i)], _DIMS["nt"],
                               preferred_element_type=F32) for i, sl_i in chains]
        dps = [lax.dot_general(dos[i][:, _slot(sl_i)], vs[i][:, _slot(sl_i)], _DIMS["nt"],
                               preferred_element_type=F32) for i, sl_i in chains]
        ps = [jnp.exp(jnp.where(masks[i][0], sc * scale - _alibi_slope(grp * DIL_SLOTS + sl_i) * masks[i][1],
                                NEG_INF) - lses[i][:, sl_i * DIL_HEAD_DIM:sl_i * DIL_HEAD_DIM + 1])
              for (i, sl_i), sc in zip(chains, scs)]
        dss = [(p * (dp - dls[i][:, sl_i * DIL_HEAD_DIM:sl_i * DIL_HEAD_DIM + 1]) * scale).astype(BF16)
               for (i, sl_i), p, dp in zip(chains, ps, dps)]
        return chains, ps, dss

    def dq_body(q_ref, kp, kc, kn, vp, vc, vn, do_ref, lse_ref, dl_ref, dq_ref, qbuf, kbuf, vbuf, dobuf, lsebuf,
                dlbuf, obuf):
        _stage(qbuf, (q_ref,))
        _stage(dobuf, (do_ref,))
        _stage(lsebuf, (lse_ref,))
        _stage(dlbuf, (dl_ref,))
        _stage(kbuf, (kp, kc, kn))
        _stage(vbuf, (vp, vc, vn))
        u_step = pl.program_id(0) * (t // d)
        tiles = _sub_tiles(d, t)
        for g0 in range(0, len(tiles), DIL_BATCH):
            batch = tiles[g0:g0 + DIL_BATCH]
            masks = [_band(u_step + u, s // d, d, False) for _, u in batch]
            narrow = [[_strided(b, row, DIL_P, d) for row, _ in batch] for b in (qbuf, dobuf, lsebuf, dlbuf)]
            ks = [_strided(kbuf, row, 3 * DIL_P, d).astype(BF16) for row, _ in batch]
            vs = [_strided(vbuf, row, 3 * DIL_P, d).astype(BF16) for row, _ in batch]
            chains, _, dss = chain_grads([a.astype(BF16) for a in narrow[0]], ks, vs,
                                         [a.astype(BF16) for a in narrow[1]], narrow[2], narrow[3], masks)
            outs = [jnp.dot(ds, ks[i][:, _slot(sl_i)], preferred_element_type=F32)
                    for (i, sl_i), ds in zip(chains, dss)]
            for i, (row, _) in enumerate(batch):
                _put_strided(obuf, row, d, jnp.concatenate(outs[i * DIL_SLOTS:(i + 1) * DIL_SLOTS], axis=1))
        _unstage(obuf, dq_ref)

    def dkv_body(k_ref, v_ref, qp, qc, qn, dop, doc, don, lp, lc, ln, dlp, dlc, dln, dk_ref, dv_ref,
                 kbuf, vbuf, qbuf, dobuf, lsebuf, dlbuf, dkbuf, dvbuf):
        _stage(kbuf, (k_ref,))
        _stage(vbuf, (v_ref,))
        _stage(qbuf, (qp, qc, qn))
        _stage(dobuf, (dop, doc, don))
        _stage(lsebuf, (lp, lc, ln))
        _stage(dlbuf, (dlp, dlc, dln))
        u_step = pl.program_id(0) * (t // d)
        tiles = _sub_tiles(d, t)
        for g0 in range(0, len(tiles), DIL_BATCH):
            batch = tiles[g0:g0 + DIL_BATCH]
            masks = [_band(u_step + u, s // d, d, True) for _, u in batch]
            ks = [_strided(kbuf, row, DIL_P, d).astype(BF16) for row, _ in batch]
            vs = [_strided(vbuf, row, DIL_P, d).astype(BF16) for row, _ in batch]
            wide_ = [[_strided(b, row, 3 * DIL_P, d) for row, _ in batch] for b in (qbuf, dobuf, lsebuf, dlbuf)]
            qs, dos = [a.astype(BF16) for a in wide_[0]], [a.astype(BF16) for a in wide_[1]]
            chains, ps, dss = chain_grads(qs, ks, vs, dos, wide_[2], wide_[3], masks)
            dvs = [lax.dot_general(p.astype(BF16), dos[i][:, _slot(sl_i)], _DIMS["tn"], preferred_element_type=F32)
                   for (i, sl_i), p in zip(chains, ps)]
            dks = [lax.dot_general(ds, qs[i][:, _slot(sl_i)], _DIMS["tn"], preferred_element_type=F32)
                   for (i, sl_i), ds in zip(chains, dss)]
            for i, (row, _) in enumerate(batch):
                pick = slice(i * DIL_SLOTS, (i + 1) * DIL_SLOTS)
                _put_strided(dkbuf, row, d, jnp.concatenate(dks[pick], axis=1))
                _put_strided(dvbuf, row, d, jnp.concatenate(dvs[pick], axis=1))
        _unstage(dkbuf, dk_ref)
        _unstage(dvbuf, dv_ref)

    def zcur(col):
        return pl.BlockSpec((t, DIL_W), lambda c: (c, col))

    own = pl.BlockSpec((t, DIL_W), lambda c: (c, 0))
    out = jax.ShapeDtypeStruct((s, DIL_W), F32)
    tile, wide = _staging(t), _staging(t + 2 * h)
    dq = pl.pallas_call(
        dq_body, name=name + "_dq", grid=(s // t,),
        in_specs=[zcur(grp)] + _halo_specs(d, 3 + grp, s, t) + _halo_specs(d, 6 + grp, s, t) + [own, own, own],
        out_specs=own, out_shape=out, scratch_shapes=[tile, wide, wide, tile, tile, tile, tile],
        compiler_params=_cparams(("parallel",)),
    )(zb, zb, zb, zb, zb, zb, zb, do, lse, dl)
    own3 = _halo_specs(d, 0, s, t)
    dk, dv = pl.pallas_call(
        dkv_body, name=name + "_dkv", grid=(s // t,),
        in_specs=[zcur(3 + grp), zcur(6 + grp)] + _halo_specs(d, grp, s, t) + own3 + own3 + own3,
        out_specs=[own, own], out_shape=[out, out],
        scratch_shapes=[tile, tile, wide, wide, wide, wide, tile, tile],
        compiler_params=_cparams(("parallel",)),
    )(zb, zb, zb, zb, zb, do, do, do, lse, lse, lse, dl, dl, dl)
    return dq, dk, dv


def _dil_combine(os_, ls_, name):
    def fn(o0, o1, o2, l0, l1, l2):
        m = jnp.maximum(jnp.maximum(l0, l1), l2)
        e0, e1, e2 = jnp.exp(l0 - m), jnp.exp(l1 - m), jnp.exp(l2 - m)
        den = e0 + e1 + e2
        comb = (e0 / den) * o0 + (e1 / den) * o1 + (e2 / den) * o2
        return comb, m + jnp.log(den)
    return _rows_call(fn, list(os_) + list(ls_), [], [(DIL_W, BF16), (DIL_W, F32)], name=name)


def _dil_combine_bwd(dcomb, os_, ls_, lt, seg64, name):
    def fn(dc, o0, o1, o2, l0, l1, l2, ltv, seg):
        w = [jnp.exp(l - ltv) for l in (l0, l1, l2)]
        comb = w[0] * o0 + w[1] * o1 + w[2] * o2
        t = _chunkdot(dc * comb, seg)
        return [wg * dc for wg in w] + [wg * t for wg in w]
    return _rows_call(fn, [dcomb] + list(os_) + list(ls_) + [lt], [seg64],
                      [(DIL_W, BF16)] * 3 + [(DIL_W, F32)] * 3, name=name)


def _mla_prep(za, gq, gkv, cs, swap, name):
    def fn(z, csv, gqv, gkvv, sw):
        return (_rms(z[:, :MLA_Q_RANK], gqv), _rms(z[:, MLA_Q_RANK:640], gkvv), _rope(z[:, 640:], csv, sw))
    return _rows_call(fn, [za, cs], [gq, gkv, swap], [(MLA_Q_RANK, BF16), (MLA_KV_RANK, BF16), (LANES, F32)],
                      name=name)


def _mla_prep_bwd(za, cs, dcq, dckv, dkr, gq, gkv, swap, name):
    def fn(z, csv, dcqv, dckvv, dkrv, gqv, gkvv, sw):
        d1, dg1 = _rms_bwd_math(z[:, :MLA_Q_RANK], gqv, dcqv)
        d2, dg2 = _rms_bwd_math(z[:, MLA_Q_RANK:640], gkvv, dckvv)
        d3 = _rope_t(dkrv, csv, sw)
        return jnp.concatenate([d1, d2, d3], axis=1), dg1, dg2
    return _rows_call(fn, [za, cs, dcq, dckv, dkr], [gq, gkv, swap], [(IN_A_PAD, BF16)],
                      [(1, MLA_Q_RANK), (1, MLA_KV_RANK)], name=name)


def _mla_qk(q_raw, k_pad, krr, cs, swap, name):
    w = MLA_HEADS * MLA_DK

    def fn(qv, kv, krv, csv, sw):
        return _rope(qv, csv, sw) * (MLA_SCALE * LOG2E), kv + _lanes(krv, w)
    return _rows_call(fn, [q_raw, k_pad, krr, cs], [swap], [(w, BF16), (w, BF16)], name=name)


def _mla_qk_bwd(dqh, dkh, cs, swap, name):
    w = MLA_HEADS * MLA_DK

    def fn(dq, dk, csv, sw):
        dk = dk * LN2
        acc = dk[:, :LANES]
        for h in range(1, MLA_HEADS):
            acc = acc + dk[:, h * LANES:(h + 1) * LANES]
        lane = lax.broadcasted_iota(jnp.int32, acc.shape, 1)
        acc = jnp.where((lane >= MLA_NOPE) & (lane < MLA_NOPE + MLA_ROPE), acc, 0.0)
        return _rope_t(dq * MLA_SCALE, csv, sw), dk, acc
    return _rows_call(fn, [dqh, dkh, cs], [swap], [(w, BF16), (w, BF16), (LANES, F32)], name=name)


def _head_norm(t, g2, seg):
    r = lax.rsqrt(_chunkdot(t * t, seg) * (1.0 / GQA_HEAD_DIM) + NORM_EPS)
    return t * r * _lanes(g2, t.shape[1]), r


def _head_norm_bwd(t, g2, seg, dn):
    w = t.shape[1]
    r = lax.rsqrt(_chunkdot(t * t, seg) * (1.0 / GQA_HEAD_DIM) + NORM_EPS)
    u = dn * _lanes(g2, w)
    dt = r * u - t * (r * r * r) * (_chunkdot(u * t, seg) * (1.0 / GQA_HEAD_DIM))
    dgw = jnp.sum(dn * t * r, axis=0, keepdims=True)
    dg = dgw[:, :LANES]
    for c in range(LANES, w, LANES):
        dg = dg + dgw[:, c:c + LANES]
    return dt, dg


def _gqa_prep(q_raw, kv_raw, cs, gq2, gk2, seg, swap, name):
    kw = GQA_KV_HEADS * GQA_HEAD_DIM

    def fn(qv, kvv, csv, gqv, gkv, sg, sw):
        qn, _ = _head_norm(qv, gqv, sg)
        kn, _ = _head_norm(kvv[:, :kw], gkv, sg)
        return _rope(qn, csv, sw) * (GQA_SCALE * LOG2E), _rope(kn, csv, sw), kvv[:, kw:]
    return _rows_call(fn, [q_raw, kv_raw, cs], [gq2, gk2, seg, swap],
                      [(GQA_HEADS * GQA_HEAD_DIM, BF16), (kw, BF16), (kw, BF16)], name=name)


def _gqa_prep_bwd(q_raw, kv_raw, cs, dqh, dkh, dv, gq2, gk2, seg, swap, name):
    kw = GQA_KV_HEADS * GQA_HEAD_DIM

    def fn(qv, kvv, csv, dq, dk, dvv, gqv, gkv, sg, sw):
        dqr, dgq = _head_norm_bwd(qv, gqv, sg, _rope_t(dq * GQA_SCALE, csv, sw))
        dkr, dgk = _head_norm_bwd(kvv[:, :kw], gkv, sg, _rope_t(dk * LN2, csv, sw))
        return dqr, jnp.concatenate([dkr, dvv], axis=1), dgq, dgk
    return _rows_call(fn, [q_raw, kv_raw, cs, dqh, dkh, dv], [gq2, gk2, seg, swap],
                      [(GQA_HEADS * GQA_HEAD_DIM, BF16), (2 * kw, BF16)], [(1, LANES), (1, LANES)], name=name)


def _swiglu(h, name):
    def fn(hv):
        gate, up = hv[:, :FFN_HIDDEN].astype(F32), hv[:, FFN_HIDDEN:].astype(F32)
        return gate / (1.0 + jnp.exp(-gate)) * up
    return _rows_call(fn, [h], [], [(FFN_HIDDEN, BF16)], name=name)[0]


def _swiglu_bwd(h, da, name):
    def fn(hv, dav):
        gate, up = hv[:, :FFN_HIDDEN].astype(F32), hv[:, FFN_HIDDEN:].astype(F32)
        dav = dav.astype(F32)
        sig = 1.0 / (1.0 + jnp.exp(-gate))
        silu = gate * sig
        dgate = dav * up * (sig + silu * (1.0 - sig))
        return jnp.concatenate([dgate, dav * silu], axis=1)
    return _rows_call(fn, [h, da], [], [(2 * FFN_HIDDEN, BF16)], name=name)[0]


def _loss_head(x, target, g, name):
    dm = x.shape[1]

    def fn(xv, tv, gv):
        err = _rms(xv, gv) - tv
        loss = 0.5 * jnp.sum(err * err) / dm
        dx, dg = _rms_bwd_math(xv, gv, err * (1.0 / dm))
        return dx, jnp.zeros((1, LANES), F32) + loss, dg
    return _rows_call(fn, [x, target], [g], [(dm, F32)], [(1, LANES), (1, dm)], name=name)


def _adamw(w, g, m, v, name):
    def fn(wv, gv, mv, vv):
        m2 = ADAM_B1 * mv + (1.0 - ADAM_B1) * gv
        v2 = ADAM_B2 * vv + (1.0 - ADAM_B2) * (gv * gv)
        m_hat = m2 / (1.0 - ADAM_B1 ** ADAM_STEP)
        v_hat = v2 / (1.0 - ADAM_B2 ** ADAM_STEP)
        return -ADAM_LR * (m_hat / (jnp.sqrt(v_hat) + ADAM_EPS) + ADAM_WD * wv), m2, v2
    c = w.shape[1]
    return _rows_call(fn, [w, g, m, v], [], [(c, F32)] * 3, bs=_pick(w.shape[0], 256, 8), name=name)


HBM_SPEC = pl.BlockSpec(memory_space=pltpu.HBM)
VMEM_SPEC = pl.BlockSpec(memory_space=pltpu.VMEM)


def _position():
    return lax.axis_index("x"), lax.axis_index("y"), lax.axis_index("c")


def _other_chips(x, y):
    return [(1 - x, y), (x, 1 - y), (1 - x, 1 - y)]


HALF_W = PACK_W // 2


def _cols(c):
    return pl.ds(pl.multiple_of(c * HALF_W, HALF_W), HALF_W)


def _all_gather_weights(packed):
    rows = packed.shape[0]

    def body(p_ref, g_ref, send_sems, recv_sems, local_sem):
        x, y, c = _position()
        chips = _other_chips(x, y)

        def half(chip, hc):
            return g_ref.at[2 * chip[0] + chip[1], :, _cols(hc)]

        def copy(j, src, dst, to):
            return pltpu.make_async_remote_copy(src_ref=src, dst_ref=dst, send_sem=send_sems.at[j],
                                                recv_sem=recv_sems.at[j], device_id=to, device_id_type=MESH)

        mine = pltpu.make_async_copy(p_ref, g_ref.at[2 * x + y], local_sem)
        mine.start()
        first = [copy(j, p_ref.at[:, _cols(c)], half((x, y), c), (*chip, c)) for j, chip in enumerate(chips)]
        for cp in first:
            cp.start()
        passed = [copy(3 + j, half(chip, c), half(chip, c), (x, y, 1 - c)) for j, chip in enumerate(chips)]
        for j, chip in enumerate(chips):
            copy(j, half(chip, c), half(chip, c), (x, y, c)).wait_recv()
            passed[j].start()
        for j, chip in enumerate(chips):
            copy(3 + j, half(chip, 1 - c), half(chip, 1 - c), (x, y, c)).wait_recv()
        for cp in first + passed:
            cp.wait_send()
        mine.wait()

    return pl.pallas_call(
        body, name="all_gather_weights", in_specs=[HBM_SPEC], out_specs=HBM_SPEC,
        out_shape=jax.ShapeDtypeStruct((4, rows, packed.shape[1]), packed.dtype),
        scratch_shapes=[pltpu.SemaphoreType.DMA((6,)), pltpu.SemaphoreType.DMA((6,)), pltpu.SemaphoreType.DMA],
    )(packed)


SEM_SPEC = pl.BlockSpec(memory_space=pltpu.SEMAPHORE)
ANY_SPEC = pl.BlockSpec(memory_space=pl.ANY)
DATAFLOW = pltpu.SideEffectType.DATAFLOW_SIDE_EFFECTING


def _hbm(a):
    return pltpu.with_memory_space_constraint(a, pltpu.HBM)


def _gather_start(packed, tag):
    rows = packed.shape[0]

    def body(p_ref, g_ref, send_sems, recv_sems, p_thru, g_thru, token):
        x, y, c = _position()
        for j, chip in enumerate(_other_chips(x, y)):
            for s in range(2):
                pltpu.make_async_remote_copy(
                    src_ref=p_ref.at[:, _cols(c)], dst_ref=g_ref.at[2 * x + y, :, _cols(c)],
                    send_sem=send_sems.at[2 * j + s], recv_sem=recv_sems.at[2 * j + s],
                    device_id=(*chip, 1 - c if s else c), device_id_type=MESH).start()
        token[...] = jnp.zeros_like(token)

    return pl.pallas_call(
        body, name=tag + "_start",
        out_shape=(pltpu.SemaphoreType.DMA((6,)), pltpu.SemaphoreType.DMA((6,)), pltpu.HBM(packed.shape, packed.dtype),
                   pltpu.HBM((4, rows, PACK_W), packed.dtype), jax.ShapeDtypeStruct((8, LANES), F32)),
        in_specs=(HBM_SPEC, HBM_SPEC), out_specs=(SEM_SPEC, SEM_SPEC, HBM_SPEC, HBM_SPEC, VMEM_SPEC),
        input_output_aliases={0: 2, 1: 3},
        compiler_params=pltpu.CompilerParams(has_side_effects=DATAFLOW),
    )(_hbm(packed), _hbm(lax.empty((4, rows, PACK_W), packed.dtype)))


def _gather_wait(send_sems, recv_sems, p_thru, g_thru, after, tag):
    def body(p_ref, g_ref, send_sems, recv_sems, after_ref, p_dead, got_ref):
        x, y, c = _position()
        for j, chip in enumerate(_other_chips(x, y)):
            for s in range(2):
                cp = pltpu.make_async_remote_copy(
                    src_ref=p_ref.at[:, _cols(c)], dst_ref=g_ref.at[2 * chip[0] + chip[1], :, _cols(1 - c if s else c)],
                    send_sem=send_sems.at[2 * j + s], recv_sem=recv_sems.at[2 * j + s],
                    device_id=(x, y, c), device_id_type=MESH)
                cp.wait_send()
                cp.wait_recv()

    return pl.pallas_call(
        body, name=tag + "_wait",
        out_shape=(pltpu.HBM(p_thru.shape, p_thru.dtype), pltpu.HBM(g_thru.shape, g_thru.dtype)),
        in_specs=(HBM_SPEC, HBM_SPEC, SEM_SPEC, SEM_SPEC, ANY_SPEC), out_specs=(HBM_SPEC, HBM_SPEC),
        input_output_aliases={0: 0, 1: 1},
        compiler_params=pltpu.CompilerParams(has_side_effects=DATAFLOW),
    )(p_thru, g_thru, send_sems, recv_sems, after)[1]


def _sibling_swap_halves(grads, tag):
    rows = grads.shape[1]

    def body(g_ref, a_ref, send_sem, recv_sem):
        x, y, c = _position()
        cp = pltpu.make_async_remote_copy(src_ref=g_ref.at[:, :, _cols(1 - c)], dst_ref=a_ref,
                                          send_sem=send_sem, recv_sem=recv_sem, device_id=(x, y, 1 - c),
                                          device_id_type=MESH)
        cp.start()
        cp.wait()

    return pl.pallas_call(
        body, name=tag + "_swap_cores", in_specs=[HBM_SPEC], out_specs=HBM_SPEC,
        out_shape=jax.ShapeDtypeStruct((4, rows, HALF_W), grads.dtype),
        scratch_shapes=[pltpu.SemaphoreType.DMA, pltpu.SemaphoreType.DMA],
    )(grads)


def _rs_block(rows):
    return max(d for d in range(16, 1601, 16) if rows % d == 0)


def _chip_sum(grads, other, c, tag):
    rows = other.shape[1]
    rb = _rs_block(rows)

    def body(c_ref, g_ref, a_ref, o_ref):
        o_ref[...] = (g_ref[...] + a_ref[...]).astype(o_ref.dtype)

    blk = (1, rb, HALF_W)
    return pl.pallas_call(
        body, name=tag + "_chip_sum",
        grid_spec=pltpu.PrefetchScalarGridSpec(
            num_scalar_prefetch=1, grid=(4, rows // rb),
            in_specs=[pl.BlockSpec(blk, lambda k, i, c_ref: (k, i, c_ref[0])),
                      pl.BlockSpec(blk, lambda k, i, c_ref: (k, i, 0))],
            out_specs=pl.BlockSpec(blk, lambda k, i, c_ref: (k, i, 0))),
        out_shape=jax.ShapeDtypeStruct(other.shape, BF16),
        compiler_params=_cparams(("parallel", "parallel")),
    )(jnp.reshape(c, (1,)).astype(jnp.int32), grads, other)


def _chip_copies(t_ref, b_ref, send_sems, recv_sems):
    x, y, c = _position()
    return [pltpu.make_async_remote_copy(src_ref=t_ref.at[2 * chip[0] + chip[1]], dst_ref=b_ref.at[j],
                                         send_sem=send_sems.at[j], recv_sem=recv_sems.at[j],
                                         device_id=(*chip, c), device_id_type=MESH)
            for j, chip in enumerate(_other_chips(x, y))]


def _send_chip_sums(sums, tag):
    def body(t_ref, b_ref, send_sems, recv_sems):
        copies = _chip_copies(t_ref, b_ref, send_sems, recv_sems)
        for cp in copies:
            cp.start()
        for cp in copies:
            cp.wait()

    return pl.pallas_call(
        body, name=tag + "_send_chips", in_specs=[HBM_SPEC], out_specs=HBM_SPEC,
        out_shape=jax.ShapeDtypeStruct((3,) + sums.shape[1:], sums.dtype),
        scratch_shapes=[pltpu.SemaphoreType.DMA((3,)), pltpu.SemaphoreType.DMA((3,))],
    )(sums)


def _send_chip_sums_start(sums, tag):
    land = (3,) + sums.shape[1:]

    def body(t_ref, b_ref, send_sems, recv_sems, t_thru, b_thru, token):
        for cp in _chip_copies(t_ref, b_ref, send_sems, recv_sems):
            cp.start()
        token[...] = jnp.zeros_like(token)

    return pl.pallas_call(
        body, name=tag + "_send_chips_start",
        out_shape=(pltpu.SemaphoreType.DMA((3,)), pltpu.SemaphoreType.DMA((3,)), pltpu.HBM(sums.shape, sums.dtype),
                   pltpu.HBM(land, sums.dtype), jax.ShapeDtypeStruct((8, LANES), F32)),
        in_specs=(HBM_SPEC, HBM_SPEC), out_specs=(SEM_SPEC, SEM_SPEC, HBM_SPEC, HBM_SPEC, VMEM_SPEC),
        input_output_aliases={0: 2, 1: 3},
        compiler_params=pltpu.CompilerParams(has_side_effects=DATAFLOW),
    )(_hbm(sums), _hbm(lax.empty(land, sums.dtype)))


def _send_chip_sums_wait(send_sems, recv_sems, t_thru, b_thru, after, tag):
    def body(t_ref, b_ref, send_sems, recv_sems, after_ref, t_dead, got_ref):
        for cp in _chip_copies(t_ref, b_ref, send_sems, recv_sems):
            cp.wait_send()
            cp.wait_recv()

    return pl.pallas_call(
        body, name=tag + "_send_chips_wait",
        out_shape=(pltpu.HBM(t_thru.shape, t_thru.dtype), pltpu.HBM(b_thru.shape, b_thru.dtype)),
        in_specs=(HBM_SPEC, HBM_SPEC, SEM_SPEC, SEM_SPEC, ANY_SPEC), out_specs=(HBM_SPEC, HBM_SPEC),
        input_output_aliases={0: 0, 1: 1},
        compiler_params=pltpu.CompilerParams(has_side_effects=DATAFLOW),
    )(t_thru, b_thru, send_sems, recv_sems, after)[1]


def _final_sum(grads, other, recv, k, c, tag):
    rows = other.shape[1]
    rb = _rs_block(rows)

    def body(k_ref, c_ref, g_ref, a_ref, b_ref, o_ref):
        own = g_ref[0] + a_ref[0]
        o_ref[...] = ((own + b_ref[0].astype(F32)) + b_ref[1].astype(F32)) + b_ref[2].astype(F32)

    return pl.pallas_call(
        body, name=tag + "_final_sum",
        grid_spec=pltpu.PrefetchScalarGridSpec(
            num_scalar_prefetch=2, grid=(rows // rb,),
            in_specs=[pl.BlockSpec((1, rb, HALF_W), lambda i, k_ref, c_ref: (k_ref[0], i, c_ref[0])),
                      pl.BlockSpec((1, rb, HALF_W), lambda i, k_ref, c_ref: (k_ref[0], i, 0)),
                      pl.BlockSpec((3, rb, HALF_W), lambda i, k_ref, c_ref: (0, i, 0))],
            out_specs=pl.BlockSpec((rb, HALF_W), lambda i, k_ref, c_ref: (i, 0))),
        out_shape=jax.ShapeDtypeStruct((rows, HALF_W), F32),
        compiler_params=_cparams(("parallel",)),
    )(jnp.reshape(k, (1,)).astype(jnp.int32), jnp.reshape(c, (1,)).astype(jnp.int32), grads, other, recv)


def _join_halves(half, core, tag):
    def body(h_ref, o_ref, send_sem, recv_sem):
        x, y, c = _position()
        cp = pltpu.make_async_remote_copy(src_ref=h_ref, dst_ref=o_ref, send_sem=send_sem, recv_sem=recv_sem,
                                          device_id=(x, y, 1 - c), device_id_type=MESH)
        cp.start()
        cp.wait()

    other = pl.pallas_call(
        body, name=tag + "_join_cores", in_specs=[HBM_SPEC], out_specs=HBM_SPEC,
        out_shape=jax.ShapeDtypeStruct(half.shape, half.dtype),
        scratch_shapes=[pltpu.SemaphoreType.DMA, pltpu.SemaphoreType.DMA],
    )(half)
    first = core == 0
    return jnp.concatenate([jnp.where(first, half, other), jnp.where(first, other, half)], axis=1)


def _all_reduce_packet(packet):
    rows = packet.shape[0]

    def body(p_ref, o_ref, buf, send_sems, recv_sems):
        x, y, c = _position()
        me = 4 * x + 2 * y + c
        buf[me] = p_ref[...]

        def flip(v, bit):
            return 1 - v if bit else v

        for p in range(1, 8):
            peer = (flip(x, p & 4), flip(y, p & 2), flip(c, p & 1))
            pltpu.make_async_remote_copy(src_ref=p_ref, dst_ref=buf.at[me], send_sem=send_sems.at[p - 1],
                                         recv_sem=recv_sems.at[p - 1], device_id=peer, device_id_type=MESH).start()
        for p in range(1, 8):
            peer = (flip(x, p & 4), flip(y, p & 2), flip(c, p & 1))
            slot = 4 * peer[0] + 2 * peer[1] + peer[2]
            cp = pltpu.make_async_remote_copy(src_ref=p_ref, dst_ref=buf.at[slot], send_sem=send_sems.at[p - 1],
                                              recv_sem=recv_sems.at[p - 1], device_id=peer, device_id_type=MESH)
            cp.wait_recv()
            cp.wait_send()
        acc = buf[0]
        for dev in range(1, 8):
            acc = acc + buf[dev]
        o_ref[...] = acc

    return pl.pallas_call(
        body, name="all_reduce_packet", in_specs=[VMEM_SPEC], out_specs=VMEM_SPEC,
        out_shape=jax.ShapeDtypeStruct(packet.shape, F32),
        scratch_shapes=[pltpu.VMEM((8, rows, LANES), F32), pltpu.SemaphoreType.DMA((7,)),
                        pltpu.SemaphoreType.DMA((7,))],
    )(packet)


def _stack_range(name, n_stack, pack):
    if name.startswith('gqa'):
        return (0, 0) if pack == 0 else (0, n_stack)
    return (0, 1) if pack == 0 else (1, n_stack)


def _pack_members(pack):
    out = []
    for n, shape, ax in BIG:
        lo, hi = _stack_range(n, shape[0], pack)
        if hi > lo:
            out.append((n, (hi - lo,) + shape[1:], ax, (lo, hi)))
    return out


PACK_ROW_MULTIPLE = 512


def _pad_rows(parts, dtype):
    rows = sum(p.shape[0] for p in parts)
    pad = -rows % PACK_ROW_MULTIPLE
    return jnp.concatenate(parts + ([jnp.zeros((pad, PACK_W), dtype)] if pad else []), axis=0)


def _pack_blocks(blocks, dtype, pack):
    return _pad_rows([blocks[n][lo:hi].astype(dtype).reshape(-1, PACK_W)
                      for n, _, _, (lo, hi) in _pack_members(pack)], dtype)


def _unpack_blocks(packed, pack):
    out, off = {}, 0
    for n, shape, _, _ in _pack_members(pack):
        r = math.prod(shape) // PACK_W
        out[n] = packed[off:off + r].reshape(shape)
        off += r
    return out


def _unpack_gathered(gathered, pack, own=None, chip=None):
    blocks = [gathered[k] if own is None else jnp.where(chip == k, own, gathered[k]) for k in range(4)]
    per_chip = [_unpack_blocks(blocks[k], pack) for k in range(4)]
    return {n: jnp.concatenate([per_chip[k][n] for k in range(4)], axis=ax) for n, _, ax, _ in _pack_members(pack)}


def _pack_full(full, dtype, pack):
    chips = []
    for k in range(4):
        parts = []
        for n, shape, ax, _ in _pack_members(pack):
            blk = lax.slice_in_dim(full[n], k * shape[ax], (k + 1) * shape[ax], axis=ax)
            parts.append(blk.astype(dtype).reshape(-1, PACK_W))
        chips.append(_pad_rows(parts, dtype))
    return jnp.stack(chips, axis=0)


def _pack_small(vals, loss_row):
    rows = [loss_row.reshape(1, LANES)]
    for n, shape in SMALL:
        v = vals.get(n)
        v = jnp.zeros(shape, F32) if v is None else v
        rows.append(v.astype(F32).reshape(-1, LANES))
    packet = jnp.concatenate(rows, axis=0)
    return jnp.pad(packet, ((0, PACKET_ROWS - packet.shape[0]), (0, 0)))


def _unpack_small(packet):
    out, off = {}, 1
    for n, shape in SMALL:
        r = math.prod(shape) // LANES
        out[n] = packet[off:off + r].reshape(shape)
        off += r
    return packet[0, 0], out


_MLA = dict(R=1, dk=MLA_DK, dv=MLA_V, hb=2, bq=512, bk=512)
_MLA_FWD_BQ = 1024
_GQA = dict(R=GQA_HEADS // GQA_KV_HEADS, dk=GQA_HEAD_DIM, dv=GQA_HEAD_DIM, hb=2, bq=256, bk=512)


def _layer_params(layer, full, gains):
    pack = 0 if layer == 0 else 1
    i = layer // 2

    def mat(name):
        lo, _ = _stack_range(name, 4 if name.startswith('ffn') else 2, pack)
        return full[name][(layer if name.startswith('ffn') else i) - lo]

    p = dict(ffn_norm=gains['ffn_norm'][layer][None], ffn_w_in=mat('ffn_w_in'), ffn_w_out=mat('ffn_w_out'))
    if layer % 2 == 0:
        w_in = mat('w_in_ab')
        zeros = jnp.zeros((D_MODEL, 32), w_in.dtype)
        p['w_a'] = jnp.concatenate([w_in[:, :640], zeros, zeros, w_in[:, 640:IN_A], zeros], axis=1)
        p['w_b'] = w_in[:, IN_A:]
        p['w_uq'] = jnp.pad(mat('mla_w_uq'), ((0, 0), (0, 0), (0, MLA_DK - 96))).reshape(MLA_Q_RANK, -1)
        ukv = mat('mla_w_ukv')
        p['w_uk'] = jnp.pad(ukv[:, :, :MLA_NOPE], ((0, 0), (0, 0), (0, MLA_DK - MLA_NOPE))).reshape(MLA_KV_RANK, -1)
        p['w_uv'] = ukv[:, :, MLA_NOPE:].reshape(MLA_KV_RANK, -1)
        p['w_out'] = mat('w_out_ab')
        p['mix_norm'] = gains['mix_norm_ab'][i][None]
        p['q_norm'] = gains['mla_q_norm'][i][None]
        p['kv_norm'] = gains['mla_kv_norm'][i][None]
    else:
        p['w_q'], p['w_kv'], p['w_o'] = mat('gqa_w_q'), mat('gqa_w_kv'), mat('gqa_w_o')
        p['mix_norm'] = gains['mix_norm_c'][i][None]
        p['q_norm'] = jnp.tile(gains['gqa_q_norm'][i][None], (1, 2))
        p['k_norm'] = jnp.tile(gains['gqa_k_norm'][i][None], (1, 2))
    return p


def _even_fwd(x, p, cs, swap, tag):
    xn = _rms_fwd(x, p['mix_norm'], tag + "_norm")
    za = _mm(xn, p['w_a'], name=tag + "_in_a")
    zb = _mm(xn, p['w_b'], out_dtype=BF16, name=tag + "_in_b")
    cq, ckv, krr = _mla_prep(za, p['q_norm'], p['kv_norm'], cs, swap, tag + "_mla_prep")
    q_raw = _mm(cq, p['w_uq'], name=tag + "_uq")
    k_pad = _mm(ckv, p['w_uk'], name=tag + "_uk")
    v = _mm(ckv, p['w_uv'], out_dtype=BF16, name=tag + "_uv")
    qh, kh = _mla_qk(q_raw, k_pad, krr, cs, swap, tag + "_mla_qk")
    o_a, lse_a = _flash_fwd(qh, kh, v, name=tag + "_mla_attn", **dict(_MLA, bq=_MLA_FWD_BQ))
    og, lg = [], []
    for grp in range(DIL_GROUPS):
        o, l = _dil_fwd(zb, grp, f"{tag}_dil{grp}")
        og.append(o)
        lg.append(l)
    o_b, lt = _dil_combine(og, lg, tag + "_dil_merge")
    ocat = jnp.concatenate([o_a, o_b], axis=1)
    x1 = _mm(ocat, p['w_out'], add=x, name=tag + "_out")
    saved = dict(x=x, xn=xn, za=za, zb=zb, cq=cq, ckv=ckv, qh=qh, kh=kh, v=v, lse_a=lse_a, og=og, lg=lg, lt=lt,
                 ocat=ocat)
    return x1, saved


def _even_bwd(dx1, p, sv, cs, swap, seg64, tag):
    docat = _mm(dx1, p['w_out'], mode="nt", name=tag + "_out_dx")
    d_w_out = _mm(sv['ocat'], dx1, mode="tn", name=tag + "_out_dw")
    n_a = MLA_HEADS * MLA_V
    do_a = docat[:, :n_a].astype(BF16)
    res = _dil_combine_bwd(docat[:, n_a:], sv['og'], sv['lg'], sv['lt'], seg64, tag + "_dil_merge_bwd")
    dqs, dks, dvs = [], [], []
    for grp in range(DIL_GROUPS):
        dq, dk, dv = _dil_bwd(sv['zb'], res[grp], sv['lg'][grp], res[3 + grp], grp, f"{tag}_dil{grp}_bwd")
        dqs.append(dq)
        dks.append(dk)
        dvs.append(dv)
    dzb = jnp.concatenate(dqs + dks + dvs, axis=1).astype(BF16)
    dqh, dkh, dv = _flash_bwd(sv['qh'], sv['kh'], sv['v'], sv['ocat'][:, :n_a], do_a, sv['lse_a'],
                              name=tag + "_mla_attn_bwd", **_MLA)
    dq_raw, dkh, dkrr = _mla_qk_bwd(dqh, dkh, cs, swap, tag + "_mla_qk_bwd")
    dcq = _mm(dq_raw, p['w_uq'], mode="nt", name=tag + "_uq_dx")
    d_w_uq = _mm(sv['cq'], dq_raw, mode="tn", name=tag + "_uq_dw")
    dckv = _mm(dkh, p['w_uk'], mode="nt", name=tag + "_uk_dx")
    dckv = _mm(dv, p['w_uv'], mode="nt", add=dckv, name=tag + "_uv_dx")
    d_w_uk = _mm(sv['ckv'], dkh, mode="tn", name=tag + "_uk_dw")
    d_w_uv = _mm(sv['ckv'], dv, mode="tn", name=tag + "_uv_dw")
    dza, d_gq, d_gkv = _mla_prep_bwd(sv['za'], cs, dcq, dckv, dkrr, p['q_norm'], p['kv_norm'], swap,
                                     tag + "_mla_prep_bwd")
    dxn = _mm(dza, p['w_a'], mode="nt", name=tag + "_in_a_dx")
    dxn = _mm(dzb, p['w_b'], mode="nt", add=dxn, name=tag + "_in_b_dx")
    d_w_a = _mm(sv['xn'], dza, mode="tn", name=tag + "_in_a_dw")
    d_w_b = _mm(sv['xn'], dzb, mode="tn", name=tag + "_in_b_dw")
    dx, d_g = _rms_bwd(sv['x'], p['mix_norm'], dxn, dx1, tag + "_norm_bwd")
    d_w_in = jnp.concatenate([d_w_a[:, :640], d_w_a[:, 704:736], d_w_b], axis=1)
    d_uq = d_w_uq.reshape(MLA_Q_RANK, MLA_HEADS, MLA_DK)[:, :, :MLA_NOPE + MLA_ROPE]
    d_ukv = jnp.concatenate([d_w_uk.reshape(MLA_KV_RANK, MLA_HEADS, MLA_DK)[:, :, :MLA_NOPE],
                             d_w_uv.reshape(MLA_KV_RANK, MLA_HEADS, MLA_V)], axis=2)
    grads = dict(w_in_ab=d_w_in, mla_w_uq=d_uq, mla_w_ukv=d_ukv, w_out_ab=d_w_out, mix_norm_ab=d_g[0],
                 mla_q_norm=d_gq[0], mla_kv_norm=d_gkv[0])
    return dx, grads


def _odd_fwd(x, p, cs, seg64, swap, tag):
    xn = _rms_fwd(x, p['mix_norm'], tag + "_norm")
    q_raw = _mm(xn, p['w_q'], name=tag + "_q")
    kv_raw = _mm(xn, p['w_kv'], name=tag + "_kv")
    qh, kh, v = _gqa_prep(q_raw, kv_raw, cs, p['q_norm'], p['k_norm'], seg64, swap, tag + "_gqa_prep")
    o, lse = _flash_fwd(qh, kh, v, name=tag + "_gqa_attn", **_GQA)
    x1 = _mm(o, p['w_o'], add=x, name=tag + "_o")
    return x1, dict(x=x, xn=xn, q_raw=q_raw, kv_raw=kv_raw, qh=qh, kh=kh, v=v, o=o, lse=lse)


def _odd_bwd(dx1, p, sv, cs, seg64, swap, tag):
    do = _mm(dx1, p['w_o'], mode="nt", out_dtype=BF16, name=tag + "_o_dx")
    d_w_o = _mm(sv['o'], dx1, mode="tn", name=tag + "_o_dw")
    dqh, dkh, dv = _flash_bwd(sv['qh'], sv['kh'], sv['v'], sv['o'], do, sv['lse'], name=tag + "_gqa_attn_bwd",
                              **_GQA)
    dq_raw, dkv_raw, d_gq, d_gk = _gqa_prep_bwd(sv['q_raw'], sv['kv_raw'], cs, dqh, dkh, dv, p['q_norm'],
                                                p['k_norm'], seg64, swap, tag + "_gqa_prep_bwd")
    dxn = _mm(dq_raw, p['w_q'], mode="nt", name=tag + "_q_dx")
    dxn = _mm(dkv_raw, p['w_kv'], mode="nt", add=dxn, name=tag + "_kv_dx")
    d_w_q = _mm(sv['xn'], dq_raw, mode="tn", name=tag + "_q_dw")
    d_w_kv = _mm(sv['xn'], dkv_raw, mode="tn", name=tag + "_kv_dw")
    dx, d_g = _rms_bwd(sv['x'], p['mix_norm'], dxn, dx1, tag + "_norm_bwd")
    grads = dict(gqa_w_q=d_w_q, gqa_w_kv=d_w_kv, gqa_w_o=d_w_o, mix_norm_c=d_g[0],
                 gqa_q_norm=d_gq[0, :GQA_HEAD_DIM] + d_gq[0, GQA_HEAD_DIM:],
                 gqa_k_norm=d_gk[0, :GQA_HEAD_DIM] + d_gk[0, GQA_HEAD_DIM:])
    return dx, grads


def _ffn_fwd(x, p, tag):
    xn = _rms_fwd(x, p['ffn_norm'], tag + "_ffn_norm")
    h = _mm(xn, p['ffn_w_in'], out_dtype=BF16, name=tag + "_ffn_in")
    a = _swiglu(h, tag + "_swiglu")
    x2 = _mm(a, p['ffn_w_out'], add=x, name=tag + "_ffn_out")
    return x2, dict(x=x, xn=xn, h=h, a=a)


def _ffn_bwd(dx2, p, sv, tag):
    da = _mm(dx2, p['ffn_w_out'], mode="nt", out_dtype=BF16, name=tag + "_ffn_out_dx")
    d_w_out = _mm(sv['a'], dx2, mode="tn", name=tag + "_ffn_out_dw")
    dh = _swiglu_bwd(sv['h'], da, tag + "_swiglu_bwd")
    d_w_in = _mm(sv['xn'], dh, mode="tn", name=tag + "_ffn_in_dw")
    dxn = _mm(dh, p['ffn_w_in'], mode="nt", name=tag + "_ffn_in_dx")
    dx, d_g = _rms_bwd(sv['x'], p['ffn_norm'], dxn, dx2, tag + "_ffn_norm_bwd")
    return dx, d_w_in, d_w_out, d_g[0]


EVEN_MATS = ('w_in_ab', 'mla_w_uq', 'mla_w_ukv', 'w_out_ab')
ODD_MATS = ('gqa_w_q', 'gqa_w_kv', 'gqa_w_o')
FFN_MATS = ('ffn_w_in', 'ffn_w_out')


def _schedule(x, target, gains, full_of_pack, rest_grads_ready):
    s = x.shape[0]
    cs_mla, cs_gqa = _rope_tables(s)
    swap, seg64 = _swap_matrix(), _seg_matrix(GQA_HEAD_DIM)
    params, saved, full = [], [], None
    for layer in range(4):
        tag = f"l{layer}"
        if layer < 2:
            full = full_of_pack(layer, x)
        p = _layer_params(layer, full, gains)
        if layer % 2 == 0:
            x, sv = _even_fwd(x, p, cs_mla, swap, tag)
        else:
            x, sv = _odd_fwd(x, p, cs_gqa, seg64, swap, tag)
        x, sv_f = _ffn_fwd(x, p, tag)
        params.append(p)
        saved.append((sv, sv_f))
    dx, loss_row, d_final = _loss_head(x, target, gains['final_norm'][None], "loss_head")

    per_layer, rest = {}, None
    for layer in reversed(range(4)):
        p, (sv, sv_f), tag = params[layer], saved[layer], f"l{layer}"
        if layer == 0:
            rest = {n: per_layer[2][n][None] for n in EVEN_MATS}
            rest.update({n: jnp.stack([per_layer[1][n], per_layer[3][n]], axis=0) for n in ODD_MATS})
            rest.update({n: jnp.stack([per_layer[l][n] for l in (1, 2, 3)], axis=0) for n in FFN_MATS})
            token = rest_grads_ready(rest)
            if token is not None:
                p = dict(p, ffn_w_out=p['ffn_w_out'] + token[0, 0].astype(p['ffn_w_out'].dtype))
        dx, d_ffn_in, d_ffn_out, d_ffn_g = _ffn_bwd(dx, p, sv_f, tag)
        if layer % 2 == 0:
            dx, g = _even_bwd(dx, p, sv, cs_mla, swap, seg64, tag)
        else:
            dx, g = _odd_bwd(dx, p, sv, cs_gqa, seg64, swap, tag)
        g.update(ffn_w_in=d_ffn_in, ffn_w_out=d_ffn_out, ffn_norm=d_ffn_g)
        per_layer[layer] = g

    first = {n: per_layer[0][n][None] for n in EVEN_MATS + FFN_MATS}
    small = {'final_norm': d_final[0], 'ffn_norm': jnp.stack([per_layer[l]['ffn_norm'] for l in range(4)], axis=0)}
    for n in ('mix_norm_ab', 'mla_q_norm', 'mla_kv_norm'):
        small[n] = jnp.stack([per_layer[0][n], per_layer[2][n]], axis=0)
    for n in ('mix_norm_c', 'gqa_q_norm', 'gqa_k_norm'):
        small[n] = jnp.stack([per_layer[1][n], per_layer[3][n]], axis=0)
    return loss_row, dx, first, rest, small


def _core_sums(grads, pack, core, tag):
    packed = _pack_full(grads, F32, pack)
    other = _sibling_swap_halves(packed, tag)
    return packed, other, _chip_sum(packed, other, core, tag)


def _finish_reduce_scatter(packed, other, recv, pack, chip, core, tag):
    return _unpack_blocks(_join_halves(_final_sum(packed, other, recv, chip, core, tag), core, tag), pack)


def _step(x, target, w, m, v):
    big_names = [n for n, _, _ in BIG]
    chip = 2 * lax.axis_index("x") + lax.axis_index("y")
    core = lax.axis_index("c")

    gains = {n: w[n] for n, _ in SMALL if n != 'mix_norm_c'}
    c_cols = w['mix_norm_c'].shape[1]
    own_c = lax.dynamic_update_slice(jnp.zeros((2, 4 * c_cols), F32), w['mix_norm_c'], (0, chip * c_cols))
    gains['mix_norm_c'] = _unpack_small(_all_reduce_packet(_pack_small(
        {'mix_norm_c': own_c * 0.5}, jnp.zeros((LANES,), F32))))[1]['mix_norm_c']

    gathered0 = _all_gather_weights(_pack_blocks(w, BF16, 0))
    packed1, gathered0 = lax.optimization_barrier((_pack_blocks(w, BF16, 1), gathered0))
    ag_send, ag_recv, p_thru, g_thru, ag_token = _gather_start(packed1, "gather_rest")
    gains['mix_norm_ab'] = gains['mix_norm_ab'] + ag_token[0, 0]
    full0 = _unpack_gathered(gathered0, 0)

    def full_of_pack(pack, after):
        if pack == 0:
            return full0
        landed = _gather_wait(ag_send, ag_recv, p_thru, g_thru, after, "gather_rest")
        return _unpack_gathered(landed, 1, own=packed1, chip=chip)

    rs = {}

    def rest_grads_ready(rest):
        rs['packed'], rs['other'], sums = _core_sums(rest, 1, core, "grad_rest")
        rs['send'], rs['recv'], rs['t'], rs['b'], token = _send_chip_sums_start(sums, "grad_rest")
        return token

    loss_row, dx, first, rest, small = _schedule(x[0], target[0], gains, full_of_pack, rest_grads_ready)
    recv1 = _send_chip_sums_wait(rs['send'], rs['recv'], rs['t'], rs['b'], dx, "grad_rest")
    g_rest = _finish_reduce_scatter(rs['packed'], rs['other'], recv1, 1, chip, core, "grad_rest")
    packed0, other0, sums0 = _core_sums(first, 0, core, "grad_first")
    g_first = _finish_reduce_scatter(packed0, other0, _send_chip_sums(sums0, "grad_first"), 0, chip, core,
                                     "grad_first")
    g_blocks = {n: (jnp.concatenate([g_first[n], g_rest[n]], axis=0) if n in g_first else g_rest[n])
                for n in big_names}

    loss, g_small = _unpack_small(_all_reduce_packet(_pack_small(small, loss_row[0])))
    g_small['mix_norm_c'] = lax.dynamic_slice(g_small['mix_norm_c'], (0, chip * c_cols), (2, c_cols))

    out_g, out_d, out_m, out_v = {}, {}, {}, {}
    for n in big_names:
        shape = w[n].shape
        cols = shape[-1]
        d_, m_, v_ = _adamw(w[n].reshape(-1, cols), g_blocks[n].reshape(-1, cols), m[n].reshape(-1, cols),
                            v[n].reshape(-1, cols), "adamw_" + n)
        out_g[n], out_d[n], out_m[n], out_v[n] = g_blocks[n], d_.reshape(shape), m_.reshape(shape), v_.reshape(shape)
    for n, _ in SMALL:
        shape = w[n].shape
        as2d = (lambda t: t.reshape(1, -1)) if len(shape) == 1 else (lambda t: t)
        d_, m_, v_ = _adamw(as2d(w[n]), as2d(g_small[n]), as2d(m[n]), as2d(v[n]), "adamw_" + n)
        out_g[n], out_d[n], out_m[n], out_v[n] = g_small[n], d_.reshape(shape), m_.reshape(shape), v_.reshape(shape)
    return (loss, dx[None], *[out_g[n] for n in WEIGHTS], *[out_d[n] for n in WEIGHTS],
            *[out_m[n] for n in WEIGHTS], *[out_v[n] for n in WEIGHTS])


def kernel(x, mix_norm_ab, w_in_ab, mla_q_norm, mla_kv_norm, mla_w_uq, mla_w_ukv, w_out_ab, mix_norm_c, gqa_w_q, gqa_w_kv, gqa_q_norm, gqa_k_norm, gqa_w_o, ffn_norm, ffn_w_in, ffn_w_out, final_norm, loss_target, m_mix_norm_ab, m_w_in_ab, m_mla_q_norm, m_mla_kv_norm, m_mla_w_uq, m_mla_w_ukv, m_w_out_ab, m_mix_norm_c, m_gqa_w_q, m_gqa_w_kv, m_gqa_q_norm, m_gqa_k_norm, m_gqa_w_o, m_ffn_norm, m_ffn_w_in, m_ffn_w_out, m_final_norm, v_mix_norm_ab, v_w_in_ab, v_mla_q_norm, v_mla_kv_norm, v_mla_w_uq, v_mla_w_ukv, v_w_out_ab, v_mix_norm_c, v_gqa_w_q, v_gqa_w_kv, v_gqa_q_norm, v_gqa_k_norm, v_gqa_w_o, v_ffn_norm, v_ffn_w_in, v_ffn_w_out, v_final_norm):
    w = dict(zip(WEIGHTS, (mix_norm_ab, w_in_ab, mla_q_norm, mla_kv_norm, mla_w_uq, mla_w_ukv, w_out_ab, mix_norm_c,
                           gqa_w_q, gqa_w_kv, gqa_q_norm, gqa_k_norm, gqa_w_o, ffn_norm, ffn_w_in, ffn_w_out,
                           final_norm)))
    m = dict(zip(WEIGHTS, (m_mix_norm_ab, m_w_in_ab, m_mla_q_norm, m_mla_kv_norm, m_mla_w_uq, m_mla_w_ukv,
                           m_w_out_ab, m_mix_norm_c, m_gqa_w_q, m_gqa_w_kv, m_gqa_q_norm, m_gqa_k_norm, m_gqa_w_o,
                           m_ffn_norm, m_ffn_w_in, m_ffn_w_out, m_final_norm)))
    v = dict(zip(WEIGHTS, (v_mix_norm_ab, v_w_in_ab, v_mla_q_norm, v_mla_kv_norm, v_mla_w_uq, v_mla_w_ukv,
                           v_w_out_ab, v_mix_norm_c, v_gqa_w_q, v_gqa_w_kv, v_gqa_q_norm, v_gqa_k_norm, v_gqa_w_o,
                           v_ffn_norm, v_ffn_w_in, v_ffn_w_out, v_final_norm)))
    return _step(x, loss_target, w, m, v)
```

```python
import math

import numpy as np
import jax
import jax.numpy as jnp
from jax import lax
from jax.experimental import pallas as pl
from jax.experimental.pallas import tpu as pltpu

F32 = jnp.float32
BF16 = jnp.bfloat16
MESH = pl.DeviceIdType.MESH

VMEM_LIMIT_BYTES = 56 * 1024 * 1024
LANES = 128

D_MODEL = 1024
NORM_EPS = 1e-6
ROPE_THETA = 10000.0
NEG_INF = -1e30
GRID_W = 64

MLA_HEADS, MLA_Q_RANK, MLA_KV_RANK, MLA_NOPE, MLA_ROPE, MLA_V = 8, 384, 256, 64, 32, 64
MLA_DK = 128
DIL_PAIRS = ((128, 1), (512, 4), (2048, 16))
DIL_HALF, DIL_SLOTS, DIL_GROUPS, DIL_HEAD_DIM = 64, 4, 3, 64
DIL_HEADS = DIL_SLOTS * DIL_GROUPS
DIL_W = DIL_SLOTS * DIL_HEAD_DIM
GQA_HEADS, GQA_KV_HEADS, GQA_HEAD_DIM = 16, 4, 64
FFN_HIDDEN = 2816
IN_A = MLA_Q_RANK + MLA_KV_RANK + MLA_ROPE
IN_A_PAD = 768
IN_B = 3 * DIL_HEADS * DIL_HEAD_DIM

ADAM_LR, ADAM_B1, ADAM_B2, ADAM_EPS, ADAM_WD, ADAM_STEP = 0.001, 0.9, 0.999, 1e-08, 0.01, 10

LOG2E, LN2 = math.log2(math.e), math.log(2.0)
MLA_SCALE = (MLA_NOPE + MLA_ROPE) ** -0.5
GQA_SCALE = GQA_HEAD_DIM ** -0.5

WEIGHTS = ['mix_norm_ab', 'w_in_ab', 'mla_q_norm', 'mla_kv_norm', 'mla_w_uq', 'mla_w_ukv', 'w_out_ab', 'mix_norm_c',
           'gqa_w_q', 'gqa_w_kv', 'gqa_q_norm', 'gqa_k_norm', 'gqa_w_o', 'ffn_norm', 'ffn_w_in', 'ffn_w_out',
           'final_norm']
BIG = (('w_in_ab', (2, 1024, 744), 2), ('mla_w_uq', (2, 96, 8, 96), 1), ('mla_w_ukv', (2, 64, 8, 128), 1),
       ('w_out_ab', (2, 768, 256), 2), ('gqa_w_q', (2, 256, 1024), 1), ('gqa_w_kv', (2, 256, 512), 1),
       ('gqa_w_o', (2, 256, 1024), 1), ('ffn_w_in', (4, 1024, 1408), 2), ('ffn_w_out', (4, 704, 1024), 1))
PACK_W = 1024
SMALL = (('mix_norm_ab', (2, 1024)), ('mla_q_norm', (2, 384)), ('mla_kv_norm', (2, 256)), ('gqa_q_norm', (2, 64)),
         ('gqa_k_norm', (2, 64)), ('ffn_norm', (4, 1024)), ('final_norm', (1024,)), ('mix_norm_c', (2, 1024)))
PACKET_ROWS = 88


def _cparams(sem=None):
    return pltpu.CompilerParams(dimension_semantics=sem, vmem_limit_bytes=VMEM_LIMIT_BYTES)


def _pick(n, pref, mult=LANES):
    if n <= pref:
        return n
    for d in range(pref - pref % mult, 0, -mult):
        if n % d == 0:
            return d
    return n


_DIMS = {"nn": (((1,), (0,)), ((), ())), "nt": (((1,), (1,)), ((), ())), "tn": (((0,), (0,)), ((), ()))}


def _mm(a, b, *, mode="nn", add=None, out_dtype=F32, name="mm"):
    if mode == "nn":
        (m, k), (k2, n) = a.shape, b.shape
    elif mode == "nt":
        (m, k), (n, k2) = a.shape, b.shape
    else:
        (k, m), (k2, n) = a.shape, b.shape
    assert k == k2, (a.shape, b.shape, mode)
    if mode == "tn":
        deep = a.dtype == BF16 and b.dtype == BF16
        bm, bn, bk = _pick(m, 1408), _pick(n, 1024), _pick(k, 2048 if deep else 1024, 16)
    else:
        bm, bn, bk = _pick(m, 512, 16), _pick(n, 1408), _pick(k, 2816)
    nk = k // bk
    assert m % bm == 0 and n % bn == 0 and k % bk == 0
    has_add = add is not None

    def body(*refs):
        a_ref, b_ref = refs[0], refs[1]
        add_ref = refs[2] if has_add else None
        o_ref = refs[3 if has_add else 2]
        part = lax.dot_general(a_ref[...].astype(BF16), b_ref[...].astype(BF16), _DIMS[mode],
                               preferred_element_type=F32)

        def finish(r):
            if has_add:
                r = r + add_ref[...]
            o_ref[...] = r.astype(o_ref.dtype)

        if nk == 1:
            finish(part)
        else:
            acc_ref = refs[-1]
            kk = pl.program_id(2)

            @pl.when(kk == 0)
            def _():
                acc_ref[...] = part

            @pl.when(kk > 0)
            def _():
                acc_ref[...] += part

            @pl.when(kk == nk - 1)
            def _():
                finish(acc_ref[...])

    a_bytes, b_bytes = a.size * a.dtype.itemsize, b.size * b.dtype.itemsize
    n_outer = nk == 1 and (n // bn) * a_bytes + b_bytes < a_bytes + (m // bm) * b_bytes

    def at(f):
        return (lambda j, i, kk: f(i, j, kk)) if n_outer else f

    if mode == "nn":
        a_spec = pl.BlockSpec((bm, bk), at(lambda i, j, kk: (i, kk)))
        b_spec = pl.BlockSpec((bk, bn), at(lambda i, j, kk: (kk, j)))
    elif mode == "nt":
        a_spec = pl.BlockSpec((bm, bk), at(lambda i, j, kk: (i, kk)))
        b_spec = pl.BlockSpec((bn, bk), at(lambda i, j, kk: (j, kk)))
    else:
        a_spec = pl.BlockSpec((bk, bm), at(lambda i, j, kk: (kk, i)))
        b_spec = pl.BlockSpec((bk, bn), at(lambda i, j, kk: (kk, j)))
    o_spec = pl.BlockSpec((bm, bn), at(lambda i, j, kk: (i, j)))
    in_specs = [a_spec, b_spec] + ([o_spec] if has_add else [])
    args = (a, b) + ((add,) if has_add else ())
    grid = (n // bn, m // bm, nk) if n_outer else (m // bm, n // bn, nk)
    return pl.pallas_call(
        body, name=name, grid=grid, in_specs=in_specs, out_specs=o_spec,
        out_shape=jax.ShapeDtypeStruct((m, n), out_dtype),
        scratch_shapes=[pltpu.VMEM((bm, bn), F32)] if nk > 1 else [],
        compiler_params=_cparams(("parallel", "parallel", "arbitrary")),
    )(*args)


def _rows_call(fn, rows, consts, out_rows, out_accs=(), *, bs=256, name):
    s = rows[0].shape[0]
    bs = min(bs, s)
    assert s % bs == 0
    nr, nc, no, na = len(rows), len(consts), len(out_rows), len(out_accs)

    def body(*refs):
        vals = [r[...] for r in refs[:nr + nc]]
        outs = refs[nr + nc:]
        res = fn(*vals)
        if not isinstance(res, (tuple, list)):
            res = (res,)
        assert len(res) == no + na, (len(res), no, na)
        for r, v in zip(outs[:no], res[:no]):
            r[...] = v.astype(r.dtype)
        if na:
            i = pl.program_id(0)
            for r, v in zip(outs[no:], res[no:]):
                @pl.when(i == 0)
                def _(r=r, v=v):
                    r[...] = v

                @pl.when(i > 0)
                def _(r=r, v=v):
                    r[...] += v

    in_specs = [pl.BlockSpec((bs, a.shape[1]), lambda i: (i, 0)) for a in rows]
    in_specs += [pl.BlockSpec(c.shape, lambda i: (0, 0)) for c in consts]
    out_specs = [pl.BlockSpec((bs, c), lambda i: (i, 0)) for c, _ in out_rows]
    out_specs += [pl.BlockSpec(tuple(sh), lambda i: (0, 0)) for sh in out_accs]
    out_shape = [jax.ShapeDtypeStruct((s, c), dt) for c, dt in out_rows]
    out_shape += [jax.ShapeDtypeStruct(tuple(sh), F32) for sh in out_accs]
    res = pl.pallas_call(
        body, name=name, grid=(s // bs,), in_specs=in_specs, out_specs=out_specs, out_shape=out_shape,
        compiler_params=_cparams(("arbitrary",) if na else ("parallel",)),
    )(*rows, *consts)
    return res


def _rms(x, g):
    return x * lax.rsqrt(jnp.mean(x * x, axis=-1, keepdims=True) + NORM_EPS) * g


def _rms_bwd_math(x, g, dy):
    r = lax.rsqrt(jnp.mean(x * x, axis=-1, keepdims=True) + NORM_EPS)
    u = dy * g
    dx = r * u - x * (r * r * r) * jnp.mean(u * x, axis=-1, keepdims=True)
    dg = jnp.sum(dy * x * r, axis=0, keepdims=True)
    return dx, dg


def _rms_fwd(x, g, name):
    return _rows_call(lambda xv, gv: _rms(xv, gv), [x], [g], [(x.shape[1], BF16)], name=name)[0]


def _rms_bwd(x, g, dy, dres, name):
    def fn(xv, dyv, dresv, gv):
        dx, dg = _rms_bwd_math(xv, gv, dyv.astype(F32))
        return dx + dresv, dg
    return _rows_call(fn, [x, dy, dres], [g], [(x.shape[1], F32)], [(1, x.shape[1])], name=name)


def _chunkdot(x, m):
    outs = [jnp.dot(x[:, c:c + LANES], m, precision=lax.Precision.HIGHEST, preferred_element_type=F32)
            for c in range(0, x.shape[1], LANES)]
    return outs[0] if len(outs) == 1 else jnp.concatenate(outs, axis=1)


def _lanes(t, width):
    n = width // LANES
    return t if n == 1 else jnp.concatenate([t] * n, axis=1)


def _rope(x, cs, swap):
    w = x.shape[1]
    return x * _lanes(cs[:, :LANES], w) + _chunkdot(x, swap) * _lanes(cs[:, LANES:], w)


def _rope_t(dy, cs, swap):
    w = dy.shape[1]
    return dy * _lanes(cs[:, :LANES], w) + _chunkdot(dy * _lanes(cs[:, LANES:], w), swap)


def _swap_matrix():
    m = np.zeros((LANES, LANES), np.float32)
    for j in range(LANES):
        src = j + 16 if (j % 32) < 16 else j - 16
        m[src, j] = 1.0
    return jnp.asarray(m)


def _seg_matrix(seg):
    idx = np.arange(LANES) // seg
    return jnp.asarray((idx[:, None] == idx[None, :]).astype(np.float32))


def _rope_tables(s):
    pos = jnp.arange(s)

    def angles(p, dim):
        freqs = ROPE_THETA ** (-jnp.arange(0, dim, 2, dtype=F32) / dim)
        ang = p.astype(F32)[:, None] * freqs[None, :]
        return jnp.cos(ang), jnp.sin(ang)

    cos_t, sin_t = angles(pos, MLA_ROPE)
    one, zero = jnp.ones((s, 64), F32), jnp.zeros((s, 64), F32)
    mla = jnp.concatenate([one, cos_t, cos_t, one[:, :32], zero, -sin_t, sin_t, zero[:, :32]], axis=1)
    cos_r, sin_r = angles(pos // GRID_W, GQA_HEAD_DIM // 2)
    cos_c, sin_c = angles(pos % GRID_W, GQA_HEAD_DIM // 2)
    c64 = jnp.concatenate([cos_r, cos_r, cos_c, cos_c], axis=1)
    s64 = jnp.concatenate([-sin_r, sin_r, -sin_c, sin_c], axis=1)
    gqa = jnp.concatenate([c64, c64, s64, s64], axis=1)
    return mla, gqa


def _col_to_row(col):
    return jnp.transpose(jnp.broadcast_to(col, (col.shape[0], LANES)))[0:1, :]


def _stack_heads(ref, heads, d, dtype=None):
    parts = [ref[:, hd * d:(hd + 1) * d] for hd in heads]
    out = parts[0] if len(parts) == 1 else jnp.concatenate(parts, axis=0)
    return out if dtype is None else out.astype(dtype)


def _fill_v_ones(v_ref, va_ref, hb, dv):
    @pl.when(pl.program_id(1) == 0)
    def _():
        ones = jnp.ones((v_ref.shape[0], dv), BF16)
        for h in range(hb):
            va_ref[:, 2 * h * dv:(2 * h + 1) * dv] = v_ref[:, h * dv:(h + 1) * dv]
            va_ref[:, (2 * h + 1) * dv:(2 * h + 2) * dv] = ones


def _flash_fwd(q, k, v, *, R, dk, dv, hb, bq, bk, name):
    s = q.shape[0]
    g = k.shape[1] // dk
    ng = g // hb
    bq, bk = min(bq, s), min(bk, s)
    nq, nkb = s // bq, s // bk
    rb = R * bq

    def body(q_ref, k_ref, v_ref, o_ref, lse_ref, vat_ref):
        @pl.when(pl.program_id(1) == 0)
        def _():
            ones = jnp.ones((bk, dv), F32)
            for h in range(hb):
                for c in range(nkb):
                    blk = jnp.concatenate([v_ref[c * bk:(c + 1) * bk, h * dv:(h + 1) * dv].astype(F32), ones], axis=1)
                    vat_ref[c, 2 * h * dv:2 * (h + 1) * dv, :] = jnp.transpose(blk)

        head_sets = [[h * R + r for r in range(R)] for h in range(hb)]
        q_t = jnp.transpose(q_ref[...].astype(F32)).astype(BF16)
        qts = [jnp.concatenate([q_t[hd * dk:(hd + 1) * dk] for hd in heads], axis=1) if R > 1
               else q_t[heads[0] * dk:(heads[0] + 1) * dk] for heads in head_sets]

        def step(jj, carry):
            carry = list(carry)
            for u in range(unroll):
                j = jj * unroll + u
                rows = pl.ds(pl.multiple_of(j * bk, bk), bk)
                for h in range(hb):
                    m, acc = carry[h]
                    st = jnp.dot(k_ref[rows, h * dk:(h + 1) * dk], qts[h], preferred_element_type=F32)
                    m2 = jnp.maximum(m, jnp.max(st, axis=0, keepdims=True))
                    pt = jnp.exp2(st - m2)
                    pv = jnp.dot(vat_ref[j, 2 * h * dv:2 * (h + 1) * dv, :], pt, preferred_element_type=F32)
                    carry[h] = (m2, jnp.exp2(m - m2) * acc + pv)
            return tuple(carry)

        unroll = 4 if nkb % 4 == 0 else 1
        init = tuple((jnp.full((1, rb), NEG_INF, F32), jnp.zeros((2 * dv, rb), F32)) for _ in range(hb))
        final = lax.fori_loop(0, nkb // unroll, step, init)
        for h, heads in enumerate(head_sets):
            m, acc = final[h]
            l = acc[dv:dv + 1, :]
            o = jnp.transpose(acc / l)[:, :dv]
            row = m + jnp.log2(l)
            for r, hd in enumerate(heads):
                o_ref[:, hd * dv:(hd + 1) * dv] = o[r * bq:(r + 1) * bq].astype(o_ref.dtype)
                lse_ref[0, hd:hd + 1, :] = row[:, r * bq:(r + 1) * bq]

    return pl.pallas_call(
        body, name=name, grid=(ng, nq),
        in_specs=[pl.BlockSpec((bq, hb * R * dk), lambda gi, i: (i, gi)),
                  pl.BlockSpec((s, hb * dk), lambda gi, i: (0, gi)),
                  pl.BlockSpec((s, hb * dv), lambda gi, i: (0, gi))],
        out_specs=[pl.BlockSpec((bq, hb * R * dv), lambda gi, i: (i, gi)),
                   pl.BlockSpec((1, hb * R, bq), lambda gi, i: (gi, 0, i))],
        out_shape=[jax.ShapeDtypeStruct((s, g * R * dv), BF16), jax.ShapeDtypeStruct((ng, hb * R, s), F32)],
        scratch_shapes=[pltpu.VMEM((nkb, 2 * hb * dv, bk), F32)],
        compiler_params=_cparams(("parallel", "arbitrary")),
    )(q, k, v)


def _flash_bwd(q, k, v, o, do, lse, *, R, dk, dv, hb, bq, bk, name):
    s = q.shape[0]
    g = k.shape[1] // dk
    ng = g // hb
    bq, bk = min(bq, s), min(bk, s)
    nq, nkb = s // bq, s // bk
    rb = R * bq

    def body(q_ref, k_ref, v_ref, o_ref, do_ref, lse_ref, dq_ref, dk_ref, dv_ref, va_ref, kt_ref):
        @pl.when(pl.program_id(1) == 0)
        def _():
            dk_ref[...] = jnp.zeros(dk_ref.shape, F32)
            dv_ref[...] = jnp.zeros(dv_ref.shape, F32)
            for c in range(nkb):
                kt_ref[c] = jnp.transpose(k_ref[c * bk:(c + 1) * bk, :].astype(F32)).astype(BF16)

        _fill_v_ones(v_ref, va_ref, hb, dv)
        lane = lax.broadcasted_iota(jnp.int32, (rb, dv), 1)
        head_sets = [[h * R + r for r in range(R)] for h in range(hb)]
        qss, doss, dosas, lrows = [], [], [], []
        for heads in head_sets:
            dos = _stack_heads(do_ref, heads, dv, BF16)
            delta = jnp.sum(dos.astype(F32) * _stack_heads(o_ref, heads, dv, F32), axis=1, keepdims=True)
            hi = delta.astype(BF16).astype(F32)
            lo = delta - hi
            qss.append(_stack_heads(q_ref, heads, dk))
            doss.append(dos)
            dosas.append(jnp.concatenate(
                [dos, jnp.where(lane == 0, -hi, jnp.where(lane == 1, -lo, 0.0)).astype(BF16)], axis=1))
            rows = [lse_ref[0, hd:hd + 1, :] for hd in heads]
            lrows.append(rows[0] if R == 1 else jnp.concatenate(rows, axis=1))

        def step(j, dqs):
            r0 = pl.multiple_of(j * bk, bk)
            hs = range(hb)
            kjs = [k_ref[pl.ds(r0, bk), h * dk:(h + 1) * dk] for h in hs]
            vas = [va_ref[pl.ds(r0, bk), 2 * h * dv:2 * (h + 1) * dv] for h in hs]
            sts = [lax.dot_general(kjs[h], qss[h], _DIMS["nt"], preferred_element_type=F32) for h in hs]
            dpts = [lax.dot_general(vas[h], dosas[h], _DIMS["nt"], preferred_element_type=F32) for h in hs]
            pts = [jnp.exp2(sts[h] - lrows[h]) for h in hs]
            pbs = [pts[h].astype(BF16) for h in hs]
            dsts = [(pts[h] * dpts[h]).astype(BF16) for h in hs]
            for h in hs:
                dv_ref[pl.ds(r0, bk), h * dv:(h + 1) * dv] += jnp.dot(pbs[h], doss[h], preferred_element_type=F32)
            for h in hs:
                dk_ref[pl.ds(r0, bk), h * dk:(h + 1) * dk] += jnp.dot(dsts[h], qss[h], preferred_element_type=F32)
            return tuple(dqs[h] + jnp.dot(kt_ref[j, h * dk:(h + 1) * dk, :], dsts[h], preferred_element_type=F32)
                         for h in hs)

        dqs = lax.fori_loop(0, nkb, step, tuple(jnp.zeros((dk, rb), F32) for _ in range(hb)))
        dq = jnp.transpose(jnp.concatenate(dqs, axis=0))
        for h, heads in enumerate(head_sets):
            for r, hd in enumerate(heads):
                dq_ref[:, hd * dk:(hd + 1) * dk] = dq[r * bq:(r + 1) * bq, h * dk:(h + 1) * dk]

    qspec = pl.BlockSpec((bq, hb * R * dk), lambda gi, i: (i, gi))
    ospec = pl.BlockSpec((bq, hb * R * dv), lambda gi, i: (i, gi))
    kspec = pl.BlockSpec((s, hb * dk), lambda gi, i: (0, gi))
    vspec = pl.BlockSpec((s, hb * dv), lambda gi, i: (0, gi))
    return pl.pallas_call(
        body, name=name, grid=(ng, nq),
        in_specs=[qspec, kspec, vspec, ospec, ospec, pl.BlockSpec((1, hb * R, bq), lambda gi, i: (gi, 0, i))],
        out_specs=[qspec, kspec, vspec],
        out_shape=[jax.ShapeDtypeStruct((s, g * R * dk), F32), jax.ShapeDtypeStruct((s, g * dk), F32),
                   jax.ShapeDtypeStruct((s, g * dv), F32)],
        scratch_shapes=[pltpu.VMEM((s, 2 * hb * dv), BF16), pltpu.VMEM((nkb, hb * dk, bk), BF16)],
        compiler_params=_cparams(("parallel", "arbitrary")),
    )(q, k, v, o, do, lse)


DIL_T = 1024
DIL_P = DIL_HALF
DIL_NCOL = IN_B // DIL_W


DIL_BATCH = 4


def _alibi_slope(head):
    return float(2.0 ** (-8.0 * (head + 1) / DIL_HEADS))


def _slot(sl_i):
    return slice(sl_i * DIL_HEAD_DIM, (sl_i + 1) * DIL_HEAD_DIM)


def _halo_specs(d, col, s, t):
    h = DIL_P * d
    per, last = t // h, s // h - 1
    return [pl.BlockSpec((h, DIL_W), lambda c: (jnp.maximum(c * per - 1, 0), col)),
            pl.BlockSpec((t, DIL_W), lambda c: (c, col)),
            pl.BlockSpec((h, DIL_W), lambda c: (jnp.minimum((c + 1) * per, last), col))]


def _staging(rows):
    return tuple(pltpu.VMEM((rows, LANES), F32) for _ in range(DIL_W // LANES))


def _stage(buf, refs):
    off = 0
    for r in refs:
        val = r[...].astype(F32)
        for j in range(DIL_W // LANES):
            buf[j][off:off + r.shape[0], :] = val[:, j * LANES:(j + 1) * LANES]
        off += r.shape[0]


def _unstage(buf, ref):
    ref[...] = jnp.concatenate([half[...] for half in buf], axis=1).astype(ref.dtype)


def _sub_tiles(d, t):
    return [(b * DIL_P * d + r, b * DIL_P) for b in range(t // (DIL_P * d)) for r in range(d)]


def _rows(start, size, d):
    return pl.ds(start, size, stride=d) if d > 1 else pl.ds(start, size)


def _strided(buf, start, size, d):
    return jnp.concatenate([half[_rows(start, size, d), :] for half in buf], axis=1)


def _put_strided(buf, start, d, val):
    for j in range(DIL_W // LANES):
        buf[j][_rows(start, val.shape[0], d), :] = val[:, j * LANES:(j + 1) * LANES]


def _band(u0, length, d, queries_wide):
    if queries_wide:
        shape = (3 * DIL_P, DIL_P)
        wide = u0 - DIL_P + lax.broadcasted_iota(jnp.int32, shape, 0)
        narrow = u0 + lax.broadcasted_iota(jnp.int32, shape, 1)
    else:
        shape = (DIL_P, 3 * DIL_P)
        narrow = u0 + lax.broadcasted_iota(jnp.int32, shape, 0)
        wide = u0 - DIL_P + lax.broadcasted_iota(jnp.int32, shape, 1)
    rel = jnp.abs(wide - narrow)
    valid = (rel <= DIL_HALF) & (wide >= 0) & (wide < length)
    return valid, rel.astype(F32) * float(d)


def _dil_fwd(zb, grp, name):
    s = zb.shape[0]
    d = DIL_PAIRS[grp][1]
    t = min(DIL_T, s)
    h = DIL_P * d
    scale = DIL_HEAD_DIM ** -0.5

    def body(q_ref, kp, kc, kn, vp, vc, vn, o_ref, lse_ref, qbuf, kbuf, vbuf, obuf, lbuf):
        _stage(qbuf, (q_ref,))
        _stage(kbuf, (kp, kc, kn))
        _stage(vbuf, (vp, vc, vn))
        u_step = pl.program_id(0) * (t // d)
        tiles = _sub_tiles(d, t)
        for g0 in range(0, len(tiles), DIL_BATCH):
            batch = tiles[g0:g0 + DIL_BATCH]
            masks = [_band(u_step + u, s // d, d, False) for _, u in batch]
            qs = [_strided(qbuf, row, DIL_P, d).astype(BF16) for row, _ in batch]
            ks = [_strided(kbuf, row, 3 * DIL_P, d).astype(BF16) for row, _ in batch]
            vs = [_strided(vbuf, row, 3 * DIL_P, d).astype(BF16) for row, _ in batch]
            chains = [(i, sl_i) for i in range(len(batch)) for sl_i in range(DIL_SLOTS)]
            scs = [lax.dot_general(qs[i][:, _slot(sl_i)], ks[i][:, _slot(sl_i)], _DIMS["nt"],
                                   preferred_element_type=F32) for i, sl_i in chains]
            scs = [jnp.where(masks[i][0], sc * scale - _alibi_slope(grp * DIL_SLOTS + sl_i) * masks[i][1], NEG_INF)
                   for (i, sl_i), sc in zip(chains, scs)]
            ms = [jnp.max(sc, axis=1, keepdims=True) for sc in scs]
            es = [jnp.exp(sc - m) for sc, m in zip(scs, ms)]
            dens = [jnp.sum(e, axis=1, keepdims=True) for e in es]
            outs = [jnp.dot((e / den).astype(BF16), vs[i][:, _slot(sl_i)], preferred_element_type=F32)
                    for (i, sl_i), e, den in zip(chains, es, dens)]
            lses = [jnp.broadcast_to(m + jnp.log(den), (DIL_P, DIL_HEAD_DIM)) for m, den in zip(ms, dens)]
            for i, (row, _) in enumerate(batch):
                pick = slice(i * DIL_SLOTS, (i + 1) * DIL_SLOTS)
                _put_strided(obuf, row, d, jnp.concatenate(outs[pick], axis=1))
                _put_strided(lbuf, row, d, jnp.concatenate(lses[pick], axis=1))
        _unstage(obuf, o_ref)
        _unstage(lbuf, lse_ref)

    own = pl.BlockSpec((t, DIL_W), lambda c: (c, 0))
    return pl.pallas_call(
        body, name=name, grid=(s // t,),
        in_specs=[pl.BlockSpec((t, DIL_W), lambda c: (c, grp))] + _halo_specs(d, 3 + grp, s, t)
        + _halo_specs(d, 6 + grp, s, t),
        out_specs=[own, own], out_shape=[jax.ShapeDtypeStruct((s, DIL_W), F32)] * 2,
        scratch_shapes=[_staging(t), _staging(t + 2 * h), _staging(t + 2 * h), _staging(t), _staging(t)],
        compiler_params=_cparams(("parallel",)),
    )(zb, zb, zb, zb, zb, zb, zb)


def _dil_bwd(zb, do, lse, dl, grp, name):
    s = zb.shape[0]
    d = DIL_PAIRS[grp][1]
    t = min(DIL_T, s)
    h = DIL_P * d
    scale = DIL_HEAD_DIM ** -0.5

    def chain_grads(qs, ks, vs, dos, lses, dls, masks):
        chains = [(i, sl_i) for i in range(len(qs)) for sl_i in range(DIL_SLOTS)]
        scs = [lax.dot_general(qs[i][:, _slot(sl_i)], ks[i][:, _slot(sl_i)], _DIMS["nt"],
                               preferred_element_type=F32) for i, sl_i in chains]
        dps = [lax.dot_general(dos[i][:, _slot(sl_i)], vs[i][:, _slot(sl_i)], _DIMS["nt"],
                               preferred_element_type=F32) for i, sl_i in chains]
        ps = [jnp.exp(jnp.where(masks[i][0], sc * scale - _alibi_slope(grp * DIL_SLOTS + sl_i) * masks[i][1],
                                NEG_INF) - lses[i][:, sl_i * DIL_HEAD_DIM:sl_i * DIL_HEAD_DIM + 1])
              for (i, sl_i), sc in zip(chains, scs)]
        dss = [(p * (dp - dls[i][:, sl_i * DIL_HEAD_DIM:sl_i * DIL_HEAD_DIM + 1]) * scale).astype(BF16)
               for (i, sl_i), p, dp in zip(chains, ps, dps)]
        return chains, ps, dss

    def dq_body(q_ref, kp, kc, kn, vp, vc, vn, do_ref, lse_ref, dl_ref, dq_ref, qbuf, kbuf, vbuf, dobuf, lsebuf,
                dlbuf, obuf):
        _stage(qbuf, (q_ref,))
        _stage(dobuf, (do_ref,))
        _stage(lsebuf, (lse_ref,))
        _stage(dlbuf, (dl_ref,))
        _stage(kbuf, (kp, kc, kn))
        _stage(vbuf, (vp, vc, vn))
        u_step = pl.program_id(0) * (t // d)
        tiles = _sub_tiles(d, t)
        for g0 in range(0, len(tiles), DIL_BATCH):
            batch = tiles[g0:g0 + DIL_BATCH]
            masks = [_band(u_step + u, s // d, d, False) for _, u in batch]
            narrow = [[_strided(b, row, DIL_P, d) for row, _ in batch] for b in (qbuf, dobuf, lsebuf, dlbuf)]
            ks = [_strided(kbuf, row, 3 * DIL_P, d).astype(BF16) for row, _ in batch]
            vs = [_strided(vbuf, row, 3 * DIL_P, d).astype(BF16) for row, _ in batch]
            chains, _, dss = chain_grads([a.astype(BF16) for a in narrow[0]], ks, vs,
                                         [a.astype(BF16) for a in narrow[1]], narrow[2], narrow[3], masks)
            outs = [jnp.dot(ds, ks[i][:, _slot(sl_i)], preferred_element_type=F32)
                    for (i, sl_i), ds in zip(chains, dss)]
            for i, (row, _) in enumerate(batch):
                _put_strided(obuf, row, d, jnp.concatenate(outs[i * DIL_SLOTS:(i + 1) * DIL_SLOTS], axis=1))
        _unstage(obuf, dq_ref)

    def dkv_body(k_ref, v_ref, qp, qc, qn, dop, doc, don, lp, lc, ln, dlp, dlc, dln, dk_ref, dv_ref,
                 kbuf, vbuf, qbuf, dobuf, lsebuf, dlbuf, dkbuf, dvbuf):
        _stage(kbuf, (k_ref,))
        _stage(vbuf, (v_ref,))
        _stage(qbuf, (qp, qc, qn))
        _stage(dobuf, (dop, doc, don))
        _stage(lsebuf, (lp, lc, ln))
        _stage(dlbuf, (dlp, dlc, dln))
        u_step = pl.program_id(0) * (t // d)
        tiles = _sub_tiles(d, t)
        for g0 in range(0, len(tiles), DIL_BATCH):
            batch = tiles[g0:g0 + DIL_BATCH]
            masks = [_band(u_step + u, s // d, d, True) for _, u in batch]
            ks = [_strided(kbuf, row, DIL_P, d).astype(BF16) for row, _ in batch]
            vs = [_strided(vbuf, row, DIL_P, d).astype(BF16) for row, _ in batch]
            wide_ = [[_strided(b, row, 3 * DIL_P, d) for row, _ in batch] for b in (qbuf, dobuf, lsebuf, dlbuf)]
            qs, dos = [a.astype(BF16) for a in wide_[0]], [a.astype(BF16) for a in wide_[1]]
            chains, ps, dss = chain_grads(qs, ks, vs, dos, wide_[2], wide_[3], masks)
            dvs = [lax.dot_general(p.astype(BF16), dos[i][:, _slot(sl_i)], _DIMS["tn"], preferred_element_type=F32)
                   for (i, sl_i), p in zip(chains, ps)]
            dks = [lax.dot_general(ds, qs[i][:, _slot(sl_i)], _DIMS["tn"], preferred_element_type=F32)
                   for (i, sl_i), ds in zip(chains, dss)]
            for i, (row, _) in enumerate(batch):
                pick = slice(i * DIL_SLOTS, (i + 1) * DIL_SLOTS)
                _put_strided(dkbuf, row, d, jnp.concatenate(dks[pick], axis=1))
                _put_strided(dvbuf, row, d, jnp.concatenate(dvs[pick], axis=1))
        _unstage(dkbuf, dk_ref)
        _unstage(dvbuf, dv_ref)

    def zcur(col):
        return pl.BlockSpec((t, DIL_W), lambda c: (c, col))

    own = pl.BlockSpec((t, DIL_W), lambda c: (c, 0))
    out = jax.ShapeDtypeStruct((s, DIL_W), F32)
    tile, wide = _staging(t), _staging(t + 2 * h)
    dq = pl.pallas_call(
        dq_body, name=name + "_dq", grid=(s // t,),
        in_specs=[zcur(grp)] + _halo_specs(d, 3 + grp, s, t) + _halo_specs(d, 6 + grp, s, t) + [own, own, own],
        out_specs=own, out_shape=out, scratch_shapes=[tile, wide, wide, tile, tile, tile, tile],
        compiler_params=_cparams(("parallel",)),
    )(zb, zb, zb, zb, zb, zb, zb, do, lse, dl)
    own3 = _halo_specs(d, 0, s, t)
    dk, dv = pl.pallas_call(
        dkv_body, name=name + "_dkv", grid=(s // t,),
        in_specs=[zcur(3 + grp), zcur(6 + grp)] + _halo_specs(d, grp, s, t) + own3 + own3 + own3,
        out_specs=[own, own], out_shape=[out, out],
        scratch_shapes=[tile, tile, wide, wide, wide, wide, tile, tile],
        compiler_params=_cparams(("parallel",)),
    )(zb, zb, zb, zb, zb, do, do, do, lse, lse, lse, dl, dl, dl)
    return dq, dk, dv


def _dil_combine(os_, ls_, name):
    def fn(o0, o1, o2, l0, l1, l2):
        m = jnp.maximum(jnp.maximum(l0, l1), l2)
        e0, e1, e2 = jnp.exp(l0 - m), jnp.exp(l1 - m), jnp.exp(l2 - m)
        den = e0 + e1 + e2
        comb = (e0 / den) * o0 + (e1 / den) * o1 + (e2 / den) * o2
        return comb, m + jnp.log(den)
    return _rows_call(fn, list(os_) + list(ls_), [], [(DIL_W, BF16), (DIL_W, F32)], name=name)


def _dil_combine_bwd(dcomb, os_, ls_, lt, seg64, name):
    def fn(dc, o0, o1, o2, l0, l1, l2, ltv, seg):
        w = [jnp.exp(l - ltv) for l in (l0, l1, l2)]
        comb = w[0] * o0 + w[1] * o1 + w[2] * o2
        t = _chunkdot(dc * comb, seg)
        return [wg * dc for wg in w] + [wg * t for wg in w]
    return _rows_call(fn, [dcomb] + list(os_) + list(ls_) + [lt], [seg64],
                      [(DIL_W, BF16)] * 3 + [(DIL_W, F32)] * 3, name=name)


def _mla_prep(za, gq, gkv, cs, swap, name):
    def fn(z, csv, gqv, gkvv, sw):
        return (_rms(z[:, :MLA_Q_RANK], gqv), _rms(z[:, MLA_Q_RANK:640], gkvv), _rope(z[:, 640:], csv, sw))
    return _rows_call(fn, [za, cs], [gq, gkv, swap], [(MLA_Q_RANK, BF16), (MLA_KV_RANK, BF16), (LANES, F32)],
                      name=name)


def _mla_prep_bwd(za, cs, dcq, dckv, dkr, gq, gkv, swap, name):
    def fn(z, csv, dcqv, dckvv, dkrv, gqv, gkvv, sw):
        d1, dg1 = _rms_bwd_math(z[:, :MLA_Q_RANK], gqv, dcqv)
        d2, dg2 = _rms_bwd_math(z[:, MLA_Q_RANK:640], gkvv, dckvv)
        d3 = _rope_t(dkrv, csv, sw)
        return jnp.concatenate([d1, d2, d3], axis=1), dg1, dg2
    return _rows_call(fn, [za, cs, dcq, dckv, dkr], [gq, gkv, swap], [(IN_A_PAD, BF16)],
                      [(1, MLA_Q_RANK), (1, MLA_KV_RANK)], name=name)


def _mla_qk(q_raw, k_pad, krr, cs, swap, name):
    w = MLA_HEADS * MLA_DK

    def fn(qv, kv, krv, csv, sw):
        return _rope(qv, csv, sw) * (MLA_SCALE * LOG2E), kv + _lanes(krv, w)
    return _rows_call(fn, [q_raw, k_pad, krr, cs], [swap], [(w, BF16), (w, BF16)], name=name)


def _mla_qk_bwd(dqh, dkh, cs, swap, name):
    w = MLA_HEADS * MLA_DK

    def fn(dq, dk, csv, sw):
        dk = dk * LN2
        acc = dk[:, :LANES]
        for h in range(1, MLA_HEADS):
            acc = acc + dk[:, h * LANES:(h + 1) * LANES]
        lane = lax.broadcasted_iota(jnp.int32, acc.shape, 1)
        acc = jnp.where((lane >= MLA_NOPE) & (lane < MLA_NOPE + MLA_ROPE), acc, 0.0)
        return _rope_t(dq * MLA_SCALE, csv, sw), dk, acc
    return _rows_call(fn, [dqh, dkh, cs], [swap], [(w, BF16), (w, BF16), (LANES, F32)], name=name)


def _head_norm(t, g2, seg):
    r = lax.rsqrt(_chunkdot(t * t, seg) * (1.0 / GQA_HEAD_DIM) + NORM_EPS)
    return t * r * _lanes(g2, t.shape[1]), r


def _head_norm_bwd(t, g2, seg, dn):
    w = t.shape[1]
    r = lax.rsqrt(_chunkdot(t * t, seg) * (1.0 / GQA_HEAD_DIM) + NORM_EPS)
    u = dn * _lanes(g2, w)
    dt = r * u - t * (r * r * r) * (_chunkdot(u * t, seg) * (1.0 / GQA_HEAD_DIM))
    dgw = jnp.sum(dn * t * r, axis=0, keepdims=True)
    dg = dgw[:, :LANES]
    for c in range(LANES, w, LANES):
        dg = dg + dgw[:, c:c + LANES]
    return dt, dg


def _gqa_prep(q_raw, kv_raw, cs, gq2, gk2, seg, swap, name):
    kw = GQA_KV_HEADS * GQA_HEAD_DIM

    def fn(qv, kvv, csv, gqv, gkv, sg, sw):
        qn, _ = _head_norm(qv, gqv, sg)
        kn, _ = _head_norm(kvv[:, :kw], gkv, sg)
        return _rope(qn, csv, sw) * (GQA_SCALE * LOG2E), _rope(kn, csv, sw), kvv[:, kw:]
    return _rows_call(fn, [q_raw, kv_raw, cs], [gq2, gk2, seg, swap],
                      [(GQA_HEADS * GQA_HEAD_DIM, BF16), (kw, BF16), (kw, BF16)], name=name)


def _gqa_prep_bwd(q_raw, kv_raw, cs, dqh, dkh, dv, gq2, gk2, seg, swap, name):
    kw = GQA_KV_HEADS * GQA_HEAD_DIM

    def fn(qv, kvv, csv, dq, dk, dvv, gqv, gkv, sg, sw):
        dqr, dgq = _head_norm_bwd(qv, gqv, sg, _rope_t(dq * GQA_SCALE, csv, sw))
        dkr, dgk = _head_norm_bwd(kvv[:, :kw], gkv, sg, _rope_t(dk * LN2, csv, sw))
        return dqr, jnp.concatenate([dkr, dvv], axis=1), dgq, dgk
    return _rows_call(fn, [q_raw, kv_raw, cs, dqh, dkh, dv], [gq2, gk2, seg, swap],
                      [(GQA_HEADS * GQA_HEAD_DIM, BF16), (2 * kw, BF16)], [(1, LANES), (1, LANES)], name=name)


def _swiglu(h, name):
    def fn(hv):
        gate, up = hv[:, :FFN_HIDDEN].astype(F32), hv[:, FFN_HIDDEN:].astype(F32)
        return gate / (1.0 + jnp.exp(-gate)) * up
    return _rows_call(fn, [h], [], [(FFN_HIDDEN, BF16)], name=name)[0]


def _swiglu_bwd(h, da, name):
    def fn(hv, dav):
        gate, up = hv[:, :FFN_HIDDEN].astype(F32), hv[:, FFN_HIDDEN:].astype(F32)
        dav = dav.astype(F32)
        sig = 1.0 / (1.0 + jnp.exp(-gate))
        silu = gate * sig
        dgate = dav * up * (sig + silu * (1.0 - sig))
        return jnp.concatenate([dgate, dav * silu], axis=1)
    return _rows_call(fn, [h, da], [], [(2 * FFN_HIDDEN, BF16)], name=name)[0]


def _loss_head(x, target, g, name):
    dm = x.shape[1]

    def fn(xv, tv, gv):
        err = _rms(xv, gv) - tv
        loss = 0.5 * jnp.sum(err * err) / dm
        dx, dg = _rms_bwd_math(xv, gv, err * (1.0 / dm))
        return dx, jnp.zeros((1, LANES), F32) + loss, dg
    return _rows_call(fn, [x, target], [g], [(dm, F32)], [(1, LANES), (1, dm)], name=name)


def _adamw(w, g, m, v, name):
    def fn(wv, gv, mv, vv):
        m2 = ADAM_B1 * mv + (1.0 - ADAM_B1) * gv
        v2 = ADAM_B2 * vv + (1.0 - ADAM_B2) * (gv * gv)
        m_hat = m2 / (1.0 - ADAM_B1 ** ADAM_STEP)
        v_hat = v2 / (1.0 - ADAM_B2 ** ADAM_STEP)
        return -ADAM_LR * (m_hat / (jnp.sqrt(v_hat) + ADAM_EPS) + ADAM_WD * wv), m2, v2
    c = w.shape[1]
    return _rows_call(fn, [w, g, m, v], [], [(c, F32)] * 3, bs=_pick(w.shape[0], 256, 8), name=name)


HBM_SPEC = pl.BlockSpec(memory_space=pltpu.HBM)
VMEM_SPEC = pl.BlockSpec(memory_space=pltpu.VMEM)


def _position():
    return lax.axis_index("x"), lax.axis_index("y"), lax.axis_index("c")


def _other_chips(x, y):
    return [(1 - x, y), (x, 1 - y), (1 - x, 1 - y)]


HALF_W = PACK_W // 2


def _cols(c):
    return pl.ds(pl.multiple_of(c * HALF_W, HALF_W), HALF_W)


def _all_gather_weights(packed):
    rows = packed.shape[0]

    def body(p_ref, g_ref, send_sems, recv_sems, local_sem):
        x, y, c = _position()
        chips = _other_chips(x, y)

        def half(chip, hc):
            return g_ref.at[2 * chip[0] + chip[1], :, _cols(hc)]

        def copy(j, src, dst, to):
            return pltpu.make_async_remote_copy(src_ref=src, dst_ref=dst, send_sem=send_sems.at[j],
                                                recv_sem=recv_sems.at[j], device_id=to, device_id_type=MESH)

        mine = pltpu.make_async_copy(p_ref, g_ref.at[2 * x + y], local_sem)
        mine.start()
        first = [copy(j, p_ref.at[:, _cols(c)], half((x, y), c), (*chip, c)) for j, chip in enumerate(chips)]
        for cp in first:
            cp.start()
        passed = [copy(3 + j, half(chip, c), half(chip, c), (x, y, 1 - c)) for j, chip in enumerate(chips)]
        for j, chip in enumerate(chips):
            copy(j, half(chip, c), half(chip, c), (x, y, c)).wait_recv()
            passed[j].start()
        for j, chip in enumerate(chips):
            copy(3 + j, half(chip, 1 - c), half(chip, 1 - c), (x, y, c)).wait_recv()
        for cp in first + passed:
            cp.wait_send()
        mine.wait()

    return pl.pallas_call(
        body, name="all_gather_weights", in_specs=[HBM_SPEC], out_specs=HBM_SPEC,
        out_shape=jax.ShapeDtypeStruct((4, rows, packed.shape[1]), packed.dtype),
        scratch_shapes=[pltpu.SemaphoreType.DMA((6,)), pltpu.SemaphoreType.DMA((6,)), pltpu.SemaphoreType.DMA],
    )(packed)


SEM_SPEC = pl.BlockSpec(memory_space=pltpu.SEMAPHORE)
ANY_SPEC = pl.BlockSpec(memory_space=pl.ANY)
DATAFLOW = pltpu.SideEffectType.DATAFLOW_SIDE_EFFECTING


def _hbm(a):
    return pltpu.with_memory_space_constraint(a, pltpu.HBM)


def _gather_start(packed, tag):
    rows = packed.shape[0]

    def body(p_ref, g_ref, send_sems, recv_sems, p_thru, g_thru, token):
        x, y, c = _position()
        for j, chip in enumerate(_other_chips(x, y)):
            for s in range(2):
                pltpu.make_async_remote_copy(
                    src_ref=p_ref.at[:, _cols(c)], dst_ref=g_ref.at[2 * x + y, :, _cols(c)],
                    send_sem=send_sems.at[2 * j + s], recv_sem=recv_sems.at[2 * j + s],
                    device_id=(*chip, 1 - c if s else c), device_id_type=MESH).start()
        token[...] = jnp.zeros_like(token)

    return pl.pallas_call(
        body, name=tag + "_start",
        out_shape=(pltpu.SemaphoreType.DMA((6,)), pltpu.SemaphoreType.DMA((6,)), pltpu.HBM(packed.shape, packed.dtype),
                   pltpu.HBM((4, rows, PACK_W), packed.dtype), jax.ShapeDtypeStruct((8, LANES), F32)),
        in_specs=(HBM_SPEC, HBM_SPEC), out_specs=(SEM_SPEC, SEM_SPEC, HBM_SPEC, HBM_SPEC, VMEM_SPEC),
        input_output_aliases={0: 2, 1: 3},
        compiler_params=pltpu.CompilerParams(has_side_effects=DATAFLOW),
    )(_hbm(packed), _hbm(lax.empty((4, rows, PACK_W), packed.dtype)))


def _gather_wait(send_sems, recv_sems, p_thru, g_thru, after, tag):
    def body(p_ref, g_ref, send_sems, recv_sems, after_ref, p_dead, got_ref):
        x, y, c = _position()
        for j, chip in enumerate(_other_chips(x, y)):
            for s in range(2):
                cp = pltpu.make_async_remote_copy(
                    src_ref=p_ref.at[:, _cols(c)], dst_ref=g_ref.at[2 * chip[0] + chip[1], :, _cols(1 - c if s else c)],
                    send_sem=send_sems.at[2 * j + s], recv_sem=recv_sems.at[2 * j + s],
                    device_id=(x, y, c), device_id_type=MESH)
                cp.wait_send()
                cp.wait_recv()

    return pl.pallas_call(
        body, name=tag + "_wait",
        out_shape=(pltpu.HBM(p_thru.shape, p_thru.dtype), pltpu.HBM(g_thru.shape, g_thru.dtype)),
        in_specs=(HBM_SPEC, HBM_SPEC, SEM_SPEC, SEM_SPEC, ANY_SPEC), out_specs=(HBM_SPEC, HBM_SPEC),
        input_output_aliases={0: 0, 1: 1},
        compiler_params=pltpu.CompilerParams(has_side_effects=DATAFLOW),
    )(p_thru, g_thru, send_sems, recv_sems, after)[1]


def _sibling_swap_halves(grads, tag):
    rows = grads.shape[1]

    def body(g_ref, a_ref, send_sem, recv_sem):
        x, y, c = _position()
        cp = pltpu.make_async_remote_copy(src_ref=g_ref.at[:, :, _cols(1 - c)], dst_ref=a_ref,
                                          send_sem=send_sem, recv_sem=recv_sem, device_id=(x, y, 1 - c),
                                          device_id_type=MESH)
        cp.start()
        cp.wait()

    return pl.pallas_call(
        body, name=tag + "_swap_cores", in_specs=[HBM_SPEC], out_specs=HBM_SPEC,
        out_shape=jax.ShapeDtypeStruct((4, rows, HALF_W), grads.dtype),
        scratch_shapes=[pltpu.SemaphoreType.DMA, pltpu.SemaphoreType.DMA],
    )(grads)


def _rs_block(rows):
    return max(d for d in range(16, 1601, 16) if rows % d == 0)


def _chip_sum(grads, other, c, tag):
    rows = other.shape[1]
    rb = _rs_block(rows)

    def body(c_ref, g_ref, a_ref, o_ref):
        o_ref[...] = (g_ref[...] + a_ref[...]).astype(o_ref.dtype)

    blk = (1, rb, HALF_W)
    return pl.pallas_call(
        body, name=tag + "_chip_sum",
        grid_spec=pltpu.PrefetchScalarGridSpec(
            num_scalar_prefetch=1, grid=(4, rows // rb),
            in_specs=[pl.BlockSpec(blk, lambda k, i, c_ref: (k, i, c_ref[0])),
                      pl.BlockSpec(blk, lambda k, i, c_ref: (k, i, 0))],
            out_specs=pl.BlockSpec(blk, lambda k, i, c_ref: (k, i, 0))),
        out_shape=jax.ShapeDtypeStruct(other.shape, BF16),
        compiler_params=_cparams(("parallel", "parallel")),
    )(jnp.reshape(c, (1,)).astype(jnp.int32), grads, other)


def _chip_copies(t_ref, b_ref, send_sems, recv_sems):
    x, y, c = _position()
    return [pltpu.make_async_remote_copy(src_ref=t_ref.at[2 * chip[0] + chip[1]], dst_ref=b_ref.at[j],
                                         send_sem=send_sems.at[j], recv_sem=recv_sems.at[j],
                                         device_id=(*chip, c), device_id_type=MESH)
            for j, chip in enumerate(_other_chips(x, y))]


def _send_chip_sums(sums, tag):
    def body(t_ref, b_ref, send_sems, recv_sems):
        copies = _chip_copies(t_ref, b_ref, send_sems, recv_sems)
        for cp in copies:
            cp.start()
        for cp in copies:
            cp.wait()

    return pl.pallas_call(
        body, name=tag + "_send_chips", in_specs=[HBM_SPEC], out_specs=HBM_SPEC,
        out_shape=jax.ShapeDtypeStruct((3,) + sums.shape[1:], sums.dtype),
        scratch_shapes=[pltpu.SemaphoreType.DMA((3,)), pltpu.SemaphoreType.DMA((3,))],
    )(sums)


def _send_chip_sums_start(sums, tag):
    land = (3,) + sums.shape[1:]

    def body(t_ref, b_ref, send_sems, recv_sems, t_thru, b_thru, token):
        for cp in _chip_copies(t_ref, b_ref, send_sems, recv_sems):
            cp.start()
        token[...] = jnp.zeros_like(token)

    return pl.pallas_call(
        body, name=tag + "_send_chips_start",
        out_shape=(pltpu.SemaphoreType.DMA((3,)), pltpu.SemaphoreType.DMA((3,)), pltpu.HBM(sums.shape, sums.dtype),
                   pltpu.HBM(land, sums.dtype), jax.ShapeDtypeStruct((8, LANES), F32)),
        in_specs=(HBM_SPEC, HBM_SPEC), out_specs=(SEM_SPEC, SEM_SPEC, HBM_SPEC, HBM_SPEC, VMEM_SPEC),
        input_output_aliases={0: 2, 1: 3},
        compiler_params=pltpu.CompilerParams(has_side_effects=DATAFLOW),
    )(_hbm(sums), _hbm(lax.empty(land, sums.dtype)))


def _send_chip_sums_wait(send_sems, recv_sems, t_thru, b_thru, after, tag):
    def body(t_ref, b_ref, send_sems, recv_sems, after_ref, t_dead, got_ref):
        for cp in _chip_copies(t_ref, b_ref, send_sems, recv_sems):
            cp.wait_send()
            cp.wait_recv()

    return pl.pallas_call(
        body, name=tag + "_send_chips_wait",
        out_shape=(pltpu.HBM(t_thru.shape, t_thru.dtype), pltpu.HBM(b_thru.shape, b_thru.dtype)),
        in_specs=(HBM_SPEC, HBM_SPEC, SEM_SPEC, SEM_SPEC, ANY_SPEC), out_specs=(HBM_SPEC, HBM_SPEC),
        input_output_aliases={0: 0, 1: 1},
        compiler_params=pltpu.CompilerParams(has_side_effects=DATAFLOW),
    )(t_thru, b_thru, send_sems, recv_sems, after)[1]


def _final_sum(grads, other, recv, k, c, tag):
    rows = other.shape[1]
    rb = _rs_block(rows)

    def body(k_ref, c_ref, g_ref, a_ref, b_ref, o_ref):
        own = g_ref[0] + a_ref[0]
        o_ref[...] = ((own + b_ref[0].astype(F32)) + b_ref[1].astype(F32)) + b_ref[2].astype(F32)

    return pl.pallas_call(
        body, name=tag + "_final_sum",
        grid_spec=pltpu.PrefetchScalarGridSpec(
            num_scalar_prefetch=2, grid=(rows // rb,),
            in_specs=[pl.BlockSpec((1, rb, HALF_W), lambda i, k_ref, c_ref: (k_ref[0], i, c_ref[0])),
                      pl.BlockSpec((1, rb, HALF_W), lambda i, k_ref, c_ref: (k_ref[0], i, 0)),
                      pl.BlockSpec((3, rb, HALF_W), lambda i, k_ref, c_ref: (0, i, 0))],
            out_specs=pl.BlockSpec((rb, HALF_W), lambda i, k_ref, c_ref: (i, 0))),
        out_shape=jax.ShapeDtypeStruct((rows, HALF_W), F32),
        compiler_params=_cparams(("parallel",)),
    )(jnp.reshape(k, (1,)).astype(jnp.int32), jnp.reshape(c, (1,)).astype(jnp.int32), grads, other, recv)


def _join_halves(half, core, tag):
    def body(h_ref, o_ref, send_sem, recv_sem):
        x, y, c = _position()
        cp = pltpu.make_async_remote_copy(src_ref=h_ref, dst_ref=o_ref, send_sem=send_sem, recv_sem=recv_sem,
                                          device_id=(x, y, 1 - c), device_id_type=MESH)
        cp.start()
        cp.wait()

    other = pl.pallas_call(
        body, name=tag + "_join_cores", in_specs=[HBM_SPEC], out_specs=HBM_SPEC,
        out_shape=jax.ShapeDtypeStruct(half.shape, half.dtype),
        scratch_shapes=[pltpu.SemaphoreType.DMA, pltpu.SemaphoreType.DMA],
    )(half)
    first = core == 0
    return jnp.concatenate([jnp.where(first, half, other), jnp.where(first, other, half)], axis=1)


def _all_reduce_packet(packet):
    rows = packet.shape[0]

    def body(p_ref, o_ref, buf, send_sems, recv_sems):
        x, y, c = _position()
        me = 4 * x + 2 * y + c
        buf[me] = p_ref[...]

        def flip(v, bit):
            return 1 - v if bit else v

        for p in range(1, 8):
            peer = (flip(x, p & 4), flip(y, p & 2), flip(c, p & 1))
            pltpu.make_async_remote_copy(src_ref=p_ref, dst_ref=buf.at[me], send_sem=send_sems.at[p - 1],
                                         recv_sem=recv_sems.at[p - 1], device_id=peer, device_id_type=MESH).start()
        for p in range(1, 8):
            peer = (flip(x, p & 4), flip(y, p & 2), flip(c, p & 1))
            slot = 4 * peer[0] + 2 * peer[1] + peer[2]
            cp = pltpu.make_async_remote_copy(src_ref=p_ref, dst_ref=buf.at[slot], send_sem=send_sems.at[p - 1],
                                              recv_sem=recv_sems.at[p - 1], device_id=peer, device_id_type=MESH)
            cp.wait_recv()
            cp.wait_send()
        acc = buf[0]
        for dev in range(1, 8):
            acc = acc + buf[dev]
        o_ref[...] = acc

    return pl.pallas_call(
        body, name="all_reduce_packet", in_specs=[VMEM_SPEC], out_specs=VMEM_SPEC,
        out_shape=jax.ShapeDtypeStruct(packet.shape, F32),
        scratch_shapes=[pltpu.VMEM((8, rows, LANES), F32), pltpu.SemaphoreType.DMA((7,)),
                        pltpu.SemaphoreType.DMA((7,))],
    )(packet)


def _stack_range(name, n_stack, pack):
    if name.startswith('gqa'):
        return (0, 0) if pack == 0 else (0, n_stack)
    return (0, 1) if pack == 0 else (1, n_stack)


def _pack_members(pack):
    out = []
    for n, shape, ax in BIG:
        lo, hi = _stack_range(n, shape[0], pack)
        if hi > lo:
            out.append((n, (hi - lo,) + shape[1:], ax, (lo, hi)))
    return out


PACK_ROW_MULTIPLE = 512


def _pad_rows(parts, dtype):
    rows = sum(p.shape[0] for p in parts)
    pad = -rows % PACK_ROW_MULTIPLE
    return jnp.concatenate(parts + ([jnp.zeros((pad, PACK_W), dtype)] if pad else []), axis=0)


def _pack_blocks(blocks, dtype, pack):
    return _pad_rows([blocks[n][lo:hi].astype(dtype).reshape(-1, PACK_W)
                      for n, _, _, (lo, hi) in _pack_members(pack)], dtype)


def _unpack_blocks(packed, pack):
    out, off = {}, 0
    for n, shape, _, _ in _pack_members(pack):
        r = math.prod(shape) // PACK_W
        out[n] = packed[off:off + r].reshape(shape)
        off += r
    return out


def _unpack_gathered(gathered, pack, own=None, chip=None):
    blocks = [gathered[k] if own is None else jnp.where(chip == k, own, gathered[k]) for k in range(4)]
    per_chip = [_unpack_blocks(blocks[k], pack) for k in range(4)]
    return {n: jnp.concatenate([per_chip[k][n] for k in range(4)], axis=ax) for n, _, ax, _ in _pack_members(pack)}


def _pack_full(full, dtype, pack):
    chips = []
    for k in range(4):
        parts = []
        for n, shape, ax, _ in _pack_members(pack):
            blk = lax.slice_in_dim(full[n], k * shape[ax], (k + 1) * shape[ax], axis=ax)
            parts.append(blk.astype(dtype).reshape(-1, PACK_W))
        chips.append(_pad_rows(parts, dtype))
    return jnp.stack(chips, axis=0)


def _pack_small(vals, loss_row):
    rows = [loss_row.reshape(1, LANES)]
    for n, shape in SMALL:
        v = vals.get(n)
        v = jnp.zeros(shape, F32) if v is None else v
        rows.append(v.astype(F32).reshape(-1, LANES))
    packet = jnp.concatenate(rows, axis=0)
    return jnp.pad(packet, ((0, PACKET_ROWS - packet.shape[0]), (0, 0)))


def _unpack_small(packet):
    out, off = {}, 1
    for n, shape in SMALL:
        r = math.prod(shape) // LANES
        out[n] = packet[off:off + r].reshape(shape)
        off += r
    return packet[0, 0], out


_MLA = dict(R=1, dk=MLA_DK, dv=MLA_V, hb=2, bq=512, bk=512)
_MLA_FWD_BQ = 1024
_GQA = dict(R=GQA_HEADS // GQA_KV_HEADS, dk=GQA_HEAD_DIM, dv=GQA_HEAD_DIM, hb=2, bq=256, bk=512)


def _layer_params(layer, full, gains):
    pack = 0 if layer == 0 else 1
    i = layer // 2

    def mat(name):
        lo, _ = _stack_range(name, 4 if name.startswith('ffn') else 2, pack)
        return full[name][(layer if name.startswith('ffn') else i) - lo]

    p = dict(ffn_norm=gains['ffn_norm'][layer][None], ffn_w_in=mat('ffn_w_in'), ffn_w_out=mat('ffn_w_out'))
    if layer % 2 == 0:
        w_in = mat('w_in_ab')
        zeros = jnp.zeros((D_MODEL, 32), w_in.dtype)
        p['w_a'] = jnp.concatenate([w_in[:, :640], zeros, zeros, w_in[:, 640:IN_A], zeros], axis=1)
        p['w_b'] = w_in[:, IN_A:]
        p['w_uq'] = jnp.pad(mat('mla_w_uq'), ((0, 0), (0, 0), (0, MLA_DK - 96))).reshape(MLA_Q_RANK, -1)
        ukv = mat('mla_w_ukv')
        p['w_uk'] = jnp.pad(ukv[:, :, :MLA_NOPE], ((0, 0), (0, 0), (0, MLA_DK - MLA_NOPE))).reshape(MLA_KV_RANK, -1)
        p['w_uv'] = ukv[:, :, MLA_NOPE:].reshape(MLA_KV_RANK, -1)
        p['w_out'] = mat('w_out_ab')
        p['mix_norm'] = gains['mix_norm_ab'][i][None]
        p['q_norm'] = gains['mla_q_norm'][i][None]
        p['kv_norm'] = gains['mla_kv_norm'][i][None]
    else:
        p['w_q'], p['w_kv'], p['w_o'] = mat('gqa_w_q'), mat('gqa_w_kv'), mat('gqa_w_o')
        p['mix_norm'] = gains['mix_norm_c'][i][None]
        p['q_norm'] = jnp.tile(gains['gqa_q_norm'][i][None], (1, 2))
        p['k_norm'] = jnp.tile(gains['gqa_k_norm'][i][None], (1, 2))
    return p


def _even_fwd(x, p, cs, swap, tag):
    xn = _rms_fwd(x, p['mix_norm'], tag + "_norm")
    za = _mm(xn, p['w_a'], name=tag + "_in_a")
    zb = _mm(xn, p['w_b'], out_dtype=BF16, name=tag + "_in_b")
    cq, ckv, krr = _mla_prep(za, p['q_norm'], p['kv_norm'], cs, swap, tag + "_mla_prep")
    q_raw = _mm(cq, p['w_uq'], name=tag + "_uq")
    k_pad = _mm(ckv, p['w_uk'], name=tag + "_uk")
    v = _mm(ckv, p['w_uv'], out_dtype=BF16, name=tag + "_uv")
    qh, kh = _mla_qk(q_raw, k_pad, krr, cs, swap, tag + "_mla_qk")
    o_a, lse_a = _flash_fwd(qh, kh, v, name=tag + "_mla_attn", **dict(_MLA, bq=_MLA_FWD_BQ))
    og, lg = [], []
    for grp in range(DIL_GROUPS):
        o, l = _dil_fwd(zb, grp, f"{tag}_dil{grp}")
        og.append(o)
        lg.append(l)
    o_b, lt = _dil_combine(og, lg, tag + "_dil_merge")
    ocat = jnp.concatenate([o_a, o_b], axis=1)
    x1 = _mm(ocat, p['w_out'], add=x, name=tag + "_out")
    saved = dict(x=x, xn=xn, za=za, zb=zb, cq=cq, ckv=ckv, qh=qh, kh=kh, v=v, lse_a=lse_a, og=og, lg=lg, lt=lt,
                 ocat=ocat)
    return x1, saved


def _even_bwd(dx1, p, sv, cs, swap, seg64, tag):
    docat = _mm(dx1, p['w_out'], mode="nt", name=tag + "_out_dx")
    d_w_out = _mm(sv['ocat'], dx1, mode="tn", name=tag + "_out_dw")
    n_a = MLA_HEADS * MLA_V
    do_a = docat[:, :n_a].astype(BF16)
    res = _dil_combine_bwd(docat[:, n_a:], sv['og'], sv['lg'], sv['lt'], seg64, tag + "_dil_merge_bwd")
    dqs, dks, dvs = [], [], []
    for grp in range(DIL_GROUPS):
        dq, dk, dv = _dil_bwd(sv['zb'], res[grp], sv['lg'][grp], res[3 + grp], grp, f"{tag}_dil{grp}_bwd")
        dqs.append(dq)
        dks.append(dk)
        dvs.append(dv)
    dzb = jnp.concatenate(dqs + dks + dvs, axis=1).astype(BF16)
    dqh, dkh, dv = _flash_bwd(sv['qh'], sv['kh'], sv['v'], sv['ocat'][:, :n_a], do_a, sv['lse_a'],
                              name=tag + "_mla_attn_bwd", **_MLA)
    dq_raw, dkh, dkrr = _mla_qk_bwd(dqh, dkh, cs, swap, tag + "_mla_qk_bwd")
    dcq = _mm(dq_raw, p['w_uq'], mode="nt", name=tag + "_uq_dx")
    d_w_uq = _mm(sv['cq'], dq_raw, mode="tn", name=tag + "_uq_dw")
    dckv = _mm(dkh, p['w_uk'], mode="nt", name=tag + "_uk_dx")
    dckv = _mm(dv, p['w_uv'], mode="nt", add=dckv, name=tag + "_uv_dx")
    d_w_uk = _mm(sv['ckv'], dkh, mode="tn", name=tag + "_uk_dw")
    d_w_uv = _mm(sv['ckv'], dv, mode="tn", name=tag + "_uv_dw")
    dza, d_gq, d_gkv = _mla_prep_bwd(sv['za'], cs, dcq, dckv, dkrr, p['q_norm'], p['kv_norm'], swap,
                                     tag + "_mla_prep_bwd")
    dxn = _mm(dza, p['w_a'], mode="nt", name=tag + "_in_a_dx")
    dxn = _mm(dzb, p['w_b'], mode="nt", add=dxn, name=tag + "_in_b_dx")
    d_w_a = _mm(sv['xn'], dza, mode="tn", name=tag + "_in_a_dw")
    d_w_b = _mm(sv['xn'], dzb, mode="tn", name=tag + "_in_b_dw")
    dx, d_g = _rms_bwd(sv['x'], p['mix_norm'], dxn, dx1, tag + "_norm_bwd")
    d_w_in = jnp.concatenate([d_w_a[:, :640], d_w_a[:, 704:736], d_w_b], axis=1)
    d_uq = d_w_uq.reshape(MLA_Q_RANK, MLA_HEADS, MLA_DK)[:, :, :MLA_NOPE + MLA_ROPE]
    d_ukv = jnp.concatenate([d_w_uk.reshape(MLA_KV_RANK, MLA_HEADS, MLA_DK)[:, :, :MLA_NOPE],
                             d_w_uv.reshape(MLA_KV_RANK, MLA_HEADS, MLA_V)], axis=2)
    grads = dict(w_in_ab=d_w_in, mla_w_uq=d_uq, mla_w_ukv=d_ukv, w_out_ab=d_w_out, mix_norm_ab=d_g[0],
                 mla_q_norm=d_gq[0], mla_kv_norm=d_gkv[0])
    return dx, grads


def _odd_fwd(x, p, cs, seg64, swap, tag):
    xn = _rms_fwd(x, p['mix_norm'], tag + "_norm")
    q_raw = _mm(xn, p['w_q'], name=tag + "_q")
    kv_raw = _mm(xn, p['w_kv'], name=tag + "_kv")
    qh, kh, v = _gqa_prep(q_raw, kv_raw, cs, p['q_norm'], p['k_norm'], seg64, swap, tag + "_gqa_prep")
    o, lse = _flash_fwd(qh, kh, v, name=tag + "_gqa_attn", **_GQA)
    x1 = _mm(o, p['w_o'], add=x, name=tag + "_o")
    return x1, dict(x=x, xn=xn, q_raw=q_raw, kv_raw=kv_raw, qh=qh, kh=kh, v=v, o=o, lse=lse)


def _odd_bwd(dx1, p, sv, cs, seg64, swap, tag):
    do = _mm(dx1, p['w_o'], mode="nt", out_dtype=BF16, name=tag + "_o_dx")
    d_w_o = _mm(sv['o'], dx1, mode="tn", name=tag + "_o_dw")
    dqh, dkh, dv = _flash_bwd(sv['qh'], sv['kh'], sv['v'], sv['o'], do, sv['lse'], name=tag + "_gqa_attn_bwd",
                              **_GQA)
    dq_raw, dkv_raw, d_gq, d_gk = _gqa_prep_bwd(sv['q_raw'], sv['kv_raw'], cs, dqh, dkh, dv, p['q_norm'],
                                                p['k_norm'], seg64, swap, tag + "_gqa_prep_bwd")
    dxn = _mm(dq_raw, p['w_q'], mode="nt", name=tag + "_q_dx")
    dxn = _mm(dkv_raw, p['w_kv'], mode="nt", add=dxn, name=tag + "_kv_dx")
    d_w_q = _mm(sv['xn'], dq_raw, mode="tn", name=tag + "_q_dw")
    d_w_kv = _mm(sv['xn'], dkv_raw, mode="tn", name=tag + "_kv_dw")
    dx, d_g = _rms_bwd(sv['x'], p['mix_norm'], dxn, dx1, tag + "_norm_bwd")
    grads = dict(gqa_w_q=d_w_q, gqa_w_kv=d_w_kv, gqa_w_o=d_w_o, mix_norm_c=d_g[0],
                 gqa_q_norm=d_gq[0, :GQA_HEAD_DIM] + d_gq[0, GQA_HEAD_DIM:],
                 gqa_k_norm=d_gk[0, :GQA_HEAD_DIM] + d_gk[0, GQA_HEAD_DIM:])
    return dx, grads


def _ffn_fwd(x, p, tag):
    xn = _rms_fwd(x, p['ffn_norm'], tag + "_ffn_norm")
    h = _mm(xn, p['ffn_w_in'], out_dtype=BF16, name=tag + "_ffn_in")
    a = _swiglu(h, tag + "_swiglu")
    x2 = _mm(a, p['ffn_w_out'], add=x, name=tag + "_ffn_out")
    return x2, dict(x=x, xn=xn, h=h, a=a)


def _ffn_bwd(dx2, p, sv, tag):
    da = _mm(dx2, p['ffn_w_out'], mode="nt", out_dtype=BF16, name=tag + "_ffn_out_dx")
    d_w_out = _mm(sv['a'], dx2, mode="tn", name=tag + "_ffn_out_dw")
    dh = _swiglu_bwd(sv['h'], da, tag + "_swiglu_bwd")
    d_w_in = _mm(sv['xn'], dh, mode="tn", name=tag + "_ffn_in_dw")
    dxn = _mm(dh, p['ffn_w_in'], mode="nt", name=tag + "_ffn_in_dx")
    dx, d_g = _rms_bwd(sv['x'], p['ffn_norm'], dxn, dx2, tag + "_ffn_norm_bwd")
    return dx, d_w_in, d_w_out, d_g[0]


EVEN_MATS = ('w_in_ab', 'mla_w_uq', 'mla_w_ukv', 'w_out_ab')
ODD_MATS = ('gqa_w_q', 'gqa_w_kv', 'gqa_w_o')
FFN_MATS = ('ffn_w_in', 'ffn_w_out')


def _schedule(x, target, gains, full_of_pack, rest_grads_ready):
    s = x.shape[0]
    cs_mla, cs_gqa = _rope_tables(s)
    swap, seg64 = _swap_matrix(), _seg_matrix(GQA_HEAD_DIM)
    params, saved, full = [], [], None
    for layer in range(4):
        tag = f"l{layer}"
        if layer < 2:
            full = full_of_pack(layer, x)
        p = _layer_params(layer, full, gains)
        if layer % 2 == 0:
            x, sv = _even_fwd(x, p, cs_mla, swap, tag)
        else:
            x, sv = _odd_fwd(x, p, cs_gqa, seg64, swap, tag)
        x, sv_f = _ffn_fwd(x, p, tag)
        params.append(p)
        saved.append((sv, sv_f))
    dx, loss_row, d_final = _loss_head(x, target, gains['final_norm'][None], "loss_head")

    per_layer, rest = {}, None
    for layer in reversed(range(4)):
        p, (sv, sv_f), tag = params[layer], saved[layer], f"l{layer}"
        if layer == 0:
            rest = {n: per_layer[2][n][None] for n in EVEN_MATS}
            rest.update({n: jnp.stack([per_layer[1][n], per_layer[3][n]], axis=0) for n in ODD_MATS})
            rest.update({n: jnp.stack([per_layer[l][n] for l in (1, 2, 3)], axis=0) for n in FFN_MATS})
            token = rest_grads_ready(rest)
            if token is not None:
                p = dict(p, ffn_w_out=p['ffn_w_out'] + token[0, 0].astype(p['ffn_w_out'].dtype))
        dx, d_ffn_in, d_ffn_out, d_ffn_g = _ffn_bwd(dx, p, sv_f, tag)
        if layer % 2 == 0:
            dx, g = _even_bwd(dx, p, sv, cs_mla, swap, seg64, tag)
        else:
            dx, g = _odd_bwd(dx, p, sv, cs_gqa, seg64, swap, tag)
        g.update(ffn_w_in=d_ffn_in, ffn_w_out=d_ffn_out, ffn_norm=d_ffn_g)
        per_layer[layer] = g

    first = {n: per_layer[0][n][None] for n in EVEN_MATS + FFN_MATS}
    small = {'final_norm': d_final[0], 'ffn_norm': jnp.stack([per_layer[l]['ffn_norm'] for l in range(4)], axis=0)}
    for n in ('mix_norm_ab', 'mla_q_norm', 'mla_kv_norm'):
        small[n] = jnp.stack([per_layer[0][n], per_layer[2][n]], axis=0)
    for n in ('mix_norm_c', 'gqa_q_norm', 'gqa_k_norm'):
        small[n] = jnp.stack([per_layer[1][n], per_layer[3][n]], axis=0)
    return loss_row, dx, first, rest, small


def _core_sums(grads, pack, core, tag):
    packed = _pack_full(grads, F32, pack)
    other = _sibling_swap_halves(packed, tag)
    return packed, other, _chip_sum(packed, other, core, tag)


def _finish_reduce_scatter(packed, other, recv, pack, chip, core, tag):
    return _unpack_blocks(_join_halves(_final_sum(packed, other, recv, chip, core, tag), core, tag), pack)


def _step(x, target, w, m, v):
    big_names = [n for n, _, _ in BIG]
    chip = 2 * lax.axis_index("x") + lax.axis_index("y")
    core = lax.axis_index("c")

    gains = {n: w[n] for n, _ in SMALL if n != 'mix_norm_c'}
    c_cols = w['mix_norm_c'].shape[1]
    own_c = lax.dynamic_update_slice(jnp.zeros((2, 4 * c_cols), F32), w['mix_norm_c'], (0, chip * c_cols))
    gains['mix_norm_c'] = _unpack_small(_all_reduce_packet(_pack_small(
        {'mix_norm_c': own_c * 0.5}, jnp.zeros((LANES,), F32))))[1]['mix_norm_c']

    gathered0 = _all_gather_weights(_pack_blocks(w, BF16, 0))
    packed1, gathered0 = lax.optimization_barrier((_pack_blocks(w, BF16, 1), gathered0))
    ag_send, ag_recv, p_thru, g_thru, ag_token = _gather_start(packed1, "gather_rest")
    gains['mix_norm_ab'] = gains['mix_norm_ab'] + ag_token[0, 0]
    full0 = _unpack_gathered(gathered0, 0)

    def full_of_pack(pack, after):
        if pack == 0:
            return full0
        landed = _gather_wait(ag_send, ag_recv, p_thru, g_thru, after, "gather_rest")
        return _unpack_gathered(landed, 1, own=packed1, chip=chip)

    rs = {}

    def rest_grads_ready(rest):
        rs['packed'], rs['other'], sums = _core_sums(rest, 1, core, "grad_rest")
        rs['send'], rs['recv'], rs['t'], rs['b'], token = _send_chip_sums_start(sums, "grad_rest")
        return token

    loss_row, dx, first, rest, small = _schedule(x[0], target[0], gains, full_of_pack, rest_grads_ready)
    recv1 = _send_chip_sums_wait(rs['send'], rs['recv'], rs['t'], rs['b'], dx, "grad_rest")
    g_rest = _finish_reduce_scatter(rs['packed'], rs['other'], recv1, 1, chip, core, "grad_rest")
    packed0, other0, sums0 = _core_sums(first, 0, core, "grad_first")
    g_first = _finish_reduce_scatter(packed0, other0, _send_chip_sums(sums0, "grad_first"), 0, chip, core,
                                     "grad_first")
    g_blocks = {n: (jnp.concatenate([g_first[n], g_rest[n]], axis=0) if n in g_first else g_rest[n])
                for n in big_names}

    loss, g_small = _unpack_small(_all_reduce_packet(_pack_small(small, loss_row[0])))
    g_small['mix_norm_c'] = lax.dynamic_slice(g_small['mix_norm_c'], (0, chip * c_cols), (2, c_cols))

    out_g, out_d, out_m, out_v = {}, {}, {}, {}
    for n in big_names:
        shape = w[n].shape
        cols = shape[-1]
        d_, m_, v_ = _adamw(w[n].reshape(-1, cols), g_blocks[n].reshape(-1, cols), m[n].reshape(-1, cols),
                            v[n].reshape(-1, cols), "adamw_" + n)
        out_g[n], out_d[n], out_m[n], out_v[n] = g_blocks[n], d_.reshape(shape), m_.reshape(shape), v_.reshape(shape)
    for n, _ in SMALL:
        shape = w[n].shape
        as2d = (lambda t: t.reshape(1, -1)) if len(shape) == 1 else (lambda t: t)
        d_, m_, v_ = _adamw(as2d(w[n]), as2d(g_small[n]), as2d(m[n]), as2d(v[n]), "adamw_" + n)
        out_g[n], out_d[n], out_m[n], out_v[n] = g_small[n], d_.reshape(shape), m_.reshape(shape), v_.reshape(shape)
    return (loss, dx[None], *[out_g[n] for n in WEIGHTS], *[out_d[n] for n in WEIGHTS],
            *[out_m[n] for n in WEIGHTS], *[out_v[n] for n in WEIGHTS])


def kernel(x, mix_norm_ab, w_in_ab, mla_q_norm, mla_kv_norm, mla_w_uq, mla_w_ukv, w_out_ab, mix_norm_c, gqa_w_q, gqa_w_kv, gqa_q_norm, gqa_k_norm, gqa_w_o, ffn_norm, ffn_w_in, ffn_w_out, final_norm, loss_target, m_mix_norm_ab, m_w_in_ab, m_mla_q_norm, m_mla_kv_norm, m_mla_w_uq, m_mla_w_ukv, m_w_out_ab, m_mix_norm_c, m_gqa_w_q, m_gqa_w_kv, m_gqa_q_norm, m_gqa_k_norm, m_gqa_w_o, m_ffn_norm, m_ffn_w_in, m_ffn_w_out, m_final_norm, v_mix_norm_ab, v_w_in_ab, v_mla_q_norm, v_mla_kv_norm, v_mla_w_uq, v_mla_w_ukv, v_w_out_ab, v_mix_norm_c, v_gqa_w_q, v_gqa_w_kv, v_gqa_q_norm, v_gqa_k_norm, v_gqa_w_o, v_ffn_norm, v_ffn_w_in, v_ffn_w_out, v_final_norm):
    w = dict(zip(WEIGHTS, (mix_norm_ab, w_in_ab, mla_q_norm, mla_kv_norm, mla_w_uq, mla_w_ukv, w_out_ab, mix_norm_c,
                           gqa_w_q, gqa_w_kv, gqa_q_norm, gqa_k_norm, gqa_w_o, ffn_norm, ffn_w_in, ffn_w_out,
                           final_norm)))
    m = dict(zip(WEIGHTS, (m_mix_norm_ab, m_w_in_ab, m_mla_q_norm, m_mla_kv_norm, m_mla_w_uq, m_mla_w_ukv,
                           m_w_out_ab, m_mix_norm_c, m_gqa_w_q, m_gqa_w_kv, m_gqa_q_norm, m_gqa_k_norm, m_gqa_w_o,
                           m_ffn_norm, m_ffn_w_in, m_ffn_w_out, m_final_norm)))
    v = dict(zip(WEIGHTS, (v_mix_norm_ab, v_w_in_ab, v_mla_q_norm, v_mla_kv_norm, v_mla_w_uq, v_mla_w_ukv,
                           v_w_out_ab, v_mix_norm_c, v_gqa_w_q, v_gqa_w_kv, v_gqa_q_norm, v_gqa_k_norm, v_gqa_w_o,
                           v_ffn_norm, v_ffn_w_in, v_ffn_w_out, v_final_norm)))
    return _step(x, loss_target, w, m, v)
```

```python
import math

import numpy as np
import jax
import jax.numpy as jnp
from jax import lax
from jax.experimental import pallas as pl
from jax.experimental.pallas import tpu as pltpu

F32 = jnp.float32
BF16 = jnp.bfloat16
MESH = pl.DeviceIdType.MESH

VMEM_LIMIT_BYTES = 56 * 1024 * 1024
LANES = 128

D_MODEL = 1024
NORM_EPS = 1e-6
ROPE_THETA = 10000.0
NEG_INF = -1e30
GRID_W = 64

MLA_HEADS, MLA_Q_RANK, MLA_KV_RANK, MLA_NOPE, MLA_ROPE, MLA_V = 8, 384, 256, 64, 32, 64
MLA_DK = 128
DIL_PAIRS = ((128, 1), (512, 4), (2048, 16))
DIL_HALF, DIL_SLOTS, DIL_GROUPS, DIL_HEAD_DIM = 64, 4, 3, 64
DIL_HEADS = DIL_SLOTS * DIL_GROUPS
DIL_W = DIL_SLOTS * DIL_HEAD_DIM
GQA_HEADS, GQA_KV_HEADS, GQA_HEAD_DIM = 16, 4, 64
FFN_HIDDEN = 2816
IN_A = MLA_Q_RANK + MLA_KV_RANK + MLA_ROPE
IN_A_PAD = 768
IN_B = 3 * DIL_HEADS * DIL_HEAD_DIM

ADAM_LR, ADAM_B1, ADAM_B2, ADAM_EPS, ADAM_WD, ADAM_STEP = 0.001, 0.9, 0.999, 1e-08, 0.01, 10

LOG2E, LN2 = math.log2(math.e), math.log(2.0)
MLA_SCALE = (MLA_NOPE + MLA_ROPE) ** -0.5
GQA_SCALE = GQA_HEAD_DIM ** -0.5

WEIGHTS = ['mix_norm_ab', 'w_in_ab', 'mla_q_norm', 'mla_kv_norm', 'mla_w_uq', 'mla_w_ukv', 'w_out_ab', 'mix_norm_c',
           'gqa_w_q', 'gqa_w_kv', 'gqa_q_norm', 'gqa_k_norm', 'gqa_w_o', 'ffn_norm', 'ffn_w_in', 'ffn_w_out',
           'final_norm']
BIG = (('w_in_ab', (2, 1024, 744), 2), ('mla_w_uq', (2, 96, 8, 96), 1), ('mla_w_ukv', (2, 64, 8, 128), 1),
       ('w_out_ab', (2, 768, 256), 2), ('gqa_w_q', (2, 256, 1024), 1), ('gqa_w_kv', (2, 256, 512), 1),
       ('gqa_w_o', (2, 256, 1024), 1), ('ffn_w_in', (4, 1024, 1408), 2), ('ffn_w_out', (4, 704, 1024), 1))
PACK_W = 1024
SMALL = (('mix_norm_ab', (2, 1024)), ('mla_q_norm', (2, 384)), ('mla_kv_norm', (2, 256)), ('gqa_q_norm', (2, 64)),
         ('gqa_k_norm', (2, 64)), ('ffn_norm', (4, 1024)), ('final_norm', (1024,)), ('mix_norm_c', (2, 1024)))
PACKET_ROWS = 88


def _cparams(sem=None):
    return pltpu.CompilerParams(dimension_semantics=sem, vmem_limit_bytes=VMEM_LIMIT_BYTES)


def _pick(n, pref, mult=LANES):
    if n <= pref:
        return n
    for d in range(pref - pref % mult, 0, -mult):
        if n % d == 0:
            return d
    return n


_DIMS = {"nn": (((1,), (0,)), ((), ())), "nt": (((1,), (1,)), ((), ())), "tn": (((0,), (0,)), ((), ()))}


def _silu_mul(gate, up):
    gate, up = gate.astype(F32), up.astype(F32)
    return gate / (1.0 + jnp.exp(-gate)) * up


def _silu_mul_bwd(gate, up, da):
    gate, up = gate.astype(F32), up.astype(F32)
    sig = 1.0 / (1.0 + jnp.exp(-gate))
    silu = gate * sig
    return da * up * (sig + silu * (1.0 - sig)), da * silu


def _mm(a, b, *, mode="nn", add=None, out_dtype=F32, gated=False, gate_up=None, name="mm"):
    if mode == "nn":
        (m, k), (k2, n) = a.shape, b.shape
    elif mode == "nt":
        (m, k), (n, k2) = a.shape, b.shape
    else:
        (k, m), (k2, n) = a.shape, b.shape
    if gated:
        k, m = (k // 2, m) if mode == "nn" else (k, m // 2)
    assert k == k2, (a.shape, b.shape, mode)
    if mode == "tn":
        deep = a.dtype == BF16 and b.dtype == BF16
        bm, bn, bk = _pick(m, 1408), _pick(n, 1024), _pick(k, 2048 if deep else 1024, 16)
    else:
        bm, bn, bk = _pick(m, 512, 16), _pick(n, 1408), _pick(k, 2816)
    nk = k // bk
    assert m % bm == 0 and n % bn == 0 and k % bk == 0
    has_add, has_gu = add is not None, gate_up is not None
    assert not (has_add and has_gu) and not (gated and mode == "nt") and not (has_gu and mode != "nt")
    n_in = 2 + int(gated) + int(has_add) + 2 * int(has_gu)

    def body(*refs):
        a_val = _silu_mul(refs[0][...], refs[1][...]) if gated else refs[0][...]
        b_ref = refs[1 + int(gated)]
        part = lax.dot_general(a_val.astype(BF16), b_ref[...].astype(BF16), _DIMS[mode],
                               preferred_element_type=F32)

        def finish(r):
            if has_gu:
                d_gate, d_up = _silu_mul_bwd(refs[2][...], refs[3][...], r)
                refs[n_in][...] = d_gate.astype(refs[n_in].dtype)
                refs[n_in + 1][...] = d_up.astype(refs[n_in + 1].dtype)
                return
            if has_add:
                r = r + refs[n_in - 1][...]
            refs[n_in][...] = r.astype(refs[n_in].dtype)

        if nk == 1:
            finish(part)
        else:
            acc_ref = refs[-1]
            kk = pl.program_id(2)

            @pl.when(kk == 0)
            def _():
                acc_ref[...] = part

            @pl.when(kk > 0)
            def _():
                acc_ref[...] += part

            @pl.when(kk == nk - 1)
            def _():
                finish(acc_ref[...])

    a_bytes, b_bytes = a.size * a.dtype.itemsize, b.size * b.dtype.itemsize
    n_outer = nk == 1 and (n // bn) * a_bytes + b_bytes < a_bytes + (m // bm) * b_bytes

    def at(f):
        return (lambda j, i, kk: f(i, j, kk)) if n_outer else f

    if mode == "nn":
        a_specs = [pl.BlockSpec((bm, bk), at(lambda i, j, kk, o=o: (i, kk + o))) for o in ((0, nk) if gated else (0,))]
        b_spec = pl.BlockSpec((bk, bn), at(lambda i, j, kk: (kk, j)))
    elif mode == "nt":
        a_specs = [pl.BlockSpec((bm, bk), at(lambda i, j, kk: (i, kk)))]
        b_spec = pl.BlockSpec((bn, bk), at(lambda i, j, kk: (j, kk)))
    else:
        a_specs = [pl.BlockSpec((bk, bm), at(lambda i, j, kk, o=o: (kk, i + o)))
                   for o in ((0, m // bm) if gated else (0,))]
        b_spec = pl.BlockSpec((bk, bn), at(lambda i, j, kk: (kk, j)))
    o_spec = pl.BlockSpec((bm, bn), at(lambda i, j, kk: (i, j)))
    in_specs, args = a_specs + [b_spec], [a] * len(a_specs) + [b]
    if has_add:
        in_specs, args = in_specs + [o_spec], args + [add]
    if has_gu:
        in_specs += [o_spec, pl.BlockSpec((bm, bn), at(lambda i, j, kk: (i, j + n // bn)))]
        args += [gate_up, gate_up]
    out = jax.ShapeDtypeStruct((m, n), out_dtype)
    grid = (n // bn, m // bm, nk) if n_outer else (m // bm, n // bn, nk)
    return pl.pallas_call(
        body, name=name, grid=grid, in_specs=in_specs, out_specs=[o_spec, o_spec] if has_gu else o_spec,
        out_shape=[out, out] if has_gu else out,
        scratch_shapes=[pltpu.VMEM((bm, bn), F32)] if nk > 1 else [],
        compiler_params=_cparams(("parallel", "parallel", "arbitrary")),
    )(*args)


def _rows_call(fn, rows, consts, out_rows, out_accs=(), *, bs=256, name):
    s = rows[0].shape[0]
    bs = min(bs, s)
    assert s % bs == 0
    nr, nc, no, na = len(rows), len(consts), len(out_rows), len(out_accs)

    def body(*refs):
        vals = [r[...] for r in refs[:nr + nc]]
        outs = refs[nr + nc:]
        res = fn(*vals)
        if not isinstance(res, (tuple, list)):
            res = (res,)
        assert len(res) == no + na, (len(res), no, na)
        for r, v in zip(outs[:no], res[:no]):
            r[...] = v.astype(r.dtype)
        if na:
            i = pl.program_id(0)
            for r, v in zip(outs[no:], res[no:]):
                @pl.when(i == 0)
                def _(r=r, v=v):
                    r[...] = v

                @pl.when(i > 0)
                def _(r=r, v=v):
                    r[...] += v

    in_specs = [pl.BlockSpec((bs, a.shape[1]), lambda i: (i, 0)) for a in rows]
    in_specs += [pl.BlockSpec(c.shape, lambda i: (0, 0)) for c in consts]
    out_specs = [pl.BlockSpec((bs, c), lambda i: (i, 0)) for c, _ in out_rows]
    out_specs += [pl.BlockSpec(tuple(sh), lambda i: (0, 0)) for sh in out_accs]
    out_shape = [jax.ShapeDtypeStruct((s, c), dt) for c, dt in out_rows]
    out_shape += [jax.ShapeDtypeStruct(tuple(sh), F32) for sh in out_accs]
    res = pl.pallas_call(
        body, name=name, grid=(s // bs,), in_specs=in_specs, out_specs=out_specs, out_shape=out_shape,
        compiler_params=_cparams(("arbitrary",) if na else ("parallel",)),
    )(*rows, *consts)
    return res


def _rms(x, g):
    return x * lax.rsqrt(jnp.mean(x * x, axis=-1, keepdims=True) + NORM_EPS) * g


def _rms_bwd_math(x, g, dy):
    r = lax.rsqrt(jnp.mean(x * x, axis=-1, keepdims=True) + NORM_EPS)
    u = dy * g
    dx = r * u - x * (r * r * r) * jnp.mean(u * x, axis=-1, keepdims=True)
    dg = jnp.sum(dy * x * r, axis=0, keepdims=True)
    return dx, dg


def _rms_fwd(x, g, name):
    return _rows_call(lambda xv, gv: _rms(xv, gv), [x], [g], [(x.shape[1], BF16)], name=name)[0]


def _rms_bwd(x, g, dy, dres, name):
    def fn(xv, dyv, dresv, gv):
        dx, dg = _rms_bwd_math(xv, gv, dyv.astype(F32))
        return dx + dresv, dg
    return _rows_call(fn, [x, dy, dres], [g], [(x.shape[1], F32)], [(1, x.shape[1])], name=name)


def _chunkdot(x, m):
    outs = [jnp.dot(x[:, c:c + LANES], m, precision=lax.Precision.HIGHEST, preferred_element_type=F32)
            for c in range(0, x.shape[1], LANES)]
    return outs[0] if len(outs) == 1 else jnp.concatenate(outs, axis=1)


def _lanes(t, width):
    n = width // LANES
    return t if n == 1 else jnp.concatenate([t] * n, axis=1)


def _rope(x, cs, swap):
    w = x.shape[1]
    return x * _lanes(cs[:, :LANES], w) + _chunkdot(x, swap) * _lanes(cs[:, LANES:], w)


def _rope_t(dy, cs, swap):
    w = dy.shape[1]
    return dy * _lanes(cs[:, :LANES], w) + _chunkdot(dy * _lanes(cs[:, LANES:], w), swap)


def _swap_matrix():
    m = np.zeros((LANES, LANES), np.float32)
    for j in range(LANES):
        src = j + 16 if (j % 32) < 16 else j - 16
        m[src, j] = 1.0
    return jnp.asarray(m)


def _seg_matrix(seg):
    idx = np.arange(LANES) // seg
    return jnp.asarray((idx[:, None] == idx[None, :]).astype(np.float32))


def _rope_tables(s):
    pos = jnp.arange(s)

    def angles(p, dim):
        freqs = ROPE_THETA ** (-jnp.arange(0, dim, 2, dtype=F32) / dim)
        ang = p.astype(F32)[:, None] * freqs[None, :]
        return jnp.cos(ang), jnp.sin(ang)

    cos_t, sin_t = angles(pos, MLA_ROPE)
    one, zero = jnp.ones((s, 64), F32), jnp.zeros((s, 64), F32)
    mla = jnp.concatenate([one, cos_t, cos_t, one[:, :32], zero, -sin_t, sin_t, zero[:, :32]], axis=1)
    cos_r, sin_r = angles(pos // GRID_W, GQA_HEAD_DIM // 2)
    cos_c, sin_c = angles(pos % GRID_W, GQA_HEAD_DIM // 2)
    c64 = jnp.concatenate([cos_r, cos_r, cos_c, cos_c], axis=1)
    s64 = jnp.concatenate([-sin_r, sin_r, -sin_c, sin_c], axis=1)
    gqa = jnp.concatenate([c64, c64, s64, s64], axis=1)
    return mla, gqa


def _col_to_row(col):
    return jnp.transpose(jnp.broadcast_to(col, (col.shape[0], LANES)))[0:1, :]


def _stack_heads(ref, heads, d, dtype=None):
    parts = [ref[:, hd * d:(hd + 1) * d] for hd in heads]
    out = parts[0] if len(parts) == 1 else jnp.concatenate(parts, axis=0)
    return out if dtype is None else out.astype(dtype)


def _fill_v_ones(v_ref, va_ref, hb, dv):
    @pl.when(pl.program_id(1) == 0)
    def _():
        ones = jnp.ones((v_ref.shape[0], dv), BF16)
        for h in range(hb):
            va_ref[:, 2 * h * dv:(2 * h + 1) * dv] = v_ref[:, h * dv:(h + 1) * dv]
            va_ref[:, (2 * h + 1) * dv:(2 * h + 2) * dv] = ones


def _flash_fwd(q, k, v, *, R, dk, dv, hb, bq, bk, name):
    s = q.shape[0]
    g = k.shape[1] // dk
    ng = g // hb
    bq, bk = min(bq, s), min(bk, s)
    nq, nkb = s // bq, s // bk
    rb = R * bq

    def body(q_ref, k_ref, v_ref, o_ref, lse_ref, va_ref):
        _fill_v_ones(v_ref, va_ref, hb, dv)
        head_sets = [[h * R + r for r in range(R)] for h in range(hb)]
        qss = [_stack_heads(q_ref, heads, dk) for heads in head_sets]

        def step(jj, carry):
            carry = list(carry)
            rows = [pl.ds(pl.multiple_of((jj * unroll + u) * bk, bk), bk) for u in range(unroll)]
            scs = [[lax.dot_general(qss[h], k_ref[rows[u], h * dk:(h + 1) * dk], _DIMS["nt"],
                                    preferred_element_type=F32) for h in range(hb)] for u in range(unroll)]
            for u in range(unroll):
                m2s = [jnp.maximum(carry[h][0], jnp.max(scs[u][h], axis=1, keepdims=True)) for h in range(hb)]
                ps = [jnp.exp2(scs[u][h] - m2s[h]).astype(BF16) for h in range(hb)]
                pvs = [jnp.dot(ps[h], va_ref[rows[u], 2 * h * dv:2 * (h + 1) * dv], preferred_element_type=F32)
                       for h in range(hb)]
                carry = [(m2s[h], jnp.exp2(carry[h][0] - m2s[h]) * carry[h][1] + pvs[h]) for h in range(hb)]
            return tuple(carry)

        unroll = 2 if nkb % 2 == 0 else 1
        init = tuple((jnp.full((rb, 1), NEG_INF, F32), jnp.zeros((rb, 2 * dv), F32)) for _ in range(hb))
        final = lax.fori_loop(0, nkb // unroll, step, init)
        for h, heads in enumerate(head_sets):
            m, acc = final[h]
            l = acc[:, dv:dv + 1]
            o = acc[:, :dv] / l
            row = _col_to_row(m + jnp.log2(l))
            for r, hd in enumerate(heads):
                o_ref[:, hd * dv:(hd + 1) * dv] = o[r * bq:(r + 1) * bq].astype(o_ref.dtype)
                lse_ref[0, hd:hd + 1, :] = row[:, r * bq:(r + 1) * bq]

    return pl.pallas_call(
        body, name=name, grid=(ng, nq),
        in_specs=[pl.BlockSpec((bq, hb * R * dk), lambda gi, i: (i, gi)),
                  pl.BlockSpec((s, hb * dk), lambda gi, i: (0, gi)),
                  pl.BlockSpec((s, hb * dv), lambda gi, i: (0, gi))],
        out_specs=[pl.BlockSpec((bq, hb * R * dv), lambda gi, i: (i, gi)),
                   pl.BlockSpec((1, hb * R, bq), lambda gi, i: (gi, 0, i))],
        out_shape=[jax.ShapeDtypeStruct((s, g * R * dv), BF16), jax.ShapeDtypeStruct((ng, hb * R, s), F32)],
        scratch_shapes=[pltpu.VMEM((s, 2 * hb * dv), BF16)],
        compiler_params=_cparams(("parallel", "arbitrary")),
    )(q, k, v)


def _flash_bwd(q, k, v, o, do, lse, *, R, dk, dv, hb, bq, bk, name):
    s = q.shape[0]
    g = k.shape[1] // dk
    ng = g // hb
    bq, bk = min(bq, s), min(bk, s)
    nq, nkb = s // bq, s // bk
    rb = R * bq

    def body(q_ref, k_ref, v_ref, o_ref, do_ref, lse_ref, dq_ref, dk_ref, dv_ref, va_ref, kt_ref):
        @pl.when(pl.program_id(1) == 0)
        def _():
            dk_ref[...] = jnp.zeros(dk_ref.shape, F32)
            dv_ref[...] = jnp.zeros(dv_ref.shape, F32)
            for c in range(nkb):
                kt_ref[c] = jnp.transpose(k_ref[c * bk:(c + 1) * bk, :].astype(F32)).astype(BF16)

        _fill_v_ones(v_ref, va_ref, hb, dv)
        lane = lax.broadcasted_iota(jnp.int32, (rb, dv), 1)
        head_sets = [[h * R + r for r in range(R)] for h in range(hb)]
        qss, doss, dosas, lrows = [], [], [], []
        for heads in head_sets:
            dos = _stack_heads(do_ref, heads, dv, BF16)
            delta = jnp.sum(dos.astype(F32) * _stack_heads(o_ref, heads, dv, F32), axis=1, keepdims=True)
            hi = delta.astype(BF16).astype(F32)
            lo = delta - hi
            qss.append(_stack_heads(q_ref, heads, dk))
            doss.append(dos)
            dosas.append(jnp.concatenate(
                [dos, jnp.where(lane == 0, -hi, jnp.where(lane == 1, -lo, 0.0)).astype(BF16)], axis=1))
            rows = [lse_ref[0, hd:hd + 1, :] for hd in heads]
            lrows.append(rows[0] if R == 1 else jnp.concatenate(rows, axis=1))

        def step(j, dqs):
            r0 = pl.multiple_of(j * bk, bk)
            hs = range(hb)
            kjs = [k_ref[pl.ds(r0, bk), h * dk:(h + 1) * dk] for h in hs]
            vas = [va_ref[pl.ds(r0, bk), 2 * h * dv:2 * (h + 1) * dv] for h in hs]
            sts = [lax.dot_general(kjs[h], qss[h], _DIMS["nt"], preferred_element_type=F32) for h in hs]
            dpts = [lax.dot_general(vas[h], dosas[h], _DIMS["nt"], preferred_element_type=F32) for h in hs]
            pts = [jnp.exp2(sts[h] - lrows[h]) for h in hs]
            pbs = [pts[h].astype(BF16) for h in hs]
            dsts = [(pts[h] * dpts[h]).astype(BF16) for h in hs]
            for h in hs:
                dv_ref[pl.ds(r0, bk), h * dv:(h + 1) * dv] += jnp.dot(pbs[h], doss[h], preferred_element_type=F32)
            for h in hs:
                dk_ref[pl.ds(r0, bk), h * dk:(h + 1) * dk] += jnp.dot(dsts[h], qss[h], preferred_element_type=F32)
            return tuple(dqs[h] + jnp.dot(kt_ref[j, h * dk:(h + 1) * dk, :], dsts[h], preferred_element_type=F32)
                         for h in hs)

        dqs = lax.fori_loop(0, nkb, step, tuple(jnp.zeros((dk, rb), F32) for _ in range(hb)))
        dq = jnp.transpose(jnp.concatenate(dqs, axis=0))
        for h, heads in enumerate(head_sets):
            for r, hd in enumerate(heads):
                dq_ref[:, hd * dk:(hd + 1) * dk] = dq[r * bq:(r + 1) * bq, h * dk:(h + 1) * dk]

    qspec = pl.BlockSpec((bq, hb * R * dk), lambda gi, i: (i, gi))
    ospec = pl.BlockSpec((bq, hb * R * dv), lambda gi, i: (i, gi))
    kspec = pl.BlockSpec((s, hb * dk), lambda gi, i: (0, gi))
    vspec = pl.BlockSpec((s, hb * dv), lambda gi, i: (0, gi))
    return pl.pallas_call(
        body, name=name, grid=(ng, nq),
        in_specs=[qspec, kspec, vspec, ospec, ospec, pl.BlockSpec((1, hb * R, bq), lambda gi, i: (gi, 0, i))],
        out_specs=[qspec, kspec, vspec],
        out_shape=[jax.ShapeDtypeStruct((s, g * R * dk), F32), jax.ShapeDtypeStruct((s, g * dk), F32),
                   jax.ShapeDtypeStruct((s, g * dv), F32)],
        scratch_shapes=[pltpu.VMEM((s, 2 * hb * dv), BF16), pltpu.VMEM((nkb, hb * dk, bk), BF16)],
        compiler_params=_cparams(("parallel", "arbitrary")),
    )(q, k, v, o, do, lse)


DIL_T = 1024
DIL_P = DIL_HALF
DIL_NCOL = IN_B // DIL_W


DIL_BATCH = 4


def _alibi_slope(head):
    return float(2.0 ** (-8.0 * (head + 1) / DIL_HEADS))


def _slot(sl_i):
    return slice(sl_i * DIL_HEAD_DIM, (sl_i + 1) * DIL_HEAD_DIM)


def _halo_specs(d, col, s, t):
    h = DIL_P * d
    per, last = t // h, s // h - 1
    return [pl.BlockSpec((h, DIL_W), lambda c: (jnp.maximum(c * per - 1, 0), col)),
            pl.BlockSpec((t, DIL_W), lambda c: (c, col)),
            pl.BlockSpec((h, DIL_W), lambda c: (jnp.minimum((c + 1) * per, last), col))]


def _staging(rows):
    return tuple(pltpu.VMEM((rows, LANES), F32) for _ in range(DIL_W // LANES))


def _stage(buf, refs):
    off = 0
    for r in refs:
        val = r[...].astype(F32)
        for j in range(DIL_W // LANES):
            buf[j][off:off + r.shape[0], :] = val[:, j * LANES:(j + 1) * LANES]
        off += r.shape[0]


def _unstage(buf, ref):
    ref[...] = jnp.concatenate([half[...] for half in buf], axis=1).astype(ref.dtype)


def _sub_tiles(d, t):
    return [(b * DIL_P * d + r, b * DIL_P) for b in range(t // (DIL_P * d)) for r in range(d)]


def _rows(start, size, d):
    return pl.ds(start, size, stride=d) if d > 1 else pl.ds(start, size)


def _strided(buf, start, size, d):
    return jnp.concatenate([half[_rows(start, size, d), :] for half in buf], axis=1)


def _put_strided(buf, start, d, val):
    for j in range(DIL_W // LANES):
        buf[j][_rows(start, val.shape[0], d), :] = val[:, j * LANES:(j + 1) * LANES]


def _band(u0, length, d, queries_wide):
    if queries_wide:
        shape = (3 * DIL_P, DIL_P)
        wide = u0 - DIL_P + lax.broadcasted_iota(jnp.int32, shape, 0)
        narrow = u0 + lax.broadcasted_iota(jnp.int32, shape, 1)
    else:
        shape = (DIL_P, 3 * DIL_P)
        narrow = u0 + lax.broadcasted_iota(jnp.int32, shape, 0)
        wide = u0 - DIL_P + lax.broadcasted_iota(jnp.int32, shape, 1)
    rel = jnp.abs(wide - narrow)
    valid = (rel <= DIL_HALF) & (wide >= 0) & (wide < length)
    return valid, rel.astype(F32) * float(d)


def _dil_fwd(zb, grp, name):
    s = zb.shape[0]
    d = DIL_PAIRS[grp][1]
    t = min(DIL_T, s)
    h = DIL_P * d
    scale = DIL_HEAD_DIM ** -0.5

    def body(q_ref, kp, kc, kn, vp, vc, vn, o_ref, lse_ref, qbuf, kbuf, vbuf, obuf, lbuf):
        _stage(qbuf, (q_ref,))
        _stage(kbuf, (kp, kc, kn))
        _stage(vbuf, (vp, vc, vn))
        u_step = pl.program_id(0) * (t // d)
        tiles = _sub_tiles(d, t)
        for g0 in range(0, len(tiles), DIL_BATCH):
            batch = tiles[g0:g0 + DIL_BATCH]
            masks = [_band(u_step + u, s // d, d, False) for _, u in batch]
            qs = [_strided(qbuf, row, DIL_P, d).astype(BF16) for row, _ in batch]
            ks = [_strided(kbuf, row, 3 * DIL_P, d).astype(BF16) for row, _ in batch]
            vs = [_strided(vbuf, row, 3 * DIL_P, d).astype(BF16) for row, _ in batch]
            chains = [(i, sl_i) for i in range(len(batch)) for sl_i in range(DIL_SLOTS)]
            scs = [lax.dot_general(qs[i][:, _slot(sl_i)], ks[i][:, _slot(sl_i)], _DIMS["nt"],
                                   preferred_element_type=F32) for i, sl_i in chains]
            scs = [jnp.where(masks[i][0], sc * scale - _alibi_slope(grp * DIL_SLOTS + sl_i) * masks[i][1], NEG_INF)
                   for (i, sl_i), sc in zip(chains, scs)]
            ms = [jnp.max(sc, axis=1, keepdims=True) for sc in scs]
            es = [jnp.exp(sc - m) for sc, m in zip(scs, ms)]
            dens = [jnp.sum(e, axis=1, keepdims=True) for e in es]
            outs = [jnp.dot((e / den).astype(BF16), vs[i][:, _slot(sl_i)], preferred_element_type=F32)
                    for (i, sl_i), e, den in zip(chains, es, dens)]
            lses = [jnp.broadcast_to(m + jnp.log(den), (DIL_P, DIL_HEAD_DIM)) for m, den in zip(ms, dens)]
            for i, (row, _) in enumerate(batch):
                pick = slice(i * DIL_SLOTS, (i + 1) * DIL_SLOTS)
                _put_strided(obuf, row, d, jnp.concatenate(outs[pick], axis=1))
                _put_strided(lbuf, row, d, jnp.concatenate(lses[pick], axis=1))
        _unstage(obuf, o_ref)
        _unstage(lbuf, lse_ref)

    own = pl.BlockSpec((t, DIL_W), lambda c: (c, 0))
    return pl.pallas_call(
        body, name=name, grid=(s // t,),
        in_specs=[pl.BlockSpec((t, DIL_W), lambda c: (c, grp))] + _halo_specs(d, 3 + grp, s, t)
        + _halo_specs(d, 6 + grp, s, t),
        out_specs=[own, own], out_shape=[jax.ShapeDtypeStruct((s, DIL_W), F32)] * 2,
        scratch_shapes=[_staging(t), _staging(t + 2 * h), _staging(t + 2 * h), _staging(t), _staging(t)],
        compiler_params=_cparams(("parallel",)),
    )(zb, zb, zb, zb, zb, zb, zb)


def _dil_bwd(zb, do, lse, dl, grp, name):
    s = zb.shape[0]
    d = DIL_PAIRS[grp][1]
    t = min(DIL_T, s)
    h = DIL_P * d
    scale = DIL_HEAD_DIM ** -0.5

    def chain_grads(qs, ks, vs, dos, lses, dls, masks):
        chains = [(i, sl_i) for i in range(len(qs)) for sl_i in range(DIL_SLOTS)]
        scs = [lax.dot_general(qs[i][:, _slot(sl_i)], ks[i][:, _slot(sl_i)], _DIMS["nt"],
                               preferred_element_type=F32) for i, sl_i in chains]
        dps = [lax.dot_general(dos[i][:, _slot(sl_i)], vs[i][:, _slot(sl_i)], _DIMS["nt"],
                               preferred_element_type=F32) for i, sl_i in chains]
        ps = [jnp.exp(jnp.where(masks[i][0], sc * scale - _alibi_slope(grp * DIL_SLOTS + sl_i) * masks[i][1],
                                NEG_INF) - lses[i][:, sl_i * DIL_HEAD_DIM:sl_i * DIL_HEAD_DIM + 1])
              for (i, sl_i), sc in zip(chains, scs)]
        dss = [(p * (dp - dls[i][:, sl_i * DIL_HEAD_DIM:sl_i * DIL_HEAD_DIM + 1]) * scale).astype(BF16)
               for (i, sl_i), p, dp in zip(chains, ps, dps)]
        return chains, ps, dss

    def dq_body(q_ref, kp, kc, kn, vp, vc, vn, do_ref, lse_ref, dl_ref, dq_ref, qbuf, kbuf, vbuf, dobuf, lsebuf,
                dlbuf, obuf):
        _stage(qbuf, (q_ref,))
        _stage(dobuf, (do_ref,))
        _stage(lsebuf, (lse_ref,))
        _stage(dlbuf, (dl_ref,))
        _stage(kbuf, (kp, kc, kn))
        _stage(vbuf, (vp, vc, vn))
        u_step = pl.program_id(0) * (t // d)
        tiles = _sub_tiles(d, t)
        for g0 in range(0, len(tiles), DIL_BATCH):
            batch = tiles[g0:g0 + DIL_BATCH]
            masks = [_band(u_step + u, s // d, d, False) for _, u in batch]
            narrow = [[_strided(b, row, DIL_P, d) for row, _ in batch] for b in (qbuf, dobuf, lsebuf, dlbuf)]
            ks = [_strided(kbuf, row, 3 * DIL_P, d).astype(BF16) for row, _ in batch]
            vs = [_strided(vbuf, row, 3 * DIL_P, d).astype(BF16) for row, _ in batch]
            chains, _, dss = chain_grads([a.astype(BF16) for a in narrow[0]], ks, vs,
                                         [a.astype(BF16) for a in narrow[1]], narrow[2], narrow[3], masks)
            outs = [jnp.dot(ds, ks[i][:, _slot(sl_i)], preferred_element_type=F32)
                    for (i, sl_i), ds in zip(chains, dss)]
            for i, (row, _) in enumerate(batch):
                _put_strided(obuf, row, d, jnp.concatenate(outs[i * DIL_SLOTS:(i + 1) * DIL_SLOTS], axis=1))
        _unstage(obuf, dq_ref)

    def dkv_body(k_ref, v_ref, qp, qc, qn, dop, doc, don, lp, lc, ln, dlp, dlc, dln, dk_ref, dv_ref,
                 kbuf, vbuf, qbuf, dobuf, lsebuf, dlbuf, dkbuf, dvbuf):
        _stage(kbuf, (k_ref,))
        _stage(vbuf, (v_ref,))
        _stage(qbuf, (qp, qc, qn))
        _stage(dobuf, (dop, doc, don))
        _stage(lsebuf, (lp, lc, ln))
        _stage(dlbuf, (dlp, dlc, dln))
        u_step = pl.program_id(0) * (t // d)
        tiles = _sub_tiles(d, t)
        for g0 in range(0, len(tiles), DIL_BATCH):
            batch = tiles[g0:g0 + DIL_BATCH]
            masks = [_band(u_step + u, s // d, d, True) for _, u in batch]
            ks = [_strided(kbuf, row, DIL_P, d).astype(BF16) for row, _ in batch]
            vs = [_strided(vbuf, row, DIL_P, d).astype(BF16) for row, _ in batch]
            wide_ = [[_strided(b, row, 3 * DIL_P, d) for row, _ in batch] for b in (qbuf, dobuf, lsebuf, dlbuf)]
            qs, dos = [a.astype(BF16) for a in wide_[0]], [a.astype(BF16) for a in wide_[1]]
            chains, ps, dss = chain_grads(qs, ks, vs, dos, wide_[2], wide_[3], masks)
            dvs = [lax.dot_general(p.astype(BF16), dos[i][:, _slot(sl_i)], _DIMS["tn"], preferred_element_type=F32)
                   for (i, sl_i), p in zip(chains, ps)]
            dks = [lax.dot_general(ds, qs[i][:, _slot(sl_i)], _DIMS["tn"], preferred_element_type=F32)
                   for (i, sl_i), ds in zip(chains, dss)]
            for i, (row, _) in enumerate(batch):
                pick = slice(i * DIL_SLOTS, (i + 1) * DIL_SLOTS)
                _put_strided(dkbuf, row, d, jnp.concatenate(dks[pick], axis=1))
                _put_strided(dvbuf, row, d, jnp.concatenate(dvs[pick], axis=1))
        _unstage(dkbuf, dk_ref)
        _unstage(dvbuf, dv_ref)

    def zcur(col):
        return pl.BlockSpec((t, DIL_W), lambda c: (c, col))

    own = pl.BlockSpec((t, DIL_W), lambda c: (c, 0))
    out = jax.ShapeDtypeStruct((s, DIL_W), F32)
    tile, wide = _staging(t), _staging(t + 2 * h)
    dq = pl.pallas_call(
        dq_body, name=name + "_dq", grid=(s // t,),
        in_specs=[zcur(grp)] + _halo_specs(d, 3 + grp, s, t) + _halo_specs(d, 6 + grp, s, t) + [own, own, own],
        out_specs=own, out_shape=out, scratch_shapes=[tile, wide, wide, tile, tile, tile, tile],
        compiler_params=_cparams(("parallel",)),
    )(zb, zb, zb, zb, zb, zb, zb, do, lse, dl)
    own3 = _halo_specs(d, 0, s, t)
    dk, dv = pl.pallas_call(
        dkv_body, name=name + "_dkv", grid=(s // t,),
        in_specs=[zcur(3 + grp), zcur(6 + grp)] + _halo_specs(d, grp, s, t) + own3 + own3 + own3,
        out_specs=[own, own], out_shape=[out, out],
        scratch_shapes=[tile, tile, wide, wide, wide, wide, tile, tile],
        compiler_params=_cparams(("parallel",)),
    )(zb, zb, zb, zb, zb, do, do, do, lse, lse, lse, dl, dl, dl)
    return dq, dk, dv


def _dil_combine(os_, ls_, name):
    def fn(o0, o1, o2, l0, l1, l2):
        m = jnp.maximum(jnp.maximum(l0, l1), l2)
        e0, e1, e2 = jnp.exp(l0 - m), jnp.exp(l1 - m), jnp.exp(l2 - m)
        den = e0 + e1 + e2
        comb = (e0 / den) * o0 + (e1 / den) * o1 + (e2 / den) * o2
        return comb, m + jnp.log(den)
    return _rows_call(fn, list(os_) + list(ls_), [], [(DIL_W, BF16), (DIL_W, F32)], name=name)


def _dil_combine_bwd(dcomb, os_, ls_, lt, seg64, name):
    def fn(dc, o0, o1, o2, l0, l1, l2, ltv, seg):
        w = [jnp.exp(l - ltv) for l in (l0, l1, l2)]
        comb = w[0] * o0 + w[1] * o1 + w[2] * o2
        t = _chunkdot(dc * comb, seg)
        return [wg * dc for wg in w] + [wg * t for wg in w]
    return _rows_call(fn, [dcomb] + list(os_) + list(ls_) + [lt], [seg64],
                      [(DIL_W, BF16)] * 3 + [(DIL_W, F32)] * 3, name=name)


def _mla_prep(za, gq, gkv, cs, swap, name):
    def fn(z, csv, gqv, gkvv, sw):
        return (_rms(z[:, :MLA_Q_RANK], gqv), _rms(z[:, MLA_Q_RANK:640], gkvv), _rope(z[:, 640:], csv, sw))
    return _rows_call(fn, [za, cs], [gq, gkv, swap], [(MLA_Q_RANK, BF16), (MLA_KV_RANK, BF16), (LANES, F32)],
                      name=name)


def _mla_prep_bwd(za, cs, dcq, dckv, dkr, gq, gkv, swap, name):
    def fn(z, csv, dcqv, dckvv, dkrv, gqv, gkvv, sw):
        d1, dg1 = _rms_bwd_math(z[:, :MLA_Q_RANK], gqv, dcqv)
        d2, dg2 = _rms_bwd_math(z[:, MLA_Q_RANK:640], gkvv, dckvv)
        d3 = _rope_t(dkrv, csv, sw)
        return jnp.concatenate([d1, d2, d3], axis=1), dg1, dg2
    return _rows_call(fn, [za, cs, dcq, dckv, dkr], [gq, gkv, swap], [(IN_A_PAD, BF16)],
                      [(1, MLA_Q_RANK), (1, MLA_KV_RANK)], name=name)


def _mla_qk(q_raw, k_pad, krr, cs, swap, name):
    w = MLA_HEADS * MLA_DK

    def fn(qv, kv, krv, csv, sw):
        return _rope(qv, csv, sw) * (MLA_SCALE * LOG2E), kv + _lanes(krv, w)
    return _rows_call(fn, [q_raw, k_pad, krr, cs], [swap], [(w, BF16), (w, BF16)], name=name)


def _mla_qk_bwd(dqh, dkh, cs, swap, name):
    w = MLA_HEADS * MLA_DK

    def fn(dq, dk, csv, sw):
        dk = dk * LN2
        acc = dk[:, :LANES]
        for h in range(1, MLA_HEADS):
            acc = acc + dk[:, h * LANES:(h + 1) * LANES]
        lane = lax.broadcasted_iota(jnp.int32, acc.shape, 1)
        acc = jnp.where((lane >= MLA_NOPE) & (lane < MLA_NOPE + MLA_ROPE), acc, 0.0)
        return _rope_t(dq * MLA_SCALE, csv, sw), dk, acc
    return _rows_call(fn, [dqh, dkh, cs], [swap], [(w, BF16), (w, BF16), (LANES, F32)], name=name)


def _head_norm(t, g2, seg):
    r = lax.rsqrt(_chunkdot(t * t, seg) * (1.0 / GQA_HEAD_DIM) + NORM_EPS)
    return t * r * _lanes(g2, t.shape[1]), r


def _head_norm_bwd(t, g2, seg, dn):
    w = t.shape[1]
    r = lax.rsqrt(_chunkdot(t * t, seg) * (1.0 / GQA_HEAD_DIM) + NORM_EPS)
    u = dn * _lanes(g2, w)
    dt = r * u - t * (r * r * r) * (_chunkdot(u * t, seg) * (1.0 / GQA_HEAD_DIM))
    dgw = jnp.sum(dn * t * r, axis=0, keepdims=True)
    dg = dgw[:, :LANES]
    for c in range(LANES, w, LANES):
        dg = dg + dgw[:, c:c + LANES]
    return dt, dg


def _gqa_prep(q_raw, kv_raw, cs, gq2, gk2, seg, swap, name):
    kw = GQA_KV_HEADS * GQA_HEAD_DIM

    def fn(qv, kvv, csv, gqv, gkv, sg, sw):
        qn, _ = _head_norm(qv, gqv, sg)
        kn, _ = _head_norm(kvv[:, :kw], gkv, sg)
        return _rope(qn, csv, sw) * (GQA_SCALE * LOG2E), _rope(kn, csv, sw), kvv[:, kw:]
    return _rows_call(fn, [q_raw, kv_raw, cs], [gq2, gk2, seg, swap],
                      [(GQA_HEADS * GQA_HEAD_DIM, BF16), (kw, BF16), (kw, BF16)], name=name)


def _gqa_prep_bwd(q_raw, kv_raw, cs, dqh, dkh, dv, gq2, gk2, seg, swap, name):
    kw = GQA_KV_HEADS * GQA_HEAD_DIM

    def fn(qv, kvv, csv, dq, dk, dvv, gqv, gkv, sg, sw):
        dqr, dgq = _head_norm_bwd(qv, gqv, sg, _rope_t(dq * GQA_SCALE, csv, sw))
        dkr, dgk = _head_norm_bwd(kvv[:, :kw], gkv, sg, _rope_t(dk * LN2, csv, sw))
        return dqr, jnp.concatenate([dkr, dvv], axis=1), dgq, dgk
    return _rows_call(fn, [q_raw, kv_raw, cs, dqh, dkh, dv], [gq2, gk2, seg, swap],
                      [(GQA_HEADS * GQA_HEAD_DIM, BF16), (2 * kw, BF16)], [(1, LANES), (1, LANES)], name=name)


def _loss_head(x, target, g, name):
    dm = x.shape[1]

    def fn(xv, tv, gv):
        err = _rms(xv, gv) - tv
        loss = 0.5 * jnp.sum(err * err) / dm
        dx, dg = _rms_bwd_math(xv, gv, err * (1.0 / dm))
        return dx, jnp.zeros((1, LANES), F32) + loss, dg
    return _rows_call(fn, [x, target], [g], [(dm, F32)], [(1, LANES), (1, dm)], name=name)


def _adamw(w, g, m, v, name):
    def fn(wv, gv, mv, vv):
        m2 = ADAM_B1 * mv + (1.0 - ADAM_B1) * gv
        v2 = ADAM_B2 * vv + (1.0 - ADAM_B2) * (gv * gv)
        m_hat = m2 / (1.0 - ADAM_B1 ** ADAM_STEP)
        v_hat = v2 / (1.0 - ADAM_B2 ** ADAM_STEP)
        return -ADAM_LR * (m_hat / (jnp.sqrt(v_hat) + ADAM_EPS) + ADAM_WD * wv), m2, v2
    c = w.shape[1]
    return _rows_call(fn, [w, g, m, v], [], [(c, F32)] * 3, bs=_pick(w.shape[0], 256, 8), name=name)


HBM_SPEC = pl.BlockSpec(memory_space=pltpu.HBM)
VMEM_SPEC = pl.BlockSpec(memory_space=pltpu.VMEM)


def _position():
    return lax.axis_index("x"), lax.axis_index("y"), lax.axis_index("c")


def _other_chips(x, y):
    return [(1 - x, y), (x, 1 - y), (1 - x, 1 - y)]


HALF_W = PACK_W // 2


def _cols(c):
    return pl.ds(pl.multiple_of(c * HALF_W, HALF_W), HALF_W)


def _all_gather_weights(packed):
    rows = packed.shape[0]

    def body(p_ref, g_ref, send_sems, recv_sems, local_sem):
        x, y, c = _position()
        chips = _other_chips(x, y)

        def half(chip, hc):
            return g_ref.at[2 * chip[0] + chip[1], :, _cols(hc)]

        def copy(j, src, dst, to):
            return pltpu.make_async_remote_copy(src_ref=src, dst_ref=dst, send_sem=send_sems.at[j],
                                                recv_sem=recv_sems.at[j], device_id=to, device_id_type=MESH)

        mine = pltpu.make_async_copy(p_ref, g_ref.at[2 * x + y], local_sem)
        mine.start()
        first = [copy(j, p_ref.at[:, _cols(c)], half((x, y), c), (*chip, c)) for j, chip in enumerate(chips)]
        for cp in first:
            cp.start()
        passed = [copy(3 + j, half(chip, c), half(chip, c), (x, y, 1 - c)) for j, chip in enumerate(chips)]
        for j, chip in enumerate(chips):
            copy(j, half(chip, c), half(chip, c), (x, y, c)).wait_recv()
            passed[j].start()
        for j, chip in enumerate(chips):
            copy(3 + j, half(chip, 1 - c), half(chip, 1 - c), (x, y, c)).wait_recv()
        for cp in first + passed:
            cp.wait_send()
        mine.wait()

    return pl.pallas_call(
        body, name="all_gather_weights", in_specs=[HBM_SPEC], out_specs=HBM_SPEC,
        out_shape=jax.ShapeDtypeStruct((4, rows, packed.shape[1]), packed.dtype),
        scratch_shapes=[pltpu.SemaphoreType.DMA((6,)), pltpu.SemaphoreType.DMA((6,)), pltpu.SemaphoreType.DMA],
    )(packed)


SEM_SPEC = pl.BlockSpec(memory_space=pltpu.SEMAPHORE)
ANY_SPEC = pl.BlockSpec(memory_space=pl.ANY)
DATAFLOW = pltpu.SideEffectType.DATAFLOW_SIDE_EFFECTING


def _hbm(a):
    return pltpu.with_memory_space_constraint(a, pltpu.HBM)


def _gather_start(packed, tag):
    rows = packed.shape[0]

    def body(p_ref, g_ref, send_sems, recv_sems, p_thru, g_thru, token):
        x, y, c = _position()
        for j, chip in enumerate(_other_chips(x, y)):
            for s in range(2):
                pltpu.make_async_remote_copy(
                    src_ref=p_ref.at[:, _cols(c)], dst_ref=g_ref.at[2 * x + y, :, _cols(c)],
                    send_sem=send_sems.at[2 * j + s], recv_sem=recv_sems.at[2 * j + s],
                    device_id=(*chip, 1 - c if s else c), device_id_type=MESH).start()
        token[...] = jnp.zeros_like(token)

    return pl.pallas_call(
        body, name=tag + "_start",
        out_shape=(pltpu.SemaphoreType.DMA((6,)), pltpu.SemaphoreType.DMA((6,)), pltpu.HBM(packed.shape, packed.dtype),
                   pltpu.HBM((4, rows, PACK_W), packed.dtype), jax.ShapeDtypeStruct((8, LANES), F32)),
        in_specs=(HBM_SPEC, HBM_SPEC), out_specs=(SEM_SPEC, SEM_SPEC, HBM_SPEC, HBM_SPEC, VMEM_SPEC),
        input_output_aliases={0: 2, 1: 3},
        compiler_params=pltpu.CompilerParams(has_side_effects=DATAFLOW),
    )(_hbm(packed), _hbm(lax.empty((4, rows, PACK_W), packed.dtype)))


def _gather_wait(send_sems, recv_sems, p_thru, g_thru, after, tag):
    def body(p_ref, g_ref, send_sems, recv_sems, after_ref, p_dead, got_ref):
        x, y, c = _position()
        for j, chip in enumerate(_other_chips(x, y)):
            for s in range(2):
                cp = pltpu.make_async_remote_copy(
                    src_ref=p_ref.at[:, _cols(c)], dst_ref=g_ref.at[2 * chip[0] + chip[1], :, _cols(1 - c if s else c)],
                    send_sem=send_sems.at[2 * j + s], recv_sem=recv_sems.at[2 * j + s],
                    device_id=(x, y, c), device_id_type=MESH)
                cp.wait_send()
                cp.wait_recv()

    return pl.pallas_call(
        body, name=tag + "_wait",
        out_shape=(pltpu.HBM(p_thru.shape, p_thru.dtype), pltpu.HBM(g_thru.shape, g_thru.dtype)),
        in_specs=(HBM_SPEC, HBM_SPEC, SEM_SPEC, SEM_SPEC, ANY_SPEC), out_specs=(HBM_SPEC, HBM_SPEC),
        input_output_aliases={0: 0, 1: 1},
        compiler_params=pltpu.CompilerParams(has_side_effects=DATAFLOW),
    )(p_thru, g_thru, send_sems, recv_sems, after)[1]


def _sibling_swap_halves(grads, tag):
    rows = grads.shape[1]

    def body(g_ref, a_ref, send_sem, recv_sem):
        x, y, c = _position()
        cp = pltpu.make_async_remote_copy(src_ref=g_ref.at[:, :, _cols(1 - c)], dst_ref=a_ref,
                                          send_sem=send_sem, recv_sem=recv_sem, device_id=(x, y, 1 - c),
                                          device_id_type=MESH)
        cp.start()
        cp.wait()

    return pl.pallas_call(
        body, name=tag + "_swap_cores", in_specs=[HBM_SPEC], out_specs=HBM_SPEC,
        out_shape=jax.ShapeDtypeStruct((4, rows, HALF_W), grads.dtype),
        scratch_shapes=[pltpu.SemaphoreType.DMA, pltpu.SemaphoreType.DMA],
    )(grads)


def _rs_block(rows):
    return max(d for d in range(16, 1601, 16) if rows % d == 0)


def _chip_sum(grads, other, c, tag):
    rows = other.shape[1]
    rb = _rs_block(rows)

    def body(c_ref, g_ref, a_ref, o_ref):
        o_ref[...] = (g_ref[...] + a_ref[...]).astype(o_ref.dtype)

    blk = (1, rb, HALF_W)
    return pl.pallas_call(
        body, name=tag + "_chip_sum",
        grid_spec=pltpu.PrefetchScalarGridSpec(
            num_scalar_prefetch=1, grid=(4, rows // rb),
            in_specs=[pl.BlockSpec(blk, lambda k, i, c_ref: (k, i, c_ref[0])),
                      pl.BlockSpec(blk, lambda k, i, c_ref: (k, i, 0))],
            out_specs=pl.BlockSpec(blk, lambda k, i, c_ref: (k, i, 0))),
        out_shape=jax.ShapeDtypeStruct(other.shape, BF16),
        compiler_params=_cparams(("parallel", "parallel")),
    )(jnp.reshape(c, (1,)).astype(jnp.int32), grads, other)


def _chip_copies(t_ref, b_ref, send_sems, recv_sems):
    x, y, c = _position()
    return [pltpu.make_async_remote_copy(src_ref=t_ref.at[2 * chip[0] + chip[1]], dst_ref=b_ref.at[j],
                                         send_sem=send_sems.at[j], recv_sem=recv_sems.at[j],
                                         device_id=(*chip, c), device_id_type=MESH)
            for j, chip in enumerate(_other_chips(x, y))]


def _send_chip_sums(sums, tag):
    def body(t_ref, b_ref, send_sems, recv_sems):
        copies = _chip_copies(t_ref, b_ref, send_sems, recv_sems)
        for cp in copies:
            cp.start()
        for cp in copies:
            cp.wait()

    return pl.pallas_call(
        body, name=tag + "_send_chips", in_specs=[HBM_SPEC], out_specs=HBM_SPEC,
        out_shape=jax.ShapeDtypeStruct((3,) + sums.shape[1:], sums.dtype),
        scratch_shapes=[pltpu.SemaphoreType.DMA((3,)), pltpu.SemaphoreType.DMA((3,))],
    )(sums)


def _send_chip_sums_start(sums, tag):
    land = (3,) + sums.shape[1:]

    def body(t_ref, b_ref, send_sems, recv_sems, t_thru, b_thru, token):
        for cp in _chip_copies(t_ref, b_ref, send_sems, recv_sems):
            cp.start()
        token[...] = jnp.zeros_like(token)

    return pl.pallas_call(
        body, name=tag + "_send_chips_start",
        out_shape=(pltpu.SemaphoreType.DMA((3,)), pltpu.SemaphoreType.DMA((3,)), pltpu.HBM(sums.shape, sums.dtype),
                   pltpu.HBM(land, sums.dtype), jax.ShapeDtypeStruct((8, LANES), F32)),
        in_specs=(HBM_SPEC, HBM_SPEC), out_specs=(SEM_SPEC, SEM_SPEC, HBM_SPEC, HBM_SPEC, VMEM_SPEC),
        input_output_aliases={0: 2, 1: 3},
        compiler_params=pltpu.CompilerParams(has_side_effects=DATAFLOW),
    )(_hbm(sums), _hbm(lax.empty(land, sums.dtype)))


def _send_chip_sums_wait(send_sems, recv_sems, t_thru, b_thru, after, tag):
    def body(t_ref, b_ref, send_sems, recv_sems, after_ref, t_dead, got_ref):
        for cp in _chip_copies(t_ref, b_ref, send_sems, recv_sems):
            cp.wait_send()
            cp.wait_recv()

    return pl.pallas_call(
        body, name=tag + "_send_chips_wait",
        out_shape=(pltpu.HBM(t_thru.shape, t_thru.dtype), pltpu.HBM(b_thru.shape, b_thru.dtype)),
        in_specs=(HBM_SPEC, HBM_SPEC, SEM_SPEC, SEM_SPEC, ANY_SPEC), out_specs=(HBM_SPEC, HBM_SPEC),
        input_output_aliases={0: 0, 1: 1},
        compiler_params=pltpu.CompilerParams(has_side_effects=DATAFLOW),
    )(t_thru, b_thru, send_sems, recv_sems, after)[1]


def _final_sum(grads, other, recv, k, c, tag):
    rows = other.shape[1]
    rb = _rs_block(rows)

    def body(k_ref, c_ref, g_ref, a_ref, b_ref, o_ref):
        own = g_ref[0] + a_ref[0]
        o_ref[...] = ((own + b_ref[0].astype(F32)) + b_ref[1].astype(F32)) + b_ref[2].astype(F32)

    return pl.pallas_call(
        body, name=tag + "_final_sum",
        grid_spec=pltpu.PrefetchScalarGridSpec(
            num_scalar_prefetch=2, grid=(rows // rb,),
            in_specs=[pl.BlockSpec((1, rb, HALF_W), lambda i, k_ref, c_ref: (k_ref[0], i, c_ref[0])),
                      pl.BlockSpec((1, rb, HALF_W), lambda i, k_ref, c_ref: (k_ref[0], i, 0)),
                      pl.BlockSpec((3, rb, HALF_W), lambda i, k_ref, c_ref: (0, i, 0))],
            out_specs=pl.BlockSpec((rb, HALF_W), lambda i, k_ref, c_ref: (i, 0))),
        out_shape=jax.ShapeDtypeStruct((rows, HALF_W), F32),
        compiler_params=_cparams(("parallel",)),
    )(jnp.reshape(k, (1,)).astype(jnp.int32), jnp.reshape(c, (1,)).astype(jnp.int32), grads, other, recv)


def _join_halves(half, core, tag):
    def body(h_ref, o_ref, send_sem, recv_sem):
        x, y, c = _position()
        cp = pltpu.make_async_remote_copy(src_ref=h_ref, dst_ref=o_ref, send_sem=send_sem, recv_sem=recv_sem,
                                          device_id=(x, y, 1 - c), device_id_type=MESH)
        cp.start()
        cp.wait()

    other = pl.pallas_call(
        body, name=tag + "_join_cores", in_specs=[HBM_SPEC], out_specs=HBM_SPEC,
        out_shape=jax.ShapeDtypeStruct(half.shape, half.dtype),
        scratch_shapes=[pltpu.SemaphoreType.DMA, pltpu.SemaphoreType.DMA],
    )(half)
    first = core == 0
    return jnp.concatenate([jnp.where(first, half, other), jnp.where(first, other, half)], axis=1)


def _all_reduce_packet(packet):
    rows = packet.shape[0]

    def body(p_ref, o_ref, buf, send_sems, recv_sems):
        x, y, c = _position()
        me = 4 * x + 2 * y + c
        buf[me] = p_ref[...]

        def flip(v, bit):
            return 1 - v if bit else v

        for p in range(1, 8):
            peer = (flip(x, p & 4), flip(y, p & 2), flip(c, p & 1))
            pltpu.make_async_remote_copy(src_ref=p_ref, dst_ref=buf.at[me], send_sem=send_sems.at[p - 1],
                                         recv_sem=recv_sems.at[p - 1], device_id=peer, device_id_type=MESH).start()
        for p in range(1, 8):
            peer = (flip(x, p & 4), flip(y, p & 2), flip(c, p & 1))
            slot = 4 * peer[0] + 2 * peer[1] + peer[2]
            cp = pltpu.make_async_remote_copy(src_ref=p_ref, dst_ref=buf.at[slot], send_sem=send_sems.at[p - 1],
                                              recv_sem=recv_sems.at[p - 1], device_id=peer, device_id_type=MESH)
            cp.wait_recv()
            cp.wait_send()
        acc = buf[0]
        for dev in range(1, 8):
            acc = acc + buf[dev]
        o_ref[...] = acc

    return pl.pallas_call(
        body, name="all_reduce_packet", in_specs=[VMEM_SPEC], out_specs=VMEM_SPEC,
        out_shape=jax.ShapeDtypeStruct(packet.shape, F32),
        scratch_shapes=[pltpu.VMEM((8, rows, LANES), F32), pltpu.SemaphoreType.DMA((7,)),
                        pltpu.SemaphoreType.DMA((7,))],
    )(packet)


def _stack_range(name, n_stack, pack):
    if name.startswith('gqa'):
        return (0, 0) if pack == 0 else (0, n_stack)
    return (0, 1) if pack == 0 else (1, n_stack)


def _pack_members(pack):
    out = []
    for n, shape, ax in BIG:
        lo, hi = _stack_range(n, shape[0], pack)
        if hi > lo:
            out.append((n, (hi - lo,) + shape[1:], ax, (lo, hi)))
    return out


PACK_ROW_MULTIPLE = 512


def _pad_rows(parts, dtype):
    rows = sum(p.shape[0] for p in parts)
    pad = -rows % PACK_ROW_MULTIPLE
    return jnp.concatenate(parts + ([jnp.zeros((pad, PACK_W), dtype)] if pad else []), axis=0)


def _pack_blocks(blocks, dtype, pack):
    return _pad_rows([blocks[n][lo:hi].astype(dtype).reshape(-1, PACK_W)
                      for n, _, _, (lo, hi) in _pack_members(pack)], dtype)


def _unpack_blocks(packed, pack):
    out, off = {}, 0
    for n, shape, _, _ in _pack_members(pack):
        r = math.prod(shape) // PACK_W
        out[n] = packed[off:off + r].reshape(shape)
        off += r
    return out


def _unpack_gathered(gathered, pack, own=None, chip=None):
    blocks = [gathered[k] if own is None else jnp.where(chip == k, own, gathered[k]) for k in range(4)]
    per_chip = [_unpack_blocks(blocks[k], pack) for k in range(4)]
    return {n: jnp.concatenate([per_chip[k][n] for k in range(4)], axis=ax) for n, _, ax, _ in _pack_members(pack)}


def _pack_full(full, dtype, pack):
    chips = []
    for k in range(4):
        parts = []
        for n, shape, ax, _ in _pack_members(pack):
            blk = lax.slice_in_dim(full[n], k * shape[ax], (k + 1) * shape[ax], axis=ax)
            parts.append(blk.astype(dtype).reshape(-1, PACK_W))
        chips.append(_pad_rows(parts, dtype))
    return jnp.stack(chips, axis=0)


def _pack_small(vals, loss_row):
    rows = [loss_row.reshape(1, LANES)]
    for n, shape in SMALL:
        v = vals.get(n)
        v = jnp.zeros(shape, F32) if v is None else v
        rows.append(v.astype(F32).reshape(-1, LANES))
    packet = jnp.concatenate(rows, axis=0)
    return jnp.pad(packet, ((0, PACKET_ROWS - packet.shape[0]), (0, 0)))


def _unpack_small(packet):
    out, off = {}, 1
    for n, shape in SMALL:
        r = math.prod(shape) // LANES
        out[n] = packet[off:off + r].reshape(shape)
        off += r
    return packet[0, 0], out


_MLA = dict(R=1, dk=MLA_DK, dv=MLA_V, hb=2, bq=512, bk=512)
_MLA_FWD_BQ = 1024
_GQA = dict(R=GQA_HEADS // GQA_KV_HEADS, dk=GQA_HEAD_DIM, dv=GQA_HEAD_DIM, hb=2, bq=256, bk=512)


def _layer_params(layer, full, gains):
    pack = 0 if layer == 0 else 1
    i = layer // 2

    def mat(name):
        lo, _ = _stack_range(name, 4 if name.startswith('ffn') else 2, pack)
        return full[name][(layer if name.startswith('ffn') else i) - lo]

    p = dict(ffn_norm=gains['ffn_norm'][layer][None], ffn_w_in=mat('ffn_w_in'), ffn_w_out=mat('ffn_w_out'))
    if layer % 2 == 0:
        w_in = mat('w_in_ab')
        zeros = jnp.zeros((D_MODEL, 32), w_in.dtype)
        p['w_a'] = jnp.concatenate([w_in[:, :640], zeros, zeros, w_in[:, 640:IN_A], zeros], axis=1)
        p['w_b'] = w_in[:, IN_A:]
        p['w_uq'] = jnp.pad(mat('mla_w_uq'), ((0, 0), (0, 0), (0, MLA_DK - 96))).reshape(MLA_Q_RANK, -1)
        ukv = mat('mla_w_ukv')
        p['w_uk'] = jnp.pad(ukv[:, :, :MLA_NOPE], ((0, 0), (0, 0), (0, MLA_DK - MLA_NOPE))).reshape(MLA_KV_RANK, -1)
        p['w_uv'] = ukv[:, :, MLA_NOPE:].reshape(MLA_KV_RANK, -1)
        p['w_out'] = mat('w_out_ab')
        p['mix_norm'] = gains['mix_norm_ab'][i][None]
        p['q_norm'] = gains['mla_q_norm'][i][None]
        p['kv_norm'] = gains['mla_kv_norm'][i][None]
    else:
        p['w_q'], p['w_kv'], p['w_o'] = mat('gqa_w_q'), mat('gqa_w_kv'), mat('gqa_w_o')
        p['mix_norm'] = gains['mix_norm_c'][i][None]
        p['q_norm'] = jnp.tile(gains['gqa_q_norm'][i][None], (1, 2))
        p['k_norm'] = jnp.tile(gains['gqa_k_norm'][i][None], (1, 2))
    return p


def _even_fwd(x, p, cs, swap, tag):
    xn = _rms_fwd(x, p['mix_norm'], tag + "_norm")
    za = _mm(xn, p['w_a'], name=tag + "_in_a")
    zb = _mm(xn, p['w_b'], out_dtype=BF16, name=tag + "_in_b")
    cq, ckv, krr = _mla_prep(za, p['q_norm'], p['kv_norm'], cs, swap, tag + "_mla_prep")
    q_raw = _mm(cq, p['w_uq'], name=tag + "_uq")
    k_pad = _mm(ckv, p['w_uk'], name=tag + "_uk")
    v = _mm(ckv, p['w_uv'], out_dtype=BF16, name=tag + "_uv")
    qh, kh = _mla_qk(q_raw, k_pad, krr, cs, swap, tag + "_mla_qk")
    o_a, lse_a = _flash_fwd(qh, kh, v, name=tag + "_mla_attn", **dict(_MLA, bq=_MLA_FWD_BQ))
    og, lg = [], []
    for grp in range(DIL_GROUPS):
        o, l = _dil_fwd(zb, grp, f"{tag}_dil{grp}")
        og.append(o)
        lg.append(l)
    o_b, lt = _dil_combine(og, lg, tag + "_dil_merge")
    ocat = jnp.concatenate([o_a, o_b], axis=1)
    x1 = _mm(ocat, p['w_out'], add=x, name=tag + "_out")
    saved = dict(x=x, xn=xn, za=za, zb=zb, cq=cq, ckv=ckv, qh=qh, kh=kh, v=v, lse_a=lse_a, og=og, lg=lg, lt=lt,
                 ocat=ocat)
    return x1, saved


def _even_bwd(dx1, p, sv, cs, swap, seg64, tag):
    docat = _mm(dx1, p['w_out'], mode="nt", name=tag + "_out_dx")
    d_w_out = _mm(sv['ocat'], dx1, mode="tn", name=tag + "_out_dw")
    n_a = MLA_HEADS * MLA_V
    do_a = docat[:, :n_a].astype(BF16)
    res = _dil_combine_bwd(docat[:, n_a:], sv['og'], sv['lg'], sv['lt'], seg64, tag + "_dil_merge_bwd")
    dqs, dks, dvs = [], [], []
    for grp in range(DIL_GROUPS):
        dq, dk, dv = _dil_bwd(sv['zb'], res[grp], sv['lg'][grp], res[3 + grp], grp, f"{tag}_dil{grp}_bwd")
        dqs.append(dq)
        dks.append(dk)
        dvs.append(dv)
    dzb = jnp.concatenate(dqs + dks + dvs, axis=1).astype(BF16)
    dqh, dkh, dv = _flash_bwd(sv['qh'], sv['kh'], sv['v'], sv['ocat'][:, :n_a], do_a, sv['lse_a'],
                              name=tag + "_mla_attn_bwd", **_MLA)
    dq_raw, dkh, dkrr = _mla_qk_bwd(dqh, dkh, cs, swap, tag + "_mla_qk_bwd")
    dcq = _mm(dq_raw, p['w_uq'], mode="nt", name=tag + "_uq_dx")
    d_w_uq = _mm(sv['cq'], dq_raw, mode="tn", name=tag + "_uq_dw")
    dckv = _mm(dkh, p['w_uk'], mode="nt", name=tag + "_uk_dx")
    dckv = _mm(dv, p['w_uv'], mode="nt", add=dckv, name=tag + "_uv_dx")
    d_w_uk = _mm(sv['ckv'], dkh, mode="tn", name=tag + "_uk_dw")
    d_w_uv = _mm(sv['ckv'], dv, mode="tn", name=tag + "_uv_dw")
    dza, d_gq, d_gkv = _mla_prep_bwd(sv['za'], cs, dcq, dckv, dkrr, p['q_norm'], p['kv_norm'], swap,
                                     tag + "_mla_prep_bwd")
    dxn = _mm(dza, p['w_a'], mode="nt", name=tag + "_in_a_dx")
    dxn = _mm(dzb, p['w_b'], mode="nt", add=dxn, name=tag + "_in_b_dx")
    d_w_a = _mm(sv['xn'], dza, mode="tn", name=tag + "_in_a_dw")
    d_w_b = _mm(sv['xn'], dzb, mode="tn", name=tag + "_in_b_dw")
    dx, d_g = _rms_bwd(sv['x'], p['mix_norm'], dxn, dx1, tag + "_norm_bwd")
    d_w_in = jnp.concatenate([d_w_a[:, :640], d_w_a[:, 704:736], d_w_b], axis=1)
    d_uq = d_w_uq.reshape(MLA_Q_RANK, MLA_HEADS, MLA_DK)[:, :, :MLA_NOPE + MLA_ROPE]
    d_ukv = jnp.concatenate([d_w_uk.reshape(MLA_KV_RANK, MLA_HEADS, MLA_DK)[:, :, :MLA_NOPE],
                             d_w_uv.reshape(MLA_KV_RANK, MLA_HEADS, MLA_V)], axis=2)
    grads = dict(w_in_ab=d_w_in, mla_w_uq=d_uq, mla_w_ukv=d_ukv, w_out_ab=d_w_out, mix_norm_ab=d_g[0],
                 mla_q_norm=d_gq[0], mla_kv_norm=d_gkv[0])
    return dx, grads


def _odd_fwd(x, p, cs, seg64, swap, tag):
    xn = _rms_fwd(x, p['mix_norm'], tag + "_norm")
    q_raw = _mm(xn, p['w_q'], name=tag + "_q")
    kv_raw = _mm(xn, p['w_kv'], name=tag + "_kv")
    qh, kh, v = _gqa_prep(q_raw, kv_raw, cs, p['q_norm'], p['k_norm'], seg64, swap, tag + "_gqa_prep")
    o, lse = _flash_fwd(qh, kh, v, name=tag + "_gqa_attn", **_GQA)
    x1 = _mm(o, p['w_o'], add=x, name=tag + "_o")
    return x1, dict(x=x, xn=xn, q_raw=q_raw, kv_raw=kv_raw, qh=qh, kh=kh, v=v, o=o, lse=lse)


def _odd_bwd(dx1, p, sv, cs, seg64, swap, tag):
    do = _mm(dx1, p['w_o'], mode="nt", out_dtype=BF16, name=tag + "_o_dx")
    d_w_o = _mm(sv['o'], dx1, mode="tn", name=tag + "_o_dw")
    dqh, dkh, dv = _flash_bwd(sv['qh'], sv['kh'], sv['v'], sv['o'], do, sv['lse'], name=tag + "_gqa_attn_bwd",
                              **_GQA)
    dq_raw, dkv_raw, d_gq, d_gk = _gqa_prep_bwd(sv['q_raw'], sv['kv_raw'], cs, dqh, dkh, dv, p['q_norm'],
                                                p['k_norm'], seg64, swap, tag + "_gqa_prep_bwd")
    dxn = _mm(dq_raw, p['w_q'], mode="nt", name=tag + "_q_dx")
    dxn = _mm(dkv_raw, p['w_kv'], mode="nt", add=dxn, name=tag + "_kv_dx")
    d_w_q = _mm(sv['xn'], dq_raw, mode="tn", name=tag + "_q_dw")
    d_w_kv = _mm(sv['xn'], dkv_raw, mode="tn", name=tag + "_kv_dw")
    dx, d_g = _rms_bwd(sv['x'], p['mix_norm'], dxn, dx1, tag + "_norm_bwd")
    grads = dict(gqa_w_q=d_w_q, gqa_w_kv=d_w_kv, gqa_w_o=d_w_o, mix_norm_c=d_g[0],
                 gqa_q_norm=d_gq[0, :GQA_HEAD_DIM] + d_gq[0, GQA_HEAD_DIM:],
                 gqa_k_norm=d_gk[0, :GQA_HEAD_DIM] + d_gk[0, GQA_HEAD_DIM:])
    return dx, grads


def _ffn_fwd(x, p, tag):
    xn = _rms_fwd(x, p['ffn_norm'], tag + "_ffn_norm")
    h = _mm(xn, p['ffn_w_in'], out_dtype=BF16, name=tag + "_ffn_in")
    x2 = _mm(h, p['ffn_w_out'], gated=True, add=x, name=tag + "_ffn_out")
    return x2, dict(x=x, xn=xn, h=h)


def _ffn_bwd(dx2, p, sv, tag):
    d_gate, d_up = _mm(dx2, p['ffn_w_out'], mode="nt", gate_up=sv['h'], out_dtype=BF16, name=tag + "_ffn_out_dx")
    d_w_out = _mm(sv['h'], dx2, mode="tn", gated=True, name=tag + "_ffn_out_dw")
    dh = jnp.concatenate([d_gate, d_up], axis=1)
    d_w_in = _mm(sv['xn'], dh, mode="tn", name=tag + "_ffn_in_dw")
    dxn = _mm(dh, p['ffn_w_in'], mode="nt", name=tag + "_ffn_in_dx")
    dx, d_g = _rms_bwd(sv['x'], p['ffn_norm'], dxn, dx2, tag + "_ffn_norm_bwd")
    return dx, d_w_in, d_w_out, d_g[0]


EVEN_MATS = ('w_in_ab', 'mla_w_uq', 'mla_w_ukv', 'w_out_ab')
ODD_MATS = ('gqa_w_q', 'gqa_w_kv', 'gqa_w_o')
FFN_MATS = ('ffn_w_in', 'ffn_w_out')


def _schedule(x, target, gains, full_of_pack, rest_grads_ready):
    s = x.shape[0]
    cs_mla, cs_gqa = _rope_tables(s)
    swap, seg64 = _swap_matrix(), _seg_matrix(GQA_HEAD_DIM)
    params, saved, full = [], [], None
    for layer in range(4):
        tag = f"l{layer}"
        if layer < 2:
            full = full_of_pack(layer, x)
        p = _layer_params(layer, full, gains)
        if layer % 2 == 0:
            x, sv = _even_fwd(x, p, cs_mla, swap, tag)
        else:
            x, sv = _odd_fwd(x, p, cs_gqa, seg64, swap, tag)
        x, sv_f = _ffn_fwd(x, p, tag)
        params.append(p)
        saved.append((sv, sv_f))
    dx, loss_row, d_final = _loss_head(x, target, gains['final_norm'][None], "loss_head")

    per_layer, rest = {}, None
    for layer in reversed(range(4)):
        p, (sv, sv_f), tag = params[layer], saved[layer], f"l{layer}"
        if layer == 0:
            rest = {n: per_layer[2][n][None] for n in EVEN_MATS}
            rest.update({n: jnp.stack([per_layer[1][n], per_layer[3][n]], axis=0) for n in ODD_MATS})
            rest.update({n: jnp.stack([per_layer[l][n] for l in (1, 2, 3)], axis=0) for n in FFN_MATS})
            token = rest_grads_ready(rest)
            if token is not None:
                p = dict(p, ffn_w_out=p['ffn_w_out'] + token[0, 0].astype(p['ffn_w_out'].dtype))
        dx, d_ffn_in, d_ffn_out, d_ffn_g = _ffn_bwd(dx, p, sv_f, tag)
        if layer % 2 == 0:
            dx, g = _even_bwd(dx, p, sv, cs_mla, swap, seg64, tag)
        else:
            dx, g = _odd_bwd(dx, p, sv, cs_gqa, seg64, swap, tag)
        g.update(ffn_w_in=d_ffn_in, ffn_w_out=d_ffn_out, ffn_norm=d_ffn_g)
        per_layer[layer] = g

    first = {n: per_layer[0][n][None] for n in EVEN_MATS + FFN_MATS}
    small = {'final_norm': d_final[0], 'ffn_norm': jnp.stack([per_layer[l]['ffn_norm'] for l in range(4)], axis=0)}
    for n in ('mix_norm_ab', 'mla_q_norm', 'mla_kv_norm'):
        small[n] = jnp.stack([per_layer[0][n], per_layer[2][n]], axis=0)
    for n in ('mix_norm_c', 'gqa_q_norm', 'gqa_k_norm'):
        small[n] = jnp.stack([per_layer[1][n], per_layer[3][n]], axis=0)
    return loss_row, dx, first, rest, small


def _core_sums(grads, pack, core, tag):
    packed = _pack_full(grads, F32, pack)
    other = _sibling_swap_halves(packed, tag)
    return packed, other, _chip_sum(packed, other, core, tag)


def _finish_reduce_scatter(packed, other, recv, pack, chip, core, tag):
    return _unpack_blocks(_join_halves(_final_sum(packed, other, recv, chip, core, tag), core, tag), pack)


def _step(x, target, w, m, v):
    big_names = [n for n, _, _ in BIG]
    chip = 2 * lax.axis_index("x") + lax.axis_index("y")
    core = lax.axis_index("c")

    gains = {n: w[n] for n, _ in SMALL if n != 'mix_norm_c'}
    c_cols = w['mix_norm_c'].shape[1]
    own_c = lax.dynamic_update_slice(jnp.zeros((2, 4 * c_cols), F32), w['mix_norm_c'], (0, chip * c_cols))
    gains['mix_norm_c'] = _unpack_small(_all_reduce_packet(_pack_small(
        {'mix_norm_c': own_c * 0.5}, jnp.zeros((LANES,), F32))))[1]['mix_norm_c']

    gathered0 = _all_gather_weights(_pack_blocks(w, BF16, 0))
    packed1, gathered0 = lax.optimization_barrier((_pack_blocks(w, BF16, 1), gathered0))
    ag_send, ag_recv, p_thru, g_thru, ag_token = _gather_start(packed1, "gather_rest")
    gains['mix_norm_ab'] = gains['mix_norm_ab'] + ag_token[0, 0]
    full0 = _unpack_gathered(gathered0, 0)

    def full_of_pack(pack, after):
        if pack == 0:
            return full0
        landed = _gather_wait(ag_send, ag_recv, p_thru, g_thru, after, "gather_rest")
        return _unpack_gathered(landed, 1, own=packed1, chip=chip)

    rs = {}

    def rest_grads_ready(rest):
        rs['packed'], rs['other'], sums = _core_sums(rest, 1, core, "grad_rest")
        rs['send'], rs['recv'], rs['t'], rs['b'], token = _send_chip_sums_start(sums, "grad_rest")
        return token

    loss_row, dx, first, rest, small = _schedule(x[0], target[0], gains, full_of_pack, rest_grads_ready)
    recv1 = _send_chip_sums_wait(rs['send'], rs['recv'], rs['t'], rs['b'], dx, "grad_rest")
    g_rest = _finish_reduce_scatter(rs['packed'], rs['other'], recv1, 1, chip, core, "grad_rest")
    packed0, other0, sums0 = _core_sums(first, 0, core, "grad_first")
    g_first = _finish_reduce_scatter(packed0, other0, _send_chip_sums(sums0, "grad_first"), 0, chip, core,
                                     "grad_first")
    g_blocks = {n: (jnp.concatenate([g_first[n], g_rest[n]], axis=0) if n in g_first else g_rest[n])
                for n in big_names}

    loss, g_small = _unpack_small(_all_reduce_packet(_pack_small(small, loss_row[0])))
    g_small['mix_norm_c'] = lax.dynamic_slice(g_small['mix_norm_c'], (0, chip * c_cols), (2, c_cols))

    out_g, out_d, out_m, out_v = {}, {}, {}, {}
    for n in big_names:
        shape = w[n].shape
        cols = shape[-1]
        d_, m_, v_ = _adamw(w[n].reshape(-1, cols), g_blocks[n].reshape(-1, cols), m[n].reshape(-1, cols),
                            v[n].reshape(-1, cols), "adamw_" + n)
        out_g[n], out_d[n], out_m[n], out_v[n] = g_blocks[n], d_.reshape(shape), m_.reshape(shape), v_.reshape(shape)
    for n, _ in SMALL:
        shape = w[n].shape
        as2d = (lambda t: t.reshape(1, -1)) if len(shape) == 1 else (lambda t: t)
        d_, m_, v_ = _adamw(as2d(w[n]), as2d(g_small[n]), as2d(m[n]), as2d(v[n]), "adamw_" + n)
        out_g[n], out_d[n], out_m[n], out_v[n] = g_small[n], d_.reshape(shape), m_.reshape(shape), v_.reshape(shape)
    return (loss, dx[None], *[out_g[n] for n in WEIGHTS], *[out_d[n] for n in WEIGHTS],
            *[out_m[n] for n in WEIGHTS], *[out_v[n] for n in WEIGHTS])


def kernel(x, mix_norm_ab, w_in_ab, mla_q_norm, mla_kv_norm, mla_w_uq, mla_w_ukv, w_out_ab, mix_norm_c, gqa_w_q, gqa_w_kv, gqa_q_norm, gqa_k_norm, gqa_w_o, ffn_norm, ffn_w_in, ffn_w_out, final_norm, loss_target, m_mix_norm_ab, m_w_in_ab, m_mla_q_norm, m_mla_kv_norm, m_mla_w_uq, m_mla_w_ukv, m_w_out_ab, m_mix_norm_c, m_gqa_w_q, m_gqa_w_kv, m_gqa_q_norm, m_gqa_k_norm, m_gqa_w_o, m_ffn_norm, m_ffn_w_in, m_ffn_w_out, m_final_norm, v_mix_norm_ab, v_w_in_ab, v_mla_q_norm, v_mla_kv_norm, v_mla_w_uq, v_mla_w_ukv, v_w_out_ab, v_mix_norm_c, v_gqa_w_q, v_gqa_w_kv, v_gqa_q_norm, v_gqa_k_norm, v_gqa_w_o, v_ffn_norm, v_ffn_w_in, v_ffn_w_out, v_final_norm):
    w = dict(zip(WEIGHTS, (mix_norm_ab, w_in_ab, mla_q_norm, mla_kv_norm, mla_w_uq, mla_w_ukv, w_out_ab, mix_norm_c,
                           gqa_w_q, gqa_w_kv, gqa_q_norm, gqa_k_norm, gqa_w_o, ffn_norm, ffn_w_in, ffn_w_out,
                           final_norm)))
    m = dict(zip(WEIGHTS, (m_mix_norm_ab, m_w_in_ab, m_mla_q_norm, m_mla_kv_norm, m_mla_w_uq, m_mla_w_ukv,
                           m_w_out_ab, m_mix_norm_c, m_gqa_w_q, m_gqa_w_kv, m_gqa_q_norm, m_gqa_k_norm, m_gqa_w_o,
                           m_ffn_norm, m_ffn_w_in, m_ffn_w_out, m_final_norm)))
    v = dict(zip(WEIGHTS, (v_mix_norm_ab, v_w_in_ab, v_mla_q_norm, v_mla_kv_norm, v_mla_w_uq, v_mla_w_ukv,
                           v_w_out_ab, v_mix_norm_c, v_gqa_w_q, v_gqa_w_kv, v_gqa_q_norm, v_gqa_k_norm, v_gqa_w_o,
                           v_ffn_norm, v_ffn_w_in, v_ffn_w_out, v_final_norm)))
    return _step(x, loss_target, w, m, v)
```

```python
import math

import numpy as np
import jax
import jax.numpy as jnp
from jax import lax
from jax.experimental import pallas as pl
from jax.experimental.pallas import tpu as pltpu

F32 = jnp.float32
BF16 = jnp.bfloat16
MESH = pl.DeviceIdType.MESH

VMEM_LIMIT_BYTES = 56 * 1024 * 1024
LANES = 128

D_MODEL = 1024
NORM_EPS = 1e-6
ROPE_THETA = 10000.0
NEG_INF = -1e30
GRID_W = 64

MLA_HEADS, MLA_Q_RANK, MLA_KV_RANK, MLA_NOPE, MLA_ROPE, MLA_V = 8, 384, 256, 64, 32, 64
MLA_DK = 128
DIL_PAIRS = ((128, 1), (512, 4), (2048, 16))
DIL_HALF, DIL_SLOTS, DIL_GROUPS, DIL_HEAD_DIM = 64, 4, 3, 64
DIL_HEADS = DIL_SLOTS * DIL_GROUPS
DIL_W = DIL_SLOTS * DIL_HEAD_DIM
GQA_HEADS, GQA_KV_HEADS, GQA_HEAD_DIM = 16, 4, 64
FFN_HIDDEN = 2816
IN_A = MLA_Q_RANK + MLA_KV_RANK + MLA_ROPE
IN_A_PAD = 768
IN_B = 3 * DIL_HEADS * DIL_HEAD_DIM

ADAM_LR, ADAM_B1, ADAM_B2, ADAM_EPS, ADAM_WD, ADAM_STEP = 0.001, 0.9, 0.999, 1e-08, 0.01, 10

LOG2E, LN2 = math.log2(math.e), math.log(2.0)
MLA_SCALE = (MLA_NOPE + MLA_ROPE) ** -0.5
GQA_SCALE = GQA_HEAD_DIM ** -0.5

WEIGHTS = ['mix_norm_ab', 'w_in_ab', 'mla_q_norm', 'mla_kv_norm', 'mla_w_uq', 'mla_w_ukv', 'w_out_ab', 'mix_norm_c',
           'gqa_w_q', 'gqa_w_kv', 'gqa_q_norm', 'gqa_k_norm', 'gqa_w_o', 'ffn_norm', 'ffn_w_in', 'ffn_w_out',
           'final_norm']
BIG = (('w_in_ab', (2, 1024, 744), 2), ('mla_w_uq', (2, 96, 8, 96), 1), ('mla_w_ukv', (2, 64, 8, 128), 1),
       ('w_out_ab', (2, 768, 256), 2), ('gqa_w_q', (2, 256, 1024), 1), ('gqa_w_kv', (2, 256, 512), 1),
       ('gqa_w_o', (2, 256, 1024), 1), ('ffn_w_in', (4, 1024, 1408), 2), ('ffn_w_out', (4, 704, 1024), 1))
PACK_W = 1024
SMALL = (('mix_norm_ab', (2, 1024)), ('mla_q_norm', (2, 384)), ('mla_kv_norm', (2, 256)), ('gqa_q_norm', (2, 64)),
         ('gqa_k_norm', (2, 64)), ('ffn_norm', (4, 1024)), ('final_norm', (1024,)), ('mix_norm_c', (2, 1024)))
PACKET_ROWS = 88


def _cparams(sem=None):
    return pltpu.CompilerParams(dimension_semantics=sem, vmem_limit_bytes=VMEM_LIMIT_BYTES)


def _pick(n, pref, mult=LANES):
    if n <= pref:
        return n
    for d in range(pref - pref % mult, 0, -mult):
        if n % d == 0:
            return d
    return n


_DIMS = {"nn": (((1,), (0,)), ((), ())), "nt": (((1,), (1,)), ((), ())), "tn": (((0,), (0,)), ((), ()))}


def _sigmoid(x):
    return 0.5 * (1.0 + jnp.tanh(0.5 * x))


def _silu_mul(gate, up):
    gate, up = gate.astype(F32), up.astype(F32)
    return gate * _sigmoid(gate) * up


def _silu_mul_bwd(gate, up, da):
    gate, up = gate.astype(F32), up.astype(F32)
    sig = _sigmoid(gate)
    silu = gate * sig
    return da * up * (sig + silu * (1.0 - sig)), da * silu


def _mm(a, b, *, mode="nn", add=None, out_dtype=F32, gated=False, gate_up=None, name="mm"):
    if mode == "nn":
        (m, k), (k2, n) = a.shape, b.shape
    elif mode == "nt":
        (m, k), (n, k2) = a.shape, b.shape
    else:
        (k, m), (k2, n) = a.shape, b.shape
    if gated:
        k, m = (k // 2, m) if mode == "nn" else (k, m // 2)
    assert k == k2, (a.shape, b.shape, mode)
    if mode == "tn":
        deep = a.dtype == BF16 and b.dtype == BF16
        bm, bn, bk = _pick(m, 1408), _pick(n, 1024), _pick(k, 2048 if deep else 1024, 16)
    else:
        bm, bn, bk = _pick(m, 512, 16), _pick(n, 1408), _pick(k, 2816)
    nk = k // bk
    assert m % bm == 0 and n % bn == 0 and k % bk == 0
    has_add, has_gu = add is not None, gate_up is not None
    assert not (has_add and has_gu) and not (gated and mode == "nt") and not (has_gu and mode != "nt")
    n_in = 2 + int(gated) + int(has_add) + 2 * int(has_gu)

    def body(*refs):
        a_val = _silu_mul(refs[0][...], refs[1][...]) if gated else refs[0][...]
        b_ref = refs[1 + int(gated)]
        part = lax.dot_general(a_val.astype(BF16), b_ref[...].astype(BF16), _DIMS[mode],
                               preferred_element_type=F32)

        def finish(r):
            if has_gu:
                d_gate, d_up = _silu_mul_bwd(refs[2][...], refs[3][...], r)
                refs[n_in][...] = d_gate.astype(refs[n_in].dtype)
                refs[n_in + 1][...] = d_up.astype(refs[n_in + 1].dtype)
                return
            if has_add:
                r = r + refs[n_in - 1][...]
            refs[n_in][...] = r.astype(refs[n_in].dtype)

        if nk == 1:
            finish(part)
        else:
            acc_ref = refs[-1]
            kk = pl.program_id(2)

            @pl.when(kk == 0)
            def _():
                acc_ref[...] = part

            @pl.when(kk > 0)
            def _():
                acc_ref[...] += part

            @pl.when(kk == nk - 1)
            def _():
                finish(acc_ref[...])

    a_bytes, b_bytes = a.size * a.dtype.itemsize, b.size * b.dtype.itemsize
    n_outer = nk == 1 and (n // bn) * a_bytes + b_bytes < a_bytes + (m // bm) * b_bytes

    def at(f):
        return (lambda j, i, kk: f(i, j, kk)) if n_outer else f

    if mode == "nn":
        a_specs = [pl.BlockSpec((bm, bk), at(lambda i, j, kk, o=o: (i, kk + o))) for o in ((0, nk) if gated else (0,))]
        b_spec = pl.BlockSpec((bk, bn), at(lambda i, j, kk: (kk, j)))
    elif mode == "nt":
        a_specs = [pl.BlockSpec((bm, bk), at(lambda i, j, kk: (i, kk)))]
        b_spec = pl.BlockSpec((bn, bk), at(lambda i, j, kk: (j, kk)))
    else:
        a_specs = [pl.BlockSpec((bk, bm), at(lambda i, j, kk, o=o: (kk, i + o)))
                   for o in ((0, m // bm) if gated else (0,))]
        b_spec = pl.BlockSpec((bk, bn), at(lambda i, j, kk: (kk, j)))
    o_spec = pl.BlockSpec((bm, bn), at(lambda i, j, kk: (i, j)))
    in_specs, args = a_specs + [b_spec], [a] * len(a_specs) + [b]
    if has_add:
        in_specs, args = in_specs + [o_spec], args + [add]
    if has_gu:
        in_specs += [o_spec, pl.BlockSpec((bm, bn), at(lambda i, j, kk: (i, j + n // bn)))]
        args += [gate_up, gate_up]
    out = jax.ShapeDtypeStruct((m, n), out_dtype)
    grid = (n // bn, m // bm, nk) if n_outer else (m // bm, n // bn, nk)
    return pl.pallas_call(
        body, name=name, grid=grid, in_specs=in_specs, out_specs=[o_spec, o_spec] if has_gu else o_spec,
        out_shape=[out, out] if has_gu else out,
        scratch_shapes=[pltpu.VMEM((bm, bn), F32)] if nk > 1 else [],
        compiler_params=_cparams(("parallel", "parallel", "arbitrary")),
    )(*args)


def _rows_call(fn, rows, consts, out_rows, out_accs=(), *, bs=256, name):
    s = rows[0].shape[0]
    bs = min(bs, s)
    assert s % bs == 0
    nr, nc, no, na = len(rows), len(consts), len(out_rows), len(out_accs)

    def body(*refs):
        vals = [r[...] for r in refs[:nr + nc]]
        outs = refs[nr + nc:]
        res = fn(*vals)
        if not isinstance(res, (tuple, list)):
            res = (res,)
        assert len(res) == no + na, (len(res), no, na)
        for r, v in zip(outs[:no], res[:no]):
            r[...] = v.astype(r.dtype)
        if na:
            i = pl.program_id(0)
            for r, v in zip(outs[no:], res[no:]):
                @pl.when(i == 0)
                def _(r=r, v=v):
                    r[...] = v

                @pl.when(i > 0)
                def _(r=r, v=v):
                    r[...] += v

    in_specs = [pl.BlockSpec((bs, a.shape[1]), lambda i: (i, 0)) for a in rows]
    in_specs += [pl.BlockSpec(c.shape, lambda i: (0, 0)) for c in consts]
    out_specs = [pl.BlockSpec((bs, c), lambda i: (i, 0)) for c, _ in out_rows]
    out_specs += [pl.BlockSpec(tuple(sh), lambda i: (0, 0)) for sh in out_accs]
    out_shape = [jax.ShapeDtypeStruct((s, c), dt) for c, dt in out_rows]
    out_shape += [jax.ShapeDtypeStruct(tuple(sh), F32) for sh in out_accs]
    res = pl.pallas_call(
        body, name=name, grid=(s // bs,), in_specs=in_specs, out_specs=out_specs, out_shape=out_shape,
        compiler_params=_cparams(("arbitrary",) if na else ("parallel",)),
    )(*rows, *consts)
    return res


def _rms(x, g):
    return x * lax.rsqrt(jnp.mean(x * x, axis=-1, keepdims=True) + NORM_EPS) * g


def _rms_bwd_math(x, g, dy):
    r = lax.rsqrt(jnp.mean(x * x, axis=-1, keepdims=True) + NORM_EPS)
    u = dy * g
    dx = r * u - x * (r * r * r) * jnp.mean(u * x, axis=-1, keepdims=True)
    dg = jnp.sum(dy * x * r, axis=0, keepdims=True)
    return dx, dg


def _rms_fwd(x, g, name):
    return _rows_call(lambda xv, gv: _rms(xv, gv), [x], [g], [(x.shape[1], BF16)], name=name)[0]


def _rms_bwd(x, g, dy, dres, name):
    def fn(xv, dyv, dresv, gv):
        dx, dg = _rms_bwd_math(xv, gv, dyv.astype(F32))
        return dx + dresv, dg
    return _rows_call(fn, [x, dy, dres], [g], [(x.shape[1], F32)], [(1, x.shape[1])], name=name)


def _chunkdot(x, m):
    outs = [jnp.dot(x[:, c:c + LANES], m, precision=lax.Precision.HIGHEST, preferred_element_type=F32)
            for c in range(0, x.shape[1], LANES)]
    return outs[0] if len(outs) == 1 else jnp.concatenate(outs, axis=1)


def _lanes(t, width):
    n = width // LANES
    return t if n == 1 else jnp.concatenate([t] * n, axis=1)


def _rope(x, cs, swap):
    w = x.shape[1]
    return x * _lanes(cs[:, :LANES], w) + _chunkdot(x, swap) * _lanes(cs[:, LANES:], w)


def _rope_t(dy, cs, swap):
    w = dy.shape[1]
    return dy * _lanes(cs[:, :LANES], w) + _chunkdot(dy * _lanes(cs[:, LANES:], w), swap)


def _swap_matrix():
    m = np.zeros((LANES, LANES), np.float32)
    for j in range(LANES):
        src = j + 16 if (j % 32) < 16 else j - 16
        m[src, j] = 1.0
    return jnp.asarray(m)


def _seg_matrix(seg):
    idx = np.arange(LANES) // seg
    return jnp.asarray((idx[:, None] == idx[None, :]).astype(np.float32))


def _rope_tables(s):
    pos = jnp.arange(s)

    def angles(p, dim):
        freqs = ROPE_THETA ** (-jnp.arange(0, dim, 2, dtype=F32) / dim)
        ang = p.astype(F32)[:, None] * freqs[None, :]
        return jnp.cos(ang), jnp.sin(ang)

    cos_t, sin_t = angles(pos, MLA_ROPE)
    one, zero = jnp.ones((s, 64), F32), jnp.zeros((s, 64), F32)
    mla = jnp.concatenate([one, cos_t, cos_t, one[:, :32], zero, -sin_t, sin_t, zero[:, :32]], axis=1)
    cos_r, sin_r = angles(pos // GRID_W, GQA_HEAD_DIM // 2)
    cos_c, sin_c = angles(pos % GRID_W, GQA_HEAD_DIM // 2)
    c64 = jnp.concatenate([cos_r, cos_r, cos_c, cos_c], axis=1)
    s64 = jnp.concatenate([-sin_r, sin_r, -sin_c, sin_c], axis=1)
    gqa = jnp.concatenate([c64, c64, s64, s64], axis=1)
    return mla, gqa


def _stack_heads(ref, heads, d, dtype=None):
    parts = [ref[:, hd * d:(hd + 1) * d] for hd in heads]
    out = parts[0] if len(parts) == 1 else jnp.concatenate(parts, axis=0)
    return out if dtype is None else out.astype(dtype)


def _fill_v_ones(v_ref, va_ref, hb, dv):
    @pl.when(pl.program_id(1) == 0)
    def _():
        ones = jnp.ones((v_ref.shape[0], dv), BF16)
        for h in range(hb):
            va_ref[:, 2 * h * dv:(2 * h + 1) * dv] = v_ref[:, h * dv:(h + 1) * dv]
            va_ref[:, (2 * h + 1) * dv:(2 * h + 2) * dv] = ones


def _flash_fwd(q, k, v, *, R, dk, dv, hb, bq, bk, name):
    s = q.shape[0]
    g = k.shape[1] // dk
    ng = g // hb
    bq, bk = min(bq, s), min(bk, s)
    nq, nkb = s // bq, s // bk
    rb = R * bq

    def body(q_ref, k_ref, v_ref, o_ref, lse_ref, va_ref):
        _fill_v_ones(v_ref, va_ref, hb, dv)
        head_sets = [[h * R + r for r in range(R)] for h in range(hb)]
        qss = [_stack_heads(q_ref, heads, dk) for heads in head_sets]

        def step(jj, carry):
            carry = list(carry)
            rows = [pl.ds(pl.multiple_of((jj * unroll + u) * bk, bk), bk) for u in range(unroll)]
            scs = [[lax.dot_general(qss[h], k_ref[rows[u], h * dk:(h + 1) * dk], _DIMS["nt"],
                                    preferred_element_type=F32) for h in range(hb)] for u in range(unroll)]
            for u in range(unroll):
                m2s = [jnp.maximum(carry[h][0], jnp.max(scs[u][h], axis=1, keepdims=True)) for h in range(hb)]
                ps = [jnp.exp2(scs[u][h] - m2s[h]).astype(BF16) for h in range(hb)]
                pvs = [jnp.dot(ps[h], va_ref[rows[u], 2 * h * dv:2 * (h + 1) * dv], preferred_element_type=F32)
                       for h in range(hb)]
                carry = [(m2s[h], jnp.exp2(carry[h][0] - m2s[h]) * carry[h][1] + pvs[h]) for h in range(hb)]
            return tuple(carry)

        unroll = 2 if nkb % 2 == 0 else 1
        init = tuple((jnp.full((rb, 1), NEG_INF, F32), jnp.zeros((rb, 2 * dv), F32)) for _ in range(hb))
        final = lax.fori_loop(0, nkb // unroll, step, init)
        for h, heads in enumerate(head_sets):
            m, acc = final[h]
            l = acc[:, dv:dv + 1]
            o = acc[:, :dv] / l
            lse = m + jnp.log2(l)
            for r, hd in enumerate(heads):
                o_ref[:, hd * dv:(hd + 1) * dv] = o[r * bq:(r + 1) * bq].astype(o_ref.dtype)
                lse_ref[0, :, hd:hd + 1] = lse[r * bq:(r + 1) * bq]

    return pl.pallas_call(
        body, name=name, grid=(ng, nq),
        in_specs=[pl.BlockSpec((bq, hb * R * dk), lambda gi, i: (i, gi)),
                  pl.BlockSpec((s, hb * dk), lambda gi, i: (0, gi)),
                  pl.BlockSpec((s, hb * dv), lambda gi, i: (0, gi))],
        out_specs=[pl.BlockSpec((bq, hb * R * dv), lambda gi, i: (i, gi)),
                   pl.BlockSpec((1, bq, hb * R), lambda gi, i: (gi, i, 0))],
        out_shape=[jax.ShapeDtypeStruct((s, g * R * dv), BF16), jax.ShapeDtypeStruct((ng, s, hb * R), F32)],
        scratch_shapes=[pltpu.VMEM((s, 2 * hb * dv), BF16)],
        compiler_params=_cparams(("parallel", "arbitrary")),
    )(q, k, v)


def _flash_bwd(q, k, v, o, do, lse, *, R, dk, dv, hb, bq, bk, name):
    s = q.shape[0]
    g = k.shape[1] // dk
    ng = g // hb
    bq, bk = min(bq, s), min(bk, s)
    nq, nkb = s // bq, s // bk
    rb = R * bq

    def body(q_ref, k_ref, v_ref, o_ref, do_ref, lse_ref, dq_ref, dkt_ref, dvt_ref, va_ref):
        @pl.when(pl.program_id(1) == 0)
        def _():
            dkt_ref[...] = jnp.zeros(dkt_ref.shape, F32)
            dvt_ref[...] = jnp.zeros(dvt_ref.shape, F32)

        _fill_v_ones(v_ref, va_ref, hb, dv)
        lane = lax.broadcasted_iota(jnp.int32, (rb, dv), 1)
        head_sets = [[h * R + r for r in range(R)] for h in range(hb)]
        q_t = jnp.transpose(q_ref[...].astype(F32)).astype(BF16)
        do_t = jnp.transpose(do_ref[...].astype(F32)).astype(BF16)

        def stack_t(t, heads, d):
            parts = [t[hd * d:(hd + 1) * d] for hd in heads]
            return parts[0] if len(parts) == 1 else jnp.concatenate(parts, axis=1)

        qss, qts, dots, dosas, lcols = [], [], [], [], []
        for heads in head_sets:
            dos = _stack_heads(do_ref, heads, dv, BF16)
            delta = jnp.sum(dos.astype(F32) * _stack_heads(o_ref, heads, dv, F32), axis=1, keepdims=True)
            hi = delta.astype(BF16).astype(F32)
            lo = delta - hi
            qss.append(_stack_heads(q_ref, heads, dk))
            qts.append(stack_t(q_t, heads, dk))
            dots.append(stack_t(do_t, heads, dv))
            dosas.append(jnp.concatenate(
                [dos, jnp.where(lane == 0, -hi, jnp.where(lane == 1, -lo, 0.0)).astype(BF16)], axis=1))
            cols = [lse_ref[0, :, hd:hd + 1] for hd in heads]
            lcols.append(cols[0] if R == 1 else jnp.concatenate(cols, axis=0))

        def step(j, dqs):
            r0 = pl.multiple_of(j * bk, bk)
            hs = range(hb)
            kjs = [k_ref[pl.ds(r0, bk), h * dk:(h + 1) * dk] for h in hs]
            vas = [va_ref[pl.ds(r0, bk), 2 * h * dv:2 * (h + 1) * dv] for h in hs]
            ss = [lax.dot_general(qss[h], kjs[h], _DIMS["nt"], preferred_element_type=F32) for h in hs]
            dps = [lax.dot_general(dosas[h], vas[h], _DIMS["nt"], preferred_element_type=F32) for h in hs]
            ps = [jnp.exp2(ss[h] - lcols[h]) for h in hs]
            pbs = [ps[h].astype(BF16) for h in hs]
            dss = [(ps[h] * dps[h]).astype(BF16) for h in hs]
            for h in hs:
                dvt_ref[0, j, h * dv:(h + 1) * dv, :] += jnp.dot(dots[h], pbs[h], preferred_element_type=F32)
            for h in hs:
                dkt_ref[0, j, h * dk:(h + 1) * dk, :] += jnp.dot(qts[h], dss[h], preferred_element_type=F32)
            return tuple(dqs[h] + jnp.dot(dss[h], kjs[h], preferred_element_type=F32) for h in hs)

        dqs = lax.fori_loop(0, nkb, step, tuple(jnp.zeros((rb, dk), F32) for _ in range(hb)))
        for h, heads in enumerate(head_sets):
            for r, hd in enumerate(heads):
                dq_ref[:, hd * dk:(hd + 1) * dk] = dqs[h][r * bq:(r + 1) * bq]

    qspec = pl.BlockSpec((bq, hb * R * dk), lambda gi, i: (i, gi))
    ospec = pl.BlockSpec((bq, hb * R * dv), lambda gi, i: (i, gi))
    kspec = pl.BlockSpec((s, hb * dk), lambda gi, i: (0, gi))
    vspec = pl.BlockSpec((s, hb * dv), lambda gi, i: (0, gi))
    dq, dkt, dvt = pl.pallas_call(
        body, name=name, grid=(ng, nq),
        in_specs=[qspec, kspec, vspec, ospec, ospec, pl.BlockSpec((1, bq, hb * R), lambda gi, i: (gi, i, 0))],
        out_specs=[qspec, pl.BlockSpec((1, nkb, hb * dk, bk), lambda gi, i: (gi, 0, 0, 0)),
                   pl.BlockSpec((1, nkb, hb * dv, bk), lambda gi, i: (gi, 0, 0, 0))],
        out_shape=[jax.ShapeDtypeStruct((s, g * R * dk), F32), jax.ShapeDtypeStruct((ng, nkb, hb * dk, bk), F32),
                   jax.ShapeDtypeStruct((ng, nkb, hb * dv, bk), F32)],
        scratch_shapes=[pltpu.VMEM((s, 2 * hb * dv), BF16)],
        compiler_params=_cparams(("parallel", "arbitrary")),
    )(q, k, v, o, do, lse)
    return dq, _keys_first(dkt, name + "_dk"), _keys_first(dvt, name + "_dv")


def _keys_first(t, name):
    ng, nkb, f, bk = t.shape

    def body(t_ref, o_ref):
        o_ref[...] = jnp.transpose(t_ref[0, 0])

    return pl.pallas_call(
        body, name=name, grid=(ng, nkb),
        in_specs=[pl.BlockSpec((1, 1, f, bk), lambda gi, j: (gi, j, 0, 0))],
        out_specs=pl.BlockSpec((bk, f), lambda gi, j: (j, gi)),
        out_shape=jax.ShapeDtypeStruct((nkb * bk, ng * f), t.dtype),
        compiler_params=_cparams(("parallel", "parallel")),
    )(t)


DIL_T = 1024
DIL_P = DIL_HALF
DIL_NCOL = IN_B // DIL_W


DIL_BATCH = 4


def _alibi_slope(head):
    return float(2.0 ** (-8.0 * (head + 1) / DIL_HEADS))


def _slot(sl_i):
    return slice(sl_i * DIL_HEAD_DIM, (sl_i + 1) * DIL_HEAD_DIM)


def _halo_specs(d, col, s, t):
    h = DIL_P * d
    per, last = t // h, s // h - 1
    return [pl.BlockSpec((h, DIL_W), lambda c: (jnp.maximum(c * per - 1, 0), col)),
            pl.BlockSpec((t, DIL_W), lambda c: (c, col)),
            pl.BlockSpec((h, DIL_W), lambda c: (jnp.minimum((c + 1) * per, last), col))]


def _staging(rows):
    return tuple(pltpu.VMEM((rows, LANES), F32) for _ in range(DIL_W // LANES))


def _stage(buf, refs):
    off = 0
    for r in refs:
        val = r[...].astype(F32)
        for j in range(DIL_W // LANES):
            buf[j][off:off + r.shape[0], :] = val[:, j * LANES:(j + 1) * LANES]
        off += r.shape[0]


def _unstage(buf, ref):
    ref[...] = jnp.concatenate([half[...] for half in buf], axis=1).astype(ref.dtype)


def _sub_tiles(d, t):
    return [(b * DIL_P * d + r, b * DIL_P) for b in range(t // (DIL_P * d)) for r in range(d)]


def _rows(start, size, d):
    return pl.ds(start, size, stride=d) if d > 1 else pl.ds(start, size)


def _strided(buf, start, size, d):
    return jnp.concatenate([half[_rows(start, size, d), :] for half in buf], axis=1)


def _put_strided(buf, start, d, val):
    for j in range(DIL_W // LANES):
        buf[j][_rows(start, val.shape[0], d), :] = val[:, j * LANES:(j + 1) * LANES]


def _band(u0, length, d, queries_wide):
    if queries_wide:
        shape = (3 * DIL_P, DIL_P)
        wide = u0 - DIL_P + lax.broadcasted_iota(jnp.int32, shape, 0)
        narrow = u0 + lax.broadcasted_iota(jnp.int32, shape, 1)
    else:
        shape = (DIL_P, 3 * DIL_P)
        narrow = u0 + lax.broadcasted_iota(jnp.int32, shape, 0)
        wide = u0 - DIL_P + lax.broadcasted_iota(jnp.int32, shape, 1)
    rel = jnp.abs(wide - narrow)
    valid = (rel <= DIL_HALF) & (wide >= 0) & (wide < length)
    return valid, rel.astype(F32) * float(d)


def _dil_fwd(zb, grp, name):
    s = zb.shape[0]
    d = DIL_PAIRS[grp][1]
    t = min(DIL_T, s)
    h = DIL_P * d
    scale = DIL_HEAD_DIM ** -0.5

    def body(q_ref, kp, kc, kn, vp, vc, vn, o_ref, lse_ref, qbuf, kbuf, vbuf, obuf, lbuf):
        _stage(qbuf, (q_ref,))
        _stage(kbuf, (kp, kc, kn))
        _stage(vbuf, (vp, vc, vn))
        u_step = pl.program_id(0) * (t // d)
        tiles = _sub_tiles(d, t)
        for g0 in range(0, len(tiles), DIL_BATCH):
            batch = tiles[g0:g0 + DIL_BATCH]
            masks = [_band(u_step + u, s // d, d, False) for _, u in batch]
            qs = [_strided(qbuf, row, DIL_P, d).astype(BF16) for row, _ in batch]
            ks = [_strided(kbuf, row, 3 * DIL_P, d).astype(BF16) for row, _ in batch]
            vs = [_strided(vbuf, row, 3 * DIL_P, d).astype(BF16) for row, _ in batch]
            chains = [(i, sl_i) for i in range(len(batch)) for sl_i in range(DIL_SLOTS)]
            scs = [lax.dot_general(qs[i][:, _slot(sl_i)], ks[i][:, _slot(sl_i)], _DIMS["nt"],
                                   preferred_element_type=F32) for i, sl_i in chains]
            scs = [jnp.where(masks[i][0], sc * scale - _alibi_slope(grp * DIL_SLOTS + sl_i) * masks[i][1], NEG_INF)
                   for (i, sl_i), sc in zip(chains, scs)]
            ms = [jnp.max(sc, axis=1, keepdims=True) for sc in scs]
            es = [jnp.exp(sc - m) for sc, m in zip(scs, ms)]
            dens = [jnp.sum(e, axis=1, keepdims=True) for e in es]
            outs = [jnp.dot((e / den).astype(BF16), vs[i][:, _slot(sl_i)], preferred_element_type=F32)
                    for (i, sl_i), e, den in zip(chains, es, dens)]
            lses = [jnp.broadcast_to(m + jnp.log(den), (DIL_P, DIL_HEAD_DIM)) for m, den in zip(ms, dens)]
            for i, (row, _) in enumerate(batch):
                pick = slice(i * DIL_SLOTS, (i + 1) * DIL_SLOTS)
                _put_strided(obuf, row, d, jnp.concatenate(outs[pick], axis=1))
                _put_strided(lbuf, row, d, jnp.concatenate(lses[pick], axis=1))
        _unstage(obuf, o_ref)
        _unstage(lbuf, lse_ref)

    own = pl.BlockSpec((t, DIL_W), lambda c: (c, 0))
    return pl.pallas_call(
        body, name=name, grid=(s // t,),
        in_specs=[pl.BlockSpec((t, DIL_W), lambda c: (c, grp))] + _halo_specs(d, 3 + grp, s, t)
        + _halo_specs(d, 6 + grp, s, t),
        out_specs=[own, own], out_shape=[jax.ShapeDtypeStruct((s, DIL_W), F32)] * 2,
        scratch_shapes=[_staging(t), _staging(t + 2 * h), _staging(t + 2 * h), _staging(t), _staging(t)],
        compiler_params=_cparams(("parallel",)),
    )(zb, zb, zb, zb, zb, zb, zb)


def _dil_bwd(zb, do, lse, dl, grp, name):
    s = zb.shape[0]
    d = DIL_PAIRS[grp][1]
    t = min(DIL_T, s)
    h = DIL_P * d
    scale = DIL_HEAD_DIM ** -0.5

    def chain_grads(qs, ks, vs, dos, lses, dls, masks):
        chains = [(i, sl_i) for i in range(len(qs)) for sl_i in range(DIL_SLOTS)]
        scs = [lax.dot_general(qs[i][:, _slot(sl_i)], ks[i][:, _slot(sl_i)], _DIMS["nt"],
                               preferred_element_type=F32) for i, sl_i in chains]
        dps = [lax.dot_general(dos[i][:, _slot(sl_i)], vs[i][:, _slot(sl_i)], _DIMS["nt"],
                               preferred_element_type=F32) for i, sl_i in chains]
        ps = [jnp.exp(jnp.where(masks[i][0], sc * scale - _alibi_slope(grp * DIL_SLOTS + sl_i) * masks[i][1],
                                NEG_INF) - lses[i][:, sl_i * DIL_HEAD_DIM:sl_i * DIL_HEAD_DIM + 1])
              for (i, sl_i), sc in zip(chains, scs)]
        dss = [(p * (dp - dls[i][:, sl_i * DIL_HEAD_DIM:sl_i * DIL_HEAD_DIM + 1]) * scale).astype(BF16)
               for (i, sl_i), p, dp in zip(chains, ps, dps)]
        return chains, ps, dss

    def dq_body(q_ref, kp, kc, kn, vp, vc, vn, do_ref, lse_ref, dl_ref, dq_ref, qbuf, kbuf, vbuf, dobuf, lsebuf,
                dlbuf, obuf):
        _stage(qbuf, (q_ref,))
        _stage(dobuf, (do_ref,))
        _stage(lsebuf, (lse_ref,))
        _stage(dlbuf, (dl_ref,))
        _stage(kbuf, (kp, kc, kn))
        _stage(vbuf, (vp, vc, vn))
        u_step = pl.program_id(0) * (t // d)
        tiles = _sub_tiles(d, t)
        for g0 in range(0, len(tiles), DIL_BATCH):
            batch = tiles[g0:g0 + DIL_BATCH]
            masks = [_band(u_step + u, s // d, d, False) for _, u in batch]
            narrow = [[_strided(b, row, DIL_P, d) for row, _ in batch] for b in (qbuf, dobuf, lsebuf, dlbuf)]
            ks = [_strided(kbuf, row, 3 * DIL_P, d).astype(BF16) for row, _ in batch]
            vs = [_strided(vbuf, row, 3 * DIL_P, d).astype(BF16) for row, _ in batch]
            chains, _, dss = chain_grads([a.astype(BF16) for a in narrow[0]], ks, vs,
                                         [a.astype(BF16) for a in narrow[1]], narrow[2], narrow[3], masks)
            outs = [jnp.dot(ds, ks[i][:, _slot(sl_i)], preferred_element_type=F32)
                    for (i, sl_i), ds in zip(chains, dss)]
            for i, (row, _) in enumerate(batch):
                _put_strided(obuf, row, d, jnp.concatenate(outs[i * DIL_SLOTS:(i + 1) * DIL_SLOTS], axis=1))
        _unstage(obuf, dq_ref)

    def dkv_body(k_ref, v_ref, qp, qc, qn, dop, doc, don, lp, lc, ln, dlp, dlc, dln, dk_ref, dv_ref,
                 kbuf, vbuf, qbuf, dobuf, lsebuf, dlbuf, dkbuf, dvbuf):
        _stage(kbuf, (k_ref,))
        _stage(vbuf, (v_ref,))
        _stage(qbuf, (qp, qc, qn))
        _stage(dobuf, (dop, doc, don))
        _stage(lsebuf, (lp, lc, ln))
        _stage(dlbuf, (dlp, dlc, dln))
        u_step = pl.program_id(0) * (t // d)
        tiles = _sub_tiles(d, t)
        for g0 in range(0, len(tiles), DIL_BATCH):
            batch = tiles[g0:g0 + DIL_BATCH]
            masks = [_band(u_step + u, s // d, d, True) for _, u in batch]
            ks = [_strided(kbuf, row, DIL_P, d).astype(BF16) for row, _ in batch]
            vs = [_strided(vbuf, row, DIL_P, d).astype(BF16) for row, _ in batch]
            wide_ = [[_strided(b, row, 3 * DIL_P, d) for row, _ in batch] for b in (qbuf, dobuf, lsebuf, dlbuf)]
            qs, dos = [a.astype(BF16) for a in wide_[0]], [a.astype(BF16) for a in wide_[1]]
            chains, ps, dss = chain_grads(qs, ks, vs, dos, wide_[2], wide_[3], masks)
            dvs = [lax.dot_general(p.astype(BF16), dos[i][:, _slot(sl_i)], _DIMS["tn"], preferred_element_type=F32)
                   for (i, sl_i), p in zip(chains, ps)]
            dks = [lax.dot_general(ds, qs[i][:, _slot(sl_i)], _DIMS["tn"], preferred_element_type=F32)
                   for (i, sl_i), ds in zip(chains, dss)]
            for i, (row, _) in enumerate(batch):
                pick = slice(i * DIL_SLOTS, (i + 1) * DIL_SLOTS)
                _put_strided(dkbuf, row, d, jnp.concatenate(dks[pick], axis=1))
                _put_strided(dvbuf, row, d, jnp.concatenate(dvs[pick], axis=1))
        _unstage(dkbuf, dk_ref)
        _unstage(dvbuf, dv_ref)

    def zcur(col):
        return pl.BlockSpec((t, DIL_W), lambda c: (c, col))

    own = pl.BlockSpec((t, DIL_W), lambda c: (c, 0))
    out = jax.ShapeDtypeStruct((s, DIL_W), F32)
    tile, wide = _staging(t), _staging(t + 2 * h)
    dq = pl.pallas_call(
        dq_body, name=name + "_dq", grid=(s // t,),
        in_specs=[zcur(grp)] + _halo_specs(d, 3 + grp, s, t) + _halo_specs(d, 6 + grp, s, t) + [own, own, own],
        out_specs=own, out_shape=out, scratch_shapes=[tile, wide, wide, tile, tile, tile, tile],
        compiler_params=_cparams(("parallel",)),
    )(zb, zb, zb, zb, zb, zb, zb, do, lse, dl)
    own3 = _halo_specs(d, 0, s, t)
    dk, dv = pl.pallas_call(
        dkv_body, name=name + "_dkv", grid=(s // t,),
        in_specs=[zcur(3 + grp), zcur(6 + grp)] + _halo_specs(d, grp, s, t) + own3 + own3 + own3,
        out_specs=[own, own], out_shape=[out, out],
        scratch_shapes=[tile, tile, wide, wide, wide, wide, tile, tile],
        compiler_params=_cparams(("parallel",)),
    )(zb, zb, zb, zb, zb, do, do, do, lse, lse, lse, dl, dl, dl)
    return dq, dk, dv


def _dil_combine(os_, ls_, name):
    def fn(o0, o1, o2, l0, l1, l2):
        m = jnp.maximum(jnp.maximum(l0, l1), l2)
        e0, e1, e2 = jnp.exp(l0 - m), jnp.exp(l1 - m), jnp.exp(l2 - m)
        den = e0 + e1 + e2
        comb = (e0 / den) * o0 + (e1 / den) * o1 + (e2 / den) * o2
        return comb, m + jnp.log(den)
    return _rows_call(fn, list(os_) + list(ls_), [], [(DIL_W, BF16), (DIL_W, F32)], name=name)


def _dil_combine_bwd(dcomb, os_, ls_, lt, seg64, name):
    def fn(dc, o0, o1, o2, l0, l1, l2, ltv, seg):
        w = [jnp.exp(l - ltv) for l in (l0, l1, l2)]
        comb = w[0] * o0 + w[1] * o1 + w[2] * o2
        t = _chunkdot(dc * comb, seg)
        return [wg * dc for wg in w] + [wg * t for wg in w]
    return _rows_call(fn, [dcomb] + list(os_) + list(ls_) + [lt], [seg64],
                      [(DIL_W, BF16)] * 3 + [(DIL_W, F32)] * 3, name=name)


def _mla_prep(za, gq, gkv, cs, swap, name):
    def fn(z, csv, gqv, gkvv, sw):
        return (_rms(z[:, :MLA_Q_RANK], gqv), _rms(z[:, MLA_Q_RANK:640], gkvv), _rope(z[:, 640:], csv, sw))
    return _rows_call(fn, [za, cs], [gq, gkv, swap], [(MLA_Q_RANK, BF16), (MLA_KV_RANK, BF16), (LANES, F32)],
                      name=name)


def _mla_prep_bwd(za, cs, dcq, dckv, dkr, gq, gkv, swap, name):
    def fn(z, csv, dcqv, dckvv, dkrv, gqv, gkvv, sw):
        d1, dg1 = _rms_bwd_math(z[:, :MLA_Q_RANK], gqv, dcqv)
        d2, dg2 = _rms_bwd_math(z[:, MLA_Q_RANK:640], gkvv, dckvv)
        d3 = _rope_t(dkrv, csv, sw)
        return jnp.concatenate([d1, d2, d3], axis=1), dg1, dg2
    return _rows_call(fn, [za, cs, dcq, dckv, dkr], [gq, gkv, swap], [(IN_A_PAD, BF16)],
                      [(1, MLA_Q_RANK), (1, MLA_KV_RANK)], name=name)


def _mla_qk(q_raw, k_pad, krr, cs, swap, name):
    w = MLA_HEADS * MLA_DK

    def fn(qv, kv, krv, csv, sw):
        return _rope(qv, csv, sw) * (MLA_SCALE * LOG2E), kv + _lanes(krv, w)
    return _rows_call(fn, [q_raw, k_pad, krr, cs], [swap], [(w, BF16), (w, BF16)], name=name)


def _mla_qk_bwd(dqh, dkh, cs, swap, name):
    w = MLA_HEADS * MLA_DK

    def fn(dq, dk, csv, sw):
        dk = dk * LN2
        acc = dk[:, :LANES]
        for h in range(1, MLA_HEADS):
            acc = acc + dk[:, h * LANES:(h + 1) * LANES]
        lane = lax.broadcasted_iota(jnp.int32, acc.shape, 1)
        acc = jnp.where((lane >= MLA_NOPE) & (lane < MLA_NOPE + MLA_ROPE), acc, 0.0)
        return _rope_t(dq * MLA_SCALE, csv, sw), dk, acc
    return _rows_call(fn, [dqh, dkh, cs], [swap], [(w, BF16), (w, BF16), (LANES, F32)], name=name)


def _head_norm(t, g2, seg):
    r = lax.rsqrt(_chunkdot(t * t, seg) * (1.0 / GQA_HEAD_DIM) + NORM_EPS)
    return t * r * _lanes(g2, t.shape[1]), r


def _head_norm_bwd(t, g2, seg, dn):
    w = t.shape[1]
    r = lax.rsqrt(_chunkdot(t * t, seg) * (1.0 / GQA_HEAD_DIM) + NORM_EPS)
    u = dn * _lanes(g2, w)
    dt = r * u - t * (r * r * r) * (_chunkdot(u * t, seg) * (1.0 / GQA_HEAD_DIM))
    dgw = jnp.sum(dn * t * r, axis=0, keepdims=True)
    dg = dgw[:, :LANES]
    for c in range(LANES, w, LANES):
        dg = dg + dgw[:, c:c + LANES]
    return dt, dg


def _gqa_prep(q_raw, kv_raw, cs, gq2, gk2, seg, swap, name):
    kw = GQA_KV_HEADS * GQA_HEAD_DIM

    def fn(qv, kvv, csv, gqv, gkv, sg, sw):
        qn, _ = _head_norm(qv, gqv, sg)
        kn, _ = _head_norm(kvv[:, :kw], gkv, sg)
        return _rope(qn, csv, sw) * (GQA_SCALE * LOG2E), _rope(kn, csv, sw), kvv[:, kw:]
    return _rows_call(fn, [q_raw, kv_raw, cs], [gq2, gk2, seg, swap],
                      [(GQA_HEADS * GQA_HEAD_DIM, BF16), (kw, BF16), (kw, BF16)], name=name)


def _gqa_prep_bwd(q_raw, kv_raw, cs, dqh, dkh, dv, gq2, gk2, seg, swap, name):
    kw = GQA_KV_HEADS * GQA_HEAD_DIM

    def fn(qv, kvv, csv, dq, dk, dvv, gqv, gkv, sg, sw):
        dqr, dgq = _head_norm_bwd(qv, gqv, sg, _rope_t(dq * GQA_SCALE, csv, sw))
        dkr, dgk = _head_norm_bwd(kvv[:, :kw], gkv, sg, _rope_t(dk * LN2, csv, sw))
        return dqr, jnp.concatenate([dkr, dvv], axis=1), dgq, dgk
    return _rows_call(fn, [q_raw, kv_raw, cs, dqh, dkh, dv], [gq2, gk2, seg, swap],
                      [(GQA_HEADS * GQA_HEAD_DIM, BF16), (2 * kw, BF16)], [(1, LANES), (1, LANES)], name=name)


def _loss_head(x, target, g, name):
    dm = x.shape[1]

    def fn(xv, tv, gv):
        err = _rms(xv, gv) - tv
        loss = 0.5 * jnp.sum(err * err) / dm
        dx, dg = _rms_bwd_math(xv, gv, err * (1.0 / dm))
        return dx, jnp.zeros((1, LANES), F32) + loss, dg
    return _rows_call(fn, [x, target], [g], [(dm, F32)], [(1, LANES), (1, dm)], name=name)


def _adamw(w, g, m, v, name):
    def fn(wv, gv, mv, vv):
        m2 = ADAM_B1 * mv + (1.0 - ADAM_B1) * gv
        v2 = ADAM_B2 * vv + (1.0 - ADAM_B2) * (gv * gv)
        m_hat = m2 / (1.0 - ADAM_B1 ** ADAM_STEP)
        v_hat = v2 / (1.0 - ADAM_B2 ** ADAM_STEP)
        return -ADAM_LR * (m_hat / (jnp.sqrt(v_hat) + ADAM_EPS) + ADAM_WD * wv), m2, v2
    c = w.shape[1]
    return _rows_call(fn, [w, g, m, v], [], [(c, F32)] * 3, bs=_pick(w.shape[0], 256, 8), name=name)


HBM_SPEC = pl.BlockSpec(memory_space=pltpu.HBM)
VMEM_SPEC = pl.BlockSpec(memory_space=pltpu.VMEM)


def _position():
    return lax.axis_index("x"), lax.axis_index("y"), lax.axis_index("c")


def _other_chips(x, y):
    return [(1 - x, y), (x, 1 - y), (1 - x, 1 - y)]


HALF_W = PACK_W // 2


def _cols(c):
    return pl.ds(pl.multiple_of(c * HALF_W, HALF_W), HALF_W)


def _all_gather_weights(packed):
    rows = packed.shape[0]

    def body(p_ref, g_ref, send_sems, recv_sems, local_sem):
        x, y, c = _position()
        chips = _other_chips(x, y)

        def half(chip, hc):
            return g_ref.at[2 * chip[0] + chip[1], :, _cols(hc)]

        def copy(j, src, dst, to):
            return pltpu.make_async_remote_copy(src_ref=src, dst_ref=dst, send_sem=send_sems.at[j],
                                                recv_sem=recv_sems.at[j], device_id=to, device_id_type=MESH)

        mine = pltpu.make_async_copy(p_ref, g_ref.at[2 * x + y], local_sem)
        mine.start()
        first = [copy(j, p_ref.at[:, _cols(c)], half((x, y), c), (*chip, c)) for j, chip in enumerate(chips)]
        for cp in first:
            cp.start()
        passed = [copy(3 + j, half(chip, c), half(chip, c), (x, y, 1 - c)) for j, chip in enumerate(chips)]
        for j, chip in enumerate(chips):
            copy(j, half(chip, c), half(chip, c), (x, y, c)).wait_recv()
            passed[j].start()
        for j, chip in enumerate(chips):
            copy(3 + j, half(chip, 1 - c), half(chip, 1 - c), (x, y, c)).wait_recv()
        for cp in first + passed:
            cp.wait_send()
        mine.wait()

    return pl.pallas_call(
        body, name="all_gather_weights", in_specs=[HBM_SPEC], out_specs=HBM_SPEC,
        out_shape=jax.ShapeDtypeStruct((4, rows, packed.shape[1]), packed.dtype),
        scratch_shapes=[pltpu.SemaphoreType.DMA((6,)), pltpu.SemaphoreType.DMA((6,)), pltpu.SemaphoreType.DMA],
    )(packed)


SEM_SPEC = pl.BlockSpec(memory_space=pltpu.SEMAPHORE)
ANY_SPEC = pl.BlockSpec(memory_space=pl.ANY)
DATAFLOW = pltpu.SideEffectType.DATAFLOW_SIDE_EFFECTING


def _hbm(a):
    return pltpu.with_memory_space_constraint(a, pltpu.HBM)


def _gather_start(packed, tag):
    rows = packed.shape[0]

    def body(p_ref, g_ref, send_sems, recv_sems, p_thru, g_thru, token):
        x, y, c = _position()
        for j, chip in enumerate(_other_chips(x, y)):
            for s in range(2):
                pltpu.make_async_remote_copy(
                    src_ref=p_ref.at[:, _cols(c)], dst_ref=g_ref.at[2 * x + y, :, _cols(c)],
                    send_sem=send_sems.at[2 * j + s], recv_sem=recv_sems.at[2 * j + s],
                    device_id=(*chip, 1 - c if s else c), device_id_type=MESH).start()
        token[...] = jnp.zeros_like(token)

    return pl.pallas_call(
        body, name=tag + "_start",
        out_shape=(pltpu.SemaphoreType.DMA((6,)), pltpu.SemaphoreType.DMA((6,)), pltpu.HBM(packed.shape, packed.dtype),
                   pltpu.HBM((4, rows, PACK_W), packed.dtype), jax.ShapeDtypeStruct((8, LANES), F32)),
        in_specs=(HBM_SPEC, HBM_SPEC), out_specs=(SEM_SPEC, SEM_SPEC, HBM_SPEC, HBM_SPEC, VMEM_SPEC),
        input_output_aliases={0: 2, 1: 3},
        compiler_params=pltpu.CompilerParams(has_side_effects=DATAFLOW),
    )(_hbm(packed), _hbm(lax.empty((4, rows, PACK_W), packed.dtype)))


def _gather_wait(send_sems, recv_sems, p_thru, g_thru, after, tag):
    def body(p_ref, g_ref, send_sems, recv_sems, after_ref, p_dead, got_ref):
        x, y, c = _position()
        for j, chip in enumerate(_other_chips(x, y)):
            for s in range(2):
                cp = pltpu.make_async_remote_copy(
                    src_ref=p_ref.at[:, _cols(c)], dst_ref=g_ref.at[2 * chip[0] + chip[1], :, _cols(1 - c if s else c)],
                    send_sem=send_sems.at[2 * j + s], recv_sem=recv_sems.at[2 * j + s],
                    device_id=(x, y, c), device_id_type=MESH)
                cp.wait_send()
                cp.wait_recv()

    return pl.pallas_call(
        body, name=tag + "_wait",
        out_shape=(pltpu.HBM(p_thru.shape, p_thru.dtype), pltpu.HBM(g_thru.shape, g_thru.dtype)),
        in_specs=(HBM_SPEC, HBM_SPEC, SEM_SPEC, SEM_SPEC, ANY_SPEC), out_specs=(HBM_SPEC, HBM_SPEC),
        input_output_aliases={0: 0, 1: 1},
        compiler_params=pltpu.CompilerParams(has_side_effects=DATAFLOW),
    )(p_thru, g_thru, send_sems, recv_sems, after)[1]


def _sibling_swap_halves(grads, tag):
    rows = grads.shape[1]

    def body(g_ref, a_ref, send_sem, recv_sem):
        x, y, c = _position()
        cp = pltpu.make_async_remote_copy(src_ref=g_ref.at[:, :, _cols(1 - c)], dst_ref=a_ref,
                                          send_sem=send_sem, recv_sem=recv_sem, device_id=(x, y, 1 - c),
                                          device_id_type=MESH)
        cp.start()
        cp.wait()

    return pl.pallas_call(
        body, name=tag + "_swap_cores", in_specs=[HBM_SPEC], out_specs=HBM_SPEC,
        out_shape=jax.ShapeDtypeStruct((4, rows, HALF_W), grads.dtype),
        scratch_shapes=[pltpu.SemaphoreType.DMA, pltpu.SemaphoreType.DMA],
    )(grads)


def _rs_block(rows):
    return max(d for d in range(16, 1601, 16) if rows % d == 0)


def _chip_sum(grads, other, c, tag):
    rows = other.shape[1]
    rb = _rs_block(rows)

    def body(c_ref, g_ref, a_ref, o_ref):
        o_ref[...] = (g_ref[...] + a_ref[...]).astype(o_ref.dtype)

    blk = (1, rb, HALF_W)
    return pl.pallas_call(
        body, name=tag + "_chip_sum",
        grid_spec=pltpu.PrefetchScalarGridSpec(
            num_scalar_prefetch=1, grid=(4, rows // rb),
            in_specs=[pl.BlockSpec(blk, lambda k, i, c_ref: (k, i, c_ref[0])),
                      pl.BlockSpec(blk, lambda k, i, c_ref: (k, i, 0))],
            out_specs=pl.BlockSpec(blk, lambda k, i, c_ref: (k, i, 0))),
        out_shape=jax.ShapeDtypeStruct(other.shape, BF16),
        compiler_params=_cparams(("parallel", "parallel")),
    )(jnp.reshape(c, (1,)).astype(jnp.int32), grads, other)


def _chip_copies(t_ref, b_ref, send_sems, recv_sems):
    x, y, c = _position()
    return [pltpu.make_async_remote_copy(src_ref=t_ref.at[2 * chip[0] + chip[1]], dst_ref=b_ref.at[j],
                                         send_sem=send_sems.at[j], recv_sem=recv_sems.at[j],
                                         device_id=(*chip, c), device_id_type=MESH)
            for j, chip in enumerate(_other_chips(x, y))]


def _send_chip_sums(sums, tag):
    def body(t_ref, b_ref, send_sems, recv_sems):
        copies = _chip_copies(t_ref, b_ref, send_sems, recv_sems)
        for cp in copies:
            cp.start()
        for cp in copies:
            cp.wait()

    return pl.pallas_call(
        body, name=tag + "_send_chips", in_specs=[HBM_SPEC], out_specs=HBM_SPEC,
        out_shape=jax.ShapeDtypeStruct((3,) + sums.shape[1:], sums.dtype),
        scratch_shapes=[pltpu.SemaphoreType.DMA((3,)), pltpu.SemaphoreType.DMA((3,))],
    )(sums)


def _send_chip_sums_start(sums, tag):
    land = (3,) + sums.shape[1:]

    def body(t_ref, b_ref, send_sems, recv_sems, t_thru, b_thru, token):
        for cp in _chip_copies(t_ref, b_ref, send_sems, recv_sems):
            cp.start()
        token[...] = jnp.zeros_like(token)

    return pl.pallas_call(
        body, name=tag + "_send_chips_start",
        out_shape=(pltpu.SemaphoreType.DMA((3,)), pltpu.SemaphoreType.DMA((3,)), pltpu.HBM(sums.shape, sums.dtype),
                   pltpu.HBM(land, sums.dtype), jax.ShapeDtypeStruct((8, LANES), F32)),
        in_specs=(HBM_SPEC, HBM_SPEC), out_specs=(SEM_SPEC, SEM_SPEC, HBM_SPEC, HBM_SPEC, VMEM_SPEC),
        input_output_aliases={0: 2, 1: 3},
        compiler_params=pltpu.CompilerParams(has_side_effects=DATAFLOW),
    )(_hbm(sums), _hbm(lax.empty(land, sums.dtype)))


def _send_chip_sums_wait(send_sems, recv_sems, t_thru, b_thru, after, tag):
    def body(t_ref, b_ref, send_sems, recv_sems, after_ref, t_dead, got_ref):
        for cp in _chip_copies(t_ref, b_ref, send_sems, recv_sems):
            cp.wait_send()
            cp.wait_recv()

    return pl.pallas_call(
        body, name=tag + "_send_chips_wait",
        out_shape=(pltpu.HBM(t_thru.shape, t_thru.dtype), pltpu.HBM(b_thru.shape, b_thru.dtype)),
        in_specs=(HBM_SPEC, HBM_SPEC, SEM_SPEC, SEM_SPEC, ANY_SPEC), out_specs=(HBM_SPEC, HBM_SPEC),
        input_output_aliases={0: 0, 1: 1},
        compiler_params=pltpu.CompilerParams(has_side_effects=DATAFLOW),
    )(t_thru, b_thru, send_sems, recv_sems, after)[1]


def _final_sum(grads, other, recv, k, c, tag):
    rows = other.shape[1]
    rb = _rs_block(rows)

    def body(k_ref, c_ref, g_ref, a_ref, b_ref, o_ref):
        own = g_ref[0] + a_ref[0]
        o_ref[...] = ((own + b_ref[0].astype(F32)) + b_ref[1].astype(F32)) + b_ref[2].astype(F32)

    return pl.pallas_call(
        body, name=tag + "_final_sum",
        grid_spec=pltpu.PrefetchScalarGridSpec(
            num_scalar_prefetch=2, grid=(rows // rb,),
            in_specs=[pl.BlockSpec((1, rb, HALF_W), lambda i, k_ref, c_ref: (k_ref[0], i, c_ref[0])),
                      pl.BlockSpec((1, rb, HALF_W), lambda i, k_ref, c_ref: (k_ref[0], i, 0)),
                      pl.BlockSpec((3, rb, HALF_W), lambda i, k_ref, c_ref: (0, i, 0))],
            out_specs=pl.BlockSpec((rb, HALF_W), lambda i, k_ref, c_ref: (i, 0))),
        out_shape=jax.ShapeDtypeStruct((rows, HALF_W), F32),
        compiler_params=_cparams(("parallel",)),
    )(jnp.reshape(k, (1,)).astype(jnp.int32), jnp.reshape(c, (1,)).astype(jnp.int32), grads, other, recv)


def _join_halves(half, core, tag):
    def body(h_ref, o_ref, send_sem, recv_sem):
        x, y, c = _position()
        cp = pltpu.make_async_remote_copy(src_ref=h_ref, dst_ref=o_ref, send_sem=send_sem, recv_sem=recv_sem,
                                          device_id=(x, y, 1 - c), device_id_type=MESH)
        cp.start()
        cp.wait()

    other = pl.pallas_call(
        body, name=tag + "_join_cores", in_specs=[HBM_SPEC], out_specs=HBM_SPEC,
        out_shape=jax.ShapeDtypeStruct(half.shape, half.dtype),
        scratch_shapes=[pltpu.SemaphoreType.DMA, pltpu.SemaphoreType.DMA],
    )(half)
    first = core == 0
    return jnp.concatenate([jnp.where(first, half, other), jnp.where(first, other, half)], axis=1)


def _all_reduce_packet(packet):
    rows = packet.shape[0]

    def body(p_ref, o_ref, buf, send_sems, recv_sems):
        x, y, c = _position()
        me = 4 * x + 2 * y + c
        buf[me] = p_ref[...]

        def flip(v, bit):
            return 1 - v if bit else v

        for p in range(1, 8):
            peer = (flip(x, p & 4), flip(y, p & 2), flip(c, p & 1))
            pltpu.make_async_remote_copy(src_ref=p_ref, dst_ref=buf.at[me], send_sem=send_sems.at[p - 1],
                                         recv_sem=recv_sems.at[p - 1], device_id=peer, device_id_type=MESH).start()
        for p in range(1, 8):
            peer = (flip(x, p & 4), flip(y, p & 2), flip(c, p & 1))
            slot = 4 * peer[0] + 2 * peer[1] + peer[2]
            cp = pltpu.make_async_remote_copy(src_ref=p_ref, dst_ref=buf.at[slot], send_sem=send_sems.at[p - 1],
                                              recv_sem=recv_sems.at[p - 1], device_id=peer, device_id_type=MESH)
            cp.wait_recv()
            cp.wait_send()
        acc = buf[0]
        for dev in range(1, 8):
            acc = acc + buf[dev]
        o_ref[...] = acc

    return pl.pallas_call(
        body, name="all_reduce_packet", in_specs=[VMEM_SPEC], out_specs=VMEM_SPEC,
        out_shape=jax.ShapeDtypeStruct(packet.shape, F32),
        scratch_shapes=[pltpu.VMEM((8, rows, LANES), F32), pltpu.SemaphoreType.DMA((7,)),
                        pltpu.SemaphoreType.DMA((7,))],
    )(packet)


def _stack_range(name, n_stack, pack):
    if name.startswith('gqa'):
        return (0, 0) if pack == 0 else (0, n_stack)
    return (0, 1) if pack == 0 else (1, n_stack)


def _pack_members(pack):
    out = []
    for n, shape, ax in BIG:
        lo, hi = _stack_range(n, shape[0], pack)
        if hi > lo:
            out.append((n, (hi - lo,) + shape[1:], ax, (lo, hi)))
    return out


PACK_ROW_MULTIPLE = 512


def _pad_rows(parts, dtype):
    rows = sum(p.shape[0] for p in parts)
    pad = -rows % PACK_ROW_MULTIPLE
    return jnp.concatenate(parts + ([jnp.zeros((pad, PACK_W), dtype)] if pad else []), axis=0)


def _pack_blocks(blocks, dtype, pack):
    return _pad_rows([blocks[n][lo:hi].astype(dtype).reshape(-1, PACK_W)
                      for n, _, _, (lo, hi) in _pack_members(pack)], dtype)


def _unpack_blocks(packed, pack):
    out, off = {}, 0
    for n, shape, _, _ in _pack_members(pack):
        r = math.prod(shape) // PACK_W
        out[n] = packed[off:off + r].reshape(shape)
        off += r
    return out


def _unpack_gathered(gathered, pack, own=None, chip=None):
    blocks = [gathered[k] if own is None else jnp.where(chip == k, own, gathered[k]) for k in range(4)]
    per_chip = [_unpack_blocks(blocks[k], pack) for k in range(4)]
    return {n: jnp.concatenate([per_chip[k][n] for k in range(4)], axis=ax) for n, _, ax, _ in _pack_members(pack)}


def _pack_full(full, dtype, pack):
    chips = []
    for k in range(4):
        parts = []
        for n, shape, ax, _ in _pack_members(pack):
            blk = lax.slice_in_dim(full[n], k * shape[ax], (k + 1) * shape[ax], axis=ax)
            parts.append(blk.astype(dtype).reshape(-1, PACK_W))
        chips.append(_pad_rows(parts, dtype))
    return jnp.stack(chips, axis=0)


def _pack_small(vals, loss_row):
    rows = [loss_row.reshape(1, LANES)]
    for n, shape in SMALL:
        v = vals.get(n)
        v = jnp.zeros(shape, F32) if v is None else v
        rows.append(v.astype(F32).reshape(-1, LANES))
    packet = jnp.concatenate(rows, axis=0)
    return jnp.pad(packet, ((0, PACKET_ROWS - packet.shape[0]), (0, 0)))


def _unpack_small(packet):
    out, off = {}, 1
    for n, shape in SMALL:
        r = math.prod(shape) // LANES
        out[n] = packet[off:off + r].reshape(shape)
        off += r
    return packet[0, 0], out


_MLA = dict(R=1, dk=MLA_DK, dv=MLA_V, hb=2, bq=512, bk=512)
_MLA_FWD_BQ = 1024
_GQA = dict(R=GQA_HEADS // GQA_KV_HEADS, dk=GQA_HEAD_DIM, dv=GQA_HEAD_DIM, hb=2, bq=256, bk=512)


def _layer_params(layer, full, gains):
    pack = 0 if layer == 0 else 1
    i = layer // 2

    def mat(name):
        lo, _ = _stack_range(name, 4 if name.startswith('ffn') else 2, pack)
        return full[name][(layer if name.startswith('ffn') else i) - lo]

    p = dict(ffn_norm=gains['ffn_norm'][layer][None], ffn_w_in=mat('ffn_w_in'), ffn_w_out=mat('ffn_w_out'))
    if layer % 2 == 0:
        w_in = mat('w_in_ab')
        zeros = jnp.zeros((D_MODEL, 32), w_in.dtype)
        p['w_a'] = jnp.concatenate([w_in[:, :640], zeros, zeros, w_in[:, 640:IN_A], zeros], axis=1)
        p['w_b'] = w_in[:, IN_A:]
        p['w_uq'] = jnp.pad(mat('mla_w_uq'), ((0, 0), (0, 0), (0, MLA_DK - 96))).reshape(MLA_Q_RANK, -1)
        ukv = mat('mla_w_ukv')
        p['w_uk'] = jnp.pad(ukv[:, :, :MLA_NOPE], ((0, 0), (0, 0), (0, MLA_DK - MLA_NOPE))).reshape(MLA_KV_RANK, -1)
        p['w_uv'] = ukv[:, :, MLA_NOPE:].reshape(MLA_KV_RANK, -1)
        p['w_out'] = mat('w_out_ab')
        p['mix_norm'] = gains['mix_norm_ab'][i][None]
        p['q_norm'] = gains['mla_q_norm'][i][None]
        p['kv_norm'] = gains['mla_kv_norm'][i][None]
    else:
        p['w_q'], p['w_kv'], p['w_o'] = mat('gqa_w_q'), mat('gqa_w_kv'), mat('gqa_w_o')
        p['mix_norm'] = gains['mix_norm_c'][i][None]
        p['q_norm'] = jnp.tile(gains['gqa_q_norm'][i][None], (1, 2))
        p['k_norm'] = jnp.tile(gains['gqa_k_norm'][i][None], (1, 2))
    return p


def _even_fwd(x, p, cs, swap, tag):
    xn = _rms_fwd(x, p['mix_norm'], tag + "_norm")
    za = _mm(xn, p['w_a'], name=tag + "_in_a")
    zb = _mm(xn, p['w_b'], out_dtype=BF16, name=tag + "_in_b")
    cq, ckv, krr = _mla_prep(za, p['q_norm'], p['kv_norm'], cs, swap, tag + "_mla_prep")
    q_raw = _mm(cq, p['w_uq'], name=tag + "_uq")
    k_pad = _mm(ckv, p['w_uk'], name=tag + "_uk")
    v = _mm(ckv, p['w_uv'], out_dtype=BF16, name=tag + "_uv")
    qh, kh = _mla_qk(q_raw, k_pad, krr, cs, swap, tag + "_mla_qk")
    o_a, lse_a = _flash_fwd(qh, kh, v, name=tag + "_mla_attn", **dict(_MLA, bq=_MLA_FWD_BQ))
    og, lg = [], []
    for grp in range(DIL_GROUPS):
        o, l = _dil_fwd(zb, grp, f"{tag}_dil{grp}")
        og.append(o)
        lg.append(l)
    o_b, lt = _dil_combine(og, lg, tag + "_dil_merge")
    ocat = jnp.concatenate([o_a, o_b], axis=1)
    x1 = _mm(ocat, p['w_out'], add=x, name=tag + "_out")
    saved = dict(x=x, xn=xn, za=za, zb=zb, cq=cq, ckv=ckv, qh=qh, kh=kh, v=v, lse_a=lse_a, og=og, lg=lg, lt=lt,
                 ocat=ocat)
    return x1, saved


def _even_bwd(dx1, p, sv, cs, swap, seg64, tag):
    docat = _mm(dx1, p['w_out'], mode="nt", name=tag + "_out_dx")
    d_w_out = _mm(sv['ocat'], dx1, mode="tn", name=tag + "_out_dw")
    n_a = MLA_HEADS * MLA_V
    do_a = docat[:, :n_a].astype(BF16)
    res = _dil_combine_bwd(docat[:, n_a:], sv['og'], sv['lg'], sv['lt'], seg64, tag + "_dil_merge_bwd")
    dqs, dks, dvs = [], [], []
    for grp in range(DIL_GROUPS):
        dq, dk, dv = _dil_bwd(sv['zb'], res[grp], sv['lg'][grp], res[3 + grp], grp, f"{tag}_dil{grp}_bwd")
        dqs.append(dq)
        dks.append(dk)
        dvs.append(dv)
    dzb = jnp.concatenate(dqs + dks + dvs, axis=1).astype(BF16)
    dqh, dkh, dv = _flash_bwd(sv['qh'], sv['kh'], sv['v'], sv['ocat'][:, :n_a], do_a, sv['lse_a'],
                              name=tag + "_mla_attn_bwd", **_MLA)
    dq_raw, dkh, dkrr = _mla_qk_bwd(dqh, dkh, cs, swap, tag + "_mla_qk_bwd")
    dcq = _mm(dq_raw, p['w_uq'], mode="nt", name=tag + "_uq_dx")
    d_w_uq = _mm(sv['cq'], dq_raw, mode="tn", name=tag + "_uq_dw")
    dckv = _mm(dkh, p['w_uk'], mode="nt", name=tag + "_uk_dx")
    dckv = _mm(dv, p['w_uv'], mode="nt", add=dckv, name=tag + "_uv_dx")
    d_w_uk = _mm(sv['ckv'], dkh, mode="tn", name=tag + "_uk_dw")
    d_w_uv = _mm(sv['ckv'], dv, mode="tn", name=tag + "_uv_dw")
    dza, d_gq, d_gkv = _mla_prep_bwd(sv['za'], cs, dcq, dckv, dkrr, p['q_norm'], p['kv_norm'], swap,
                                     tag + "_mla_prep_bwd")
    dxn = _mm(dza, p['w_a'], mode="nt", name=tag + "_in_a_dx")
    dxn = _mm(dzb, p['w_b'], mode="nt", add=dxn, name=tag + "_in_b_dx")
    d_w_a = _mm(sv['xn'], dza, mode="tn", name=tag + "_in_a_dw")
    d_w_b = _mm(sv['xn'], dzb, mode="tn", name=tag + "_in_b_dw")
    dx, d_g = _rms_bwd(sv['x'], p['mix_norm'], dxn, dx1, tag + "_norm_bwd")
    d_w_in = jnp.concatenate([d_w_a[:, :640], d_w_a[:, 704:736], d_w_b], axis=1)
    d_uq = d_w_uq.reshape(MLA_Q_RANK, MLA_HEADS, MLA_DK)[:, :, :MLA_NOPE + MLA_ROPE]
    d_ukv = jnp.concatenate([d_w_uk.reshape(MLA_KV_RANK, MLA_HEADS, MLA_DK)[:, :, :MLA_NOPE],
                             d_w_uv.reshape(MLA_KV_RANK, MLA_HEADS, MLA_V)], axis=2)
    grads = dict(w_in_ab=d_w_in, mla_w_uq=d_uq, mla_w_ukv=d_ukv, w_out_ab=d_w_out, mix_norm_ab=d_g[0],
                 mla_q_norm=d_gq[0], mla_kv_norm=d_gkv[0])
    return dx, grads


def _odd_fwd(x, p, cs, seg64, swap, tag):
    xn = _rms_fwd(x, p['mix_norm'], tag + "_norm")
    q_raw = _mm(xn, p['w_q'], name=tag + "_q")
    kv_raw = _mm(xn, p['w_kv'], name=tag + "_kv")
    qh, kh, v = _gqa_prep(q_raw, kv_raw, cs, p['q_norm'], p['k_norm'], seg64, swap, tag + "_gqa_prep")
    o, lse = _flash_fwd(qh, kh, v, name=tag + "_gqa_attn", **_GQA)
    x1 = _mm(o, p['w_o'], add=x, name=tag + "_o")
    return x1, dict(x=x, xn=xn, q_raw=q_raw, kv_raw=kv_raw, qh=qh, kh=kh, v=v, o=o, lse=lse)


def _odd_bwd(dx1, p, sv, cs, seg64, swap, tag):
    do = _mm(dx1, p['w_o'], mode="nt", out_dtype=BF16, name=tag + "_o_dx")
    d_w_o = _mm(sv['o'], dx1, mode="tn", name=tag + "_o_dw")
    dqh, dkh, dv = _flash_bwd(sv['qh'], sv['kh'], sv['v'], sv['o'], do, sv['lse'], name=tag + "_gqa_attn_bwd",
                              **_GQA)
    dq_raw, dkv_raw, d_gq, d_gk = _gqa_prep_bwd(sv['q_raw'], sv['kv_raw'], cs, dqh, dkh, dv, p['q_norm'],
                                                p['k_norm'], seg64, swap, tag + "_gqa_prep_bwd")
    dxn = _mm(dq_raw, p['w_q'], mode="nt", name=tag + "_q_dx")
    dxn = _mm(dkv_raw, p['w_kv'], mode="nt", add=dxn, name=tag + "_kv_dx")
    d_w_q = _mm(sv['xn'], dq_raw, mode="tn", name=tag + "_q_dw")
    d_w_kv = _mm(sv['xn'], dkv_raw, mode="tn", name=tag + "_kv_dw")
    dx, d_g = _rms_bwd(sv['x'], p['mix_norm'], dxn, dx1, tag + "_norm_bwd")
    grads = dict(gqa_w_q=d_w_q, gqa_w_kv=d_w_kv, gqa_w_o=d_w_o, mix_norm_c=d_g[0],
                 gqa_q_norm=d_gq[0, :GQA_HEAD_DIM] + d_gq[0, GQA_HEAD_DIM:],
                 gqa_k_norm=d_gk[0, :GQA_HEAD_DIM] + d_gk[0, GQA_HEAD_DIM:])
    return dx, grads


def _ffn_fwd(x, p, tag):
    xn = _rms_fwd(x, p['ffn_norm'], tag + "_ffn_norm")
    h = _mm(xn, p['ffn_w_in'], out_dtype=BF16, name=tag + "_ffn_in")
    x2 = _mm(h, p['ffn_w_out'], gated=True, add=x, name=tag + "_ffn_out")
    return x2, dict(x=x, xn=xn, h=h)


def _ffn_bwd(dx2, p, sv, tag):
    d_gate, d_up = _mm(dx2, p['ffn_w_out'], mode="nt", gate_up=sv['h'], out_dtype=BF16, name=tag + "_ffn_out_dx")
    d_w_out = _mm(sv['h'], dx2, mode="tn", gated=True, name=tag + "_ffn_out_dw")
    dh = jnp.concatenate([d_gate, d_up], axis=1)
    d_w_in = _mm(sv['xn'], dh, mode="tn", name=tag + "_ffn_in_dw")
    dxn = _mm(dh, p['ffn_w_in'], mode="nt", name=tag + "_ffn_in_dx")
    dx, d_g = _rms_bwd(sv['x'], p['ffn_norm'], dxn, dx2, tag + "_ffn_norm_bwd")
    return dx, d_w_in, d_w_out, d_g[0]


EVEN_MATS = ('w_in_ab', 'mla_w_uq', 'mla_w_ukv', 'w_out_ab')
ODD_MATS = ('gqa_w_q', 'gqa_w_kv', 'gqa_w_o')
FFN_MATS = ('ffn_w_in', 'ffn_w_out')


def _schedule(x, target, gains, full_of_pack, rest_grads_ready):
    s = x.shape[0]
    cs_mla, cs_gqa = _rope_tables(s)
    swap, seg64 = _swap_matrix(), _seg_matrix(GQA_HEAD_DIM)
    params, saved, full = [], [], None
    for layer in range(4):
        tag = f"l{layer}"
        if layer < 2:
            full = full_of_pack(layer, x)
        p = _layer_params(layer, full, gains)
        if layer % 2 == 0:
            x, sv = _even_fwd(x, p, cs_mla, swap, tag)
        else:
            x, sv = _odd_fwd(x, p, cs_gqa, seg64, swap, tag)
        x, sv_f = _ffn_fwd(x, p, tag)
        params.append(p)
        saved.append((sv, sv_f))
    dx, loss_row, d_final = _loss_head(x, target, gains['final_norm'][None], "loss_head")

    per_layer, rest = {}, None
    for layer in reversed(range(4)):
        p, (sv, sv_f), tag = params[layer], saved[layer], f"l{layer}"
        if layer == 0:
            rest = {n: per_layer[2][n][None] for n in EVEN_MATS}
            rest.update({n: jnp.stack([per_layer[1][n], per_layer[3][n]], axis=0) for n in ODD_MATS})
            rest.update({n: jnp.stack([per_layer[l][n] for l in (1, 2, 3)], axis=0) for n in FFN_MATS})
            token = rest_grads_ready(rest)
            if token is not None:
                p = dict(p, ffn_w_out=p['ffn_w_out'] + token[0, 0].astype(p['ffn_w_out'].dtype))
        dx, d_ffn_in, d_ffn_out, d_ffn_g = _ffn_bwd(dx, p, sv_f, tag)
        if layer % 2 == 0:
            dx, g = _even_bwd(dx, p, sv, cs_mla, swap, seg64, tag)
        else:
            dx, g = _odd_bwd(dx, p, sv, cs_gqa, seg64, swap, tag)
        g.update(ffn_w_in=d_ffn_in, ffn_w_out=d_ffn_out, ffn_norm=d_ffn_g)
        per_layer[layer] = g

    first = {n: per_layer[0][n][None] for n in EVEN_MATS + FFN_MATS}
    small = {'final_norm': d_final[0], 'ffn_norm': jnp.stack([per_layer[l]['ffn_norm'] for l in range(4)], axis=0)}
    for n in ('mix_norm_ab', 'mla_q_norm', 'mla_kv_norm'):
        small[n] = jnp.stack([per_layer[0][n], per_layer[2][n]], axis=0)
    for n in ('mix_norm_c', 'gqa_q_norm', 'gqa_k_norm'):
        small[n] = jnp.stack([per_layer[1][n], per_layer[3][n]], axis=0)
    return loss_row, dx, first, rest, small


def _core_sums(grads, pack, core, tag):
    packed = _pack_full(grads, F32, pack)
    other = _sibling_swap_halves(packed, tag)
    return packed, other, _chip_sum(packed, other, core, tag)


def _finish_reduce_scatter(packed, other, recv, pack, chip, core, tag):
    return _unpack_blocks(_join_halves(_final_sum(packed, other, recv, chip, core, tag), core, tag), pack)


def _step(x, target, w, m, v):
    big_names = [n for n, _, _ in BIG]
    chip = 2 * lax.axis_index("x") + lax.axis_index("y")
    core = lax.axis_index("c")

    gains = {n: w[n] for n, _ in SMALL if n != 'mix_norm_c'}
    c_cols = w['mix_norm_c'].shape[1]
    own_c = lax.dynamic_update_slice(jnp.zeros((2, 4 * c_cols), F32), w['mix_norm_c'], (0, chip * c_cols))
    gains['mix_norm_c'] = _unpack_small(_all_reduce_packet(_pack_small(
        {'mix_norm_c': own_c * 0.5}, jnp.zeros((LANES,), F32))))[1]['mix_norm_c']

    gathered0 = _all_gather_weights(_pack_blocks(w, BF16, 0))
    packed1, gathered0 = lax.optimization_barrier((_pack_blocks(w, BF16, 1), gathered0))
    ag_send, ag_recv, p_thru, g_thru, ag_token = _gather_start(packed1, "gather_rest")
    gains['mix_norm_ab'] = gains['mix_norm_ab'] + ag_token[0, 0]
    full0 = _unpack_gathered(gathered0, 0)

    def full_of_pack(pack, after):
        if pack == 0:
            return full0
        landed = _gather_wait(ag_send, ag_recv, p_thru, g_thru, after, "gather_rest")
        return _unpack_gathered(landed, 1, own=packed1, chip=chip)

    rs = {}

    def rest_grads_ready(rest):
        rs['packed'], rs['other'], sums = _core_sums(rest, 1, core, "grad_rest")
        rs['send'], rs['recv'], rs['t'], rs['b'], token = _send_chip_sums_start(sums, "grad_rest")
        return token

    loss_row, dx, first, rest, small = _schedule(x[0], target[0], gains, full_of_pack, rest_grads_ready)
    recv1 = _send_chip_sums_wait(rs['send'], rs['recv'], rs['t'], rs['b'], dx, "grad_rest")
    g_rest = _finish_reduce_scatter(rs['packed'], rs['other'], recv1, 1, chip, core, "grad_rest")
    packed0, other0, sums0 = _core_sums(first, 0, core, "grad_first")
    g_first = _finish_reduce_scatter(packed0, other0, _send_chip_sums(sums0, "grad_first"), 0, chip, core,
                                     "grad_first")
    g_blocks = {n: (jnp.concatenate([g_first[n], g_rest[n]], axis=0) if n in g_first else g_rest[n])
                for n in big_names}

    loss, g_small = _unpack_small(_all_reduce_packet(_pack_small(small, loss_row[0])))
    g_small['mix_norm_c'] = lax.dynamic_slice(g_small['mix_norm_c'], (0, chip * c_cols), (2, c_cols))

    out_g, out_d, out_m, out_v = {}, {}, {}, {}
    for n in big_names:
        shape = w[n].shape
        cols = shape[-1]
        d_, m_, v_ = _adamw(w[n].reshape(-1, cols), g_blocks[n].reshape(-1, cols), m[n].reshape(-1, cols),
                            v[n].reshape(-1, cols), "adamw_" + n)
        out_g[n], out_d[n], out_m[n], out_v[n] = g_blocks[n], d_.reshape(shape), m_.reshape(shape), v_.reshape(shape)
    for n, _ in SMALL:
        shape = w[n].shape
        as2d = (lambda t: t.reshape(1, -1)) if len(shape) == 1 else (lambda t: t)
        d_, m_, v_ = _adamw(as2d(w[n]), as2d(g_small[n]), as2d(m[n]), as2d(v[n]), "adamw_" + n)
        out_g[n], out_d[n], out_m[n], out_v[n] = g_small[n], d_.reshape(shape), m_.reshape(shape), v_.reshape(shape)
    return (loss, dx[None], *[out_g[n] for n in WEIGHTS], *[out_d[n] for n in WEIGHTS],
            *[out_m[n] for n in WEIGHTS], *[out_v[n] for n in WEIGHTS])


def kernel(x, mix_norm_ab, w_in_ab, mla_q_norm, mla_kv_norm, mla_w_uq, mla_w_ukv, w_out_ab, mix_norm_c, gqa_w_q, gqa_w_kv, gqa_q_norm, gqa_k_norm, gqa_w_o, ffn_norm, ffn_w_in, ffn_w_out, final_norm, loss_target, m_mix_norm_ab, m_w_in_ab, m_mla_q_norm, m_mla_kv_norm, m_mla_w_uq, m_mla_w_ukv, m_w_out_ab, m_mix_norm_c, m_gqa_w_q, m_gqa_w_kv, m_gqa_q_norm, m_gqa_k_norm, m_gqa_w_o, m_ffn_norm, m_ffn_w_in, m_ffn_w_out, m_final_norm, v_mix_norm_ab, v_w_in_ab, v_mla_q_norm, v_mla_kv_norm, v_mla_w_uq, v_mla_w_ukv, v_w_out_ab, v_mix_norm_c, v_gqa_w_q, v_gqa_w_kv, v_gqa_q_norm, v_gqa_k_norm, v_gqa_w_o, v_ffn_norm, v_ffn_w_in, v_ffn_w_out, v_final_norm):
    w = dict(zip(WEIGHTS, (mix_norm_ab, w_in_ab, mla_q_norm, mla_kv_norm, mla_w_uq, mla_w_ukv, w_out_ab, mix_norm_c,
                           gqa_w_q, gqa_w_kv, gqa_q_norm, gqa_k_norm, gqa_w_o, ffn_norm, ffn_w_in, ffn_w_out,
                           final_norm)))
    m = dict(zip(WEIGHTS, (m_mix_norm_ab, m_w_in_ab, m_mla_q_norm, m_mla_kv_norm, m_mla_w_uq, m_mla_w_ukv,
                           m_w_out_ab, m_mix_norm_c, m_gqa_w_q, m_gqa_w_kv, m_gqa_q_norm, m_gqa_k_norm, m_gqa_w_o,
                           m_ffn_norm, m_ffn_w_in, m_ffn_w_out, m_final_norm)))
    v = dict(zip(WEIGHTS, (v_mix_norm_ab, v_w_in_ab, v_mla_q_norm, v_mla_kv_norm, v_mla_w_uq, v_mla_w_ukv,
                           v_w_out_ab, v_mix_norm_c, v_gqa_w_q, v_gqa_w_kv, v_gqa_q_norm, v_gqa_k_norm, v_gqa_w_o,
                           v_ffn_norm, v_ffn_w_in, v_ffn_w_out, v_final_norm)))
    return _step(x, loss_target, w, m, v)
```

```python
import math

import numpy as np
import jax
import jax.numpy as jnp
from jax import lax
from jax.experimental import pallas as pl
from jax.experimental.pallas import tpu as pltpu

F32 = jnp.float32
BF16 = jnp.bfloat16
MESH = pl.DeviceIdType.MESH

VMEM_LIMIT_BYTES = 56 * 1024 * 1024
LANES = 128

D_MODEL = 1024
NORM_EPS = 1e-6
ROPE_THETA = 10000.0
NEG_INF = -1e30
GRID_W = 64

MLA_HEADS, MLA_Q_RANK, MLA_KV_RANK, MLA_NOPE, MLA_ROPE, MLA_V = 8, 384, 256, 64, 32, 64
MLA_DK = 128
DIL_PAIRS = ((128, 1), (512, 4), (2048, 16))
DIL_HALF, DIL_SLOTS, DIL_GROUPS, DIL_HEAD_DIM = 64, 4, 3, 64
DIL_HEADS = DIL_SLOTS * DIL_GROUPS
DIL_W = DIL_SLOTS * DIL_HEAD_DIM
GQA_HEADS, GQA_KV_HEADS, GQA_HEAD_DIM = 16, 4, 64
FFN_HIDDEN = 2816
IN_A = MLA_Q_RANK + MLA_KV_RANK + MLA_ROPE
IN_A_PAD = 768
IN_B = 3 * DIL_HEADS * DIL_HEAD_DIM

ADAM_LR, ADAM_B1, ADAM_B2, ADAM_EPS, ADAM_WD, ADAM_STEP = 0.001, 0.9, 0.999, 1e-08, 0.01, 10

LOG2E, LN2 = math.log2(math.e), math.log(2.0)
MLA_SCALE = (MLA_NOPE + MLA_ROPE) ** -0.5
GQA_SCALE = GQA_HEAD_DIM ** -0.5

WEIGHTS = ['mix_norm_ab', 'w_in_ab', 'mla_q_norm', 'mla_kv_norm', 'mla_w_uq', 'mla_w_ukv', 'w_out_ab', 'mix_norm_c',
           'gqa_w_q', 'gqa_w_kv', 'gqa_q_norm', 'gqa_k_norm', 'gqa_w_o', 'ffn_norm', 'ffn_w_in', 'ffn_w_out',
           'final_norm']
BIG = (('w_in_ab', (2, 1024, 744), 2), ('mla_w_uq', (2, 96, 8, 96), 1), ('mla_w_ukv', (2, 64, 8, 128), 1),
       ('w_out_ab', (2, 768, 256), 2), ('gqa_w_q', (2, 256, 1024), 1), ('gqa_w_kv', (2, 256, 512), 1),
       ('gqa_w_o', (2, 256, 1024), 1), ('ffn_w_in', (4, 1024, 1408), 2), ('ffn_w_out', (4, 704, 1024), 1))
PACK_W = 1024
SMALL = (('mix_norm_ab', (2, 1024)), ('mla_q_norm', (2, 384)), ('mla_kv_norm', (2, 256)), ('gqa_q_norm', (2, 64)),
         ('gqa_k_norm', (2, 64)), ('ffn_norm', (4, 1024)), ('final_norm', (1024,)), ('mix_norm_c', (2, 1024)))
PACKET_ROWS = 88


def _cparams(sem=None):
    return pltpu.CompilerParams(dimension_semantics=sem, vmem_limit_bytes=VMEM_LIMIT_BYTES)


def _pick(n, pref, mult=LANES):
    if n <= pref:
        return n
    for d in range(pref - pref % mult, 0, -mult):
        if n % d == 0:
            return d
    return n


_DIMS = {"nn": (((1,), (0,)), ((), ())), "nt": (((1,), (1,)), ((), ())), "tn": (((0,), (0,)), ((), ()))}


def _sigmoid(x):
    return 0.5 * (1.0 + jnp.tanh(0.5 * x))


def _silu_mul(gate, up):
    gate, up = gate.astype(F32), up.astype(F32)
    return gate * _sigmoid(gate) * up


def _silu_mul_bwd(gate, up, da):
    gate, up = gate.astype(F32), up.astype(F32)
    sig = _sigmoid(gate)
    silu = gate * sig
    return da * up * (sig + silu * (1.0 - sig)), da * silu


def _mm(a, b, *, mode="nn", add=None, out_dtype=F32, gated=False, gate_up=None, name="mm"):
    if mode == "nn":
        (m, k), (k2, n) = a.shape, b.shape
    elif mode == "nt":
        (m, k), (n, k2) = a.shape, b.shape
    else:
        (k, m), (k2, n) = a.shape, b.shape
    if gated:
        k, m = (k // 2, m) if mode == "nn" else (k, m // 2)
    assert k == k2, (a.shape, b.shape, mode)
    if mode == "tn":
        deep = a.dtype == BF16 and b.dtype == BF16
        bm, bn, bk = _pick(m, 1408), _pick(n, 1024), _pick(k, 2048 if deep else 1024, 16)
    else:
        bm, bn, bk = _pick(m, 512, 16), _pick(n, 1408), _pick(k, 2816)
    nk = k // bk
    assert m % bm == 0 and n % bn == 0 and k % bk == 0
    has_add, has_gu = add is not None, gate_up is not None
    assert not (has_add and has_gu) and not (gated and mode == "nt") and not (has_gu and mode != "nt")
    n_in = 2 + int(gated) + int(has_add) + 2 * int(has_gu)

    def body(*refs):
        a_val = _silu_mul(refs[0][...], refs[1][...]) if gated else refs[0][...]
        b_ref = refs[1 + int(gated)]
        part = lax.dot_general(a_val.astype(BF16), b_ref[...].astype(BF16), _DIMS[mode],
                               preferred_element_type=F32)

        def finish(r):
            if has_gu:
                d_gate, d_up = _silu_mul_bwd(refs[2][...], refs[3][...], r)
                refs[n_in][...] = d_gate.astype(refs[n_in].dtype)
                refs[n_in + 1][...] = d_up.astype(refs[n_in + 1].dtype)
                return
            if has_add:
                r = r + refs[n_in - 1][...]
            refs[n_in][...] = r.astype(refs[n_in].dtype)

        if nk == 1:
            finish(part)
        else:
            acc_ref = refs[-1]
            kk = pl.program_id(2)

            @pl.when(kk == 0)
            def _():
                acc_ref[...] = part

            @pl.when(kk > 0)
            def _():
                acc_ref[...] += part

            @pl.when(kk == nk - 1)
            def _():
                finish(acc_ref[...])

    a_bytes, b_bytes = a.size * a.dtype.itemsize, b.size * b.dtype.itemsize
    n_outer = nk == 1 and (n // bn) * a_bytes + b_bytes < a_bytes + (m // bm) * b_bytes

    def at(f):
        return (lambda j, i, kk: f(i, j, kk)) if n_outer else f

    if mode == "nn":
        a_specs = [pl.BlockSpec((bm, bk), at(lambda i, j, kk, o=o: (i, kk + o))) for o in ((0, nk) if gated else (0,))]
        b_spec = pl.BlockSpec((bk, bn), at(lambda i, j, kk: (kk, j)))
    elif mode == "nt":
        a_specs = [pl.BlockSpec((bm, bk), at(lambda i, j, kk: (i, kk)))]
        b_spec = pl.BlockSpec((bn, bk), at(lambda i, j, kk: (j, kk)))
    else:
        a_specs = [pl.BlockSpec((bk, bm), at(lambda i, j, kk, o=o: (kk, i + o)))
                   for o in ((0, m // bm) if gated else (0,))]
        b_spec = pl.BlockSpec((bk, bn), at(lambda i, j, kk: (kk, j)))
    o_spec = pl.BlockSpec((bm, bn), at(lambda i, j, kk: (i, j)))
    in_specs, args = a_specs + [b_spec], [a] * len(a_specs) + [b]
    if has_add:
        in_specs, args = in_specs + [o_spec], args + [add]
    if has_gu:
        in_specs += [o_spec, pl.BlockSpec((bm, bn), at(lambda i, j, kk: (i, j + n // bn)))]
        args += [gate_up, gate_up]
    out = jax.ShapeDtypeStruct((m, n), out_dtype)
    grid = (n // bn, m // bm, nk) if n_outer else (m // bm, n // bn, nk)
    return pl.pallas_call(
        body, name=name, grid=grid, in_specs=in_specs, out_specs=[o_spec, o_spec] if has_gu else o_spec,
        out_shape=[out, out] if has_gu else out,
        scratch_shapes=[pltpu.VMEM((bm, bn), F32)] if nk > 1 else [],
        compiler_params=_cparams(("parallel", "parallel", "arbitrary")),
    )(*args)


def _rows_call(fn, rows, consts, out_rows, out_accs=(), *, bs=256, name):
    s = rows[0].shape[0]
    bs = min(bs, s)
    assert s % bs == 0
    nr, nc, no, na = len(rows), len(consts), len(out_rows), len(out_accs)

    def body(*refs):
        vals = [r[...] for r in refs[:nr + nc]]
        outs = refs[nr + nc:]
        res = fn(*vals)
        if not isinstance(res, (tuple, list)):
            res = (res,)
        assert len(res) == no + na, (len(res), no, na)
        for r, v in zip(outs[:no], res[:no]):
            r[...] = v.astype(r.dtype)
        if na:
            i = pl.program_id(0)
            for r, v in zip(outs[no:], res[no:]):
                @pl.when(i == 0)
                def _(r=r, v=v):
                    r[...] = v

                @pl.when(i > 0)
                def _(r=r, v=v):
                    r[...] += v

    in_specs = [pl.BlockSpec((bs, a.shape[1]), lambda i: (i, 0)) for a in rows]
    in_specs += [pl.BlockSpec(c.shape, lambda i: (0, 0)) for c in consts]
    out_specs = [pl.BlockSpec((bs, c), lambda i: (i, 0)) for c, _ in out_rows]
    out_specs += [pl.BlockSpec(tuple(sh), lambda i: (0, 0)) for sh in out_accs]
    out_shape = [jax.ShapeDtypeStruct((s, c), dt) for c, dt in out_rows]
    out_shape += [jax.ShapeDtypeStruct(tuple(sh), F32) for sh in out_accs]
    res = pl.pallas_call(
        body, name=name, grid=(s // bs,), in_specs=in_specs, out_specs=out_specs, out_shape=out_shape,
        compiler_params=_cparams(("arbitrary",) if na else ("parallel",)),
    )(*rows, *consts)
    return res


def _rms(x, g):
    return x * lax.rsqrt(jnp.mean(x * x, axis=-1, keepdims=True) + NORM_EPS) * g


def _rms_bwd_math(x, g, dy):
    r = lax.rsqrt(jnp.mean(x * x, axis=-1, keepdims=True) + NORM_EPS)
    u = dy * g
    dx = r * u - x * (r * r * r) * jnp.mean(u * x, axis=-1, keepdims=True)
    dg = jnp.sum(dy * x * r, axis=0, keepdims=True)
    return dx, dg


NORM_ROWS = 512


def _rms_fwd(x, g, name):
    return _rows_call(lambda xv, gv: _rms(xv, gv), [x], [g], [(x.shape[1], BF16)], bs=NORM_ROWS, name=name)[0]


def _rms_bwd(x, g, dy, dres, name):
    def fn(xv, dyv, dresv, gv):
        dx, dg = _rms_bwd_math(xv, gv, dyv.astype(F32))
        return dx + dresv, dg
    return _rows_call(fn, [x, dy, dres], [g], [(x.shape[1], F32)], [(1, x.shape[1])], bs=NORM_ROWS, name=name)


def _chunkdot(x, m):
    outs = [jnp.dot(x[:, c:c + LANES], m, precision=lax.Precision.HIGHEST, preferred_element_type=F32)
            for c in range(0, x.shape[1], LANES)]
    return outs[0] if len(outs) == 1 else jnp.concatenate(outs, axis=1)


def _lanes(t, width):
    n = width // LANES
    return t if n == 1 else jnp.concatenate([t] * n, axis=1)


def _rope(x, cs, swap):
    w = x.shape[1]
    return x * _lanes(cs[:, :LANES], w) + _chunkdot(x, swap) * _lanes(cs[:, LANES:], w)


def _rope_t(dy, cs, swap):
    w = dy.shape[1]
    return dy * _lanes(cs[:, :LANES], w) + _chunkdot(dy * _lanes(cs[:, LANES:], w), swap)


def _swap_matrix():
    m = np.zeros((LANES, LANES), np.float32)
    for j in range(LANES):
        src = j + 16 if (j % 32) < 16 else j - 16
        m[src, j] = 1.0
    return jnp.asarray(m)


def _seg_matrix(seg):
    idx = np.arange(LANES) // seg
    return jnp.asarray((idx[:, None] == idx[None, :]).astype(np.float32))


def _rope_tables(s):
    pos = jnp.arange(s)

    def angles(p, dim):
        freqs = ROPE_THETA ** (-jnp.arange(0, dim, 2, dtype=F32) / dim)
        ang = p.astype(F32)[:, None] * freqs[None, :]
        return jnp.cos(ang), jnp.sin(ang)

    cos_t, sin_t = angles(pos, MLA_ROPE)
    one, zero = jnp.ones((s, 64), F32), jnp.zeros((s, 64), F32)
    mla = jnp.concatenate([one, cos_t, cos_t, one[:, :32], zero, -sin_t, sin_t, zero[:, :32]], axis=1)
    cos_r, sin_r = angles(pos // GRID_W, GQA_HEAD_DIM // 2)
    cos_c, sin_c = angles(pos % GRID_W, GQA_HEAD_DIM // 2)
    c64 = jnp.concatenate([cos_r, cos_r, cos_c, cos_c], axis=1)
    s64 = jnp.concatenate([-sin_r, sin_r, -sin_c, sin_c], axis=1)
    gqa = jnp.concatenate([c64, c64, s64, s64], axis=1)
    return mla, gqa


def _stack_heads(ref, heads, d, dtype=None):
    parts = [ref[:, hd * d:(hd + 1) * d] for hd in heads]
    out = parts[0] if len(parts) == 1 else jnp.concatenate(parts, axis=0)
    return out if dtype is None else out.astype(dtype)


def _fill_v_ones(v_ref, va_ref, hb, dv):
    @pl.when(pl.program_id(1) == 0)
    def _():
        ones = jnp.ones((v_ref.shape[0], dv), BF16)
        for h in range(hb):
            va_ref[:, 2 * h * dv:(2 * h + 1) * dv] = v_ref[:, h * dv:(h + 1) * dv]
            va_ref[:, (2 * h + 1) * dv:(2 * h + 2) * dv] = ones


def _flash_fwd(q, k, v, *, R, dk, dv, hb, bq, bk, name):
    s = q.shape[0]
    g = k.shape[1] // dk
    ng = g // hb
    bq, bk = min(bq, s), min(bk, s)
    nq, nkb = s // bq, s // bk
    rb = R * bq

    def body(q_ref, k_ref, v_ref, o_ref, lse_ref, va_ref):
        _fill_v_ones(v_ref, va_ref, hb, dv)
        head_sets = [[h * R + r for r in range(R)] for h in range(hb)]
        qss = [_stack_heads(q_ref, heads, dk) for heads in head_sets]

        def step(jj, carry):
            carry = list(carry)
            rows = [pl.ds(pl.multiple_of((jj * unroll + u) * bk, bk), bk) for u in range(unroll)]
            scs = [[lax.dot_general(qss[h], k_ref[rows[u], h * dk:(h + 1) * dk], _DIMS["nt"],
                                    preferred_element_type=F32) for h in range(hb)] for u in range(unroll)]
            for u in range(unroll):
                m2s = [jnp.maximum(carry[h][0], jnp.max(scs[u][h], axis=1, keepdims=True)) for h in range(hb)]
                ps = [jnp.exp2(scs[u][h] - m2s[h]).astype(BF16) for h in range(hb)]
                pvs = [jnp.dot(ps[h], va_ref[rows[u], 2 * h * dv:2 * (h + 1) * dv], preferred_element_type=F32)
                       for h in range(hb)]
                carry = [(m2s[h], jnp.exp2(carry[h][0] - m2s[h]) * carry[h][1] + pvs[h]) for h in range(hb)]
            return tuple(carry)

        unroll = 2 if nkb % 2 == 0 else 1
        init = tuple((jnp.full((rb, 1), NEG_INF, F32), jnp.zeros((rb, 2 * dv), F32)) for _ in range(hb))
        final = lax.fori_loop(0, nkb // unroll, step, init)
        for h, heads in enumerate(head_sets):
            m, acc = final[h]
            l = acc[:, dv:dv + 1]
            o = acc[:, :dv] / l
            lse = m + jnp.log2(l)
            for r, hd in enumerate(heads):
                o_ref[:, hd * dv:(hd + 1) * dv] = o[r * bq:(r + 1) * bq].astype(o_ref.dtype)
                lse_ref[0, :, hd:hd + 1] = lse[r * bq:(r + 1) * bq]

    return pl.pallas_call(
        body, name=name, grid=(ng, nq),
        in_specs=[pl.BlockSpec((bq, hb * R * dk), lambda gi, i: (i, gi)),
                  pl.BlockSpec((s, hb * dk), lambda gi, i: (0, gi)),
                  pl.BlockSpec((s, hb * dv), lambda gi, i: (0, gi))],
        out_specs=[pl.BlockSpec((bq, hb * R * dv), lambda gi, i: (i, gi)),
                   pl.BlockSpec((1, bq, hb * R), lambda gi, i: (gi, i, 0))],
        out_shape=[jax.ShapeDtypeStruct((s, g * R * dv), BF16), jax.ShapeDtypeStruct((ng, s, hb * R), F32)],
        scratch_shapes=[pltpu.VMEM((s, 2 * hb * dv), BF16)],
        compiler_params=_cparams(("parallel", "arbitrary")),
    )(q, k, v)


def _flash_bwd(q, k, v, o, do, lse, *, R, dk, dv, hb, bq, bk, name):
    s = q.shape[0]
    g = k.shape[1] // dk
    ng = g // hb
    bq, bk = min(bq, s), min(bk, s)
    nq, nkb = s // bq, s // bk
    rb = R * bq

    def body(q_ref, k_ref, v_ref, o_ref, do_ref, lse_ref, dq_ref, dkt_ref, dvt_ref, va_ref):
        @pl.when(pl.program_id(1) == 0)
        def _():
            dkt_ref[...] = jnp.zeros(dkt_ref.shape, F32)
            dvt_ref[...] = jnp.zeros(dvt_ref.shape, F32)

        _fill_v_ones(v_ref, va_ref, hb, dv)
        lane = lax.broadcasted_iota(jnp.int32, (rb, dv), 1)
        head_sets = [[h * R + r for r in range(R)] for h in range(hb)]
        q_t = jnp.transpose(q_ref[...].astype(F32)).astype(BF16)
        do_t = jnp.transpose(do_ref[...].astype(F32)).astype(BF16)

        def stack_t(t, heads, d):
            parts = [t[hd * d:(hd + 1) * d] for hd in heads]
            return parts[0] if len(parts) == 1 else jnp.concatenate(parts, axis=1)

        qss, qts, dots, dosas, lcols = [], [], [], [], []
        for heads in head_sets:
            dos = _stack_heads(do_ref, heads, dv, BF16)
            delta = jnp.sum(dos.astype(F32) * _stack_heads(o_ref, heads, dv, F32), axis=1, keepdims=True)
            hi = delta.astype(BF16).astype(F32)
            lo = delta - hi
            qss.append(_stack_heads(q_ref, heads, dk))
            qts.append(stack_t(q_t, heads, dk))
            dots.append(stack_t(do_t, heads, dv))
            dosas.append(jnp.concatenate(
                [dos, jnp.where(lane == 0, -hi, jnp.where(lane == 1, -lo, 0.0)).astype(BF16)], axis=1))
            cols = [lse_ref[0, :, hd:hd + 1] for hd in heads]
            lcols.append(cols[0] if R == 1 else jnp.concatenate(cols, axis=0))

        def step(j, dqs):
            r0 = pl.multiple_of(j * bk, bk)
            hs = range(hb)
            kjs = [k_ref[pl.ds(r0, bk), h * dk:(h + 1) * dk] for h in hs]
            vas = [va_ref[pl.ds(r0, bk), 2 * h * dv:2 * (h + 1) * dv] for h in hs]
            ss = [lax.dot_general(qss[h], kjs[h], _DIMS["nt"], preferred_element_type=F32) for h in hs]
            dps = [lax.dot_general(dosas[h], vas[h], _DIMS["nt"], preferred_element_type=F32) for h in hs]
            ps = [jnp.exp2(ss[h] - lcols[h]) for h in hs]
            pbs = [ps[h].astype(BF16) for h in hs]
            dss = [(ps[h] * dps[h]).astype(BF16) for h in hs]
            for h in hs:
                dvt_ref[0, j, h * dv:(h + 1) * dv, :] += jnp.dot(dots[h], pbs[h], preferred_element_type=F32)
            for h in hs:
                dkt_ref[0, j, h * dk:(h + 1) * dk, :] += jnp.dot(qts[h], dss[h], preferred_element_type=F32)
            return tuple(dqs[h] + jnp.dot(dss[h], kjs[h], preferred_element_type=F32) for h in hs)

        dqs = lax.fori_loop(0, nkb, step, tuple(jnp.zeros((rb, dk), F32) for _ in range(hb)))
        for h, heads in enumerate(head_sets):
            for r, hd in enumerate(heads):
                dq_ref[:, hd * dk:(hd + 1) * dk] = dqs[h][r * bq:(r + 1) * bq]

    qspec = pl.BlockSpec((bq, hb * R * dk), lambda gi, i: (i, gi))
    ospec = pl.BlockSpec((bq, hb * R * dv), lambda gi, i: (i, gi))
    kspec = pl.BlockSpec((s, hb * dk), lambda gi, i: (0, gi))
    vspec = pl.BlockSpec((s, hb * dv), lambda gi, i: (0, gi))
    dq, dkt, dvt = pl.pallas_call(
        body, name=name, grid=(ng, nq),
        in_specs=[qspec, kspec, vspec, ospec, ospec, pl.BlockSpec((1, bq, hb * R), lambda gi, i: (gi, i, 0))],
        out_specs=[qspec, pl.BlockSpec((1, nkb, hb * dk, bk), lambda gi, i: (gi, 0, 0, 0)),
                   pl.BlockSpec((1, nkb, hb * dv, bk), lambda gi, i: (gi, 0, 0, 0))],
        out_shape=[jax.ShapeDtypeStruct((s, g * R * dk), F32), jax.ShapeDtypeStruct((ng, nkb, hb * dk, bk), F32),
                   jax.ShapeDtypeStruct((ng, nkb, hb * dv, bk), F32)],
        scratch_shapes=[pltpu.VMEM((s, 2 * hb * dv), BF16)],
        compiler_params=_cparams(("parallel", "arbitrary")),
    )(q, k, v, o, do, lse)
    return dq, _keys_first(dkt, name + "_dk"), _keys_first(dvt, name + "_dv")


def _keys_first(t, name):
    ng, nkb, f, bk = t.shape

    def body(t_ref, o_ref):
        for j in range(nkb):
            o_ref[j * bk:(j + 1) * bk, :] = jnp.transpose(t_ref[0, j])

    return pl.pallas_call(
        body, name=name, grid=(ng,),
        in_specs=[pl.BlockSpec((1, nkb, f, bk), lambda gi: (gi, 0, 0, 0))],
        out_specs=pl.BlockSpec((nkb * bk, f), lambda gi: (0, gi)),
        out_shape=jax.ShapeDtypeStruct((nkb * bk, ng * f), t.dtype),
        compiler_params=_cparams(("parallel",)),
    )(t)


DIL_T = 1024
DIL_P = DIL_HALF
DIL_NCOL = IN_B // DIL_W


DIL_BATCH = 4


def _alibi_slope(head):
    return float(2.0 ** (-8.0 * (head + 1) / DIL_HEADS))


def _slot(sl_i):
    return slice(sl_i * DIL_HEAD_DIM, (sl_i + 1) * DIL_HEAD_DIM)


def _halo_specs(d, col, s, t):
    h = DIL_P * d
    per, last = t // h, s // h - 1
    return [pl.BlockSpec((h, DIL_W), lambda c: (jnp.maximum(c * per - 1, 0), col)),
            pl.BlockSpec((t, DIL_W), lambda c: (c, col)),
            pl.BlockSpec((h, DIL_W), lambda c: (jnp.minimum((c + 1) * per, last), col))]


def _staging(rows):
    return tuple(pltpu.VMEM((rows, LANES), F32) for _ in range(DIL_W // LANES))


def _stage(buf, refs):
    off = 0
    for r in refs:
        val = r[...].astype(F32)
        for j in range(DIL_W // LANES):
            buf[j][off:off + r.shape[0], :] = val[:, j * LANES:(j + 1) * LANES]
        off += r.shape[0]


def _unstage(buf, ref):
    ref[...] = jnp.concatenate([half[...] for half in buf], axis=1).astype(ref.dtype)


def _sub_tiles(d, t):
    return [(b * DIL_P * d + r, b * DIL_P) for b in range(t // (DIL_P * d)) for r in range(d)]


def _rows(start, size, d):
    return pl.ds(start, size, stride=d) if d > 1 else pl.ds(start, size)


def _strided(buf, start, size, d):
    return jnp.concatenate([half[_rows(start, size, d), :] for half in buf], axis=1)


def _put_strided(buf, start, d, val):
    for j in range(DIL_W // LANES):
        buf[j][_rows(start, val.shape[0], d), :] = val[:, j * LANES:(j + 1) * LANES]


def _band(u0, length, d, queries_wide):
    if queries_wide:
        shape = (3 * DIL_P, DIL_P)
        wide = u0 - DIL_P + lax.broadcasted_iota(jnp.int32, shape, 0)
        narrow = u0 + lax.broadcasted_iota(jnp.int32, shape, 1)
    else:
        shape = (DIL_P, 3 * DIL_P)
        narrow = u0 + lax.broadcasted_iota(jnp.int32, shape, 0)
        wide = u0 - DIL_P + lax.broadcasted_iota(jnp.int32, shape, 1)
    rel = jnp.abs(wide - narrow)
    valid = (rel <= DIL_HALF) & (wide >= 0) & (wide < length)
    return valid, rel.astype(F32) * float(d)


def _dil_fwd(zb, grp, name):
    s = zb.shape[0]
    d = DIL_PAIRS[grp][1]
    t = min(DIL_T, s)
    h = DIL_P * d
    scale = DIL_HEAD_DIM ** -0.5

    def body(q_ref, kp, kc, kn, vp, vc, vn, o_ref, lse_ref, qbuf, kbuf, vbuf, obuf, lbuf):
        _stage(qbuf, (q_ref,))
        _stage(kbuf, (kp, kc, kn))
        _stage(vbuf, (vp, vc, vn))
        u_step = pl.program_id(0) * (t // d)
        tiles = _sub_tiles(d, t)
        for g0 in range(0, len(tiles), DIL_BATCH):
            batch = tiles[g0:g0 + DIL_BATCH]
            masks = [_band(u_step + u, s // d, d, False) for _, u in batch]
            qs = [_strided(qbuf, row, DIL_P, d).astype(BF16) for row, _ in batch]
            ks = [_strided(kbuf, row, 3 * DIL_P, d).astype(BF16) for row, _ in batch]
            vs = [_strided(vbuf, row, 3 * DIL_P, d).astype(BF16) for row, _ in batch]
            chains = [(i, sl_i) for i in range(len(batch)) for sl_i in range(DIL_SLOTS)]
            scs = [lax.dot_general(qs[i][:, _slot(sl_i)], ks[i][:, _slot(sl_i)], _DIMS["nt"],
                                   preferred_element_type=F32) for i, sl_i in chains]
            scs = [jnp.where(masks[i][0], sc * scale - _alibi_slope(grp * DIL_SLOTS + sl_i) * masks[i][1], NEG_INF)
                   for (i, sl_i), sc in zip(chains, scs)]
            ms = [jnp.max(sc, axis=1, keepdims=True) for sc in scs]
            es = [jnp.exp(sc - m) for sc, m in zip(scs, ms)]
            dens = [jnp.sum(e, axis=1, keepdims=True) for e in es]
            outs = [jnp.dot((e / den).astype(BF16), vs[i][:, _slot(sl_i)], preferred_element_type=F32)
                    for (i, sl_i), e, den in zip(chains, es, dens)]
            lses = [jnp.broadcast_to(m + jnp.log(den), (DIL_P, DIL_HEAD_DIM)) for m, den in zip(ms, dens)]
            for i, (row, _) in enumerate(batch):
                pick = slice(i * DIL_SLOTS, (i + 1) * DIL_SLOTS)
                _put_strided(obuf, row, d, jnp.concatenate(outs[pick], axis=1))
                _put_strided(lbuf, row, d, jnp.concatenate(lses[pick], axis=1))
        _unstage(obuf, o_ref)
        _unstage(lbuf, lse_ref)

    own = pl.BlockSpec((t, DIL_W), lambda c: (c, 0))
    return pl.pallas_call(
        body, name=name, grid=(s // t,),
        in_specs=[pl.BlockSpec((t, DIL_W), lambda c: (c, grp))] + _halo_specs(d, 3 + grp, s, t)
        + _halo_specs(d, 6 + grp, s, t),
        out_specs=[own, own], out_shape=[jax.ShapeDtypeStruct((s, DIL_W), F32)] * 2,
        scratch_shapes=[_staging(t), _staging(t + 2 * h), _staging(t + 2 * h), _staging(t), _staging(t)],
        compiler_params=_cparams(("parallel",)),
    )(zb, zb, zb, zb, zb, zb, zb)


def _dil_bwd(zb, do, lse, dl, grp, name):
    s = zb.shape[0]
    d = DIL_PAIRS[grp][1]
    t = min(DIL_T, s)
    h = DIL_P * d
    scale = DIL_HEAD_DIM ** -0.5

    def chain_grads(qs, ks, vs, dos, lses, dls, masks):
        chains = [(i, sl_i) for i in range(len(qs)) for sl_i in range(DIL_SLOTS)]
        scs = [lax.dot_general(qs[i][:, _slot(sl_i)], ks[i][:, _slot(sl_i)], _DIMS["nt"],
                               preferred_element_type=F32) for i, sl_i in chains]
        dps = [lax.dot_general(dos[i][:, _slot(sl_i)], vs[i][:, _slot(sl_i)], _DIMS["nt"],
                               preferred_element_type=F32) for i, sl_i in chains]
        ps = [jnp.exp(jnp.where(masks[i][0], sc * scale - _alibi_slope(grp * DIL_SLOTS + sl_i) * masks[i][1],
                                NEG_INF) - lses[i][:, sl_i * DIL_HEAD_DIM:sl_i * DIL_HEAD_DIM + 1])
              for (i, sl_i), sc in zip(chains, scs)]
        dss = [(p * (dp - dls[i][:, sl_i * DIL_HEAD_DIM:sl_i * DIL_HEAD_DIM + 1]) * scale).astype(BF16)
               for (i, sl_i), p, dp in zip(chains, ps, dps)]
        return chains, ps, dss

    def dq_body(q_ref, kp, kc, kn, vp, vc, vn, do_ref, lse_ref, dl_ref, dq_ref, qbuf, kbuf, vbuf, dobuf, lsebuf,
                dlbuf, obuf):
        _stage(qbuf, (q_ref,))
        _stage(dobuf, (do_ref,))
        _stage(lsebuf, (lse_ref,))
        _stage(dlbuf, (dl_ref,))
        _stage(kbuf, (kp, kc, kn))
        _stage(vbuf, (vp, vc, vn))
        u_step = pl.program_id(0) * (t // d)
        tiles = _sub_tiles(d, t)
        for g0 in range(0, len(tiles), DIL_BATCH):
            batch = tiles[g0:g0 + DIL_BATCH]
            masks = [_band(u_step + u, s // d, d, False) for _, u in batch]
            narrow = [[_strided(b, row, DIL_P, d) for row, _ in batch] for b in (qbuf, dobuf, lsebuf, dlbuf)]
            ks = [_strided(kbuf, row, 3 * DIL_P, d).astype(BF16) for row, _ in batch]
            vs = [_strided(vbuf, row, 3 * DIL_P, d).astype(BF16) for row, _ in batch]
            chains, _, dss = chain_grads([a.astype(BF16) for a in narrow[0]], ks, vs,
                                         [a.astype(BF16) for a in narrow[1]], narrow[2], narrow[3], masks)
            outs = [jnp.dot(ds, ks[i][:, _slot(sl_i)], preferred_element_type=F32)
                    for (i, sl_i), ds in zip(chains, dss)]
            for i, (row, _) in enumerate(batch):
                _put_strided(obuf, row, d, jnp.concatenate(outs[i * DIL_SLOTS:(i + 1) * DIL_SLOTS], axis=1))
        _unstage(obuf, dq_ref)

    def dkv_body(k_ref, v_ref, qp, qc, qn, dop, doc, don, lp, lc, ln, dlp, dlc, dln, dk_ref, dv_ref,
                 kbuf, vbuf, qbuf, dobuf, lsebuf, dlbuf, dkbuf, dvbuf):
        _stage(kbuf, (k_ref,))
        _stage(vbuf, (v_ref,))
        _stage(qbuf, (qp, qc, qn))
        _stage(dobuf, (dop, doc, don))
        _stage(lsebuf, (lp, lc, ln))
        _stage(dlbuf, (dlp, dlc, dln))
        u_step = pl.program_id(0) * (t // d)
        tiles = _sub_tiles(d, t)
        for g0 in range(0, len(tiles), DIL_BATCH):
            batch = tiles[g0:g0 + DIL_BATCH]
            masks = [_band(u_step + u, s // d, d, True) for _, u in batch]
            ks = [_strided(kbuf, row, DIL_P, d).astype(BF16) for row, _ in batch]
            vs = [_strided(vbuf, row, DIL_P, d).astype(BF16) for row, _ in batch]
            wide_ = [[_strided(b, row, 3 * DIL_P, d) for row, _ in batch] for b in (qbuf, dobuf, lsebuf, dlbuf)]
            qs, dos = [a.astype(BF16) for a in wide_[0]], [a.astype(BF16) for a in wide_[1]]
            chains, ps, dss = chain_grads(qs, ks, vs, dos, wide_[2], wide_[3], masks)
            dvs = [lax.dot_general(p.astype(BF16), dos[i][:, _slot(sl_i)], _DIMS["tn"], preferred_element_type=F32)
                   for (i, sl_i), p in zip(chains, ps)]
            dks = [lax.dot_general(ds, qs[i][:, _slot(sl_i)], _DIMS["tn"], preferred_element_type=F32)
                   for (i, sl_i), ds in zip(chains, dss)]
            for i, (row, _) in enumerate(batch):
                pick = slice(i * DIL_SLOTS, (i + 1) * DIL_SLOTS)
                _put_strided(dkbuf, row, d, jnp.concatenate(dks[pick], axis=1))
                _put_strided(dvbuf, row, d, jnp.concatenate(dvs[pick], axis=1))
        _unstage(dkbuf, dk_ref)
        _unstage(dvbuf, dv_ref)

    def zcur(col):
        return pl.BlockSpec((t, DIL_W), lambda c: (c, col))

    own = pl.BlockSpec((t, DIL_W), lambda c: (c, 0))
    out = jax.ShapeDtypeStruct((s, DIL_W), F32)
    tile, wide = _staging(t), _staging(t + 2 * h)
    dq = pl.pallas_call(
        dq_body, name=name + "_dq", grid=(s // t,),
        in_specs=[zcur(grp)] + _halo_specs(d, 3 + grp, s, t) + _halo_specs(d, 6 + grp, s, t) + [own, own, own],
        out_specs=own, out_shape=out, scratch_shapes=[tile, wide, wide, tile, tile, tile, tile],
        compiler_params=_cparams(("parallel",)),
    )(zb, zb, zb, zb, zb, zb, zb, do, lse, dl)
    own3 = _halo_specs(d, 0, s, t)
    dk, dv = pl.pallas_call(
        dkv_body, name=name + "_dkv", grid=(s // t,),
        in_specs=[zcur(3 + grp), zcur(6 + grp)] + _halo_specs(d, grp, s, t) + own3 + own3 + own3,
        out_specs=[own, own], out_shape=[out, out],
        scratch_shapes=[tile, tile, wide, wide, wide, wide, tile, tile],
        compiler_params=_cparams(("parallel",)),
    )(zb, zb, zb, zb, zb, do, do, do, lse, lse, lse, dl, dl, dl)
    return dq, dk, dv


def _dil_combine(os_, ls_, name):
    def fn(o0, o1, o2, l0, l1, l2):
        m = jnp.maximum(jnp.maximum(l0, l1), l2)
        e0, e1, e2 = jnp.exp(l0 - m), jnp.exp(l1 - m), jnp.exp(l2 - m)
        den = e0 + e1 + e2
        comb = (e0 / den) * o0 + (e1 / den) * o1 + (e2 / den) * o2
        return comb, m + jnp.log(den)
    return _rows_call(fn, list(os_) + list(ls_), [], [(DIL_W, BF16), (DIL_W, F32)], name=name)


def _dil_combine_bwd(dcomb, os_, ls_, lt, seg64, name):
    def fn(dc, o0, o1, o2, l0, l1, l2, ltv, seg):
        w = [jnp.exp(l - ltv) for l in (l0, l1, l2)]
        comb = w[0] * o0 + w[1] * o1 + w[2] * o2
        t = _chunkdot(dc * comb, seg)
        return [wg * dc for wg in w] + [wg * t for wg in w]
    return _rows_call(fn, [dcomb] + list(os_) + list(ls_) + [lt], [seg64],
                      [(DIL_W, BF16)] * 3 + [(DIL_W, F32)] * 3, name=name)


def _mla_prep(za, gq, gkv, cs, swap, name):
    def fn(z, csv, gqv, gkvv, sw):
        return (_rms(z[:, :MLA_Q_RANK], gqv), _rms(z[:, MLA_Q_RANK:640], gkvv), _rope(z[:, 640:], csv, sw))
    return _rows_call(fn, [za, cs], [gq, gkv, swap], [(MLA_Q_RANK, BF16), (MLA_KV_RANK, BF16), (LANES, F32)],
                      name=name)


def _mla_prep_bwd(za, cs, dcq, dckv, dkr, gq, gkv, swap, name):
    def fn(z, csv, dcqv, dckvv, dkrv, gqv, gkvv, sw):
        d1, dg1 = _rms_bwd_math(z[:, :MLA_Q_RANK], gqv, dcqv)
        d2, dg2 = _rms_bwd_math(z[:, MLA_Q_RANK:640], gkvv, dckvv)
        d3 = _rope_t(dkrv, csv, sw)
        return jnp.concatenate([d1, d2, d3], axis=1), dg1, dg2
    return _rows_call(fn, [za, cs, dcq, dckv, dkr], [gq, gkv, swap], [(IN_A_PAD, BF16)],
                      [(1, MLA_Q_RANK), (1, MLA_KV_RANK)], name=name)


def _mla_qk(q_raw, k_pad, krr, cs, swap, name):
    w = MLA_HEADS * MLA_DK

    def fn(qv, kv, krv, csv, sw):
        return _rope(qv, csv, sw) * (MLA_SCALE * LOG2E), kv + _lanes(krv, w)
    return _rows_call(fn, [q_raw, k_pad, krr, cs], [swap], [(w, BF16), (w, BF16)], name=name)


def _mla_qk_bwd(dqh, dkh, cs, swap, name):
    w = MLA_HEADS * MLA_DK

    def fn(dq, dk, csv, sw):
        dk = dk * LN2
        acc = dk[:, :LANES]
        for h in range(1, MLA_HEADS):
            acc = acc + dk[:, h * LANES:(h + 1) * LANES]
        lane = lax.broadcasted_iota(jnp.int32, acc.shape, 1)
        acc = jnp.where((lane >= MLA_NOPE) & (lane < MLA_NOPE + MLA_ROPE), acc, 0.0)
        return _rope_t(dq * MLA_SCALE, csv, sw), dk, acc
    return _rows_call(fn, [dqh, dkh, cs], [swap], [(w, BF16), (w, BF16), (LANES, F32)], name=name)


def _head_norm(t, g2, seg):
    r = lax.rsqrt(_chunkdot(t * t, seg) * (1.0 / GQA_HEAD_DIM) + NORM_EPS)
    return t * r * _lanes(g2, t.shape[1]), r


def _head_norm_bwd(t, g2, seg, dn):
    w = t.shape[1]
    r = lax.rsqrt(_chunkdot(t * t, seg) * (1.0 / GQA_HEAD_DIM) + NORM_EPS)
    u = dn * _lanes(g2, w)
    dt = r * u - t * (r * r * r) * (_chunkdot(u * t, seg) * (1.0 / GQA_HEAD_DIM))
    dgw = jnp.sum(dn * t * r, axis=0, keepdims=True)
    dg = dgw[:, :LANES]
    for c in range(LANES, w, LANES):
        dg = dg + dgw[:, c:c + LANES]
    return dt, dg


def _gqa_prep(q_raw, kv_raw, cs, gq2, gk2, seg, swap, name):
    kw = GQA_KV_HEADS * GQA_HEAD_DIM

    def fn(qv, kvv, csv, gqv, gkv, sg, sw):
        qn, _ = _head_norm(qv, gqv, sg)
        kn, _ = _head_norm(kvv[:, :kw], gkv, sg)
        return _rope(qn, csv, sw) * (GQA_SCALE * LOG2E), _rope(kn, csv, sw), kvv[:, kw:]
    return _rows_call(fn, [q_raw, kv_raw, cs], [gq2, gk2, seg, swap],
                      [(GQA_HEADS * GQA_HEAD_DIM, BF16), (kw, BF16), (kw, BF16)], name=name)


def _gqa_prep_bwd(q_raw, kv_raw, cs, dqh, dkh, dv, gq2, gk2, seg, swap, name):
    kw = GQA_KV_HEADS * GQA_HEAD_DIM

    def fn(qv, kvv, csv, dq, dk, dvv, gqv, gkv, sg, sw):
        dqr, dgq = _head_norm_bwd(qv, gqv, sg, _rope_t(dq * GQA_SCALE, csv, sw))
        dkr, dgk = _head_norm_bwd(kvv[:, :kw], gkv, sg, _rope_t(dk * LN2, csv, sw))
        return dqr, jnp.concatenate([dkr, dvv], axis=1), dgq, dgk
    return _rows_call(fn, [q_raw, kv_raw, cs, dqh, dkh, dv], [gq2, gk2, seg, swap],
                      [(GQA_HEADS * GQA_HEAD_DIM, BF16), (2 * kw, BF16)], [(1, LANES), (1, LANES)], name=name)


def _loss_head(x, target, g, name):
    dm = x.shape[1]

    def fn(xv, tv, gv):
        err = _rms(xv, gv) - tv
        loss = 0.5 * jnp.sum(err * err) / dm
        dx, dg = _rms_bwd_math(xv, gv, err * (1.0 / dm))
        return dx, jnp.zeros((1, LANES), F32) + loss, dg
    return _rows_call(fn, [x, target], [g], [(dm, F32)], [(1, LANES), (1, dm)], name=name)


def _adamw(w, g, m, v, name):
    def fn(wv, gv, mv, vv):
        m2 = ADAM_B1 * mv + (1.0 - ADAM_B1) * gv
        v2 = ADAM_B2 * vv + (1.0 - ADAM_B2) * (gv * gv)
        m_hat = m2 / (1.0 - ADAM_B1 ** ADAM_STEP)
        v_hat = v2 / (1.0 - ADAM_B2 ** ADAM_STEP)
        return -ADAM_LR * (m_hat / (jnp.sqrt(v_hat) + ADAM_EPS) + ADAM_WD * wv), m2, v2
    c = w.shape[1]
    return _rows_call(fn, [w, g, m, v], [], [(c, F32)] * 3, bs=_pick(w.shape[0], 256, 8), name=name)


HBM_SPEC = pl.BlockSpec(memory_space=pltpu.HBM)
VMEM_SPEC = pl.BlockSpec(memory_space=pltpu.VMEM)


def _position():
    return lax.axis_index("x"), lax.axis_index("y"), lax.axis_index("c")


def _other_chips(x, y):
    return [(1 - x, y), (x, 1 - y), (1 - x, 1 - y)]


HALF_W = PACK_W // 2


def _cols(c):
    return pl.ds(pl.multiple_of(c * HALF_W, HALF_W), HALF_W)


def _all_gather_weights(packed):
    rows = packed.shape[0]

    def body(p_ref, g_ref, send_sems, recv_sems, local_sem):
        x, y, c = _position()
        chips = _other_chips(x, y)

        def half(chip, hc):
            return g_ref.at[2 * chip[0] + chip[1], :, _cols(hc)]

        def copy(j, src, dst, to):
            return pltpu.make_async_remote_copy(src_ref=src, dst_ref=dst, send_sem=send_sems.at[j],
                                                recv_sem=recv_sems.at[j], device_id=to, device_id_type=MESH)

        mine = pltpu.make_async_copy(p_ref, g_ref.at[2 * x + y], local_sem)
        mine.start()
        first = [copy(j, p_ref.at[:, _cols(c)], half((x, y), c), (*chip, c)) for j, chip in enumerate(chips)]
        for cp in first:
            cp.start()
        passed = [copy(3 + j, half(chip, c), half(chip, c), (x, y, 1 - c)) for j, chip in enumerate(chips)]
        for j, chip in enumerate(chips):
            copy(j, half(chip, c), half(chip, c), (x, y, c)).wait_recv()
            passed[j].start()
        for j, chip in enumerate(chips):
            copy(3 + j, half(chip, 1 - c), half(chip, 1 - c), (x, y, c)).wait_recv()
        for cp in first + passed:
            cp.wait_send()
        mine.wait()

    return pl.pallas_call(
        body, name="all_gather_weights", in_specs=[HBM_SPEC], out_specs=HBM_SPEC,
        out_shape=jax.ShapeDtypeStruct((4, rows, packed.shape[1]), packed.dtype),
        scratch_shapes=[pltpu.SemaphoreType.DMA((6,)), pltpu.SemaphoreType.DMA((6,)), pltpu.SemaphoreType.DMA],
    )(packed)


SEM_SPEC = pl.BlockSpec(memory_space=pltpu.SEMAPHORE)
ANY_SPEC = pl.BlockSpec(memory_space=pl.ANY)
DATAFLOW = pltpu.SideEffectType.DATAFLOW_SIDE_EFFECTING


def _hbm(a):
    return pltpu.with_memory_space_constraint(a, pltpu.HBM)


def _gather_start(packed, tag):
    rows = packed.shape[0]

    def body(p_ref, g_ref, send_sems, recv_sems, p_thru, g_thru, token):
        x, y, c = _position()
        for j, chip in enumerate(_other_chips(x, y)):
            for s in range(2):
                pltpu.make_async_remote_copy(
                    src_ref=p_ref.at[:, _cols(c)], dst_ref=g_ref.at[2 * x + y, :, _cols(c)],
                    send_sem=send_sems.at[2 * j + s], recv_sem=recv_sems.at[2 * j + s],
                    device_id=(*chip, 1 - c if s else c), device_id_type=MESH).start()
        token[...] = jnp.zeros_like(token)

    return pl.pallas_call(
        body, name=tag + "_start",
        out_shape=(pltpu.SemaphoreType.DMA((6,)), pltpu.SemaphoreType.DMA((6,)), pltpu.HBM(packed.shape, packed.dtype),
                   pltpu.HBM((4, rows, PACK_W), packed.dtype), jax.ShapeDtypeStruct((8, LANES), F32)),
        in_specs=(HBM_SPEC, HBM_SPEC), out_specs=(SEM_SPEC, SEM_SPEC, HBM_SPEC, HBM_SPEC, VMEM_SPEC),
        input_output_aliases={0: 2, 1: 3},
        compiler_params=pltpu.CompilerParams(has_side_effects=DATAFLOW),
    )(_hbm(packed), _hbm(lax.empty((4, rows, PACK_W), packed.dtype)))


def _gather_wait(send_sems, recv_sems, p_thru, g_thru, after, tag):
    def body(p_ref, g_ref, send_sems, recv_sems, after_ref, p_dead, got_ref):
        x, y, c = _position()
        for j, chip in enumerate(_other_chips(x, y)):
            for s in range(2):
                cp = pltpu.make_async_remote_copy(
                    src_ref=p_ref.at[:, _cols(c)], dst_ref=g_ref.at[2 * chip[0] + chip[1], :, _cols(1 - c if s else c)],
                    send_sem=send_sems.at[2 * j + s], recv_sem=recv_sems.at[2 * j + s],
                    device_id=(x, y, c), device_id_type=MESH)
                cp.wait_send()
                cp.wait_recv()

    return pl.pallas_call(
        body, name=tag + "_wait",
        out_shape=(pltpu.HBM(p_thru.shape, p_thru.dtype), pltpu.HBM(g_thru.shape, g_thru.dtype)),
        in_specs=(HBM_SPEC, HBM_SPEC, SEM_SPEC, SEM_SPEC, ANY_SPEC), out_specs=(HBM_SPEC, HBM_SPEC),
        input_output_aliases={0: 0, 1: 1},
        compiler_params=pltpu.CompilerParams(has_side_effects=DATAFLOW),
    )(p_thru, g_thru, send_sems, recv_sems, after)[1]


def _sibling_swap_halves(grads, tag):
    rows = grads.shape[1]

    def body(g_ref, a_ref, send_sem, recv_sem):
        x, y, c = _position()
        cp = pltpu.make_async_remote_copy(src_ref=g_ref.at[:, :, _cols(1 - c)], dst_ref=a_ref,
                                          send_sem=send_sem, recv_sem=recv_sem, device_id=(x, y, 1 - c),
                                          device_id_type=MESH)
        cp.start()
        cp.wait()

    return pl.pallas_call(
        body, name=tag + "_swap_cores", in_specs=[HBM_SPEC], out_specs=HBM_SPEC,
        out_shape=jax.ShapeDtypeStruct((4, rows, HALF_W), grads.dtype),
        scratch_shapes=[pltpu.SemaphoreType.DMA, pltpu.SemaphoreType.DMA],
    )(grads)


def _rs_block(rows):
    return max(d for d in range(16, 1601, 16) if rows % d == 0)


def _chip_sum(grads, other, c, tag):
    rows = other.shape[1]
    rb = _rs_block(rows)

    def body(c_ref, g_ref, a_ref, o_ref):
        o_ref[...] = (g_ref[...] + a_ref[...]).astype(o_ref.dtype)

    blk = (1, rb, HALF_W)
    return pl.pallas_call(
        body, name=tag + "_chip_sum",
        grid_spec=pltpu.PrefetchScalarGridSpec(
            num_scalar_prefetch=1, grid=(4, rows // rb),
            in_specs=[pl.BlockSpec(blk, lambda k, i, c_ref: (k, i, c_ref[0])),
                      pl.BlockSpec(blk, lambda k, i, c_ref: (k, i, 0))],
            out_specs=pl.BlockSpec(blk, lambda k, i, c_ref: (k, i, 0))),
        out_shape=jax.ShapeDtypeStruct(other.shape, BF16),
        compiler_params=_cparams(("parallel", "parallel")),
    )(jnp.reshape(c, (1,)).astype(jnp.int32), grads, other)


def _chip_copies(t_ref, b_ref, send_sems, recv_sems):
    x, y, c = _position()
    return [pltpu.make_async_remote_copy(src_ref=t_ref.at[2 * chip[0] + chip[1]], dst_ref=b_ref.at[j],
                                         send_sem=send_sems.at[j], recv_sem=recv_sems.at[j],
                                         device_id=(*chip, c), device_id_type=MESH)
            for j, chip in enumerate(_other_chips(x, y))]


def _send_chip_sums(sums, tag):
    def body(t_ref, b_ref, send_sems, recv_sems):
        copies = _chip_copies(t_ref, b_ref, send_sems, recv_sems)
        for cp in copies:
            cp.start()
        for cp in copies:
            cp.wait()

    return pl.pallas_call(
        body, name=tag + "_send_chips", in_specs=[HBM_SPEC], out_specs=HBM_SPEC,
        out_shape=jax.ShapeDtypeStruct((3,) + sums.shape[1:], sums.dtype),
        scratch_shapes=[pltpu.SemaphoreType.DMA((3,)), pltpu.SemaphoreType.DMA((3,))],
    )(sums)


def _send_chip_sums_start(sums, tag):
    land = (3,) + sums.shape[1:]

    def body(t_ref, b_ref, send_sems, recv_sems, t_thru, b_thru, token):
        for cp in _chip_copies(t_ref, b_ref, send_sems, recv_sems):
            cp.start()
        token[...] = jnp.zeros_like(token)

    return pl.pallas_call(
        body, name=tag + "_send_chips_start",
        out_shape=(pltpu.SemaphoreType.DMA((3,)), pltpu.SemaphoreType.DMA((3,)), pltpu.HBM(sums.shape, sums.dtype),
                   pltpu.HBM(land, sums.dtype), jax.ShapeDtypeStruct((8, LANES), F32)),
        in_specs=(HBM_SPEC, HBM_SPEC), out_specs=(SEM_SPEC, SEM_SPEC, HBM_SPEC, HBM_SPEC, VMEM_SPEC),
        input_output_aliases={0: 2, 1: 3},
        compiler_params=pltpu.CompilerParams(has_side_effects=DATAFLOW),
    )(_hbm(sums), _hbm(lax.empty(land, sums.dtype)))


def _send_chip_sums_wait(send_sems, recv_sems, t_thru, b_thru, after, tag):
    def body(t_ref, b_ref, send_sems, recv_sems, after_ref, t_dead, got_ref):
        for cp in _chip_copies(t_ref, b_ref, send_sems, recv_sems):
            cp.wait_send()
            cp.wait_recv()

    return pl.pallas_call(
        body, name=tag + "_send_chips_wait",
        out_shape=(pltpu.HBM(t_thru.shape, t_thru.dtype), pltpu.HBM(b_thru.shape, b_thru.dtype)),
        in_specs=(HBM_SPEC, HBM_SPEC, SEM_SPEC, SEM_SPEC, ANY_SPEC), out_specs=(HBM_SPEC, HBM_SPEC),
        input_output_aliases={0: 0, 1: 1},
        compiler_params=pltpu.CompilerParams(has_side_effects=DATAFLOW),
    )(t_thru, b_thru, send_sems, recv_sems, after)[1]


def _final_sum(grads, other, recv, k, c, tag):
    rows = other.shape[1]
    rb = _rs_block(rows)

    def body(k_ref, c_ref, g_ref, a_ref, b_ref, o_ref):
        own = g_ref[0] + a_ref[0]
        o_ref[...] = ((own + b_ref[0].astype(F32)) + b_ref[1].astype(F32)) + b_ref[2].astype(F32)

    return pl.pallas_call(
        body, name=tag + "_final_sum",
        grid_spec=pltpu.PrefetchScalarGridSpec(
            num_scalar_prefetch=2, grid=(rows // rb,),
            in_specs=[pl.BlockSpec((1, rb, HALF_W), lambda i, k_ref, c_ref: (k_ref[0], i, c_ref[0])),
                      pl.BlockSpec((1, rb, HALF_W), lambda i, k_ref, c_ref: (k_ref[0], i, 0)),
                      pl.BlockSpec((3, rb, HALF_W), lambda i, k_ref, c_ref: (0, i, 0))],
            out_specs=pl.BlockSpec((rb, HALF_W), lambda i, k_ref, c_ref: (i, 0))),
        out_shape=jax.ShapeDtypeStruct((rows, HALF_W), F32),
        compiler_params=_cparams(("parallel",)),
    )(jnp.reshape(k, (1,)).astype(jnp.int32), jnp.reshape(c, (1,)).astype(jnp.int32), grads, other, recv)


def _join_halves(half, core, tag):
    def body(h_ref, o_ref, send_sem, recv_sem):
        x, y, c = _position()
        cp = pltpu.make_async_remote_copy(src_ref=h_ref, dst_ref=o_ref, send_sem=send_sem, recv_sem=recv_sem,
                                          device_id=(x, y, 1 - c), device_id_type=MESH)
        cp.start()
        cp.wait()

    other = pl.pallas_call(
        body, name=tag + "_join_cores", in_specs=[HBM_SPEC], out_specs=HBM_SPEC,
        out_shape=jax.ShapeDtypeStruct(half.shape, half.dtype),
        scratch_shapes=[pltpu.SemaphoreType.DMA, pltpu.SemaphoreType.DMA],
    )(half)
    first = core == 0
    return jnp.concatenate([jnp.where(first, half, other), jnp.where(first, other, half)], axis=1)


def _all_reduce_packet(packet):
    rows = packet.shape[0]

    def body(p_ref, o_ref, buf, send_sems, recv_sems):
        x, y, c = _position()
        me = 4 * x + 2 * y + c
        buf[me] = p_ref[...]

        def flip(v, bit):
            return 1 - v if bit else v

        for p in range(1, 8):
            peer = (flip(x, p & 4), flip(y, p & 2), flip(c, p & 1))
            pltpu.make_async_remote_copy(src_ref=p_ref, dst_ref=buf.at[me], send_sem=send_sems.at[p - 1],
                                         recv_sem=recv_sems.at[p - 1], device_id=peer, device_id_type=MESH).start()
        for p in range(1, 8):
            peer = (flip(x, p & 4), flip(y, p & 2), flip(c, p & 1))
            slot = 4 * peer[0] + 2 * peer[1] + peer[2]
            cp = pltpu.make_async_remote_copy(src_ref=p_ref, dst_ref=buf.at[slot], send_sem=send_sems.at[p - 1],
                                              recv_sem=recv_sems.at[p - 1], device_id=peer, device_id_type=MESH)
            cp.wait_recv()
            cp.wait_send()
        acc = buf[0]
        for dev in range(1, 8):
            acc = acc + buf[dev]
        o_ref[...] = acc

    return pl.pallas_call(
        body, name="all_reduce_packet", in_specs=[VMEM_SPEC], out_specs=VMEM_SPEC,
        out_shape=jax.ShapeDtypeStruct(packet.shape, F32),
        scratch_shapes=[pltpu.VMEM((8, rows, LANES), F32), pltpu.SemaphoreType.DMA((7,)),
                        pltpu.SemaphoreType.DMA((7,))],
    )(packet)


def _stack_range(name, n_stack, pack):
    if name.startswith('gqa'):
        return (0, 0) if pack == 0 else (0, n_stack)
    return (0, 1) if pack == 0 else (1, n_stack)


def _pack_members(pack):
    out = []
    for n, shape, ax in BIG:
        lo, hi = _stack_range(n, shape[0], pack)
        if hi > lo:
            out.append((n, (hi - lo,) + shape[1:], ax, (lo, hi)))
    return out


PACK_ROW_MULTIPLE = 512


def _pad_rows(parts, dtype):
    rows = sum(p.shape[0] for p in parts)
    pad = -rows % PACK_ROW_MULTIPLE
    return jnp.concatenate(parts + ([jnp.zeros((pad, PACK_W), dtype)] if pad else []), axis=0)


def _pack_blocks(blocks, dtype, pack):
    return _pad_rows([blocks[n][lo:hi].astype(dtype).reshape(-1, PACK_W)
                      for n, _, _, (lo, hi) in _pack_members(pack)], dtype)


def _unpack_blocks(packed, pack):
    out, off = {}, 0
    for n, shape, _, _ in _pack_members(pack):
        r = math.prod(shape) // PACK_W
        out[n] = packed[off:off + r].reshape(shape)
        off += r
    return out


def _unpack_gathered(gathered, pack, own=None, chip=None):
    blocks = [gathered[k] if own is None else jnp.where(chip == k, own, gathered[k]) for k in range(4)]
    per_chip = [_unpack_blocks(blocks[k], pack) for k in range(4)]
    return {n: jnp.concatenate([per_chip[k][n] for k in range(4)], axis=ax) for n, _, ax, _ in _pack_members(pack)}


def _pack_full(full, dtype, pack):
    chips = []
    for k in range(4):
        parts = []
        for n, shape, ax, _ in _pack_members(pack):
            blk = lax.slice_in_dim(full[n], k * shape[ax], (k + 1) * shape[ax], axis=ax)
            parts.append(blk.astype(dtype).reshape(-1, PACK_W))
        chips.append(_pad_rows(parts, dtype))
    return jnp.stack(chips, axis=0)


def _pack_small(vals, loss_row):
    rows = [loss_row.reshape(1, LANES)]
    for n, shape in SMALL:
        v = vals.get(n)
        v = jnp.zeros(shape, F32) if v is None else v
        rows.append(v.astype(F32).reshape(-1, LANES))
    packet = jnp.concatenate(rows, axis=0)
    return jnp.pad(packet, ((0, PACKET_ROWS - packet.shape[0]), (0, 0)))


def _unpack_small(packet):
    out, off = {}, 1
    for n, shape in SMALL:
        r = math.prod(shape) // LANES
        out[n] = packet[off:off + r].reshape(shape)
        off += r
    return packet[0, 0], out


_MLA = dict(R=1, dk=MLA_DK, dv=MLA_V, hb=2, bq=512, bk=512)
_MLA_FWD_BQ = 1024
_GQA = dict(R=GQA_HEADS // GQA_KV_HEADS, dk=GQA_HEAD_DIM, dv=GQA_HEAD_DIM, hb=2, bq=256, bk=512)


def _layer_params(layer, full, gains):
    pack = 0 if layer == 0 else 1
    i = layer // 2

    def mat(name):
        lo, _ = _stack_range(name, 4 if name.startswith('ffn') else 2, pack)
        return full[name][(layer if name.startswith('ffn') else i) - lo]

    p = dict(ffn_norm=gains['ffn_norm'][layer][None], ffn_w_in=mat('ffn_w_in'), ffn_w_out=mat('ffn_w_out'))
    if layer % 2 == 0:
        w_in = mat('w_in_ab')
        zeros = jnp.zeros((D_MODEL, 32), w_in.dtype)
        p['w_a'] = jnp.concatenate([w_in[:, :640], zeros, zeros, w_in[:, 640:IN_A], zeros], axis=1)
        p['w_b'] = w_in[:, IN_A:]
        p['w_uq'] = jnp.pad(mat('mla_w_uq'), ((0, 0), (0, 0), (0, MLA_DK - 96))).reshape(MLA_Q_RANK, -1)
        ukv = mat('mla_w_ukv')
        p['w_uk'] = jnp.pad(ukv[:, :, :MLA_NOPE], ((0, 0), (0, 0), (0, MLA_DK - MLA_NOPE))).reshape(MLA_KV_RANK, -1)
        p['w_uv'] = ukv[:, :, MLA_NOPE:].reshape(MLA_KV_RANK, -1)
        p['w_out'] = mat('w_out_ab')
        p['mix_norm'] = gains['mix_norm_ab'][i][None]
        p['q_norm'] = gains['mla_q_norm'][i][None]
        p['kv_norm'] = gains['mla_kv_norm'][i][None]
    else:
        p['w_q'], p['w_kv'], p['w_o'] = mat('gqa_w_q'), mat('gqa_w_kv'), mat('gqa_w_o')
        p['mix_norm'] = gains['mix_norm_c'][i][None]
        p['q_norm'] = jnp.tile(gains['gqa_q_norm'][i][None], (1, 2))
        p['k_norm'] = jnp.tile(gains['gqa_k_norm'][i][None], (1, 2))
    return p


def _even_fwd(x, p, cs, swap, tag):
    xn = _rms_fwd(x, p['mix_norm'], tag + "_norm")
    za = _mm(xn, p['w_a'], name=tag + "_in_a")
    zb = _mm(xn, p['w_b'], out_dtype=BF16, name=tag + "_in_b")
    cq, ckv, krr = _mla_prep(za, p['q_norm'], p['kv_norm'], cs, swap, tag + "_mla_prep")
    q_raw = _mm(cq, p['w_uq'], name=tag + "_uq")
    k_pad = _mm(ckv, p['w_uk'], name=tag + "_uk")
    v = _mm(ckv, p['w_uv'], out_dtype=BF16, name=tag + "_uv")
    qh, kh = _mla_qk(q_raw, k_pad, krr, cs, swap, tag + "_mla_qk")
    o_a, lse_a = _flash_fwd(qh, kh, v, name=tag + "_mla_attn", **dict(_MLA, bq=_MLA_FWD_BQ))
    og, lg = [], []
    for grp in range(DIL_GROUPS):
        o, l = _dil_fwd(zb, grp, f"{tag}_dil{grp}")
        og.append(o)
        lg.append(l)
    o_b, lt = _dil_combine(og, lg, tag + "_dil_merge")
    ocat = jnp.concatenate([o_a, o_b], axis=1)
    x1 = _mm(ocat, p['w_out'], add=x, name=tag + "_out")
    saved = dict(x=x, xn=xn, za=za, zb=zb, cq=cq, ckv=ckv, qh=qh, kh=kh, v=v, lse_a=lse_a, og=og, lg=lg, lt=lt,
                 ocat=ocat)
    return x1, saved


def _even_bwd(dx1, p, sv, cs, swap, seg64, tag):
    docat = _mm(dx1, p['w_out'], mode="nt", name=tag + "_out_dx")
    d_w_out = _mm(sv['ocat'], dx1, mode="tn", name=tag + "_out_dw")
    n_a = MLA_HEADS * MLA_V
    do_a = docat[:, :n_a].astype(BF16)
    res = _dil_combine_bwd(docat[:, n_a:], sv['og'], sv['lg'], sv['lt'], seg64, tag + "_dil_merge_bwd")
    dqs, dks, dvs = [], [], []
    for grp in range(DIL_GROUPS):
        dq, dk, dv = _dil_bwd(sv['zb'], res[grp], sv['lg'][grp], res[3 + grp], grp, f"{tag}_dil{grp}_bwd")
        dqs.append(dq)
        dks.append(dk)
        dvs.append(dv)
    dzb = jnp.concatenate(dqs + dks + dvs, axis=1).astype(BF16)
    dqh, dkh, dv = _flash_bwd(sv['qh'], sv['kh'], sv['v'], sv['ocat'][:, :n_a], do_a, sv['lse_a'],
                              name=tag + "_mla_attn_bwd", **_MLA)
    dq_raw, dkh, dkrr = _mla_qk_bwd(dqh, dkh, cs, swap, tag + "_mla_qk_bwd")
    dcq = _mm(dq_raw, p['w_uq'], mode="nt", name=tag + "_uq_dx")
    d_w_uq = _mm(sv['cq'], dq_raw, mode="tn", name=tag + "_uq_dw")
    dckv = _mm(dkh, p['w_uk'], mode="nt", name=tag + "_uk_dx")
    dckv = _mm(dv, p['w_uv'], mode="nt", add=dckv, name=tag + "_uv_dx")
    d_w_uk = _mm(sv['ckv'], dkh, mode="tn", name=tag + "_uk_dw")
    d_w_uv = _mm(sv['ckv'], dv, mode="tn", name=tag + "_uv_dw")
    dza, d_gq, d_gkv = _mla_prep_bwd(sv['za'], cs, dcq, dckv, dkrr, p['q_norm'], p['kv_norm'], swap,
                                     tag + "_mla_prep_bwd")
    dxn = _mm(dza, p['w_a'], mode="nt", name=tag + "_in_a_dx")
    dxn = _mm(dzb, p['w_b'], mode="nt", add=dxn, name=tag + "_in_b_dx")
    d_w_a = _mm(sv['xn'], dza, mode="tn", name=tag + "_in_a_dw")
    d_w_b = _mm(sv['xn'], dzb, mode="tn", name=tag + "_in_b_dw")
    dx, d_g = _rms_bwd(sv['x'], p['mix_norm'], dxn, dx1, tag + "_norm_bwd")
    d_w_in = jnp.concatenate([d_w_a[:, :640], d_w_a[:, 704:736], d_w_b], axis=1)
    d_uq = d_w_uq.reshape(MLA_Q_RANK, MLA_HEADS, MLA_DK)[:, :, :MLA_NOPE + MLA_ROPE]
    d_ukv = jnp.concatenate([d_w_uk.reshape(MLA_KV_RANK, MLA_HEADS, MLA_DK)[:, :, :MLA_NOPE],
                             d_w_uv.reshape(MLA_KV_RANK, MLA_HEADS, MLA_V)], axis=2)
    grads = dict(w_in_ab=d_w_in, mla_w_uq=d_uq, mla_w_ukv=d_ukv, w_out_ab=d_w_out, mix_norm_ab=d_g[0],
                 mla_q_norm=d_gq[0], mla_kv_norm=d_gkv[0])
    return dx, grads


def _odd_fwd(x, p, cs, seg64, swap, tag):
    xn = _rms_fwd(x, p['mix_norm'], tag + "_norm")
    q_raw = _mm(xn, p['w_q'], name=tag + "_q")
    kv_raw = _mm(xn, p['w_kv'], name=tag + "_kv")
    qh, kh, v = _gqa_prep(q_raw, kv_raw, cs, p['q_norm'], p['k_norm'], seg64, swap, tag + "_gqa_prep")
    o, lse = _flash_fwd(qh, kh, v, name=tag + "_gqa_attn", **_GQA)
    x1 = _mm(o, p['w_o'], add=x, name=tag + "_o")
    return x1, dict(x=x, xn=xn, q_raw=q_raw, kv_raw=kv_raw, qh=qh, kh=kh, v=v, o=o, lse=lse)


def _odd_bwd(dx1, p, sv, cs, seg64, swap, tag):
    do = _mm(dx1, p['w_o'], mode="nt", out_dtype=BF16, name=tag + "_o_dx")
    d_w_o = _mm(sv['o'], dx1, mode="tn", name=tag + "_o_dw")
    dqh, dkh, dv = _flash_bwd(sv['qh'], sv['kh'], sv['v'], sv['o'], do, sv['lse'], name=tag + "_gqa_attn_bwd",
                              **_GQA)
    dq_raw, dkv_raw, d_gq, d_gk = _gqa_prep_bwd(sv['q_raw'], sv['kv_raw'], cs, dqh, dkh, dv, p['q_norm'],
                                                p['k_norm'], seg64, swap, tag + "_gqa_prep_bwd")
    dxn = _mm(dq_raw, p['w_q'], mode="nt", name=tag + "_q_dx")
    dxn = _mm(dkv_raw, p['w_kv'], mode="nt", add=dxn, name=tag + "_kv_dx")
    d_w_q = _mm(sv['xn'], dq_raw, mode="tn", name=tag + "_q_dw")
    d_w_kv = _mm(sv['xn'], dkv_raw, mode="tn", name=tag + "_kv_dw")
    dx, d_g = _rms_bwd(sv['x'], p['mix_norm'], dxn, dx1, tag + "_norm_bwd")
    grads = dict(gqa_w_q=d_w_q, gqa_w_kv=d_w_kv, gqa_w_o=d_w_o, mix_norm_c=d_g[0],
                 gqa_q_norm=d_gq[0, :GQA_HEAD_DIM] + d_gq[0, GQA_HEAD_DIM:],
                 gqa_k_norm=d_gk[0, :GQA_HEAD_DIM] + d_gk[0, GQA_HEAD_DIM:])
    return dx, grads


def _ffn_fwd(x, p, tag):
    xn = _rms_fwd(x, p['ffn_norm'], tag + "_ffn_norm")
    h = _mm(xn, p['ffn_w_in'], out_dtype=BF16, name=tag + "_ffn_in")
    x2 = _mm(h, p['ffn_w_out'], gated=True, add=x, name=tag + "_ffn_out")
    return x2, dict(x=x, xn=xn, h=h)


def _ffn_bwd(dx2, p, sv, tag):
    d_gate, d_up = _mm(dx2, p['ffn_w_out'], mode="nt", gate_up=sv['h'], out_dtype=BF16, name=tag + "_ffn_out_dx")
    d_w_out = _mm(sv['h'], dx2, mode="tn", gated=True, name=tag + "_ffn_out_dw")
    dh = jnp.concatenate([d_gate, d_up], axis=1)
    d_w_in = _mm(sv['xn'], dh, mode="tn", name=tag + "_ffn_in_dw")
    dxn = _mm(dh, p['ffn_w_in'], mode="nt", name=tag + "_ffn_in_dx")
    dx, d_g = _rms_bwd(sv['x'], p['ffn_norm'], dxn, dx2, tag + "_ffn_norm_bwd")
    return dx, d_w_in, d_w_out, d_g[0]


EVEN_MATS = ('w_in_ab', 'mla_w_uq', 'mla_w_ukv', 'w_out_ab')
ODD_MATS = ('gqa_w_q', 'gqa_w_kv', 'gqa_w_o')
FFN_MATS = ('ffn_w_in', 'ffn_w_out')


def _schedule(x, target, gains, full_of_pack, rest_grads_ready):
    s = x.shape[0]
    cs_mla, cs_gqa = _rope_tables(s)
    swap, seg64 = _swap_matrix(), _seg_matrix(GQA_HEAD_DIM)
    params, saved, full = [], [], None
    for layer in range(4):
        tag = f"l{layer}"
        if layer < 2:
            full = full_of_pack(layer, x)
        p = _layer_params(layer, full, gains)
        if layer % 2 == 0:
            x, sv = _even_fwd(x, p, cs_mla, swap, tag)
        else:
            x, sv = _odd_fwd(x, p, cs_gqa, seg64, swap, tag)
        x, sv_f = _ffn_fwd(x, p, tag)
        params.append(p)
        saved.append((sv, sv_f))
    dx, loss_row, d_final = _loss_head(x, target, gains['final_norm'][None], "loss_head")

    per_layer, rest = {}, None
    for layer in reversed(range(4)):
        p, (sv, sv_f), tag = params[layer], saved[layer], f"l{layer}"
        if layer == 0:
            rest = {n: per_layer[2][n][None] for n in EVEN_MATS}
            rest.update({n: jnp.stack([per_layer[1][n], per_layer[3][n]], axis=0) for n in ODD_MATS})
            rest.update({n: jnp.stack([per_layer[l][n] for l in (1, 2, 3)], axis=0) for n in FFN_MATS})
            token = rest_grads_ready(rest)
            if token is not None:
                p = dict(p, ffn_w_out=p['ffn_w_out'] + token[0, 0].astype(p['ffn_w_out'].dtype))
        dx, d_ffn_in, d_ffn_out, d_ffn_g = _ffn_bwd(dx, p, sv_f, tag)
        if layer % 2 == 0:
            dx, g = _even_bwd(dx, p, sv, cs_mla, swap, seg64, tag)
        else:
            dx, g = _odd_bwd(dx, p, sv, cs_gqa, seg64, swap, tag)
        g.update(ffn_w_in=d_ffn_in, ffn_w_out=d_ffn_out, ffn_norm=d_ffn_g)
        per_layer[layer] = g

    first = {n: per_layer[0][n][None] for n in EVEN_MATS + FFN_MATS}
    small = {'final_norm': d_final[0], 'ffn_norm': jnp.stack([per_layer[l]['ffn_norm'] for l in range(4)], axis=0)}
    for n in ('mix_norm_ab', 'mla_q_norm', 'mla_kv_norm'):
        small[n] = jnp.stack([per_layer[0][n], per_layer[2][n]], axis=0)
    for n in ('mix_norm_c', 'gqa_q_norm', 'gqa_k_norm'):
        small[n] = jnp.stack([per_layer[1][n], per_layer[3][n]], axis=0)
    return loss_row, dx, first, rest, small


def _core_sums(grads, pack, core, tag):
    packed = _pack_full(grads, F32, pack)
    other = _sibling_swap_halves(packed, tag)
    return packed, other, _chip_sum(packed, other, core, tag)


def _finish_reduce_scatter(packed, other, recv, pack, chip, core, tag):
    return _unpack_blocks(_join_halves(_final_sum(packed, other, recv, chip, core, tag), core, tag), pack)


def _step(x, target, w, m, v):
    big_names = [n for n, _, _ in BIG]
    chip = 2 * lax.axis_index("x") + lax.axis_index("y")
    core = lax.axis_index("c")

    gains = {n: w[n] for n, _ in SMALL if n != 'mix_norm_c'}
    c_cols = w['mix_norm_c'].shape[1]
    own_c = lax.dynamic_update_slice(jnp.zeros((2, 4 * c_cols), F32), w['mix_norm_c'], (0, chip * c_cols))
    gains['mix_norm_c'] = _unpack_small(_all_reduce_packet(_pack_small(
        {'mix_norm_c': own_c * 0.5}, jnp.zeros((LANES,), F32))))[1]['mix_norm_c']

    gathered0 = _all_gather_weights(_pack_blocks(w, BF16, 0))
    packed1, gathered0 = lax.optimization_barrier((_pack_blocks(w, BF16, 1), gathered0))
    ag_send, ag_recv, p_thru, g_thru, ag_token = _gather_start(packed1, "gather_rest")
    gains['mix_norm_ab'] = gains['mix_norm_ab'] + ag_token[0, 0]
    full0 = _unpack_gathered(gathered0, 0)

    def full_of_pack(pack, after):
        if pack == 0:
            return full0
        landed = _gather_wait(ag_send, ag_recv, p_thru, g_thru, after, "gather_rest")
        return _unpack_gathered(landed, 1, own=packed1, chip=chip)

    rs = {}

    def rest_grads_ready(rest):
        rs['packed'], rs['other'], sums = _core_sums(rest, 1, core, "grad_rest")
        rs['send'], rs['recv'], rs['t'], rs['b'], token = _send_chip_sums_start(sums, "grad_rest")
        return token

    loss_row, dx, first, rest, small = _schedule(x[0], target[0], gains, full_of_pack, rest_grads_ready)
    recv1 = _send_chip_sums_wait(rs['send'], rs['recv'], rs['t'], rs['b'], dx, "grad_rest")
    g_rest = _finish_reduce_scatter(rs['packed'], rs['other'], recv1, 1, chip, core, "grad_rest")
    packed0, other0, sums0 = _core_sums(first, 0, core, "grad_first")
    g_first = _finish_reduce_scatter(packed0, other0, _send_chip_sums(sums0, "grad_first"), 0, chip, core,
                                     "grad_first")
    g_blocks = {n: (jnp.concatenate([g_first[n], g_rest[n]], axis=0) if n in g_first else g_rest[n])
                for n in big_names}

    loss, g_small = _unpack_small(_all_reduce_packet(_pack_small(small, loss_row[0])))
    g_small['mix_norm_c'] = lax.dynamic_slice(g_small['mix_norm_c'], (0, chip * c_cols), (2, c_cols))

    out_g, out_d, out_m, out_v = {}, {}, {}, {}
    for n in big_names:
        shape = w[n].shape
        cols = shape[-1]
        d_, m_, v_ = _adamw(w[n].reshape(-1, cols), g_blocks[n].reshape(-1, cols), m[n].reshape(-1, cols),
                            v[n].reshape(-1, cols), "adamw_" + n)
        out_g[n], out_d[n], out_m[n], out_v[n] = g_blocks[n], d_.reshape(shape), m_.reshape(shape), v_.reshape(shape)
    for n, _ in SMALL:
        shape = w[n].shape
        as2d = (lambda t: t.reshape(1, -1)) if len(shape) == 1 else (lambda t: t)
        d_, m_, v_ = _adamw(as2d(w[n]), as2d(g_small[n]), as2d(m[n]), as2d(v[n]), "adamw_" + n)
        out_g[n], out_d[n], out_m[n], out_v[n] = g_small[n], d_.reshape(shape), m_.reshape(shape), v_.reshape(shape)
    return (loss, dx[None], *[out_g[n] for n in WEIGHTS], *[out_d[n] for n in WEIGHTS],
            *[out_m[n] for n in WEIGHTS], *[out_v[n] for n in WEIGHTS])


def kernel(x, mix_norm_ab, w_in_ab, mla_q_norm, mla_kv_norm, mla_w_uq, mla_w_ukv, w_out_ab, mix_norm_c, gqa_w_q, gqa_w_kv, gqa_q_norm, gqa_k_norm, gqa_w_o, ffn_norm, ffn_w_in, ffn_w_out, final_norm, loss_target, m_mix_norm_ab, m_w_in_ab, m_mla_q_norm, m_mla_kv_norm, m_mla_w_uq, m_mla_w_ukv, m_w_out_ab, m_mix_norm_c, m_gqa_w_q, m_gqa_w_kv, m_gqa_q_norm, m_gqa_k_norm, m_gqa_w_o, m_ffn_norm, m_ffn_w_in, m_ffn_w_out, m_final_norm, v_mix_norm_ab, v_w_in_ab, v_mla_q_norm, v_mla_kv_norm, v_mla_w_uq, v_mla_w_ukv, v_w_out_ab, v_mix_norm_c, v_gqa_w_q, v_gqa_w_kv, v_gqa_q_norm, v_gqa_k_norm, v_gqa_w_o, v_ffn_norm, v_ffn_w_in, v_ffn_w_out, v_final_norm):
    w = dict(zip(WEIGHTS, (mix_norm_ab, w_in_ab, mla_q_norm, mla_kv_norm, mla_w_uq, mla_w_ukv, w_out_ab, mix_norm_c,
                           gqa_w_q, gqa_w_kv, gqa_q_norm, gqa_k_norm, gqa_w_o, ffn_norm, ffn_w_in, ffn_w_out,
                           final_norm)))
    m = dict(zip(WEIGHTS, (m_mix_norm_ab, m_w_in_ab, m_mla_q_norm, m_mla_kv_norm, m_mla_w_uq, m_mla_w_ukv,
                           m_w_out_ab, m_mix_norm_c, m_gqa_w_q, m_gqa_w_kv, m_gqa_q_norm, m_gqa_k_norm, m_gqa_w_o,
                           m_ffn_norm, m_ffn_w_in, m_ffn_w_out, m_final_norm)))
    v = dict(zip(WEIGHTS, (v_mix_norm_ab, v_w_in_ab, v_mla_q_norm, v_mla_kv_norm, v_mla_w_uq, v_mla_w_ukv,
                           v_w_out_ab, v_mix_norm_c, v_gqa_w_q, v_gqa_w_kv, v_gqa_q_norm, v_gqa_k_norm, v_gqa_w_o,
                           v_ffn_norm, v_ffn_w_in, v_ffn_w_out, v_final_norm)))
    return _step(x, loss_target, w, m, v)
```

```python
import math

import numpy as np
import jax
import jax.numpy as jnp
from jax import lax
from jax.experimental import pallas as pl
from jax.experimental.pallas import tpu as pltpu

F32 = jnp.float32
BF16 = jnp.bfloat16
MESH = pl.DeviceIdType.MESH

VMEM_LIMIT_BYTES = 56 * 1024 * 1024
LANES = 128

D_MODEL = 1024
NORM_EPS = 1e-6
ROPE_THETA = 10000.0
NEG_INF = -1e30
GRID_W = 64

MLA_HEADS, MLA_Q_RANK, MLA_KV_RANK, MLA_NOPE, MLA_ROPE, MLA_V = 8, 384, 256, 64, 32, 64
MLA_DK = 128
DIL_PAIRS = ((128, 1), (512, 4), (2048, 16))
DIL_HALF, DIL_SLOTS, DIL_GROUPS, DIL_HEAD_DIM = 64, 4, 3, 64
DIL_HEADS = DIL_SLOTS * DIL_GROUPS
DIL_W = DIL_SLOTS * DIL_HEAD_DIM
GQA_HEADS, GQA_KV_HEADS, GQA_HEAD_DIM = 16, 4, 64
FFN_HIDDEN = 2816
IN_A = MLA_Q_RANK + MLA_KV_RANK + MLA_ROPE
IN_A_PAD = 768
IN_B = 3 * DIL_HEADS * DIL_HEAD_DIM

ADAM_LR, ADAM_B1, ADAM_B2, ADAM_EPS, ADAM_WD, ADAM_STEP = 0.001, 0.9, 0.999, 1e-08, 0.01, 10

LOG2E, LN2 = math.log2(math.e), math.log(2.0)
MLA_SCALE = (MLA_NOPE + MLA_ROPE) ** -0.5
GQA_SCALE = GQA_HEAD_DIM ** -0.5

WEIGHTS = ['mix_norm_ab', 'w_in_ab', 'mla_q_norm', 'mla_kv_norm', 'mla_w_uq', 'mla_w_ukv', 'w_out_ab', 'mix_norm_c',
           'gqa_w_q', 'gqa_w_kv', 'gqa_q_norm', 'gqa_k_norm', 'gqa_w_o', 'ffn_norm', 'ffn_w_in', 'ffn_w_out',
           'final_norm']
BIG = (('w_in_ab', (2, 1024, 744), 2), ('mla_w_uq', (2, 96, 8, 96), 1), ('mla_w_ukv', (2, 64, 8, 128), 1),
       ('w_out_ab', (2, 768, 256), 2), ('gqa_w_q', (2, 256, 1024), 1), ('gqa_w_kv', (2, 256, 512), 1),
       ('gqa_w_o', (2, 256, 1024), 1), ('ffn_w_in', (4, 1024, 1408), 2), ('ffn_w_out', (4, 704, 1024), 1))
PACK_W = 1024
SMALL = (('mix_norm_ab', (2, 1024)), ('mla_q_norm', (2, 384)), ('mla_kv_norm', (2, 256)), ('gqa_q_norm', (2, 64)),
         ('gqa_k_norm', (2, 64)), ('ffn_norm', (4, 1024)), ('final_norm', (1024,)), ('mix_norm_c', (2, 1024)))
PACKET_ROWS = 88


def _cparams(sem=None):
    return pltpu.CompilerParams(dimension_semantics=sem, vmem_limit_bytes=VMEM_LIMIT_BYTES)


def _pick(n, pref, mult=LANES):
    if n <= pref:
        return n
    for d in range(pref - pref % mult, 0, -mult):
        if n % d == 0:
            return d
    return n


_DIMS = {"nn": (((1,), (0,)), ((), ())), "nt": (((1,), (1,)), ((), ())), "tn": (((0,), (0,)), ((), ()))}


def _sigmoid(x):
    return 0.5 * (1.0 + jnp.tanh(0.5 * x))


def _silu_mul(gate, up):
    gate, up = gate.astype(F32), up.astype(F32)
    return gate * _sigmoid(gate) * up


def _silu_mul_bwd(gate, up, da):
    gate, up = gate.astype(F32), up.astype(F32)
    sig = _sigmoid(gate)
    silu = gate * sig
    return da * up * (sig + silu * (1.0 - sig)), da * silu


def _mm(a, b, *, mode="nn", add=None, out_dtype=F32, gated=False, gate_up=None, b_part=None, name="mm"):
    if mode == "nn":
        (m, k), (k2, n) = a.shape, b.shape
    elif mode == "nt":
        (m, k), (n, k2) = a.shape, b.shape
    else:
        (k, m), (k2, n) = a.shape, b.shape
    if gated:
        k, m = (k // 2, m) if mode == "nn" else (k, m // 2)
    if b_part is not None:
        assert mode == "nt" and k2 == k * b_part[1]
        k2 = k
    assert k == k2, (a.shape, b.shape, mode)
    if mode == "tn":
        deep = a.dtype == BF16 and b.dtype == BF16
        bm, bn, bk = _pick(m, 1408), _pick(n, 1024), _pick(k, 2048 if deep else 1024, 16)
    else:
        bm, bn, bk = _pick(m, 512, 16), _pick(n, 1408), _pick(k, 2816)
    nk = k // bk
    assert m % bm == 0 and n % bn == 0 and k % bk == 0
    has_add, has_gu = add is not None, gate_up is not None
    assert not (has_add and has_gu) and not (gated and mode == "nt") and not (has_gu and mode != "nt")
    n_in = 2 + int(gated) + int(has_add) + 2 * int(has_gu)

    def body(*refs):
        a_val = _silu_mul(refs[0][...], refs[1][...]) if gated else refs[0][...]
        b_ref = refs[1 + int(gated)]
        part = lax.dot_general(a_val.astype(BF16), b_ref[...].astype(BF16), _DIMS[mode],
                               preferred_element_type=F32)

        def finish(r):
            if has_gu:
                d_gate, d_up = _silu_mul_bwd(refs[2][...], refs[3][...], r)
                refs[n_in][...] = d_gate.astype(refs[n_in].dtype)
                refs[n_in + 1][...] = d_up.astype(refs[n_in + 1].dtype)
                return
            if has_add:
                r = r + refs[n_in - 1][...]
            refs[n_in][...] = r.astype(refs[n_in].dtype)

        if nk == 1:
            finish(part)
        else:
            acc_ref = refs[-1]
            kk = pl.program_id(2)

            @pl.when(kk == 0)
            def _():
                acc_ref[...] = part

            @pl.when(kk > 0)
            def _():
                acc_ref[...] += part

            @pl.when(kk == nk - 1)
            def _():
                finish(acc_ref[...])

    a_bytes, b_bytes = a.size * a.dtype.itemsize, b.size * b.dtype.itemsize
    n_outer = nk == 1 and (n // bn) * a_bytes + b_bytes < a_bytes + (m // bm) * b_bytes

    def at(f):
        return (lambda j, i, kk: f(i, j, kk)) if n_outer else f

    if mode == "nn":
        a_specs = [pl.BlockSpec((bm, bk), at(lambda i, j, kk, o=o: (i, kk + o))) for o in ((0, nk) if gated else (0,))]
        b_spec = pl.BlockSpec((bk, bn), at(lambda i, j, kk: (kk, j)))
    elif mode == "nt":
        a_specs = [pl.BlockSpec((bm, bk), at(lambda i, j, kk: (i, kk)))]
        b_off = 0 if b_part is None else b_part[0] * nk
        b_spec = pl.BlockSpec((bn, bk), at(lambda i, j, kk: (j, kk + b_off)))
    else:
        a_specs = [pl.BlockSpec((bk, bm), at(lambda i, j, kk, o=o: (kk, i + o)))
                   for o in ((0, m // bm) if gated else (0,))]
        b_spec = pl.BlockSpec((bk, bn), at(lambda i, j, kk: (kk, j)))
    o_spec = pl.BlockSpec((bm, bn), at(lambda i, j, kk: (i, j)))
    in_specs, args = a_specs + [b_spec], [a] * len(a_specs) + [b]
    if has_add:
        in_specs, args = in_specs + [o_spec], args + [add]
    if has_gu:
        in_specs += [o_spec, pl.BlockSpec((bm, bn), at(lambda i, j, kk: (i, j + n // bn)))]
        args += [gate_up, gate_up]
    out = jax.ShapeDtypeStruct((m, n), out_dtype)
    grid = (n // bn, m // bm, nk) if n_outer else (m // bm, n // bn, nk)
    return pl.pallas_call(
        body, name=name, grid=grid, in_specs=in_specs, out_specs=[o_spec, o_spec] if has_gu else o_spec,
        out_shape=[out, out] if has_gu else out,
        scratch_shapes=[pltpu.VMEM((bm, bn), F32)] if nk > 1 else [],
        compiler_params=_cparams(("parallel", "parallel", "arbitrary")),
    )(*args)


def _rows_call(fn, rows, consts, out_rows, out_accs=(), *, bs=256, name):
    s = rows[0].shape[0]
    bs = min(bs, s)
    assert s % bs == 0
    nr, nc, no, na = len(rows), len(consts), len(out_rows), len(out_accs)

    def body(*refs):
        vals = [r[...] for r in refs[:nr + nc]]
        outs = refs[nr + nc:]
        res = fn(*vals)
        if not isinstance(res, (tuple, list)):
            res = (res,)
        assert len(res) == no + na, (len(res), no, na)
        for r, v in zip(outs[:no], res[:no]):
            r[...] = v.astype(r.dtype)
        if na:
            i = pl.program_id(0)
            for r, v in zip(outs[no:], res[no:]):
                @pl.when(i == 0)
                def _(r=r, v=v):
                    r[...] = v

                @pl.when(i > 0)
                def _(r=r, v=v):
                    r[...] += v

    in_specs = [pl.BlockSpec((bs, a.shape[1]), lambda i: (i, 0)) for a in rows]
    in_specs += [pl.BlockSpec(c.shape, lambda i: (0, 0)) for c in consts]
    out_specs = [pl.BlockSpec((bs, c), lambda i: (i, 0)) for c, _ in out_rows]
    out_specs += [pl.BlockSpec(tuple(sh), lambda i: (0, 0)) for sh in out_accs]
    out_shape = [jax.ShapeDtypeStruct((s, c), dt) for c, dt in out_rows]
    out_shape += [jax.ShapeDtypeStruct(tuple(sh), F32) for sh in out_accs]
    res = pl.pallas_call(
        body, name=name, grid=(s // bs,), in_specs=in_specs, out_specs=out_specs, out_shape=out_shape,
        compiler_params=_cparams(("arbitrary",) if na else ("parallel",)),
    )(*rows, *consts)
    return res


def _rms(x, g):
    return x * lax.rsqrt(jnp.mean(x * x, axis=-1, keepdims=True) + NORM_EPS) * g


def _rms_bwd_math(x, g, dy):
    r = lax.rsqrt(jnp.mean(x * x, axis=-1, keepdims=True) + NORM_EPS)
    u = dy * g
    dx = r * u - x * (r * r * r) * jnp.mean(u * x, axis=-1, keepdims=True)
    dg = jnp.sum(dy * x * r, axis=0, keepdims=True)
    return dx, dg


NORM_ROWS = 512


def _rms_fwd(x, g, name):
    return _rows_call(lambda xv, gv: _rms(xv, gv), [x], [g], [(x.shape[1], BF16)], bs=NORM_ROWS, name=name)[0]


def _rms_bwd(x, g, dy, dres, name):
    def fn(xv, dyv, dresv, gv):
        dx, dg = _rms_bwd_math(xv, gv, dyv.astype(F32))
        return dx + dresv, dg
    return _rows_call(fn, [x, dy, dres], [g], [(x.shape[1], F32)], [(1, x.shape[1])], bs=NORM_ROWS, name=name)


def _chunkdot(x, m):
    outs = [jnp.dot(x[:, c:c + LANES], m, precision=lax.Precision.HIGHEST, preferred_element_type=F32)
            for c in range(0, x.shape[1], LANES)]
    return outs[0] if len(outs) == 1 else jnp.concatenate(outs, axis=1)


def _lanes(t, width):
    n = width // LANES
    return t if n == 1 else jnp.concatenate([t] * n, axis=1)


def _rope(x, cs, swap):
    w = x.shape[1]
    return x * _lanes(cs[:, :LANES], w) + _chunkdot(x, swap) * _lanes(cs[:, LANES:], w)


def _rope_t(dy, cs, swap):
    w = dy.shape[1]
    return dy * _lanes(cs[:, :LANES], w) + _chunkdot(dy * _lanes(cs[:, LANES:], w), swap)


def _swap_matrix():
    m = np.zeros((LANES, LANES), np.float32)
    for j in range(LANES):
        src = j + 16 if (j % 32) < 16 else j - 16
        m[src, j] = 1.0
    return jnp.asarray(m)


def _seg_matrix(seg):
    idx = np.arange(LANES) // seg
    return jnp.asarray((idx[:, None] == idx[None, :]).astype(np.float32))


def _rope_tables(s):
    pos = jnp.arange(s)

    def angles(p, dim):
        freqs = ROPE_THETA ** (-jnp.arange(0, dim, 2, dtype=F32) / dim)
        ang = p.astype(F32)[:, None] * freqs[None, :]
        return jnp.cos(ang), jnp.sin(ang)

    cos_t, sin_t = angles(pos, MLA_ROPE)
    one, zero = jnp.ones((s, 64), F32), jnp.zeros((s, 64), F32)
    mla = jnp.concatenate([one, cos_t, cos_t, one[:, :32], zero, -sin_t, sin_t, zero[:, :32]], axis=1)
    cos_r, sin_r = angles(pos // GRID_W, GQA_HEAD_DIM // 2)
    cos_c, sin_c = angles(pos % GRID_W, GQA_HEAD_DIM // 2)
    c64 = jnp.concatenate([cos_r, cos_r, cos_c, cos_c], axis=1)
    s64 = jnp.concatenate([-sin_r, sin_r, -sin_c, sin_c], axis=1)
    gqa = jnp.concatenate([c64, c64, s64, s64], axis=1)
    return mla, gqa


def _stack_heads(ref, heads, d, dtype=None):
    parts = [ref[:, hd * d:(hd + 1) * d] for hd in heads]
    out = parts[0] if len(parts) == 1 else jnp.concatenate(parts, axis=0)
    return out if dtype is None else out.astype(dtype)


def _fill_v_ones(v_ref, va_ref, hb, dv):
    @pl.when(pl.program_id(1) == 0)
    def _():
        ones = jnp.ones((v_ref.shape[0], dv), BF16)
        for h in range(hb):
            va_ref[:, 2 * h * dv:(2 * h + 1) * dv] = v_ref[:, h * dv:(h + 1) * dv]
            va_ref[:, (2 * h + 1) * dv:(2 * h + 2) * dv] = ones


def _flash_fwd(q, k, v, *, R, dk, dv, hb, bq, bk, name):
    s = q.shape[0]
    g = k.shape[1] // dk
    ng = g // hb
    bq, bk = min(bq, s), min(bk, s)
    nq, nkb = s // bq, s // bk
    rb = R * bq

    def body(q_ref, k_ref, v_ref, o_ref, lse_ref, va_ref):
        _fill_v_ones(v_ref, va_ref, hb, dv)
        head_sets = [[h * R + r for r in range(R)] for h in range(hb)]
        qss = [_stack_heads(q_ref, heads, dk) for heads in head_sets]

        def step(jj, carry):
            carry = list(carry)
            rows = [pl.ds(pl.multiple_of((jj * unroll + u) * bk, bk), bk) for u in range(unroll)]
            scs = [[lax.dot_general(qss[h], k_ref[rows[u], h * dk:(h + 1) * dk], _DIMS["nt"],
                                    preferred_element_type=F32) for h in range(hb)] for u in range(unroll)]
            for u in range(unroll):
                m2s = [jnp.maximum(carry[h][0], jnp.max(scs[u][h], axis=1, keepdims=True)) for h in range(hb)]
                ps = [jnp.exp2(scs[u][h] - m2s[h]).astype(BF16) for h in range(hb)]
                pvs = [jnp.dot(ps[h], va_ref[rows[u], 2 * h * dv:2 * (h + 1) * dv], preferred_element_type=F32)
                       for h in range(hb)]
                carry = [(m2s[h], jnp.exp2(carry[h][0] - m2s[h]) * carry[h][1] + pvs[h]) for h in range(hb)]
            return tuple(carry)

        unroll = 2 if nkb % 2 == 0 else 1
        init = tuple((jnp.full((rb, 1), NEG_INF, F32), jnp.zeros((rb, 2 * dv), F32)) for _ in range(hb))
        final = lax.fori_loop(0, nkb // unroll, step, init)
        for h, heads in enumerate(head_sets):
            m, acc = final[h]
            l = acc[:, dv:dv + 1]
            o = acc[:, :dv] / l
            lse = m + jnp.log2(l)
            for r, hd in enumerate(heads):
                o_ref[:, hd * dv:(hd + 1) * dv] = o[r * bq:(r + 1) * bq].astype(o_ref.dtype)
                lse_ref[0, :, hd:hd + 1] = lse[r * bq:(r + 1) * bq]

    return pl.pallas_call(
        body, name=name, grid=(ng, nq),
        in_specs=[pl.BlockSpec((bq, hb * R * dk), lambda gi, i: (i, gi)),
                  pl.BlockSpec((s, hb * dk), lambda gi, i: (0, gi)),
                  pl.BlockSpec((s, hb * dv), lambda gi, i: (0, gi))],
        out_specs=[pl.BlockSpec((bq, hb * R * dv), lambda gi, i: (i, gi)),
                   pl.BlockSpec((1, bq, hb * R), lambda gi, i: (gi, i, 0))],
        out_shape=[jax.ShapeDtypeStruct((s, g * R * dv), BF16), jax.ShapeDtypeStruct((ng, s, hb * R), F32)],
        scratch_shapes=[pltpu.VMEM((s, 2 * hb * dv), BF16)],
        compiler_params=_cparams(("parallel", "arbitrary")),
    )(q, k, v)


def _flash_bwd(q, k, v, o, do, lse, *, R, dk, dv, hb, bq, bk, name, unroll=1):
    s = q.shape[0]
    g = k.shape[1] // dk
    ng = g // hb
    bq, bk = min(bq, s), min(bk, s)
    nq, nkb = s // bq, s // bk
    rb = R * bq

    def body(q_ref, k_ref, v_ref, o_ref, do_ref, lse_ref, dq_ref, dkt_ref, dvt_ref, va_ref):
        @pl.when(pl.program_id(1) == 0)
        def _():
            dkt_ref[...] = jnp.zeros(dkt_ref.shape, F32)
            dvt_ref[...] = jnp.zeros(dvt_ref.shape, F32)

        _fill_v_ones(v_ref, va_ref, hb, dv)
        lane = lax.broadcasted_iota(jnp.int32, (rb, dv), 1)
        head_sets = [[h * R + r for r in range(R)] for h in range(hb)]
        q_t = jnp.transpose(q_ref[...].astype(F32)).astype(BF16)
        do_t = jnp.transpose(do_ref[...].astype(F32)).astype(BF16)

        def stack_t(t, heads, d):
            parts = [t[hd * d:(hd + 1) * d] for hd in heads]
            return parts[0] if len(parts) == 1 else jnp.concatenate(parts, axis=1)

        qss, qts, dots, dosas, lcols = [], [], [], [], []
        for heads in head_sets:
            dos = _stack_heads(do_ref, heads, dv, BF16)
            delta = jnp.sum(dos.astype(F32) * _stack_heads(o_ref, heads, dv, F32), axis=1, keepdims=True)
            hi = delta.astype(BF16).astype(F32)
            lo = delta - hi
            qss.append(_stack_heads(q_ref, heads, dk))
            qts.append(stack_t(q_t, heads, dk))
            dots.append(stack_t(do_t, heads, dv))
            dosas.append(jnp.concatenate(
                [dos, jnp.where(lane == 0, -hi, jnp.where(lane == 1, -lo, 0.0)).astype(BF16)], axis=1))
            cols = [lse_ref[0, :, hd:hd + 1] for hd in heads]
            lcols.append(cols[0] if R == 1 else jnp.concatenate(cols, axis=0))

        def step(jj, dqs):
            hs = range(hb)
            for u in range(unroll):
                j = jj * unroll + u
                r0 = pl.multiple_of(j * bk, bk)
                kjs = [k_ref[pl.ds(r0, bk), h * dk:(h + 1) * dk] for h in hs]
                vas = [va_ref[pl.ds(r0, bk), 2 * h * dv:2 * (h + 1) * dv] for h in hs]
                ss = [lax.dot_general(qss[h], kjs[h], _DIMS["nt"], preferred_element_type=F32) for h in hs]
                dps = [lax.dot_general(dosas[h], vas[h], _DIMS["nt"], preferred_element_type=F32) for h in hs]
                ps = [jnp.exp2(ss[h] - lcols[h]) for h in hs]
                pbs = [ps[h].astype(BF16) for h in hs]
                dss = [(ps[h] * dps[h]).astype(BF16) for h in hs]
                for h in hs:
                    dvt_ref[0, j, h * dv:(h + 1) * dv, :] += jnp.dot(dots[h], pbs[h], preferred_element_type=F32)
                for h in hs:
                    dkt_ref[0, j, h * dk:(h + 1) * dk, :] += jnp.dot(qts[h], dss[h], preferred_element_type=F32)
                dqs = tuple(dqs[h] + jnp.dot(dss[h], kjs[h], preferred_element_type=F32) for h in hs)
            return dqs

        assert nkb % unroll == 0
        dqs = lax.fori_loop(0, nkb // unroll, step, tuple(jnp.zeros((rb, dk), F32) for _ in range(hb)))
        for h, heads in enumerate(head_sets):
            for r, hd in enumerate(heads):
                dq_ref[:, hd * dk:(hd + 1) * dk] = dqs[h][r * bq:(r + 1) * bq]

    qspec = pl.BlockSpec((bq, hb * R * dk), lambda gi, i: (i, gi))
    ospec = pl.BlockSpec((bq, hb * R * dv), lambda gi, i: (i, gi))
    kspec = pl.BlockSpec((s, hb * dk), lambda gi, i: (0, gi))
    vspec = pl.BlockSpec((s, hb * dv), lambda gi, i: (0, gi))
    dq, dkt, dvt = pl.pallas_call(
        body, name=name, grid=(ng, nq),
        in_specs=[qspec, kspec, vspec, ospec, ospec, pl.BlockSpec((1, bq, hb * R), lambda gi, i: (gi, i, 0))],
        out_specs=[qspec, pl.BlockSpec((1, nkb, hb * dk, bk), lambda gi, i: (gi, 0, 0, 0)),
                   pl.BlockSpec((1, nkb, hb * dv, bk), lambda gi, i: (gi, 0, 0, 0))],
        out_shape=[jax.ShapeDtypeStruct((s, g * R * dk), F32), jax.ShapeDtypeStruct((ng, nkb, hb * dk, bk), F32),
                   jax.ShapeDtypeStruct((ng, nkb, hb * dv, bk), F32)],
        scratch_shapes=[pltpu.VMEM((s, 2 * hb * dv), BF16)],
        compiler_params=_cparams(("parallel", "arbitrary")),
    )(q, k, v, o, do, lse)
    return dq, _keys_first(dkt, name + "_dk"), _keys_first(dvt, name + "_dv")


def _keys_first(t, name):
    ng, nkb, f, bk = t.shape

    def body(t_ref, o_ref):
        for j in range(nkb):
            o_ref[j * bk:(j + 1) * bk, :] = jnp.transpose(t_ref[0, j])

    return pl.pallas_call(
        body, name=name, grid=(ng,),
        in_specs=[pl.BlockSpec((1, nkb, f, bk), lambda gi: (gi, 0, 0, 0))],
        out_specs=pl.BlockSpec((nkb * bk, f), lambda gi: (0, gi)),
        out_shape=jax.ShapeDtypeStruct((nkb * bk, ng * f), t.dtype),
        compiler_params=_cparams(("parallel",)),
    )(t)


DIL_T = 1024
DIL_P = DIL_HALF
DIL_NCOL = IN_B // DIL_W


DIL_BATCH = 4


def _alibi_slope(head):
    return float(2.0 ** (-8.0 * (head + 1) / DIL_HEADS))


def _slot(sl_i):
    return slice(sl_i * DIL_HEAD_DIM, (sl_i + 1) * DIL_HEAD_DIM)


def _halo_specs(d, col, s, t):
    h = DIL_P * d
    per, last = t // h, s // h - 1
    return [pl.BlockSpec((h, DIL_W), lambda c: (jnp.maximum(c * per - 1, 0), col)),
            pl.BlockSpec((t, DIL_W), lambda c: (c, col)),
            pl.BlockSpec((h, DIL_W), lambda c: (jnp.minimum((c + 1) * per, last), col))]


def _staging(rows):
    return tuple(pltpu.VMEM((rows, LANES), F32) for _ in range(DIL_W // LANES))


def _stage(buf, refs):
    off = 0
    for r in refs:
        val = r[...].astype(F32)
        for j in range(DIL_W // LANES):
            buf[j][off:off + r.shape[0], :] = val[:, j * LANES:(j + 1) * LANES]
        off += r.shape[0]


def _unstage(buf, ref):
    ref[...] = jnp.concatenate([half[...] for half in buf], axis=1).astype(ref.dtype)


def _sub_tiles(d, t):
    return [(b * DIL_P * d + r, b * DIL_P) for b in range(t // (DIL_P * d)) for r in range(d)]


def _rows(start, size, d):
    return pl.ds(start, size, stride=d) if d > 1 else pl.ds(start, size)


def _strided(buf, start, size, d):
    return jnp.concatenate([half[_rows(start, size, d), :] for half in buf], axis=1)


def _put_strided(buf, start, d, val):
    for j in range(DIL_W // LANES):
        buf[j][_rows(start, val.shape[0], d), :] = val[:, j * LANES:(j + 1) * LANES]


def _band(u0, length, d, queries_wide):
    if queries_wide:
        shape = (3 * DIL_P, DIL_P)
        wide = u0 - DIL_P + lax.broadcasted_iota(jnp.int32, shape, 0)
        narrow = u0 + lax.broadcasted_iota(jnp.int32, shape, 1)
    else:
        shape = (DIL_P, 3 * DIL_P)
        narrow = u0 + lax.broadcasted_iota(jnp.int32, shape, 0)
        wide = u0 - DIL_P + lax.broadcasted_iota(jnp.int32, shape, 1)
    rel = jnp.abs(wide - narrow)
    valid = (rel <= DIL_HALF) & (wide >= 0) & (wide < length)
    return valid, rel.astype(F32) * float(d)


def _dil_fwd(zb, grp, name):
    s = zb.shape[0]
    d = DIL_PAIRS[grp][1]
    t = min(DIL_T, s)
    h = DIL_P * d
    scale = DIL_HEAD_DIM ** -0.5

    def body(q_ref, kp, kc, kn, vp, vc, vn, o_ref, lse_ref, qbuf, kbuf, vbuf, obuf, lbuf):
        _stage(qbuf, (q_ref,))
        _stage(kbuf, (kp, kc, kn))
        _stage(vbuf, (vp, vc, vn))
        u_step = pl.program_id(0) * (t // d)
        tiles = _sub_tiles(d, t)
        for g0 in range(0, len(tiles), DIL_BATCH):
            batch = tiles[g0:g0 + DIL_BATCH]
            masks = [_band(u_step + u, s // d, d, False) for _, u in batch]
            qs = [_strided(qbuf, row, DIL_P, d).astype(BF16) for row, _ in batch]
            ks = [_strided(kbuf, row, 3 * DIL_P, d).astype(BF16) for row, _ in batch]
            vs = [_strided(vbuf, row, 3 * DIL_P, d).astype(BF16) for row, _ in batch]
            chains = [(i, sl_i) for i in range(len(batch)) for sl_i in range(DIL_SLOTS)]
            scs = [lax.dot_general(qs[i][:, _slot(sl_i)], ks[i][:, _slot(sl_i)], _DIMS["nt"],
                                   preferred_element_type=F32) for i, sl_i in chains]
            scs = [jnp.where(masks[i][0], sc * scale - _alibi_slope(grp * DIL_SLOTS + sl_i) * masks[i][1], NEG_INF)
                   for (i, sl_i), sc in zip(chains, scs)]
            ms = [jnp.max(sc, axis=1, keepdims=True) for sc in scs]
            es = [jnp.exp(sc - m) for sc, m in zip(scs, ms)]
            dens = [jnp.sum(e, axis=1, keepdims=True) for e in es]
            outs = [jnp.dot((e / den).astype(BF16), vs[i][:, _slot(sl_i)], preferred_element_type=F32)
                    for (i, sl_i), e, den in zip(chains, es, dens)]
            lses = [jnp.broadcast_to(m + jnp.log(den), (DIL_P, DIL_HEAD_DIM)) for m, den in zip(ms, dens)]
            for i, (row, _) in enumerate(batch):
                pick = slice(i * DIL_SLOTS, (i + 1) * DIL_SLOTS)
                _put_strided(obuf, row, d, jnp.concatenate(outs[pick], axis=1))
                _put_strided(lbuf, row, d, jnp.concatenate(lses[pick], axis=1))
        _unstage(obuf, o_ref)
        _unstage(lbuf, lse_ref)

    own = pl.BlockSpec((t, DIL_W), lambda c: (c, 0))
    return pl.pallas_call(
        body, name=name, grid=(s // t,),
        in_specs=[pl.BlockSpec((t, DIL_W), lambda c: (c, grp))] + _halo_specs(d, 3 + grp, s, t)
        + _halo_specs(d, 6 + grp, s, t),
        out_specs=[own, own], out_shape=[jax.ShapeDtypeStruct((s, DIL_W), F32)] * 2,
        scratch_shapes=[_staging(t), _staging(t + 2 * h), _staging(t + 2 * h), _staging(t), _staging(t)],
        compiler_params=_cparams(("parallel",)),
    )(zb, zb, zb, zb, zb, zb, zb)


def _dil_bwd(zb, do, lse, dl, grp, name):
    s = zb.shape[0]
    d = DIL_PAIRS[grp][1]
    t = min(DIL_T, s)
    h = DIL_P * d
    scale = DIL_HEAD_DIM ** -0.5

    def chain_grads(qs, ks, vs, dos, lses, dls, masks):
        chains = [(i, sl_i) for i in range(len(qs)) for sl_i in range(DIL_SLOTS)]
        scs = [lax.dot_general(qs[i][:, _slot(sl_i)], ks[i][:, _slot(sl_i)], _DIMS["nt"],
                               preferred_element_type=F32) for i, sl_i in chains]
        dps = [lax.dot_general(dos[i][:, _slot(sl_i)], vs[i][:, _slot(sl_i)], _DIMS["nt"],
                               preferred_element_type=F32) for i, sl_i in chains]
        ps = [jnp.exp(jnp.where(masks[i][0], sc * scale - _alibi_slope(grp * DIL_SLOTS + sl_i) * masks[i][1],
                                NEG_INF) - lses[i][:, sl_i * DIL_HEAD_DIM:sl_i * DIL_HEAD_DIM + 1])
              for (i, sl_i), sc in zip(chains, scs)]
        dss = [(p * (dp - dls[i][:, sl_i * DIL_HEAD_DIM:sl_i * DIL_HEAD_DIM + 1]) * scale).astype(BF16)
               for (i, sl_i), p, dp in zip(chains, ps, dps)]
        return chains, ps, dss

    def dq_body(q_ref, kp, kc, kn, vp, vc, vn, do_ref, lse_ref, dl_ref, dq_ref, qbuf, kbuf, vbuf, dobuf, lsebuf,
                dlbuf, obuf):
        _stage(qbuf, (q_ref,))
        _stage(dobuf, (do_ref,))
        _stage(lsebuf, (lse_ref,))
        _stage(dlbuf, (dl_ref,))
        _stage(kbuf, (kp, kc, kn))
        _stage(vbuf, (vp, vc, vn))
        u_step = pl.program_id(0) * (t // d)
        tiles = _sub_tiles(d, t)
        for g0 in range(0, len(tiles), DIL_BATCH):
            batch = tiles[g0:g0 + DIL_BATCH]
            masks = [_band(u_step + u, s // d, d, False) for _, u in batch]
            narrow = [[_strided(b, row, DIL_P, d) for row, _ in batch] for b in (qbuf, dobuf, lsebuf, dlbuf)]
            ks = [_strided(kbuf, row, 3 * DIL_P, d).astype(BF16) for row, _ in batch]
            vs = [_strided(vbuf, row, 3 * DIL_P, d).astype(BF16) for row, _ in batch]
            chains, _, dss = chain_grads([a.astype(BF16) for a in narrow[0]], ks, vs,
                                         [a.astype(BF16) for a in narrow[1]], narrow[2], narrow[3], masks)
            outs = [jnp.dot(ds, ks[i][:, _slot(sl_i)], preferred_element_type=F32)
                    for (i, sl_i), ds in zip(chains, dss)]
            for i, (row, _) in enumerate(batch):
                _put_strided(obuf, row, d, jnp.concatenate(outs[i * DIL_SLOTS:(i + 1) * DIL_SLOTS], axis=1))
        _unstage(obuf, dq_ref)

    def dkv_body(k_ref, v_ref, qp, qc, qn, dop, doc, don, lp, lc, ln, dlp, dlc, dln, dk_ref, dv_ref,
                 kbuf, vbuf, qbuf, dobuf, lsebuf, dlbuf, dkbuf, dvbuf):
        _stage(kbuf, (k_ref,))
        _stage(vbuf, (v_ref,))
        _stage(qbuf, (qp, qc, qn))
        _stage(dobuf, (dop, doc, don))
        _stage(lsebuf, (lp, lc, ln))
        _stage(dlbuf, (dlp, dlc, dln))
        u_step = pl.program_id(0) * (t // d)
        tiles = _sub_tiles(d, t)
        for g0 in range(0, len(tiles), DIL_BATCH):
            batch = tiles[g0:g0 + DIL_BATCH]
            masks = [_band(u_step + u, s // d, d, True) for _, u in batch]
            ks = [_strided(kbuf, row, DIL_P, d).astype(BF16) for row, _ in batch]
            vs = [_strided(vbuf, row, DIL_P, d).astype(BF16) for row, _ in batch]
            wide_ = [[_strided(b, row, 3 * DIL_P, d) for row, _ in batch] for b in (qbuf, dobuf, lsebuf, dlbuf)]
            qs, dos = [a.astype(BF16) for a in wide_[0]], [a.astype(BF16) for a in wide_[1]]
            chains, ps, dss = chain_grads(qs, ks, vs, dos, wide_[2], wide_[3], masks)
            dvs = [lax.dot_general(p.astype(BF16), dos[i][:, _slot(sl_i)], _DIMS["tn"], preferred_element_type=F32)
                   for (i, sl_i), p in zip(chains, ps)]
            dks = [lax.dot_general(ds, qs[i][:, _slot(sl_i)], _DIMS["tn"], preferred_element_type=F32)
                   for (i, sl_i), ds in zip(chains, dss)]
            for i, (row, _) in enumerate(batch):
                pick = slice(i * DIL_SLOTS, (i + 1) * DIL_SLOTS)
                _put_strided(dkbuf, row, d, jnp.concatenate(dks[pick], axis=1))
                _put_strided(dvbuf, row, d, jnp.concatenate(dvs[pick], axis=1))
        _unstage(dkbuf, dk_ref)
        _unstage(dvbuf, dv_ref)

    def zcur(col):
        return pl.BlockSpec((t, DIL_W), lambda c: (c, col))

    own = pl.BlockSpec((t, DIL_W), lambda c: (c, 0))
    out = jax.ShapeDtypeStruct((s, DIL_W), F32)
    tile, wide = _staging(t), _staging(t + 2 * h)
    dq = pl.pallas_call(
        dq_body, name=name + "_dq", grid=(s // t,),
        in_specs=[zcur(grp)] + _halo_specs(d, 3 + grp, s, t) + _halo_specs(d, 6 + grp, s, t) + [own, own, own],
        out_specs=own, out_shape=out, scratch_shapes=[tile, wide, wide, tile, tile, tile, tile],
        compiler_params=_cparams(("parallel",)),
    )(zb, zb, zb, zb, zb, zb, zb, do, lse, dl)
    own3 = _halo_specs(d, 0, s, t)
    dk, dv = pl.pallas_call(
        dkv_body, name=name + "_dkv", grid=(s // t,),
        in_specs=[zcur(3 + grp), zcur(6 + grp)] + _halo_specs(d, grp, s, t) + own3 + own3 + own3,
        out_specs=[own, own], out_shape=[out, out],
        scratch_shapes=[tile, tile, wide, wide, wide, wide, tile, tile],
        compiler_params=_cparams(("parallel",)),
    )(zb, zb, zb, zb, zb, do, do, do, lse, lse, lse, dl, dl, dl)
    return dq, dk, dv


def _dil_combine(os_, ls_, name):
    def fn(o0, o1, o2, l0, l1, l2):
        m = jnp.maximum(jnp.maximum(l0, l1), l2)
        e0, e1, e2 = jnp.exp(l0 - m), jnp.exp(l1 - m), jnp.exp(l2 - m)
        den = e0 + e1 + e2
        comb = (e0 / den) * o0 + (e1 / den) * o1 + (e2 / den) * o2
        return comb, m + jnp.log(den)
    return _rows_call(fn, list(os_) + list(ls_), [], [(DIL_W, BF16), (DIL_W, F32)], name=name)


def _dil_combine_bwd(dcomb, os_, ls_, lt, seg64, name):
    def fn(dc, o0, o1, o2, l0, l1, l2, ltv, seg):
        w = [jnp.exp(l - ltv) for l in (l0, l1, l2)]
        comb = w[0] * o0 + w[1] * o1 + w[2] * o2
        t = _chunkdot(dc * comb, seg)
        return [wg * dc for wg in w] + [wg * t for wg in w]
    return _rows_call(fn, [dcomb] + list(os_) + list(ls_) + [lt], [seg64],
                      [(DIL_W, BF16)] * 3 + [(DIL_W, F32)] * 3, name=name)


def _mla_prep(za, gq, gkv, cs, swap, name):
    def fn(z, csv, gqv, gkvv, sw):
        return (_rms(z[:, :MLA_Q_RANK], gqv), _rms(z[:, MLA_Q_RANK:640], gkvv), _rope(z[:, 640:], csv, sw))
    return _rows_call(fn, [za, cs], [gq, gkv, swap], [(MLA_Q_RANK, BF16), (MLA_KV_RANK, BF16), (LANES, F32)],
                      name=name)


def _mla_prep_bwd(za, cs, dcq, dckv, dkr, gq, gkv, swap, name):
    def fn(z, csv, dcqv, dckvv, dkrv, gqv, gkvv, sw):
        d1, dg1 = _rms_bwd_math(z[:, :MLA_Q_RANK], gqv, dcqv)
        d2, dg2 = _rms_bwd_math(z[:, MLA_Q_RANK:640], gkvv, dckvv)
        d3 = _rope_t(dkrv, csv, sw)
        return jnp.concatenate([d1, d2, d3], axis=1), dg1, dg2
    return _rows_call(fn, [za, cs, dcq, dckv, dkr], [gq, gkv, swap], [(IN_A_PAD, BF16)],
                      [(1, MLA_Q_RANK), (1, MLA_KV_RANK)], name=name)


def _mla_qk(q_raw, k_pad, krr, cs, swap, name):
    w = MLA_HEADS * MLA_DK

    def fn(qv, kv, krv, csv, sw):
        return _rope(qv, csv, sw) * (MLA_SCALE * LOG2E), kv + _lanes(krv, w)
    return _rows_call(fn, [q_raw, k_pad, krr, cs], [swap], [(w, BF16), (w, BF16)], name=name)


def _mla_qk_bwd(dqh, dkh, cs, swap, name):
    w = MLA_HEADS * MLA_DK

    def fn(dq, dk, csv, sw):
        dk = dk * LN2
        acc = dk[:, :LANES]
        for h in range(1, MLA_HEADS):
            acc = acc + dk[:, h * LANES:(h + 1) * LANES]
        lane = lax.broadcasted_iota(jnp.int32, acc.shape, 1)
        acc = jnp.where((lane >= MLA_NOPE) & (lane < MLA_NOPE + MLA_ROPE), acc, 0.0)
        return _rope_t(dq * MLA_SCALE, csv, sw), dk, acc
    return _rows_call(fn, [dqh, dkh, cs], [swap], [(w, BF16), (w, BF16), (LANES, F32)], name=name)


def _head_norm(t, g2, seg):
    r = lax.rsqrt(_chunkdot(t * t, seg) * (1.0 / GQA_HEAD_DIM) + NORM_EPS)
    return t * r * _lanes(g2, t.shape[1]), r


def _head_norm_bwd(t, g2, seg, dn):
    w = t.shape[1]
    r = lax.rsqrt(_chunkdot(t * t, seg) * (1.0 / GQA_HEAD_DIM) + NORM_EPS)
    u = dn * _lanes(g2, w)
    dt = r * u - t * (r * r * r) * (_chunkdot(u * t, seg) * (1.0 / GQA_HEAD_DIM))
    dgw = jnp.sum(dn * t * r, axis=0, keepdims=True)
    dg = dgw[:, :LANES]
    for c in range(LANES, w, LANES):
        dg = dg + dgw[:, c:c + LANES]
    return dt, dg


def _gqa_prep(q_raw, kv_raw, cs, gq2, gk2, seg, swap, name):
    kw = GQA_KV_HEADS * GQA_HEAD_DIM

    def fn(qv, kvv, csv, gqv, gkv, sg, sw):
        qn, _ = _head_norm(qv, gqv, sg)
        kn, _ = _head_norm(kvv[:, :kw], gkv, sg)
        return _rope(qn, csv, sw) * (GQA_SCALE * LOG2E), _rope(kn, csv, sw), kvv[:, kw:]
    return _rows_call(fn, [q_raw, kv_raw, cs], [gq2, gk2, seg, swap],
                      [(GQA_HEADS * GQA_HEAD_DIM, BF16), (kw, BF16), (kw, BF16)], name=name)


def _gqa_prep_bwd(q_raw, kv_raw, cs, dqh, dkh, dv, gq2, gk2, seg, swap, name):
    kw = GQA_KV_HEADS * GQA_HEAD_DIM

    def fn(qv, kvv, csv, dq, dk, dvv, gqv, gkv, sg, sw):
        dqr, dgq = _head_norm_bwd(qv, gqv, sg, _rope_t(dq * GQA_SCALE, csv, sw))
        dkr, dgk = _head_norm_bwd(kvv[:, :kw], gkv, sg, _rope_t(dk * LN2, csv, sw))
        return dqr, jnp.concatenate([dkr, dvv], axis=1), dgq, dgk
    return _rows_call(fn, [q_raw, kv_raw, cs, dqh, dkh, dv], [gq2, gk2, seg, swap],
                      [(GQA_HEADS * GQA_HEAD_DIM, BF16), (2 * kw, BF16)], [(1, LANES), (1, LANES)], name=name)


def _loss_head(x, target, g, name):
    dm = x.shape[1]

    def fn(xv, tv, gv):
        err = _rms(xv, gv) - tv
        loss = 0.5 * jnp.sum(err * err) / dm
        dx, dg = _rms_bwd_math(xv, gv, err * (1.0 / dm))
        return dx, jnp.zeros((1, LANES), F32) + loss, dg
    return _rows_call(fn, [x, target], [g], [(dm, F32)], [(1, LANES), (1, dm)], name=name)


def _adamw(w, g, m, v, name):
    def fn(wv, gv, mv, vv):
        m2 = ADAM_B1 * mv + (1.0 - ADAM_B1) * gv
        v2 = ADAM_B2 * vv + (1.0 - ADAM_B2) * (gv * gv)
        m_hat = m2 / (1.0 - ADAM_B1 ** ADAM_STEP)
        v_hat = v2 / (1.0 - ADAM_B2 ** ADAM_STEP)
        return -ADAM_LR * (m_hat / (jnp.sqrt(v_hat) + ADAM_EPS) + ADAM_WD * wv), m2, v2
    c = w.shape[1]
    return _rows_call(fn, [w, g, m, v], [], [(c, F32)] * 3, bs=_pick(w.shape[0], 256, 8), name=name)


HBM_SPEC = pl.BlockSpec(memory_space=pltpu.HBM)
VMEM_SPEC = pl.BlockSpec(memory_space=pltpu.VMEM)


def _position():
    return lax.axis_index("x"), lax.axis_index("y"), lax.axis_index("c")


def _other_chips(x, y):
    return [(1 - x, y), (x, 1 - y), (1 - x, 1 - y)]


HALF_W = PACK_W // 2


def _cols(c):
    return pl.ds(pl.multiple_of(c * HALF_W, HALF_W), HALF_W)


def _all_gather_weights(packed):
    rows = packed.shape[0]

    def body(p_ref, g_ref, send_sems, recv_sems, local_sem):
        x, y, c = _position()
        chips = _other_chips(x, y)

        def half(chip, hc):
            return g_ref.at[2 * chip[0] + chip[1], :, _cols(hc)]

        def copy(j, src, dst, to):
            return pltpu.make_async_remote_copy(src_ref=src, dst_ref=dst, send_sem=send_sems.at[j],
                                                recv_sem=recv_sems.at[j], device_id=to, device_id_type=MESH)

        mine = pltpu.make_async_copy(p_ref, g_ref.at[2 * x + y], local_sem)
        mine.start()
        first = [copy(j, p_ref.at[:, _cols(c)], half((x, y), c), (*chip, c)) for j, chip in enumerate(chips)]
        for cp in first:
            cp.start()
        passed = [copy(3 + j, half(chip, c), half(chip, c), (x, y, 1 - c)) for j, chip in enumerate(chips)]
        for j, chip in enumerate(chips):
            copy(j, half(chip, c), half(chip, c), (x, y, c)).wait_recv()
            passed[j].start()
        for j, chip in enumerate(chips):
            copy(3 + j, half(chip, 1 - c), half(chip, 1 - c), (x, y, c)).wait_recv()
        for cp in first + passed:
            cp.wait_send()
        mine.wait()

    return pl.pallas_call(
        body, name="all_gather_weights", in_specs=[HBM_SPEC], out_specs=HBM_SPEC,
        out_shape=jax.ShapeDtypeStruct((4, rows, packed.shape[1]), packed.dtype),
        scratch_shapes=[pltpu.SemaphoreType.DMA((6,)), pltpu.SemaphoreType.DMA((6,)), pltpu.SemaphoreType.DMA],
    )(packed)


SEM_SPEC = pl.BlockSpec(memory_space=pltpu.SEMAPHORE)
ANY_SPEC = pl.BlockSpec(memory_space=pl.ANY)
DATAFLOW = pltpu.SideEffectType.DATAFLOW_SIDE_EFFECTING


def _hbm(a):
    return pltpu.with_memory_space_constraint(a, pltpu.HBM)


def _gather_start(packed, tag):
    rows = packed.shape[0]

    def body(p_ref, g_ref, send_sems, recv_sems, p_thru, g_thru, token):
        x, y, c = _position()
        for j, chip in enumerate(_other_chips(x, y)):
            for s in range(2):
                pltpu.make_async_remote_copy(
                    src_ref=p_ref.at[:, _cols(c)], dst_ref=g_ref.at[2 * x + y, :, _cols(c)],
                    send_sem=send_sems.at[2 * j + s], recv_sem=recv_sems.at[2 * j + s],
                    device_id=(*chip, 1 - c if s else c), device_id_type=MESH).start()
        token[...] = jnp.zeros_like(token)

    return pl.pallas_call(
        body, name=tag + "_start",
        out_shape=(pltpu.SemaphoreType.DMA((6,)), pltpu.SemaphoreType.DMA((6,)), pltpu.HBM(packed.shape, packed.dtype),
                   pltpu.HBM((4, rows, PACK_W), packed.dtype), jax.ShapeDtypeStruct((8, LANES), F32)),
        in_specs=(HBM_SPEC, HBM_SPEC), out_specs=(SEM_SPEC, SEM_SPEC, HBM_SPEC, HBM_SPEC, VMEM_SPEC),
        input_output_aliases={0: 2, 1: 3},
        compiler_params=pltpu.CompilerParams(has_side_effects=DATAFLOW),
    )(_hbm(packed), _hbm(lax.empty((4, rows, PACK_W), packed.dtype)))


def _gather_wait(send_sems, recv_sems, p_thru, g_thru, after, tag):
    def body(p_ref, g_ref, send_sems, recv_sems, after_ref, p_dead, got_ref):
        x, y, c = _position()
        for j, chip in enumerate(_other_chips(x, y)):
            for s in range(2):
                cp = pltpu.make_async_remote_copy(
                    src_ref=p_ref.at[:, _cols(c)], dst_ref=g_ref.at[2 * chip[0] + chip[1], :, _cols(1 - c if s else c)],
                    send_sem=send_sems.at[2 * j + s], recv_sem=recv_sems.at[2 * j + s],
                    device_id=(x, y, c), device_id_type=MESH)
                cp.wait_send()
                cp.wait_recv()

    return pl.pallas_call(
        body, name=tag + "_wait",
        out_shape=(pltpu.HBM(p_thru.shape, p_thru.dtype), pltpu.HBM(g_thru.shape, g_thru.dtype)),
        in_specs=(HBM_SPEC, HBM_SPEC, SEM_SPEC, SEM_SPEC, ANY_SPEC), out_specs=(HBM_SPEC, HBM_SPEC),
        input_output_aliases={0: 0, 1: 1},
        compiler_params=pltpu.CompilerParams(has_side_effects=DATAFLOW),
    )(p_thru, g_thru, send_sems, recv_sems, after)[1]


def _sibling_swap_halves(grads, tag):
    rows = grads.shape[1]

    def body(g_ref, a_ref, send_sem, recv_sem):
        x, y, c = _position()
        cp = pltpu.make_async_remote_copy(src_ref=g_ref.at[:, :, _cols(1 - c)], dst_ref=a_ref,
                                          send_sem=send_sem, recv_sem=recv_sem, device_id=(x, y, 1 - c),
                                          device_id_type=MESH)
        cp.start()
        cp.wait()

    return pl.pallas_call(
        body, name=tag + "_swap_cores", in_specs=[HBM_SPEC], out_specs=HBM_SPEC,
        out_shape=jax.ShapeDtypeStruct((4, rows, HALF_W), grads.dtype),
        scratch_shapes=[pltpu.SemaphoreType.DMA, pltpu.SemaphoreType.DMA],
    )(grads)


def _rs_block(rows):
    return max(d for d in range(16, 1601, 16) if rows % d == 0)


def _chip_sum(grads, other, c, tag):
    rows = other.shape[1]
    rb = _rs_block(rows)

    def body(c_ref, g_ref, a_ref, o_ref):
        o_ref[...] = (g_ref[...] + a_ref[...]).astype(o_ref.dtype)

    blk = (1, rb, HALF_W)
    return pl.pallas_call(
        body, name=tag + "_chip_sum",
        grid_spec=pltpu.PrefetchScalarGridSpec(
            num_scalar_prefetch=1, grid=(4, rows // rb),
            in_specs=[pl.BlockSpec(blk, lambda k, i, c_ref: (k, i, c_ref[0])),
                      pl.BlockSpec(blk, lambda k, i, c_ref: (k, i, 0))],
            out_specs=pl.BlockSpec(blk, lambda k, i, c_ref: (k, i, 0))),
        out_shape=jax.ShapeDtypeStruct(other.shape, BF16),
        compiler_params=_cparams(("parallel", "parallel")),
    )(jnp.reshape(c, (1,)).astype(jnp.int32), grads, other)


def _chip_copies(t_ref, b_ref, send_sems, recv_sems):
    x, y, c = _position()
    return [pltpu.make_async_remote_copy(src_ref=t_ref.at[2 * chip[0] + chip[1]], dst_ref=b_ref.at[j],
                                         send_sem=send_sems.at[j], recv_sem=recv_sems.at[j],
                                         device_id=(*chip, c), device_id_type=MESH)
            for j, chip in enumerate(_other_chips(x, y))]


def _send_chip_sums(sums, tag):
    def body(t_ref, b_ref, send_sems, recv_sems):
        copies = _chip_copies(t_ref, b_ref, send_sems, recv_sems)
        for cp in copies:
            cp.start()
        for cp in copies:
            cp.wait()

    return pl.pallas_call(
        body, name=tag + "_send_chips", in_specs=[HBM_SPEC], out_specs=HBM_SPEC,
        out_shape=jax.ShapeDtypeStruct((3,) + sums.shape[1:], sums.dtype),
        scratch_shapes=[pltpu.SemaphoreType.DMA((3,)), pltpu.SemaphoreType.DMA((3,))],
    )(sums)


def _send_chip_sums_start(sums, tag):
    land = (3,) + sums.shape[1:]

    def body(t_ref, b_ref, send_sems, recv_sems, t_thru, b_thru, token):
        for cp in _chip_copies(t_ref, b_ref, send_sems, recv_sems):
            cp.start()
        token[...] = jnp.zeros_like(token)

    return pl.pallas_call(
        body, name=tag + "_send_chips_start",
        out_shape=(pltpu.SemaphoreType.DMA((3,)), pltpu.SemaphoreType.DMA((3,)), pltpu.HBM(sums.shape, sums.dtype),
                   pltpu.HBM(land, sums.dtype), jax.ShapeDtypeStruct((8, LANES), F32)),
        in_specs=(HBM_SPEC, HBM_SPEC), out_specs=(SEM_SPEC, SEM_SPEC, HBM_SPEC, HBM_SPEC, VMEM_SPEC),
        input_output_aliases={0: 2, 1: 3},
        compiler_params=pltpu.CompilerParams(has_side_effects=DATAFLOW),
    )(_hbm(sums), _hbm(lax.empty(land, sums.dtype)))


def _send_chip_sums_wait(send_sems, recv_sems, t_thru, b_thru, after, tag):
    def body(t_ref, b_ref, send_sems, recv_sems, after_ref, t_dead, got_ref):
        for cp in _chip_copies(t_ref, b_ref, send_sems, recv_sems):
            cp.wait_send()
            cp.wait_recv()

    return pl.pallas_call(
        body, name=tag + "_send_chips_wait",
        out_shape=(pltpu.HBM(t_thru.shape, t_thru.dtype), pltpu.HBM(b_thru.shape, b_thru.dtype)),
        in_specs=(HBM_SPEC, HBM_SPEC, SEM_SPEC, SEM_SPEC, ANY_SPEC), out_specs=(HBM_SPEC, HBM_SPEC),
        input_output_aliases={0: 0, 1: 1},
        compiler_params=pltpu.CompilerParams(has_side_effects=DATAFLOW),
    )(t_thru, b_thru, send_sems, recv_sems, after)[1]


def _final_sum(grads, other, recv, k, c, tag):
    rows = other.shape[1]
    rb = _rs_block(rows)

    def body(k_ref, c_ref, g_ref, a_ref, b_ref, o_ref):
        own = g_ref[0] + a_ref[0]
        o_ref[...] = ((own + b_ref[0].astype(F32)) + b_ref[1].astype(F32)) + b_ref[2].astype(F32)

    return pl.pallas_call(
        body, name=tag + "_final_sum",
        grid_spec=pltpu.PrefetchScalarGridSpec(
            num_scalar_prefetch=2, grid=(rows // rb,),
            in_specs=[pl.BlockSpec((1, rb, HALF_W), lambda i, k_ref, c_ref: (k_ref[0], i, c_ref[0])),
                      pl.BlockSpec((1, rb, HALF_W), lambda i, k_ref, c_ref: (k_ref[0], i, 0)),
                      pl.BlockSpec((3, rb, HALF_W), lambda i, k_ref, c_ref: (0, i, 0))],
            out_specs=pl.BlockSpec((rb, HALF_W), lambda i, k_ref, c_ref: (i, 0))),
        out_shape=jax.ShapeDtypeStruct((rows, HALF_W), F32),
        compiler_params=_cparams(("parallel",)),
    )(jnp.reshape(k, (1,)).astype(jnp.int32), jnp.reshape(c, (1,)).astype(jnp.int32), grads, other, recv)


def _join_halves(half, core, tag):
    def body(h_ref, o_ref, send_sem, recv_sem):
        x, y, c = _position()
        cp = pltpu.make_async_remote_copy(src_ref=h_ref, dst_ref=o_ref, send_sem=send_sem, recv_sem=recv_sem,
                                          device_id=(x, y, 1 - c), device_id_type=MESH)
        cp.start()
        cp.wait()

    other = pl.pallas_call(
        body, name=tag + "_join_cores", in_specs=[HBM_SPEC], out_specs=HBM_SPEC,
        out_shape=jax.ShapeDtypeStruct(half.shape, half.dtype),
        scratch_shapes=[pltpu.SemaphoreType.DMA, pltpu.SemaphoreType.DMA],
    )(half)
    first = core == 0
    return jnp.concatenate([jnp.where(first, half, other), jnp.where(first, other, half)], axis=1)


def _all_reduce_packet(packet):
    rows = packet.shape[0]

    def body(p_ref, o_ref, buf, send_sems, recv_sems):
        x, y, c = _position()
        me = 4 * x + 2 * y + c
        buf[me] = p_ref[...]

        def flip(v, bit):
            return 1 - v if bit else v

        for p in range(1, 8):
            peer = (flip(x, p & 4), flip(y, p & 2), flip(c, p & 1))
            pltpu.make_async_remote_copy(src_ref=p_ref, dst_ref=buf.at[me], send_sem=send_sems.at[p - 1],
                                         recv_sem=recv_sems.at[p - 1], device_id=peer, device_id_type=MESH).start()
        for p in range(1, 8):
            peer = (flip(x, p & 4), flip(y, p & 2), flip(c, p & 1))
            slot = 4 * peer[0] + 2 * peer[1] + peer[2]
            cp = pltpu.make_async_remote_copy(src_ref=p_ref, dst_ref=buf.at[slot], send_sem=send_sems.at[p - 1],
                                              recv_sem=recv_sems.at[p - 1], device_id=peer, device_id_type=MESH)
            cp.wait_recv()
            cp.wait_send()
        acc = buf[0]
        for dev in range(1, 8):
            acc = acc + buf[dev]
        o_ref[...] = acc

    return pl.pallas_call(
        body, name="all_reduce_packet", in_specs=[VMEM_SPEC], out_specs=VMEM_SPEC,
        out_shape=jax.ShapeDtypeStruct(packet.shape, F32),
        scratch_shapes=[pltpu.VMEM((8, rows, LANES), F32), pltpu.SemaphoreType.DMA((7,)),
                        pltpu.SemaphoreType.DMA((7,))],
    )(packet)


def _stack_range(name, n_stack, pack):
    if name.startswith('gqa'):
        return (0, 0) if pack == 0 else (0, n_stack)
    return (0, 1) if pack == 0 else (1, n_stack)


def _pack_members(pack):
    out = []
    for n, shape, ax in BIG:
        lo, hi = _stack_range(n, shape[0], pack)
        if hi > lo:
            out.append((n, (hi - lo,) + shape[1:], ax, (lo, hi)))
    return out


PACK_ROW_MULTIPLE = 512


def _pad_rows(parts, dtype):
    rows = sum(p.shape[0] for p in parts)
    pad = -rows % PACK_ROW_MULTIPLE
    return jnp.concatenate(parts + ([jnp.zeros((pad, PACK_W), dtype)] if pad else []), axis=0)


def _pack_blocks(blocks, dtype, pack):
    return _pad_rows([blocks[n][lo:hi].astype(dtype).reshape(-1, PACK_W)
                      for n, _, _, (lo, hi) in _pack_members(pack)], dtype)


def _unpack_blocks(packed, pack):
    out, off = {}, 0
    for n, shape, _, _ in _pack_members(pack):
        r = math.prod(shape) // PACK_W
        out[n] = packed[off:off + r].reshape(shape)
        off += r
    return out


def _unpack_gathered(gathered, pack, own=None, chip=None):
    blocks = [gathered[k] if own is None else jnp.where(chip == k, own, gathered[k]) for k in range(4)]
    per_chip = [_unpack_blocks(blocks[k], pack) for k in range(4)]
    return {n: jnp.concatenate([per_chip[k][n] for k in range(4)], axis=ax) for n, _, ax, _ in _pack_members(pack)}


def _pack_full(full, dtype, pack):
    chips = []
    for k in range(4):
        parts = []
        for n, shape, ax, _ in _pack_members(pack):
            blk = lax.slice_in_dim(full[n], k * shape[ax], (k + 1) * shape[ax], axis=ax)
            parts.append(blk.astype(dtype).reshape(-1, PACK_W))
        chips.append(_pad_rows(parts, dtype))
    return jnp.stack(chips, axis=0)


def _pack_small(vals, loss_row):
    rows = [loss_row.reshape(1, LANES)]
    for n, shape in SMALL:
        v = vals.get(n)
        v = jnp.zeros(shape, F32) if v is None else v
        rows.append(v.astype(F32).reshape(-1, LANES))
    packet = jnp.concatenate(rows, axis=0)
    return jnp.pad(packet, ((0, PACKET_ROWS - packet.shape[0]), (0, 0)))


def _unpack_small(packet):
    out, off = {}, 1
    for n, shape in SMALL:
        r = math.prod(shape) // LANES
        out[n] = packet[off:off + r].reshape(shape)
        off += r
    return packet[0, 0], out


_MLA = dict(R=1, dk=MLA_DK, dv=MLA_V, hb=2, bq=512, bk=512)
_MLA_FWD_BQ = 1024
_GQA_BWD_UNROLL = 2
_GQA = dict(R=GQA_HEADS // GQA_KV_HEADS, dk=GQA_HEAD_DIM, dv=GQA_HEAD_DIM, hb=2, bq=256, bk=512)


def _layer_params(layer, full, gains):
    pack = 0 if layer == 0 else 1
    i = layer // 2

    def mat(name):
        lo, _ = _stack_range(name, 4 if name.startswith('ffn') else 2, pack)
        return full[name][(layer if name.startswith('ffn') else i) - lo]

    p = dict(ffn_norm=gains['ffn_norm'][layer][None], ffn_w_in=mat('ffn_w_in'), ffn_w_out=mat('ffn_w_out'))
    if layer % 2 == 0:
        w_in = mat('w_in_ab')
        zeros = jnp.zeros((D_MODEL, 32), w_in.dtype)
        p['w_a'] = jnp.concatenate([w_in[:, :640], zeros, zeros, w_in[:, 640:IN_A], zeros], axis=1)
        p['w_b'] = w_in[:, IN_A:]
        p['w_uq'] = jnp.pad(mat('mla_w_uq'), ((0, 0), (0, 0), (0, MLA_DK - 96))).reshape(MLA_Q_RANK, -1)
        ukv = mat('mla_w_ukv')
        p['w_uk'] = jnp.pad(ukv[:, :, :MLA_NOPE], ((0, 0), (0, 0), (0, MLA_DK - MLA_NOPE))).reshape(MLA_KV_RANK, -1)
        p['w_uv'] = ukv[:, :, MLA_NOPE:].reshape(MLA_KV_RANK, -1)
        p['w_out'] = mat('w_out_ab')
        p['mix_norm'] = gains['mix_norm_ab'][i][None]
        p['q_norm'] = gains['mla_q_norm'][i][None]
        p['kv_norm'] = gains['mla_kv_norm'][i][None]
    else:
        p['w_q'], p['w_kv'], p['w_o'] = mat('gqa_w_q'), mat('gqa_w_kv'), mat('gqa_w_o')
        p['mix_norm'] = gains['mix_norm_c'][i][None]
        p['q_norm'] = jnp.tile(gains['gqa_q_norm'][i][None], (1, 2))
        p['k_norm'] = jnp.tile(gains['gqa_k_norm'][i][None], (1, 2))
    return p


def _even_fwd(x, p, cs, swap, tag):
    xn = _rms_fwd(x, p['mix_norm'], tag + "_norm")
    za = _mm(xn, p['w_a'], name=tag + "_in_a")
    zb = _mm(xn, p['w_b'], out_dtype=BF16, name=tag + "_in_b")
    cq, ckv, krr = _mla_prep(za, p['q_norm'], p['kv_norm'], cs, swap, tag + "_mla_prep")
    q_raw = _mm(cq, p['w_uq'], name=tag + "_uq")
    k_pad = _mm(ckv, p['w_uk'], name=tag + "_uk")
    v = _mm(ckv, p['w_uv'], out_dtype=BF16, name=tag + "_uv")
    qh, kh = _mla_qk(q_raw, k_pad, krr, cs, swap, tag + "_mla_qk")
    o_a, lse_a = _flash_fwd(qh, kh, v, name=tag + "_mla_attn", **dict(_MLA, bq=_MLA_FWD_BQ))
    og, lg = [], []
    for grp in range(DIL_GROUPS):
        o, l = _dil_fwd(zb, grp, f"{tag}_dil{grp}")
        og.append(o)
        lg.append(l)
    o_b, lt = _dil_combine(og, lg, tag + "_dil_merge")
    ocat = jnp.concatenate([o_a, o_b], axis=1)
    x1 = _mm(ocat, p['w_out'], add=x, name=tag + "_out")
    saved = dict(x=x, xn=xn, za=za, zb=zb, cq=cq, ckv=ckv, qh=qh, kh=kh, v=v, lse_a=lse_a, og=og, lg=lg, lt=lt,
                 ocat=ocat)
    return x1, saved


def _even_bwd(dx1, p, sv, cs, swap, seg64, tag):
    docat = _mm(dx1, p['w_out'], mode="nt", name=tag + "_out_dx")
    d_w_out = _mm(sv['ocat'], dx1, mode="tn", name=tag + "_out_dw")
    n_a = MLA_HEADS * MLA_V
    do_a = docat[:, :n_a].astype(BF16)
    res = _dil_combine_bwd(docat[:, n_a:], sv['og'], sv['lg'], sv['lt'], seg64, tag + "_dil_merge_bwd")
    dqs, dks, dvs = [], [], []
    for grp in range(DIL_GROUPS):
        dq, dk, dv = _dil_bwd(sv['zb'], res[grp], sv['lg'][grp], res[3 + grp], grp, f"{tag}_dil{grp}_bwd")
        dqs.append(dq)
        dks.append(dk)
        dvs.append(dv)
    dzb = jnp.concatenate(dqs + dks + dvs, axis=1).astype(BF16)
    dqh, dkh, dv = _flash_bwd(sv['qh'], sv['kh'], sv['v'], sv['ocat'][:, :n_a], do_a, sv['lse_a'],
                              name=tag + "_mla_attn_bwd", **_MLA)
    dq_raw, dkh, dkrr = _mla_qk_bwd(dqh, dkh, cs, swap, tag + "_mla_qk_bwd")
    dcq = _mm(dq_raw, p['w_uq'], mode="nt", name=tag + "_uq_dx")
    d_w_uq = _mm(sv['cq'], dq_raw, mode="tn", name=tag + "_uq_dw")
    dckv = _mm(dkh, p['w_uk'], mode="nt", name=tag + "_uk_dx")
    dckv = _mm(dv, p['w_uv'], mode="nt", add=dckv, name=tag + "_uv_dx")
    d_w_uk = _mm(sv['ckv'], dkh, mode="tn", name=tag + "_uk_dw")
    d_w_uv = _mm(sv['ckv'], dv, mode="tn", name=tag + "_uv_dw")
    dza, d_gq, d_gkv = _mla_prep_bwd(sv['za'], cs, dcq, dckv, dkrr, p['q_norm'], p['kv_norm'], swap,
                                     tag + "_mla_prep_bwd")
    dxn = _mm(dza, p['w_a'], mode="nt", name=tag + "_in_a_dx")
    dxn = _mm(dzb, p['w_b'], mode="nt", add=dxn, name=tag + "_in_b_dx")
    d_w_a = _mm(sv['xn'], dza, mode="tn", name=tag + "_in_a_dw")
    d_w_b = _mm(sv['xn'], dzb, mode="tn", name=tag + "_in_b_dw")
    dx, d_g = _rms_bwd(sv['x'], p['mix_norm'], dxn, dx1, tag + "_norm_bwd")
    d_w_in = jnp.concatenate([d_w_a[:, :640], d_w_a[:, 704:736], d_w_b], axis=1)
    d_uq = d_w_uq.reshape(MLA_Q_RANK, MLA_HEADS, MLA_DK)[:, :, :MLA_NOPE + MLA_ROPE]
    d_ukv = jnp.concatenate([d_w_uk.reshape(MLA_KV_RANK, MLA_HEADS, MLA_DK)[:, :, :MLA_NOPE],
                             d_w_uv.reshape(MLA_KV_RANK, MLA_HEADS, MLA_V)], axis=2)
    grads = dict(w_in_ab=d_w_in, mla_w_uq=d_uq, mla_w_ukv=d_ukv, w_out_ab=d_w_out, mix_norm_ab=d_g[0],
                 mla_q_norm=d_gq[0], mla_kv_norm=d_gkv[0])
    return dx, grads


def _odd_fwd(x, p, cs, seg64, swap, tag):
    xn = _rms_fwd(x, p['mix_norm'], tag + "_norm")
    q_raw = _mm(xn, p['w_q'], name=tag + "_q")
    kv_raw = _mm(xn, p['w_kv'], name=tag + "_kv")
    qh, kh, v = _gqa_prep(q_raw, kv_raw, cs, p['q_norm'], p['k_norm'], seg64, swap, tag + "_gqa_prep")
    o, lse = _flash_fwd(qh, kh, v, name=tag + "_gqa_attn", **_GQA)
    x1 = _mm(o, p['w_o'], add=x, name=tag + "_o")
    return x1, dict(x=x, xn=xn, q_raw=q_raw, kv_raw=kv_raw, qh=qh, kh=kh, v=v, o=o, lse=lse)


def _odd_bwd(dx1, p, sv, cs, seg64, swap, tag):
    do = _mm(dx1, p['w_o'], mode="nt", out_dtype=BF16, name=tag + "_o_dx")
    d_w_o = _mm(sv['o'], dx1, mode="tn", name=tag + "_o_dw")
    dqh, dkh, dv = _flash_bwd(sv['qh'], sv['kh'], sv['v'], sv['o'], do, sv['lse'], name=tag + "_gqa_attn_bwd",
                              unroll=_GQA_BWD_UNROLL, **_GQA)
    dq_raw, dkv_raw, d_gq, d_gk = _gqa_prep_bwd(sv['q_raw'], sv['kv_raw'], cs, dqh, dkh, dv, p['q_norm'],
                                                p['k_norm'], seg64, swap, tag + "_gqa_prep_bwd")
    dxn = _mm(dq_raw, p['w_q'], mode="nt", name=tag + "_q_dx")
    dxn = _mm(dkv_raw, p['w_kv'], mode="nt", add=dxn, name=tag + "_kv_dx")
    d_w_q = _mm(sv['xn'], dq_raw, mode="tn", name=tag + "_q_dw")
    d_w_kv = _mm(sv['xn'], dkv_raw, mode="tn", name=tag + "_kv_dw")
    dx, d_g = _rms_bwd(sv['x'], p['mix_norm'], dxn, dx1, tag + "_norm_bwd")
    grads = dict(gqa_w_q=d_w_q, gqa_w_kv=d_w_kv, gqa_w_o=d_w_o, mix_norm_c=d_g[0],
                 gqa_q_norm=d_gq[0, :GQA_HEAD_DIM] + d_gq[0, GQA_HEAD_DIM:],
                 gqa_k_norm=d_gk[0, :GQA_HEAD_DIM] + d_gk[0, GQA_HEAD_DIM:])
    return dx, grads


def _ffn_fwd(x, p, tag):
    xn = _rms_fwd(x, p['ffn_norm'], tag + "_ffn_norm")
    h = _mm(xn, p['ffn_w_in'], out_dtype=BF16, name=tag + "_ffn_in")
    x2 = _mm(h, p['ffn_w_out'], gated=True, add=x, name=tag + "_ffn_out")
    return x2, dict(x=x, xn=xn, h=h)


def _ffn_bwd(dx2, p, sv, tag):
    d_gate, d_up = _mm(dx2, p['ffn_w_out'], mode="nt", gate_up=sv['h'], out_dtype=BF16, name=tag + "_ffn_out_dx")
    d_w_out = _mm(sv['h'], dx2, mode="tn", gated=True, name=tag + "_ffn_out_dw")
    d_w_in = jnp.concatenate([_mm(sv['xn'], d_gate, mode="tn", name=tag + "_ffn_in_dw_gate"),
                              _mm(sv['xn'], d_up, mode="tn", name=tag + "_ffn_in_dw_up")], axis=1)
    dxn = _mm(d_gate, p['ffn_w_in'], mode="nt", b_part=(0, 2), name=tag + "_ffn_in_dx_gate")
    dxn = _mm(d_up, p['ffn_w_in'], mode="nt", b_part=(1, 2), add=dxn, name=tag + "_ffn_in_dx_up")
    dx, d_g = _rms_bwd(sv['x'], p['ffn_norm'], dxn, dx2, tag + "_ffn_norm_bwd")
    return dx, d_w_in, d_w_out, d_g[0]


EVEN_MATS = ('w_in_ab', 'mla_w_uq', 'mla_w_ukv', 'w_out_ab')
ODD_MATS = ('gqa_w_q', 'gqa_w_kv', 'gqa_w_o')
FFN_MATS = ('ffn_w_in', 'ffn_w_out')


def _schedule(x, target, gains, full_of_pack, rest_grads_ready):
    s = x.shape[0]
    cs_mla, cs_gqa = _rope_tables(s)
    swap, seg64 = _swap_matrix(), _seg_matrix(GQA_HEAD_DIM)
    params, saved, full = [], [], None
    for layer in range(4):
        tag = f"l{layer}"
        if layer < 2:
            full = full_of_pack(layer, x)
        p = _layer_params(layer, full, gains)
        if layer % 2 == 0:
            x, sv = _even_fwd(x, p, cs_mla, swap, tag)
        else:
            x, sv = _odd_fwd(x, p, cs_gqa, seg64, swap, tag)
        x, sv_f = _ffn_fwd(x, p, tag)
        params.append(p)
        saved.append((sv, sv_f))
    dx, loss_row, d_final = _loss_head(x, target, gains['final_norm'][None], "loss_head")

    per_layer, rest = {}, None
    for layer in reversed(range(4)):
        p, (sv, sv_f), tag = params[layer], saved[layer], f"l{layer}"
        if layer == 0:
            rest = {n: per_layer[2][n][None] for n in EVEN_MATS}
            rest.update({n: jnp.stack([per_layer[1][n], per_layer[3][n]], axis=0) for n in ODD_MATS})
            rest.update({n: jnp.stack([per_layer[l][n] for l in (1, 2, 3)], axis=0) for n in FFN_MATS})
            token = rest_grads_ready(rest)
            if token is not None:
                p = dict(p, ffn_w_out=p['ffn_w_out'] + token[0, 0].astype(p['ffn_w_out'].dtype))
        dx, d_ffn_in, d_ffn_out, d_ffn_g = _ffn_bwd(dx, p, sv_f, tag)
        if layer % 2 == 0:
            dx, g = _even_bwd(dx, p, sv, cs_mla, swap, seg64, tag)
        else:
            dx, g = _odd_bwd(dx, p, sv, cs_gqa, seg64, swap, tag)
        g.update(ffn_w_in=d_ffn_in, ffn_w_out=d_ffn_out, ffn_norm=d_ffn_g)
        per_layer[layer] = g

    first = {n: per_layer[0][n][None] for n in EVEN_MATS + FFN_MATS}
    small = {'final_norm': d_final[0], 'ffn_norm': jnp.stack([per_layer[l]['ffn_norm'] for l in range(4)], axis=0)}
    for n in ('mix_norm_ab', 'mla_q_norm', 'mla_kv_norm'):
        small[n] = jnp.stack([per_layer[0][n], per_layer[2][n]], axis=0)
    for n in ('mix_norm_c', 'gqa_q_norm', 'gqa_k_norm'):
        small[n] = jnp.stack([per_layer[1][n], per_layer[3][n]], axis=0)
    return loss_row, dx, first, rest, small


def _core_sums(grads, pack, core, tag):
    packed = _pack_full(grads, F32, pack)
    other = _sibling_swap_halves(packed, tag)
    return packed, other, _chip_sum(packed, other, core, tag)


def _finish_reduce_scatter(packed, other, recv, pack, chip, core, tag):
    return _unpack_blocks(_join_halves(_final_sum(packed, other, recv, chip, core, tag), core, tag), pack)


def _step(x, target, w, m, v):
    big_names = [n for n, _, _ in BIG]
    chip = 2 * lax.axis_index("x") + lax.axis_index("y")
    core = lax.axis_index("c")

    gains = {n: w[n] for n, _ in SMALL if n != 'mix_norm_c'}
    c_cols = w['mix_norm_c'].shape[1]
    own_c = lax.dynamic_update_slice(jnp.zeros((2, 4 * c_cols), F32), w['mix_norm_c'], (0, chip * c_cols))
    gains['mix_norm_c'] = _unpack_small(_all_reduce_packet(_pack_small(
        {'mix_norm_c': own_c * 0.5}, jnp.zeros((LANES,), F32))))[1]['mix_norm_c']

    gathered0 = _all_gather_weights(_pack_blocks(w, BF16, 0))
    packed1, gathered0 = lax.optimization_barrier((_pack_blocks(w, BF16, 1), gathered0))
    ag_send, ag_recv, p_thru, g_thru, ag_token = _gather_start(packed1, "gather_rest")
    gains['mix_norm_ab'] = gains['mix_norm_ab'] + ag_token[0, 0]
    full0 = _unpack_gathered(gathered0, 0)

    def full_of_pack(pack, after):
        if pack == 0:
            return full0
        landed = _gather_wait(ag_send, ag_recv, p_thru, g_thru, after, "gather_rest")
        return _unpack_gathered(landed, 1, own=packed1, chip=chip)

    rs = {}

    def rest_grads_ready(rest):
        rs['packed'], rs['other'], sums = _core_sums(rest, 1, core, "grad_rest")
        rs['send'], rs['recv'], rs['t'], rs['b'], token = _send_chip_sums_start(sums, "grad_rest")
        return token

    loss_row, dx, first, rest, small = _schedule(x[0], target[0], gains, full_of_pack, rest_grads_ready)
    recv1 = _send_chip_sums_wait(rs['send'], rs['recv'], rs['t'], rs['b'], dx, "grad_rest")
    g_rest = _finish_reduce_scatter(rs['packed'], rs['other'], recv1, 1, chip, core, "grad_rest")
    packed0, other0, sums0 = _core_sums(first, 0, core, "grad_first")
    g_first = _finish_reduce_scatter(packed0, other0, _send_chip_sums(sums0, "grad_first"), 0, chip, core,
                                     "grad_first")
    g_blocks = {n: (jnp.concatenate([g_first[n], g_rest[n]], axis=0) if n in g_first else g_rest[n])
                for n in big_names}

    loss, g_small = _unpack_small(_all_reduce_packet(_pack_small(small, loss_row[0])))
    g_small['mix_norm_c'] = lax.dynamic_slice(g_small['mix_norm_c'], (0, chip * c_cols), (2, c_cols))

    out_g, out_d, out_m, out_v = {}, {}, {}, {}
    for n in big_names:
        shape = w[n].shape
        cols = shape[-1]
        d_, m_, v_ = _adamw(w[n].reshape(-1, cols), g_blocks[n].reshape(-1, cols), m[n].reshape(-1, cols),
                            v[n].reshape(-1, cols), "adamw_" + n)
        out_g[n], out_d[n], out_m[n], out_v[n] = g_blocks[n], d_.reshape(shape), m_.reshape(shape), v_.reshape(shape)
    for n, _ in SMALL:
        shape = w[n].shape
        as2d = (lambda t: t.reshape(1, -1)) if len(shape) == 1 else (lambda t: t)
        d_, m_, v_ = _adamw(as2d(w[n]), as2d(g_small[n]), as2d(m[n]), as2d(v[n]), "adamw_" + n)
        out_g[n], out_d[n], out_m[n], out_v[n] = g_small[n], d_.reshape(shape), m_.reshape(shape), v_.reshape(shape)
    return (loss, dx[None], *[out_g[n] for n in WEIGHTS], *[out_d[n] for n in WEIGHTS],
            *[out_m[n] for n in WEIGHTS], *[out_v[n] for n in WEIGHTS])


def kernel(x, mix_norm_ab, w_in_ab, mla_q_norm, mla_kv_norm, mla_w_uq, mla_w_ukv, w_out_ab, mix_norm_c, gqa_w_q, gqa_w_kv, gqa_q_norm, gqa_k_norm, gqa_w_o, ffn_norm, ffn_w_in, ffn_w_out, final_norm, loss_target, m_mix_norm_ab, m_w_in_ab, m_mla_q_norm, m_mla_kv_norm, m_mla_w_uq, m_mla_w_ukv, m_w_out_ab, m_mix_norm_c, m_gqa_w_q, m_gqa_w_kv, m_gqa_q_norm, m_gqa_k_norm, m_gqa_w_o, m_ffn_norm, m_ffn_w_in, m_ffn_w_out, m_final_norm, v_mix_norm_ab, v_w_in_ab, v_mla_q_norm, v_mla_kv_norm, v_mla_w_uq, v_mla_w_ukv, v_w_out_ab, v_mix_norm_c, v_gqa_w_q, v_gqa_w_kv, v_gqa_q_norm, v_gqa_k_norm, v_gqa_w_o, v_ffn_norm, v_ffn_w_in, v_ffn_w_out, v_final_norm):
    w = dict(zip(WEIGHTS, (mix_norm_ab, w_in_ab, mla_q_norm, mla_kv_norm, mla_w_uq, mla_w_ukv, w_out_ab, mix_norm_c,
                           gqa_w_q, gqa_w_kv, gqa_q_norm, gqa_k_norm, gqa_w_o, ffn_norm, ffn_w_in, ffn_w_out,
                           final_norm)))
    m = dict(zip(WEIGHTS, (m_mix_norm_ab, m_w_in_ab, m_mla_q_norm, m_mla_kv_norm, m_mla_w_uq, m_mla_w_ukv,
                           m_w_out_ab, m_mix_norm_c, m_gqa_w_q, m_gqa_w_kv, m_gqa_q_norm, m_gqa_k_norm, m_gqa_w_o,
                           m_ffn_norm, m_ffn_w_in, m_ffn_w_out, m_final_norm)))
    v = dict(zip(WEIGHTS, (v_mix_norm_ab, v_w_in_ab, v_mla_q_norm, v_mla_kv_norm, v_mla_w_uq, v_mla_w_ukv,
                           v_w_out_ab, v_mix_norm_c, v_gqa_w_q, v_gqa_w_kv, v_gqa_q_norm, v_gqa_k_norm, v_gqa_w_o,
                           v_ffn_norm, v_ffn_w_in, v_ffn_w_out, v_final_norm)))
    return _step(x, loss_target, w, m, v)
```

```python
import math

import numpy as np
import jax
import jax.numpy as jnp
from jax import lax
from jax.experimental import pallas as pl
from jax.experimental.pallas import tpu as pltpu

F32 = jnp.float32
BF16 = jnp.bfloat16
MESH = pl.DeviceIdType.MESH

VMEM_LIMIT_BYTES = 56 * 1024 * 1024
LANES = 128

D_MODEL = 1024
NORM_EPS = 1e-6
ROPE_THETA = 10000.0
NEG_INF = -1e30
GRID_W = 64

MLA_HEADS, MLA_Q_RANK, MLA_KV_RANK, MLA_NOPE, MLA_ROPE, MLA_V = 8, 384, 256, 64, 32, 64
MLA_DK = 128
DIL_PAIRS = ((128, 1), (512, 4), (2048, 16))
DIL_HALF, DIL_SLOTS, DIL_GROUPS, DIL_HEAD_DIM = 64, 4, 3, 64
DIL_HEADS = DIL_SLOTS * DIL_GROUPS
DIL_W = DIL_SLOTS * DIL_HEAD_DIM
GQA_HEADS, GQA_KV_HEADS, GQA_HEAD_DIM = 16, 4, 64
FFN_HIDDEN = 2816
IN_A = MLA_Q_RANK + MLA_KV_RANK + MLA_ROPE
IN_A_PAD = 768
IN_B = 3 * DIL_HEADS * DIL_HEAD_DIM

ADAM_LR, ADAM_B1, ADAM_B2, ADAM_EPS, ADAM_WD, ADAM_STEP = 0.001, 0.9, 0.999, 1e-08, 0.01, 10

LOG2E, LN2 = math.log2(math.e), math.log(2.0)
MLA_SCALE = (MLA_NOPE + MLA_ROPE) ** -0.5
GQA_SCALE = GQA_HEAD_DIM ** -0.5

WEIGHTS = ['mix_norm_ab', 'w_in_ab', 'mla_q_norm', 'mla_kv_norm', 'mla_w_uq', 'mla_w_ukv', 'w_out_ab', 'mix_norm_c',
           'gqa_w_q', 'gqa_w_kv', 'gqa_q_norm', 'gqa_k_norm', 'gqa_w_o', 'ffn_norm', 'ffn_w_in', 'ffn_w_out',
           'final_norm']
BIG = (('w_in_ab', (2, 1024, 744), 2), ('mla_w_uq', (2, 96, 8, 96), 1), ('mla_w_ukv', (2, 64, 8, 128), 1),
       ('w_out_ab', (2, 768, 256), 2), ('gqa_w_q', (2, 256, 1024), 1), ('gqa_w_kv', (2, 256, 512), 1),
       ('gqa_w_o', (2, 256, 1024), 1), ('ffn_w_in', (4, 1024, 1408), 2), ('ffn_w_out', (4, 704, 1024), 1))
PACK_W = 1024
SMALL = (('mix_norm_ab', (2, 1024)), ('mla_q_norm', (2, 384)), ('mla_kv_norm', (2, 256)), ('gqa_q_norm', (2, 64)),
         ('gqa_k_norm', (2, 64)), ('ffn_norm', (4, 1024)), ('final_norm', (1024,)), ('mix_norm_c', (2, 1024)))
PACKET_ROWS = 88


def _cparams(sem=None):
    return pltpu.CompilerParams(dimension_semantics=sem, vmem_limit_bytes=VMEM_LIMIT_BYTES)


def _pick(n, pref, mult=LANES):
    if n <= pref:
        return n
    for d in range(pref - pref % mult, 0, -mult):
        if n % d == 0:
            return d
    return n


_DIMS = {"nn": (((1,), (0,)), ((), ())), "nt": (((1,), (1,)), ((), ())), "tn": (((0,), (0,)), ((), ()))}


def _sigmoid(x):
    return 0.5 * (1.0 + jnp.tanh(0.5 * x))


def _silu_mul(gate, up):
    gate, up = gate.astype(F32), up.astype(F32)
    return gate * _sigmoid(gate) * up


def _silu_mul_bwd(gate, up, da):
    gate, up = gate.astype(F32), up.astype(F32)
    sig = _sigmoid(gate)
    silu = gate * sig
    return da * up * (sig + silu * (1.0 - sig)), da * silu


def _mm(a, b, *, mode="nn", add=None, out_dtype=F32, gated=False, gate_up=None, b_part=None, name="mm"):
    if mode == "nn":
        (m, k), (k2, n) = a.shape, b.shape
    elif mode == "nt":
        (m, k), (n, k2) = a.shape, b.shape
    else:
        (k, m), (k2, n) = a.shape, b.shape
    if gated:
        k, m = (k // 2, m) if mode == "nn" else (k, m // 2)
    if b_part is not None:
        assert mode == "nt" and k2 == k * b_part[1]
        k2 = k
    assert k == k2, (a.shape, b.shape, mode)
    if mode == "tn":
        deep = a.dtype == BF16 and b.dtype == BF16
        bm, bn, bk = _pick(m, 1408), _pick(n, 1024), _pick(k, 2048 if deep else 1024, 16)
    else:
        bm, bn, bk = _pick(m, 512, 16), _pick(n, 1408), _pick(k, 2816)
    nk = k // bk
    assert m % bm == 0 and n % bn == 0 and k % bk == 0
    has_add, has_gu = add is not None, gate_up is not None
    assert not (has_add and has_gu) and not (gated and mode == "nt") and not (has_gu and mode != "nt")
    n_in = 2 + int(gated) + int(has_add) + 2 * int(has_gu)

    def body(*refs):
        a_val = _silu_mul(refs[0][...], refs[1][...]) if gated else refs[0][...]
        b_ref = refs[1 + int(gated)]
        part = lax.dot_general(a_val.astype(BF16), b_ref[...].astype(BF16), _DIMS[mode],
                               preferred_element_type=F32)

        def finish(r):
            if has_gu:
                d_gate, d_up = _silu_mul_bwd(refs[2][...], refs[3][...], r)
                refs[n_in][...] = d_gate.astype(refs[n_in].dtype)
                refs[n_in + 1][...] = d_up.astype(refs[n_in + 1].dtype)
                return
            if has_add:
                r = r + refs[n_in - 1][...]
            refs[n_in][...] = r.astype(refs[n_in].dtype)

        if nk == 1:
            finish(part)
        else:
            acc_ref = refs[-1]
            kk = pl.program_id(2)

            @pl.when(kk == 0)
            def _():
                acc_ref[...] = part

            @pl.when(kk > 0)
            def _():
                acc_ref[...] += part

            @pl.when(kk == nk - 1)
            def _():
                finish(acc_ref[...])

    a_bytes, b_bytes = a.size * a.dtype.itemsize, b.size * b.dtype.itemsize
    n_outer = nk == 1 and (n // bn) * a_bytes + b_bytes < a_bytes + (m // bm) * b_bytes

    def at(f):
        return (lambda j, i, kk: f(i, j, kk)) if n_outer else f

    if mode == "nn":
        a_specs = [pl.BlockSpec((bm, bk), at(lambda i, j, kk, o=o: (i, kk + o))) for o in ((0, nk) if gated else (0,))]
        b_spec = pl.BlockSpec((bk, bn), at(lambda i, j, kk: (kk, j)))
    elif mode == "nt":
        a_specs = [pl.BlockSpec((bm, bk), at(lambda i, j, kk: (i, kk)))]
        b_off = 0 if b_part is None else b_part[0] * nk
        b_spec = pl.BlockSpec((bn, bk), at(lambda i, j, kk: (j, kk + b_off)))
    else:
        a_specs = [pl.BlockSpec((bk, bm), at(lambda i, j, kk, o=o: (kk, i + o)))
                   for o in ((0, m // bm) if gated else (0,))]
        b_spec = pl.BlockSpec((bk, bn), at(lambda i, j, kk: (kk, j)))
    o_spec = pl.BlockSpec((bm, bn), at(lambda i, j, kk: (i, j)))
    in_specs, args = a_specs + [b_spec], [a] * len(a_specs) + [b]
    if has_add:
        in_specs, args = in_specs + [o_spec], args + [add]
    if has_gu:
        in_specs += [o_spec, pl.BlockSpec((bm, bn), at(lambda i, j, kk: (i, j + n // bn)))]
        args += [gate_up, gate_up]
    out = jax.ShapeDtypeStruct((m, n), out_dtype)
    grid = (n // bn, m // bm, nk) if n_outer else (m // bm, n // bn, nk)
    return pl.pallas_call(
        body, name=name, grid=grid, in_specs=in_specs, out_specs=[o_spec, o_spec] if has_gu else o_spec,
        out_shape=[out, out] if has_gu else out,
        scratch_shapes=[pltpu.VMEM((bm, bn), F32)] if nk > 1 else [],
        compiler_params=_cparams(("parallel", "parallel", "arbitrary")),
    )(*args)


def _rows_call(fn, rows, consts, out_rows, out_accs=(), *, bs=256, name):
    s = rows[0].shape[0]
    bs = min(bs, s)
    assert s % bs == 0
    nr, nc, no, na = len(rows), len(consts), len(out_rows), len(out_accs)

    def body(*refs):
        vals = [r[...] for r in refs[:nr + nc]]
        outs = refs[nr + nc:]
        res = fn(*vals)
        if not isinstance(res, (tuple, list)):
            res = (res,)
        assert len(res) == no + na, (len(res), no, na)
        for r, v in zip(outs[:no], res[:no]):
            r[...] = v.astype(r.dtype)
        if na:
            i = pl.program_id(0)
            for r, v in zip(outs[no:], res[no:]):
                @pl.when(i == 0)
                def _(r=r, v=v):
                    r[...] = v

                @pl.when(i > 0)
                def _(r=r, v=v):
                    r[...] += v

    in_specs = [pl.BlockSpec((bs, a.shape[1]), lambda i: (i, 0)) for a in rows]
    in_specs += [pl.BlockSpec(c.shape, lambda i: (0, 0)) for c in consts]
    out_specs = [pl.BlockSpec((bs, c), lambda i: (i, 0)) for c, _ in out_rows]
    out_specs += [pl.BlockSpec(tuple(sh), lambda i: (0, 0)) for sh in out_accs]
    out_shape = [jax.ShapeDtypeStruct((s, c), dt) for c, dt in out_rows]
    out_shape += [jax.ShapeDtypeStruct(tuple(sh), F32) for sh in out_accs]
    res = pl.pallas_call(
        body, name=name, grid=(s // bs,), in_specs=in_specs, out_specs=out_specs, out_shape=out_shape,
        compiler_params=_cparams(("arbitrary",) if na else ("parallel",)),
    )(*rows, *consts)
    return res


def _rms(x, g):
    return x * lax.rsqrt(jnp.mean(x * x, axis=-1, keepdims=True) + NORM_EPS) * g


def _rms_bwd_math(x, g, dy):
    r = lax.rsqrt(jnp.mean(x * x, axis=-1, keepdims=True) + NORM_EPS)
    u = dy * g
    dx = r * u - x * (r * r * r) * jnp.mean(u * x, axis=-1, keepdims=True)
    dg = jnp.sum(dy * x * r, axis=0, keepdims=True)
    return dx, dg


NORM_ROWS = 512


def _rms_fwd(x, g, name):
    return _rows_call(lambda xv, gv: _rms(xv, gv), [x], [g], [(x.shape[1], BF16)], bs=NORM_ROWS, name=name)[0]


def _rms_bwd(x, g, dy, dres, name):
    def fn(xv, dyv, dresv, gv):
        dx, dg = _rms_bwd_math(xv, gv, dyv.astype(F32))
        return dx + dresv, dg
    return _rows_call(fn, [x, dy, dres], [g], [(x.shape[1], F32)], [(1, x.shape[1])], bs=NORM_ROWS, name=name)


def _chunkdot(x, m):
    outs = [jnp.dot(x[:, c:c + LANES], m, precision=lax.Precision.HIGHEST, preferred_element_type=F32)
            for c in range(0, x.shape[1], LANES)]
    return outs[0] if len(outs) == 1 else jnp.concatenate(outs, axis=1)


def _lanes(t, width):
    n = width // LANES
    return t if n == 1 else jnp.concatenate([t] * n, axis=1)


def _rope(x, cs, swap):
    w = x.shape[1]
    return x * _lanes(cs[:, :LANES], w) + _chunkdot(x, swap) * _lanes(cs[:, LANES:], w)


def _rope_t(dy, cs, swap):
    w = dy.shape[1]
    return dy * _lanes(cs[:, :LANES], w) + _chunkdot(dy * _lanes(cs[:, LANES:], w), swap)


def _swap_matrix():
    m = np.zeros((LANES, LANES), np.float32)
    for j in range(LANES):
        src = j + 16 if (j % 32) < 16 else j - 16
        m[src, j] = 1.0
    return jnp.asarray(m)


def _seg_matrix(seg):
    idx = np.arange(LANES) // seg
    return jnp.asarray((idx[:, None] == idx[None, :]).astype(np.float32))


def _rope_tables(s):
    pos = jnp.arange(s)

    def angles(p, dim):
        freqs = ROPE_THETA ** (-jnp.arange(0, dim, 2, dtype=F32) / dim)
        ang = p.astype(F32)[:, None] * freqs[None, :]
        return jnp.cos(ang), jnp.sin(ang)

    cos_t, sin_t = angles(pos, MLA_ROPE)
    one, zero = jnp.ones((s, 64), F32), jnp.zeros((s, 64), F32)
    mla = jnp.concatenate([one, cos_t, cos_t, one[:, :32], zero, -sin_t, sin_t, zero[:, :32]], axis=1)
    cos_r, sin_r = angles(pos // GRID_W, GQA_HEAD_DIM // 2)
    cos_c, sin_c = angles(pos % GRID_W, GQA_HEAD_DIM // 2)
    c64 = jnp.concatenate([cos_r, cos_r, cos_c, cos_c], axis=1)
    s64 = jnp.concatenate([-sin_r, sin_r, -sin_c, sin_c], axis=1)
    gqa = jnp.concatenate([c64, c64, s64, s64], axis=1)
    return mla, gqa


def _stack_heads(ref, heads, d, dtype=None):
    parts = [ref[:, hd * d:(hd + 1) * d] for hd in heads]
    out = parts[0] if len(parts) == 1 else jnp.concatenate(parts, axis=0)
    return out if dtype is None else out.astype(dtype)


def _fill_v_ones(v_ref, va_ref, hb, dv):
    @pl.when(pl.program_id(1) == 0)
    def _():
        ones = jnp.ones((v_ref.shape[0], dv), BF16)
        for h in range(hb):
            va_ref[:, 2 * h * dv:(2 * h + 1) * dv] = v_ref[:, h * dv:(h + 1) * dv]
            va_ref[:, (2 * h + 1) * dv:(2 * h + 2) * dv] = ones


def _flash_fwd(q, k, v, *, R, dk, dv, hb, bq, bk, name):
    s = q.shape[0]
    g = k.shape[1] // dk
    ng = g // hb
    bq, bk = min(bq, s), min(bk, s)
    nq, nkb = s // bq, s // bk
    rb = R * bq

    def body(q_ref, k_ref, v_ref, o_ref, lse_ref, va_ref):
        _fill_v_ones(v_ref, va_ref, hb, dv)
        head_sets = [[h * R + r for r in range(R)] for h in range(hb)]
        qss = [_stack_heads(q_ref, heads, dk) for heads in head_sets]

        def step(jj, carry):
            carry = list(carry)
            rows = [pl.ds(pl.multiple_of((jj * unroll + u) * bk, bk), bk) for u in range(unroll)]
            scs = [[lax.dot_general(qss[h], k_ref[rows[u], h * dk:(h + 1) * dk], _DIMS["nt"],
                                    preferred_element_type=F32) for h in range(hb)] for u in range(unroll)]
            for u in range(unroll):
                m2s = [jnp.maximum(carry[h][0], jnp.max(scs[u][h], axis=1, keepdims=True)) for h in range(hb)]
                ps = [jnp.exp2(scs[u][h] - m2s[h]).astype(BF16) for h in range(hb)]
                pvs = [jnp.dot(ps[h], va_ref[rows[u], 2 * h * dv:2 * (h + 1) * dv], preferred_element_type=F32)
                       for h in range(hb)]
                carry = [(m2s[h], jnp.exp2(carry[h][0] - m2s[h]) * carry[h][1] + pvs[h]) for h in range(hb)]
            return tuple(carry)

        unroll = 4 if nkb % 4 == 0 else 1
        init = tuple((jnp.full((rb, 1), NEG_INF, F32), jnp.zeros((rb, 2 * dv), F32)) for _ in range(hb))
        final = lax.fori_loop(0, nkb // unroll, step, init)
        for h, heads in enumerate(head_sets):
            m, acc = final[h]
            l = acc[:, dv:dv + 1]
            o = acc[:, :dv] / l
            lse = m + jnp.log2(l)
            for r, hd in enumerate(heads):
                o_ref[:, hd * dv:(hd + 1) * dv] = o[r * bq:(r + 1) * bq].astype(o_ref.dtype)
                lse_ref[0, :, hd:hd + 1] = lse[r * bq:(r + 1) * bq]

    return pl.pallas_call(
        body, name=name, grid=(ng, nq),
        in_specs=[pl.BlockSpec((bq, hb * R * dk), lambda gi, i: (i, gi)),
                  pl.BlockSpec((s, hb * dk), lambda gi, i: (0, gi)),
                  pl.BlockSpec((s, hb * dv), lambda gi, i: (0, gi))],
        out_specs=[pl.BlockSpec((bq, hb * R * dv), lambda gi, i: (i, gi)),
                   pl.BlockSpec((1, bq, hb * R), lambda gi, i: (gi, i, 0))],
        out_shape=[jax.ShapeDtypeStruct((s, g * R * dv), BF16), jax.ShapeDtypeStruct((ng, s, hb * R), F32)],
        scratch_shapes=[pltpu.VMEM((s, 2 * hb * dv), BF16)],
        compiler_params=_cparams(("parallel", "arbitrary")),
    )(q, k, v)


def _flash_bwd(q, k, v, o, do, lse, *, R, dk, dv, hb, bq, bk, name, unroll=1):
    s = q.shape[0]
    g = k.shape[1] // dk
    ng = g // hb
    bq, bk = min(bq, s), min(bk, s)
    nq, nkb = s // bq, s // bk
    rb = R * bq

    def body(q_ref, k_ref, v_ref, o_ref, do_ref, lse_ref, dq_ref, dkt_ref, dvt_ref, va_ref):
        @pl.when(pl.program_id(1) == 0)
        def _():
            dkt_ref[...] = jnp.zeros(dkt_ref.shape, F32)
            dvt_ref[...] = jnp.zeros(dvt_ref.shape, F32)

        _fill_v_ones(v_ref, va_ref, hb, dv)
        lane = lax.broadcasted_iota(jnp.int32, (rb, dv), 1)
        head_sets = [[h * R + r for r in range(R)] for h in range(hb)]
        q_t = jnp.transpose(q_ref[...].astype(F32)).astype(BF16)
        do_t = jnp.transpose(do_ref[...].astype(F32)).astype(BF16)

        def stack_t(t, heads, d):
            parts = [t[hd * d:(hd + 1) * d] for hd in heads]
            return parts[0] if len(parts) == 1 else jnp.concatenate(parts, axis=1)

        qss, qts, dots, dosas, lcols = [], [], [], [], []
        for heads in head_sets:
            dos = _stack_heads(do_ref, heads, dv, BF16)
            delta = jnp.sum(dos.astype(F32) * _stack_heads(o_ref, heads, dv, F32), axis=1, keepdims=True)
            hi = delta.astype(BF16).astype(F32)
            lo = delta - hi
            qss.append(_stack_heads(q_ref, heads, dk))
            qts.append(stack_t(q_t, heads, dk))
            dots.append(stack_t(do_t, heads, dv))
            dosas.append(jnp.concatenate(
                [dos, jnp.where(lane == 0, -hi, jnp.where(lane == 1, -lo, 0.0)).astype(BF16)], axis=1))
            cols = [lse_ref[0, :, hd:hd + 1] for hd in heads]
            lcols.append(cols[0] if R == 1 else jnp.concatenate(cols, axis=0))

        def step(jj, dqs):
            hs = range(hb)
            for u in range(unroll):
                j = jj * unroll + u
                r0 = pl.multiple_of(j * bk, bk)
                kjs = [k_ref[pl.ds(r0, bk), h * dk:(h + 1) * dk] for h in hs]
                vas = [va_ref[pl.ds(r0, bk), 2 * h * dv:2 * (h + 1) * dv] for h in hs]
                ss = [lax.dot_general(qss[h], kjs[h], _DIMS["nt"], preferred_element_type=F32) for h in hs]
                dps = [lax.dot_general(dosas[h], vas[h], _DIMS["nt"], preferred_element_type=F32) for h in hs]
                ps = [jnp.exp2(ss[h] - lcols[h]) for h in hs]
                pbs = [ps[h].astype(BF16) for h in hs]
                dss = [(ps[h] * dps[h]).astype(BF16) for h in hs]
                for h in hs:
                    dvt_ref[0, j, h * dv:(h + 1) * dv, :] += jnp.dot(dots[h], pbs[h], preferred_element_type=F32)
                for h in hs:
                    dkt_ref[0, j, h * dk:(h + 1) * dk, :] += jnp.dot(qts[h], dss[h], preferred_element_type=F32)
                dqs = tuple(dqs[h] + jnp.dot(dss[h], kjs[h], preferred_element_type=F32) for h in hs)
            return dqs

        assert nkb % unroll == 0
        dqs = lax.fori_loop(0, nkb // unroll, step, tuple(jnp.zeros((rb, dk), F32) for _ in range(hb)))
        for h, heads in enumerate(head_sets):
            for r, hd in enumerate(heads):
                dq_ref[:, hd * dk:(hd + 1) * dk] = dqs[h][r * bq:(r + 1) * bq]

    qspec = pl.BlockSpec((bq, hb * R * dk), lambda gi, i: (i, gi))
    ospec = pl.BlockSpec((bq, hb * R * dv), lambda gi, i: (i, gi))
    kspec = pl.BlockSpec((s, hb * dk), lambda gi, i: (0, gi))
    vspec = pl.BlockSpec((s, hb * dv), lambda gi, i: (0, gi))
    dq, dkt, dvt = pl.pallas_call(
        body, name=name, grid=(ng, nq),
        in_specs=[qspec, kspec, vspec, ospec, ospec, pl.BlockSpec((1, bq, hb * R), lambda gi, i: (gi, i, 0))],
        out_specs=[qspec, pl.BlockSpec((1, nkb, hb * dk, bk), lambda gi, i: (gi, 0, 0, 0)),
                   pl.BlockSpec((1, nkb, hb * dv, bk), lambda gi, i: (gi, 0, 0, 0))],
        out_shape=[jax.ShapeDtypeStruct((s, g * R * dk), F32), jax.ShapeDtypeStruct((ng, nkb, hb * dk, bk), F32),
                   jax.ShapeDtypeStruct((ng, nkb, hb * dv, bk), F32)],
        scratch_shapes=[pltpu.VMEM((s, 2 * hb * dv), BF16)],
        compiler_params=_cparams(("parallel", "arbitrary")),
    )(q, k, v, o, do, lse)
    return dq, _keys_first(dkt, name + "_dk"), _keys_first(dvt, name + "_dv")


def _keys_first(t, name):
    ng, nkb, f, bk = t.shape

    def body(t_ref, o_ref):
        for j in range(nkb):
            o_ref[j * bk:(j + 1) * bk, :] = jnp.transpose(t_ref[0, j])

    return pl.pallas_call(
        body, name=name, grid=(ng,),
        in_specs=[pl.BlockSpec((1, nkb, f, bk), lambda gi: (gi, 0, 0, 0))],
        out_specs=pl.BlockSpec((nkb * bk, f), lambda gi: (0, gi)),
        out_shape=jax.ShapeDtypeStruct((nkb * bk, ng * f), t.dtype),
        compiler_params=_cparams(("parallel",)),
    )(t)


DIL_T = 1024
DIL_P = DIL_HALF
DIL_NCOL = IN_B // DIL_W


DIL_BATCH = 4


def _alibi_slope(head):
    return float(2.0 ** (-8.0 * (head + 1) / DIL_HEADS))


def _slot(sl_i):
    return slice(sl_i * DIL_HEAD_DIM, (sl_i + 1) * DIL_HEAD_DIM)


def _halo_specs(d, col, s, t):
    h = DIL_P * d
    per, last = t // h, s // h - 1
    return [pl.BlockSpec((h, DIL_W), lambda c: (jnp.maximum(c * per - 1, 0), col)),
            pl.BlockSpec((t, DIL_W), lambda c: (c, col)),
            pl.BlockSpec((h, DIL_W), lambda c: (jnp.minimum((c + 1) * per, last), col))]


def _staging(rows):
    return tuple(pltpu.VMEM((rows, LANES), F32) for _ in range(DIL_W // LANES))


def _stage(buf, refs):
    off = 0
    for r in refs:
        val = r[...].astype(F32)
        for j in range(DIL_W // LANES):
            buf[j][off:off + r.shape[0], :] = val[:, j * LANES:(j + 1) * LANES]
        off += r.shape[0]


def _unstage(buf, ref):
    ref[...] = jnp.concatenate([half[...] for half in buf], axis=1).astype(ref.dtype)


def _sub_tiles(d, t):
    return [(b * DIL_P * d + r, b * DIL_P) for b in range(t // (DIL_P * d)) for r in range(d)]


def _rows(start, size, d):
    return pl.ds(start, size, stride=d) if d > 1 else pl.ds(start, size)


def _strided(buf, start, size, d):
    return jnp.concatenate([half[_rows(start, size, d), :] for half in buf], axis=1)


def _put_strided(buf, start, d, val):
    for j in range(DIL_W // LANES):
        buf[j][_rows(start, val.shape[0], d), :] = val[:, j * LANES:(j + 1) * LANES]


def _band(u0, length, d, queries_wide):
    if queries_wide:
        shape = (3 * DIL_P, DIL_P)
        wide = u0 - DIL_P + lax.broadcasted_iota(jnp.int32, shape, 0)
        narrow = u0 + lax.broadcasted_iota(jnp.int32, shape, 1)
    else:
        shape = (DIL_P, 3 * DIL_P)
        narrow = u0 + lax.broadcasted_iota(jnp.int32, shape, 0)
        wide = u0 - DIL_P + lax.broadcasted_iota(jnp.int32, shape, 1)
    rel = jnp.abs(wide - narrow)
    valid = (rel <= DIL_HALF) & (wide >= 0) & (wide < length)
    return valid, rel.astype(F32) * float(d)


def _dil_fwd(zb, grp, name):
    s = zb.shape[0]
    d = DIL_PAIRS[grp][1]
    t = min(DIL_T, s)
    h = DIL_P * d
    scale = DIL_HEAD_DIM ** -0.5

    def body(q_ref, kp, kc, kn, vp, vc, vn, o_ref, lse_ref, qbuf, kbuf, vbuf, obuf, lbuf):
        _stage(qbuf, (q_ref,))
        _stage(kbuf, (kp, kc, kn))
        _stage(vbuf, (vp, vc, vn))
        u_step = pl.program_id(0) * (t // d)
        tiles = _sub_tiles(d, t)
        for g0 in range(0, len(tiles), DIL_BATCH):
            batch = tiles[g0:g0 + DIL_BATCH]
            masks = [_band(u_step + u, s // d, d, False) for _, u in batch]
            qs = [_strided(qbuf, row, DIL_P, d).astype(BF16) for row, _ in batch]
            ks = [_strided(kbuf, row, 3 * DIL_P, d).astype(BF16) for row, _ in batch]
            vs = [_strided(vbuf, row, 3 * DIL_P, d).astype(BF16) for row, _ in batch]
            chains = [(i, sl_i) for i in range(len(batch)) for sl_i in range(DIL_SLOTS)]
            scs = [lax.dot_general(qs[i][:, _slot(sl_i)], ks[i][:, _slot(sl_i)], _DIMS["nt"],
                                   preferred_element_type=F32) for i, sl_i in chains]
            scs = [jnp.where(masks[i][0], sc * scale - _alibi_slope(grp * DIL_SLOTS + sl_i) * masks[i][1], NEG_INF)
                   for (i, sl_i), sc in zip(chains, scs)]
            ms = [jnp.max(sc, axis=1, keepdims=True) for sc in scs]
            es = [jnp.exp(sc - m) for sc, m in zip(scs, ms)]
            dens = [jnp.sum(e, axis=1, keepdims=True) for e in es]
            outs = [jnp.dot((e / den).astype(BF16), vs[i][:, _slot(sl_i)], preferred_element_type=F32)
                    for (i, sl_i), e, den in zip(chains, es, dens)]
            lses = [jnp.broadcast_to(m + jnp.log(den), (DIL_P, DIL_HEAD_DIM)) for m, den in zip(ms, dens)]
            for i, (row, _) in enumerate(batch):
                pick = slice(i * DIL_SLOTS, (i + 1) * DIL_SLOTS)
                _put_strided(obuf, row, d, jnp.concatenate(outs[pick], axis=1))
                _put_strided(lbuf, row, d, jnp.concatenate(lses[pick], axis=1))
        _unstage(obuf, o_ref)
        _unstage(lbuf, lse_ref)

    own = pl.BlockSpec((t, DIL_W), lambda c: (c, 0))
    return pl.pallas_call(
        body, name=name, grid=(s // t,),
        in_specs=[pl.BlockSpec((t, DIL_W), lambda c: (c, grp))] + _halo_specs(d, 3 + grp, s, t)
        + _halo_specs(d, 6 + grp, s, t),
        out_specs=[own, own], out_shape=[jax.ShapeDtypeStruct((s, DIL_W), F32)] * 2,
        scratch_shapes=[_staging(t), _staging(t + 2 * h), _staging(t + 2 * h), _staging(t), _staging(t)],
        compiler_params=_cparams(("parallel",)),
    )(zb, zb, zb, zb, zb, zb, zb)


def _dil_bwd(zb, do, lse, dl, grp, name):
    s = zb.shape[0]
    d = DIL_PAIRS[grp][1]
    t = min(DIL_T, s)
    h = DIL_P * d
    scale = DIL_HEAD_DIM ** -0.5

    def chain_grads(qs, ks, vs, dos, lses, dls, masks):
        chains = [(i, sl_i) for i in range(len(qs)) for sl_i in range(DIL_SLOTS)]
        scs = [lax.dot_general(qs[i][:, _slot(sl_i)], ks[i][:, _slot(sl_i)], _DIMS["nt"],
                               preferred_element_type=F32) for i, sl_i in chains]
        dps = [lax.dot_general(dos[i][:, _slot(sl_i)], vs[i][:, _slot(sl_i)], _DIMS["nt"],
                               preferred_element_type=F32) for i, sl_i in chains]
        ps = [jnp.exp(jnp.where(masks[i][0], sc * scale - _alibi_slope(grp * DIL_SLOTS + sl_i) * masks[i][1],
                                NEG_INF) - lses[i][:, sl_i * DIL_HEAD_DIM:sl_i * DIL_HEAD_DIM + 1])
              for (i, sl_i), sc in zip(chains, scs)]
        dss = [(p * (dp - dls[i][:, sl_i * DIL_HEAD_DIM:sl_i * DIL_HEAD_DIM + 1]) * scale).astype(BF16)
               for (i, sl_i), p, dp in zip(chains, ps, dps)]
        return chains, ps, dss

    def dq_body(q_ref, kp, kc, kn, vp, vc, vn, do_ref, lse_ref, dl_ref, dq_ref, qbuf, kbuf, vbuf, dobuf, lsebuf,
                dlbuf, obuf):
        _stage(qbuf, (q_ref,))
        _stage(dobuf, (do_ref,))
        _stage(lsebuf, (lse_ref,))
        _stage(dlbuf, (dl_ref,))
        _stage(kbuf, (kp, kc, kn))
        _stage(vbuf, (vp, vc, vn))
        u_step = pl.program_id(0) * (t // d)
        tiles = _sub_tiles(d, t)
        for g0 in range(0, len(tiles), DIL_BATCH):
            batch = tiles[g0:g0 + DIL_BATCH]
            masks = [_band(u_step + u, s // d, d, False) for _, u in batch]
            narrow = [[_strided(b, row, DIL_P, d) for row, _ in batch] for b in (qbuf, dobuf, lsebuf, dlbuf)]
            ks = [_strided(kbuf, row, 3 * DIL_P, d).astype(BF16) for row, _ in batch]
            vs = [_strided(vbuf, row, 3 * DIL_P, d).astype(BF16) for row, _ in batch]
            chains, _, dss = chain_grads([a.astype(BF16) for a in narrow[0]], ks, vs,
                                         [a.astype(BF16) for a in narrow[1]], narrow[2], narrow[3], masks)
            outs = [jnp.dot(ds, ks[i][:, _slot(sl_i)], preferred_element_type=F32)
                    for (i, sl_i), ds in zip(chains, dss)]
            for i, (row, _) in enumerate(batch):
                _put_strided(obuf, row, d, jnp.concatenate(outs[i * DIL_SLOTS:(i + 1) * DIL_SLOTS], axis=1))
        _unstage(obuf, dq_ref)

    def dkv_body(k_ref, v_ref, qp, qc, qn, dop, doc, don, lp, lc, ln, dlp, dlc, dln, dk_ref, dv_ref,
                 kbuf, vbuf, qbuf, dobuf, lsebuf, dlbuf, dkbuf, dvbuf):
        _stage(kbuf, (k_ref,))
        _stage(vbuf, (v_ref,))
        _stage(qbuf, (qp, qc, qn))
        _stage(dobuf, (dop, doc, don))
        _stage(lsebuf, (lp, lc, ln))
        _stage(dlbuf, (dlp, dlc, dln))
        u_step = pl.program_id(0) * (t // d)
        tiles = _sub_tiles(d, t)
        for g0 in range(0, len(tiles), DIL_BATCH):
            batch = tiles[g0:g0 + DIL_BATCH]
            masks = [_band(u_step + u, s // d, d, True) for _, u in batch]
            ks = [_strided(kbuf, row, DIL_P, d).astype(BF16) for row, _ in batch]
            vs = [_strided(vbuf, row, DIL_P, d).astype(BF16) for row, _ in batch]
            wide_ = [[_strided(b, row, 3 * DIL_P, d) for row, _ in batch] for b in (qbuf, dobuf, lsebuf, dlbuf)]
            qs, dos = [a.astype(BF16) for a in wide_[0]], [a.astype(BF16) for a in wide_[1]]
            chains, ps, dss = chain_grads(qs, ks, vs, dos, wide_[2], wide_[3], masks)
            dvs = [lax.dot_general(p.astype(BF16), dos[i][:, _slot(sl_i)], _DIMS["tn"], preferred_element_type=F32)
                   for (i, sl_i), p in zip(chains, ps)]
            dks = [lax.dot_general(ds, qs[i][:, _slot(sl_i)], _DIMS["tn"], preferred_element_type=F32)
                   for (i, sl_i), ds in zip(chains, dss)]
            for i, (row, _) in enumerate(batch):
                pick = slice(i * DIL_SLOTS, (i + 1) * DIL_SLOTS)
                _put_strided(dkbuf, row, d, jnp.concatenate(dks[pick], axis=1))
                _put_strided(dvbuf, row, d, jnp.concatenate(dvs[pick], axis=1))
        _unstage(dkbuf, dk_ref)
        _unstage(dvbuf, dv_ref)

    def zcur(col):
        return pl.BlockSpec((t, DIL_W), lambda c: (c, col))

    own = pl.BlockSpec((t, DIL_W), lambda c: (c, 0))
    out = jax.ShapeDtypeStruct((s, DIL_W), F32)
    tile, wide = _staging(t), _staging(t + 2 * h)
    dq = pl.pallas_call(
        dq_body, name=name + "_dq", grid=(s // t,),
        in_specs=[zcur(grp)] + _halo_specs(d, 3 + grp, s, t) + _halo_specs(d, 6 + grp, s, t) + [own, own, own],
        out_specs=own, out_shape=out, scratch_shapes=[tile, wide, wide, tile, tile, tile, tile],
        compiler_params=_cparams(("parallel",)),
    )(zb, zb, zb, zb, zb, zb, zb, do, lse, dl)
    own3 = _halo_specs(d, 0, s, t)
    dk, dv = pl.pallas_call(
        dkv_body, name=name + "_dkv", grid=(s // t,),
        in_specs=[zcur(3 + grp), zcur(6 + grp)] + _halo_specs(d, grp, s, t) + own3 + own3 + own3,
        out_specs=[own, own], out_shape=[out, out],
        scratch_shapes=[tile, tile, wide, wide, wide, wide, tile, tile],
        compiler_params=_cparams(("parallel",)),
    )(zb, zb, zb, zb, zb, do, do, do, lse, lse, lse, dl, dl, dl)
    return dq, dk, dv


def _dil_combine(os_, ls_, name):
    def fn(o0, o1, o2, l0, l1, l2):
        m = jnp.maximum(jnp.maximum(l0, l1), l2)
        e0, e1, e2 = jnp.exp(l0 - m), jnp.exp(l1 - m), jnp.exp(l2 - m)
        den = e0 + e1 + e2
        comb = (e0 / den) * o0 + (e1 / den) * o1 + (e2 / den) * o2
        return comb, m + jnp.log(den)
    return _rows_call(fn, list(os_) + list(ls_), [], [(DIL_W, BF16), (DIL_W, F32)], name=name)


def _dil_combine_bwd(dcomb, os_, ls_, lt, seg64, name):
    def fn(dc, o0, o1, o2, l0, l1, l2, ltv, seg):
        w = [jnp.exp(l - ltv) for l in (l0, l1, l2)]
        comb = w[0] * o0 + w[1] * o1 + w[2] * o2
        t = _chunkdot(dc * comb, seg)
        return [wg * dc for wg in w] + [wg * t for wg in w]
    return _rows_call(fn, [dcomb] + list(os_) + list(ls_) + [lt], [seg64],
                      [(DIL_W, BF16)] * 3 + [(DIL_W, F32)] * 3, name=name)


def _mla_prep(za, gq, gkv, cs, swap, name):
    def fn(z, csv, gqv, gkvv, sw):
        return (_rms(z[:, :MLA_Q_RANK], gqv), _rms(z[:, MLA_Q_RANK:640], gkvv), _rope(z[:, 640:], csv, sw))
    return _rows_call(fn, [za, cs], [gq, gkv, swap], [(MLA_Q_RANK, BF16), (MLA_KV_RANK, BF16), (LANES, F32)],
                      name=name)


def _mla_prep_bwd(za, cs, dcq, dckv, dkr, gq, gkv, swap, name):
    def fn(z, csv, dcqv, dckvv, dkrv, gqv, gkvv, sw):
        d1, dg1 = _rms_bwd_math(z[:, :MLA_Q_RANK], gqv, dcqv)
        d2, dg2 = _rms_bwd_math(z[:, MLA_Q_RANK:640], gkvv, dckvv)
        d3 = _rope_t(dkrv, csv, sw)
        return jnp.concatenate([d1, d2, d3], axis=1), dg1, dg2
    return _rows_call(fn, [za, cs, dcq, dckv, dkr], [gq, gkv, swap], [(IN_A_PAD, BF16)],
                      [(1, MLA_Q_RANK), (1, MLA_KV_RANK)], name=name)


def _mla_qk(q_raw, k_pad, krr, cs, swap, name):
    w = MLA_HEADS * MLA_DK

    def fn(qv, kv, krv, csv, sw):
        return _rope(qv, csv, sw) * (MLA_SCALE * LOG2E), kv + _lanes(krv, w)
    return _rows_call(fn, [q_raw, k_pad, krr, cs], [swap], [(w, BF16), (w, BF16)], name=name)


def _mla_qk_bwd(dqh, dkh, cs, swap, name):
    w = MLA_HEADS * MLA_DK

    def fn(dq, dk, csv, sw):
        dk = dk * LN2
        acc = dk[:, :LANES]
        for h in range(1, MLA_HEADS):
            acc = acc + dk[:, h * LANES:(h + 1) * LANES]
        lane = lax.broadcasted_iota(jnp.int32, acc.shape, 1)
        acc = jnp.where((lane >= MLA_NOPE) & (lane < MLA_NOPE + MLA_ROPE), acc, 0.0)
        return _rope_t(dq * MLA_SCALE, csv, sw), dk, acc
    return _rows_call(fn, [dqh, dkh, cs], [swap], [(w, BF16), (w, BF16), (LANES, F32)], name=name)


def _head_norm(t, g2, seg):
    r = lax.rsqrt(_chunkdot(t * t, seg) * (1.0 / GQA_HEAD_DIM) + NORM_EPS)
    return t * r * _lanes(g2, t.shape[1]), r


def _head_norm_bwd(t, g2, seg, dn):
    w = t.shape[1]
    r = lax.rsqrt(_chunkdot(t * t, seg) * (1.0 / GQA_HEAD_DIM) + NORM_EPS)
    u = dn * _lanes(g2, w)
    dt = r * u - t * (r * r * r) * (_chunkdot(u * t, seg) * (1.0 / GQA_HEAD_DIM))
    dgw = jnp.sum(dn * t * r, axis=0, keepdims=True)
    dg = dgw[:, :LANES]
    for c in range(LANES, w, LANES):
        dg = dg + dgw[:, c:c + LANES]
    return dt, dg


def _gqa_prep(q_raw, kv_raw, cs, gq2, gk2, seg, swap, name):
    kw = GQA_KV_HEADS * GQA_HEAD_DIM

    def fn(qv, kvv, csv, gqv, gkv, sg, sw):
        qn, _ = _head_norm(qv, gqv, sg)
        kn, _ = _head_norm(kvv[:, :kw], gkv, sg)
        return _rope(qn, csv, sw) * (GQA_SCALE * LOG2E), _rope(kn, csv, sw), kvv[:, kw:]
    return _rows_call(fn, [q_raw, kv_raw, cs], [gq2, gk2, seg, swap],
                      [(GQA_HEADS * GQA_HEAD_DIM, BF16), (kw, BF16), (kw, BF16)], name=name)


def _gqa_prep_bwd(q_raw, kv_raw, cs, dqh, dkh, dv, gq2, gk2, seg, swap, name):
    kw = GQA_KV_HEADS * GQA_HEAD_DIM

    def fn(qv, kvv, csv, dq, dk, dvv, gqv, gkv, sg, sw):
        dqr, dgq = _head_norm_bwd(qv, gqv, sg, _rope_t(dq * GQA_SCALE, csv, sw))
        dkr, dgk = _head_norm_bwd(kvv[:, :kw], gkv, sg, _rope_t(dk * LN2, csv, sw))
        return dqr, jnp.concatenate([dkr, dvv], axis=1), dgq, dgk
    return _rows_call(fn, [q_raw, kv_raw, cs, dqh, dkh, dv], [gq2, gk2, seg, swap],
                      [(GQA_HEADS * GQA_HEAD_DIM, BF16), (2 * kw, BF16)], [(1, LANES), (1, LANES)], name=name)


def _loss_head(x, target, g, name):
    dm = x.shape[1]

    def fn(xv, tv, gv):
        err = _rms(xv, gv) - tv
        loss = 0.5 * jnp.sum(err * err) / dm
        dx, dg = _rms_bwd_math(xv, gv, err * (1.0 / dm))
        return dx, jnp.zeros((1, LANES), F32) + loss, dg
    return _rows_call(fn, [x, target], [g], [(dm, F32)], [(1, LANES), (1, dm)], name=name)


def _adamw(w, g, m, v, name):
    def fn(wv, gv, mv, vv):
        m2 = ADAM_B1 * mv + (1.0 - ADAM_B1) * gv
        v2 = ADAM_B2 * vv + (1.0 - ADAM_B2) * (gv * gv)
        m_hat = m2 / (1.0 - ADAM_B1 ** ADAM_STEP)
        v_hat = v2 / (1.0 - ADAM_B2 ** ADAM_STEP)
        return -ADAM_LR * (m_hat / (jnp.sqrt(v_hat) + ADAM_EPS) + ADAM_WD * wv), m2, v2
    c = w.shape[1]
    return _rows_call(fn, [w, g, m, v], [], [(c, F32)] * 3, bs=_pick(w.shape[0], 256, 8), name=name)


HBM_SPEC = pl.BlockSpec(memory_space=pltpu.HBM)
VMEM_SPEC = pl.BlockSpec(memory_space=pltpu.VMEM)


def _position():
    return lax.axis_index("x"), lax.axis_index("y"), lax.axis_index("c")


def _other_chips(x, y):
    return [(1 - x, y), (x, 1 - y), (1 - x, 1 - y)]


HALF_W = PACK_W // 2


def _cols(c):
    return pl.ds(pl.multiple_of(c * HALF_W, HALF_W), HALF_W)


def _all_gather_weights(packed):
    rows = packed.shape[0]

    def body(p_ref, g_ref, send_sems, recv_sems, local_sem):
        x, y, c = _position()
        chips = _other_chips(x, y)

        def half(chip, hc):
            return g_ref.at[2 * chip[0] + chip[1], :, _cols(hc)]

        def copy(j, src, dst, to):
            return pltpu.make_async_remote_copy(src_ref=src, dst_ref=dst, send_sem=send_sems.at[j],
                                                recv_sem=recv_sems.at[j], device_id=to, device_id_type=MESH)

        mine = pltpu.make_async_copy(p_ref, g_ref.at[2 * x + y], local_sem)
        mine.start()
        first = [copy(j, p_ref.at[:, _cols(c)], half((x, y), c), (*chip, c)) for j, chip in enumerate(chips)]
        for cp in first:
            cp.start()
        passed = [copy(3 + j, half(chip, c), half(chip, c), (x, y, 1 - c)) for j, chip in enumerate(chips)]
        for j, chip in enumerate(chips):
            copy(j, half(chip, c), half(chip, c), (x, y, c)).wait_recv()
            passed[j].start()
        for j, chip in enumerate(chips):
            copy(3 + j, half(chip, 1 - c), half(chip, 1 - c), (x, y, c)).wait_recv()
        for cp in first + passed:
            cp.wait_send()
        mine.wait()

    return pl.pallas_call(
        body, name="all_gather_weights", in_specs=[HBM_SPEC], out_specs=HBM_SPEC,
        out_shape=jax.ShapeDtypeStruct((4, rows, packed.shape[1]), packed.dtype),
        scratch_shapes=[pltpu.SemaphoreType.DMA((6,)), pltpu.SemaphoreType.DMA((6,)), pltpu.SemaphoreType.DMA],
    )(packed)


SEM_SPEC = pl.BlockSpec(memory_space=pltpu.SEMAPHORE)
ANY_SPEC = pl.BlockSpec(memory_space=pl.ANY)
DATAFLOW = pltpu.SideEffectType.DATAFLOW_SIDE_EFFECTING


def _hbm(a):
    return pltpu.with_memory_space_constraint(a, pltpu.HBM)


def _gather_start(packed, tag):
    rows = packed.shape[0]

    def body(p_ref, g_ref, send_sems, recv_sems, p_thru, g_thru, token):
        x, y, c = _position()
        for j, chip in enumerate(_other_chips(x, y)):
            for s in range(2):
                pltpu.make_async_remote_copy(
                    src_ref=p_ref.at[:, _cols(c)], dst_ref=g_ref.at[2 * x + y, :, _cols(c)],
                    send_sem=send_sems.at[2 * j + s], recv_sem=recv_sems.at[2 * j + s],
                    device_id=(*chip, 1 - c if s else c), device_id_type=MESH).start()
        token[...] = jnp.zeros_like(token)

    return pl.pallas_call(
        body, name=tag + "_start",
        out_shape=(pltpu.SemaphoreType.DMA((6,)), pltpu.SemaphoreType.DMA((6,)), pltpu.HBM(packed.shape, packed.dtype),
                   pltpu.HBM((4, rows, PACK_W), packed.dtype), jax.ShapeDtypeStruct((8, LANES), F32)),
        in_specs=(HBM_SPEC, HBM_SPEC), out_specs=(SEM_SPEC, SEM_SPEC, HBM_SPEC, HBM_SPEC, VMEM_SPEC),
        input_output_aliases={0: 2, 1: 3},
        compiler_params=pltpu.CompilerParams(has_side_effects=DATAFLOW),
    )(_hbm(packed), _hbm(lax.empty((4, rows, PACK_W), packed.dtype)))


def _gather_wait(send_sems, recv_sems, p_thru, g_thru, after, tag):
    def body(p_ref, g_ref, send_sems, recv_sems, after_ref, p_dead, got_ref):
        x, y, c = _position()
        for j, chip in enumerate(_other_chips(x, y)):
            for s in range(2):
                cp = pltpu.make_async_remote_copy(
                    src_ref=p_ref.at[:, _cols(c)], dst_ref=g_ref.at[2 * chip[0] + chip[1], :, _cols(1 - c if s else c)],
                    send_sem=send_sems.at[2 * j + s], recv_sem=recv_sems.at[2 * j + s],
                    device_id=(x, y, c), device_id_type=MESH)
                cp.wait_send()
                cp.wait_recv()

    return pl.pallas_call(
        body, name=tag + "_wait",
        out_shape=(pltpu.HBM(p_thru.shape, p_thru.dtype), pltpu.HBM(g_thru.shape, g_thru.dtype)),
        in_specs=(HBM_SPEC, HBM_SPEC, SEM_SPEC, SEM_SPEC, ANY_SPEC), out_specs=(HBM_SPEC, HBM_SPEC),
        input_output_aliases={0: 0, 1: 1},
        compiler_params=pltpu.CompilerParams(has_side_effects=DATAFLOW),
    )(p_thru, g_thru, send_sems, recv_sems, after)[1]


def _sibling_swap_halves(grads, tag):
    rows = grads.shape[1]

    def body(g_ref, a_ref, send_sem, recv_sem):
        x, y, c = _position()
        cp = pltpu.make_async_remote_copy(src_ref=g_ref.at[:, :, _cols(1 - c)], dst_ref=a_ref,
                                          send_sem=send_sem, recv_sem=recv_sem, device_id=(x, y, 1 - c),
                                          device_id_type=MESH)
        cp.start()
        cp.wait()

    return pl.pallas_call(
        body, name=tag + "_swap_cores", in_specs=[HBM_SPEC], out_specs=HBM_SPEC,
        out_shape=jax.ShapeDtypeStruct((4, rows, HALF_W), grads.dtype),
        scratch_shapes=[pltpu.SemaphoreType.DMA, pltpu.SemaphoreType.DMA],
    )(grads)


def _rs_block(rows):
    return max(d for d in range(16, 1601, 16) if rows % d == 0)


def _chip_sum(grads, other, c, tag):
    rows = other.shape[1]
    rb = _rs_block(rows)

    def body(c_ref, g_ref, a_ref, o_ref):
        o_ref[...] = (g_ref[...] + a_ref[...]).astype(o_ref.dtype)

    blk = (1, rb, HALF_W)
    return pl.pallas_call(
        body, name=tag + "_chip_sum",
        grid_spec=pltpu.PrefetchScalarGridSpec(
            num_scalar_prefetch=1, grid=(4, rows // rb),
            in_specs=[pl.BlockSpec(blk, lambda k, i, c_ref: (k, i, c_ref[0])),
                      pl.BlockSpec(blk, lambda k, i, c_ref: (k, i, 0))],
            out_specs=pl.BlockSpec(blk, lambda k, i, c_ref: (k, i, 0))),
        out_shape=jax.ShapeDtypeStruct(other.shape, BF16),
        compiler_params=_cparams(("parallel", "parallel")),
    )(jnp.reshape(c, (1,)).astype(jnp.int32), grads, other)


def _chip_copies(t_ref, b_ref, send_sems, recv_sems):
    x, y, c = _position()
    return [pltpu.make_async_remote_copy(src_ref=t_ref.at[2 * chip[0] + chip[1]], dst_ref=b_ref.at[j],
                                         send_sem=send_sems.at[j], recv_sem=recv_sems.at[j],
                                         device_id=(*chip, c), device_id_type=MESH)
            for j, chip in enumerate(_other_chips(x, y))]


def _send_chip_sums(sums, tag):
    def body(t_ref, b_ref, send_sems, recv_sems):
        copies = _chip_copies(t_ref, b_ref, send_sems, recv_sems)
        for cp in copies:
            cp.start()
        for cp in copies:
            cp.wait()

    return pl.pallas_call(
        body, name=tag + "_send_chips", in_specs=[HBM_SPEC], out_specs=HBM_SPEC,
        out_shape=jax.ShapeDtypeStruct((3,) + sums.shape[1:], sums.dtype),
        scratch_shapes=[pltpu.SemaphoreType.DMA((3,)), pltpu.SemaphoreType.DMA((3,))],
    )(sums)


def _send_chip_sums_start(sums, tag):
    land = (3,) + sums.shape[1:]

    def body(t_ref, b_ref, send_sems, recv_sems, t_thru, b_thru, token):
        for cp in _chip_copies(t_ref, b_ref, send_sems, recv_sems):
            cp.start()
        token[...] = jnp.zeros_like(token)

    return pl.pallas_call(
        body, name=tag + "_send_chips_start",
        out_shape=(pltpu.SemaphoreType.DMA((3,)), pltpu.SemaphoreType.DMA((3,)), pltpu.HBM(sums.shape, sums.dtype),
                   pltpu.HBM(land, sums.dtype), jax.ShapeDtypeStruct((8, LANES), F32)),
        in_specs=(HBM_SPEC, HBM_SPEC), out_specs=(SEM_SPEC, SEM_SPEC, HBM_SPEC, HBM_SPEC, VMEM_SPEC),
        input_output_aliases={0: 2, 1: 3},
        compiler_params=pltpu.CompilerParams(has_side_effects=DATAFLOW),
    )(_hbm(sums), _hbm(lax.empty(land, sums.dtype)))


def _send_chip_sums_wait(send_sems, recv_sems, t_thru, b_thru, after, tag):
    def body(t_ref, b_ref, send_sems, recv_sems, after_ref, t_dead, got_ref):
        for cp in _chip_copies(t_ref, b_ref, send_sems, recv_sems):
            cp.wait_send()
            cp.wait_recv()

    return pl.pallas_call(
        body, name=tag + "_send_chips_wait",
        out_shape=(pltpu.HBM(t_thru.shape, t_thru.dtype), pltpu.HBM(b_thru.shape, b_thru.dtype)),
        in_specs=(HBM_SPEC, HBM_SPEC, SEM_SPEC, SEM_SPEC, ANY_SPEC), out_specs=(HBM_SPEC, HBM_SPEC),
        input_output_aliases={0: 0, 1: 1},
        compiler_params=pltpu.CompilerParams(has_side_effects=DATAFLOW),
    )(t_thru, b_thru, send_sems, recv_sems, after)[1]


def _final_sum(grads, other, recv, k, c, tag):
    rows = other.shape[1]
    rb = _rs_block(rows)

    def body(k_ref, c_ref, g_ref, a_ref, b_ref, o_ref):
        own = g_ref[0] + a_ref[0]
        o_ref[...] = ((own + b_ref[0].astype(F32)) + b_ref[1].astype(F32)) + b_ref[2].astype(F32)

    return pl.pallas_call(
        body, name=tag + "_final_sum",
        grid_spec=pltpu.PrefetchScalarGridSpec(
            num_scalar_prefetch=2, grid=(rows // rb,),
            in_specs=[pl.BlockSpec((1, rb, HALF_W), lambda i, k_ref, c_ref: (k_ref[0], i, c_ref[0])),
                      pl.BlockSpec((1, rb, HALF_W), lambda i, k_ref, c_ref: (k_ref[0], i, 0)),
                      pl.BlockSpec((3, rb, HALF_W), lambda i, k_ref, c_ref: (0, i, 0))],
            out_specs=pl.BlockSpec((rb, HALF_W), lambda i, k_ref, c_ref: (i, 0))),
        out_shape=jax.ShapeDtypeStruct((rows, HALF_W), F32),
        compiler_params=_cparams(("parallel",)),
    )(jnp.reshape(k, (1,)).astype(jnp.int32), jnp.reshape(c, (1,)).astype(jnp.int32), grads, other, recv)


def _join_halves(half, core, tag):
    def body(h_ref, o_ref, send_sem, recv_sem):
        x, y, c = _position()
        cp = pltpu.make_async_remote_copy(src_ref=h_ref, dst_ref=o_ref, send_sem=send_sem, recv_sem=recv_sem,
                                          device_id=(x, y, 1 - c), device_id_type=MESH)
        cp.start()
        cp.wait()

    other = pl.pallas_call(
        body, name=tag + "_join_cores", in_specs=[HBM_SPEC], out_specs=HBM_SPEC,
        out_shape=jax.ShapeDtypeStruct(half.shape, half.dtype),
        scratch_shapes=[pltpu.SemaphoreType.DMA, pltpu.SemaphoreType.DMA],
    )(half)
    first = core == 0
    return jnp.concatenate([jnp.where(first, half, other), jnp.where(first, other, half)], axis=1)


def _all_reduce_packet(packet):
    rows = packet.shape[0]

    def body(p_ref, o_ref, buf, send_sems, recv_sems):
        x, y, c = _position()
        me = 4 * x + 2 * y + c
        buf[me] = p_ref[...]

        def flip(v, bit):
            return 1 - v if bit else v

        for p in range(1, 8):
            peer = (flip(x, p & 4), flip(y, p & 2), flip(c, p & 1))
            pltpu.make_async_remote_copy(src_ref=p_ref, dst_ref=buf.at[me], send_sem=send_sems.at[p - 1],
                                         recv_sem=recv_sems.at[p - 1], device_id=peer, device_id_type=MESH).start()
        for p in range(1, 8):
            peer = (flip(x, p & 4), flip(y, p & 2), flip(c, p & 1))
            slot = 4 * peer[0] + 2 * peer[1] + peer[2]
            cp = pltpu.make_async_remote_copy(src_ref=p_ref, dst_ref=buf.at[slot], send_sem=send_sems.at[p - 1],
                                              recv_sem=recv_sems.at[p - 1], device_id=peer, device_id_type=MESH)
            cp.wait_recv()
            cp.wait_send()
        acc = buf[0]
        for dev in range(1, 8):
            acc = acc + buf[dev]
        o_ref[...] = acc

    return pl.pallas_call(
        body, name="all_reduce_packet", in_specs=[VMEM_SPEC], out_specs=VMEM_SPEC,
        out_shape=jax.ShapeDtypeStruct(packet.shape, F32),
        scratch_shapes=[pltpu.VMEM((8, rows, LANES), F32), pltpu.SemaphoreType.DMA((7,)),
                        pltpu.SemaphoreType.DMA((7,))],
    )(packet)


def _stack_range(name, n_stack, pack):
    if name.startswith('gqa'):
        return (0, 0) if pack == 0 else (0, n_stack)
    return (0, 1) if pack == 0 else (1, n_stack)


def _pack_members(pack):
    out = []
    for n, shape, ax in BIG:
        lo, hi = _stack_range(n, shape[0], pack)
        if hi > lo:
            out.append((n, (hi - lo,) + shape[1:], ax, (lo, hi)))
    return out


PACK_ROW_MULTIPLE = 512


def _pad_rows(parts, dtype):
    rows = sum(p.shape[0] for p in parts)
    pad = -rows % PACK_ROW_MULTIPLE
    return jnp.concatenate(parts + ([jnp.zeros((pad, PACK_W), dtype)] if pad else []), axis=0)


def _pack_blocks(blocks, dtype, pack):
    return _pad_rows([blocks[n][lo:hi].astype(dtype).reshape(-1, PACK_W)
                      for n, _, _, (lo, hi) in _pack_members(pack)], dtype)


def _unpack_blocks(packed, pack):
    out, off = {}, 0
    for n, shape, _, _ in _pack_members(pack):
        r = math.prod(shape) // PACK_W
        out[n] = packed[off:off + r].reshape(shape)
        off += r
    return out


def _unpack_gathered(gathered, pack, own=None, chip=None):
    blocks = [gathered[k] if own is None else jnp.where(chip == k, own, gathered[k]) for k in range(4)]
    per_chip = [_unpack_blocks(blocks[k], pack) for k in range(4)]
    return {n: jnp.concatenate([per_chip[k][n] for k in range(4)], axis=ax) for n, _, ax, _ in _pack_members(pack)}


def _pack_full(full, dtype, pack):
    chips = []
    for k in range(4):
        parts = []
        for n, shape, ax, _ in _pack_members(pack):
            blk = lax.slice_in_dim(full[n], k * shape[ax], (k + 1) * shape[ax], axis=ax)
            parts.append(blk.astype(dtype).reshape(-1, PACK_W))
        chips.append(_pad_rows(parts, dtype))
    return jnp.stack(chips, axis=0)


def _pack_small(vals, loss_row):
    rows = [loss_row.reshape(1, LANES)]
    for n, shape in SMALL:
        v = vals.get(n)
        v = jnp.zeros(shape, F32) if v is None else v
        rows.append(v.astype(F32).reshape(-1, LANES))
    packet = jnp.concatenate(rows, axis=0)
    return jnp.pad(packet, ((0, PACKET_ROWS - packet.shape[0]), (0, 0)))


def _unpack_small(packet):
    out, off = {}, 1
    for n, shape in SMALL:
        r = math.prod(shape) // LANES
        out[n] = packet[off:off + r].reshape(shape)
        off += r
    return packet[0, 0], out


_MLA = dict(R=1, dk=MLA_DK, dv=MLA_V, hb=2, bq=512, bk=512)
_MLA_FWD_BQ = 1024
_BWD_UNROLL = 2
_GQA = dict(R=GQA_HEADS // GQA_KV_HEADS, dk=GQA_HEAD_DIM, dv=GQA_HEAD_DIM, hb=2, bq=256, bk=512)


def _layer_params(layer, full, gains):
    pack = 0 if layer == 0 else 1
    i = layer // 2

    def mat(name):
        lo, _ = _stack_range(name, 4 if name.startswith('ffn') else 2, pack)
        return full[name][(layer if name.startswith('ffn') else i) - lo]

    p = dict(ffn_norm=gains['ffn_norm'][layer][None], ffn_w_in=mat('ffn_w_in'), ffn_w_out=mat('ffn_w_out'))
    if layer % 2 == 0:
        w_in = mat('w_in_ab')
        zeros = jnp.zeros((D_MODEL, 32), w_in.dtype)
        p['w_a'] = jnp.concatenate([w_in[:, :640], zeros, zeros, w_in[:, 640:IN_A], zeros], axis=1)
        p['w_b'] = w_in[:, IN_A:]
        p['w_uq'] = jnp.pad(mat('mla_w_uq'), ((0, 0), (0, 0), (0, MLA_DK - 96))).reshape(MLA_Q_RANK, -1)
        ukv = mat('mla_w_ukv')
        p['w_uk'] = jnp.pad(ukv[:, :, :MLA_NOPE], ((0, 0), (0, 0), (0, MLA_DK - MLA_NOPE))).reshape(MLA_KV_RANK, -1)
        p['w_uv'] = ukv[:, :, MLA_NOPE:].reshape(MLA_KV_RANK, -1)
        p['w_out'] = mat('w_out_ab')
        p['mix_norm'] = gains['mix_norm_ab'][i][None]
        p['q_norm'] = gains['mla_q_norm'][i][None]
        p['kv_norm'] = gains['mla_kv_norm'][i][None]
    else:
        p['w_q'], p['w_kv'], p['w_o'] = mat('gqa_w_q'), mat('gqa_w_kv'), mat('gqa_w_o')
        p['mix_norm'] = gains['mix_norm_c'][i][None]
        p['q_norm'] = jnp.tile(gains['gqa_q_norm'][i][None], (1, 2))
        p['k_norm'] = jnp.tile(gains['gqa_k_norm'][i][None], (1, 2))
    return p


def _even_fwd(x, p, cs, swap, tag):
    xn = _rms_fwd(x, p['mix_norm'], tag + "_norm")
    za = _mm(xn, p['w_a'], name=tag + "_in_a")
    zb = _mm(xn, p['w_b'], out_dtype=BF16, name=tag + "_in_b")
    cq, ckv, krr = _mla_prep(za, p['q_norm'], p['kv_norm'], cs, swap, tag + "_mla_prep")
    q_raw = _mm(cq, p['w_uq'], name=tag + "_uq")
    k_pad = _mm(ckv, p['w_uk'], name=tag + "_uk")
    v = _mm(ckv, p['w_uv'], out_dtype=BF16, name=tag + "_uv")
    qh, kh = _mla_qk(q_raw, k_pad, krr, cs, swap, tag + "_mla_qk")
    o_a, lse_a = _flash_fwd(qh, kh, v, name=tag + "_mla_attn", **dict(_MLA, bq=_MLA_FWD_BQ))
    og, lg = [], []
    for grp in range(DIL_GROUPS):
        o, l = _dil_fwd(zb, grp, f"{tag}_dil{grp}")
        og.append(o)
        lg.append(l)
    o_b, lt = _dil_combine(og, lg, tag + "_dil_merge")
    ocat = jnp.concatenate([o_a, o_b], axis=1)
    x1 = _mm(ocat, p['w_out'], add=x, name=tag + "_out")
    saved = dict(x=x, xn=xn, za=za, zb=zb, cq=cq, ckv=ckv, qh=qh, kh=kh, v=v, lse_a=lse_a, og=og, lg=lg, lt=lt,
                 ocat=ocat)
    return x1, saved


def _even_bwd(dx1, p, sv, cs, swap, seg64, tag):
    docat = _mm(dx1, p['w_out'], mode="nt", name=tag + "_out_dx")
    d_w_out = _mm(sv['ocat'], dx1, mode="tn", name=tag + "_out_dw")
    n_a = MLA_HEADS * MLA_V
    do_a = docat[:, :n_a].astype(BF16)
    res = _dil_combine_bwd(docat[:, n_a:], sv['og'], sv['lg'], sv['lt'], seg64, tag + "_dil_merge_bwd")
    dqs, dks, dvs = [], [], []
    for grp in range(DIL_GROUPS):
        dq, dk, dv = _dil_bwd(sv['zb'], res[grp], sv['lg'][grp], res[3 + grp], grp, f"{tag}_dil{grp}_bwd")
        dqs.append(dq)
        dks.append(dk)
        dvs.append(dv)
    dzb = jnp.concatenate(dqs + dks + dvs, axis=1).astype(BF16)
    dqh, dkh, dv = _flash_bwd(sv['qh'], sv['kh'], sv['v'], sv['ocat'][:, :n_a], do_a, sv['lse_a'],
                              name=tag + "_mla_attn_bwd", unroll=_BWD_UNROLL, **_MLA)
    dq_raw, dkh, dkrr = _mla_qk_bwd(dqh, dkh, cs, swap, tag + "_mla_qk_bwd")
    dcq = _mm(dq_raw, p['w_uq'], mode="nt", name=tag + "_uq_dx")
    d_w_uq = _mm(sv['cq'], dq_raw, mode="tn", name=tag + "_uq_dw")
    dckv = _mm(dkh, p['w_uk'], mode="nt", name=tag + "_uk_dx")
    dckv = _mm(dv, p['w_uv'], mode="nt", add=dckv, name=tag + "_uv_dx")
    d_w_uk = _mm(sv['ckv'], dkh, mode="tn", name=tag + "_uk_dw")
    d_w_uv = _mm(sv['ckv'], dv, mode="tn", name=tag + "_uv_dw")
    dza, d_gq, d_gkv = _mla_prep_bwd(sv['za'], cs, dcq, dckv, dkrr, p['q_norm'], p['kv_norm'], swap,
                                     tag + "_mla_prep_bwd")
    dxn = _mm(dza, p['w_a'], mode="nt", name=tag + "_in_a_dx")
    dxn = _mm(dzb, p['w_b'], mode="nt", add=dxn, name=tag + "_in_b_dx")
    d_w_a = _mm(sv['xn'], dza, mode="tn", name=tag + "_in_a_dw")
    d_w_b = _mm(sv['xn'], dzb, mode="tn", name=tag + "_in_b_dw")
    dx, d_g = _rms_bwd(sv['x'], p['mix_norm'], dxn, dx1, tag + "_norm_bwd")
    d_w_in = jnp.concatenate([d_w_a[:, :640], d_w_a[:, 704:736], d_w_b], axis=1)
    d_uq = d_w_uq.reshape(MLA_Q_RANK, MLA_HEADS, MLA_DK)[:, :, :MLA_NOPE + MLA_ROPE]
    d_ukv = jnp.concatenate([d_w_uk.reshape(MLA_KV_RANK, MLA_HEADS, MLA_DK)[:, :, :MLA_NOPE],
                             d_w_uv.reshape(MLA_KV_RANK, MLA_HEADS, MLA_V)], axis=2)
    grads = dict(w_in_ab=d_w_in, mla_w_uq=d_uq, mla_w_ukv=d_ukv, w_out_ab=d_w_out, mix_norm_ab=d_g[0],
                 mla_q_norm=d_gq[0], mla_kv_norm=d_gkv[0])
    return dx, grads


def _odd_fwd(x, p, cs, seg64, swap, tag):
    xn = _rms_fwd(x, p['mix_norm'], tag + "_norm")
    q_raw = _mm(xn, p['w_q'], name=tag + "_q")
    kv_raw = _mm(xn, p['w_kv'], name=tag + "_kv")
    qh, kh, v = _gqa_prep(q_raw, kv_raw, cs, p['q_norm'], p['k_norm'], seg64, swap, tag + "_gqa_prep")
    o, lse = _flash_fwd(qh, kh, v, name=tag + "_gqa_attn", **_GQA)
    x1 = _mm(o, p['w_o'], add=x, name=tag + "_o")
    return x1, dict(x=x, xn=xn, q_raw=q_raw, kv_raw=kv_raw, qh=qh, kh=kh, v=v, o=o, lse=lse)


def _odd_bwd(dx1, p, sv, cs, seg64, swap, tag):
    do = _mm(dx1, p['w_o'], mode="nt", out_dtype=BF16, name=tag + "_o_dx")
    d_w_o = _mm(sv['o'], dx1, mode="tn", name=tag + "_o_dw")
    dqh, dkh, dv = _flash_bwd(sv['qh'], sv['kh'], sv['v'], sv['o'], do, sv['lse'], name=tag + "_gqa_attn_bwd",
                              unroll=_BWD_UNROLL, **_GQA)
    dq_raw, dkv_raw, d_gq, d_gk = _gqa_prep_bwd(sv['q_raw'], sv['kv_raw'], cs, dqh, dkh, dv, p['q_norm'],
                                                p['k_norm'], seg64, swap, tag + "_gqa_prep_bwd")
    dxn = _mm(dq_raw, p['w_q'], mode="nt", name=tag + "_q_dx")
    dxn = _mm(dkv_raw, p['w_kv'], mode="nt", add=dxn, name=tag + "_kv_dx")
    d_w_q = _mm(sv['xn'], dq_raw, mode="tn", name=tag + "_q_dw")
    d_w_kv = _mm(sv['xn'], dkv_raw, mode="tn", name=tag + "_kv_dw")
    dx, d_g = _rms_bwd(sv['x'], p['mix_norm'], dxn, dx1, tag + "_norm_bwd")
    grads = dict(gqa_w_q=d_w_q, gqa_w_kv=d_w_kv, gqa_w_o=d_w_o, mix_norm_c=d_g[0],
                 gqa_q_norm=d_gq[0, :GQA_HEAD_DIM] + d_gq[0, GQA_HEAD_DIM:],
                 gqa_k_norm=d_gk[0, :GQA_HEAD_DIM] + d_gk[0, GQA_HEAD_DIM:])
    return dx, grads


def _ffn_fwd(x, p, tag):
    xn = _rms_fwd(x, p['ffn_norm'], tag + "_ffn_norm")
    h = _mm(xn, p['ffn_w_in'], out_dtype=BF16, name=tag + "_ffn_in")
    x2 = _mm(h, p['ffn_w_out'], gated=True, add=x, name=tag + "_ffn_out")
    return x2, dict(x=x, xn=xn, h=h)


def _ffn_bwd(dx2, p, sv, tag):
    d_gate, d_up = _mm(dx2, p['ffn_w_out'], mode="nt", gate_up=sv['h'], out_dtype=BF16, name=tag + "_ffn_out_dx")
    d_w_out = _mm(sv['h'], dx2, mode="tn", gated=True, name=tag + "_ffn_out_dw")
    d_w_in = jnp.concatenate([_mm(sv['xn'], d_gate, mode="tn", name=tag + "_ffn_in_dw_gate"),
                              _mm(sv['xn'], d_up, mode="tn", name=tag + "_ffn_in_dw_up")], axis=1)
    dxn = _mm(d_gate, p['ffn_w_in'], mode="nt", b_part=(0, 2), name=tag + "_ffn_in_dx_gate")
    dxn = _mm(d_up, p['ffn_w_in'], mode="nt", b_part=(1, 2), add=dxn, name=tag + "_ffn_in_dx_up")
    dx, d_g = _rms_bwd(sv['x'], p['ffn_norm'], dxn, dx2, tag + "_ffn_norm_bwd")
    return dx, d_w_in, d_w_out, d_g[0]


EVEN_MATS = ('w_in_ab', 'mla_w_uq', 'mla_w_ukv', 'w_out_ab')
ODD_MATS = ('gqa_w_q', 'gqa_w_kv', 'gqa_w_o')
FFN_MATS = ('ffn_w_in', 'ffn_w_out')


def _schedule(x, target, gains, full_of_pack, rest_grads_ready):
    s = x.shape[0]
    cs_mla, cs_gqa = _rope_tables(s)
    swap, seg64 = _swap_matrix(), _seg_matrix(GQA_HEAD_DIM)
    params, saved, full = [], [], None
    for layer in range(4):
        tag = f"l{layer}"
        if layer < 2:
            full = full_of_pack(layer, x)
        p = _layer_params(layer, full, gains)
        if layer % 2 == 0:
            x, sv = _even_fwd(x, p, cs_mla, swap, tag)
        else:
            x, sv = _odd_fwd(x, p, cs_gqa, seg64, swap, tag)
        x, sv_f = _ffn_fwd(x, p, tag)
        params.append(p)
        saved.append((sv, sv_f))
    dx, loss_row, d_final = _loss_head(x, target, gains['final_norm'][None], "loss_head")

    per_layer, rest = {}, None
    for layer in reversed(range(4)):
        p, (sv, sv_f), tag = params[layer], saved[layer], f"l{layer}"
        if layer == 0:
            rest = {n: per_layer[2][n][None] for n in EVEN_MATS}
            rest.update({n: jnp.stack([per_layer[1][n], per_layer[3][n]], axis=0) for n in ODD_MATS})
            rest.update({n: jnp.stack([per_layer[l][n] for l in (1, 2, 3)], axis=0) for n in FFN_MATS})
            token = rest_grads_ready(rest)
            if token is not None:
                p = dict(p, ffn_w_out=p['ffn_w_out'] + token[0, 0].astype(p['ffn_w_out'].dtype))
        dx, d_ffn_in, d_ffn_out, d_ffn_g = _ffn_bwd(dx, p, sv_f, tag)
        if layer % 2 == 0:
            dx, g = _even_bwd(dx, p, sv, cs_mla, swap, seg64, tag)
        else:
            dx, g = _odd_bwd(dx, p, sv, cs_gqa, seg64, swap, tag)
        g.update(ffn_w_in=d_ffn_in, ffn_w_out=d_ffn_out, ffn_norm=d_ffn_g)
        per_layer[layer] = g

    first = {n: per_layer[0][n][None] for n in EVEN_MATS + FFN_MATS}
    small = {'final_norm': d_final[0], 'ffn_norm': jnp.stack([per_layer[l]['ffn_norm'] for l in range(4)], axis=0)}
    for n in ('mix_norm_ab', 'mla_q_norm', 'mla_kv_norm'):
        small[n] = jnp.stack([per_layer[0][n], per_layer[2][n]], axis=0)
    for n in ('mix_norm_c', 'gqa_q_norm', 'gqa_k_norm'):
        small[n] = jnp.stack([per_layer[1][n], per_layer[3][n]], axis=0)
    return loss_row, dx, first, rest, small


def _core_sums(grads, pack, core, tag):
    packed = _pack_full(grads, F32, pack)
    other = _sibling_swap_halves(packed, tag)
    return packed, other, _chip_sum(packed, other, core, tag)


def _finish_reduce_scatter(packed, other, recv, pack, chip, core, tag):
    return _unpack_blocks(_join_halves(_final_sum(packed, other, recv, chip, core, tag), core, tag), pack)


def _step(x, target, w, m, v):
    big_names = [n for n, _, _ in BIG]
    chip = 2 * lax.axis_index("x") + lax.axis_index("y")
    core = lax.axis_index("c")

    gains = {n: w[n] for n, _ in SMALL if n != 'mix_norm_c'}
    c_cols = w['mix_norm_c'].shape[1]
    own_c = lax.dynamic_update_slice(jnp.zeros((2, 4 * c_cols), F32), w['mix_norm_c'], (0, chip * c_cols))
    gains['mix_norm_c'] = _unpack_small(_all_reduce_packet(_pack_small(
        {'mix_norm_c': own_c * 0.5}, jnp.zeros((LANES,), F32))))[1]['mix_norm_c']

    gathered0 = _all_gather_weights(_pack_blocks(w, BF16, 0))
    packed1, gathered0 = lax.optimization_barrier((_pack_blocks(w, BF16, 1), gathered0))
    ag_send, ag_recv, p_thru, g_thru, ag_token = _gather_start(packed1, "gather_rest")
    gains['mix_norm_ab'] = gains['mix_norm_ab'] + ag_token[0, 0]
    full0 = _unpack_gathered(gathered0, 0)

    def full_of_pack(pack, after):
        if pack == 0:
            return full0
        landed = _gather_wait(ag_send, ag_recv, p_thru, g_thru, after, "gather_rest")
        return _unpack_gathered(landed, 1, own=packed1, chip=chip)

    rs = {}

    def rest_grads_ready(rest):
        rs['packed'], rs['other'], sums = _core_sums(rest, 1, core, "grad_rest")
        rs['send'], rs['recv'], rs['t'], rs['b'], token = _send_chip_sums_start(sums, "grad_rest")
        return token

    loss_row, dx, first, rest, small = _schedule(x[0], target[0], gains, full_of_pack, rest_grads_ready)
    recv1 = _send_chip_sums_wait(rs['send'], rs['recv'], rs['t'], rs['b'], dx, "grad_rest")
    g_rest = _finish_reduce_scatter(rs['packed'], rs['other'], recv1, 1, chip, core, "grad_rest")
    packed0, other0, sums0 = _core_sums(first, 0, core, "grad_first")
    g_first = _finish_reduce_scatter(packed0, other0, _send_chip_sums(sums0, "grad_first"), 0, chip, core,
                                     "grad_first")
    g_blocks = {n: (jnp.concatenate([g_first[n], g_rest[n]], axis=0) if n in g_first else g_rest[n])
                for n in big_names}

    loss, g_small = _unpack_small(_all_reduce_packet(_pack_small(small, loss_row[0])))
    g_small['mix_norm_c'] = lax.dynamic_slice(g_small['mix_norm_c'], (0, chip * c_cols), (2, c_cols))

    out_g, out_d, out_m, out_v = {}, {}, {}, {}
    for n in big_names:
        shape = w[n].shape
        cols = shape[-1]
        d_, m_, v_ = _adamw(w[n].reshape(-1, cols), g_blocks[n].reshape(-1, cols), m[n].reshape(-1, cols),
                            v[n].reshape(-1, cols), "adamw_" + n)
        out_g[n], out_d[n], out_m[n], out_v[n] = g_blocks[n], d_.reshape(shape), m_.reshape(shape), v_.reshape(shape)
    for n, _ in SMALL:
        shape = w[n].shape
        as2d = (lambda t: t.reshape(1, -1)) if len(shape) == 1 else (lambda t: t)
        d_, m_, v_ = _adamw(as2d(w[n]), as2d(g_small[n]), as2d(m[n]), as2d(v[n]), "adamw_" + n)
        out_g[n], out_d[n], out_m[n], out_v[n] = g_small[n], d_.reshape(shape), m_.reshape(shape), v_.reshape(shape)
    return (loss, dx[None], *[out_g[n] for n in WEIGHTS], *[out_d[n] for n in WEIGHTS],
            *[out_m[n] for n in WEIGHTS], *[out_v[n] for n in WEIGHTS])


def kernel(x, mix_norm_ab, w_in_ab, mla_q_norm, mla_kv_norm, mla_w_uq, mla_w_ukv, w_out_ab, mix_norm_c, gqa_w_q, gqa_w_kv, gqa_q_norm, gqa_k_norm, gqa_w_o, ffn_norm, ffn_w_in, ffn_w_out, final_norm, loss_target, m_mix_norm_ab, m_w_in_ab, m_mla_q_norm, m_mla_kv_norm, m_mla_w_uq, m_mla_w_ukv, m_w_out_ab, m_mix_norm_c, m_gqa_w_q, m_gqa_w_kv, m_gqa_q_norm, m_gqa_k_norm, m_gqa_w_o, m_ffn_norm, m_ffn_w_in, m_ffn_w_out, m_final_norm, v_mix_norm_ab, v_w_in_ab, v_mla_q_norm, v_mla_kv_norm, v_mla_w_uq, v_mla_w_ukv, v_w_out_ab, v_mix_norm_c, v_gqa_w_q, v_gqa_w_kv, v_gqa_q_norm, v_gqa_k_norm, v_gqa_w_o, v_ffn_norm, v_ffn_w_in, v_ffn_w_out, v_final_norm):
    w = dict(zip(WEIGHTS, (mix_norm_ab, w_in_ab, mla_q_norm, mla_kv_norm, mla_w_uq, mla_w_ukv, w_out_ab, mix_norm_c,
                           gqa_w_q, gqa_w_kv, gqa_q_norm, gqa_k_norm, gqa_w_o, ffn_norm, ffn_w_in, ffn_w_out,
                           final_norm)))
    m = dict(zip(WEIGHTS, (m_mix_norm_ab, m_w_in_ab, m_mla_q_norm, m_mla_kv_norm, m_mla_w_uq, m_mla_w_ukv,
                           m_w_out_ab, m_mix_norm_c, m_gqa_w_q, m_gqa_w_kv, m_gqa_q_norm, m_gqa_k_norm, m_gqa_w_o,
                           m_ffn_norm, m_ffn_w_in, m_ffn_w_out, m_final_norm)))
    v = dict(zip(WEIGHTS, (v_mix_norm_ab, v_w_in_ab, v_mla_q_norm, v_mla_kv_norm, v_mla_w_uq, v_mla_w_ukv,
                           v_w_out_ab, v_mix_norm_c, v_gqa_w_q, v_gqa_w_kv, v_gqa_q_norm, v_gqa_k_norm, v_gqa_w_o,
                           v_ffn_norm, v_ffn_w_in, v_ffn_w_out, v_final_norm)))
    return _step(x, loss_target, w, m, v)
```

```python
import math

import numpy as np
import jax
import jax.numpy as jnp
from jax import lax
from jax.experimental import pallas as pl
from jax.experimental.pallas import tpu as pltpu

F32 = jnp.float32
BF16 = jnp.bfloat16
MESH = pl.DeviceIdType.MESH

VMEM_LIMIT_BYTES = 56 * 1024 * 1024
LANES = 128

D_MODEL = 1024
NORM_EPS = 1e-6
ROPE_THETA = 10000.0
NEG_INF = -1e30
GRID_W = 64

MLA_HEADS, MLA_Q_RANK, MLA_KV_RANK, MLA_NOPE, MLA_ROPE, MLA_V = 8, 384, 256, 64, 32, 64
MLA_DK = 128
DIL_PAIRS = ((128, 1), (512, 4), (2048, 16))
DIL_HALF, DIL_SLOTS, DIL_GROUPS, DIL_HEAD_DIM = 64, 4, 3, 64
DIL_HEADS = DIL_SLOTS * DIL_GROUPS
DIL_W = DIL_SLOTS * DIL_HEAD_DIM
GQA_HEADS, GQA_KV_HEADS, GQA_HEAD_DIM = 16, 4, 64
FFN_HIDDEN = 2816
IN_A = MLA_Q_RANK + MLA_KV_RANK + MLA_ROPE
IN_A_PAD = 768
IN_B = 3 * DIL_HEADS * DIL_HEAD_DIM

ADAM_LR, ADAM_B1, ADAM_B2, ADAM_EPS, ADAM_WD, ADAM_STEP = 0.001, 0.9, 0.999, 1e-08, 0.01, 10

LOG2E, LN2 = math.log2(math.e), math.log(2.0)
MLA_SCALE = (MLA_NOPE + MLA_ROPE) ** -0.5
GQA_SCALE = GQA_HEAD_DIM ** -0.5

WEIGHTS = ['mix_norm_ab', 'w_in_ab', 'mla_q_norm', 'mla_kv_norm', 'mla_w_uq', 'mla_w_ukv', 'w_out_ab', 'mix_norm_c',
           'gqa_w_q', 'gqa_w_kv', 'gqa_q_norm', 'gqa_k_norm', 'gqa_w_o', 'ffn_norm', 'ffn_w_in', 'ffn_w_out',
           'final_norm']
BIG = (('w_in_ab', (2, 1024, 744), 2), ('mla_w_uq', (2, 96, 8, 96), 1), ('mla_w_ukv', (2, 64, 8, 128), 1),
       ('w_out_ab', (2, 768, 256), 2), ('gqa_w_q', (2, 256, 1024), 1), ('gqa_w_kv', (2, 256, 512), 1),
       ('gqa_w_o', (2, 256, 1024), 1), ('ffn_w_in', (4, 1024, 1408), 2), ('ffn_w_out', (4, 704, 1024), 1))
PACK_W = 1024
SMALL = (('mix_norm_ab', (2, 1024)), ('mla_q_norm', (2, 384)), ('mla_kv_norm', (2, 256)), ('gqa_q_norm', (2, 64)),
         ('gqa_k_norm', (2, 64)), ('ffn_norm', (4, 1024)), ('final_norm', (1024,)), ('mix_norm_c', (2, 1024)))
PACKET_ROWS = 88


def _cparams(sem=None):
    return pltpu.CompilerParams(dimension_semantics=sem, vmem_limit_bytes=VMEM_LIMIT_BYTES)


def _pick(n, pref, mult=LANES):
    if n <= pref:
        return n
    for d in range(pref - pref % mult, 0, -mult):
        if n % d == 0:
            return d
    return n


_DIMS = {"nn": (((1,), (0,)), ((), ())), "nt": (((1,), (1,)), ((), ())), "tn": (((0,), (0,)), ((), ()))}


def _sigmoid(x):
    return 0.5 * (1.0 + jnp.tanh(0.5 * x))


def _silu_mul(gate, up):
    gate, up = gate.astype(F32), up.astype(F32)
    return gate * _sigmoid(gate) * up


def _silu_mul_bwd(gate, up, da):
    gate, up = gate.astype(F32), up.astype(F32)
    sig = _sigmoid(gate)
    silu = gate * sig
    return da * up * (sig + silu * (1.0 - sig)), da * silu


def _mm(a, b, *, mode="nn", add=None, out_dtype=F32, gated=False, gate_up=None, b_part=None, name="mm"):
    if mode == "nn":
        (m, k), (k2, n) = a.shape, b.shape
    elif mode == "nt":
        (m, k), (n, k2) = a.shape, b.shape
    else:
        (k, m), (k2, n) = a.shape, b.shape
    if gated:
        k, m = (k // 2, m) if mode == "nn" else (k, m // 2)
    if b_part is not None:
        assert mode == "nt" and k2 == k * b_part[1]
        k2 = k
    assert k == k2, (a.shape, b.shape, mode)
    if mode == "tn":
        deep = a.dtype == BF16 and b.dtype == BF16
        bm, bn, bk = _pick(m, 1408), _pick(n, 1024), _pick(k, 2048 if deep else 1024, 16)
    else:
        bm, bn, bk = _pick(m, 512, 16), _pick(n, 1408), _pick(k, 2816)
    nk = k // bk
    assert m % bm == 0 and n % bn == 0 and k % bk == 0
    has_add, has_gu = add is not None, gate_up is not None
    assert not (has_add and has_gu) and not (gated and mode == "nt") and not (has_gu and mode != "nt")
    n_in = 2 + int(gated) + int(has_add) + 2 * int(has_gu)

    def body(*refs):
        a_val = _silu_mul(refs[0][...], refs[1][...]) if gated else refs[0][...]
        b_ref = refs[1 + int(gated)]
        part = lax.dot_general(a_val.astype(BF16), b_ref[...].astype(BF16), _DIMS[mode],
                               preferred_element_type=F32)

        def finish(r):
            if has_gu:
                d_gate, d_up = _silu_mul_bwd(refs[2][...], refs[3][...], r)
                refs[n_in][...] = d_gate.astype(refs[n_in].dtype)
                refs[n_in + 1][...] = d_up.astype(refs[n_in + 1].dtype)
                return
            if has_add:
                r = r + refs[n_in - 1][...]
            refs[n_in][...] = r.astype(refs[n_in].dtype)

        if nk == 1:
            finish(part)
        else:
            acc_ref = refs[-1]
            kk = pl.program_id(2)

            @pl.when(kk == 0)
            def _():
                acc_ref[...] = part

            @pl.when(kk > 0)
            def _():
                acc_ref[...] += part

            @pl.when(kk == nk - 1)
            def _():
                finish(acc_ref[...])

    a_bytes, b_bytes = a.size * a.dtype.itemsize, b.size * b.dtype.itemsize
    n_outer = nk == 1 and (n // bn) * a_bytes + b_bytes < a_bytes + (m // bm) * b_bytes

    def at(f):
        return (lambda j, i, kk: f(i, j, kk)) if n_outer else f

    if mode == "nn":
        a_specs = [pl.BlockSpec((bm, bk), at(lambda i, j, kk, o=o: (i, kk + o))) for o in ((0, nk) if gated else (0,))]
        b_spec = pl.BlockSpec((bk, bn), at(lambda i, j, kk: (kk, j)))
    elif mode == "nt":
        a_specs = [pl.BlockSpec((bm, bk), at(lambda i, j, kk: (i, kk)))]
        b_off = 0 if b_part is None else b_part[0] * nk
        b_spec = pl.BlockSpec((bn, bk), at(lambda i, j, kk: (j, kk + b_off)))
    else:
        a_specs = [pl.BlockSpec((bk, bm), at(lambda i, j, kk, o=o: (kk, i + o)))
                   for o in ((0, m // bm) if gated else (0,))]
        b_spec = pl.BlockSpec((bk, bn), at(lambda i, j, kk: (kk, j)))
    o_spec = pl.BlockSpec((bm, bn), at(lambda i, j, kk: (i, j)))
    in_specs, args = a_specs + [b_spec], [a] * len(a_specs) + [b]
    if has_add:
        in_specs, args = in_specs + [o_spec], args + [add]
    if has_gu:
        in_specs += [o_spec, pl.BlockSpec((bm, bn), at(lambda i, j, kk: (i, j + n // bn)))]
        args += [gate_up, gate_up]
    out = jax.ShapeDtypeStruct((m, n), out_dtype)
    grid = (n // bn, m // bm, nk) if n_outer else (m // bm, n // bn, nk)
    return pl.pallas_call(
        body, name=name, grid=grid, in_specs=in_specs, out_specs=[o_spec, o_spec] if has_gu else o_spec,
        out_shape=[out, out] if has_gu else out,
        scratch_shapes=[pltpu.VMEM((bm, bn), F32)] if nk > 1 else [],
        compiler_params=_cparams(("parallel", "parallel", "arbitrary")),
    )(*args)


def _rows_call(fn, rows, consts, out_rows, out_accs=(), *, bs=256, name):
    s = rows[0].shape[0]
    bs = min(bs, s)
    assert s % bs == 0
    nr, nc, no, na = len(rows), len(consts), len(out_rows), len(out_accs)

    def body(*refs):
        vals = [r[...] for r in refs[:nr + nc]]
        outs = refs[nr + nc:]
        res = fn(*vals)
        if not isinstance(res, (tuple, list)):
            res = (res,)
        assert len(res) == no + na, (len(res), no, na)
        for r, v in zip(outs[:no], res[:no]):
            r[...] = v.astype(r.dtype)
        if na:
            i = pl.program_id(0)
            for r, v in zip(outs[no:], res[no:]):
                @pl.when(i == 0)
                def _(r=r, v=v):
                    r[...] = v

                @pl.when(i > 0)
                def _(r=r, v=v):
                    r[...] += v

    in_specs = [pl.BlockSpec((bs, a.shape[1]), lambda i: (i, 0)) for a in rows]
    in_specs += [pl.BlockSpec(c.shape, lambda i: (0, 0)) for c in consts]
    out_specs = [pl.BlockSpec((bs, c), lambda i: (i, 0)) for c, _ in out_rows]
    out_specs += [pl.BlockSpec(tuple(sh), lambda i: (0, 0)) for sh in out_accs]
    out_shape = [jax.ShapeDtypeStruct((s, c), dt) for c, dt in out_rows]
    out_shape += [jax.ShapeDtypeStruct(tuple(sh), F32) for sh in out_accs]
    res = pl.pallas_call(
        body, name=name, grid=(s // bs,), in_specs=in_specs, out_specs=out_specs, out_shape=out_shape,
        compiler_params=_cparams(("arbitrary",) if na else ("parallel",)),
    )(*rows, *consts)
    return res


def _rms(x, g):
    return x * lax.rsqrt(jnp.mean(x * x, axis=-1, keepdims=True) + NORM_EPS) * g


def _rms_bwd_math(x, g, dy):
    r = lax.rsqrt(jnp.mean(x * x, axis=-1, keepdims=True) + NORM_EPS)
    u = dy * g
    dx = r * u - x * (r * r * r) * jnp.mean(u * x, axis=-1, keepdims=True)
    dg = jnp.sum(dy * x * r, axis=0, keepdims=True)
    return dx, dg


NORM_ROWS = 512


def _rms_fwd(x, g, name):
    return _rows_call(lambda xv, gv: _rms(xv, gv), [x], [g], [(x.shape[1], BF16)], bs=NORM_ROWS, name=name)[0]


def _rms_bwd(x, g, dy, dres, name):
    def fn(xv, dyv, dresv, gv):
        dx, dg = _rms_bwd_math(xv, gv, dyv.astype(F32))
        return dx + dresv, dg
    return _rows_call(fn, [x, dy, dres], [g], [(x.shape[1], F32)], [(1, x.shape[1])], bs=NORM_ROWS, name=name)


def _chunkdot(x, m):
    outs = [jnp.dot(x[:, c:c + LANES], m, precision=lax.Precision.HIGHEST, preferred_element_type=F32)
            for c in range(0, x.shape[1], LANES)]
    return outs[0] if len(outs) == 1 else jnp.concatenate(outs, axis=1)


def _lanes(t, width):
    n = width // LANES
    return t if n == 1 else jnp.concatenate([t] * n, axis=1)


def _rope(x, cs, swap):
    w = x.shape[1]
    return x * _lanes(cs[:, :LANES], w) + _chunkdot(x, swap) * _lanes(cs[:, LANES:], w)


def _rope_t(dy, cs, swap):
    w = dy.shape[1]
    return dy * _lanes(cs[:, :LANES], w) + _chunkdot(dy * _lanes(cs[:, LANES:], w), swap)


def _swap_matrix():
    m = np.zeros((LANES, LANES), np.float32)
    for j in range(LANES):
        src = j + 16 if (j % 32) < 16 else j - 16
        m[src, j] = 1.0
    return jnp.asarray(m)


def _seg_matrix(seg):
    idx = np.arange(LANES) // seg
    return jnp.asarray((idx[:, None] == idx[None, :]).astype(np.float32))


def _rope_tables(s):
    pos = jnp.arange(s)

    def angles(p, dim):
        freqs = ROPE_THETA ** (-jnp.arange(0, dim, 2, dtype=F32) / dim)
        ang = p.astype(F32)[:, None] * freqs[None, :]
        return jnp.cos(ang), jnp.sin(ang)

    cos_t, sin_t = angles(pos, MLA_ROPE)
    one, zero = jnp.ones((s, 64), F32), jnp.zeros((s, 64), F32)
    mla = jnp.concatenate([one, cos_t, cos_t, one[:, :32], zero, -sin_t, sin_t, zero[:, :32]], axis=1)
    cos_r, sin_r = angles(pos // GRID_W, GQA_HEAD_DIM // 2)
    cos_c, sin_c = angles(pos % GRID_W, GQA_HEAD_DIM // 2)
    c64 = jnp.concatenate([cos_r, cos_r, cos_c, cos_c], axis=1)
    s64 = jnp.concatenate([-sin_r, sin_r, -sin_c, sin_c], axis=1)
    gqa = jnp.concatenate([c64, c64, s64, s64], axis=1)
    return mla, gqa


def _stack_heads(ref, heads, d, dtype=None):
    parts = [ref[:, hd * d:(hd + 1) * d] for hd in heads]
    out = parts[0] if len(parts) == 1 else jnp.concatenate(parts, axis=0)
    return out if dtype is None else out.astype(dtype)


def _fill_v_ones(v_ref, va_ref, hb, dv):
    @pl.when(pl.program_id(1) == 0)
    def _():
        ones = jnp.ones((v_ref.shape[0], dv), BF16)
        for h in range(hb):
            va_ref[:, 2 * h * dv:(2 * h + 1) * dv] = v_ref[:, h * dv:(h + 1) * dv]
            va_ref[:, (2 * h + 1) * dv:(2 * h + 2) * dv] = ones


def _flash_fwd(q, k, v, *, R, dk, dv, hb, bq, bk, name):
    s = q.shape[0]
    g = k.shape[1] // dk
    ng = g // hb
    bq, bk = min(bq, s), min(bk, s)
    nq, nkb = s // bq, s // bk
    rb = R * bq

    def body(q_ref, k_ref, v_ref, o_ref, lse_ref, va_ref):
        _fill_v_ones(v_ref, va_ref, hb, dv)
        head_sets = [[h * R + r for r in range(R)] for h in range(hb)]
        qss = [_stack_heads(q_ref, heads, dk) for heads in head_sets]

        def step(jj, carry):
            carry = list(carry)
            rows = [pl.ds(pl.multiple_of((jj * unroll + u) * bk, bk), bk) for u in range(unroll)]
            scs = [[lax.dot_general(qss[h], k_ref[rows[u], h * dk:(h + 1) * dk], _DIMS["nt"],
                                    preferred_element_type=F32) for h in range(hb)] for u in range(unroll)]
            for u in range(unroll):
                m2s = [jnp.maximum(carry[h][0], jnp.max(scs[u][h], axis=1, keepdims=True)) for h in range(hb)]
                ps = [jnp.exp2(scs[u][h] - m2s[h]).astype(BF16) for h in range(hb)]
                pvs = [jnp.dot(ps[h], va_ref[rows[u], 2 * h * dv:2 * (h + 1) * dv], preferred_element_type=F32)
                       for h in range(hb)]
                carry = [(m2s[h], jnp.exp2(carry[h][0] - m2s[h]) * carry[h][1] + pvs[h]) for h in range(hb)]
            return tuple(carry)

        unroll = 4 if nkb % 4 == 0 else 1
        init = tuple((jnp.full((rb, 1), NEG_INF, F32), jnp.zeros((rb, 2 * dv), F32)) for _ in range(hb))
        final = lax.fori_loop(0, nkb // unroll, step, init)
        for h, heads in enumerate(head_sets):
            m, acc = final[h]
            l = acc[:, dv:dv + 1]
            o = acc[:, :dv] / l
            lse = m + jnp.log2(l)
            for r, hd in enumerate(heads):
                o_ref[:, hd * dv:(hd + 1) * dv] = o[r * bq:(r + 1) * bq].astype(o_ref.dtype)
                lse_ref[0, :, hd:hd + 1] = lse[r * bq:(r + 1) * bq]

    return pl.pallas_call(
        body, name=name, grid=(ng, nq),
        in_specs=[pl.BlockSpec((bq, hb * R * dk), lambda gi, i: (i, gi)),
                  pl.BlockSpec((s, hb * dk), lambda gi, i: (0, gi)),
                  pl.BlockSpec((s, hb * dv), lambda gi, i: (0, gi))],
        out_specs=[pl.BlockSpec((bq, hb * R * dv), lambda gi, i: (i, gi)),
                   pl.BlockSpec((1, bq, hb * R), lambda gi, i: (gi, i, 0))],
        out_shape=[jax.ShapeDtypeStruct((s, g * R * dv), BF16), jax.ShapeDtypeStruct((ng, s, hb * R), F32)],
        scratch_shapes=[pltpu.VMEM((s, 2 * hb * dv), BF16)],
        compiler_params=_cparams(("parallel", "arbitrary")),
    )(q, k, v)


def _flash_bwd(q, k, v, o, do, lse, *, R, dk, dv, hb, bq, bk, name, unroll=1):
    s = q.shape[0]
    g = k.shape[1] // dk
    ng = g // hb
    bq, bk = min(bq, s), min(bk, s)
    nq, nkb = s // bq, s // bk
    rb = R * bq

    def body(q_ref, k_ref, v_ref, o_ref, do_ref, lse_ref, dq_ref, dkt_ref, dvt_ref, va_ref):
        @pl.when(pl.program_id(1) == 0)
        def _():
            dkt_ref[...] = jnp.zeros(dkt_ref.shape, F32)
            dvt_ref[...] = jnp.zeros(dvt_ref.shape, F32)

        _fill_v_ones(v_ref, va_ref, hb, dv)
        lane = lax.broadcasted_iota(jnp.int32, (rb, dv), 1)
        head_sets = [[h * R + r for r in range(R)] for h in range(hb)]
        q_t = jnp.transpose(q_ref[...].astype(F32)).astype(BF16)
        do_t = jnp.transpose(do_ref[...].astype(F32)).astype(BF16)

        def stack_t(t, heads, d):
            parts = [t[hd * d:(hd + 1) * d] for hd in heads]
            return parts[0] if len(parts) == 1 else jnp.concatenate(parts, axis=1)

        qss, qts, dots, dosas, lcols = [], [], [], [], []
        for heads in head_sets:
            dos = _stack_heads(do_ref, heads, dv, BF16)
            delta = jnp.sum(dos.astype(F32) * _stack_heads(o_ref, heads, dv, F32), axis=1, keepdims=True)
            hi = delta.astype(BF16).astype(F32)
            lo = delta - hi
            qss.append(_stack_heads(q_ref, heads, dk))
            qts.append(stack_t(q_t, heads, dk))
            dots.append(stack_t(do_t, heads, dv))
            dosas.append(jnp.concatenate(
                [dos, jnp.where(lane == 0, -hi, jnp.where(lane == 1, -lo, 0.0)).astype(BF16)], axis=1))
            cols = [lse_ref[0, :, hd:hd + 1] for hd in heads]
            lcols.append(cols[0] if R == 1 else jnp.concatenate(cols, axis=0))

        def step(jj, dqs):
            hs = range(hb)
            for u in range(unroll):
                j = jj * unroll + u
                r0 = pl.multiple_of(j * bk, bk)
                kjs = [k_ref[pl.ds(r0, bk), h * dk:(h + 1) * dk] for h in hs]
                vas = [va_ref[pl.ds(r0, bk), 2 * h * dv:2 * (h + 1) * dv] for h in hs]
                ss = [lax.dot_general(qss[h], kjs[h], _DIMS["nt"], preferred_element_type=F32) for h in hs]
                dps = [lax.dot_general(dosas[h], vas[h], _DIMS["nt"], preferred_element_type=F32) for h in hs]
                ps = [jnp.exp2(ss[h] - lcols[h]) for h in hs]
                pbs = [ps[h].astype(BF16) for h in hs]
                dss = [(ps[h] * dps[h]).astype(BF16) for h in hs]
                for h in hs:
                    dvt_ref[0, j, h * dv:(h + 1) * dv, :] += jnp.dot(dots[h], pbs[h], preferred_element_type=F32)
                for h in hs:
                    dkt_ref[0, j, h * dk:(h + 1) * dk, :] += jnp.dot(qts[h], dss[h], preferred_element_type=F32)
                dqs = tuple(dqs[h] + jnp.dot(dss[h], kjs[h], preferred_element_type=F32) for h in hs)
            return dqs

        assert nkb % unroll == 0
        dqs = lax.fori_loop(0, nkb // unroll, step, tuple(jnp.zeros((rb, dk), F32) for _ in range(hb)))
        for h, heads in enumerate(head_sets):
            for r, hd in enumerate(heads):
                dq_ref[:, hd * dk:(hd + 1) * dk] = dqs[h][r * bq:(r + 1) * bq]

    qspec = pl.BlockSpec((bq, hb * R * dk), lambda gi, i: (i, gi))
    ospec = pl.BlockSpec((bq, hb * R * dv), lambda gi, i: (i, gi))
    kspec = pl.BlockSpec((s, hb * dk), lambda gi, i: (0, gi))
    vspec = pl.BlockSpec((s, hb * dv), lambda gi, i: (0, gi))
    dq, dkt, dvt = pl.pallas_call(
        body, name=name, grid=(ng, nq),
        in_specs=[qspec, kspec, vspec, ospec, ospec, pl.BlockSpec((1, bq, hb * R), lambda gi, i: (gi, i, 0))],
        out_specs=[qspec, pl.BlockSpec((1, nkb, hb * dk, bk), lambda gi, i: (gi, 0, 0, 0)),
                   pl.BlockSpec((1, nkb, hb * dv, bk), lambda gi, i: (gi, 0, 0, 0))],
        out_shape=[jax.ShapeDtypeStruct((s, g * R * dk), F32), jax.ShapeDtypeStruct((ng, nkb, hb * dk, bk), F32),
                   jax.ShapeDtypeStruct((ng, nkb, hb * dv, bk), F32)],
        scratch_shapes=[pltpu.VMEM((s, 2 * hb * dv), BF16)],
        compiler_params=_cparams(("parallel", "arbitrary")),
    )(q, k, v, o, do, lse)
    return dq, _keys_first(dkt, name + "_dk"), _keys_first(dvt, name + "_dv")


def _keys_first(t, name):
    ng, nkb, f, bk = t.shape

    def body(t_ref, o_ref):
        for j in range(nkb):
            o_ref[j * bk:(j + 1) * bk, :] = jnp.transpose(t_ref[0, j])

    return pl.pallas_call(
        body, name=name, grid=(ng,),
        in_specs=[pl.BlockSpec((1, nkb, f, bk), lambda gi: (gi, 0, 0, 0))],
        out_specs=pl.BlockSpec((nkb * bk, f), lambda gi: (0, gi)),
        out_shape=jax.ShapeDtypeStruct((nkb * bk, ng * f), t.dtype),
        compiler_params=_cparams(("parallel",)),
    )(t)


DIL_T = 1024
DIL_P = DIL_HALF
DIL_NCOL = IN_B // DIL_W


DIL_BATCH = 4


def _alibi_slope(head):
    return float(2.0 ** (-8.0 * (head + 1) / DIL_HEADS))


def _slot(sl_i):
    return slice(sl_i * DIL_HEAD_DIM, (sl_i + 1) * DIL_HEAD_DIM)


def _halo_specs(d, col, s, t):
    h = DIL_P * d
    per, last = t // h, s // h - 1
    return [pl.BlockSpec((h, DIL_W), lambda c: (jnp.maximum(c * per - 1, 0), col)),
            pl.BlockSpec((t, DIL_W), lambda c: (c, col)),
            pl.BlockSpec((h, DIL_W), lambda c: (jnp.minimum((c + 1) * per, last), col))]


def _staging(rows):
    return tuple(pltpu.VMEM((rows, LANES), F32) for _ in range(DIL_W // LANES))


def _stage(buf, refs):
    off = 0
    for r in refs:
        val = r[...].astype(F32)
        for j in range(DIL_W // LANES):
            buf[j][off:off + r.shape[0], :] = val[:, j * LANES:(j + 1) * LANES]
        off += r.shape[0]


def _unstage(buf, ref):
    ref[...] = jnp.concatenate([half[...] for half in buf], axis=1).astype(ref.dtype)


def _sub_tiles(d, t):
    return [(b * DIL_P * d + r, b * DIL_P) for b in range(t // (DIL_P * d)) for r in range(d)]


def _rows(start, size, d):
    return pl.ds(start, size, stride=d) if d > 1 else pl.ds(start, size)


def _strided(buf, start, size, d):
    return jnp.concatenate([half[_rows(start, size, d), :] for half in buf], axis=1)


def _put_strided(buf, start, d, val):
    for j in range(DIL_W // LANES):
        buf[j][_rows(start, val.shape[0], d), :] = val[:, j * LANES:(j + 1) * LANES]


def _band(u0, length, d, queries_wide):
    if queries_wide:
        shape = (3 * DIL_P, DIL_P)
        wide = u0 - DIL_P + lax.broadcasted_iota(jnp.int32, shape, 0)
        narrow = u0 + lax.broadcasted_iota(jnp.int32, shape, 1)
    else:
        shape = (DIL_P, 3 * DIL_P)
        narrow = u0 + lax.broadcasted_iota(jnp.int32, shape, 0)
        wide = u0 - DIL_P + lax.broadcasted_iota(jnp.int32, shape, 1)
    rel = jnp.abs(wide - narrow)
    valid = (rel <= DIL_HALF) & (wide >= 0) & (wide < length)
    return valid, rel.astype(F32) * float(d)


def _dil_fwd(zb, grp, name):
    s = zb.shape[0]
    d = DIL_PAIRS[grp][1]
    t = min(DIL_T, s)
    h = DIL_P * d
    scale = DIL_HEAD_DIM ** -0.5

    def body(q_ref, kp, kc, kn, vp, vc, vn, o_ref, lse_ref, qbuf, kbuf, vbuf, obuf, lbuf):
        _stage(qbuf, (q_ref,))
        _stage(kbuf, (kp, kc, kn))
        _stage(vbuf, (vp, vc, vn))
        u_step = pl.program_id(0) * (t // d)
        tiles = _sub_tiles(d, t)
        for g0 in range(0, len(tiles), DIL_BATCH):
            batch = tiles[g0:g0 + DIL_BATCH]
            masks = [_band(u_step + u, s // d, d, False) for _, u in batch]
            qs = [_strided(qbuf, row, DIL_P, d).astype(BF16) for row, _ in batch]
            ks = [_strided(kbuf, row, 3 * DIL_P, d).astype(BF16) for row, _ in batch]
            vs = [_strided(vbuf, row, 3 * DIL_P, d).astype(BF16) for row, _ in batch]
            chains = [(i, sl_i) for i in range(len(batch)) for sl_i in range(DIL_SLOTS)]
            scs = [lax.dot_general(qs[i][:, _slot(sl_i)], ks[i][:, _slot(sl_i)], _DIMS["nt"],
                                   preferred_element_type=F32) for i, sl_i in chains]
            scs = [jnp.where(masks[i][0], sc * scale - _alibi_slope(grp * DIL_SLOTS + sl_i) * masks[i][1], NEG_INF)
                   for (i, sl_i), sc in zip(chains, scs)]
            ms = [jnp.max(sc, axis=1, keepdims=True) for sc in scs]
            es = [jnp.exp(sc - m) for sc, m in zip(scs, ms)]
            dens = [jnp.sum(e, axis=1, keepdims=True) for e in es]
            outs = [jnp.dot((e / den).astype(BF16), vs[i][:, _slot(sl_i)], preferred_element_type=F32)
                    for (i, sl_i), e, den in zip(chains, es, dens)]
            lses = [jnp.broadcast_to(m + jnp.log(den), (DIL_P, DIL_HEAD_DIM)) for m, den in zip(ms, dens)]
            for i, (row, _) in enumerate(batch):
                pick = slice(i * DIL_SLOTS, (i + 1) * DIL_SLOTS)
                _put_strided(obuf, row, d, jnp.concatenate(outs[pick], axis=1))
                _put_strided(lbuf, row, d, jnp.concatenate(lses[pick], axis=1))
        _unstage(obuf, o_ref)
        _unstage(lbuf, lse_ref)

    own = pl.BlockSpec((t, DIL_W), lambda c: (c, 0))
    return pl.pallas_call(
        body, name=name, grid=(s // t,),
        in_specs=[pl.BlockSpec((t, DIL_W), lambda c: (c, grp))] + _halo_specs(d, 3 + grp, s, t)
        + _halo_specs(d, 6 + grp, s, t),
        out_specs=[own, own], out_shape=[jax.ShapeDtypeStruct((s, DIL_W), F32)] * 2,
        scratch_shapes=[_staging(t), _staging(t + 2 * h), _staging(t + 2 * h), _staging(t), _staging(t)],
        compiler_params=_cparams(("parallel",)),
    )(zb, zb, zb, zb, zb, zb, zb)


def _dil_bwd(zb, do, lse, dl, grp, name):
    s = zb.shape[0]
    d = DIL_PAIRS[grp][1]
    t = min(DIL_T, s)
    h = DIL_P * d
    scale = DIL_HEAD_DIM ** -0.5

    def chain_grads(qs, ks, vs, dos, lses, dls, masks):
        chains = [(i, sl_i) for i in range(len(qs)) for sl_i in range(DIL_SLOTS)]
        scs = [lax.dot_general(qs[i][:, _slot(sl_i)], ks[i][:, _slot(sl_i)], _DIMS["nt"],
                               preferred_element_type=F32) for i, sl_i in chains]
        dps = [lax.dot_general(dos[i][:, _slot(sl_i)], vs[i][:, _slot(sl_i)], _DIMS["nt"],
                               preferred_element_type=F32) for i, sl_i in chains]
        ps = [jnp.exp(jnp.where(masks[i][0], sc * scale - _alibi_slope(grp * DIL_SLOTS + sl_i) * masks[i][1],
                                NEG_INF) - lses[i][:, sl_i * DIL_HEAD_DIM:sl_i * DIL_HEAD_DIM + 1])
              for (i, sl_i), sc in zip(chains, scs)]
        dss = [(p * (dp - dls[i][:, sl_i * DIL_HEAD_DIM:sl_i * DIL_HEAD_DIM + 1]) * scale).astype(BF16)
               for (i, sl_i), p, dp in zip(chains, ps, dps)]
        return chains, ps, dss

    def dq_body(q_ref, kp, kc, kn, vp, vc, vn, do_ref, lse_ref, dl_ref, dq_ref, qbuf, kbuf, vbuf, dobuf, lsebuf,
                dlbuf, obuf):
        _stage(qbuf, (q_ref,))
        _stage(dobuf, (do_ref,))
        _stage(lsebuf, (lse_ref,))
        _stage(dlbuf, (dl_ref,))
        _stage(kbuf, (kp, kc, kn))
        _stage(vbuf, (vp, vc, vn))
        u_step = pl.program_id(0) * (t // d)
        tiles = _sub_tiles(d, t)
        for g0 in range(0, len(tiles), DIL_BATCH):
            batch = tiles[g0:g0 + DIL_BATCH]
            masks = [_band(u_step + u, s // d, d, False) for _, u in batch]
            narrow = [[_strided(b, row, DIL_P, d) for row, _ in batch] for b in (qbuf, dobuf, lsebuf, dlbuf)]
            ks = [_strided(kbuf, row, 3 * DIL_P, d).astype(BF16) for row, _ in batch]
            vs = [_strided(vbuf, row, 3 * DIL_P, d).astype(BF16) for row, _ in batch]
            chains, _, dss = chain_grads([a.astype(BF16) for a in narrow[0]], ks, vs,
                                         [a.astype(BF16) for a in narrow[1]], narrow[2], narrow[3], masks)
            outs = [jnp.dot(ds, ks[i][:, _slot(sl_i)], preferred_element_type=F32)
                    for (i, sl_i), ds in zip(chains, dss)]
            for i, (row, _) in enumerate(batch):
                _put_strided(obuf, row, d, jnp.concatenate(outs[i * DIL_SLOTS:(i + 1) * DIL_SLOTS], axis=1))
        _unstage(obuf, dq_ref)

    def dkv_body(k_ref, v_ref, qp, qc, qn, dop, doc, don, lp, lc, ln, dlp, dlc, dln, dk_ref, dv_ref,
                 kbuf, vbuf, qbuf, dobuf, lsebuf, dlbuf, dkbuf, dvbuf):
        _stage(kbuf, (k_ref,))
        _stage(vbuf, (v_ref,))
        _stage(qbuf, (qp, qc, qn))
        _stage(dobuf, (dop, doc, don))
        _stage(lsebuf, (lp, lc, ln))
        _stage(dlbuf, (dlp, dlc, dln))
        u_step = pl.program_id(0) * (t // d)
        tiles = _sub_tiles(d, t)
        for g0 in range(0, len(tiles), DIL_BATCH):
            batch = tiles[g0:g0 + DIL_BATCH]
            masks = [_band(u_step + u, s // d, d, True) for _, u in batch]
            ks = [_strided(kbuf, row, DIL_P, d).astype(BF16) for row, _ in batch]
            vs = [_strided(vbuf, row, DIL_P, d).astype(BF16) for row, _ in batch]
            wide_ = [[_strided(b, row, 3 * DIL_P, d) for row, _ in batch] for b in (qbuf, dobuf, lsebuf, dlbuf)]
            qs, dos = [a.astype(BF16) for a in wide_[0]], [a.astype(BF16) for a in wide_[1]]
            chains, ps, dss = chain_grads(qs, ks, vs, dos, wide_[2], wide_[3], masks)
            dvs = [lax.dot_general(p.astype(BF16), dos[i][:, _slot(sl_i)], _DIMS["tn"], preferred_element_type=F32)
                   for (i, sl_i), p in zip(chains, ps)]
            dks = [lax.dot_general(ds, qs[i][:, _slot(sl_i)], _DIMS["tn"], preferred_element_type=F32)
                   for (i, sl_i), ds in zip(chains, dss)]
            for i, (row, _) in enumerate(batch):
                pick = slice(i * DIL_SLOTS, (i + 1) * DIL_SLOTS)
                _put_strided(dkbuf, row, d, jnp.concatenate(dks[pick], axis=1))
                _put_strided(dvbuf, row, d, jnp.concatenate(dvs[pick], axis=1))
        _unstage(dkbuf, dk_ref)
        _unstage(dvbuf, dv_ref)

    def zcur(col):
        return pl.BlockSpec((t, DIL_W), lambda c: (c, col))

    own = pl.BlockSpec((t, DIL_W), lambda c: (c, 0))
    out = jax.ShapeDtypeStruct((s, DIL_W), F32)
    tile, wide = _staging(t), _staging(t + 2 * h)
    dq = pl.pallas_call(
        dq_body, name=name + "_dq", grid=(s // t,),
        in_specs=[zcur(grp)] + _halo_specs(d, 3 + grp, s, t) + _halo_specs(d, 6 + grp, s, t) + [own, own, own],
        out_specs=own, out_shape=out, scratch_shapes=[tile, wide, wide, tile, tile, tile, tile],
        compiler_params=_cparams(("parallel",)),
    )(zb, zb, zb, zb, zb, zb, zb, do, lse, dl)
    own3 = _halo_specs(d, 0, s, t)
    dk, dv = pl.pallas_call(
        dkv_body, name=name + "_dkv", grid=(s // t,),
        in_specs=[zcur(3 + grp), zcur(6 + grp)] + _halo_specs(d, grp, s, t) + own3 + own3 + own3,
        out_specs=[own, own], out_shape=[out, out],
        scratch_shapes=[tile, tile, wide, wide, wide, wide, tile, tile],
        compiler_params=_cparams(("parallel",)),
    )(zb, zb, zb, zb, zb, do, do, do, lse, lse, lse, dl, dl, dl)
    return dq, dk, dv


def _dil_combine(os_, ls_, name):
    def fn(o0, o1, o2, l0, l1, l2):
        m = jnp.maximum(jnp.maximum(l0, l1), l2)
        e0, e1, e2 = jnp.exp(l0 - m), jnp.exp(l1 - m), jnp.exp(l2 - m)
        den = e0 + e1 + e2
        comb = (e0 / den) * o0 + (e1 / den) * o1 + (e2 / den) * o2
        return comb, m + jnp.log(den)
    return _rows_call(fn, list(os_) + list(ls_), [], [(DIL_W, BF16), (DIL_W, F32)], name=name)


def _dil_combine_bwd(dcomb, os_, ls_, lt, seg64, name):
    def fn(dc, o0, o1, o2, l0, l1, l2, ltv, seg):
        w = [jnp.exp(l - ltv) for l in (l0, l1, l2)]
        comb = w[0] * o0 + w[1] * o1 + w[2] * o2
        t = _chunkdot(dc * comb, seg)
        return [wg * dc for wg in w] + [wg * t for wg in w]
    return _rows_call(fn, [dcomb] + list(os_) + list(ls_) + [lt], [seg64],
                      [(DIL_W, BF16)] * 3 + [(DIL_W, F32)] * 3, name=name)


def _mla_prep(za, gq, gkv, cs, swap, name):
    def fn(z, csv, gqv, gkvv, sw):
        return (_rms(z[:, :MLA_Q_RANK], gqv), _rms(z[:, MLA_Q_RANK:640], gkvv), _rope(z[:, 640:], csv, sw))
    return _rows_call(fn, [za, cs], [gq, gkv, swap], [(MLA_Q_RANK, BF16), (MLA_KV_RANK, BF16), (LANES, F32)],
                      name=name)


def _mla_prep_bwd(za, cs, dcq, dckv, dkr, gq, gkv, swap, name):
    def fn(z, csv, dcqv, dckvv, dkrv, gqv, gkvv, sw):
        d1, dg1 = _rms_bwd_math(z[:, :MLA_Q_RANK], gqv, dcqv)
        d2, dg2 = _rms_bwd_math(z[:, MLA_Q_RANK:640], gkvv, dckvv)
        d3 = _rope_t(dkrv, csv, sw)
        return jnp.concatenate([d1, d2, d3], axis=1), dg1, dg2
    return _rows_call(fn, [za, cs, dcq, dckv, dkr], [gq, gkv, swap], [(IN_A_PAD, BF16)],
                      [(1, MLA_Q_RANK), (1, MLA_KV_RANK)], name=name)


def _mla_qk(q_raw, k_pad, krr, cs, swap, name):
    w = MLA_HEADS * MLA_DK

    def fn(qv, kv, krv, csv, sw):
        return _rope(qv, csv, sw) * (MLA_SCALE * LOG2E), kv + _lanes(krv, w)
    return _rows_call(fn, [q_raw, k_pad, krr, cs], [swap], [(w, BF16), (w, BF16)], name=name)


def _mla_qk_bwd(dqh, dkh, cs, swap, name):
    w = MLA_HEADS * MLA_DK

    def fn(dq, dk, csv, sw):
        dk = dk * LN2
        acc = dk[:, :LANES]
        for h in range(1, MLA_HEADS):
            acc = acc + dk[:, h * LANES:(h + 1) * LANES]
        lane = lax.broadcasted_iota(jnp.int32, acc.shape, 1)
        acc = jnp.where((lane >= MLA_NOPE) & (lane < MLA_NOPE + MLA_ROPE), acc, 0.0)
        return _rope_t(dq * MLA_SCALE, csv, sw), dk, acc
    return _rows_call(fn, [dqh, dkh, cs], [swap], [(w, BF16), (w, BF16), (LANES, F32)], name=name)


def _head_norm(t, g2, seg):
    r = lax.rsqrt(_chunkdot(t * t, seg) * (1.0 / GQA_HEAD_DIM) + NORM_EPS)
    return t * r * _lanes(g2, t.shape[1]), r


def _head_norm_bwd(t, g2, seg, dn):
    w = t.shape[1]
    r = lax.rsqrt(_chunkdot(t * t, seg) * (1.0 / GQA_HEAD_DIM) + NORM_EPS)
    u = dn * _lanes(g2, w)
    dt = r * u - t * (r * r * r) * (_chunkdot(u * t, seg) * (1.0 / GQA_HEAD_DIM))
    dgw = jnp.sum(dn * t * r, axis=0, keepdims=True)
    dg = dgw[:, :LANES]
    for c in range(LANES, w, LANES):
        dg = dg + dgw[:, c:c + LANES]
    return dt, dg


def _gqa_prep(q_raw, kv_raw, cs, gq2, gk2, seg, swap, name):
    kw = GQA_KV_HEADS * GQA_HEAD_DIM

    def fn(qv, kvv, csv, gqv, gkv, sg, sw):
        qn, _ = _head_norm(qv, gqv, sg)
        kn, _ = _head_norm(kvv[:, :kw], gkv, sg)
        return _rope(qn, csv, sw) * (GQA_SCALE * LOG2E), _rope(kn, csv, sw), kvv[:, kw:]
    return _rows_call(fn, [q_raw, kv_raw, cs], [gq2, gk2, seg, swap],
                      [(GQA_HEADS * GQA_HEAD_DIM, BF16), (kw, BF16), (kw, BF16)], name=name)


def _gqa_prep_bwd(q_raw, kv_raw, cs, dqh, dkh, dv, gq2, gk2, seg, swap, name):
    kw = GQA_KV_HEADS * GQA_HEAD_DIM

    def fn(qv, kvv, csv, dq, dk, dvv, gqv, gkv, sg, sw):
        dqr, dgq = _head_norm_bwd(qv, gqv, sg, _rope_t(dq * GQA_SCALE, csv, sw))
        dkr, dgk = _head_norm_bwd(kvv[:, :kw], gkv, sg, _rope_t(dk * LN2, csv, sw))
        return dqr, jnp.concatenate([dkr, dvv], axis=1), dgq, dgk
    return _rows_call(fn, [q_raw, kv_raw, cs, dqh, dkh, dv], [gq2, gk2, seg, swap],
                      [(GQA_HEADS * GQA_HEAD_DIM, BF16), (2 * kw, BF16)], [(1, LANES), (1, LANES)], name=name)


def _loss_head(x, target, g, name):
    dm = x.shape[1]

    def fn(xv, tv, gv):
        err = _rms(xv, gv) - tv
        loss = 0.5 * jnp.sum(err * err) / dm
        dx, dg = _rms_bwd_math(xv, gv, err * (1.0 / dm))
        return dx, jnp.zeros((1, LANES), F32) + loss, dg
    return _rows_call(fn, [x, target], [g], [(dm, F32)], [(1, LANES), (1, dm)], name=name)


def _adamw(w, g, m, v, name):
    def fn(wv, gv, mv, vv):
        m2 = ADAM_B1 * mv + (1.0 - ADAM_B1) * gv
        v2 = ADAM_B2 * vv + (1.0 - ADAM_B2) * (gv * gv)
        m_hat = m2 / (1.0 - ADAM_B1 ** ADAM_STEP)
        v_hat = v2 / (1.0 - ADAM_B2 ** ADAM_STEP)
        return -ADAM_LR * (m_hat / (jnp.sqrt(v_hat) + ADAM_EPS) + ADAM_WD * wv), m2, v2
    c = w.shape[1]
    return _rows_call(fn, [w, g, m, v], [], [(c, F32)] * 3, bs=_pick(w.shape[0], 256, 8), name=name)


HBM_SPEC = pl.BlockSpec(memory_space=pltpu.HBM)
VMEM_SPEC = pl.BlockSpec(memory_space=pltpu.VMEM)


def _position():
    return lax.axis_index("x"), lax.axis_index("y"), lax.axis_index("c")


def _other_chips(x, y):
    return [(1 - x, y), (x, 1 - y), (1 - x, 1 - y)]


HALF_W = PACK_W // 2


def _cols(c):
    return pl.ds(pl.multiple_of(c * HALF_W, HALF_W), HALF_W)


def _all_gather_weights(packed):
    rows = packed.shape[0]

    def body(p_ref, g_ref, send_sems, recv_sems, local_sem):
        x, y, c = _position()
        chips = _other_chips(x, y)

        def half(chip, hc):
            return g_ref.at[2 * chip[0] + chip[1], :, _cols(hc)]

        def copy(j, src, dst, to):
            return pltpu.make_async_remote_copy(src_ref=src, dst_ref=dst, send_sem=send_sems.at[j],
                                                recv_sem=recv_sems.at[j], device_id=to, device_id_type=MESH)

        mine = pltpu.make_async_copy(p_ref, g_ref.at[2 * x + y], local_sem)
        mine.start()
        first = [copy(j, p_ref.at[:, _cols(c)], half((x, y), c), (*chip, c)) for j, chip in enumerate(chips)]
        for cp in first:
            cp.start()
        passed = [copy(3 + j, half(chip, c), half(chip, c), (x, y, 1 - c)) for j, chip in enumerate(chips)]
        for j, chip in enumerate(chips):
            copy(j, half(chip, c), half(chip, c), (x, y, c)).wait_recv()
            passed[j].start()
        for j, chip in enumerate(chips):
            copy(3 + j, half(chip, 1 - c), half(chip, 1 - c), (x, y, c)).wait_recv()
        for cp in first + passed:
            cp.wait_send()
        mine.wait()

    return pl.pallas_call(
        body, name="all_gather_weights", in_specs=[HBM_SPEC], out_specs=HBM_SPEC,
        out_shape=jax.ShapeDtypeStruct((4, rows, packed.shape[1]), packed.dtype),
        scratch_shapes=[pltpu.SemaphoreType.DMA((6,)), pltpu.SemaphoreType.DMA((6,)), pltpu.SemaphoreType.DMA],
    )(packed)


SEM_SPEC = pl.BlockSpec(memory_space=pltpu.SEMAPHORE)
ANY_SPEC = pl.BlockSpec(memory_space=pl.ANY)
DATAFLOW = pltpu.SideEffectType.DATAFLOW_SIDE_EFFECTING


def _hbm(a):
    return pltpu.with_memory_space_constraint(a, pltpu.HBM)


def _gather_start(packed, tag):
    rows = packed.shape[0]

    def body(p_ref, g_ref, send_sems, recv_sems, p_thru, g_thru, token):
        x, y, c = _position()
        for j, chip in enumerate(_other_chips(x, y)):
            for s in range(2):
                pltpu.make_async_remote_copy(
                    src_ref=p_ref.at[:, _cols(c)], dst_ref=g_ref.at[2 * x + y, :, _cols(c)],
                    send_sem=send_sems.at[2 * j + s], recv_sem=recv_sems.at[2 * j + s],
                    device_id=(*chip, 1 - c if s else c), device_id_type=MESH).start()
        token[...] = jnp.zeros_like(token)

    return pl.pallas_call(
        body, name=tag + "_start",
        out_shape=(pltpu.SemaphoreType.DMA((6,)), pltpu.SemaphoreType.DMA((6,)), pltpu.HBM(packed.shape, packed.dtype),
                   pltpu.HBM((4, rows, PACK_W), packed.dtype), jax.ShapeDtypeStruct((8, LANES), F32)),
        in_specs=(HBM_SPEC, HBM_SPEC), out_specs=(SEM_SPEC, SEM_SPEC, HBM_SPEC, HBM_SPEC, VMEM_SPEC),
        input_output_aliases={0: 2, 1: 3},
        compiler_params=pltpu.CompilerParams(has_side_effects=DATAFLOW),
    )(_hbm(packed), _hbm(lax.empty((4, rows, PACK_W), packed.dtype)))


def _gather_wait(send_sems, recv_sems, p_thru, g_thru, after, tag):
    def body(p_ref, g_ref, send_sems, recv_sems, after_ref, p_dead, got_ref):
        x, y, c = _position()
        for j, chip in enumerate(_other_chips(x, y)):
            for s in range(2):
                cp = pltpu.make_async_remote_copy(
                    src_ref=p_ref.at[:, _cols(c)], dst_ref=g_ref.at[2 * chip[0] + chip[1], :, _cols(1 - c if s else c)],
                    send_sem=send_sems.at[2 * j + s], recv_sem=recv_sems.at[2 * j + s],
                    device_id=(x, y, c), device_id_type=MESH)
                cp.wait_send()
                cp.wait_recv()

    return pl.pallas_call(
        body, name=tag + "_wait",
        out_shape=(pltpu.HBM(p_thru.shape, p_thru.dtype), pltpu.HBM(g_thru.shape, g_thru.dtype)),
        in_specs=(HBM_SPEC, HBM_SPEC, SEM_SPEC, SEM_SPEC, ANY_SPEC), out_specs=(HBM_SPEC, HBM_SPEC),
        input_output_aliases={0: 0, 1: 1},
        compiler_params=pltpu.CompilerParams(has_side_effects=DATAFLOW),
    )(p_thru, g_thru, send_sems, recv_sems, after)[1]


def _sibling_swap_halves(grads, tag):
    rows = grads.shape[1]

    def body(g_ref, a_ref, send_sem, recv_sem):
        x, y, c = _position()
        cp = pltpu.make_async_remote_copy(src_ref=g_ref.at[:, :, _cols(1 - c)], dst_ref=a_ref,
                                          send_sem=send_sem, recv_sem=recv_sem, device_id=(x, y, 1 - c),
                                          device_id_type=MESH)
        cp.start()
        cp.wait()

    return pl.pallas_call(
        body, name=tag + "_swap_cores", in_specs=[HBM_SPEC], out_specs=HBM_SPEC,
        out_shape=jax.ShapeDtypeStruct((4, rows, HALF_W), grads.dtype),
        scratch_shapes=[pltpu.SemaphoreType.DMA, pltpu.SemaphoreType.DMA],
    )(grads)


def _rs_block(rows):
    return max(d for d in range(16, 1601, 16) if rows % d == 0)


def _chip_sum(grads, other, c, tag):
    rows = other.shape[1]
    rb = _rs_block(rows)

    def body(c_ref, g_ref, a_ref, o_ref):
        o_ref[...] = (g_ref[...] + a_ref[...]).astype(o_ref.dtype)

    blk = (1, rb, HALF_W)
    return pl.pallas_call(
        body, name=tag + "_chip_sum",
        grid_spec=pltpu.PrefetchScalarGridSpec(
            num_scalar_prefetch=1, grid=(4, rows // rb),
            in_specs=[pl.BlockSpec(blk, lambda k, i, c_ref: (k, i, c_ref[0])),
                      pl.BlockSpec(blk, lambda k, i, c_ref: (k, i, 0))],
            out_specs=pl.BlockSpec(blk, lambda k, i, c_ref: (k, i, 0))),
        out_shape=jax.ShapeDtypeStruct(other.shape, BF16),
        compiler_params=_cparams(("parallel", "parallel")),
    )(jnp.reshape(c, (1,)).astype(jnp.int32), grads, other)


def _chip_copies(t_ref, b_ref, send_sems, recv_sems):
    x, y, c = _position()
    return [pltpu.make_async_remote_copy(src_ref=t_ref.at[2 * chip[0] + chip[1]], dst_ref=b_ref.at[j],
                                         send_sem=send_sems.at[j], recv_sem=recv_sems.at[j],
                                         device_id=(*chip, c), device_id_type=MESH)
            for j, chip in enumerate(_other_chips(x, y))]


def _send_chip_sums(sums, tag):
    def body(t_ref, b_ref, send_sems, recv_sems):
        copies = _chip_copies(t_ref, b_ref, send_sems, recv_sems)
        for cp in copies:
            cp.start()
        for cp in copies:
            cp.wait()

    return pl.pallas_call(
        body, name=tag + "_send_chips", in_specs=[HBM_SPEC], out_specs=HBM_SPEC,
        out_shape=jax.ShapeDtypeStruct((3,) + sums.shape[1:], sums.dtype),
        scratch_shapes=[pltpu.SemaphoreType.DMA((3,)), pltpu.SemaphoreType.DMA((3,))],
    )(sums)


def _send_chip_sums_start(sums, tag):
    land = (3,) + sums.shape[1:]

    def body(t_ref, b_ref, send_sems, recv_sems, t_thru, b_thru, token):
        for cp in _chip_copies(t_ref, b_ref, send_sems, recv_sems):
            cp.start()
        token[...] = jnp.zeros_like(token)

    return pl.pallas_call(
        body, name=tag + "_send_chips_start",
        out_shape=(pltpu.SemaphoreType.DMA((3,)), pltpu.SemaphoreType.DMA((3,)), pltpu.HBM(sums.shape, sums.dtype),
                   pltpu.HBM(land, sums.dtype), jax.ShapeDtypeStruct((8, LANES), F32)),
        in_specs=(HBM_SPEC, HBM_SPEC), out_specs=(SEM_SPEC, SEM_SPEC, HBM_SPEC, HBM_SPEC, VMEM_SPEC),
        input_output_aliases={0: 2, 1: 3},
        compiler_params=pltpu.CompilerParams(has_side_effects=DATAFLOW),
    )(_hbm(sums), _hbm(lax.empty(land, sums.dtype)))


def _send_chip_sums_wait(send_sems, recv_sems, t_thru, b_thru, after, tag):
    def body(t_ref, b_ref, send_sems, recv_sems, after_ref, t_dead, got_ref):
        for cp in _chip_copies(t_ref, b_ref, send_sems, recv_sems):
            cp.wait_send()
            cp.wait_recv()

    return pl.pallas_call(
        body, name=tag + "_send_chips_wait",
        out_shape=(pltpu.HBM(t_thru.shape, t_thru.dtype), pltpu.HBM(b_thru.shape, b_thru.dtype)),
        in_specs=(HBM_SPEC, HBM_SPEC, SEM_SPEC, SEM_SPEC, ANY_SPEC), out_specs=(HBM_SPEC, HBM_SPEC),
        input_output_aliases={0: 0, 1: 1},
        compiler_params=pltpu.CompilerParams(has_side_effects=DATAFLOW),
    )(t_thru, b_thru, send_sems, recv_sems, after)[1]


def _final_sum(grads, other, recv, k, c, tag):
    rows = other.shape[1]
    rb = _rs_block(rows)

    def body(k_ref, c_ref, g_ref, a_ref, b_ref, o_ref):
        own = g_ref[0] + a_ref[0]
        o_ref[...] = ((own + b_ref[0].astype(F32)) + b_ref[1].astype(F32)) + b_ref[2].astype(F32)

    return pl.pallas_call(
        body, name=tag + "_final_sum",
        grid_spec=pltpu.PrefetchScalarGridSpec(
            num_scalar_prefetch=2, grid=(rows // rb,),
            in_specs=[pl.BlockSpec((1, rb, HALF_W), lambda i, k_ref, c_ref: (k_ref[0], i, c_ref[0])),
                      pl.BlockSpec((1, rb, HALF_W), lambda i, k_ref, c_ref: (k_ref[0], i, 0)),
                      pl.BlockSpec((3, rb, HALF_W), lambda i, k_ref, c_ref: (0, i, 0))],
            out_specs=pl.BlockSpec((rb, HALF_W), lambda i, k_ref, c_ref: (i, 0))),
        out_shape=jax.ShapeDtypeStruct((rows, HALF_W), F32),
        compiler_params=_cparams(("parallel",)),
    )(jnp.reshape(k, (1,)).astype(jnp.int32), jnp.reshape(c, (1,)).astype(jnp.int32), grads, other, recv)


def _join_halves(half, core, tag):
    def body(h_ref, o_ref, send_sem, recv_sem):
        x, y, c = _position()
        cp = pltpu.make_async_remote_copy(src_ref=h_ref, dst_ref=o_ref, send_sem=send_sem, recv_sem=recv_sem,
                                          device_id=(x, y, 1 - c), device_id_type=MESH)
        cp.start()
        cp.wait()

    other = pl.pallas_call(
        body, name=tag + "_join_cores", in_specs=[HBM_SPEC], out_specs=HBM_SPEC,
        out_shape=jax.ShapeDtypeStruct(half.shape, half.dtype),
        scratch_shapes=[pltpu.SemaphoreType.DMA, pltpu.SemaphoreType.DMA],
    )(half)
    first = core == 0
    return jnp.concatenate([jnp.where(first, half, other), jnp.where(first, other, half)], axis=1)


def _all_reduce_packet(packet):
    rows = packet.shape[0]

    def body(p_ref, o_ref, buf, send_sems, recv_sems):
        x, y, c = _position()
        me = 4 * x + 2 * y + c
        buf[me] = p_ref[...]

        def flip(v, bit):
            return 1 - v if bit else v

        for p in range(1, 8):
            peer = (flip(x, p & 4), flip(y, p & 2), flip(c, p & 1))
            pltpu.make_async_remote_copy(src_ref=p_ref, dst_ref=buf.at[me], send_sem=send_sems.at[p - 1],
                                         recv_sem=recv_sems.at[p - 1], device_id=peer, device_id_type=MESH).start()
        for p in range(1, 8):
            peer = (flip(x, p & 4), flip(y, p & 2), flip(c, p & 1))
            slot = 4 * peer[0] + 2 * peer[1] + peer[2]
            cp = pltpu.make_async_remote_copy(src_ref=p_ref, dst_ref=buf.at[slot], send_sem=send_sems.at[p - 1],
                                              recv_sem=recv_sems.at[p - 1], device_id=peer, device_id_type=MESH)
            cp.wait_recv()
            cp.wait_send()
        acc = buf[0]
        for dev in range(1, 8):
            acc = acc + buf[dev]
        o_ref[...] = acc

    return pl.pallas_call(
        body, name="all_reduce_packet", in_specs=[VMEM_SPEC], out_specs=VMEM_SPEC,
        out_shape=jax.ShapeDtypeStruct(packet.shape, F32),
        scratch_shapes=[pltpu.VMEM((8, rows, LANES), F32), pltpu.SemaphoreType.DMA((7,)),
                        pltpu.SemaphoreType.DMA((7,))],
    )(packet)


def _stack_range(name, n_stack, pack):
    if name.startswith('gqa'):
        return (0, 0) if pack == 0 else (0, n_stack)
    return (0, 1) if pack == 0 else (1, n_stack)


def _pack_members(pack):
    out = []
    for n, shape, ax in BIG:
        lo, hi = _stack_range(n, shape[0], pack)
        if hi > lo:
            out.append((n, (hi - lo,) + shape[1:], ax, (lo, hi)))
    return out


PACK_ROW_MULTIPLE = 512


def _pad_rows(parts, dtype):
    rows = sum(p.shape[0] for p in parts)
    pad = -rows % PACK_ROW_MULTIPLE
    return jnp.concatenate(parts + ([jnp.zeros((pad, PACK_W), dtype)] if pad else []), axis=0)


def _pack_blocks(blocks, dtype, pack):
    return _pad_rows([blocks[n][lo:hi].astype(dtype).reshape(-1, PACK_W)
                      for n, _, _, (lo, hi) in _pack_members(pack)], dtype)


def _unpack_blocks(packed, pack):
    out, off = {}, 0
    for n, shape, _, _ in _pack_members(pack):
        r = math.prod(shape) // PACK_W
        out[n] = packed[off:off + r].reshape(shape)
        off += r
    return out


def _unpack_gathered(gathered, pack, own=None, chip=None):
    blocks = [gathered[k] if own is None else jnp.where(chip == k, own, gathered[k]) for k in range(4)]
    per_chip = [_unpack_blocks(blocks[k], pack) for k in range(4)]
    return {n: jnp.concatenate([per_chip[k][n] for k in range(4)], axis=ax) for n, _, ax, _ in _pack_members(pack)}


def _pack_full(full, dtype, pack):
    chips = []
    for k in range(4):
        parts = []
        for n, shape, ax, _ in _pack_members(pack):
            blk = lax.slice_in_dim(full[n], k * shape[ax], (k + 1) * shape[ax], axis=ax)
            parts.append(blk.astype(dtype).reshape(-1, PACK_W))
        chips.append(_pad_rows(parts, dtype))
    return jnp.stack(chips, axis=0)


def _pack_small(vals, loss_row):
    rows = [loss_row.reshape(1, LANES)]
    for n, shape in SMALL:
        v = vals.get(n)
        v = jnp.zeros(shape, F32) if v is None else v
        rows.append(v.astype(F32).reshape(-1, LANES))
    packet = jnp.concatenate(rows, axis=0)
    return jnp.pad(packet, ((0, PACKET_ROWS - packet.shape[0]), (0, 0)))


def _unpack_small(packet):
    out, off = {}, 1
    for n, shape in SMALL:
        r = math.prod(shape) // LANES
        out[n] = packet[off:off + r].reshape(shape)
        off += r
    return packet[0, 0], out


_MLA = dict(R=1, dk=MLA_DK, dv=MLA_V, hb=2, bq=512, bk=512)
_MLA_FWD_BQ = 1024
_BWD_UNROLL = 4
_GQA = dict(R=GQA_HEADS // GQA_KV_HEADS, dk=GQA_HEAD_DIM, dv=GQA_HEAD_DIM, hb=2, bq=256, bk=512)


def _layer_params(layer, full, gains):
    pack = 0 if layer == 0 else 1
    i = layer // 2

    def mat(name):
        lo, _ = _stack_range(name, 4 if name.startswith('ffn') else 2, pack)
        return full[name][(layer if name.startswith('ffn') else i) - lo]

    p = dict(ffn_norm=gains['ffn_norm'][layer][None], ffn_w_in=mat('ffn_w_in'), ffn_w_out=mat('ffn_w_out'))
    if layer % 2 == 0:
        w_in = mat('w_in_ab')
        zeros = jnp.zeros((D_MODEL, 32), w_in.dtype)
        p['w_a'] = jnp.concatenate([w_in[:, :640], zeros, zeros, w_in[:, 640:IN_A], zeros], axis=1)
        p['w_b'] = w_in[:, IN_A:]
        p['w_uq'] = jnp.pad(mat('mla_w_uq'), ((0, 0), (0, 0), (0, MLA_DK - 96))).reshape(MLA_Q_RANK, -1)
        ukv = mat('mla_w_ukv')
        p['w_uk'] = jnp.pad(ukv[:, :, :MLA_NOPE], ((0, 0), (0, 0), (0, MLA_DK - MLA_NOPE))).reshape(MLA_KV_RANK, -1)
        p['w_uv'] = ukv[:, :, MLA_NOPE:].reshape(MLA_KV_RANK, -1)
        p['w_out'] = mat('w_out_ab')
        p['mix_norm'] = gains['mix_norm_ab'][i][None]
        p['q_norm'] = gains['mla_q_norm'][i][None]
        p['kv_norm'] = gains['mla_kv_norm'][i][None]
    else:
        p['w_q'], p['w_kv'], p['w_o'] = mat('gqa_w_q'), mat('gqa_w_kv'), mat('gqa_w_o')
        p['mix_norm'] = gains['mix_norm_c'][i][None]
        p['q_norm'] = jnp.tile(gains['gqa_q_norm'][i][None], (1, 2))
        p['k_norm'] = jnp.tile(gains['gqa_k_norm'][i][None], (1, 2))
    return p


def _even_fwd(x, p, cs, swap, tag):
    xn = _rms_fwd(x, p['mix_norm'], tag + "_norm")
    za = _mm(xn, p['w_a'], name=tag + "_in_a")
    zb = _mm(xn, p['w_b'], out_dtype=BF16, name=tag + "_in_b")
    cq, ckv, krr = _mla_prep(za, p['q_norm'], p['kv_norm'], cs, swap, tag + "_mla_prep")
    q_raw = _mm(cq, p['w_uq'], name=tag + "_uq")
    k_pad = _mm(ckv, p['w_uk'], name=tag + "_uk")
    v = _mm(ckv, p['w_uv'], out_dtype=BF16, name=tag + "_uv")
    qh, kh = _mla_qk(q_raw, k_pad, krr, cs, swap, tag + "_mla_qk")
    o_a, lse_a = _flash_fwd(qh, kh, v, name=tag + "_mla_attn", **dict(_MLA, bq=_MLA_FWD_BQ))
    og, lg = [], []
    for grp in range(DIL_GROUPS):
        o, l = _dil_fwd(zb, grp, f"{tag}_dil{grp}")
        og.append(o)
        lg.append(l)
    o_b, lt = _dil_combine(og, lg, tag + "_dil_merge")
    ocat = jnp.concatenate([o_a, o_b], axis=1)
    x1 = _mm(ocat, p['w_out'], add=x, name=tag + "_out")
    saved = dict(x=x, xn=xn, za=za, zb=zb, cq=cq, ckv=ckv, qh=qh, kh=kh, v=v, lse_a=lse_a, og=og, lg=lg, lt=lt,
                 ocat=ocat)
    return x1, saved


def _even_bwd(dx1, p, sv, cs, swap, seg64, tag):
    docat = _mm(dx1, p['w_out'], mode="nt", name=tag + "_out_dx")
    d_w_out = _mm(sv['ocat'], dx1, mode="tn", name=tag + "_out_dw")
    n_a = MLA_HEADS * MLA_V
    do_a = docat[:, :n_a].astype(BF16)
    res = _dil_combine_bwd(docat[:, n_a:], sv['og'], sv['lg'], sv['lt'], seg64, tag + "_dil_merge_bwd")
    dqs, dks, dvs = [], [], []
    for grp in range(DIL_GROUPS):
        dq, dk, dv = _dil_bwd(sv['zb'], res[grp], sv['lg'][grp], res[3 + grp], grp, f"{tag}_dil{grp}_bwd")
        dqs.append(dq)
        dks.append(dk)
        dvs.append(dv)
    dzb = jnp.concatenate(dqs + dks + dvs, axis=1).astype(BF16)
    dqh, dkh, dv = _flash_bwd(sv['qh'], sv['kh'], sv['v'], sv['ocat'][:, :n_a], do_a, sv['lse_a'],
                              name=tag + "_mla_attn_bwd", unroll=_BWD_UNROLL, **_MLA)
    dq_raw, dkh, dkrr = _mla_qk_bwd(dqh, dkh, cs, swap, tag + "_mla_qk_bwd")
    dcq = _mm(dq_raw, p['w_uq'], mode="nt", name=tag + "_uq_dx")
    d_w_uq = _mm(sv['cq'], dq_raw, mode="tn", name=tag + "_uq_dw")
    dckv = _mm(dkh, p['w_uk'], mode="nt", name=tag + "_uk_dx")
    dckv = _mm(dv, p['w_uv'], mode="nt", add=dckv, name=tag + "_uv_dx")
    d_w_uk = _mm(sv['ckv'], dkh, mode="tn", name=tag + "_uk_dw")
    d_w_uv = _mm(sv['ckv'], dv, mode="tn", name=tag + "_uv_dw")
    dza, d_gq, d_gkv = _mla_prep_bwd(sv['za'], cs, dcq, dckv, dkrr, p['q_norm'], p['kv_norm'], swap,
                                     tag + "_mla_prep_bwd")
    dxn = _mm(dza, p['w_a'], mode="nt", name=tag + "_in_a_dx")
    dxn = _mm(dzb, p['w_b'], mode="nt", add=dxn, name=tag + "_in_b_dx")
    d_w_a = _mm(sv['xn'], dza, mode="tn", name=tag + "_in_a_dw")
    d_w_b = _mm(sv['xn'], dzb, mode="tn", name=tag + "_in_b_dw")
    dx, d_g = _rms_bwd(sv['x'], p['mix_norm'], dxn, dx1, tag + "_norm_bwd")
    d_w_in = jnp.concatenate([d_w_a[:, :640], d_w_a[:, 704:736], d_w_b], axis=1)
    d_uq = d_w_uq.reshape(MLA_Q_RANK, MLA_HEADS, MLA_DK)[:, :, :MLA_NOPE + MLA_ROPE]
    d_ukv = jnp.concatenate([d_w_uk.reshape(MLA_KV_RANK, MLA_HEADS, MLA_DK)[:, :, :MLA_NOPE],
                             d_w_uv.reshape(MLA_KV_RANK, MLA_HEADS, MLA_V)], axis=2)
    grads = dict(w_in_ab=d_w_in, mla_w_uq=d_uq, mla_w_ukv=d_ukv, w_out_ab=d_w_out, mix_norm_ab=d_g[0],
                 mla_q_norm=d_gq[0], mla_kv_norm=d_gkv[0])
    return dx, grads


def _odd_fwd(x, p, cs, seg64, swap, tag):
    xn = _rms_fwd(x, p['mix_norm'], tag + "_norm")
    q_raw = _mm(xn, p['w_q'], name=tag + "_q")
    kv_raw = _mm(xn, p['w_kv'], name=tag + "_kv")
    qh, kh, v = _gqa_prep(q_raw, kv_raw, cs, p['q_norm'], p['k_norm'], seg64, swap, tag + "_gqa_prep")
    o, lse = _flash_fwd(qh, kh, v, name=tag + "_gqa_attn", **_GQA)
    x1 = _mm(o, p['w_o'], add=x, name=tag + "_o")
    return x1, dict(x=x, xn=xn, q_raw=q_raw, kv_raw=kv_raw, qh=qh, kh=kh, v=v, o=o, lse=lse)


def _odd_bwd(dx1, p, sv, cs, seg64, swap, tag):
    do = _mm(dx1, p['w_o'], mode="nt", out_dtype=BF16, name=tag + "_o_dx")
    d_w_o = _mm(sv['o'], dx1, mode="tn", name=tag + "_o_dw")
    dqh, dkh, dv = _flash_bwd(sv['qh'], sv['kh'], sv['v'], sv['o'], do, sv['lse'], name=tag + "_gqa_attn_bwd",
                              unroll=_BWD_UNROLL, **_GQA)
    dq_raw, dkv_raw, d_gq, d_gk = _gqa_prep_bwd(sv['q_raw'], sv['kv_raw'], cs, dqh, dkh, dv, p['q_norm'],
                                                p['k_norm'], seg64, swap, tag + "_gqa_prep_bwd")
    dxn = _mm(dq_raw, p['w_q'], mode="nt", name=tag + "_q_dx")
    dxn = _mm(dkv_raw, p['w_kv'], mode="nt", add=dxn, name=tag + "_kv_dx")
    d_w_q = _mm(sv['xn'], dq_raw, mode="tn", name=tag + "_q_dw")
    d_w_kv = _mm(sv['xn'], dkv_raw, mode="tn", name=tag + "_kv_dw")
    dx, d_g = _rms_bwd(sv['x'], p['mix_norm'], dxn, dx1, tag + "_norm_bwd")
    grads = dict(gqa_w_q=d_w_q, gqa_w_kv=d_w_kv, gqa_w_o=d_w_o, mix_norm_c=d_g[0],
                 gqa_q_norm=d_gq[0, :GQA_HEAD_DIM] + d_gq[0, GQA_HEAD_DIM:],
                 gqa_k_norm=d_gk[0, :GQA_HEAD_DIM] + d_gk[0, GQA_HEAD_DIM:])
    return dx, grads


def _ffn_fwd(x, p, tag):
    xn = _rms_fwd(x, p['ffn_norm'], tag + "_ffn_norm")
    h = _mm(xn, p['ffn_w_in'], out_dtype=BF16, name=tag + "_ffn_in")
    x2 = _mm(h, p['ffn_w_out'], gated=True, add=x, name=tag + "_ffn_out")
    return x2, dict(x=x, xn=xn, h=h)


def _ffn_bwd(dx2, p, sv, tag):
    d_gate, d_up = _mm(dx2, p['ffn_w_out'], mode="nt", gate_up=sv['h'], out_dtype=BF16, name=tag + "_ffn_out_dx")
    d_w_out = _mm(sv['h'], dx2, mode="tn", gated=True, name=tag + "_ffn_out_dw")
    d_w_in = jnp.concatenate([_mm(sv['xn'], d_gate, mode="tn", name=tag + "_ffn_in_dw_gate"),
                              _mm(sv['xn'], d_up, mode="tn", name=tag + "_ffn_in_dw_up")], axis=1)
    dxn = _mm(d_gate, p['ffn_w_in'], mode="nt", b_part=(0, 2), name=tag + "_ffn_in_dx_gate")
    dxn = _mm(d_up, p['ffn_w_in'], mode="nt", b_part=(1, 2), add=dxn, name=tag + "_ffn_in_dx_up")
    dx, d_g = _rms_bwd(sv['x'], p['ffn_norm'], dxn, dx2, tag + "_ffn_norm_bwd")
    return dx, d_w_in, d_w_out, d_g[0]


EVEN_MATS = ('w_in_ab', 'mla_w_uq', 'mla_w_ukv', 'w_out_ab')
ODD_MATS = ('gqa_w_q', 'gqa_w_kv', 'gqa_w_o')
FFN_MATS = ('ffn_w_in', 'ffn_w_out')


def _schedule(x, target, gains, full_of_pack, rest_grads_ready):
    s = x.shape[0]
    cs_mla, cs_gqa = _rope_tables(s)
    swap, seg64 = _swap_matrix(), _seg_matrix(GQA_HEAD_DIM)
    params, saved, full = [], [], None
    for layer in range(4):
        tag = f"l{layer}"
        if layer < 2:
            full = full_of_pack(layer, x)
        p = _layer_params(layer, full, gains)
        if layer % 2 == 0:
            x, sv = _even_fwd(x, p, cs_mla, swap, tag)
        else:
            x, sv = _odd_fwd(x, p, cs_gqa, seg64, swap, tag)
        x, sv_f = _ffn_fwd(x, p, tag)
        params.append(p)
        saved.append((sv, sv_f))
    dx, loss_row, d_final = _loss_head(x, target, gains['final_norm'][None], "loss_head")

    per_layer, rest = {}, None
    for layer in reversed(range(4)):
        p, (sv, sv_f), tag = params[layer], saved[layer], f"l{layer}"
        if layer == 0:
            rest = {n: per_layer[2][n][None] for n in EVEN_MATS}
            rest.update({n: jnp.stack([per_layer[1][n], per_layer[3][n]], axis=0) for n in ODD_MATS})
            rest.update({n: jnp.stack([per_layer[l][n] for l in (1, 2, 3)], axis=0) for n in FFN_MATS})
            token = rest_grads_ready(rest)
            if token is not None:
                p = dict(p, ffn_w_out=p['ffn_w_out'] + token[0, 0].astype(p['ffn_w_out'].dtype))
        dx, d_ffn_in, d_ffn_out, d_ffn_g = _ffn_bwd(dx, p, sv_f, tag)
        if layer % 2 == 0:
            dx, g = _even_bwd(dx, p, sv, cs_mla, swap, seg64, tag)
        else:
            dx, g = _odd_bwd(dx, p, sv, cs_gqa, seg64, swap, tag)
        g.update(ffn_w_in=d_ffn_in, ffn_w_out=d_ffn_out, ffn_norm=d_ffn_g)
        per_layer[layer] = g

    first = {n: per_layer[0][n][None] for n in EVEN_MATS + FFN_MATS}
    small = {'final_norm': d_final[0], 'ffn_norm': jnp.stack([per_layer[l]['ffn_norm'] for l in range(4)], axis=0)}
    for n in ('mix_norm_ab', 'mla_q_norm', 'mla_kv_norm'):
        small[n] = jnp.stack([per_layer[0][n], per_layer[2][n]], axis=0)
    for n in ('mix_norm_c', 'gqa_q_norm', 'gqa_k_norm'):
        small[n] = jnp.stack([per_layer[1][n], per_layer[3][n]], axis=0)
    return loss_row, dx, first, rest, small


def _core_sums(grads, pack, core, tag):
    packed = _pack_full(grads, F32, pack)
    other = _sibling_swap_halves(packed, tag)
    return packed, other, _chip_sum(packed, other, core, tag)


def _finish_reduce_scatter(packed, other, recv, pack, chip, core, tag):
    return _unpack_blocks(_join_halves(_final_sum(packed, other, recv, chip, core, tag), core, tag), pack)


def _step(x, target, w, m, v):
    big_names = [n for n, _, _ in BIG]
    chip = 2 * lax.axis_index("x") + lax.axis_index("y")
    core = lax.axis_index("c")

    gains = {n: w[n] for n, _ in SMALL if n != 'mix_norm_c'}
    c_cols = w['mix_norm_c'].shape[1]
    own_c = lax.dynamic_update_slice(jnp.zeros((2, 4 * c_cols), F32), w['mix_norm_c'], (0, chip * c_cols))
    gains['mix_norm_c'] = _unpack_small(_all_reduce_packet(_pack_small(
        {'mix_norm_c': own_c * 0.5}, jnp.zeros((LANES,), F32))))[1]['mix_norm_c']

    gathered0 = _all_gather_weights(_pack_blocks(w, BF16, 0))
    packed1, gathered0 = lax.optimization_barrier((_pack_blocks(w, BF16, 1), gathered0))
    ag_send, ag_recv, p_thru, g_thru, ag_token = _gather_start(packed1, "gather_rest")
    gains['mix_norm_ab'] = gains['mix_norm_ab'] + ag_token[0, 0]
    full0 = _unpack_gathered(gathered0, 0)

    def full_of_pack(pack, after):
        if pack == 0:
            return full0
        landed = _gather_wait(ag_send, ag_recv, p_thru, g_thru, after, "gather_rest")
        return _unpack_gathered(landed, 1, own=packed1, chip=chip)

    rs = {}

    def rest_grads_ready(rest):
        rs['packed'], rs['other'], sums = _core_sums(rest, 1, core, "grad_rest")
        rs['send'], rs['recv'], rs['t'], rs['b'], token = _send_chip_sums_start(sums, "grad_rest")
        return token

    loss_row, dx, first, rest, small = _schedule(x[0], target[0], gains, full_of_pack, rest_grads_ready)
    recv1 = _send_chip_sums_wait(rs['send'], rs['recv'], rs['t'], rs['b'], dx, "grad_rest")
    g_rest = _finish_reduce_scatter(rs['packed'], rs['other'], recv1, 1, chip, core, "grad_rest")
    packed0, other0, sums0 = _core_sums(first, 0, core, "grad_first")
    g_first = _finish_reduce_scatter(packed0, other0, _send_chip_sums(sums0, "grad_first"), 0, chip, core,
                                     "grad_first")
    g_blocks = {n: (jnp.concatenate([g_first[n], g_rest[n]], axis=0) if n in g_first else g_rest[n])
                for n in big_names}

    loss, g_small = _unpack_small(_all_reduce_packet(_pack_small(small, loss_row[0])))
    g_small['mix_norm_c'] = lax.dynamic_slice(g_small['mix_norm_c'], (0, chip * c_cols), (2, c_cols))

    out_g, out_d, out_m, out_v = {}, {}, {}, {}
    for n in big_names:
        shape = w[n].shape
        cols = shape[-1]
        d_, m_, v_ = _adamw(w[n].reshape(-1, cols), g_blocks[n].reshape(-1, cols), m[n].reshape(-1, cols),
                            v[n].reshape(-1, cols), "adamw_" + n)
        out_g[n], out_d[n], out_m[n], out_v[n] = g_blocks[n], d_.reshape(shape), m_.reshape(shape), v_.reshape(shape)
    for n, _ in SMALL:
        shape = w[n].shape
        as2d = (lambda t: t.reshape(1, -1)) if len(shape) == 1 else (lambda t: t)
        d_, m_, v_ = _adamw(as2d(w[n]), as2d(g_small[n]), as2d(m[n]), as2d(v[n]), "adamw_" + n)
        out_g[n], out_d[n], out_m[n], out_v[n] = g_small[n], d_.reshape(shape), m_.reshape(shape), v_.reshape(shape)
    return (loss, dx[None], *[out_g[n] for n in WEIGHTS], *[out_d[n] for n in WEIGHTS],
            *[out_m[n] for n in WEIGHTS], *[out_v[n] for n in WEIGHTS])


def kernel(x, mix_norm_ab, w_in_ab, mla_q_norm, mla_kv_norm, mla_w_uq, mla_w_ukv, w_out_ab, mix_norm_c, gqa_w_q, gqa_w_kv, gqa_q_norm, gqa_k_norm, gqa_w_o, ffn_norm, ffn_w_in, ffn_w_out, final_norm, loss_target, m_mix_norm_ab, m_w_in_ab, m_mla_q_norm, m_mla_kv_norm, m_mla_w_uq, m_mla_w_ukv, m_w_out_ab, m_mix_norm_c, m_gqa_w_q, m_gqa_w_kv, m_gqa_q_norm, m_gqa_k_norm, m_gqa_w_o, m_ffn_norm, m_ffn_w_in, m_ffn_w_out, m_final_norm, v_mix_norm_ab, v_w_in_ab, v_mla_q_norm, v_mla_kv_norm, v_mla_w_uq, v_mla_w_ukv, v_w_out_ab, v_mix_norm_c, v_gqa_w_q, v_gqa_w_kv, v_gqa_q_norm, v_gqa_k_norm, v_gqa_w_o, v_ffn_norm, v_ffn_w_in, v_ffn_w_out, v_final_norm):
    w = dict(zip(WEIGHTS, (mix_norm_ab, w_in_ab, mla_q_norm, mla_kv_norm, mla_w_uq, mla_w_ukv, w_out_ab, mix_norm_c,
                           gqa_w_q, gqa_w_kv, gqa_q_norm, gqa_k_norm, gqa_w_o, ffn_norm, ffn_w_in, ffn_w_out,
                           final_norm)))
    m = dict(zip(WEIGHTS, (m_mix_norm_ab, m_w_in_ab, m_mla_q_norm, m_mla_kv_norm, m_mla_w_uq, m_mla_w_ukv,
                           m_w_out_ab, m_mix_norm_c, m_gqa_w_q, m_gqa_w_kv, m_gqa_q_norm, m_gqa_k_norm, m_gqa_w_o,
                           m_ffn_norm, m_ffn_w_in, m_ffn_w_out, m_final_norm)))
    v = dict(zip(WEIGHTS, (v_mix_norm_ab, v_w_in_ab, v_mla_q_norm, v_mla_kv_norm, v_mla_w_uq, v_mla_w_ukv,
                           v_w_out_ab, v_mix_norm_c, v_gqa_w_q, v_gqa_w_kv, v_gqa_q_norm, v_gqa_k_norm, v_gqa_w_o,
                           v_ffn_norm, v_ffn_w_in, v_ffn_w_out, v_final_norm)))
    return _step(x, loss_target, w, m, v)
```

```python
import math

import numpy as np
import jax
import jax.numpy as jnp
from jax import lax
from jax.experimental import pallas as pl
from jax.experimental.pallas import tpu as pltpu

F32 = jnp.float32
BF16 = jnp.bfloat16
MESH = pl.DeviceIdType.MESH

VMEM_LIMIT_BYTES = 56 * 1024 * 1024
LANES = 128

D_MODEL = 1024
NORM_EPS = 1e-6
ROPE_THETA = 10000.0
NEG_INF = -1e30
GRID_W = 64

MLA_HEADS, MLA_Q_RANK, MLA_KV_RANK, MLA_NOPE, MLA_ROPE, MLA_V = 8, 384, 256, 64, 32, 64
MLA_DK = 128
DIL_PAIRS = ((128, 1), (512, 4), (2048, 16))
DIL_HALF, DIL_SLOTS, DIL_GROUPS, DIL_HEAD_DIM = 64, 4, 3, 64
DIL_HEADS = DIL_SLOTS * DIL_GROUPS
DIL_W = DIL_SLOTS * DIL_HEAD_DIM
GQA_HEADS, GQA_KV_HEADS, GQA_HEAD_DIM = 16, 4, 64
FFN_HIDDEN = 2816
IN_A = MLA_Q_RANK + MLA_KV_RANK + MLA_ROPE
IN_A_PAD = 768
IN_B = 3 * DIL_HEADS * DIL_HEAD_DIM

ADAM_LR, ADAM_B1, ADAM_B2, ADAM_EPS, ADAM_WD, ADAM_STEP = 0.001, 0.9, 0.999, 1e-08, 0.01, 10

LOG2E, LN2 = math.log2(math.e), math.log(2.0)
MLA_SCALE = (MLA_NOPE + MLA_ROPE) ** -0.5
GQA_SCALE = GQA_HEAD_DIM ** -0.5

WEIGHTS = ['mix_norm_ab', 'w_in_ab', 'mla_q_norm', 'mla_kv_norm', 'mla_w_uq', 'mla_w_ukv', 'w_out_ab', 'mix_norm_c',
           'gqa_w_q', 'gqa_w_kv', 'gqa_q_norm', 'gqa_k_norm', 'gqa_w_o', 'ffn_norm', 'ffn_w_in', 'ffn_w_out',
           'final_norm']
BIG = (('w_in_ab', (2, 1024, 744), 2), ('mla_w_uq', (2, 96, 8, 96), 1), ('mla_w_ukv', (2, 64, 8, 128), 1),
       ('w_out_ab', (2, 768, 256), 2), ('gqa_w_q', (2, 256, 1024), 1), ('gqa_w_kv', (2, 256, 512), 1),
       ('gqa_w_o', (2, 256, 1024), 1), ('ffn_w_in', (4, 1024, 1408), 2), ('ffn_w_out', (4, 704, 1024), 1))
PACK_W = 1024
SMALL = (('mix_norm_ab', (2, 1024)), ('mla_q_norm', (2, 384)), ('mla_kv_norm', (2, 256)), ('gqa_q_norm', (2, 64)),
         ('gqa_k_norm', (2, 64)), ('ffn_norm', (4, 1024)), ('final_norm', (1024,)), ('mix_norm_c', (2, 1024)))
PACKET_ROWS = 88


def _cparams(sem=None):
    return pltpu.CompilerParams(dimension_semantics=sem, vmem_limit_bytes=VMEM_LIMIT_BYTES)


def _pick(n, pref, mult=LANES):
    if n <= pref:
        return n
    for d in range(pref - pref % mult, 0, -mult):
        if n % d == 0:
            return d
    return n


_DIMS = {"nn": (((1,), (0,)), ((), ())), "nt": (((1,), (1,)), ((), ())), "tn": (((0,), (0,)), ((), ()))}


def _sigmoid(x):
    return 0.5 * (1.0 + jnp.tanh(0.5 * x))


def _silu_mul(gate, up):
    gate, up = gate.astype(F32), up.astype(F32)
    return gate * _sigmoid(gate) * up


def _silu_mul_bwd(gate, up, da):
    gate, up = gate.astype(F32), up.astype(F32)
    sig = _sigmoid(gate)
    silu = gate * sig
    return da * up * (sig + silu * (1.0 - sig)), da * silu


def _mm(a, b, *, mode="nn", add=None, out_dtype=F32, gated=False, gate_up=None, b_part=None, name="mm"):
    if mode == "nn":
        (m, k), (k2, n) = a.shape, b.shape
    elif mode == "nt":
        (m, k), (n, k2) = a.shape, b.shape
    else:
        (k, m), (k2, n) = a.shape, b.shape
    if gated:
        k, m = (k // 2, m) if mode == "nn" else (k, m // 2)
    if b_part is not None:
        assert mode == "nt" and k2 == k * b_part[1]
        k2 = k
    assert k == k2, (a.shape, b.shape, mode)
    if mode == "tn":
        deep = a.dtype == BF16 and b.dtype == BF16
        bm, bn, bk = _pick(m, 1408), _pick(n, 1024), _pick(k, 2048 if deep else 1024, 16)
    else:
        bm, bn, bk = _pick(m, 512, 16), _pick(n, 1408), _pick(k, 2816)
    nk = k // bk
    assert m % bm == 0 and n % bn == 0 and k % bk == 0
    has_add, has_gu = add is not None, gate_up is not None
    assert not (has_add and has_gu) and not (gated and mode == "nt") and not (has_gu and mode != "nt")
    n_in = 2 + int(gated) + int(has_add) + 2 * int(has_gu)

    def body(*refs):
        a_val = _silu_mul(refs[0][...], refs[1][...]) if gated else refs[0][...]
        b_ref = refs[1 + int(gated)]
        part = lax.dot_general(a_val.astype(BF16), b_ref[...].astype(BF16), _DIMS[mode],
                               preferred_element_type=F32)

        def finish(r):
            if has_gu:
                d_gate, d_up = _silu_mul_bwd(refs[2][...], refs[3][...], r)
                refs[n_in][...] = d_gate.astype(refs[n_in].dtype)
                refs[n_in + 1][...] = d_up.astype(refs[n_in + 1].dtype)
                return
            if has_add:
                r = r + refs[n_in - 1][...]
            refs[n_in][...] = r.astype(refs[n_in].dtype)

        if nk == 1:
            finish(part)
        else:
            acc_ref = refs[-1]
            kk = pl.program_id(2)

            @pl.when(kk == 0)
            def _():
                acc_ref[...] = part

            @pl.when(kk > 0)
            def _():
                acc_ref[...] += part

            @pl.when(kk == nk - 1)
            def _():
                finish(acc_ref[...])

    a_bytes, b_bytes = a.size * a.dtype.itemsize, b.size * b.dtype.itemsize
    n_outer = nk == 1 and (n // bn) * a_bytes + b_bytes < a_bytes + (m // bm) * b_bytes

    def at(f):
        return (lambda j, i, kk: f(i, j, kk)) if n_outer else f

    if mode == "nn":
        a_specs = [pl.BlockSpec((bm, bk), at(lambda i, j, kk, o=o: (i, kk + o))) for o in ((0, nk) if gated else (0,))]
        b_spec = pl.BlockSpec((bk, bn), at(lambda i, j, kk: (kk, j)))
    elif mode == "nt":
        a_specs = [pl.BlockSpec((bm, bk), at(lambda i, j, kk: (i, kk)))]
        b_off = 0 if b_part is None else b_part[0] * nk
        b_spec = pl.BlockSpec((bn, bk), at(lambda i, j, kk: (j, kk + b_off)))
    else:
        a_specs = [pl.BlockSpec((bk, bm), at(lambda i, j, kk, o=o: (kk, i + o)))
                   for o in ((0, m // bm) if gated else (0,))]
        b_spec = pl.BlockSpec((bk, bn), at(lambda i, j, kk: (kk, j)))
    o_spec = pl.BlockSpec((bm, bn), at(lambda i, j, kk: (i, j)))
    in_specs, args = a_specs + [b_spec], [a] * len(a_specs) + [b]
    if has_add:
        in_specs, args = in_specs + [o_spec], args + [add]
    if has_gu:
        in_specs += [o_spec, pl.BlockSpec((bm, bn), at(lambda i, j, kk: (i, j + n // bn)))]
        args += [gate_up, gate_up]
    out = jax.ShapeDtypeStruct((m, n), out_dtype)
    grid = (n // bn, m // bm, nk) if n_outer else (m // bm, n // bn, nk)
    return pl.pallas_call(
        body, name=name, grid=grid, in_specs=in_specs, out_specs=[o_spec, o_spec] if has_gu else o_spec,
        out_shape=[out, out] if has_gu else out,
        scratch_shapes=[pltpu.VMEM((bm, bn), F32)] if nk > 1 else [],
        compiler_params=_cparams(("parallel", "parallel", "arbitrary")),
    )(*args)


def _rows_call(fn, rows, consts, out_rows, out_accs=(), *, bs=256, name):
    s = rows[0].shape[0]
    bs = min(bs, s)
    assert s % bs == 0
    nr, nc, no, na = len(rows), len(consts), len(out_rows), len(out_accs)

    def body(*refs):
        vals = [r[...] for r in refs[:nr + nc]]
        outs = refs[nr + nc:]
        res = fn(*vals)
        if not isinstance(res, (tuple, list)):
            res = (res,)
        assert len(res) == no + na, (len(res), no, na)
        for r, v in zip(outs[:no], res[:no]):
            r[...] = v.astype(r.dtype)
        if na:
            i = pl.program_id(0)
            for r, v in zip(outs[no:], res[no:]):
                @pl.when(i == 0)
                def _(r=r, v=v):
                    r[...] = v

                @pl.when(i > 0)
                def _(r=r, v=v):
                    r[...] += v

    in_specs = [pl.BlockSpec((bs, a.shape[1]), lambda i: (i, 0)) for a in rows]
    in_specs += [pl.BlockSpec(c.shape, lambda i: (0, 0)) for c in consts]
    out_specs = [pl.BlockSpec((bs, c), lambda i: (i, 0)) for c, _ in out_rows]
    out_specs += [pl.BlockSpec(tuple(sh), lambda i: (0, 0)) for sh in out_accs]
    out_shape = [jax.ShapeDtypeStruct((s, c), dt) for c, dt in out_rows]
    out_shape += [jax.ShapeDtypeStruct(tuple(sh), F32) for sh in out_accs]
    res = pl.pallas_call(
        body, name=name, grid=(s // bs,), in_specs=in_specs, out_specs=out_specs, out_shape=out_shape,
        compiler_params=_cparams(("arbitrary",) if na else ("parallel",)),
    )(*rows, *consts)
    return res


def _rms(x, g):
    return x * lax.rsqrt(jnp.mean(x * x, axis=-1, keepdims=True) + NORM_EPS) * g


def _rms_bwd_math(x, g, dy):
    r = lax.rsqrt(jnp.mean(x * x, axis=-1, keepdims=True) + NORM_EPS)
    u = dy * g
    dx = r * u - x * (r * r * r) * jnp.mean(u * x, axis=-1, keepdims=True)
    dg = jnp.sum(dy * x * r, axis=0, keepdims=True)
    return dx, dg


NORM_ROWS = 512


def _rms_fwd(x, g, name):
    return _rows_call(lambda xv, gv: _rms(xv, gv), [x], [g], [(x.shape[1], BF16)], bs=NORM_ROWS, name=name)[0]


def _rms_bwd(x, g, dy, dres, name):
    def fn(xv, dyv, dresv, gv):
        dx, dg = _rms_bwd_math(xv, gv, dyv.astype(F32))
        return dx + dresv, dg
    return _rows_call(fn, [x, dy, dres], [g], [(x.shape[1], F32)], [(1, x.shape[1])], bs=NORM_ROWS, name=name)


def _chunkdot(x, m):
    outs = [jnp.dot(x[:, c:c + LANES], m, precision=lax.Precision.HIGHEST, preferred_element_type=F32)
            for c in range(0, x.shape[1], LANES)]
    return outs[0] if len(outs) == 1 else jnp.concatenate(outs, axis=1)


def _lanes(t, width):
    n = width // LANES
    return t if n == 1 else jnp.concatenate([t] * n, axis=1)


def _rope(x, cs, swap):
    w = x.shape[1]
    return x * _lanes(cs[:, :LANES], w) + _chunkdot(x, swap) * _lanes(cs[:, LANES:], w)


def _rope_t(dy, cs, swap):
    w = dy.shape[1]
    return dy * _lanes(cs[:, :LANES], w) + _chunkdot(dy * _lanes(cs[:, LANES:], w), swap)


def _swap_matrix():
    m = np.zeros((LANES, LANES), np.float32)
    for j in range(LANES):
        src = j + 16 if (j % 32) < 16 else j - 16
        m[src, j] = 1.0
    return jnp.asarray(m)


def _seg_matrix(seg):
    idx = np.arange(LANES) // seg
    return jnp.asarray((idx[:, None] == idx[None, :]).astype(np.float32))


def _rope_tables(s):
    pos = jnp.arange(s)

    def angles(p, dim):
        freqs = ROPE_THETA ** (-jnp.arange(0, dim, 2, dtype=F32) / dim)
        ang = p.astype(F32)[:, None] * freqs[None, :]
        return jnp.cos(ang), jnp.sin(ang)

    cos_t, sin_t = angles(pos, MLA_ROPE)
    one, zero = jnp.ones((s, 64), F32), jnp.zeros((s, 64), F32)
    mla = jnp.concatenate([one, cos_t, cos_t, one[:, :32], zero, -sin_t, sin_t, zero[:, :32]], axis=1)
    cos_r, sin_r = angles(pos // GRID_W, GQA_HEAD_DIM // 2)
    cos_c, sin_c = angles(pos % GRID_W, GQA_HEAD_DIM // 2)
    c64 = jnp.concatenate([cos_r, cos_r, cos_c, cos_c], axis=1)
    s64 = jnp.concatenate([-sin_r, sin_r, -sin_c, sin_c], axis=1)
    gqa = jnp.concatenate([c64, c64, s64, s64], axis=1)
    return mla, gqa


def _stack_heads(ref, heads, d, dtype=None):
    parts = [ref[:, hd * d:(hd + 1) * d] for hd in heads]
    out = parts[0] if len(parts) == 1 else jnp.concatenate(parts, axis=0)
    return out if dtype is None else out.astype(dtype)


def _fill_v_ones(v_ref, va_ref, hb, dv):
    @pl.when(pl.program_id(1) == 0)
    def _():
        ones = jnp.ones((v_ref.shape[0], dv), BF16)
        for h in range(hb):
            va_ref[:, 2 * h * dv:(2 * h + 1) * dv] = v_ref[:, h * dv:(h + 1) * dv]
            va_ref[:, (2 * h + 1) * dv:(2 * h + 2) * dv] = ones


def _flash_fwd(q, k, v, *, R, dk, dv, hb, bq, bk, name):
    s = q.shape[0]
    g = k.shape[1] // dk
    ng = g // hb
    bq, bk = min(bq, s), min(bk, s)
    nq, nkb = s // bq, s // bk
    rb = R * bq

    def body(q_ref, k_ref, v_ref, o_ref, lse_ref, va_ref):
        _fill_v_ones(v_ref, va_ref, hb, dv)
        head_sets = [[h * R + r for r in range(R)] for h in range(hb)]
        qss = [_stack_heads(q_ref, heads, dk) for heads in head_sets]

        def step(jj, carry):
            carry = list(carry)
            rows = [pl.ds(pl.multiple_of((jj * unroll + u) * bk, bk), bk) for u in range(unroll)]
            scs = [[lax.dot_general(qss[h], k_ref[rows[u], h * dk:(h + 1) * dk], _DIMS["nt"],
                                    preferred_element_type=F32) for h in range(hb)] for u in range(unroll)]
            for u in range(unroll):
                m2s = [jnp.maximum(carry[h][0], jnp.max(scs[u][h], axis=1, keepdims=True)) for h in range(hb)]
                ps = [jnp.exp2(scs[u][h] - m2s[h]).astype(BF16) for h in range(hb)]
                pvs = [jnp.dot(ps[h], va_ref[rows[u], 2 * h * dv:2 * (h + 1) * dv], preferred_element_type=F32)
                       for h in range(hb)]
                carry = [(m2s[h], jnp.exp2(carry[h][0] - m2s[h]) * carry[h][1] + pvs[h]) for h in range(hb)]
            return tuple(carry)

        unroll = 4 if nkb % 4 == 0 else 1
        init = tuple((jnp.full((rb, 1), NEG_INF, F32), jnp.zeros((rb, 2 * dv), F32)) for _ in range(hb))
        final = lax.fori_loop(0, nkb // unroll, step, init)
        for h, heads in enumerate(head_sets):
            m, acc = final[h]
            l = acc[:, dv:dv + 1]
            o = acc[:, :dv] / l
            lse = m + jnp.log2(l)
            for r, hd in enumerate(heads):
                o_ref[:, hd * dv:(hd + 1) * dv] = o[r * bq:(r + 1) * bq].astype(o_ref.dtype)
                lse_ref[0, :, hd:hd + 1] = lse[r * bq:(r + 1) * bq]

    return pl.pallas_call(
        body, name=name, grid=(ng, nq),
        in_specs=[pl.BlockSpec((bq, hb * R * dk), lambda gi, i: (i, gi)),
                  pl.BlockSpec((s, hb * dk), lambda gi, i: (0, gi)),
                  pl.BlockSpec((s, hb * dv), lambda gi, i: (0, gi))],
        out_specs=[pl.BlockSpec((bq, hb * R * dv), lambda gi, i: (i, gi)),
                   pl.BlockSpec((1, bq, hb * R), lambda gi, i: (gi, i, 0))],
        out_shape=[jax.ShapeDtypeStruct((s, g * R * dv), BF16), jax.ShapeDtypeStruct((ng, s, hb * R), F32)],
        scratch_shapes=[pltpu.VMEM((s, 2 * hb * dv), BF16)],
        compiler_params=_cparams(("parallel", "arbitrary")),
    )(q, k, v)


def _flash_bwd(q, k, v, o, do, lse, *, R, dk, dv, hb, bq, bk, name, unroll=1):
    s = q.shape[0]
    g = k.shape[1] // dk
    ng = g // hb
    bq, bk = min(bq, s), min(bk, s)
    nq, nkb = s // bq, s // bk
    rb = R * bq

    def body(q_ref, k_ref, v_ref, o_ref, do_ref, lse_ref, dq_ref, dkt_ref, dvt_ref, va_ref):
        @pl.when(pl.program_id(1) == 0)
        def _():
            dkt_ref[...] = jnp.zeros(dkt_ref.shape, F32)
            dvt_ref[...] = jnp.zeros(dvt_ref.shape, F32)

        _fill_v_ones(v_ref, va_ref, hb, dv)
        lane = lax.broadcasted_iota(jnp.int32, (rb, dv), 1)
        head_sets = [[h * R + r for r in range(R)] for h in range(hb)]
        q_t = jnp.transpose(q_ref[...].astype(F32)).astype(BF16)
        do_t = jnp.transpose(do_ref[...].astype(F32)).astype(BF16)

        def stack_t(t, heads, d):
            parts = [t[hd * d:(hd + 1) * d] for hd in heads]
            return parts[0] if len(parts) == 1 else jnp.concatenate(parts, axis=1)

        qss, qts, dots, dosas, lcols = [], [], [], [], []
        for heads in head_sets:
            dos = _stack_heads(do_ref, heads, dv, BF16)
            delta = jnp.sum(dos.astype(F32) * _stack_heads(o_ref, heads, dv, F32), axis=1, keepdims=True)
            hi = delta.astype(BF16).astype(F32)
            lo = delta - hi
            qss.append(_stack_heads(q_ref, heads, dk))
            qts.append(stack_t(q_t, heads, dk))
            dots.append(stack_t(do_t, heads, dv))
            dosas.append(jnp.concatenate(
                [dos, jnp.where(lane == 0, -hi, jnp.where(lane == 1, -lo, 0.0)).astype(BF16)], axis=1))
            cols = [lse_ref[0, :, hd:hd + 1] for hd in heads]
            lcols.append(cols[0] if R == 1 else jnp.concatenate(cols, axis=0))

        def step(jj, dqs):
            hs = range(hb)
            for u in range(unroll):
                j = jj * unroll + u
                r0 = pl.multiple_of(j * bk, bk)
                kjs = [k_ref[pl.ds(r0, bk), h * dk:(h + 1) * dk] for h in hs]
                vas = [va_ref[pl.ds(r0, bk), 2 * h * dv:2 * (h + 1) * dv] for h in hs]
                ss = [lax.dot_general(qss[h], kjs[h], _DIMS["nt"], preferred_element_type=F32) for h in hs]
                dps = [lax.dot_general(dosas[h], vas[h], _DIMS["nt"], preferred_element_type=F32) for h in hs]
                ps = [jnp.exp2(ss[h] - lcols[h]) for h in hs]
                pbs = [ps[h].astype(BF16) for h in hs]
                dss = [(ps[h] * dps[h]).astype(BF16) for h in hs]
                for h in hs:
                    dvt_ref[0, j, h * dv:(h + 1) * dv, :] += jnp.dot(dots[h], pbs[h], preferred_element_type=F32)
                for h in hs:
                    dkt_ref[0, j, h * dk:(h + 1) * dk, :] += jnp.dot(qts[h], dss[h], preferred_element_type=F32)
                dqs = tuple(dqs[h] + jnp.dot(dss[h], kjs[h], preferred_element_type=F32) for h in hs)
            return dqs

        assert nkb % unroll == 0
        dqs = lax.fori_loop(0, nkb // unroll, step, tuple(jnp.zeros((rb, dk), F32) for _ in range(hb)))
        for h, heads in enumerate(head_sets):
            for r, hd in enumerate(heads):
                dq_ref[:, hd * dk:(hd + 1) * dk] = dqs[h][r * bq:(r + 1) * bq]

    qspec = pl.BlockSpec((bq, hb * R * dk), lambda gi, i: (i, gi))
    ospec = pl.BlockSpec((bq, hb * R * dv), lambda gi, i: (i, gi))
    kspec = pl.BlockSpec((s, hb * dk), lambda gi, i: (0, gi))
    vspec = pl.BlockSpec((s, hb * dv), lambda gi, i: (0, gi))
    dq, dkt, dvt = pl.pallas_call(
        body, name=name, grid=(ng, nq),
        in_specs=[qspec, kspec, vspec, ospec, ospec, pl.BlockSpec((1, bq, hb * R), lambda gi, i: (gi, i, 0))],
        out_specs=[qspec, pl.BlockSpec((1, nkb, hb * dk, bk), lambda gi, i: (gi, 0, 0, 0)),
                   pl.BlockSpec((1, nkb, hb * dv, bk), lambda gi, i: (gi, 0, 0, 0))],
        out_shape=[jax.ShapeDtypeStruct((s, g * R * dk), F32), jax.ShapeDtypeStruct((ng, nkb, hb * dk, bk), F32),
                   jax.ShapeDtypeStruct((ng, nkb, hb * dv, bk), F32)],
        scratch_shapes=[pltpu.VMEM((s, 2 * hb * dv), BF16)],
        compiler_params=_cparams(("parallel", "arbitrary")),
    )(q, k, v, o, do, lse)
    return dq, _keys_first(dkt, name + "_dk"), _keys_first(dvt, name + "_dv")


def _keys_first(t, name):
    ng, nkb, f, bk = t.shape

    def body(t_ref, o_ref):
        for j in range(nkb):
            o_ref[j * bk:(j + 1) * bk, :] = jnp.transpose(t_ref[0, j])

    return pl.pallas_call(
        body, name=name, grid=(ng,),
        in_specs=[pl.BlockSpec((1, nkb, f, bk), lambda gi: (gi, 0, 0, 0))],
        out_specs=pl.BlockSpec((nkb * bk, f), lambda gi: (0, gi)),
        out_shape=jax.ShapeDtypeStruct((nkb * bk, ng * f), t.dtype),
        compiler_params=_cparams(("parallel",)),
    )(t)


DIL_T = 1024
DIL_P = DIL_HALF
DIL_NCOL = IN_B // DIL_W


DIL_BATCH = 4


def _alibi_slope(head):
    return float(2.0 ** (-8.0 * (head + 1) / DIL_HEADS))


def _slot(sl_i):
    return slice(sl_i * DIL_HEAD_DIM, (sl_i + 1) * DIL_HEAD_DIM)


def _halo_specs(d, col, s, t):
    h = DIL_P * d
    per, last = t // h, s // h - 1
    return [pl.BlockSpec((h, DIL_W), lambda c: (jnp.maximum(c * per - 1, 0), col)),
            pl.BlockSpec((t, DIL_W), lambda c: (c, col)),
            pl.BlockSpec((h, DIL_W), lambda c: (jnp.minimum((c + 1) * per, last), col))]


def _staging(rows):
    return tuple(pltpu.VMEM((rows, LANES), F32) for _ in range(DIL_W // LANES))


def _stage(buf, refs):
    off = 0
    for r in refs:
        val = r[...].astype(F32)
        for j in range(DIL_W // LANES):
            buf[j][off:off + r.shape[0], :] = val[:, j * LANES:(j + 1) * LANES]
        off += r.shape[0]


def _unstage(buf, ref):
    ref[...] = jnp.concatenate([half[...] for half in buf], axis=1).astype(ref.dtype)


def _sub_tiles(d, t):
    return [(b * DIL_P * d + r, b * DIL_P) for b in range(t // (DIL_P * d)) for r in range(d)]


def _rows(start, size, d):
    return pl.ds(start, size, stride=d) if d > 1 else pl.ds(start, size)


def _strided(buf, start, size, d):
    return jnp.concatenate([half[_rows(start, size, d), :] for half in buf], axis=1)


def _put_strided(buf, start, d, val):
    for j in range(DIL_W // LANES):
        buf[j][_rows(start, val.shape[0], d), :] = val[:, j * LANES:(j + 1) * LANES]


def _band(u0, length, d, queries_wide):
    if queries_wide:
        shape = (3 * DIL_P, DIL_P)
        wide = u0 - DIL_P + lax.broadcasted_iota(jnp.int32, shape, 0)
        narrow = u0 + lax.broadcasted_iota(jnp.int32, shape, 1)
    else:
        shape = (DIL_P, 3 * DIL_P)
        narrow = u0 + lax.broadcasted_iota(jnp.int32, shape, 0)
        wide = u0 - DIL_P + lax.broadcasted_iota(jnp.int32, shape, 1)
    rel = jnp.abs(wide - narrow)
    valid = (rel <= DIL_HALF) & (wide >= 0) & (wide < length)
    return valid, rel.astype(F32) * float(d)


def _dil_fwd(zb, grp, name):
    s = zb.shape[0]
    d = DIL_PAIRS[grp][1]
    t = min(DIL_T, s)
    h = DIL_P * d
    scale = DIL_HEAD_DIM ** -0.5

    def body(q_ref, kp, kc, kn, vp, vc, vn, o_ref, lse_ref, qbuf, kbuf, vbuf, obuf, lbuf):
        _stage(qbuf, (q_ref,))
        _stage(kbuf, (kp, kc, kn))
        _stage(vbuf, (vp, vc, vn))
        u_step = pl.program_id(0) * (t // d)
        tiles = _sub_tiles(d, t)
        for g0 in range(0, len(tiles), DIL_BATCH):
            batch = tiles[g0:g0 + DIL_BATCH]
            masks = [_band(u_step + u, s // d, d, False) for _, u in batch]
            qs = [_strided(qbuf, row, DIL_P, d).astype(BF16) for row, _ in batch]
            ks = [_strided(kbuf, row, 3 * DIL_P, d).astype(BF16) for row, _ in batch]
            vs = [_strided(vbuf, row, 3 * DIL_P, d).astype(BF16) for row, _ in batch]
            chains = [(i, sl_i) for i in range(len(batch)) for sl_i in range(DIL_SLOTS)]
            scs = [lax.dot_general(qs[i][:, _slot(sl_i)], ks[i][:, _slot(sl_i)], _DIMS["nt"],
                                   preferred_element_type=F32) for i, sl_i in chains]
            scs = [jnp.where(masks[i][0], sc * scale - _alibi_slope(grp * DIL_SLOTS + sl_i) * masks[i][1], NEG_INF)
                   for (i, sl_i), sc in zip(chains, scs)]
            ms = [jnp.max(sc, axis=1, keepdims=True) for sc in scs]
            es = [jnp.exp(sc - m) for sc, m in zip(scs, ms)]
            dens = [jnp.sum(e, axis=1, keepdims=True) for e in es]
            outs = [jnp.dot((e / den).astype(BF16), vs[i][:, _slot(sl_i)], preferred_element_type=F32)
                    for (i, sl_i), e, den in zip(chains, es, dens)]
            lses = [jnp.broadcast_to(m + jnp.log(den), (DIL_P, DIL_HEAD_DIM)) for m, den in zip(ms, dens)]
            for i, (row, _) in enumerate(batch):
                pick = slice(i * DIL_SLOTS, (i + 1) * DIL_SLOTS)
                _put_strided(obuf, row, d, jnp.concatenate(outs[pick], axis=1))
                _put_strided(lbuf, row, d, jnp.concatenate(lses[pick], axis=1))
        _unstage(obuf, o_ref)
        _unstage(lbuf, lse_ref)

    own = pl.BlockSpec((t, DIL_W), lambda c: (c, 0))
    return pl.pallas_call(
        body, name=name, grid=(s // t,),
        in_specs=[pl.BlockSpec((t, DIL_W), lambda c: (c, grp))] + _halo_specs(d, 3 + grp, s, t)
        + _halo_specs(d, 6 + grp, s, t),
        out_specs=[own, own], out_shape=[jax.ShapeDtypeStruct((s, DIL_W), F32)] * 2,
        scratch_shapes=[_staging(t), _staging(t + 2 * h), _staging(t + 2 * h), _staging(t), _staging(t)],
        compiler_params=_cparams(("parallel",)),
    )(zb, zb, zb, zb, zb, zb, zb)


def _dil_bwd(zb, do, lse, dl, grp, name):
    s = zb.shape[0]
    d = DIL_PAIRS[grp][1]
    t = min(DIL_T, s)
    h = DIL_P * d
    scale = DIL_HEAD_DIM ** -0.5

    def chain_grads(qs, ks, vs, dos, lses, dls, masks):
        chains = [(i, sl_i) for i in range(len(qs)) for sl_i in range(DIL_SLOTS)]
        scs = [lax.dot_general(qs[i][:, _slot(sl_i)], ks[i][:, _slot(sl_i)], _DIMS["nt"],
                               preferred_element_type=F32) for i, sl_i in chains]
        dps = [lax.dot_general(dos[i][:, _slot(sl_i)], vs[i][:, _slot(sl_i)], _DIMS["nt"],
                               preferred_element_type=F32) for i, sl_i in chains]
        ps = [jnp.exp(jnp.where(masks[i][0], sc * scale - _alibi_slope(grp * DIL_SLOTS + sl_i) * masks[i][1],
                                NEG_INF) - lses[i][:, sl_i * DIL_HEAD_DIM:sl_i * DIL_HEAD_DIM + 1])
              for (i, sl_i), sc in zip(chains, scs)]
        dss = [(p * (dp - dls[i][:, sl_i * DIL_HEAD_DIM:sl_i * DIL_HEAD_DIM + 1]) * scale).astype(BF16)
               for (i, sl_i), p, dp in zip(chains, ps, dps)]
        return chains, ps, dss

    def dq_body(q_ref, kp, kc, kn, vp, vc, vn, do_ref, lse_ref, dl_ref, dq_ref, qbuf, kbuf, vbuf, dobuf, lsebuf,
                dlbuf, obuf):
        _stage(qbuf, (q_ref,))
        _stage(dobuf, (do_ref,))
        _stage(lsebuf, (lse_ref,))
        _stage(dlbuf, (dl_ref,))
        _stage(kbuf, (kp, kc, kn))
        _stage(vbuf, (vp, vc, vn))
        u_step = pl.program_id(0) * (t // d)
        tiles = _sub_tiles(d, t)
        for g0 in range(0, len(tiles), DIL_BATCH):
            batch = tiles[g0:g0 + DIL_BATCH]
            masks = [_band(u_step + u, s // d, d, False) for _, u in batch]
            narrow = [[_strided(b, row, DIL_P, d) for row, _ in batch] for b in (qbuf, dobuf, lsebuf, dlbuf)]
            ks = [_strided(kbuf, row, 3 * DIL_P, d).astype(BF16) for row, _ in batch]
            vs = [_strided(vbuf, row, 3 * DIL_P, d).astype(BF16) for row, _ in batch]
            chains, _, dss = chain_grads([a.astype(BF16) for a in narrow[0]], ks, vs,
                                         [a.astype(BF16) for a in narrow[1]], narrow[2], narrow[3], masks)
            outs = [jnp.dot(ds, ks[i][:, _slot(sl_i)], preferred_element_type=F32)
                    for (i, sl_i), ds in zip(chains, dss)]
            for i, (row, _) in enumerate(batch):
                _put_strided(obuf, row, d, jnp.concatenate(outs[i * DIL_SLOTS:(i + 1) * DIL_SLOTS], axis=1))
        _unstage(obuf, dq_ref)

    def dkv_body(k_ref, v_ref, qp, qc, qn, dop, doc, don, lp, lc, ln, dlp, dlc, dln, dk_ref, dv_ref,
                 kbuf, vbuf, qbuf, dobuf, lsebuf, dlbuf, dkbuf, dvbuf):
        _stage(kbuf, (k_ref,))
        _stage(vbuf, (v_ref,))
        _stage(qbuf, (qp, qc, qn))
        _stage(dobuf, (dop, doc, don))
        _stage(lsebuf, (lp, lc, ln))
        _stage(dlbuf, (dlp, dlc, dln))
        u_step = pl.program_id(0) * (t // d)
        tiles = _sub_tiles(d, t)
        for g0 in range(0, len(tiles), DIL_BATCH):
            batch = tiles[g0:g0 + DIL_BATCH]
            masks = [_band(u_step + u, s // d, d, True) for _, u in batch]
            ks = [_strided(kbuf, row, DIL_P, d).astype(BF16) for row, _ in batch]
            vs = [_strided(vbuf, row, DIL_P, d).astype(BF16) for row, _ in batch]
            wide_ = [[_strided(b, row, 3 * DIL_P, d) for row, _ in batch] for b in (qbuf, dobuf, lsebuf, dlbuf)]
            qs, dos = [a.astype(BF16) for a in wide_[0]], [a.astype(BF16) for a in wide_[1]]
            chains, ps, dss = chain_grads(qs, ks, vs, dos, wide_[2], wide_[3], masks)
            dvs = [lax.dot_general(p.astype(BF16), dos[i][:, _slot(sl_i)], _DIMS["tn"], preferred_element_type=F32)
                   for (i, sl_i), p in zip(chains, ps)]
            dks = [lax.dot_general(ds, qs[i][:, _slot(sl_i)], _DIMS["tn"], preferred_element_type=F32)
                   for (i, sl_i), ds in zip(chains, dss)]
            for i, (row, _) in enumerate(batch):
                pick = slice(i * DIL_SLOTS, (i + 1) * DIL_SLOTS)
                _put_strided(dkbuf, row, d, jnp.concatenate(dks[pick], axis=1))
                _put_strided(dvbuf, row, d, jnp.concatenate(dvs[pick], axis=1))
        _unstage(dkbuf, dk_ref)
        _unstage(dvbuf, dv_ref)

    def zcur(col):
        return pl.BlockSpec((t, DIL_W), lambda c: (c, col))

    own = pl.BlockSpec((t, DIL_W), lambda c: (c, 0))
    out = jax.ShapeDtypeStruct((s, DIL_W), F32)
    tile, wide = _staging(t), _staging(t + 2 * h)
    dq = pl.pallas_call(
        dq_body, name=name + "_dq", grid=(s // t,),
        in_specs=[zcur(grp)] + _halo_specs(d, 3 + grp, s, t) + _halo_specs(d, 6 + grp, s, t) + [own, own, own],
        out_specs=own, out_shape=out, scratch_shapes=[tile, wide, wide, tile, tile, tile, tile],
        compiler_params=_cparams(("parallel",)),
    )(zb, zb, zb, zb, zb, zb, zb, do, lse, dl)
    own3 = _halo_specs(d, 0, s, t)
    dk, dv = pl.pallas_call(
        dkv_body, name=name + "_dkv", grid=(s // t,),
        in_specs=[zcur(3 + grp), zcur(6 + grp)] + _halo_specs(d, grp, s, t) + own3 + own3 + own3,
        out_specs=[own, own], out_shape=[out, out],
        scratch_shapes=[tile, tile, wide, wide, wide, wide, tile, tile],
        compiler_params=_cparams(("parallel",)),
    )(zb, zb, zb, zb, zb, do, do, do, lse, lse, lse, dl, dl, dl)
    return dq, dk, dv


def _dil_combine(os_, ls_, name):
    def fn(o0, o1, o2, l0, l1, l2):
        m = jnp.maximum(jnp.maximum(l0, l1), l2)
        e0, e1, e2 = jnp.exp(l0 - m), jnp.exp(l1 - m), jnp.exp(l2 - m)
        den = e0 + e1 + e2
        comb = (e0 / den) * o0 + (e1 / den) * o1 + (e2 / den) * o2
        return comb, m + jnp.log(den)
    return _rows_call(fn, list(os_) + list(ls_), [], [(DIL_W, BF16), (DIL_W, F32)], name=name)


def _dil_combine_bwd(dcomb, os_, ls_, lt, seg64, name):
    def fn(dc, o0, o1, o2, l0, l1, l2, ltv, seg):
        w = [jnp.exp(l - ltv) for l in (l0, l1, l2)]
        comb = w[0] * o0 + w[1] * o1 + w[2] * o2
        t = _chunkdot(dc * comb, seg)
        return [wg * dc for wg in w] + [wg * t for wg in w]
    return _rows_call(fn, [dcomb] + list(os_) + list(ls_) + [lt], [seg64],
                      [(DIL_W, BF16)] * 3 + [(DIL_W, F32)] * 3, name=name)


def _mla_prep(za, gq, gkv, cs, swap, name):
    def fn(z, csv, gqv, gkvv, sw):
        return (_rms(z[:, :MLA_Q_RANK], gqv), _rms(z[:, MLA_Q_RANK:640], gkvv), _rope(z[:, 640:], csv, sw))
    return _rows_call(fn, [za, cs], [gq, gkv, swap], [(MLA_Q_RANK, BF16), (MLA_KV_RANK, BF16), (LANES, F32)],
                      name=name)


def _mla_prep_bwd(za, cs, dcq, dckv, dkr, gq, gkv, swap, name):
    def fn(z, csv, dcqv, dckvv, dkrv, gqv, gkvv, sw):
        d1, dg1 = _rms_bwd_math(z[:, :MLA_Q_RANK], gqv, dcqv)
        d2, dg2 = _rms_bwd_math(z[:, MLA_Q_RANK:640], gkvv, dckvv)
        d3 = _rope_t(dkrv, csv, sw)
        return jnp.concatenate([d1, d2, d3], axis=1), dg1, dg2
    return _rows_call(fn, [za, cs, dcq, dckv, dkr], [gq, gkv, swap], [(IN_A_PAD, BF16)],
                      [(1, MLA_Q_RANK), (1, MLA_KV_RANK)], name=name)


def _mla_qk(q_raw, k_pad, krr, cs, swap, name):
    w = MLA_HEADS * MLA_DK

    def fn(qv, kv, krv, csv, sw):
        return _rope(qv, csv, sw) * (MLA_SCALE * LOG2E), kv + _lanes(krv, w)
    return _rows_call(fn, [q_raw, k_pad, krr, cs], [swap], [(w, BF16), (w, BF16)], name=name)


def _mla_qk_bwd(dqh, dkh, cs, swap, name):
    w = MLA_HEADS * MLA_DK

    def fn(dq, dk, csv, sw):
        dk = dk * LN2
        acc = dk[:, :LANES]
        for h in range(1, MLA_HEADS):
            acc = acc + dk[:, h * LANES:(h + 1) * LANES]
        lane = lax.broadcasted_iota(jnp.int32, acc.shape, 1)
        acc = jnp.where((lane >= MLA_NOPE) & (lane < MLA_NOPE + MLA_ROPE), acc, 0.0)
        return _rope_t(dq * MLA_SCALE, csv, sw), dk, acc
    return _rows_call(fn, [dqh, dkh, cs], [swap], [(w, BF16), (w, BF16), (LANES, F32)], name=name)


def _head_norm(t, g2, seg):
    r = lax.rsqrt(_chunkdot(t * t, seg) * (1.0 / GQA_HEAD_DIM) + NORM_EPS)
    return t * r * _lanes(g2, t.shape[1]), r


def _head_norm_bwd(t, g2, seg, dn):
    w = t.shape[1]
    r = lax.rsqrt(_chunkdot(t * t, seg) * (1.0 / GQA_HEAD_DIM) + NORM_EPS)
    u = dn * _lanes(g2, w)
    dt = r * u - t * (r * r * r) * (_chunkdot(u * t, seg) * (1.0 / GQA_HEAD_DIM))
    dgw = jnp.sum(dn * t * r, axis=0, keepdims=True)
    dg = dgw[:, :LANES]
    for c in range(LANES, w, LANES):
        dg = dg + dgw[:, c:c + LANES]
    return dt, dg


def _gqa_prep(q_raw, kv_raw, cs, gq2, gk2, seg, swap, name):
    kw = GQA_KV_HEADS * GQA_HEAD_DIM

    def fn(qv, kvv, csv, gqv, gkv, sg, sw):
        qn, _ = _head_norm(qv, gqv, sg)
        kn, _ = _head_norm(kvv[:, :kw], gkv, sg)
        return _rope(qn, csv, sw) * (GQA_SCALE * LOG2E), _rope(kn, csv, sw), kvv[:, kw:]
    return _rows_call(fn, [q_raw, kv_raw, cs], [gq2, gk2, seg, swap],
                      [(GQA_HEADS * GQA_HEAD_DIM, BF16), (kw, BF16), (kw, BF16)], name=name)


def _gqa_prep_bwd(q_raw, kv_raw, cs, dqh, dkh, dv, gq2, gk2, seg, swap, name):
    kw = GQA_KV_HEADS * GQA_HEAD_DIM

    def fn(qv, kvv, csv, dq, dk, dvv, gqv, gkv, sg, sw):
        dqr, dgq = _head_norm_bwd(qv, gqv, sg, _rope_t(dq * GQA_SCALE, csv, sw))
        dkr, dgk = _head_norm_bwd(kvv[:, :kw], gkv, sg, _rope_t(dk * LN2, csv, sw))
        return dqr, jnp.concatenate([dkr, dvv], axis=1), dgq, dgk
    return _rows_call(fn, [q_raw, kv_raw, cs, dqh, dkh, dv], [gq2, gk2, seg, swap],
                      [(GQA_HEADS * GQA_HEAD_DIM, BF16), (2 * kw, BF16)], [(1, LANES), (1, LANES)], name=name)


def _loss_head(x, target, g, name):
    dm = x.shape[1]

    def fn(xv, tv, gv):
        err = _rms(xv, gv) - tv
        loss = 0.5 * jnp.sum(err * err) / dm
        dx, dg = _rms_bwd_math(xv, gv, err * (1.0 / dm))
        return dx, jnp.zeros((1, LANES), F32) + loss, dg
    return _rows_call(fn, [x, target], [g], [(dm, F32)], [(1, LANES), (1, dm)], name=name)


def _adamw(w, g, m, v, name):
    def fn(wv, gv, mv, vv):
        m2 = ADAM_B1 * mv + (1.0 - ADAM_B1) * gv
        v2 = ADAM_B2 * vv + (1.0 - ADAM_B2) * (gv * gv)
        m_hat = m2 / (1.0 - ADAM_B1 ** ADAM_STEP)
        v_hat = v2 / (1.0 - ADAM_B2 ** ADAM_STEP)
        return -ADAM_LR * (m_hat / (jnp.sqrt(v_hat) + ADAM_EPS) + ADAM_WD * wv), m2, v2
    c = w.shape[1]
    return _rows_call(fn, [w, g, m, v], [], [(c, F32)] * 3, bs=_pick(w.shape[0], 256, 8), name=name)


HBM_SPEC = pl.BlockSpec(memory_space=pltpu.HBM)
VMEM_SPEC = pl.BlockSpec(memory_space=pltpu.VMEM)


def _position():
    return lax.axis_index("x"), lax.axis_index("y"), lax.axis_index("c")


def _other_chips(x, y):
    return [(1 - x, y), (x, 1 - y), (1 - x, 1 - y)]


HALF_W = PACK_W // 2


def _cols(c):
    return pl.ds(pl.multiple_of(c * HALF_W, HALF_W), HALF_W)


def _all_gather_weights(packed):
    rows = packed.shape[0]

    def body(p_ref, g_ref, send_sems, recv_sems, local_sem):
        x, y, c = _position()
        chips = _other_chips(x, y)

        def half(chip, hc):
            return g_ref.at[2 * chip[0] + chip[1], :, _cols(hc)]

        def copy(j, src, dst, to):
            return pltpu.make_async_remote_copy(src_ref=src, dst_ref=dst, send_sem=send_sems.at[j],
                                                recv_sem=recv_sems.at[j], device_id=to, device_id_type=MESH)

        mine = pltpu.make_async_copy(p_ref, g_ref.at[2 * x + y], local_sem)
        mine.start()
        first = [copy(j, p_ref.at[:, _cols(c)], half((x, y), c), (*chip, c)) for j, chip in enumerate(chips)]
        for cp in first:
            cp.start()
        passed = [copy(3 + j, half(chip, c), half(chip, c), (x, y, 1 - c)) for j, chip in enumerate(chips)]
        for j, chip in enumerate(chips):
            copy(j, half(chip, c), half(chip, c), (x, y, c)).wait_recv()
            passed[j].start()
        for j, chip in enumerate(chips):
            copy(3 + j, half(chip, 1 - c), half(chip, 1 - c), (x, y, c)).wait_recv()
        for cp in first + passed:
            cp.wait_send()
        mine.wait()

    return pl.pallas_call(
        body, name="all_gather_weights", in_specs=[HBM_SPEC], out_specs=HBM_SPEC,
        out_shape=jax.ShapeDtypeStruct((4, rows, packed.shape[1]), packed.dtype),
        scratch_shapes=[pltpu.SemaphoreType.DMA((6,)), pltpu.SemaphoreType.DMA((6,)), pltpu.SemaphoreType.DMA],
    )(packed)


SEM_SPEC = pl.BlockSpec(memory_space=pltpu.SEMAPHORE)
ANY_SPEC = pl.BlockSpec(memory_space=pl.ANY)
DATAFLOW = pltpu.SideEffectType.DATAFLOW_SIDE_EFFECTING


def _hbm(a):
    return pltpu.with_memory_space_constraint(a, pltpu.HBM)


def _gather_start(packed, tag):
    rows = packed.shape[0]

    def body(p_ref, g_ref, send_sems, recv_sems, p_thru, g_thru, token):
        x, y, c = _position()
        for j, chip in enumerate(_other_chips(x, y)):
            for s in range(2):
                pltpu.make_async_remote_copy(
                    src_ref=p_ref.at[:, _cols(c)], dst_ref=g_ref.at[2 * x + y, :, _cols(c)],
                    send_sem=send_sems.at[2 * j + s], recv_sem=recv_sems.at[2 * j + s],
                    device_id=(*chip, 1 - c if s else c), device_id_type=MESH).start()
        token[...] = jnp.zeros_like(token)

    return pl.pallas_call(
        body, name=tag + "_start",
        out_shape=(pltpu.SemaphoreType.DMA((6,)), pltpu.SemaphoreType.DMA((6,)), pltpu.HBM(packed.shape, packed.dtype),
                   pltpu.HBM((4, rows, PACK_W), packed.dtype), jax.ShapeDtypeStruct((8, LANES), F32)),
        in_specs=(HBM_SPEC, HBM_SPEC), out_specs=(SEM_SPEC, SEM_SPEC, HBM_SPEC, HBM_SPEC, VMEM_SPEC),
        input_output_aliases={0: 2, 1: 3},
        compiler_params=pltpu.CompilerParams(has_side_effects=DATAFLOW),
    )(_hbm(packed), _hbm(lax.empty((4, rows, PACK_W), packed.dtype)))


def _gather_wait(send_sems, recv_sems, p_thru, g_thru, after, tag):
    def body(p_ref, g_ref, send_sems, recv_sems, after_ref, p_dead, got_ref):
        x, y, c = _position()
        for j, chip in enumerate(_other_chips(x, y)):
            for s in range(2):
                cp = pltpu.make_async_remote_copy(
                    src_ref=p_ref.at[:, _cols(c)], dst_ref=g_ref.at[2 * chip[0] + chip[1], :, _cols(1 - c if s else c)],
                    send_sem=send_sems.at[2 * j + s], recv_sem=recv_sems.at[2 * j + s],
                    device_id=(x, y, c), device_id_type=MESH)
                cp.wait_send()
                cp.wait_recv()

    return pl.pallas_call(
        body, name=tag + "_wait",
        out_shape=(pltpu.HBM(p_thru.shape, p_thru.dtype), pltpu.HBM(g_thru.shape, g_thru.dtype)),
        in_specs=(HBM_SPEC, HBM_SPEC, SEM_SPEC, SEM_SPEC, ANY_SPEC), out_specs=(HBM_SPEC, HBM_SPEC),
        input_output_aliases={0: 0, 1: 1},
        compiler_params=pltpu.CompilerParams(has_side_effects=DATAFLOW),
    )(p_thru, g_thru, send_sems, recv_sems, after)[1]


def _sibling_swap_halves(grads, tag):
    rows = grads.shape[1]

    def body(g_ref, a_ref, send_sem, recv_sem):
        x, y, c = _position()
        cp = pltpu.make_async_remote_copy(src_ref=g_ref.at[:, :, _cols(1 - c)], dst_ref=a_ref,
                                          send_sem=send_sem, recv_sem=recv_sem, device_id=(x, y, 1 - c),
                                          device_id_type=MESH)
        cp.start()
        cp.wait()

    return pl.pallas_call(
        body, name=tag + "_swap_cores", in_specs=[HBM_SPEC], out_specs=HBM_SPEC,
        out_shape=jax.ShapeDtypeStruct((4, rows, HALF_W), grads.dtype),
        scratch_shapes=[pltpu.SemaphoreType.DMA, pltpu.SemaphoreType.DMA],
    )(grads)


def _rs_block(rows):
    return max(d for d in range(16, 1601, 16) if rows % d == 0)


def _chip_sum(grads, other, c, tag):
    rows = other.shape[1]
    rb = _rs_block(rows)

    def body(c_ref, g_ref, a_ref, o_ref):
        o_ref[...] = (g_ref[...] + a_ref[...]).astype(o_ref.dtype)

    blk = (1, rb, HALF_W)
    return pl.pallas_call(
        body, name=tag + "_chip_sum",
        grid_spec=pltpu.PrefetchScalarGridSpec(
            num_scalar_prefetch=1, grid=(4, rows // rb),
            in_specs=[pl.BlockSpec(blk, lambda k, i, c_ref: (k, i, c_ref[0])),
                      pl.BlockSpec(blk, lambda k, i, c_ref: (k, i, 0))],
            out_specs=pl.BlockSpec(blk, lambda k, i, c_ref: (k, i, 0))),
        out_shape=jax.ShapeDtypeStruct(other.shape, BF16),
        compiler_params=_cparams(("parallel", "parallel")),
    )(jnp.reshape(c, (1,)).astype(jnp.int32), grads, other)


def _chip_copies(t_ref, b_ref, send_sems, recv_sems):
    x, y, c = _position()
    return [pltpu.make_async_remote_copy(src_ref=t_ref.at[2 * chip[0] + chip[1]], dst_ref=b_ref.at[j],
                                         send_sem=send_sems.at[j], recv_sem=recv_sems.at[j],
                                         device_id=(*chip, c), device_id_type=MESH)
            for j, chip in enumerate(_other_chips(x, y))]


def _send_chip_sums(sums, tag):
    def body(t_ref, b_ref, send_sems, recv_sems):
        copies = _chip_copies(t_ref, b_ref, send_sems, recv_sems)
        for cp in copies:
            cp.start()
        for cp in copies:
            cp.wait()

    return pl.pallas_call(
        body, name=tag + "_send_chips", in_specs=[HBM_SPEC], out_specs=HBM_SPEC,
        out_shape=jax.ShapeDtypeStruct((3,) + sums.shape[1:], sums.dtype),
        scratch_shapes=[pltpu.SemaphoreType.DMA((3,)), pltpu.SemaphoreType.DMA((3,))],
    )(sums)


def _send_chip_sums_start(sums, tag):
    land = (3,) + sums.shape[1:]

    def body(t_ref, b_ref, send_sems, recv_sems, t_thru, b_thru, token):
        for cp in _chip_copies(t_ref, b_ref, send_sems, recv_sems):
            cp.start()
        token[...] = jnp.zeros_like(token)

    return pl.pallas_call(
        body, name=tag + "_send_chips_start",
        out_shape=(pltpu.SemaphoreType.DMA((3,)), pltpu.SemaphoreType.DMA((3,)), pltpu.HBM(sums.shape, sums.dtype),
                   pltpu.HBM(land, sums.dtype), jax.ShapeDtypeStruct((8, LANES), F32)),
        in_specs=(HBM_SPEC, HBM_SPEC), out_specs=(SEM_SPEC, SEM_SPEC, HBM_SPEC, HBM_SPEC, VMEM_SPEC),
        input_output_aliases={0: 2, 1: 3},
        compiler_params=pltpu.CompilerParams(has_side_effects=DATAFLOW),
    )(_hbm(sums), _hbm(lax.empty(land, sums.dtype)))


def _send_chip_sums_wait(send_sems, recv_sems, t_thru, b_thru, after, tag):
    def body(t_ref, b_ref, send_sems, recv_sems, after_ref, t_dead, got_ref):
        for cp in _chip_copies(t_ref, b_ref, send_sems, recv_sems):
            cp.wait_send()
            cp.wait_recv()

    return pl.pallas_call(
        body, name=tag + "_send_chips_wait",
        out_shape=(pltpu.HBM(t_thru.shape, t_thru.dtype), pltpu.HBM(b_thru.shape, b_thru.dtype)),
        in_specs=(HBM_SPEC, HBM_SPEC, SEM_SPEC, SEM_SPEC, ANY_SPEC), out_specs=(HBM_SPEC, HBM_SPEC),
        input_output_aliases={0: 0, 1: 1},
        compiler_params=pltpu.CompilerParams(has_side_effects=DATAFLOW),
    )(t_thru, b_thru, send_sems, recv_sems, after)[1]


def _final_sum(grads, other, recv, k, c, tag):
    rows = other.shape[1]
    rb = _rs_block(rows)

    def body(k_ref, c_ref, g_ref, a_ref, b_ref, o_ref):
        own = g_ref[0] + a_ref[0]
        o_ref[...] = ((own + b_ref[0].astype(F32)) + b_ref[1].astype(F32)) + b_ref[2].astype(F32)

    return pl.pallas_call(
        body, name=tag + "_final_sum",
        grid_spec=pltpu.PrefetchScalarGridSpec(
            num_scalar_prefetch=2, grid=(rows // rb,),
            in_specs=[pl.BlockSpec((1, rb, HALF_W), lambda i, k_ref, c_ref: (k_ref[0], i, c_ref[0])),
                      pl.BlockSpec((1, rb, HALF_W), lambda i, k_ref, c_ref: (k_ref[0], i, 0)),
                      pl.BlockSpec((3, rb, HALF_W), lambda i, k_ref, c_ref: (0, i, 0))],
            out_specs=pl.BlockSpec((rb, HALF_W), lambda i, k_ref, c_ref: (i, 0))),
        out_shape=jax.ShapeDtypeStruct((rows, HALF_W), F32),
        compiler_params=_cparams(("parallel",)),
    )(jnp.reshape(k, (1,)).astype(jnp.int32), jnp.reshape(c, (1,)).astype(jnp.int32), grads, other, recv)


def _join_halves(half, core, tag):
    def body(h_ref, o_ref, send_sem, recv_sem):
        x, y, c = _position()
        cp = pltpu.make_async_remote_copy(src_ref=h_ref, dst_ref=o_ref, send_sem=send_sem, recv_sem=recv_sem,
                                          device_id=(x, y, 1 - c), device_id_type=MESH)
        cp.start()
        cp.wait()

    other = pl.pallas_call(
        body, name=tag + "_join_cores", in_specs=[HBM_SPEC], out_specs=HBM_SPEC,
        out_shape=jax.ShapeDtypeStruct(half.shape, half.dtype),
        scratch_shapes=[pltpu.SemaphoreType.DMA, pltpu.SemaphoreType.DMA],
    )(half)
    first = core == 0
    return jnp.concatenate([jnp.where(first, half, other), jnp.where(first, other, half)], axis=1)


def _all_reduce_packet(packet):
    rows = packet.shape[0]

    def body(p_ref, o_ref, buf, send_sems, recv_sems):
        x, y, c = _position()
        me = 4 * x + 2 * y + c
        buf[me] = p_ref[...]

        def flip(v, bit):
            return 1 - v if bit else v

        for p in range(1, 8):
            peer = (flip(x, p & 4), flip(y, p & 2), flip(c, p & 1))
            pltpu.make_async_remote_copy(src_ref=p_ref, dst_ref=buf.at[me], send_sem=send_sems.at[p - 1],
                                         recv_sem=recv_sems.at[p - 1], device_id=peer, device_id_type=MESH).start()
        for p in range(1, 8):
            peer = (flip(x, p & 4), flip(y, p & 2), flip(c, p & 1))
            slot = 4 * peer[0] + 2 * peer[1] + peer[2]
            cp = pltpu.make_async_remote_copy(src_ref=p_ref, dst_ref=buf.at[slot], send_sem=send_sems.at[p - 1],
                                              recv_sem=recv_sems.at[p - 1], device_id=peer, device_id_type=MESH)
            cp.wait_recv()
            cp.wait_send()
        acc = buf[0]
        for dev in range(1, 8):
            acc = acc + buf[dev]
        o_ref[...] = acc

    return pl.pallas_call(
        body, name="all_reduce_packet", in_specs=[VMEM_SPEC], out_specs=VMEM_SPEC,
        out_shape=jax.ShapeDtypeStruct(packet.shape, F32),
        scratch_shapes=[pltpu.VMEM((8, rows, LANES), F32), pltpu.SemaphoreType.DMA((7,)),
                        pltpu.SemaphoreType.DMA((7,))],
    )(packet)


def _stack_range(name, n_stack, pack):
    if name.startswith('gqa'):
        return (0, 0) if pack == 0 else (0, n_stack)
    return (0, 1) if pack == 0 else (1, n_stack)


def _pack_members(pack):
    out = []
    for n, shape, ax in BIG:
        lo, hi = _stack_range(n, shape[0], pack)
        if hi > lo:
            out.append((n, (hi - lo,) + shape[1:], ax, (lo, hi)))
    return out


PACK_ROW_MULTIPLE = 512


def _pad_rows(parts, dtype):
    rows = sum(p.shape[0] for p in parts)
    pad = -rows % PACK_ROW_MULTIPLE
    return jnp.concatenate(parts + ([jnp.zeros((pad, PACK_W), dtype)] if pad else []), axis=0)


def _pack_blocks(blocks, dtype, pack):
    return _pad_rows([blocks[n][lo:hi].astype(dtype).reshape(-1, PACK_W)
                      for n, _, _, (lo, hi) in _pack_members(pack)], dtype)


def _unpack_blocks(packed, pack):
    out, off = {}, 0
    for n, shape, _, _ in _pack_members(pack):
        r = math.prod(shape) // PACK_W
        out[n] = packed[off:off + r].reshape(shape)
        off += r
    return out


def _unpack_gathered(gathered, pack, own=None, chip=None):
    blocks = [gathered[k] if own is None else jnp.where(chip == k, own, gathered[k]) for k in range(4)]
    per_chip = [_unpack_blocks(blocks[k], pack) for k in range(4)]
    return {n: jnp.concatenate([per_chip[k][n] for k in range(4)], axis=ax) for n, _, ax, _ in _pack_members(pack)}


def _pack_full(full, dtype, pack):
    chips = []
    for k in range(4):
        parts = []
        for n, shape, ax, _ in _pack_members(pack):
            blk = lax.slice_in_dim(full[n], k * shape[ax], (k + 1) * shape[ax], axis=ax)
            parts.append(blk.astype(dtype).reshape(-1, PACK_W))
        chips.append(_pad_rows(parts, dtype))
    return jnp.stack(chips, axis=0)


def _pack_small(vals, loss_row):
    rows = [loss_row.reshape(1, LANES)]
    for n, shape in SMALL:
        v = vals.get(n)
        v = jnp.zeros(shape, F32) if v is None else v
        rows.append(v.astype(F32).reshape(-1, LANES))
    packet = jnp.concatenate(rows, axis=0)
    return jnp.pad(packet, ((0, PACKET_ROWS - packet.shape[0]), (0, 0)))


def _unpack_small(packet):
    out, off = {}, 1
    for n, shape in SMALL:
        r = math.prod(shape) // LANES
        out[n] = packet[off:off + r].reshape(shape)
        off += r
    return packet[0, 0], out


_MLA = dict(R=1, dk=MLA_DK, dv=MLA_V, hb=2, bq=512, bk=512)
_MLA_FWD_BQ = 1024
_BWD_UNROLL = 8
_GQA = dict(R=GQA_HEADS // GQA_KV_HEADS, dk=GQA_HEAD_DIM, dv=GQA_HEAD_DIM, hb=2, bq=256, bk=512)


def _layer_params(layer, full, gains):
    pack = 0 if layer == 0 else 1
    i = layer // 2

    def mat(name):
        lo, _ = _stack_range(name, 4 if name.startswith('ffn') else 2, pack)
        return full[name][(layer if name.startswith('ffn') else i) - lo]

    p = dict(ffn_norm=gains['ffn_norm'][layer][None], ffn_w_in=mat('ffn_w_in'), ffn_w_out=mat('ffn_w_out'))
    if layer % 2 == 0:
        w_in = mat('w_in_ab')
        zeros = jnp.zeros((D_MODEL, 32), w_in.dtype)
        p['w_a'] = jnp.concatenate([w_in[:, :640], zeros, zeros, w_in[:, 640:IN_A], zeros], axis=1)
        p['w_b'] = w_in[:, IN_A:]
        p['w_uq'] = jnp.pad(mat('mla_w_uq'), ((0, 0), (0, 0), (0, MLA_DK - 96))).reshape(MLA_Q_RANK, -1)
        ukv = mat('mla_w_ukv')
        p['w_uk'] = jnp.pad(ukv[:, :, :MLA_NOPE], ((0, 0), (0, 0), (0, MLA_DK - MLA_NOPE))).reshape(MLA_KV_RANK, -1)
        p['w_uv'] = ukv[:, :, MLA_NOPE:].reshape(MLA_KV_RANK, -1)
        p['w_out'] = mat('w_out_ab')
        p['mix_norm'] = gains['mix_norm_ab'][i][None]
        p['q_norm'] = gains['mla_q_norm'][i][None]
        p['kv_norm'] = gains['mla_kv_norm'][i][None]
    else:
        p['w_q'], p['w_kv'], p['w_o'] = mat('gqa_w_q'), mat('gqa_w_kv'), mat('gqa_w_o')
        p['mix_norm'] = gains['mix_norm_c'][i][None]
        p['q_norm'] = jnp.tile(gains['gqa_q_norm'][i][None], (1, 2))
        p['k_norm'] = jnp.tile(gains['gqa_k_norm'][i][None], (1, 2))
    return p


def _even_fwd(x, p, cs, swap, tag):
    xn = _rms_fwd(x, p['mix_norm'], tag + "_norm")
    za = _mm(xn, p['w_a'], name=tag + "_in_a")
    zb = _mm(xn, p['w_b'], out_dtype=BF16, name=tag + "_in_b")
    cq, ckv, krr = _mla_prep(za, p['q_norm'], p['kv_norm'], cs, swap, tag + "_mla_prep")
    q_raw = _mm(cq, p['w_uq'], name=tag + "_uq")
    k_pad = _mm(ckv, p['w_uk'], name=tag + "_uk")
    v = _mm(ckv, p['w_uv'], out_dtype=BF16, name=tag + "_uv")
    qh, kh = _mla_qk(q_raw, k_pad, krr, cs, swap, tag + "_mla_qk")
    o_a, lse_a = _flash_fwd(qh, kh, v, name=tag + "_mla_attn", **dict(_MLA, bq=_MLA_FWD_BQ))
    og, lg = [], []
    for grp in range(DIL_GROUPS):
        o, l = _dil_fwd(zb, grp, f"{tag}_dil{grp}")
        og.append(o)
        lg.append(l)
    o_b, lt = _dil_combine(og, lg, tag + "_dil_merge")
    ocat = jnp.concatenate([o_a, o_b], axis=1)
    x1 = _mm(ocat, p['w_out'], add=x, name=tag + "_out")
    saved = dict(x=x, xn=xn, za=za, zb=zb, cq=cq, ckv=ckv, qh=qh, kh=kh, v=v, lse_a=lse_a, og=og, lg=lg, lt=lt,
                 ocat=ocat)
    return x1, saved


def _even_bwd(dx1, p, sv, cs, swap, seg64, tag):
    docat = _mm(dx1, p['w_out'], mode="nt", name=tag + "_out_dx")
    d_w_out = _mm(sv['ocat'], dx1, mode="tn", name=tag + "_out_dw")
    n_a = MLA_HEADS * MLA_V
    do_a = docat[:, :n_a].astype(BF16)
    res = _dil_combine_bwd(docat[:, n_a:], sv['og'], sv['lg'], sv['lt'], seg64, tag + "_dil_merge_bwd")
    dqs, dks, dvs = [], [], []
    for grp in range(DIL_GROUPS):
        dq, dk, dv = _dil_bwd(sv['zb'], res[grp], sv['lg'][grp], res[3 + grp], grp, f"{tag}_dil{grp}_bwd")
        dqs.append(dq)
        dks.append(dk)
        dvs.append(dv)
    dzb = jnp.concatenate(dqs + dks + dvs, axis=1).astype(BF16)
    dqh, dkh, dv = _flash_bwd(sv['qh'], sv['kh'], sv['v'], sv['ocat'][:, :n_a], do_a, sv['lse_a'],
                              name=tag + "_mla_attn_bwd", unroll=_BWD_UNROLL, **_MLA)
    dq_raw, dkh, dkrr = _mla_qk_bwd(dqh, dkh, cs, swap, tag + "_mla_qk_bwd")
    dcq = _mm(dq_raw, p['w_uq'], mode="nt", name=tag + "_uq_dx")
    d_w_uq = _mm(sv['cq'], dq_raw, mode="tn", name=tag + "_uq_dw")
    dckv = _mm(dkh, p['w_uk'], mode="nt", name=tag + "_uk_dx")
    dckv = _mm(dv, p['w_uv'], mode="nt", add=dckv, name=tag + "_uv_dx")
    d_w_uk = _mm(sv['ckv'], dkh, mode="tn", name=tag + "_uk_dw")
    d_w_uv = _mm(sv['ckv'], dv, mode="tn", name=tag + "_uv_dw")
    dza, d_gq, d_gkv = _mla_prep_bwd(sv['za'], cs, dcq, dckv, dkrr, p['q_norm'], p['kv_norm'], swap,
                                     tag + "_mla_prep_bwd")
    dxn = _mm(dza, p['w_a'], mode="nt", name=tag + "_in_a_dx")
    dxn = _mm(dzb, p['w_b'], mode="nt", add=dxn, name=tag + "_in_b_dx")
    d_w_a = _mm(sv['xn'], dza, mode="tn", name=tag + "_in_a_dw")
    d_w_b = _mm(sv['xn'], dzb, mode="tn", name=tag + "_in_b_dw")
    dx, d_g = _rms_bwd(sv['x'], p['mix_norm'], dxn, dx1, tag + "_norm_bwd")
    d_w_in = jnp.concatenate([d_w_a[:, :640], d_w_a[:, 704:736], d_w_b], axis=1)
    d_uq = d_w_uq.reshape(MLA_Q_RANK, MLA_HEADS, MLA_DK)[:, :, :MLA_NOPE + MLA_ROPE]
    d_ukv = jnp.concatenate([d_w_uk.reshape(MLA_KV_RANK, MLA_HEADS, MLA_DK)[:, :, :MLA_NOPE],
                             d_w_uv.reshape(MLA_KV_RANK, MLA_HEADS, MLA_V)], axis=2)
    grads = dict(w_in_ab=d_w_in, mla_w_uq=d_uq, mla_w_ukv=d_ukv, w_out_ab=d_w_out, mix_norm_ab=d_g[0],
                 mla_q_norm=d_gq[0], mla_kv_norm=d_gkv[0])
    return dx, grads


def _odd_fwd(x, p, cs, seg64, swap, tag):
    xn = _rms_fwd(x, p['mix_norm'], tag + "_norm")
    q_raw = _mm(xn, p['w_q'], name=tag + "_q")
    kv_raw = _mm(xn, p['w_kv'], name=tag + "_kv")
    qh, kh, v = _gqa_prep(q_raw, kv_raw, cs, p['q_norm'], p['k_norm'], seg64, swap, tag + "_gqa_prep")
    o, lse = _flash_fwd(qh, kh, v, name=tag + "_gqa_attn", **_GQA)
    x1 = _mm(o, p['w_o'], add=x, name=tag + "_o")
    return x1, dict(x=x, xn=xn, q_raw=q_raw, kv_raw=kv_raw, qh=qh, kh=kh, v=v, o=o, lse=lse)


def _odd_bwd(dx1, p, sv, cs, seg64, swap, tag):
    do = _mm(dx1, p['w_o'], mode="nt", out_dtype=BF16, name=tag + "_o_dx")
    d_w_o = _mm(sv['o'], dx1, mode="tn", name=tag + "_o_dw")
    dqh, dkh, dv = _flash_bwd(sv['qh'], sv['kh'], sv['v'], sv['o'], do, sv['lse'], name=tag + "_gqa_attn_bwd",
                              unroll=_BWD_UNROLL, **_GQA)
    dq_raw, dkv_raw, d_gq, d_gk = _gqa_prep_bwd(sv['q_raw'], sv['kv_raw'], cs, dqh, dkh, dv, p['q_norm'],
                                                p['k_norm'], seg64, swap, tag + "_gqa_prep_bwd")
    dxn = _mm(dq_raw, p['w_q'], mode="nt", name=tag + "_q_dx")
    dxn = _mm(dkv_raw, p['w_kv'], mode="nt", add=dxn, name=tag + "_kv_dx")
    d_w_q = _mm(sv['xn'], dq_raw, mode="tn", name=tag + "_q_dw")
    d_w_kv = _mm(sv['xn'], dkv_raw, mode="tn", name=tag + "_kv_dw")
    dx, d_g = _rms_bwd(sv['x'], p['mix_norm'], dxn, dx1, tag + "_norm_bwd")
    grads = dict(gqa_w_q=d_w_q, gqa_w_kv=d_w_kv, gqa_w_o=d_w_o, mix_norm_c=d_g[0],
                 gqa_q_norm=d_gq[0, :GQA_HEAD_DIM] + d_gq[0, GQA_HEAD_DIM:],
                 gqa_k_norm=d_gk[0, :GQA_HEAD_DIM] + d_gk[0, GQA_HEAD_DIM:])
    return dx, grads


def _ffn_fwd(x, p, tag):
    xn = _rms_fwd(x, p['ffn_norm'], tag + "_ffn_norm")
    h = _mm(xn, p['ffn_w_in'], out_dtype=BF16, name=tag + "_ffn_in")
    x2 = _mm(h, p['ffn_w_out'], gated=True, add=x, name=tag + "_ffn_out")
    return x2, dict(x=x, xn=xn, h=h)


def _ffn_bwd(dx2, p, sv, tag):
    d_gate, d_up = _mm(dx2, p['ffn_w_out'], mode="nt", gate_up=sv['h'], out_dtype=BF16, name=tag + "_ffn_out_dx")
    d_w_out = _mm(sv['h'], dx2, mode="tn", gated=True, name=tag + "_ffn_out_dw")
    d_w_in = jnp.concatenate([_mm(sv['xn'], d_gate, mode="tn", name=tag + "_ffn_in_dw_gate"),
                              _mm(sv['xn'], d_up, mode="tn", name=tag + "_ffn_in_dw_up")], axis=1)
    dxn = _mm(d_gate, p['ffn_w_in'], mode="nt", b_part=(0, 2), name=tag + "_ffn_in_dx_gate")
    dxn = _mm(d_up, p['ffn_w_in'], mode="nt", b_part=(1, 2), add=dxn, name=tag + "_ffn_in_dx_up")
    dx, d_g = _rms_bwd(sv['x'], p['ffn_norm'], dxn, dx2, tag + "_ffn_norm_bwd")
    return dx, d_w_in, d_w_out, d_g[0]


EVEN_MATS = ('w_in_ab', 'mla_w_uq', 'mla_w_ukv', 'w_out_ab')
ODD_MATS = ('gqa_w_q', 'gqa_w_kv', 'gqa_w_o')
FFN_MATS = ('ffn_w_in', 'ffn_w_out')


def _schedule(x, target, gains, full_of_pack, rest_grads_ready):
    s = x.shape[0]
    cs_mla, cs_gqa = _rope_tables(s)
    swap, seg64 = _swap_matrix(), _seg_matrix(GQA_HEAD_DIM)
    params, saved, full = [], [], None
    for layer in range(4):
        tag = f"l{layer}"
        if layer < 2:
            full = full_of_pack(layer, x)
        p = _layer_params(layer, full, gains)
        if layer % 2 == 0:
            x, sv = _even_fwd(x, p, cs_mla, swap, tag)
        else:
            x, sv = _odd_fwd(x, p, cs_gqa, seg64, swap, tag)
        x, sv_f = _ffn_fwd(x, p, tag)
        params.append(p)
        saved.append((sv, sv_f))
    dx, loss_row, d_final = _loss_head(x, target, gains['final_norm'][None], "loss_head")

    per_layer, rest = {}, None
    for layer in reversed(range(4)):
        p, (sv, sv_f), tag = params[layer], saved[layer], f"l{layer}"
        if layer == 0:
            rest = {n: per_layer[2][n][None] for n in EVEN_MATS}
            rest.update({n: jnp.stack([per_layer[1][n], per_layer[3][n]], axis=0) for n in ODD_MATS})
            rest.update({n: jnp.stack([per_layer[l][n] for l in (1, 2, 3)], axis=0) for n in FFN_MATS})
            token = rest_grads_ready(rest)
            if token is not None:
                p = dict(p, ffn_w_out=p['ffn_w_out'] + token[0, 0].astype(p['ffn_w_out'].dtype))
        dx, d_ffn_in, d_ffn_out, d_ffn_g = _ffn_bwd(dx, p, sv_f, tag)
        if layer % 2 == 0:
            dx, g = _even_bwd(dx, p, sv, cs_mla, swap, seg64, tag)
        else:
            dx, g = _odd_bwd(dx, p, sv, cs_gqa, seg64, swap, tag)
        g.update(ffn_w_in=d_ffn_in, ffn_w_out=d_ffn_out, ffn_norm=d_ffn_g)
        per_layer[layer] = g

    first = {n: per_layer[0][n][None] for n in EVEN_MATS + FFN_MATS}
    small = {'final_norm': d_final[0], 'ffn_norm': jnp.stack([per_layer[l]['ffn_norm'] for l in range(4)], axis=0)}
    for n in ('mix_norm_ab', 'mla_q_norm', 'mla_kv_norm'):
        small[n] = jnp.stack([per_layer[0][n], per_layer[2][n]], axis=0)
    for n in ('mix_norm_c', 'gqa_q_norm', 'gqa_k_norm'):
        small[n] = jnp.stack([per_layer[1][n], per_layer[3][n]], axis=0)
    return loss_row, dx, first, rest, small


def _core_sums(grads, pack, core, tag):
    packed = _pack_full(grads, F32, pack)
    other = _sibling_swap_halves(packed, tag)
    return packed, other, _chip_sum(packed, other, core, tag)


def _finish_reduce_scatter(packed, other, recv, pack, chip, core, tag):
    return _unpack_blocks(_join_halves(_final_sum(packed, other, recv, chip, core, tag), core, tag), pack)


def _step(x, target, w, m, v):
    big_names = [n for n, _, _ in BIG]
    chip = 2 * lax.axis_index("x") + lax.axis_index("y")
    core = lax.axis_index("c")

    gains = {n: w[n] for n, _ in SMALL if n != 'mix_norm_c'}
    c_cols = w['mix_norm_c'].shape[1]
    own_c = lax.dynamic_update_slice(jnp.zeros((2, 4 * c_cols), F32), w['mix_norm_c'], (0, chip * c_cols))
    gains['mix_norm_c'] = _unpack_small(_all_reduce_packet(_pack_small(
        {'mix_norm_c': own_c * 0.5}, jnp.zeros((LANES,), F32))))[1]['mix_norm_c']

    gathered0 = _all_gather_weights(_pack_blocks(w, BF16, 0))
    packed1, gathered0 = lax.optimization_barrier((_pack_blocks(w, BF16, 1), gathered0))
    ag_send, ag_recv, p_thru, g_thru, ag_token = _gather_start(packed1, "gather_rest")
    gains['mix_norm_ab'] = gains['mix_norm_ab'] + ag_token[0, 0]
    full0 = _unpack_gathered(gathered0, 0)

    def full_of_pack(pack, after):
        if pack == 0:
            return full0
        landed = _gather_wait(ag_send, ag_recv, p_thru, g_thru, after, "gather_rest")
        return _unpack_gathered(landed, 1, own=packed1, chip=chip)

    rs = {}

    def rest_grads_ready(rest):
        rs['packed'], rs['other'], sums = _core_sums(rest, 1, core, "grad_rest")
        rs['send'], rs['recv'], rs['t'], rs['b'], token = _send_chip_sums_start(sums, "grad_rest")
        return token

    loss_row, dx, first, rest, small = _schedule(x[0], target[0], gains, full_of_pack, rest_grads_ready)
    recv1 = _send_chip_sums_wait(rs['send'], rs['recv'], rs['t'], rs['b'], dx, "grad_rest")
    g_rest = _finish_reduce_scatter(rs['packed'], rs['other'], recv1, 1, chip, core, "grad_rest")
    packed0, other0, sums0 = _core_sums(first, 0, core, "grad_first")
    g_first = _finish_reduce_scatter(packed0, other0, _send_chip_sums(sums0, "grad_first"), 0, chip, core,
                                     "grad_first")
    g_blocks = {n: (jnp.concatenate([g_first[n], g_rest[n]], axis=0) if n in g_first else g_rest[n])
                for n in big_names}

    loss, g_small = _unpack_small(_all_reduce_packet(_pack_small(small, loss_row[0])))
    g_small['mix_norm_c'] = lax.dynamic_slice(g_small['mix_norm_c'], (0, chip * c_cols), (2, c_cols))

    out_g, out_d, out_m, out_v = {}, {}, {}, {}
    for n in big_names:
        shape = w[n].shape
        cols = shape[-1]
        d_, m_, v_ = _adamw(w[n].reshape(-1, cols), g_blocks[n].reshape(-1, cols), m[n].reshape(-1, cols),
                            v[n].reshape(-1, cols), "adamw_" + n)
        out_g[n], out_d[n], out_m[n], out_v[n] = g_blocks[n], d_.reshape(shape), m_.reshape(shape), v_.reshape(shape)
    for n, _ in SMALL:
        shape = w[n].shape
        as2d = (lambda t: t.reshape(1, -1)) if len(shape) == 1 else (lambda t: t)
        d_, m_, v_ = _adamw(as2d(w[n]), as2d(g_small[n]), as2d(m[n]), as2d(v[n]), "adamw_" + n)
        out_g[n], out_d[n], out_m[n], out_v[n] = g_small[n], d_.reshape(shape), m_.reshape(shape), v_.reshape(shape)
    return (loss, dx[None], *[out_g[n] for n in WEIGHTS], *[out_d[n] for n in WEIGHTS],
            *[out_m[n] for n in WEIGHTS], *[out_v[n] for n in WEIGHTS])


def kernel(x, mix_norm_ab, w_in_ab, mla_q_norm, mla_kv_norm, mla_w_uq, mla_w_ukv, w_out_ab, mix_norm_c, gqa_w_q, gqa_w_kv, gqa_q_norm, gqa_k_norm, gqa_w_o, ffn_norm, ffn_w_in, ffn_w_out, final_norm, loss_target, m_mix_norm_ab, m_w_in_ab, m_mla_q_norm, m_mla_kv_norm, m_mla_w_uq, m_mla_w_ukv, m_w_out_ab, m_mix_norm_c, m_gqa_w_q, m_gqa_w_kv, m_gqa_q_norm, m_gqa_k_norm, m_gqa_w_o, m_ffn_norm, m_ffn_w_in, m_ffn_w_out, m_final_norm, v_mix_norm_ab, v_w_in_ab, v_mla_q_norm, v_mla_kv_norm, v_mla_w_uq, v_mla_w_ukv, v_w_out_ab, v_mix_norm_c, v_gqa_w_q, v_gqa_w_kv, v_gqa_q_norm, v_gqa_k_norm, v_gqa_w_o, v_ffn_norm, v_ffn_w_in, v_ffn_w_out, v_final_norm):
    w = dict(zip(WEIGHTS, (mix_norm_ab, w_in_ab, mla_q_norm, mla_kv_norm, mla_w_uq, mla_w_ukv, w_out_ab, mix_norm_c,
                           gqa_w_q, gqa_w_kv, gqa_q_norm, gqa_k_norm, gqa_w_o, ffn_norm, ffn_w_in, ffn_w_out,
                           final_norm)))
    m = dict(zip(WEIGHTS, (m_mix_norm_ab, m_w_in_ab, m_mla_q_norm, m_mla_kv_norm, m_mla_w_uq, m_mla_w_ukv,
                           m_w_out_ab, m_mix_norm_c, m_gqa_w_q, m_gqa_w_kv, m_gqa_q_norm, m_gqa_k_norm, m_gqa_w_o,
                           m_ffn_norm, m_ffn_w_in, m_ffn_w_out, m_final_norm)))
    v = dict(zip(WEIGHTS, (v_mix_norm_ab, v_w_in_ab, v_mla_q_norm, v_mla_kv_norm, v_mla_w_uq, v_mla_w_ukv,
                           v_w_out_ab, v_mix_norm_c, v_gqa_w_q, v_gqa_w_kv, v_gqa_q_norm, v_gqa_k_norm, v_gqa_w_o,
                           v_ffn_norm, v_ffn_w_in, v_ffn_w_out, v_final_norm)))
    return _step(x, loss_target, w, m, v)
```

```python
import math

import numpy as np
import jax
import jax.numpy as jnp
from jax import lax
from jax.experimental import pallas as pl
from jax.experimental.pallas import tpu as pltpu

F32 = jnp.float32
BF16 = jnp.bfloat16
MESH = pl.DeviceIdType.MESH

VMEM_LIMIT_BYTES = 56 * 1024 * 1024
LANES = 128

D_MODEL = 1024
NORM_EPS = 1e-6
ROPE_THETA = 10000.0
NEG_INF = -1e30
GRID_W = 64

MLA_HEADS, MLA_Q_RANK, MLA_KV_RANK, MLA_NOPE, MLA_ROPE, MLA_V = 8, 384, 256, 64, 32, 64
MLA_DK = 128
DIL_PAIRS = ((128, 1), (512, 4), (2048, 16))
DIL_HALF, DIL_SLOTS, DIL_GROUPS, DIL_HEAD_DIM = 64, 4, 3, 64
DIL_HEADS = DIL_SLOTS * DIL_GROUPS
DIL_W = DIL_SLOTS * DIL_HEAD_DIM
GQA_HEADS, GQA_KV_HEADS, GQA_HEAD_DIM = 16, 4, 64
FFN_HIDDEN = 2816
IN_A = MLA_Q_RANK + MLA_KV_RANK + MLA_ROPE
IN_A_PAD = 768
IN_B = 3 * DIL_HEADS * DIL_HEAD_DIM

ADAM_LR, ADAM_B1, ADAM_B2, ADAM_EPS, ADAM_WD, ADAM_STEP = 0.001, 0.9, 0.999, 1e-08, 0.01, 10

LOG2E, LN2 = math.log2(math.e), math.log(2.0)
MLA_SCALE = (MLA_NOPE + MLA_ROPE) ** -0.5
GQA_SCALE = GQA_HEAD_DIM ** -0.5

WEIGHTS = ['mix_norm_ab', 'w_in_ab', 'mla_q_norm', 'mla_kv_norm', 'mla_w_uq', 'mla_w_ukv', 'w_out_ab', 'mix_norm_c',
           'gqa_w_q', 'gqa_w_kv', 'gqa_q_norm', 'gqa_k_norm', 'gqa_w_o', 'ffn_norm', 'ffn_w_in', 'ffn_w_out',
           'final_norm']
BIG = (('w_in_ab', (2, 1024, 744), 2), ('mla_w_uq', (2, 96, 8, 96), 1), ('mla_w_ukv', (2, 64, 8, 128), 1),
       ('w_out_ab', (2, 768, 256), 2), ('gqa_w_q', (2, 256, 1024), 1), ('gqa_w_kv', (2, 256, 512), 1),
       ('gqa_w_o', (2, 256, 1024), 1), ('ffn_w_in', (4, 1024, 1408), 2), ('ffn_w_out', (4, 704, 1024), 1))
PACK_W = 1024
SMALL = (('mix_norm_ab', (2, 1024)), ('mla_q_norm', (2, 384)), ('mla_kv_norm', (2, 256)), ('gqa_q_norm', (2, 64)),
         ('gqa_k_norm', (2, 64)), ('ffn_norm', (4, 1024)), ('final_norm', (1024,)), ('mix_norm_c', (2, 1024)))
PACKET_ROWS = 88


def _cparams(sem=None):
    return pltpu.CompilerParams(dimension_semantics=sem, vmem_limit_bytes=VMEM_LIMIT_BYTES)


def _pick(n, pref, mult=LANES):
    if n <= pref:
        return n
    for d in range(pref - pref % mult, 0, -mult):
        if n % d == 0:
            return d
    return n


_DIMS = {"nn": (((1,), (0,)), ((), ())), "nt": (((1,), (1,)), ((), ())), "tn": (((0,), (0,)), ((), ()))}


def _sigmoid(x):
    return 0.5 * (1.0 + jnp.tanh(0.5 * x))


def _silu_mul(gate, up):
    gate, up = gate.astype(F32), up.astype(F32)
    return gate * _sigmoid(gate) * up


def _silu_mul_bwd(gate, up, da):
    gate, up = gate.astype(F32), up.astype(F32)
    sig = _sigmoid(gate)
    silu = gate * sig
    return da * up * (sig + silu * (1.0 - sig)), da * silu


def _mm(a, b, *, mode="nn", add=None, out_dtype=F32, gated=False, gate_up=None, b_part=None, name="mm"):
    if mode == "nn":
        (m, k), (k2, n) = a.shape, b.shape
    elif mode == "nt":
        (m, k), (n, k2) = a.shape, b.shape
    else:
        (k, m), (k2, n) = a.shape, b.shape
    if gated:
        k, m = (k // 2, m) if mode == "nn" else (k, m // 2)
    if b_part is not None:
        assert mode == "nt" and k2 == k * b_part[1]
        k2 = k
    assert k == k2, (a.shape, b.shape, mode)
    if mode == "tn":
        deep = a.dtype == BF16 and b.dtype == BF16
        bm, bn, bk = _pick(m, 1408), _pick(n, 1024), _pick(k, 2048 if deep else 1024, 16)
    else:
        bm, bn, bk = _pick(m, 512, 16), _pick(n, 1408), _pick(k, 2816)
    nk = k // bk
    assert m % bm == 0 and n % bn == 0 and k % bk == 0
    has_add, has_gu = add is not None, gate_up is not None
    assert not (has_add and has_gu) and not (gated and mode == "nt") and not (has_gu and mode != "nt")
    n_in = 2 + int(gated) + int(has_add) + 2 * int(has_gu)

    def body(*refs):
        a_val = _silu_mul(refs[0][...], refs[1][...]) if gated else refs[0][...]
        b_ref = refs[1 + int(gated)]
        part = lax.dot_general(a_val.astype(BF16), b_ref[...].astype(BF16), _DIMS[mode],
                               preferred_element_type=F32)

        def finish(r):
            if has_gu:
                d_gate, d_up = _silu_mul_bwd(refs[2][...], refs[3][...], r)
                refs[n_in][...] = d_gate.astype(refs[n_in].dtype)
                refs[n_in + 1][...] = d_up.astype(refs[n_in + 1].dtype)
                return
            if has_add:
                r = r + refs[n_in - 1][...]
            refs[n_in][...] = r.astype(refs[n_in].dtype)

        if nk == 1:
            finish(part)
        else:
            acc_ref = refs[-1]
            kk = pl.program_id(2)

            @pl.when(kk == 0)
            def _():
                acc_ref[...] = part

            @pl.when(kk > 0)
            def _():
                acc_ref[...] += part

            @pl.when(kk == nk - 1)
            def _():
                finish(acc_ref[...])

    a_bytes, b_bytes = a.size * a.dtype.itemsize, b.size * b.dtype.itemsize
    n_outer = nk == 1 and (n // bn) * a_bytes + b_bytes < a_bytes + (m // bm) * b_bytes

    def at(f):
        return (lambda j, i, kk: f(i, j, kk)) if n_outer else f

    if mode == "nn":
        a_specs = [pl.BlockSpec((bm, bk), at(lambda i, j, kk, o=o: (i, kk + o))) for o in ((0, nk) if gated else (0,))]
        b_spec = pl.BlockSpec((bk, bn), at(lambda i, j, kk: (kk, j)))
    elif mode == "nt":
        a_specs = [pl.BlockSpec((bm, bk), at(lambda i, j, kk: (i, kk)))]
        b_off = 0 if b_part is None else b_part[0] * nk
        b_spec = pl.BlockSpec((bn, bk), at(lambda i, j, kk: (j, kk + b_off)))
    else:
        a_specs = [pl.BlockSpec((bk, bm), at(lambda i, j, kk, o=o: (kk, i + o)))
                   for o in ((0, m // bm) if gated else (0,))]
        b_spec = pl.BlockSpec((bk, bn), at(lambda i, j, kk: (kk, j)))
    o_spec = pl.BlockSpec((bm, bn), at(lambda i, j, kk: (i, j)))
    in_specs, args = a_specs + [b_spec], [a] * len(a_specs) + [b]
    if has_add:
        in_specs, args = in_specs + [o_spec], args + [add]
    if has_gu:
        in_specs += [o_spec, pl.BlockSpec((bm, bn), at(lambda i, j, kk: (i, j + n // bn)))]
        args += [gate_up, gate_up]
    out = jax.ShapeDtypeStruct((m, n), out_dtype)
    grid = (n // bn, m // bm, nk) if n_outer else (m // bm, n // bn, nk)
    return pl.pallas_call(
        body, name=name, grid=grid, in_specs=in_specs, out_specs=[o_spec, o_spec] if has_gu else o_spec,
        out_shape=[out, out] if has_gu else out,
        scratch_shapes=[pltpu.VMEM((bm, bn), F32)] if nk > 1 else [],
        compiler_params=_cparams(("parallel", "parallel", "arbitrary")),
    )(*args)


def _rows_call(fn, rows, consts, out_rows, out_accs=(), *, bs=256, name):
    s = rows[0].shape[0]
    bs = min(bs, s)
    assert s % bs == 0
    nr, nc, no, na = len(rows), len(consts), len(out_rows), len(out_accs)

    def body(*refs):
        vals = [r[...] for r in refs[:nr + nc]]
        outs = refs[nr + nc:]
        res = fn(*vals)
        if not isinstance(res, (tuple, list)):
            res = (res,)
        assert len(res) == no + na, (len(res), no, na)
        for r, v in zip(outs[:no], res[:no]):
            r[...] = v.astype(r.dtype)
        if na:
            i = pl.program_id(0)
            for r, v in zip(outs[no:], res[no:]):
                @pl.when(i == 0)
                def _(r=r, v=v):
                    r[...] = v

                @pl.when(i > 0)
                def _(r=r, v=v):
                    r[...] += v

    in_specs = [pl.BlockSpec((bs, a.shape[1]), lambda i: (i, 0)) for a in rows]
    in_specs += [pl.BlockSpec(c.shape, lambda i: (0, 0)) for c in consts]
    out_specs = [pl.BlockSpec((bs, c), lambda i: (i, 0)) for c, _ in out_rows]
    out_specs += [pl.BlockSpec(tuple(sh), lambda i: (0, 0)) for sh in out_accs]
    out_shape = [jax.ShapeDtypeStruct((s, c), dt) for c, dt in out_rows]
    out_shape += [jax.ShapeDtypeStruct(tuple(sh), F32) for sh in out_accs]
    res = pl.pallas_call(
        body, name=name, grid=(s // bs,), in_specs=in_specs, out_specs=out_specs, out_shape=out_shape,
        compiler_params=_cparams(("arbitrary",) if na else ("parallel",)),
    )(*rows, *consts)
    return res


def _rms(x, g):
    return x * lax.rsqrt(jnp.mean(x * x, axis=-1, keepdims=True) + NORM_EPS) * g


def _rms_bwd_math(x, g, dy):
    r = lax.rsqrt(jnp.mean(x * x, axis=-1, keepdims=True) + NORM_EPS)
    u = dy * g
    dx = r * u - x * (r * r * r) * jnp.mean(u * x, axis=-1, keepdims=True)
    dg = jnp.sum(dy * x * r, axis=0, keepdims=True)
    return dx, dg


NORM_ROWS = 512


def _rms_fwd(x, g, name):
    return _rows_call(lambda xv, gv: _rms(xv, gv), [x], [g], [(x.shape[1], BF16)], bs=NORM_ROWS, name=name)[0]


def _rms_bwd(x, g, dy, dres, name):
    def fn(xv, dyv, dresv, gv):
        dx, dg = _rms_bwd_math(xv, gv, dyv.astype(F32))
        return dx + dresv, dg
    return _rows_call(fn, [x, dy, dres], [g], [(x.shape[1], F32)], [(1, x.shape[1])], bs=NORM_ROWS, name=name)


def _chunkdot(x, m):
    outs = [jnp.dot(x[:, c:c + LANES], m, precision=lax.Precision.HIGHEST, preferred_element_type=F32)
            for c in range(0, x.shape[1], LANES)]
    return outs[0] if len(outs) == 1 else jnp.concatenate(outs, axis=1)


def _lanes(t, width):
    n = width // LANES
    return t if n == 1 else jnp.concatenate([t] * n, axis=1)


def _rope(x, cs, swap):
    w = x.shape[1]
    return x * _lanes(cs[:, :LANES], w) + _chunkdot(x, swap) * _lanes(cs[:, LANES:], w)


def _rope_t(dy, cs, swap):
    w = dy.shape[1]
    return dy * _lanes(cs[:, :LANES], w) + _chunkdot(dy * _lanes(cs[:, LANES:], w), swap)


def _swap_matrix():
    m = np.zeros((LANES, LANES), np.float32)
    for j in range(LANES):
        src = j + 16 if (j % 32) < 16 else j - 16
        m[src, j] = 1.0
    return jnp.asarray(m)


def _seg_matrix(seg):
    idx = np.arange(LANES) // seg
    return jnp.asarray((idx[:, None] == idx[None, :]).astype(np.float32))


def _rope_tables(s):
    pos = jnp.arange(s)

    def angles(p, dim):
        freqs = ROPE_THETA ** (-jnp.arange(0, dim, 2, dtype=F32) / dim)
        ang = p.astype(F32)[:, None] * freqs[None, :]
        return jnp.cos(ang), jnp.sin(ang)

    cos_t, sin_t = angles(pos, MLA_ROPE)
    one, zero = jnp.ones((s, 64), F32), jnp.zeros((s, 64), F32)
    mla = jnp.concatenate([one, cos_t, cos_t, one[:, :32], zero, -sin_t, sin_t, zero[:, :32]], axis=1)
    cos_r, sin_r = angles(pos // GRID_W, GQA_HEAD_DIM // 2)
    cos_c, sin_c = angles(pos % GRID_W, GQA_HEAD_DIM // 2)
    c64 = jnp.concatenate([cos_r, cos_r, cos_c, cos_c], axis=1)
    s64 = jnp.concatenate([-sin_r, sin_r, -sin_c, sin_c], axis=1)
    gqa = jnp.concatenate([c64, c64, s64, s64], axis=1)
    return mla, gqa


def _stack_heads(ref, heads, d, dtype=None):
    parts = [ref[:, hd * d:(hd + 1) * d] for hd in heads]
    out = parts[0] if len(parts) == 1 else jnp.concatenate(parts, axis=0)
    return out if dtype is None else out.astype(dtype)


def _fill_v_ones(v_ref, va_ref, hb, dv):
    @pl.when(pl.program_id(1) == 0)
    def _():
        ones = jnp.ones((v_ref.shape[0], dv), BF16)
        for h in range(hb):
            va_ref[:, 2 * h * dv:(2 * h + 1) * dv] = v_ref[:, h * dv:(h + 1) * dv]
            va_ref[:, (2 * h + 1) * dv:(2 * h + 2) * dv] = ones


def _flash_fwd(q, k, v, *, R, dk, dv, hb, bq, bk, name):
    s = q.shape[0]
    g = k.shape[1] // dk
    ng = g // hb
    bq, bk = min(bq, s), min(bk, s)
    nq, nkb = s // bq, s // bk
    rb = R * bq

    def body(q_ref, k_ref, v_ref, o_ref, lse_ref, va_ref):
        _fill_v_ones(v_ref, va_ref, hb, dv)
        head_sets = [[h * R + r for r in range(R)] for h in range(hb)]
        qss = [_stack_heads(q_ref, heads, dk) for heads in head_sets]

        def step(jj, carry):
            carry = list(carry)
            rows = [pl.ds(pl.multiple_of((jj * unroll + u) * bk, bk), bk) for u in range(unroll)]
            scs = [[lax.dot_general(qss[h], k_ref[rows[u], h * dk:(h + 1) * dk], _DIMS["nt"],
                                    preferred_element_type=F32) for h in range(hb)] for u in range(unroll)]
            for u in range(unroll):
                m2s = [jnp.maximum(carry[h][0], jnp.max(scs[u][h], axis=1, keepdims=True)) for h in range(hb)]
                ps = [jnp.exp2(scs[u][h] - m2s[h]).astype(BF16) for h in range(hb)]
                pvs = [jnp.dot(ps[h], va_ref[rows[u], 2 * h * dv:2 * (h + 1) * dv], preferred_element_type=F32)
                       for h in range(hb)]
                carry = [(m2s[h], jnp.exp2(carry[h][0] - m2s[h]) * carry[h][1] + pvs[h]) for h in range(hb)]
            return tuple(carry)

        unroll = 4 if nkb % 4 == 0 else 1
        init = tuple((jnp.full((rb, 1), NEG_INF, F32), jnp.zeros((rb, 2 * dv), F32)) for _ in range(hb))
        final = lax.fori_loop(0, nkb // unroll, step, init)
        for h, heads in enumerate(head_sets):
            m, acc = final[h]
            l = acc[:, dv:dv + 1]
            o = acc[:, :dv] / l
            lse = m + jnp.log2(l)
            for r, hd in enumerate(heads):
                o_ref[:, hd * dv:(hd + 1) * dv] = o[r * bq:(r + 1) * bq].astype(o_ref.dtype)
                lse_ref[0, :, hd:hd + 1] = lse[r * bq:(r + 1) * bq]

    return pl.pallas_call(
        body, name=name, grid=(ng, nq),
        in_specs=[pl.BlockSpec((bq, hb * R * dk), lambda gi, i: (i, gi)),
                  pl.BlockSpec((s, hb * dk), lambda gi, i: (0, gi)),
                  pl.BlockSpec((s, hb * dv), lambda gi, i: (0, gi))],
        out_specs=[pl.BlockSpec((bq, hb * R * dv), lambda gi, i: (i, gi)),
                   pl.BlockSpec((1, bq, hb * R), lambda gi, i: (gi, i, 0))],
        out_shape=[jax.ShapeDtypeStruct((s, g * R * dv), BF16), jax.ShapeDtypeStruct((ng, s, hb * R), F32)],
        scratch_shapes=[pltpu.VMEM((s, 2 * hb * dv), BF16)],
        compiler_params=_cparams(("parallel", "arbitrary")),
    )(q, k, v)


def _flash_bwd(q, k, v, o, do, lse, *, R, dk, dv, hb, bq, bk, name, unroll=1):
    s = q.shape[0]
    g = k.shape[1] // dk
    ng = g // hb
    bq, bk = min(bq, s), min(bk, s)
    nq, nkb = s // bq, s // bk
    rb = R * bq

    def body(q_ref, k_ref, v_ref, o_ref, do_ref, lse_ref, dq_ref, dkt_ref, dvt_ref, va_ref):
        @pl.when(pl.program_id(1) == 0)
        def _():
            dkt_ref[...] = jnp.zeros(dkt_ref.shape, F32)
            dvt_ref[...] = jnp.zeros(dvt_ref.shape, F32)

        _fill_v_ones(v_ref, va_ref, hb, dv)
        lane = lax.broadcasted_iota(jnp.int32, (rb, dv), 1)
        head_sets = [[h * R + r for r in range(R)] for h in range(hb)]
        q_t = jnp.transpose(q_ref[...].astype(F32)).astype(BF16)
        do_t = jnp.transpose(do_ref[...].astype(F32)).astype(BF16)

        def stack_t(t, heads, d):
            parts = [t[hd * d:(hd + 1) * d] for hd in heads]
            return parts[0] if len(parts) == 1 else jnp.concatenate(parts, axis=1)

        qss, qts, dots, dosas, lcols = [], [], [], [], []
        for heads in head_sets:
            dos = _stack_heads(do_ref, heads, dv, BF16)
            delta = jnp.sum(dos.astype(F32) * _stack_heads(o_ref, heads, dv, F32), axis=1, keepdims=True)
            hi = delta.astype(BF16).astype(F32)
            lo = delta - hi
            qss.append(_stack_heads(q_ref, heads, dk))
            qts.append(stack_t(q_t, heads, dk))
            dots.append(stack_t(do_t, heads, dv))
            dosas.append(jnp.concatenate(
                [dos, jnp.where(lane == 0, -hi, jnp.where(lane == 1, -lo, 0.0)).astype(BF16)], axis=1))
            cols = [lse_ref[0, :, hd:hd + 1] for hd in heads]
            lcols.append(cols[0] if R == 1 else jnp.concatenate(cols, axis=0))

        def step(jj, dqs):
            hs = range(hb)
            for u in range(unroll):
                j = jj * unroll + u
                r0 = pl.multiple_of(j * bk, bk)
                kjs = [k_ref[pl.ds(r0, bk), h * dk:(h + 1) * dk] for h in hs]
                vas = [va_ref[pl.ds(r0, bk), 2 * h * dv:2 * (h + 1) * dv] for h in hs]
                ss = [lax.dot_general(qss[h], kjs[h], _DIMS["nt"], preferred_element_type=F32) for h in hs]
                dps = [lax.dot_general(dosas[h], vas[h], _DIMS["nt"], preferred_element_type=F32) for h in hs]
                ps = [jnp.exp2(ss[h] - lcols[h]) for h in hs]
                pbs = [ps[h].astype(BF16) for h in hs]
                dss = [(ps[h] * dps[h]).astype(BF16) for h in hs]
                for h in hs:
                    dvt_ref[0, j, h * dv:(h + 1) * dv, :] += jnp.dot(dots[h], pbs[h], preferred_element_type=F32)
                for h in hs:
                    dkt_ref[0, j, h * dk:(h + 1) * dk, :] += jnp.dot(qts[h], dss[h], preferred_element_type=F32)
                dqs = tuple(dqs[h] + jnp.dot(dss[h], kjs[h], preferred_element_type=F32) for h in hs)
            return dqs

        assert nkb % unroll == 0
        dqs = lax.fori_loop(0, nkb // unroll, step, tuple(jnp.zeros((rb, dk), F32) for _ in range(hb)))
        for h, heads in enumerate(head_sets):
            for r, hd in enumerate(heads):
                dq_ref[:, hd * dk:(hd + 1) * dk] = dqs[h][r * bq:(r + 1) * bq]

    qspec = pl.BlockSpec((bq, hb * R * dk), lambda gi, i: (i, gi))
    ospec = pl.BlockSpec((bq, hb * R * dv), lambda gi, i: (i, gi))
    kspec = pl.BlockSpec((s, hb * dk), lambda gi, i: (0, gi))
    vspec = pl.BlockSpec((s, hb * dv), lambda gi, i: (0, gi))
    dq, dkt, dvt = pl.pallas_call(
        body, name=name, grid=(ng, nq),
        in_specs=[qspec, kspec, vspec, ospec, ospec, pl.BlockSpec((1, bq, hb * R), lambda gi, i: (gi, i, 0))],
        out_specs=[qspec, pl.BlockSpec((1, nkb, hb * dk, bk), lambda gi, i: (gi, 0, 0, 0)),
                   pl.BlockSpec((1, nkb, hb * dv, bk), lambda gi, i: (gi, 0, 0, 0))],
        out_shape=[jax.ShapeDtypeStruct((s, g * R * dk), F32), jax.ShapeDtypeStruct((ng, nkb, hb * dk, bk), F32),
                   jax.ShapeDtypeStruct((ng, nkb, hb * dv, bk), F32)],
        scratch_shapes=[pltpu.VMEM((s, 2 * hb * dv), BF16)],
        compiler_params=_cparams(("parallel", "arbitrary")),
    )(q, k, v, o, do, lse)
    return dq, _keys_first(dkt, name + "_dk"), _keys_first(dvt, name + "_dv")


def _keys_first(t, name):
    ng, nkb, f, bk = t.shape

    def body(t_ref, o_ref):
        for j in range(nkb):
            o_ref[j * bk:(j + 1) * bk, :] = jnp.transpose(t_ref[0, j])

    return pl.pallas_call(
        body, name=name, grid=(ng,),
        in_specs=[pl.BlockSpec((1, nkb, f, bk), lambda gi: (gi, 0, 0, 0))],
        out_specs=pl.BlockSpec((nkb * bk, f), lambda gi: (0, gi)),
        out_shape=jax.ShapeDtypeStruct((nkb * bk, ng * f), t.dtype),
        compiler_params=_cparams(("parallel",)),
    )(t)


DIL_T = 1024
DIL_P = DIL_HALF
DIL_NCOL = IN_B // DIL_W


DIL_BATCH = 8


def _alibi_slope(head):
    return float(2.0 ** (-8.0 * (head + 1) / DIL_HEADS))


def _slot(sl_i):
    return slice(sl_i * DIL_HEAD_DIM, (sl_i + 1) * DIL_HEAD_DIM)


def _halo_specs(d, col, s, t):
    h = DIL_P * d
    per, last = t // h, s // h - 1
    return [pl.BlockSpec((h, DIL_W), lambda c: (jnp.maximum(c * per - 1, 0), col)),
            pl.BlockSpec((t, DIL_W), lambda c: (c, col)),
            pl.BlockSpec((h, DIL_W), lambda c: (jnp.minimum((c + 1) * per, last), col))]


def _staging(rows):
    return tuple(pltpu.VMEM((rows, LANES), F32) for _ in range(DIL_W // LANES))


def _stage(buf, refs):
    off = 0
    for r in refs:
        val = r[...].astype(F32)
        for j in range(DIL_W // LANES):
            buf[j][off:off + r.shape[0], :] = val[:, j * LANES:(j + 1) * LANES]
        off += r.shape[0]


def _unstage(buf, ref):
    ref[...] = jnp.concatenate([half[...] for half in buf], axis=1).astype(ref.dtype)


def _sub_tiles(d, t):
    return [(b * DIL_P * d + r, b * DIL_P) for b in range(t // (DIL_P * d)) for r in range(d)]


def _rows(start, size, d):
    return pl.ds(start, size, stride=d) if d > 1 else pl.ds(start, size)


def _strided(buf, start, size, d):
    return jnp.concatenate([half[_rows(start, size, d), :] for half in buf], axis=1)


def _put_strided(buf, start, d, val):
    for j in range(DIL_W // LANES):
        buf[j][_rows(start, val.shape[0], d), :] = val[:, j * LANES:(j + 1) * LANES]


def _band(u0, length, d, queries_wide):
    if queries_wide:
        shape = (3 * DIL_P, DIL_P)
        wide = u0 - DIL_P + lax.broadcasted_iota(jnp.int32, shape, 0)
        narrow = u0 + lax.broadcasted_iota(jnp.int32, shape, 1)
    else:
        shape = (DIL_P, 3 * DIL_P)
        narrow = u0 + lax.broadcasted_iota(jnp.int32, shape, 0)
        wide = u0 - DIL_P + lax.broadcasted_iota(jnp.int32, shape, 1)
    rel = jnp.abs(wide - narrow)
    valid = (rel <= DIL_HALF) & (wide >= 0) & (wide < length)
    return valid, rel.astype(F32) * float(d)


def _dil_fwd(zb, grp, name):
    s = zb.shape[0]
    d = DIL_PAIRS[grp][1]
    t = min(DIL_T, s)
    h = DIL_P * d
    scale = DIL_HEAD_DIM ** -0.5

    def body(q_ref, kp, kc, kn, vp, vc, vn, o_ref, lse_ref, qbuf, kbuf, vbuf, obuf, lbuf):
        _stage(qbuf, (q_ref,))
        _stage(kbuf, (kp, kc, kn))
        _stage(vbuf, (vp, vc, vn))
        u_step = pl.program_id(0) * (t // d)
        tiles = _sub_tiles(d, t)
        for g0 in range(0, len(tiles), DIL_BATCH):
            batch = tiles[g0:g0 + DIL_BATCH]
            masks = [_band(u_step + u, s // d, d, False) for _, u in batch]
            qs = [_strided(qbuf, row, DIL_P, d).astype(BF16) for row, _ in batch]
            ks = [_strided(kbuf, row, 3 * DIL_P, d).astype(BF16) for row, _ in batch]
            vs = [_strided(vbuf, row, 3 * DIL_P, d).astype(BF16) for row, _ in batch]
            chains = [(i, sl_i) for i in range(len(batch)) for sl_i in range(DIL_SLOTS)]
            scs = [lax.dot_general(qs[i][:, _slot(sl_i)], ks[i][:, _slot(sl_i)], _DIMS["nt"],
                                   preferred_element_type=F32) for i, sl_i in chains]
            scs = [jnp.where(masks[i][0], sc * scale - _alibi_slope(grp * DIL_SLOTS + sl_i) * masks[i][1], NEG_INF)
                   for (i, sl_i), sc in zip(chains, scs)]
            ms = [jnp.max(sc, axis=1, keepdims=True) for sc in scs]
            es = [jnp.exp(sc - m) for sc, m in zip(scs, ms)]
            dens = [jnp.sum(e, axis=1, keepdims=True) for e in es]
            outs = [jnp.dot((e / den).astype(BF16), vs[i][:, _slot(sl_i)], preferred_element_type=F32)
                    for (i, sl_i), e, den in zip(chains, es, dens)]
            lses = [jnp.broadcast_to(m + jnp.log(den), (DIL_P, DIL_HEAD_DIM)) for m, den in zip(ms, dens)]
            for i, (row, _) in enumerate(batch):
                pick = slice(i * DIL_SLOTS, (i + 1) * DIL_SLOTS)
                _put_strided(obuf, row, d, jnp.concatenate(outs[pick], axis=1))
                _put_strided(lbuf, row, d, jnp.concatenate(lses[pick], axis=1))
        _unstage(obuf, o_ref)
        _unstage(lbuf, lse_ref)

    own = pl.BlockSpec((t, DIL_W), lambda c: (c, 0))
    return pl.pallas_call(
        body, name=name, grid=(s // t,),
        in_specs=[pl.BlockSpec((t, DIL_W), lambda c: (c, grp))] + _halo_specs(d, 3 + grp, s, t)
        + _halo_specs(d, 6 + grp, s, t),
        out_specs=[own, own], out_shape=[jax.ShapeDtypeStruct((s, DIL_W), F32)] * 2,
        scratch_shapes=[_staging(t), _staging(t + 2 * h), _staging(t + 2 * h), _staging(t), _staging(t)],
        compiler_params=_cparams(("parallel",)),
    )(zb, zb, zb, zb, zb, zb, zb)


def _dil_bwd(zb, do, lse, dl, grp, name):
    s = zb.shape[0]
    d = DIL_PAIRS[grp][1]
    t = min(DIL_T, s)
    h = DIL_P * d
    scale = DIL_HEAD_DIM ** -0.5

    def chain_grads(qs, ks, vs, dos, lses, dls, masks):
        chains = [(i, sl_i) for i in range(len(qs)) for sl_i in range(DIL_SLOTS)]
        scs = [lax.dot_general(qs[i][:, _slot(sl_i)], ks[i][:, _slot(sl_i)], _DIMS["nt"],
                               preferred_element_type=F32) for i, sl_i in chains]
        dps = [lax.dot_general(dos[i][:, _slot(sl_i)], vs[i][:, _slot(sl_i)], _DIMS["nt"],
                               preferred_element_type=F32) for i, sl_i in chains]
        ps = [jnp.exp(jnp.where(masks[i][0], sc * scale - _alibi_slope(grp * DIL_SLOTS + sl_i) * masks[i][1],
                                NEG_INF) - lses[i][:, sl_i * DIL_HEAD_DIM:sl_i * DIL_HEAD_DIM + 1])
              for (i, sl_i), sc in zip(chains, scs)]
        dss = [(p * (dp - dls[i][:, sl_i * DIL_HEAD_DIM:sl_i * DIL_HEAD_DIM + 1]) * scale).astype(BF16)
               for (i, sl_i), p, dp in zip(chains, ps, dps)]
        return chains, ps, dss

    def dq_body(q_ref, kp, kc, kn, vp, vc, vn, do_ref, lse_ref, dl_ref, dq_ref, qbuf, kbuf, vbuf, dobuf, lsebuf,
                dlbuf, obuf):
        _stage(qbuf, (q_ref,))
        _stage(dobuf, (do_ref,))
        _stage(lsebuf, (lse_ref,))
        _stage(dlbuf, (dl_ref,))
        _stage(kbuf, (kp, kc, kn))
        _stage(vbuf, (vp, vc, vn))
        u_step = pl.program_id(0) * (t // d)
        tiles = _sub_tiles(d, t)
        for g0 in range(0, len(tiles), DIL_BATCH):
            batch = tiles[g0:g0 + DIL_BATCH]
            masks = [_band(u_step + u, s // d, d, False) for _, u in batch]
            narrow = [[_strided(b, row, DIL_P, d) for row, _ in batch] for b in (qbuf, dobuf, lsebuf, dlbuf)]
            ks = [_strided(kbuf, row, 3 * DIL_P, d).astype(BF16) for row, _ in batch]
            vs = [_strided(vbuf, row, 3 * DIL_P, d).astype(BF16) for row, _ in batch]
            chains, _, dss = chain_grads([a.astype(BF16) for a in narrow[0]], ks, vs,
                                         [a.astype(BF16) for a in narrow[1]], narrow[2], narrow[3], masks)
            outs = [jnp.dot(ds, ks[i][:, _slot(sl_i)], preferred_element_type=F32)
                    for (i, sl_i), ds in zip(chains, dss)]
            for i, (row, _) in enumerate(batch):
                _put_strided(obuf, row, d, jnp.concatenate(outs[i * DIL_SLOTS:(i + 1) * DIL_SLOTS], axis=1))
        _unstage(obuf, dq_ref)

    def dkv_body(k_ref, v_ref, qp, qc, qn, dop, doc, don, lp, lc, ln, dlp, dlc, dln, dk_ref, dv_ref,
                 kbuf, vbuf, qbuf, dobuf, lsebuf, dlbuf, dkbuf, dvbuf):
        _stage(kbuf, (k_ref,))
        _stage(vbuf, (v_ref,))
        _stage(qbuf, (qp, qc, qn))
        _stage(dobuf, (dop, doc, don))
        _stage(lsebuf, (lp, lc, ln))
        _stage(dlbuf, (dlp, dlc, dln))
        u_step = pl.program_id(0) * (t // d)
        tiles = _sub_tiles(d, t)
        for g0 in range(0, len(tiles), DIL_BATCH):
            batch = tiles[g0:g0 + DIL_BATCH]
            masks = [_band(u_step + u, s // d, d, True) for _, u in batch]
            ks = [_strided(kbuf, row, DIL_P, d).astype(BF16) for row, _ in batch]
            vs = [_strided(vbuf, row, DIL_P, d).astype(BF16) for row, _ in batch]
            wide_ = [[_strided(b, row, 3 * DIL_P, d) for row, _ in batch] for b in (qbuf, dobuf, lsebuf, dlbuf)]
            qs, dos = [a.astype(BF16) for a in wide_[0]], [a.astype(BF16) for a in wide_[1]]
            chains, ps, dss = chain_grads(qs, ks, vs, dos, wide_[2], wide_[3], masks)
            dvs = [lax.dot_general(p.astype(BF16), dos[i][:, _slot(sl_i)], _DIMS["tn"], preferred_element_type=F32)
                   for (i, sl_i), p in zip(chains, ps)]
            dks = [lax.dot_general(ds, qs[i][:, _slot(sl_i)], _DIMS["tn"], preferred_element_type=F32)
                   for (i, sl_i), ds in zip(chains, dss)]
            for i, (row, _) in enumerate(batch):
                pick = slice(i * DIL_SLOTS, (i + 1) * DIL_SLOTS)
                _put_strided(dkbuf, row, d, jnp.concatenate(dks[pick], axis=1))
                _put_strided(dvbuf, row, d, jnp.concatenate(dvs[pick], axis=1))
        _unstage(dkbuf, dk_ref)
        _unstage(dvbuf, dv_ref)

    def zcur(col):
        return pl.BlockSpec((t, DIL_W), lambda c: (c, col))

    own = pl.BlockSpec((t, DIL_W), lambda c: (c, 0))
    out = jax.ShapeDtypeStruct((s, DIL_W), F32)
    tile, wide = _staging(t), _staging(t + 2 * h)
    dq = pl.pallas_call(
        dq_body, name=name + "_dq", grid=(s // t,),
        in_specs=[zcur(grp)] + _halo_specs(d, 3 + grp, s, t) + _halo_specs(d, 6 + grp, s, t) + [own, own, own],
        out_specs=own, out_shape=out, scratch_shapes=[tile, wide, wide, tile, tile, tile, tile],
        compiler_params=_cparams(("parallel",)),
    )(zb, zb, zb, zb, zb, zb, zb, do, lse, dl)
    own3 = _halo_specs(d, 0, s, t)
    dk, dv = pl.pallas_call(
        dkv_body, name=name + "_dkv", grid=(s // t,),
        in_specs=[zcur(3 + grp), zcur(6 + grp)] + _halo_specs(d, grp, s, t) + own3 + own3 + own3,
        out_specs=[own, own], out_shape=[out, out],
        scratch_shapes=[tile, tile, wide, wide, wide, wide, tile, tile],
        compiler_params=_cparams(("parallel",)),
    )(zb, zb, zb, zb, zb, do, do, do, lse, lse, lse, dl, dl, dl)
    return dq, dk, dv


def _dil_combine(os_, ls_, name):
    def fn(o0, o1, o2, l0, l1, l2):
        m = jnp.maximum(jnp.maximum(l0, l1), l2)
        e0, e1, e2 = jnp.exp(l0 - m), jnp.exp(l1 - m), jnp.exp(l2 - m)
        den = e0 + e1 + e2
        comb = (e0 / den) * o0 + (e1 / den) * o1 + (e2 / den) * o2
        return comb, m + jnp.log(den)
    return _rows_call(fn, list(os_) + list(ls_), [], [(DIL_W, BF16), (DIL_W, F32)], name=name)


def _dil_combine_bwd(dcomb, os_, ls_, lt, seg64, name):
    def fn(dc, o0, o1, o2, l0, l1, l2, ltv, seg):
        w = [jnp.exp(l - ltv) for l in (l0, l1, l2)]
        comb = w[0] * o0 + w[1] * o1 + w[2] * o2
        t = _chunkdot(dc * comb, seg)
        return [wg * dc for wg in w] + [wg * t for wg in w]
    return _rows_call(fn, [dcomb] + list(os_) + list(ls_) + [lt], [seg64],
                      [(DIL_W, BF16)] * 3 + [(DIL_W, F32)] * 3, name=name)


def _mla_prep(za, gq, gkv, cs, swap, name):
    def fn(z, csv, gqv, gkvv, sw):
        return (_rms(z[:, :MLA_Q_RANK], gqv), _rms(z[:, MLA_Q_RANK:640], gkvv), _rope(z[:, 640:], csv, sw))
    return _rows_call(fn, [za, cs], [gq, gkv, swap], [(MLA_Q_RANK, BF16), (MLA_KV_RANK, BF16), (LANES, F32)],
                      name=name)


def _mla_prep_bwd(za, cs, dcq, dckv, dkr, gq, gkv, swap, name):
    def fn(z, csv, dcqv, dckvv, dkrv, gqv, gkvv, sw):
        d1, dg1 = _rms_bwd_math(z[:, :MLA_Q_RANK], gqv, dcqv)
        d2, dg2 = _rms_bwd_math(z[:, MLA_Q_RANK:640], gkvv, dckvv)
        d3 = _rope_t(dkrv, csv, sw)
        return jnp.concatenate([d1, d2, d3], axis=1), dg1, dg2
    return _rows_call(fn, [za, cs, dcq, dckv, dkr], [gq, gkv, swap], [(IN_A_PAD, BF16)],
                      [(1, MLA_Q_RANK), (1, MLA_KV_RANK)], name=name)


def _mla_qk(q_raw, k_pad, krr, cs, swap, name):
    w = MLA_HEADS * MLA_DK

    def fn(qv, kv, krv, csv, sw):
        return _rope(qv, csv, sw) * (MLA_SCALE * LOG2E), kv + _lanes(krv, w)
    return _rows_call(fn, [q_raw, k_pad, krr, cs], [swap], [(w, BF16), (w, BF16)], name=name)


def _mla_qk_bwd(dqh, dkh, cs, swap, name):
    w = MLA_HEADS * MLA_DK

    def fn(dq, dk, csv, sw):
        dk = dk * LN2
        acc = dk[:, :LANES]
        for h in range(1, MLA_HEADS):
            acc = acc + dk[:, h * LANES:(h + 1) * LANES]
        lane = lax.broadcasted_iota(jnp.int32, acc.shape, 1)
        acc = jnp.where((lane >= MLA_NOPE) & (lane < MLA_NOPE + MLA_ROPE), acc, 0.0)
        return _rope_t(dq * MLA_SCALE, csv, sw), dk, acc
    return _rows_call(fn, [dqh, dkh, cs], [swap], [(w, BF16), (w, BF16), (LANES, F32)], name=name)


def _head_norm(t, g2, seg):
    r = lax.rsqrt(_chunkdot(t * t, seg) * (1.0 / GQA_HEAD_DIM) + NORM_EPS)
    return t * r * _lanes(g2, t.shape[1]), r


def _head_norm_bwd(t, g2, seg, dn):
    w = t.shape[1]
    r = lax.rsqrt(_chunkdot(t * t, seg) * (1.0 / GQA_HEAD_DIM) + NORM_EPS)
    u = dn * _lanes(g2, w)
    dt = r * u - t * (r * r * r) * (_chunkdot(u * t, seg) * (1.0 / GQA_HEAD_DIM))
    dgw = jnp.sum(dn * t * r, axis=0, keepdims=True)
    dg = dgw[:, :LANES]
    for c in range(LANES, w, LANES):
        dg = dg + dgw[:, c:c + LANES]
    return dt, dg


def _gqa_prep(q_raw, kv_raw, cs, gq2, gk2, seg, swap, name):
    kw = GQA_KV_HEADS * GQA_HEAD_DIM

    def fn(qv, kvv, csv, gqv, gkv, sg, sw):
        qn, _ = _head_norm(qv, gqv, sg)
        kn, _ = _head_norm(kvv[:, :kw], gkv, sg)
        return _rope(qn, csv, sw) * (GQA_SCALE * LOG2E), _rope(kn, csv, sw), kvv[:, kw:]
    return _rows_call(fn, [q_raw, kv_raw, cs], [gq2, gk2, seg, swap],
                      [(GQA_HEADS * GQA_HEAD_DIM, BF16), (kw, BF16), (kw, BF16)], name=name)


def _gqa_prep_bwd(q_raw, kv_raw, cs, dqh, dkh, dv, gq2, gk2, seg, swap, name):
    kw = GQA_KV_HEADS * GQA_HEAD_DIM

    def fn(qv, kvv, csv, dq, dk, dvv, gqv, gkv, sg, sw):
        dqr, dgq = _head_norm_bwd(qv, gqv, sg, _rope_t(dq * GQA_SCALE, csv, sw))
        dkr, dgk = _head_norm_bwd(kvv[:, :kw], gkv, sg, _rope_t(dk * LN2, csv, sw))
        return dqr, jnp.concatenate([dkr, dvv], axis=1), dgq, dgk
    return _rows_call(fn, [q_raw, kv_raw, cs, dqh, dkh, dv], [gq2, gk2, seg, swap],
                      [(GQA_HEADS * GQA_HEAD_DIM, BF16), (2 * kw, BF16)], [(1, LANES), (1, LANES)], name=name)


def _loss_head(x, target, g, name):
    dm = x.shape[1]

    def fn(xv, tv, gv):
        err = _rms(xv, gv) - tv
        loss = 0.5 * jnp.sum(err * err) / dm
        dx, dg = _rms_bwd_math(xv, gv, err * (1.0 / dm))
        return dx, jnp.zeros((1, LANES), F32) + loss, dg
    return _rows_call(fn, [x, target], [g], [(dm, F32)], [(1, LANES), (1, dm)], name=name)


def _adamw(w, g, m, v, name):
    def fn(wv, gv, mv, vv):
        m2 = ADAM_B1 * mv + (1.0 - ADAM_B1) * gv
        v2 = ADAM_B2 * vv + (1.0 - ADAM_B2) * (gv * gv)
        m_hat = m2 / (1.0 - ADAM_B1 ** ADAM_STEP)
        v_hat = v2 / (1.0 - ADAM_B2 ** ADAM_STEP)
        return -ADAM_LR * (m_hat / (jnp.sqrt(v_hat) + ADAM_EPS) + ADAM_WD * wv), m2, v2
    c = w.shape[1]
    return _rows_call(fn, [w, g, m, v], [], [(c, F32)] * 3, bs=_pick(w.shape[0], 256, 8), name=name)


HBM_SPEC = pl.BlockSpec(memory_space=pltpu.HBM)
VMEM_SPEC = pl.BlockSpec(memory_space=pltpu.VMEM)


def _position():
    return lax.axis_index("x"), lax.axis_index("y"), lax.axis_index("c")


def _other_chips(x, y):
    return [(1 - x, y), (x, 1 - y), (1 - x, 1 - y)]


HALF_W = PACK_W // 2


def _cols(c):
    return pl.ds(pl.multiple_of(c * HALF_W, HALF_W), HALF_W)


def _all_gather_weights(packed):
    rows = packed.shape[0]

    def body(p_ref, g_ref, send_sems, recv_sems, local_sem):
        x, y, c = _position()
        chips = _other_chips(x, y)

        def half(chip, hc):
            return g_ref.at[2 * chip[0] + chip[1], :, _cols(hc)]

        def copy(j, src, dst, to):
            return pltpu.make_async_remote_copy(src_ref=src, dst_ref=dst, send_sem=send_sems.at[j],
                                                recv_sem=recv_sems.at[j], device_id=to, device_id_type=MESH)

        mine = pltpu.make_async_copy(p_ref, g_ref.at[2 * x + y], local_sem)
        mine.start()
        first = [copy(j, p_ref.at[:, _cols(c)], half((x, y), c), (*chip, c)) for j, chip in enumerate(chips)]
        for cp in first:
            cp.start()
        passed = [copy(3 + j, half(chip, c), half(chip, c), (x, y, 1 - c)) for j, chip in enumerate(chips)]
        for j, chip in enumerate(chips):
            copy(j, half(chip, c), half(chip, c), (x, y, c)).wait_recv()
            passed[j].start()
        for j, chip in enumerate(chips):
            copy(3 + j, half(chip, 1 - c), half(chip, 1 - c), (x, y, c)).wait_recv()
        for cp in first + passed:
            cp.wait_send()
        mine.wait()

    return pl.pallas_call(
        body, name="all_gather_weights", in_specs=[HBM_SPEC], out_specs=HBM_SPEC,
        out_shape=jax.ShapeDtypeStruct((4, rows, packed.shape[1]), packed.dtype),
        scratch_shapes=[pltpu.SemaphoreType.DMA((6,)), pltpu.SemaphoreType.DMA((6,)), pltpu.SemaphoreType.DMA],
    )(packed)


SEM_SPEC = pl.BlockSpec(memory_space=pltpu.SEMAPHORE)
ANY_SPEC = pl.BlockSpec(memory_space=pl.ANY)
DATAFLOW = pltpu.SideEffectType.DATAFLOW_SIDE_EFFECTING


def _hbm(a):
    return pltpu.with_memory_space_constraint(a, pltpu.HBM)


def _gather_start(packed, tag):
    rows = packed.shape[0]

    def body(p_ref, g_ref, send_sems, recv_sems, p_thru, g_thru, token):
        x, y, c = _position()
        for j, chip in enumerate(_other_chips(x, y)):
            for s in range(2):
                pltpu.make_async_remote_copy(
                    src_ref=p_ref.at[:, _cols(c)], dst_ref=g_ref.at[2 * x + y, :, _cols(c)],
                    send_sem=send_sems.at[2 * j + s], recv_sem=recv_sems.at[2 * j + s],
                    device_id=(*chip, 1 - c if s else c), device_id_type=MESH).start()
        token[...] = jnp.zeros_like(token)

    return pl.pallas_call(
        body, name=tag + "_start",
        out_shape=(pltpu.SemaphoreType.DMA((6,)), pltpu.SemaphoreType.DMA((6,)), pltpu.HBM(packed.shape, packed.dtype),
                   pltpu.HBM((4, rows, PACK_W), packed.dtype), jax.ShapeDtypeStruct((8, LANES), F32)),
        in_specs=(HBM_SPEC, HBM_SPEC), out_specs=(SEM_SPEC, SEM_SPEC, HBM_SPEC, HBM_SPEC, VMEM_SPEC),
        input_output_aliases={0: 2, 1: 3},
        compiler_params=pltpu.CompilerParams(has_side_effects=DATAFLOW),
    )(_hbm(packed), _hbm(lax.empty((4, rows, PACK_W), packed.dtype)))


def _gather_wait(send_sems, recv_sems, p_thru, g_thru, after, tag):
    def body(p_ref, g_ref, send_sems, recv_sems, after_ref, p_dead, got_ref):
        x, y, c = _position()
        for j, chip in enumerate(_other_chips(x, y)):
            for s in range(2):
                cp = pltpu.make_async_remote_copy(
                    src_ref=p_ref.at[:, _cols(c)], dst_ref=g_ref.at[2 * chip[0] + chip[1], :, _cols(1 - c if s else c)],
                    send_sem=send_sems.at[2 * j + s], recv_sem=recv_sems.at[2 * j + s],
                    device_id=(x, y, c), device_id_type=MESH)
                cp.wait_send()
                cp.wait_recv()

    return pl.pallas_call(
        body, name=tag + "_wait",
        out_shape=(pltpu.HBM(p_thru.shape, p_thru.dtype), pltpu.HBM(g_thru.shape, g_thru.dtype)),
        in_specs=(HBM_SPEC, HBM_SPEC, SEM_SPEC, SEM_SPEC, ANY_SPEC), out_specs=(HBM_SPEC, HBM_SPEC),
        input_output_aliases={0: 0, 1: 1},
        compiler_params=pltpu.CompilerParams(has_side_effects=DATAFLOW),
    )(p_thru, g_thru, send_sems, recv_sems, after)[1]


def _sibling_swap_halves(grads, tag):
    rows = grads.shape[1]

    def body(g_ref, a_ref, send_sem, recv_sem):
        x, y, c = _position()
        cp = pltpu.make_async_remote_copy(src_ref=g_ref.at[:, :, _cols(1 - c)], dst_ref=a_ref,
                                          send_sem=send_sem, recv_sem=recv_sem, device_id=(x, y, 1 - c),
                                          device_id_type=MESH)
        cp.start()
        cp.wait()

    return pl.pallas_call(
        body, name=tag + "_swap_cores", in_specs=[HBM_SPEC], out_specs=HBM_SPEC,
        out_shape=jax.ShapeDtypeStruct((4, rows, HALF_W), grads.dtype),
        scratch_shapes=[pltpu.SemaphoreType.DMA, pltpu.SemaphoreType.DMA],
    )(grads)


def _rs_block(rows):
    return max(d for d in range(16, 1601, 16) if rows % d == 0)


def _chip_sum(grads, other, c, tag):
    rows = other.shape[1]
    rb = _rs_block(rows)

    def body(c_ref, g_ref, a_ref, o_ref):
        o_ref[...] = (g_ref[...] + a_ref[...]).astype(o_ref.dtype)

    blk = (1, rb, HALF_W)
    return pl.pallas_call(
        body, name=tag + "_chip_sum",
        grid_spec=pltpu.PrefetchScalarGridSpec(
            num_scalar_prefetch=1, grid=(4, rows // rb),
            in_specs=[pl.BlockSpec(blk, lambda k, i, c_ref: (k, i, c_ref[0])),
                      pl.BlockSpec(blk, lambda k, i, c_ref: (k, i, 0))],
            out_specs=pl.BlockSpec(blk, lambda k, i, c_ref: (k, i, 0))),
        out_shape=jax.ShapeDtypeStruct(other.shape, BF16),
        compiler_params=_cparams(("parallel", "parallel")),
    )(jnp.reshape(c, (1,)).astype(jnp.int32), grads, other)


def _chip_copies(t_ref, b_ref, send_sems, recv_sems):
    x, y, c = _position()
    return [pltpu.make_async_remote_copy(src_ref=t_ref.at[2 * chip[0] + chip[1]], dst_ref=b_ref.at[j],
                                         send_sem=send_sems.at[j], recv_sem=recv_sems.at[j],
                                         device_id=(*chip, c), device_id_type=MESH)
            for j, chip in enumerate(_other_chips(x, y))]


def _send_chip_sums(sums, tag):
    def body(t_ref, b_ref, send_sems, recv_sems):
        copies = _chip_copies(t_ref, b_ref, send_sems, recv_sems)
        for cp in copies:
            cp.start()
        for cp in copies:
            cp.wait()

    return pl.pallas_call(
        body, name=tag + "_send_chips", in_specs=[HBM_SPEC], out_specs=HBM_SPEC,
        out_shape=jax.ShapeDtypeStruct((3,) + sums.shape[1:], sums.dtype),
        scratch_shapes=[pltpu.SemaphoreType.DMA((3,)), pltpu.SemaphoreType.DMA((3,))],
    )(sums)


def _send_chip_sums_start(sums, tag):
    land = (3,) + sums.shape[1:]

    def body(t_ref, b_ref, send_sems, recv_sems, t_thru, b_thru, token):
        for cp in _chip_copies(t_ref, b_ref, send_sems, recv_sems):
            cp.start()
        token[...] = jnp.zeros_like(token)

    return pl.pallas_call(
        body, name=tag + "_send_chips_start",
        out_shape=(pltpu.SemaphoreType.DMA((3,)), pltpu.SemaphoreType.DMA((3,)), pltpu.HBM(sums.shape, sums.dtype),
                   pltpu.HBM(land, sums.dtype), jax.ShapeDtypeStruct((8, LANES), F32)),
        in_specs=(HBM_SPEC, HBM_SPEC), out_specs=(SEM_SPEC, SEM_SPEC, HBM_SPEC, HBM_SPEC, VMEM_SPEC),
        input_output_aliases={0: 2, 1: 3},
        compiler_params=pltpu.CompilerParams(has_side_effects=DATAFLOW),
    )(_hbm(sums), _hbm(lax.empty(land, sums.dtype)))


def _send_chip_sums_wait(send_sems, recv_sems, t_thru, b_thru, after, tag):
    def body(t_ref, b_ref, send_sems, recv_sems, after_ref, t_dead, got_ref):
        for cp in _chip_copies(t_ref, b_ref, send_sems, recv_sems):
            cp.wait_send()
            cp.wait_recv()

    return pl.pallas_call(
        body, name=tag + "_send_chips_wait",
        out_shape=(pltpu.HBM(t_thru.shape, t_thru.dtype), pltpu.HBM(b_thru.shape, b_thru.dtype)),
        in_specs=(HBM_SPEC, HBM_SPEC, SEM_SPEC, SEM_SPEC, ANY_SPEC), out_specs=(HBM_SPEC, HBM_SPEC),
        input_output_aliases={0: 0, 1: 1},
        compiler_params=pltpu.CompilerParams(has_side_effects=DATAFLOW),
    )(t_thru, b_thru, send_sems, recv_sems, after)[1]


def _final_sum(grads, other, recv, k, c, tag):
    rows = other.shape[1]
    rb = _rs_block(rows)

    def body(k_ref, c_ref, g_ref, a_ref, b_ref, o_ref):
        own = g_ref[0] + a_ref[0]
        o_ref[...] = ((own + b_ref[0].astype(F32)) + b_ref[1].astype(F32)) + b_ref[2].astype(F32)

    return pl.pallas_call(
        body, name=tag + "_final_sum",
        grid_spec=pltpu.PrefetchScalarGridSpec(
            num_scalar_prefetch=2, grid=(rows // rb,),
            in_specs=[pl.BlockSpec((1, rb, HALF_W), lambda i, k_ref, c_ref: (k_ref[0], i, c_ref[0])),
                      pl.BlockSpec((1, rb, HALF_W), lambda i, k_ref, c_ref: (k_ref[0], i, 0)),
                      pl.BlockSpec((3, rb, HALF_W), lambda i, k_ref, c_ref: (0, i, 0))],
            out_specs=pl.BlockSpec((rb, HALF_W), lambda i, k_ref, c_ref: (i, 0))),
        out_shape=jax.ShapeDtypeStruct((rows, HALF_W), F32),
        compiler_params=_cparams(("parallel",)),
    )(jnp.reshape(k, (1,)).astype(jnp.int32), jnp.reshape(c, (1,)).astype(jnp.int32), grads, other, recv)


def _join_halves(half, core, tag):
    def body(h_ref, o_ref, send_sem, recv_sem):
        x, y, c = _position()
        cp = pltpu.make_async_remote_copy(src_ref=h_ref, dst_ref=o_ref, send_sem=send_sem, recv_sem=recv_sem,
                                          device_id=(x, y, 1 - c), device_id_type=MESH)
        cp.start()
        cp.wait()

    other = pl.pallas_call(
        body, name=tag + "_join_cores", in_specs=[HBM_SPEC], out_specs=HBM_SPEC,
        out_shape=jax.ShapeDtypeStruct(half.shape, half.dtype),
        scratch_shapes=[pltpu.SemaphoreType.DMA, pltpu.SemaphoreType.DMA],
    )(half)
    first = core == 0
    return jnp.concatenate([jnp.where(first, half, other), jnp.where(first, other, half)], axis=1)


def _all_reduce_packet(packet):
    rows = packet.shape[0]

    def body(p_ref, o_ref, buf, send_sems, recv_sems):
        x, y, c = _position()
        me = 4 * x + 2 * y + c
        buf[me] = p_ref[...]

        def flip(v, bit):
            return 1 - v if bit else v

        for p in range(1, 8):
            peer = (flip(x, p & 4), flip(y, p & 2), flip(c, p & 1))
            pltpu.make_async_remote_copy(src_ref=p_ref, dst_ref=buf.at[me], send_sem=send_sems.at[p - 1],
                                         recv_sem=recv_sems.at[p - 1], device_id=peer, device_id_type=MESH).start()
        for p in range(1, 8):
            peer = (flip(x, p & 4), flip(y, p & 2), flip(c, p & 1))
            slot = 4 * peer[0] + 2 * peer[1] + peer[2]
            cp = pltpu.make_async_remote_copy(src_ref=p_ref, dst_ref=buf.at[slot], send_sem=send_sems.at[p - 1],
                                              recv_sem=recv_sems.at[p - 1], device_id=peer, device_id_type=MESH)
            cp.wait_recv()
            cp.wait_send()
        acc = buf[0]
        for dev in range(1, 8):
            acc = acc + buf[dev]
        o_ref[...] = acc

    return pl.pallas_call(
        body, name="all_reduce_packet", in_specs=[VMEM_SPEC], out_specs=VMEM_SPEC,
        out_shape=jax.ShapeDtypeStruct(packet.shape, F32),
        scratch_shapes=[pltpu.VMEM((8, rows, LANES), F32), pltpu.SemaphoreType.DMA((7,)),
                        pltpu.SemaphoreType.DMA((7,))],
    )(packet)


def _stack_range(name, n_stack, pack):
    if name.startswith('gqa'):
        return (0, 0) if pack == 0 else (0, n_stack)
    return (0, 1) if pack == 0 else (1, n_stack)


def _pack_members(pack):
    out = []
    for n, shape, ax in BIG:
        lo, hi = _stack_range(n, shape[0], pack)
        if hi > lo:
            out.append((n, (hi - lo,) + shape[1:], ax, (lo, hi)))
    return out


PACK_ROW_MULTIPLE = 512


def _pad_rows(parts, dtype):
    rows = sum(p.shape[0] for p in parts)
    pad = -rows % PACK_ROW_MULTIPLE
    return jnp.concatenate(parts + ([jnp.zeros((pad, PACK_W), dtype)] if pad else []), axis=0)


def _pack_blocks(blocks, dtype, pack):
    return _pad_rows([blocks[n][lo:hi].astype(dtype).reshape(-1, PACK_W)
                      for n, _, _, (lo, hi) in _pack_members(pack)], dtype)


def _unpack_blocks(packed, pack):
    out, off = {}, 0
    for n, shape, _, _ in _pack_members(pack):
        r = math.prod(shape) // PACK_W
        out[n] = packed[off:off + r].reshape(shape)
        off += r
    return out


def _unpack_gathered(gathered, pack, own=None, chip=None):
    blocks = [gathered[k] if own is None else jnp.where(chip == k, own, gathered[k]) for k in range(4)]
    per_chip = [_unpack_blocks(blocks[k], pack) for k in range(4)]
    return {n: jnp.concatenate([per_chip[k][n] for k in range(4)], axis=ax) for n, _, ax, _ in _pack_members(pack)}


def _pack_full(full, dtype, pack):
    chips = []
    for k in range(4):
        parts = []
        for n, shape, ax, _ in _pack_members(pack):
            blk = lax.slice_in_dim(full[n], k * shape[ax], (k + 1) * shape[ax], axis=ax)
            parts.append(blk.astype(dtype).reshape(-1, PACK_W))
        chips.append(_pad_rows(parts, dtype))
    return jnp.stack(chips, axis=0)


def _pack_small(vals, loss_row):
    rows = [loss_row.reshape(1, LANES)]
    for n, shape in SMALL:
        v = vals.get(n)
        v = jnp.zeros(shape, F32) if v is None else v
        rows.append(v.astype(F32).reshape(-1, LANES))
    packet = jnp.concatenate(rows, axis=0)
    return jnp.pad(packet, ((0, PACKET_ROWS - packet.shape[0]), (0, 0)))


def _unpack_small(packet):
    out, off = {}, 1
    for n, shape in SMALL:
        r = math.prod(shape) // LANES
        out[n] = packet[off:off + r].reshape(shape)
        off += r
    return packet[0, 0], out


_MLA = dict(R=1, dk=MLA_DK, dv=MLA_V, hb=2, bq=512, bk=512)
_MLA_FWD_BQ = 1024
_BWD_UNROLL = 8
_GQA = dict(R=GQA_HEADS // GQA_KV_HEADS, dk=GQA_HEAD_DIM, dv=GQA_HEAD_DIM, hb=2, bq=256, bk=512)


def _layer_params(layer, full, gains):
    pack = 0 if layer == 0 else 1
    i = layer // 2

    def mat(name):
        lo, _ = _stack_range(name, 4 if name.startswith('ffn') else 2, pack)
        return full[name][(layer if name.startswith('ffn') else i) - lo]

    p = dict(ffn_norm=gains['ffn_norm'][layer][None], ffn_w_in=mat('ffn_w_in'), ffn_w_out=mat('ffn_w_out'))
    if layer % 2 == 0:
        w_in = mat('w_in_ab')
        zeros = jnp.zeros((D_MODEL, 32), w_in.dtype)
        p['w_a'] = jnp.concatenate([w_in[:, :640], zeros, zeros, w_in[:, 640:IN_A], zeros], axis=1)
        p['w_b'] = w_in[:, IN_A:]
        p['w_uq'] = jnp.pad(mat('mla_w_uq'), ((0, 0), (0, 0), (0, MLA_DK - 96))).reshape(MLA_Q_RANK, -1)
        ukv = mat('mla_w_ukv')
        p['w_uk'] = jnp.pad(ukv[:, :, :MLA_NOPE], ((0, 0), (0, 0), (0, MLA_DK - MLA_NOPE))).reshape(MLA_KV_RANK, -1)
        p['w_uv'] = ukv[:, :, MLA_NOPE:].reshape(MLA_KV_RANK, -1)
        p['w_out'] = mat('w_out_ab')
        p['mix_norm'] = gains['mix_norm_ab'][i][None]
        p['q_norm'] = gains['mla_q_norm'][i][None]
        p['kv_norm'] = gains['mla_kv_norm'][i][None]
    else:
        p['w_q'], p['w_kv'], p['w_o'] = mat('gqa_w_q'), mat('gqa_w_kv'), mat('gqa_w_o')
        p['mix_norm'] = gains['mix_norm_c'][i][None]
        p['q_norm'] = jnp.tile(gains['gqa_q_norm'][i][None], (1, 2))
        p['k_norm'] = jnp.tile(gains['gqa_k_norm'][i][None], (1, 2))
    return p


def _even_fwd(x, p, cs, swap, tag):
    xn = _rms_fwd(x, p['mix_norm'], tag + "_norm")
    za = _mm(xn, p['w_a'], name=tag + "_in_a")
    zb = _mm(xn, p['w_b'], out_dtype=BF16, name=tag + "_in_b")
    cq, ckv, krr = _mla_prep(za, p['q_norm'], p['kv_norm'], cs, swap, tag + "_mla_prep")
    q_raw = _mm(cq, p['w_uq'], name=tag + "_uq")
    k_pad = _mm(ckv, p['w_uk'], name=tag + "_uk")
    v = _mm(ckv, p['w_uv'], out_dtype=BF16, name=tag + "_uv")
    qh, kh = _mla_qk(q_raw, k_pad, krr, cs, swap, tag + "_mla_qk")
    o_a, lse_a = _flash_fwd(qh, kh, v, name=tag + "_mla_attn", **dict(_MLA, bq=_MLA_FWD_BQ))
    og, lg = [], []
    for grp in range(DIL_GROUPS):
        o, l = _dil_fwd(zb, grp, f"{tag}_dil{grp}")
        og.append(o)
        lg.append(l)
    o_b, lt = _dil_combine(og, lg, tag + "_dil_merge")
    ocat = jnp.concatenate([o_a, o_b], axis=1)
    x1 = _mm(ocat, p['w_out'], add=x, name=tag + "_out")
    saved = dict(x=x, xn=xn, za=za, zb=zb, cq=cq, ckv=ckv, qh=qh, kh=kh, v=v, lse_a=lse_a, og=og, lg=lg, lt=lt,
                 ocat=ocat)
    return x1, saved


def _even_bwd(dx1, p, sv, cs, swap, seg64, tag):
    docat = _mm(dx1, p['w_out'], mode="nt", name=tag + "_out_dx")
    d_w_out = _mm(sv['ocat'], dx1, mode="tn", name=tag + "_out_dw")
    n_a = MLA_HEADS * MLA_V
    do_a = docat[:, :n_a].astype(BF16)
    res = _dil_combine_bwd(docat[:, n_a:], sv['og'], sv['lg'], sv['lt'], seg64, tag + "_dil_merge_bwd")
    dqs, dks, dvs = [], [], []
    for grp in range(DIL_GROUPS):
        dq, dk, dv = _dil_bwd(sv['zb'], res[grp], sv['lg'][grp], res[3 + grp], grp, f"{tag}_dil{grp}_bwd")
        dqs.append(dq)
        dks.append(dk)
        dvs.append(dv)
    dzb = jnp.concatenate(dqs + dks + dvs, axis=1).astype(BF16)
    dqh, dkh, dv = _flash_bwd(sv['qh'], sv['kh'], sv['v'], sv['ocat'][:, :n_a], do_a, sv['lse_a'],
                              name=tag + "_mla_attn_bwd", unroll=_BWD_UNROLL, **_MLA)
    dq_raw, dkh, dkrr = _mla_qk_bwd(dqh, dkh, cs, swap, tag + "_mla_qk_bwd")
    dcq = _mm(dq_raw, p['w_uq'], mode="nt", name=tag + "_uq_dx")
    d_w_uq = _mm(sv['cq'], dq_raw, mode="tn", name=tag + "_uq_dw")
    dckv = _mm(dkh, p['w_uk'], mode="nt", name=tag + "_uk_dx")
    dckv = _mm(dv, p['w_uv'], mode="nt", add=dckv, name=tag + "_uv_dx")
    d_w_uk = _mm(sv['ckv'], dkh, mode="tn", name=tag + "_uk_dw")
    d_w_uv = _mm(sv['ckv'], dv, mode="tn", name=tag + "_uv_dw")
    dza, d_gq, d_gkv = _mla_prep_bwd(sv['za'], cs, dcq, dckv, dkrr, p['q_norm'], p['kv_norm'], swap,
                                     tag + "_mla_prep_bwd")
    dxn = _mm(dza, p['w_a'], mode="nt", name=tag + "_in_a_dx")
    dxn = _mm(dzb, p['w_b'], mode="nt", add=dxn, name=tag + "_in_b_dx")
    d_w_a = _mm(sv['xn'], dza, mode="tn", name=tag + "_in_a_dw")
    d_w_b = _mm(sv['xn'], dzb, mode="tn", name=tag + "_in_b_dw")
    dx, d_g = _rms_bwd(sv['x'], p['mix_norm'], dxn, dx1, tag + "_norm_bwd")
    d_w_in = jnp.concatenate([d_w_a[:, :640], d_w_a[:, 704:736], d_w_b], axis=1)
    d_uq = d_w_uq.reshape(MLA_Q_RANK, MLA_HEADS, MLA_DK)[:, :, :MLA_NOPE + MLA_ROPE]
    d_ukv = jnp.concatenate([d_w_uk.reshape(MLA_KV_RANK, MLA_HEADS, MLA_DK)[:, :, :MLA_NOPE],
                             d_w_uv.reshape(MLA_KV_RANK, MLA_HEADS, MLA_V)], axis=2)
    grads = dict(w_in_ab=d_w_in, mla_w_uq=d_uq, mla_w_ukv=d_ukv, w_out_ab=d_w_out, mix_norm_ab=d_g[0],
                 mla_q_norm=d_gq[0], mla_kv_norm=d_gkv[0])
    return dx, grads


def _odd_fwd(x, p, cs, seg64, swap, tag):
    xn = _rms_fwd(x, p['mix_norm'], tag + "_norm")
    q_raw = _mm(xn, p['w_q'], name=tag + "_q")
    kv_raw = _mm(xn, p['w_kv'], name=tag + "_kv")
    qh, kh, v = _gqa_prep(q_raw, kv_raw, cs, p['q_norm'], p['k_norm'], seg64, swap, tag + "_gqa_prep")
    o, lse = _flash_fwd(qh, kh, v, name=tag + "_gqa_attn", **_GQA)
    x1 = _mm(o, p['w_o'], add=x, name=tag + "_o")
    return x1, dict(x=x, xn=xn, q_raw=q_raw, kv_raw=kv_raw, qh=qh, kh=kh, v=v, o=o, lse=lse)


def _odd_bwd(dx1, p, sv, cs, seg64, swap, tag):
    do = _mm(dx1, p['w_o'], mode="nt", out_dtype=BF16, name=tag + "_o_dx")
    d_w_o = _mm(sv['o'], dx1, mode="tn", name=tag + "_o_dw")
    dqh, dkh, dv = _flash_bwd(sv['qh'], sv['kh'], sv['v'], sv['o'], do, sv['lse'], name=tag + "_gqa_attn_bwd",
                              unroll=_BWD_UNROLL, **_GQA)
    dq_raw, dkv_raw, d_gq, d_gk = _gqa_prep_bwd(sv['q_raw'], sv['kv_raw'], cs, dqh, dkh, dv, p['q_norm'],
                                                p['k_norm'], seg64, swap, tag + "_gqa_prep_bwd")
    dxn = _mm(dq_raw, p['w_q'], mode="nt", name=tag + "_q_dx")
    dxn = _mm(dkv_raw, p['w_kv'], mode="nt", add=dxn, name=tag + "_kv_dx")
    d_w_q = _mm(sv['xn'], dq_raw, mode="tn", name=tag + "_q_dw")
    d_w_kv = _mm(sv['xn'], dkv_raw, mode="tn", name=tag + "_kv_dw")
    dx, d_g = _rms_bwd(sv['x'], p['mix_norm'], dxn, dx1, tag + "_norm_bwd")
    grads = dict(gqa_w_q=d_w_q, gqa_w_kv=d_w_kv, gqa_w_o=d_w_o, mix_norm_c=d_g[0],
                 gqa_q_norm=d_gq[0, :GQA_HEAD_DIM] + d_gq[0, GQA_HEAD_DIM:],
                 gqa_k_norm=d_gk[0, :GQA_HEAD_DIM] + d_gk[0, GQA_HEAD_DIM:])
    return dx, grads


def _ffn_fwd(x, p, tag):
    xn = _rms_fwd(x, p['ffn_norm'], tag + "_ffn_norm")
    h = _mm(xn, p['ffn_w_in'], out_dtype=BF16, name=tag + "_ffn_in")
    x2 = _mm(h, p['ffn_w_out'], gated=True, add=x, name=tag + "_ffn_out")
    return x2, dict(x=x, xn=xn, h=h)


def _ffn_bwd(dx2, p, sv, tag):
    d_gate, d_up = _mm(dx2, p['ffn_w_out'], mode="nt", gate_up=sv['h'], out_dtype=BF16, name=tag + "_ffn_out_dx")
    d_w_out = _mm(sv['h'], dx2, mode="tn", gated=True, name=tag + "_ffn_out_dw")
    d_w_in = jnp.concatenate([_mm(sv['xn'], d_gate, mode="tn", name=tag + "_ffn_in_dw_gate"),
                              _mm(sv['xn'], d_up, mode="tn", name=tag + "_ffn_in_dw_up")], axis=1)
    dxn = _mm(d_gate, p['ffn_w_in'], mode="nt", b_part=(0, 2), name=tag + "_ffn_in_dx_gate")
    dxn = _mm(d_up, p['ffn_w_in'], mode="nt", b_part=(1, 2), add=dxn, name=tag + "_ffn_in_dx_up")
    dx, d_g = _rms_bwd(sv['x'], p['ffn_norm'], dxn, dx2, tag + "_ffn_norm_bwd")
    return dx, d_w_in, d_w_out, d_g[0]


EVEN_MATS = ('w_in_ab', 'mla_w_uq', 'mla_w_ukv', 'w_out_ab')
ODD_MATS = ('gqa_w_q', 'gqa_w_kv', 'gqa_w_o')
FFN_MATS = ('ffn_w_in', 'ffn_w_out')


def _schedule(x, target, gains, full_of_pack, rest_grads_ready):
    s = x.shape[0]
    cs_mla, cs_gqa = _rope_tables(s)
    swap, seg64 = _swap_matrix(), _seg_matrix(GQA_HEAD_DIM)
    params, saved, full = [], [], None
    for layer in range(4):
        tag = f"l{layer}"
        if layer < 2:
            full = full_of_pack(layer, x)
        p = _layer_params(layer, full, gains)
        if layer % 2 == 0:
            x, sv = _even_fwd(x, p, cs_mla, swap, tag)
        else:
            x, sv = _odd_fwd(x, p, cs_gqa, seg64, swap, tag)
        x, sv_f = _ffn_fwd(x, p, tag)
        params.append(p)
        saved.append((sv, sv_f))
    dx, loss_row, d_final = _loss_head(x, target, gains['final_norm'][None], "loss_head")

    per_layer, rest = {}, None
    for layer in reversed(range(4)):
        p, (sv, sv_f), tag = params[layer], saved[layer], f"l{layer}"
        if layer == 0:
            rest = {n: per_layer[2][n][None] for n in EVEN_MATS}
            rest.update({n: jnp.stack([per_layer[1][n], per_layer[3][n]], axis=0) for n in ODD_MATS})
            rest.update({n: jnp.stack([per_layer[l][n] for l in (1, 2, 3)], axis=0) for n in FFN_MATS})
            token = rest_grads_ready(rest)
            if token is not None:
                p = dict(p, ffn_w_out=p['ffn_w_out'] + token[0, 0].astype(p['ffn_w_out'].dtype))
        dx, d_ffn_in, d_ffn_out, d_ffn_g = _ffn_bwd(dx, p, sv_f, tag)
        if layer % 2 == 0:
            dx, g = _even_bwd(dx, p, sv, cs_mla, swap, seg64, tag)
        else:
            dx, g = _odd_bwd(dx, p, sv, cs_gqa, seg64, swap, tag)
        g.update(ffn_w_in=d_ffn_in, ffn_w_out=d_ffn_out, ffn_norm=d_ffn_g)
        per_layer[layer] = g

    first = {n: per_layer[0][n][None] for n in EVEN_MATS + FFN_MATS}
    small = {'final_norm': d_final[0], 'ffn_norm': jnp.stack([per_layer[l]['ffn_norm'] for l in range(4)], axis=0)}
    for n in ('mix_norm_ab', 'mla_q_norm', 'mla_kv_norm'):
        small[n] = jnp.stack([per_layer[0][n], per_layer[2][n]], axis=0)
    for n in ('mix_norm_c', 'gqa_q_norm', 'gqa_k_norm'):
        small[n] = jnp.stack([per_layer[1][n], per_layer[3][n]], axis=0)
    return loss_row, dx, first, rest, small


def _core_sums(grads, pack, core, tag):
    packed = _pack_full(grads, F32, pack)
    other = _sibling_swap_halves(packed, tag)
    return packed, other, _chip_sum(packed, other, core, tag)


def _finish_reduce_scatter(packed, other, recv, pack, chip, core, tag):
    return _unpack_blocks(_join_halves(_final_sum(packed, other, recv, chip, core, tag), core, tag), pack)


def _step(x, target, w, m, v):
    big_names = [n for n, _, _ in BIG]
    chip = 2 * lax.axis_index("x") + lax.axis_index("y")
    core = lax.axis_index("c")

    gains = {n: w[n] for n, _ in SMALL if n != 'mix_norm_c'}
    c_cols = w['mix_norm_c'].shape[1]
    own_c = lax.dynamic_update_slice(jnp.zeros((2, 4 * c_cols), F32), w['mix_norm_c'], (0, chip * c_cols))
    gains['mix_norm_c'] = _unpack_small(_all_reduce_packet(_pack_small(
        {'mix_norm_c': own_c * 0.5}, jnp.zeros((LANES,), F32))))[1]['mix_norm_c']

    gathered0 = _all_gather_weights(_pack_blocks(w, BF16, 0))
    packed1, gathered0 = lax.optimization_barrier((_pack_blocks(w, BF16, 1), gathered0))
    ag_send, ag_recv, p_thru, g_thru, ag_token = _gather_start(packed1, "gather_rest")
    gains['mix_norm_ab'] = gains['mix_norm_ab'] + ag_token[0, 0]
    full0 = _unpack_gathered(gathered0, 0)

    def full_of_pack(pack, after):
        if pack == 0:
            return full0
        landed = _gather_wait(ag_send, ag_recv, p_thru, g_thru, after, "gather_rest")
        return _unpack_gathered(landed, 1, own=packed1, chip=chip)

    rs = {}

    def rest_grads_ready(rest):
        rs['packed'], rs['other'], sums = _core_sums(rest, 1, core, "grad_rest")
        rs['send'], rs['recv'], rs['t'], rs['b'], token = _send_chip_sums_start(sums, "grad_rest")
        return token

    loss_row, dx, first, rest, small = _schedule(x[0], target[0], gains, full_of_pack, rest_grads_ready)
    recv1 = _send_chip_sums_wait(rs['send'], rs['recv'], rs['t'], rs['b'], dx, "grad_rest")
    g_rest = _finish_reduce_scatter(rs['packed'], rs['other'], recv1, 1, chip, core, "grad_rest")
    packed0, other0, sums0 = _core_sums(first, 0, core, "grad_first")
    g_first = _finish_reduce_scatter(packed0, other0, _send_chip_sums(sums0, "grad_first"), 0, chip, core,
                                     "grad_first")
    g_blocks = {n: (jnp.concatenate([g_first[n], g_rest[n]], axis=0) if n in g_first else g_rest[n])
                for n in big_names}

    loss, g_small = _unpack_small(_all_reduce_packet(_pack_small(small, loss_row[0])))
    g_small['mix_norm_c'] = lax.dynamic_slice(g_small['mix_norm_c'], (0, chip * c_cols), (2, c_cols))

    out_g, out_d, out_m, out_v = {}, {}, {}, {}
    for n in big_names:
        shape = w[n].shape
        cols = shape[-1]
        d_, m_, v_ = _adamw(w[n].reshape(-1, cols), g_blocks[n].reshape(-1, cols), m[n].reshape(-1, cols),
                            v[n].reshape(-1, cols), "adamw_" + n)
        out_g[n], out_d[n], out_m[n], out_v[n] = g_blocks[n], d_.reshape(shape), m_.reshape(shape), v_.reshape(shape)
    for n, _ in SMALL:
        shape = w[n].shape
        as2d = (lambda t: t.reshape(1, -1)) if len(shape) == 1 else (lambda t: t)
        d_, m_, v_ = _adamw(as2d(w[n]), as2d(g_small[n]), as2d(m[n]), as2d(v[n]), "adamw_" + n)
        out_g[n], out_d[n], out_m[n], out_v[n] = g_small[n], d_.reshape(shape), m_.reshape(shape), v_.reshape(shape)
    return (loss, dx[None], *[out_g[n] for n in WEIGHTS], *[out_d[n] for n in WEIGHTS],
            *[out_m[n] for n in WEIGHTS], *[out_v[n] for n in WEIGHTS])


def kernel(x, mix_norm_ab, w_in_ab, mla_q_norm, mla_kv_norm, mla_w_uq, mla_w_ukv, w_out_ab, mix_norm_c, gqa_w_q, gqa_w_kv, gqa_q_norm, gqa_k_norm, gqa_w_o, ffn_norm, ffn_w_in, ffn_w_out, final_norm, loss_target, m_mix_norm_ab, m_w_in_ab, m_mla_q_norm, m_mla_kv_norm, m_mla_w_uq, m_mla_w_ukv, m_w_out_ab, m_mix_norm_c, m_gqa_w_q, m_gqa_w_kv, m_gqa_q_norm, m_gqa_k_norm, m_gqa_w_o, m_ffn_norm, m_ffn_w_in, m_ffn_w_out, m_final_norm, v_mix_norm_ab, v_w_in_ab, v_mla_q_norm, v_mla_kv_norm, v_mla_w_uq, v_mla_w_ukv, v_w_out_ab, v_mix_norm_c, v_gqa_w_q, v_gqa_w_kv, v_gqa_q_norm, v_gqa_k_norm, v_gqa_w_o, v_ffn_norm, v_ffn_w_in, v_ffn_w_out, v_final_norm):
    w = dict(zip(WEIGHTS, (mix_norm_ab, w_in_ab, mla_q_norm, mla_kv_norm, mla_w_uq, mla_w_ukv, w_out_ab, mix_norm_c,
                           gqa_w_q, gqa_w_kv, gqa_q_norm, gqa_k_norm, gqa_w_o, ffn_norm, ffn_w_in, ffn_w_out,
                           final_norm)))
    m = dict(zip(WEIGHTS, (m_mix_norm_ab, m_w_in_ab, m_mla_q_norm, m_mla_kv_norm, m_mla_w_uq, m_mla_w_ukv,
                           m_w_out_ab, m_mix_norm_c, m_gqa_w_q, m_gqa_w_kv, m_gqa_q_norm, m_gqa_k_norm, m_gqa_w_o,
                           m_ffn_norm, m_ffn_w_in, m_ffn_w_out, m_final_norm)))
    v = dict(zip(WEIGHTS, (v_mix_norm_ab, v_w_in_ab, v_mla_q_norm, v_mla_kv_norm, v_mla_w_uq, v_mla_w_ukv,
                           v_w_out_ab, v_mix_norm_c, v_gqa_w_q, v_gqa_w_kv, v_gqa_q_norm, v_gqa_k_norm, v_gqa_w_o,
                           v_ffn_norm, v_ffn_w_in, v_ffn_w_out, v_final_norm)))
    return _step(x, loss_target, w, m, v)
```

```python
import math

import numpy as np
import jax
import jax.numpy as jnp
from jax import lax
from jax.experimental import pallas as pl
from jax.experimental.pallas import tpu as pltpu

F32 = jnp.float32
BF16 = jnp.bfloat16
MESH = pl.DeviceIdType.MESH

VMEM_LIMIT_BYTES = 56 * 1024 * 1024
LANES = 128

D_MODEL = 1024
NORM_EPS = 1e-6
ROPE_THETA = 10000.0
NEG_INF = -1e30
GRID_W = 64

MLA_HEADS, MLA_Q_RANK, MLA_KV_RANK, MLA_NOPE, MLA_ROPE, MLA_V = 8, 384, 256, 64, 32, 64
MLA_DK = 128
DIL_PAIRS = ((128, 1), (512, 4), (2048, 16))
DIL_HALF, DIL_SLOTS, DIL_GROUPS, DIL_HEAD_DIM = 64, 4, 3, 64
DIL_HEADS = DIL_SLOTS * DIL_GROUPS
DIL_W = DIL_SLOTS * DIL_HEAD_DIM
GQA_HEADS, GQA_KV_HEADS, GQA_HEAD_DIM = 16, 4, 64
FFN_HIDDEN = 2816
IN_A = MLA_Q_RANK + MLA_KV_RANK + MLA_ROPE
IN_A_PAD = 768
IN_B = 3 * DIL_HEADS * DIL_HEAD_DIM

ADAM_LR, ADAM_B1, ADAM_B2, ADAM_EPS, ADAM_WD, ADAM_STEP = 0.001, 0.9, 0.999, 1e-08, 0.01, 10

LOG2E, LN2 = math.log2(math.e), math.log(2.0)
MLA_SCALE = (MLA_NOPE + MLA_ROPE) ** -0.5
GQA_SCALE = GQA_HEAD_DIM ** -0.5

WEIGHTS = ['mix_norm_ab', 'w_in_ab', 'mla_q_norm', 'mla_kv_norm', 'mla_w_uq', 'mla_w_ukv', 'w_out_ab', 'mix_norm_c',
           'gqa_w_q', 'gqa_w_kv', 'gqa_q_norm', 'gqa_k_norm', 'gqa_w_o', 'ffn_norm', 'ffn_w_in', 'ffn_w_out',
           'final_norm']
BIG = (('w_in_ab', (2, 1024, 744), 2), ('mla_w_uq', (2, 96, 8, 96), 1), ('mla_w_ukv', (2, 64, 8, 128), 1),
       ('w_out_ab', (2, 768, 256), 2), ('gqa_w_q', (2, 256, 1024), 1), ('gqa_w_kv', (2, 256, 512), 1),
       ('gqa_w_o', (2, 256, 1024), 1), ('ffn_w_in', (4, 1024, 1408), 2), ('ffn_w_out', (4, 704, 1024), 1))
PACK_W = 1024
SMALL = (('mix_norm_ab', (2, 1024)), ('mla_q_norm', (2, 384)), ('mla_kv_norm', (2, 256)), ('gqa_q_norm', (2, 64)),
         ('gqa_k_norm', (2, 64)), ('ffn_norm', (4, 1024)), ('final_norm', (1024,)), ('mix_norm_c', (2, 1024)))
PACKET_ROWS = 88


def _cparams(sem=None):
    return pltpu.CompilerParams(dimension_semantics=sem, vmem_limit_bytes=VMEM_LIMIT_BYTES)


def _pick(n, pref, mult=LANES):
    if n <= pref:
        return n
    for d in range(pref - pref % mult, 0, -mult):
        if n % d == 0:
            return d
    return n


_DIMS = {"nn": (((1,), (0,)), ((), ())), "nt": (((1,), (1,)), ((), ())), "tn": (((0,), (0,)), ((), ()))}


def _sigmoid(x):
    return 0.5 * (1.0 + jnp.tanh(0.5 * x))


def _silu_mul(gate, up):
    gate, up = gate.astype(F32), up.astype(F32)
    return gate * _sigmoid(gate) * up


def _silu_mul_bwd(gate, up, da):
    gate, up = gate.astype(F32), up.astype(F32)
    sig = _sigmoid(gate)
    silu = gate * sig
    return da * up * (sig + silu * (1.0 - sig)), da * silu


def _mm(a, b, *, mode="nn", add=None, out_dtype=F32, gated=False, gate_up=None, b_part=None, norm_gain=None,
        name="mm"):
    if mode == "nn":
        (m, k), (k2, n) = a.shape, b.shape
    elif mode == "nt":
        (m, k), (n, k2) = a.shape, b.shape
    else:
        (k, m), (k2, n) = a.shape, b.shape
    if gated:
        k, m = (k // 2, m) if mode == "nn" else (k, m // 2)
    if b_part is not None:
        assert mode == "nt" and k2 == k * b_part[1]
        k2 = k
    assert k == k2, (a.shape, b.shape, mode)
    if mode == "tn":
        deep = a.dtype == BF16 and b.dtype == BF16
        bm, bn, bk = _pick(m, 1408), _pick(n, 1024), _pick(k, 2048 if deep else 1024, 16)
    else:
        bm, bn, bk = _pick(m, 512, 16), _pick(n, 1408), _pick(k, 2816)
    nk = k // bk
    assert m % bm == 0 and n % bn == 0 and k % bk == 0
    has_add, has_gu = add is not None, gate_up is not None
    assert not (has_add and has_gu) and not (gated and mode == "nt") and not (has_gu and mode != "nt")
    has_norm = norm_gain is not None
    assert not has_norm or (bn == n and not has_gu)
    n_in = 2 + int(gated) + int(has_add) + 2 * int(has_gu) + int(has_norm)

    def body(*refs):
        a_val = _silu_mul(refs[0][...], refs[1][...]) if gated else refs[0][...]
        b_ref = refs[1 + int(gated)]
        part = lax.dot_general(a_val.astype(BF16), b_ref[...].astype(BF16), _DIMS[mode],
                               preferred_element_type=F32)

        def finish(r):
            if has_gu:
                d_gate, d_up = _silu_mul_bwd(refs[2][...], refs[3][...], r)
                refs[n_in][...] = d_gate.astype(refs[n_in].dtype)
                refs[n_in + 1][...] = d_up.astype(refs[n_in + 1].dtype)
                return
            if has_add:
                r = r + refs[n_in - 1 - int(has_norm)][...]
            refs[n_in][...] = r.astype(refs[n_in].dtype)
            if has_norm:
                refs[n_in + 1][...] = _rms(r, refs[n_in - 1][...]).astype(BF16)

        if nk == 1:
            finish(part)
        else:
            acc_ref = refs[-1]
            kk = pl.program_id(2)

            @pl.when(kk == 0)
            def _():
                acc_ref[...] = part

            @pl.when(kk > 0)
            def _():
                acc_ref[...] += part

            @pl.when(kk == nk - 1)
            def _():
                finish(acc_ref[...])

    a_bytes, b_bytes = a.size * a.dtype.itemsize, b.size * b.dtype.itemsize
    n_outer = nk == 1 and (n // bn) * a_bytes + b_bytes < a_bytes + (m // bm) * b_bytes

    def at(f):
        return (lambda j, i, kk: f(i, j, kk)) if n_outer else f

    if mode == "nn":
        a_specs = [pl.BlockSpec((bm, bk), at(lambda i, j, kk, o=o: (i, kk + o))) for o in ((0, nk) if gated else (0,))]
        b_spec = pl.BlockSpec((bk, bn), at(lambda i, j, kk: (kk, j)))
    elif mode == "nt":
        a_specs = [pl.BlockSpec((bm, bk), at(lambda i, j, kk: (i, kk)))]
        b_off = 0 if b_part is None else b_part[0] * nk
        b_spec = pl.BlockSpec((bn, bk), at(lambda i, j, kk: (j, kk + b_off)))
    else:
        a_specs = [pl.BlockSpec((bk, bm), at(lambda i, j, kk, o=o: (kk, i + o)))
                   for o in ((0, m // bm) if gated else (0,))]
        b_spec = pl.BlockSpec((bk, bn), at(lambda i, j, kk: (kk, j)))
    o_spec = pl.BlockSpec((bm, bn), at(lambda i, j, kk: (i, j)))
    in_specs, args = a_specs + [b_spec], [a] * len(a_specs) + [b]
    if has_add:
        in_specs, args = in_specs + [o_spec], args + [add]
    if has_gu:
        in_specs += [o_spec, pl.BlockSpec((bm, bn), at(lambda i, j, kk: (i, j + n // bn)))]
        args += [gate_up, gate_up]
    if has_norm:
        in_specs, args = in_specs + [pl.BlockSpec((1, bn), at(lambda i, j, kk: (0, j)))], args + [norm_gain]
    out = jax.ShapeDtypeStruct((m, n), out_dtype)
    two = has_gu or has_norm
    grid = (n // bn, m // bm, nk) if n_outer else (m // bm, n // bn, nk)
    return pl.pallas_call(
        body, name=name, grid=grid, in_specs=in_specs, out_specs=[o_spec, o_spec] if two else o_spec,
        out_shape=[out, jax.ShapeDtypeStruct((m, n), BF16)] if has_norm else [out, out] if has_gu else out,
        scratch_shapes=[pltpu.VMEM((bm, bn), F32)] if nk > 1 else [],
        compiler_params=_cparams(("parallel", "parallel", "arbitrary")),
    )(*args)


def _rows_call(fn, rows, consts, out_rows, out_accs=(), *, bs=256, name):
    s = rows[0].shape[0]
    bs = min(bs, s)
    assert s % bs == 0
    nr, nc, no, na = len(rows), len(consts), len(out_rows), len(out_accs)

    def body(*refs):
        vals = [r[...] for r in refs[:nr + nc]]
        outs = refs[nr + nc:]
        res = fn(*vals)
        if not isinstance(res, (tuple, list)):
            res = (res,)
        assert len(res) == no + na, (len(res), no, na)
        for r, v in zip(outs[:no], res[:no]):
            r[...] = v.astype(r.dtype)
        if na:
            i = pl.program_id(0)
            for r, v in zip(outs[no:], res[no:]):
                @pl.when(i == 0)
                def _(r=r, v=v):
                    r[...] = v

                @pl.when(i > 0)
                def _(r=r, v=v):
                    r[...] += v

    in_specs = [pl.BlockSpec((bs, a.shape[1]), lambda i: (i, 0)) for a in rows]
    in_specs += [pl.BlockSpec(c.shape, lambda i: (0, 0)) for c in consts]
    out_specs = [pl.BlockSpec((bs, c), lambda i: (i, 0)) for c, _ in out_rows]
    out_specs += [pl.BlockSpec(tuple(sh), lambda i: (0, 0)) for sh in out_accs]
    out_shape = [jax.ShapeDtypeStruct((s, c), dt) for c, dt in out_rows]
    out_shape += [jax.ShapeDtypeStruct(tuple(sh), F32) for sh in out_accs]
    res = pl.pallas_call(
        body, name=name, grid=(s // bs,), in_specs=in_specs, out_specs=out_specs, out_shape=out_shape,
        compiler_params=_cparams(("arbitrary",) if na else ("parallel",)),
    )(*rows, *consts)
    return res


def _rms(x, g):
    return x * lax.rsqrt(jnp.mean(x * x, axis=-1, keepdims=True) + NORM_EPS) * g


def _rms_bwd_math(x, g, dy):
    r = lax.rsqrt(jnp.mean(x * x, axis=-1, keepdims=True) + NORM_EPS)
    u = dy * g
    dx = r * u - x * (r * r * r) * jnp.mean(u * x, axis=-1, keepdims=True)
    dg = jnp.sum(dy * x * r, axis=0, keepdims=True)
    return dx, dg


NORM_ROWS = 512


def _rms_fwd(x, g, name):
    return _rows_call(lambda xv, gv: _rms(xv, gv), [x], [g], [(x.shape[1], BF16)], bs=NORM_ROWS, name=name)[0]


def _rms_bwd(x, g, dy, dres, name):
    def fn(xv, dyv, dresv, gv):
        dx, dg = _rms_bwd_math(xv, gv, dyv.astype(F32))
        return dx + dresv, dg
    return _rows_call(fn, [x, dy, dres], [g], [(x.shape[1], F32)], [(1, x.shape[1])], bs=NORM_ROWS, name=name)


def _chunkdot(x, m):
    outs = [jnp.dot(x[:, c:c + LANES], m, precision=lax.Precision.HIGHEST, preferred_element_type=F32)
            for c in range(0, x.shape[1], LANES)]
    return outs[0] if len(outs) == 1 else jnp.concatenate(outs, axis=1)


def _lanes(t, width):
    n = width // LANES
    return t if n == 1 else jnp.concatenate([t] * n, axis=1)


def _rope(x, cs, swap):
    w = x.shape[1]
    return x * _lanes(cs[:, :LANES], w) + _chunkdot(x, swap) * _lanes(cs[:, LANES:], w)


def _rope_t(dy, cs, swap):
    w = dy.shape[1]
    return dy * _lanes(cs[:, :LANES], w) + _chunkdot(dy * _lanes(cs[:, LANES:], w), swap)


def _swap_matrix():
    m = np.zeros((LANES, LANES), np.float32)
    for j in range(LANES):
        src = j + 16 if (j % 32) < 16 else j - 16
        m[src, j] = 1.0
    return jnp.asarray(m)


def _seg_matrix(seg):
    idx = np.arange(LANES) // seg
    return jnp.asarray((idx[:, None] == idx[None, :]).astype(np.float32))


def _rope_tables(s):
    pos = jnp.arange(s)

    def angles(p, dim):
        freqs = ROPE_THETA ** (-jnp.arange(0, dim, 2, dtype=F32) / dim)
        ang = p.astype(F32)[:, None] * freqs[None, :]
        return jnp.cos(ang), jnp.sin(ang)

    cos_t, sin_t = angles(pos, MLA_ROPE)
    one, zero = jnp.ones((s, 64), F32), jnp.zeros((s, 64), F32)
    mla = jnp.concatenate([one, cos_t, cos_t, one[:, :32], zero, -sin_t, sin_t, zero[:, :32]], axis=1)
    cos_r, sin_r = angles(pos // GRID_W, GQA_HEAD_DIM // 2)
    cos_c, sin_c = angles(pos % GRID_W, GQA_HEAD_DIM // 2)
    c64 = jnp.concatenate([cos_r, cos_r, cos_c, cos_c], axis=1)
    s64 = jnp.concatenate([-sin_r, sin_r, -sin_c, sin_c], axis=1)
    gqa = jnp.concatenate([c64, c64, s64, s64], axis=1)
    return mla, gqa


def _stack_heads(ref, heads, d, dtype=None):
    parts = [ref[:, hd * d:(hd + 1) * d] for hd in heads]
    out = parts[0] if len(parts) == 1 else jnp.concatenate(parts, axis=0)
    return out if dtype is None else out.astype(dtype)


def _fill_v_ones(v_ref, va_ref, hb, dv):
    @pl.when(pl.program_id(1) == 0)
    def _():
        ones = jnp.ones((v_ref.shape[0], dv), BF16)
        for h in range(hb):
            va_ref[:, 2 * h * dv:(2 * h + 1) * dv] = v_ref[:, h * dv:(h + 1) * dv]
            va_ref[:, (2 * h + 1) * dv:(2 * h + 2) * dv] = ones


def _flash_fwd(q, k, v, *, R, dk, dv, hb, bq, bk, name):
    s = q.shape[0]
    g = k.shape[1] // dk
    ng = g // hb
    bq, bk = min(bq, s), min(bk, s)
    nq, nkb = s // bq, s // bk
    rb = R * bq

    def body(q_ref, k_ref, v_ref, o_ref, lse_ref, va_ref):
        _fill_v_ones(v_ref, va_ref, hb, dv)
        head_sets = [[h * R + r for r in range(R)] for h in range(hb)]
        qss = [_stack_heads(q_ref, heads, dk) for heads in head_sets]

        def step(jj, carry):
            carry = list(carry)
            rows = [pl.ds(pl.multiple_of((jj * unroll + u) * bk, bk), bk) for u in range(unroll)]
            scs = [[lax.dot_general(qss[h], k_ref[rows[u], h * dk:(h + 1) * dk], _DIMS["nt"],
                                    preferred_element_type=F32) for h in range(hb)] for u in range(unroll)]
            for u in range(unroll):
                m2s = [jnp.maximum(carry[h][0], jnp.max(scs[u][h], axis=1, keepdims=True)) for h in range(hb)]
                ps = [jnp.exp2(scs[u][h] - m2s[h]).astype(BF16) for h in range(hb)]
                pvs = [jnp.dot(ps[h], va_ref[rows[u], 2 * h * dv:2 * (h + 1) * dv], preferred_element_type=F32)
                       for h in range(hb)]
                carry = [(m2s[h], jnp.exp2(carry[h][0] - m2s[h]) * carry[h][1] + pvs[h]) for h in range(hb)]
            return tuple(carry)

        unroll = 4 if nkb % 4 == 0 else 1
        init = tuple((jnp.full((rb, 1), NEG_INF, F32), jnp.zeros((rb, 2 * dv), F32)) for _ in range(hb))
        final = lax.fori_loop(0, nkb // unroll, step, init)
        for h, heads in enumerate(head_sets):
            m, acc = final[h]
            l = acc[:, dv:dv + 1]
            o = acc[:, :dv] / l
            lse = m + jnp.log2(l)
            for r, hd in enumerate(heads):
                o_ref[:, hd * dv:(hd + 1) * dv] = o[r * bq:(r + 1) * bq].astype(o_ref.dtype)
                lse_ref[0, :, hd:hd + 1] = lse[r * bq:(r + 1) * bq]

    return pl.pallas_call(
        body, name=name, grid=(ng, nq),
        in_specs=[pl.BlockSpec((bq, hb * R * dk), lambda gi, i: (i, gi)),
                  pl.BlockSpec((s, hb * dk), lambda gi, i: (0, gi)),
                  pl.BlockSpec((s, hb * dv), lambda gi, i: (0, gi))],
        out_specs=[pl.BlockSpec((bq, hb * R * dv), lambda gi, i: (i, gi)),
                   pl.BlockSpec((1, bq, hb * R), lambda gi, i: (gi, i, 0))],
        out_shape=[jax.ShapeDtypeStruct((s, g * R * dv), BF16), jax.ShapeDtypeStruct((ng, s, hb * R), F32)],
        scratch_shapes=[pltpu.VMEM((s, 2 * hb * dv), BF16)],
        compiler_params=_cparams(("parallel", "arbitrary")),
    )(q, k, v)


def _flash_bwd(q, k, v, o, do, lse, *, R, dk, dv, hb, bq, bk, name, unroll=1):
    s = q.shape[0]
    g = k.shape[1] // dk
    ng = g // hb
    bq, bk = min(bq, s), min(bk, s)
    nq, nkb = s // bq, s // bk
    rb = R * bq

    def body(q_ref, k_ref, v_ref, o_ref, do_ref, lse_ref, dq_ref, dkt_ref, dvt_ref, va_ref):
        @pl.when(pl.program_id(1) == 0)
        def _():
            dkt_ref[...] = jnp.zeros(dkt_ref.shape, F32)
            dvt_ref[...] = jnp.zeros(dvt_ref.shape, F32)

        _fill_v_ones(v_ref, va_ref, hb, dv)
        lane = lax.broadcasted_iota(jnp.int32, (rb, dv), 1)
        head_sets = [[h * R + r for r in range(R)] for h in range(hb)]
        q_t = jnp.transpose(q_ref[...].astype(F32)).astype(BF16)
        do_t = jnp.transpose(do_ref[...].astype(F32)).astype(BF16)

        def stack_t(t, heads, d):
            parts = [t[hd * d:(hd + 1) * d] for hd in heads]
            return parts[0] if len(parts) == 1 else jnp.concatenate(parts, axis=1)

        qss, qts, dots, dosas, lcols = [], [], [], [], []
        for heads in head_sets:
            dos = _stack_heads(do_ref, heads, dv, BF16)
            delta = jnp.sum(dos.astype(F32) * _stack_heads(o_ref, heads, dv, F32), axis=1, keepdims=True)
            hi = delta.astype(BF16).astype(F32)
            lo = delta - hi
            qss.append(_stack_heads(q_ref, heads, dk))
            qts.append(stack_t(q_t, heads, dk))
            dots.append(stack_t(do_t, heads, dv))
            dosas.append(jnp.concatenate(
                [dos, jnp.where(lane == 0, -hi, jnp.where(lane == 1, -lo, 0.0)).astype(BF16)], axis=1))
            cols = [lse_ref[0, :, hd:hd + 1] for hd in heads]
            lcols.append(cols[0] if R == 1 else jnp.concatenate(cols, axis=0))

        def step(jj, dqs):
            hs = range(hb)
            for u in range(unroll):
                j = jj * unroll + u
                r0 = pl.multiple_of(j * bk, bk)
                kjs = [k_ref[pl.ds(r0, bk), h * dk:(h + 1) * dk] for h in hs]
                vas = [va_ref[pl.ds(r0, bk), 2 * h * dv:2 * (h + 1) * dv] for h in hs]
                ss = [lax.dot_general(qss[h], kjs[h], _DIMS["nt"], preferred_element_type=F32) for h in hs]
                dps = [lax.dot_general(dosas[h], vas[h], _DIMS["nt"], preferred_element_type=F32) for h in hs]
                ps = [jnp.exp2(ss[h] - lcols[h]) for h in hs]
                pbs = [ps[h].astype(BF16) for h in hs]
                dss = [(ps[h] * dps[h]).astype(BF16) for h in hs]
                for h in hs:
                    dvt_ref[0, j, h * dv:(h + 1) * dv, :] += jnp.dot(dots[h], pbs[h], preferred_element_type=F32)
                for h in hs:
                    dkt_ref[0, j, h * dk:(h + 1) * dk, :] += jnp.dot(qts[h], dss[h], preferred_element_type=F32)
                dqs = tuple(dqs[h] + jnp.dot(dss[h], kjs[h], preferred_element_type=F32) for h in hs)
            return dqs

        assert nkb % unroll == 0
        dqs = lax.fori_loop(0, nkb // unroll, step, tuple(jnp.zeros((rb, dk), F32) for _ in range(hb)))
        for h, heads in enumerate(head_sets):
            for r, hd in enumerate(heads):
                dq_ref[:, hd * dk:(hd + 1) * dk] = dqs[h][r * bq:(r + 1) * bq]

    qspec = pl.BlockSpec((bq, hb * R * dk), lambda gi, i: (i, gi))
    ospec = pl.BlockSpec((bq, hb * R * dv), lambda gi, i: (i, gi))
    kspec = pl.BlockSpec((s, hb * dk), lambda gi, i: (0, gi))
    vspec = pl.BlockSpec((s, hb * dv), lambda gi, i: (0, gi))
    dq, dkt, dvt = pl.pallas_call(
        body, name=name, grid=(ng, nq),
        in_specs=[qspec, kspec, vspec, ospec, ospec, pl.BlockSpec((1, bq, hb * R), lambda gi, i: (gi, i, 0))],
        out_specs=[qspec, pl.BlockSpec((1, nkb, hb * dk, bk), lambda gi, i: (gi, 0, 0, 0)),
                   pl.BlockSpec((1, nkb, hb * dv, bk), lambda gi, i: (gi, 0, 0, 0))],
        out_shape=[jax.ShapeDtypeStruct((s, g * R * dk), F32), jax.ShapeDtypeStruct((ng, nkb, hb * dk, bk), F32),
                   jax.ShapeDtypeStruct((ng, nkb, hb * dv, bk), F32)],
        scratch_shapes=[pltpu.VMEM((s, 2 * hb * dv), BF16)],
        compiler_params=_cparams(("parallel", "arbitrary")),
    )(q, k, v, o, do, lse)
    return dq, _keys_first(dkt, name + "_dk"), _keys_first(dvt, name + "_dv")


def _keys_first(t, name):
    ng, nkb, f, bk = t.shape

    def body(t_ref, o_ref):
        for j in range(nkb):
            o_ref[j * bk:(j + 1) * bk, :] = jnp.transpose(t_ref[0, j])

    return pl.pallas_call(
        body, name=name, grid=(ng,),
        in_specs=[pl.BlockSpec((1, nkb, f, bk), lambda gi: (gi, 0, 0, 0))],
        out_specs=pl.BlockSpec((nkb * bk, f), lambda gi: (0, gi)),
        out_shape=jax.ShapeDtypeStruct((nkb * bk, ng * f), t.dtype),
        compiler_params=_cparams(("parallel",)),
    )(t)


DIL_T = 1024
DIL_P = DIL_HALF
DIL_NCOL = IN_B // DIL_W


DIL_BATCH = 8


def _alibi_slope(head):
    return float(2.0 ** (-8.0 * (head + 1) / DIL_HEADS))


def _slot(sl_i):
    return slice(sl_i * DIL_HEAD_DIM, (sl_i + 1) * DIL_HEAD_DIM)


def _halo_specs(d, col, s, t):
    h = DIL_P * d
    per, last = t // h, s // h - 1
    return [pl.BlockSpec((h, DIL_W), lambda c: (jnp.maximum(c * per - 1, 0), col)),
            pl.BlockSpec((t, DIL_W), lambda c: (c, col)),
            pl.BlockSpec((h, DIL_W), lambda c: (jnp.minimum((c + 1) * per, last), col))]


def _staging(rows):
    return tuple(pltpu.VMEM((rows, LANES), F32) for _ in range(DIL_W // LANES))


def _stage(buf, refs):
    off = 0
    for r in refs:
        val = r[...].astype(F32)
        for j in range(DIL_W // LANES):
            buf[j][off:off + r.shape[0], :] = val[:, j * LANES:(j + 1) * LANES]
        off += r.shape[0]


def _unstage(buf, ref):
    ref[...] = jnp.concatenate([half[...] for half in buf], axis=1).astype(ref.dtype)


def _sub_tiles(d, t):
    return [(b * DIL_P * d + r, b * DIL_P) for b in range(t // (DIL_P * d)) for r in range(d)]


def _rows(start, size, d):
    return pl.ds(start, size, stride=d) if d > 1 else pl.ds(start, size)


def _strided(buf, start, size, d):
    return jnp.concatenate([half[_rows(start, size, d), :] for half in buf], axis=1)


def _put_strided(buf, start, d, val):
    for j in range(DIL_W // LANES):
        buf[j][_rows(start, val.shape[0], d), :] = val[:, j * LANES:(j + 1) * LANES]


def _band(u0, length, d, queries_wide):
    if queries_wide:
        shape = (3 * DIL_P, DIL_P)
        wide = u0 - DIL_P + lax.broadcasted_iota(jnp.int32, shape, 0)
        narrow = u0 + lax.broadcasted_iota(jnp.int32, shape, 1)
    else:
        shape = (DIL_P, 3 * DIL_P)
        narrow = u0 + lax.broadcasted_iota(jnp.int32, shape, 0)
        wide = u0 - DIL_P + lax.broadcasted_iota(jnp.int32, shape, 1)
    rel = jnp.abs(wide - narrow)
    valid = (rel <= DIL_HALF) & (wide >= 0) & (wide < length)
    return valid, rel.astype(F32) * float(d)


def _dil_fwd(zb, grp, name):
    s = zb.shape[0]
    d = DIL_PAIRS[grp][1]
    t = min(DIL_T, s)
    h = DIL_P * d
    scale = DIL_HEAD_DIM ** -0.5

    def body(q_ref, kp, kc, kn, vp, vc, vn, o_ref, lse_ref, qbuf, kbuf, vbuf, obuf, lbuf):
        _stage(qbuf, (q_ref,))
        _stage(kbuf, (kp, kc, kn))
        _stage(vbuf, (vp, vc, vn))
        u_step = pl.program_id(0) * (t // d)
        tiles = _sub_tiles(d, t)
        for g0 in range(0, len(tiles), DIL_BATCH):
            batch = tiles[g0:g0 + DIL_BATCH]
            masks = [_band(u_step + u, s // d, d, False) for _, u in batch]
            qs = [_strided(qbuf, row, DIL_P, d).astype(BF16) for row, _ in batch]
            ks = [_strided(kbuf, row, 3 * DIL_P, d).astype(BF16) for row, _ in batch]
            vs = [_strided(vbuf, row, 3 * DIL_P, d).astype(BF16) for row, _ in batch]
            chains = [(i, sl_i) for i in range(len(batch)) for sl_i in range(DIL_SLOTS)]
            scs = [lax.dot_general(qs[i][:, _slot(sl_i)], ks[i][:, _slot(sl_i)], _DIMS["nt"],
                                   preferred_element_type=F32) for i, sl_i in chains]
            scs = [jnp.where(masks[i][0], sc * scale - _alibi_slope(grp * DIL_SLOTS + sl_i) * masks[i][1], NEG_INF)
                   for (i, sl_i), sc in zip(chains, scs)]
            ms = [jnp.max(sc, axis=1, keepdims=True) for sc in scs]
            es = [jnp.exp(sc - m) for sc, m in zip(scs, ms)]
            dens = [jnp.sum(e, axis=1, keepdims=True) for e in es]
            outs = [jnp.dot((e / den).astype(BF16), vs[i][:, _slot(sl_i)], preferred_element_type=F32)
                    for (i, sl_i), e, den in zip(chains, es, dens)]
            lses = [jnp.broadcast_to(m + jnp.log(den), (DIL_P, DIL_HEAD_DIM)) for m, den in zip(ms, dens)]
            for i, (row, _) in enumerate(batch):
                pick = slice(i * DIL_SLOTS, (i + 1) * DIL_SLOTS)
                _put_strided(obuf, row, d, jnp.concatenate(outs[pick], axis=1))
                _put_strided(lbuf, row, d, jnp.concatenate(lses[pick], axis=1))
        _unstage(obuf, o_ref)
        _unstage(lbuf, lse_ref)

    own = pl.BlockSpec((t, DIL_W), lambda c: (c, 0))
    return pl.pallas_call(
        body, name=name, grid=(s // t,),
        in_specs=[pl.BlockSpec((t, DIL_W), lambda c: (c, grp))] + _halo_specs(d, 3 + grp, s, t)
        + _halo_specs(d, 6 + grp, s, t),
        out_specs=[own, own], out_shape=[jax.ShapeDtypeStruct((s, DIL_W), F32)] * 2,
        scratch_shapes=[_staging(t), _staging(t + 2 * h), _staging(t + 2 * h), _staging(t), _staging(t)],
        compiler_params=_cparams(("parallel",)),
    )(zb, zb, zb, zb, zb, zb, zb)


def _dil_bwd(zb, do, lse, dl, grp, name):
    s = zb.shape[0]
    d = DIL_PAIRS[grp][1]
    t = min(DIL_T, s)
    h = DIL_P * d
    scale = DIL_HEAD_DIM ** -0.5

    def chain_grads(qs, ks, vs, dos, lses, dls, masks):
        chains = [(i, sl_i) for i in range(len(qs)) for sl_i in range(DIL_SLOTS)]
        scs = [lax.dot_general(qs[i][:, _slot(sl_i)], ks[i][:, _slot(sl_i)], _DIMS["nt"],
                               preferred_element_type=F32) for i, sl_i in chains]
        dps = [lax.dot_general(dos[i][:, _slot(sl_i)], vs[i][:, _slot(sl_i)], _DIMS["nt"],
                               preferred_element_type=F32) for i, sl_i in chains]
        ps = [jnp.exp(jnp.where(masks[i][0], sc * scale - _alibi_slope(grp * DIL_SLOTS + sl_i) * masks[i][1],
                                NEG_INF) - lses[i][:, sl_i * DIL_HEAD_DIM:sl_i * DIL_HEAD_DIM + 1])
              for (i, sl_i), sc in zip(chains, scs)]
        dss = [(p * (dp - dls[i][:, sl_i * DIL_HEAD_DIM:sl_i * DIL_HEAD_DIM + 1]) * scale).astype(BF16)
               for (i, sl_i), p, dp in zip(chains, ps, dps)]
        return chains, ps, dss

    def dq_body(q_ref, kp, kc, kn, vp, vc, vn, do_ref, lse_ref, dl_ref, dq_ref, qbuf, kbuf, vbuf, dobuf, lsebuf,
                dlbuf, obuf):
        _stage(qbuf, (q_ref,))
        _stage(dobuf, (do_ref,))
        _stage(lsebuf, (lse_ref,))
        _stage(dlbuf, (dl_ref,))
        _stage(kbuf, (kp, kc, kn))
        _stage(vbuf, (vp, vc, vn))
        u_step = pl.program_id(0) * (t // d)
        tiles = _sub_tiles(d, t)
        for g0 in range(0, len(tiles), DIL_BATCH):
            batch = tiles[g0:g0 + DIL_BATCH]
            masks = [_band(u_step + u, s // d, d, False) for _, u in batch]
            narrow = [[_strided(b, row, DIL_P, d) for row, _ in batch] for b in (qbuf, dobuf, lsebuf, dlbuf)]
            ks = [_strided(kbuf, row, 3 * DIL_P, d).astype(BF16) for row, _ in batch]
            vs = [_strided(vbuf, row, 3 * DIL_P, d).astype(BF16) for row, _ in batch]
            chains, _, dss = chain_grads([a.astype(BF16) for a in narrow[0]], ks, vs,
                                         [a.astype(BF16) for a in narrow[1]], narrow[2], narrow[3], masks)
            outs = [jnp.dot(ds, ks[i][:, _slot(sl_i)], preferred_element_type=F32)
                    for (i, sl_i), ds in zip(chains, dss)]
            for i, (row, _) in enumerate(batch):
                _put_strided(obuf, row, d, jnp.concatenate(outs[i * DIL_SLOTS:(i + 1) * DIL_SLOTS], axis=1))
        _unstage(obuf, dq_ref)

    def dkv_body(k_ref, v_ref, qp, qc, qn, dop, doc, don, lp, lc, ln, dlp, dlc, dln, dk_ref, dv_ref,
                 kbuf, vbuf, qbuf, dobuf, lsebuf, dlbuf, dkbuf, dvbuf):
        _stage(kbuf, (k_ref,))
        _stage(vbuf, (v_ref,))
        _stage(qbuf, (qp, qc, qn))
        _stage(dobuf, (dop, doc, don))
        _stage(lsebuf, (lp, lc, ln))
        _stage(dlbuf, (dlp, dlc, dln))
        u_step = pl.program_id(0) * (t // d)
        tiles = _sub_tiles(d, t)
        for g0 in range(0, len(tiles), DIL_BATCH):
            batch = tiles[g0:g0 + DIL_BATCH]
            masks = [_band(u_step + u, s // d, d, True) for _, u in batch]
            ks = [_strided(kbuf, row, DIL_P, d).astype(BF16) for row, _ in batch]
            vs = [_strided(vbuf, row, DIL_P, d).astype(BF16) for row, _ in batch]
            wide_ = [[_strided(b, row, 3 * DIL_P, d) for row, _ in batch] for b in (qbuf, dobuf, lsebuf, dlbuf)]
            qs, dos = [a.astype(BF16) for a in wide_[0]], [a.astype(BF16) for a in wide_[1]]
            chains, ps, dss = chain_grads(qs, ks, vs, dos, wide_[2], wide_[3], masks)
            dvs = [lax.dot_general(p.astype(BF16), dos[i][:, _slot(sl_i)], _DIMS["tn"], preferred_element_type=F32)
                   for (i, sl_i), p in zip(chains, ps)]
            dks = [lax.dot_general(ds, qs[i][:, _slot(sl_i)], _DIMS["tn"], preferred_element_type=F32)
                   for (i, sl_i), ds in zip(chains, dss)]
            for i, (row, _) in enumerate(batch):
                pick = slice(i * DIL_SLOTS, (i + 1) * DIL_SLOTS)
                _put_strided(dkbuf, row, d, jnp.concatenate(dks[pick], axis=1))
                _put_strided(dvbuf, row, d, jnp.concatenate(dvs[pick], axis=1))
        _unstage(dkbuf, dk_ref)
        _unstage(dvbuf, dv_ref)

    def zcur(col):
        return pl.BlockSpec((t, DIL_W), lambda c: (c, col))

    own = pl.BlockSpec((t, DIL_W), lambda c: (c, 0))
    out = jax.ShapeDtypeStruct((s, DIL_W), F32)
    tile, wide = _staging(t), _staging(t + 2 * h)
    dq = pl.pallas_call(
        dq_body, name=name + "_dq", grid=(s // t,),
        in_specs=[zcur(grp)] + _halo_specs(d, 3 + grp, s, t) + _halo_specs(d, 6 + grp, s, t) + [own, own, own],
        out_specs=own, out_shape=out, scratch_shapes=[tile, wide, wide, tile, tile, tile, tile],
        compiler_params=_cparams(("parallel",)),
    )(zb, zb, zb, zb, zb, zb, zb, do, lse, dl)
    own3 = _halo_specs(d, 0, s, t)
    dk, dv = pl.pallas_call(
        dkv_body, name=name + "_dkv", grid=(s // t,),
        in_specs=[zcur(3 + grp), zcur(6 + grp)] + _halo_specs(d, grp, s, t) + own3 + own3 + own3,
        out_specs=[own, own], out_shape=[out, out],
        scratch_shapes=[tile, tile, wide, wide, wide, wide, tile, tile],
        compiler_params=_cparams(("parallel",)),
    )(zb, zb, zb, zb, zb, do, do, do, lse, lse, lse, dl, dl, dl)
    return dq, dk, dv


def _dil_combine(os_, ls_, name):
    def fn(o0, o1, o2, l0, l1, l2):
        m = jnp.maximum(jnp.maximum(l0, l1), l2)
        e0, e1, e2 = jnp.exp(l0 - m), jnp.exp(l1 - m), jnp.exp(l2 - m)
        den = e0 + e1 + e2
        comb = (e0 / den) * o0 + (e1 / den) * o1 + (e2 / den) * o2
        return comb, m + jnp.log(den)
    return _rows_call(fn, list(os_) + list(ls_), [], [(DIL_W, BF16), (DIL_W, F32)], name=name)


def _dil_combine_bwd(dcomb, os_, ls_, lt, seg64, name):
    def fn(dc, o0, o1, o2, l0, l1, l2, ltv, seg):
        w = [jnp.exp(l - ltv) for l in (l0, l1, l2)]
        comb = w[0] * o0 + w[1] * o1 + w[2] * o2
        t = _chunkdot(dc * comb, seg)
        return [wg * dc for wg in w] + [wg * t for wg in w]
    return _rows_call(fn, [dcomb] + list(os_) + list(ls_) + [lt], [seg64],
                      [(DIL_W, BF16)] * 3 + [(DIL_W, F32)] * 3, name=name)


def _mla_prep(za, gq, gkv, cs, swap, name):
    def fn(z, csv, gqv, gkvv, sw):
        return (_rms(z[:, :MLA_Q_RANK], gqv), _rms(z[:, MLA_Q_RANK:640], gkvv), _rope(z[:, 640:], csv, sw))
    return _rows_call(fn, [za, cs], [gq, gkv, swap], [(MLA_Q_RANK, BF16), (MLA_KV_RANK, BF16), (LANES, F32)],
                      name=name)


def _mla_prep_bwd(za, cs, dcq, dckv, dkr, gq, gkv, swap, name):
    def fn(z, csv, dcqv, dckvv, dkrv, gqv, gkvv, sw):
        d1, dg1 = _rms_bwd_math(z[:, :MLA_Q_RANK], gqv, dcqv)
        d2, dg2 = _rms_bwd_math(z[:, MLA_Q_RANK:640], gkvv, dckvv)
        d3 = _rope_t(dkrv, csv, sw)
        return jnp.concatenate([d1, d2, d3], axis=1), dg1, dg2
    return _rows_call(fn, [za, cs, dcq, dckv, dkr], [gq, gkv, swap], [(IN_A_PAD, BF16)],
                      [(1, MLA_Q_RANK), (1, MLA_KV_RANK)], name=name)


def _mla_qk(q_raw, k_pad, krr, cs, swap, name):
    w = MLA_HEADS * MLA_DK

    def fn(qv, kv, krv, csv, sw):
        return _rope(qv, csv, sw) * (MLA_SCALE * LOG2E), kv + _lanes(krv, w)
    return _rows_call(fn, [q_raw, k_pad, krr, cs], [swap], [(w, BF16), (w, BF16)], name=name)


def _mla_qk_bwd(dqh, dkh, cs, swap, name):
    w = MLA_HEADS * MLA_DK

    def fn(dq, dk, csv, sw):
        dk = dk * LN2
        acc = dk[:, :LANES]
        for h in range(1, MLA_HEADS):
            acc = acc + dk[:, h * LANES:(h + 1) * LANES]
        lane = lax.broadcasted_iota(jnp.int32, acc.shape, 1)
        acc = jnp.where((lane >= MLA_NOPE) & (lane < MLA_NOPE + MLA_ROPE), acc, 0.0)
        return _rope_t(dq * MLA_SCALE, csv, sw), dk, acc
    return _rows_call(fn, [dqh, dkh, cs], [swap], [(w, BF16), (w, BF16), (LANES, F32)], name=name)


def _head_norm(t, g2, seg):
    r = lax.rsqrt(_chunkdot(t * t, seg) * (1.0 / GQA_HEAD_DIM) + NORM_EPS)
    return t * r * _lanes(g2, t.shape[1]), r


def _head_norm_bwd(t, g2, seg, dn):
    w = t.shape[1]
    r = lax.rsqrt(_chunkdot(t * t, seg) * (1.0 / GQA_HEAD_DIM) + NORM_EPS)
    u = dn * _lanes(g2, w)
    dt = r * u - t * (r * r * r) * (_chunkdot(u * t, seg) * (1.0 / GQA_HEAD_DIM))
    dgw = jnp.sum(dn * t * r, axis=0, keepdims=True)
    dg = dgw[:, :LANES]
    for c in range(LANES, w, LANES):
        dg = dg + dgw[:, c:c + LANES]
    return dt, dg


def _gqa_prep(q_raw, kv_raw, cs, gq2, gk2, seg, swap, name):
    kw = GQA_KV_HEADS * GQA_HEAD_DIM

    def fn(qv, kvv, csv, gqv, gkv, sg, sw):
        qn, _ = _head_norm(qv, gqv, sg)
        kn, _ = _head_norm(kvv[:, :kw], gkv, sg)
        return _rope(qn, csv, sw) * (GQA_SCALE * LOG2E), _rope(kn, csv, sw), kvv[:, kw:]
    return _rows_call(fn, [q_raw, kv_raw, cs], [gq2, gk2, seg, swap],
                      [(GQA_HEADS * GQA_HEAD_DIM, BF16), (kw, BF16), (kw, BF16)], name=name)


def _gqa_prep_bwd(q_raw, kv_raw, cs, dqh, dkh, dv, gq2, gk2, seg, swap, name):
    kw = GQA_KV_HEADS * GQA_HEAD_DIM

    def fn(qv, kvv, csv, dq, dk, dvv, gqv, gkv, sg, sw):
        dqr, dgq = _head_norm_bwd(qv, gqv, sg, _rope_t(dq * GQA_SCALE, csv, sw))
        dkr, dgk = _head_norm_bwd(kvv[:, :kw], gkv, sg, _rope_t(dk * LN2, csv, sw))
        return dqr, jnp.concatenate([dkr, dvv], axis=1), dgq, dgk
    return _rows_call(fn, [q_raw, kv_raw, cs, dqh, dkh, dv], [gq2, gk2, seg, swap],
                      [(GQA_HEADS * GQA_HEAD_DIM, BF16), (2 * kw, BF16)], [(1, LANES), (1, LANES)], name=name)


def _loss_head(x, target, g, name):
    dm = x.shape[1]

    def fn(xv, tv, gv):
        err = _rms(xv, gv) - tv
        loss = 0.5 * jnp.sum(err * err) / dm
        dx, dg = _rms_bwd_math(xv, gv, err * (1.0 / dm))
        return dx, jnp.zeros((1, LANES), F32) + loss, dg
    return _rows_call(fn, [x, target], [g], [(dm, F32)], [(1, LANES), (1, dm)], name=name)


def _adamw(w, g, m, v, name):
    def fn(wv, gv, mv, vv):
        m2 = ADAM_B1 * mv + (1.0 - ADAM_B1) * gv
        v2 = ADAM_B2 * vv + (1.0 - ADAM_B2) * (gv * gv)
        m_hat = m2 / (1.0 - ADAM_B1 ** ADAM_STEP)
        v_hat = v2 / (1.0 - ADAM_B2 ** ADAM_STEP)
        return -ADAM_LR * (m_hat / (jnp.sqrt(v_hat) + ADAM_EPS) + ADAM_WD * wv), m2, v2
    c = w.shape[1]
    return _rows_call(fn, [w, g, m, v], [], [(c, F32)] * 3, bs=_pick(w.shape[0], 256, 8), name=name)


HBM_SPEC = pl.BlockSpec(memory_space=pltpu.HBM)
VMEM_SPEC = pl.BlockSpec(memory_space=pltpu.VMEM)


def _position():
    return lax.axis_index("x"), lax.axis_index("y"), lax.axis_index("c")


def _other_chips(x, y):
    return [(1 - x, y), (x, 1 - y), (1 - x, 1 - y)]


HALF_W = PACK_W // 2


def _cols(c):
    return pl.ds(pl.multiple_of(c * HALF_W, HALF_W), HALF_W)


def _all_gather_weights(packed):
    rows = packed.shape[0]

    def body(p_ref, g_ref, send_sems, recv_sems, local_sem):
        x, y, c = _position()
        chips = _other_chips(x, y)

        def half(chip, hc):
            return g_ref.at[2 * chip[0] + chip[1], :, _cols(hc)]

        def copy(j, src, dst, to):
            return pltpu.make_async_remote_copy(src_ref=src, dst_ref=dst, send_sem=send_sems.at[j],
                                                recv_sem=recv_sems.at[j], device_id=to, device_id_type=MESH)

        mine = pltpu.make_async_copy(p_ref, g_ref.at[2 * x + y], local_sem)
        mine.start()
        first = [copy(j, p_ref.at[:, _cols(c)], half((x, y), c), (*chip, c)) for j, chip in enumerate(chips)]
        for cp in first:
            cp.start()
        passed = [copy(3 + j, half(chip, c), half(chip, c), (x, y, 1 - c)) for j, chip in enumerate(chips)]
        for j, chip in enumerate(chips):
            copy(j, half(chip, c), half(chip, c), (x, y, c)).wait_recv()
            passed[j].start()
        for j, chip in enumerate(chips):
            copy(3 + j, half(chip, 1 - c), half(chip, 1 - c), (x, y, c)).wait_recv()
        for cp in first + passed:
            cp.wait_send()
        mine.wait()

    return pl.pallas_call(
        body, name="all_gather_weights", in_specs=[HBM_SPEC], out_specs=HBM_SPEC,
        out_shape=jax.ShapeDtypeStruct((4, rows, packed.shape[1]), packed.dtype),
        scratch_shapes=[pltpu.SemaphoreType.DMA((6,)), pltpu.SemaphoreType.DMA((6,)), pltpu.SemaphoreType.DMA],
    )(packed)


SEM_SPEC = pl.BlockSpec(memory_space=pltpu.SEMAPHORE)
ANY_SPEC = pl.BlockSpec(memory_space=pl.ANY)
DATAFLOW = pltpu.SideEffectType.DATAFLOW_SIDE_EFFECTING


def _hbm(a):
    return pltpu.with_memory_space_constraint(a, pltpu.HBM)


def _gather_start(packed, tag):
    rows = packed.shape[0]

    def body(p_ref, g_ref, send_sems, recv_sems, p_thru, g_thru, token):
        x, y, c = _position()
        for j, chip in enumerate(_other_chips(x, y)):
            for s in range(2):
                pltpu.make_async_remote_copy(
                    src_ref=p_ref.at[:, _cols(c)], dst_ref=g_ref.at[2 * x + y, :, _cols(c)],
                    send_sem=send_sems.at[2 * j + s], recv_sem=recv_sems.at[2 * j + s],
                    device_id=(*chip, 1 - c if s else c), device_id_type=MESH).start()
        token[...] = jnp.zeros_like(token)

    return pl.pallas_call(
        body, name=tag + "_start",
        out_shape=(pltpu.SemaphoreType.DMA((6,)), pltpu.SemaphoreType.DMA((6,)), pltpu.HBM(packed.shape, packed.dtype),
                   pltpu.HBM((4, rows, PACK_W), packed.dtype), jax.ShapeDtypeStruct((8, LANES), F32)),
        in_specs=(HBM_SPEC, HBM_SPEC), out_specs=(SEM_SPEC, SEM_SPEC, HBM_SPEC, HBM_SPEC, VMEM_SPEC),
        input_output_aliases={0: 2, 1: 3},
        compiler_params=pltpu.CompilerParams(has_side_effects=DATAFLOW),
    )(_hbm(packed), _hbm(lax.empty((4, rows, PACK_W), packed.dtype)))


def _gather_wait(send_sems, recv_sems, p_thru, g_thru, after, tag):
    def body(p_ref, g_ref, send_sems, recv_sems, after_ref, p_dead, got_ref):
        x, y, c = _position()
        for j, chip in enumerate(_other_chips(x, y)):
            for s in range(2):
                cp = pltpu.make_async_remote_copy(
                    src_ref=p_ref.at[:, _cols(c)], dst_ref=g_ref.at[2 * chip[0] + chip[1], :, _cols(1 - c if s else c)],
                    send_sem=send_sems.at[2 * j + s], recv_sem=recv_sems.at[2 * j + s],
                    device_id=(x, y, c), device_id_type=MESH)
                cp.wait_send()
                cp.wait_recv()

    return pl.pallas_call(
        body, name=tag + "_wait",
        out_shape=(pltpu.HBM(p_thru.shape, p_thru.dtype), pltpu.HBM(g_thru.shape, g_thru.dtype)),
        in_specs=(HBM_SPEC, HBM_SPEC, SEM_SPEC, SEM_SPEC, ANY_SPEC), out_specs=(HBM_SPEC, HBM_SPEC),
        input_output_aliases={0: 0, 1: 1},
        compiler_params=pltpu.CompilerParams(has_side_effects=DATAFLOW),
    )(p_thru, g_thru, send_sems, recv_sems, after)[1]


def _sibling_swap_halves(grads, tag):
    rows = grads.shape[1]

    def body(g_ref, a_ref, send_sem, recv_sem):
        x, y, c = _position()
        cp = pltpu.make_async_remote_copy(src_ref=g_ref.at[:, :, _cols(1 - c)], dst_ref=a_ref,
                                          send_sem=send_sem, recv_sem=recv_sem, device_id=(x, y, 1 - c),
                                          device_id_type=MESH)
        cp.start()
        cp.wait()

    return pl.pallas_call(
        body, name=tag + "_swap_cores", in_specs=[HBM_SPEC], out_specs=HBM_SPEC,
        out_shape=jax.ShapeDtypeStruct((4, rows, HALF_W), grads.dtype),
        scratch_shapes=[pltpu.SemaphoreType.DMA, pltpu.SemaphoreType.DMA],
    )(grads)


def _rs_block(rows):
    return max(d for d in range(16, 1601, 16) if rows % d == 0)


def _chip_sum(grads, other, c, tag):
    rows = other.shape[1]
    rb = _rs_block(rows)

    def body(c_ref, g_ref, a_ref, o_ref):
        o_ref[...] = (g_ref[...] + a_ref[...]).astype(o_ref.dtype)

    blk = (1, rb, HALF_W)
    return pl.pallas_call(
        body, name=tag + "_chip_sum",
        grid_spec=pltpu.PrefetchScalarGridSpec(
            num_scalar_prefetch=1, grid=(4, rows // rb),
            in_specs=[pl.BlockSpec(blk, lambda k, i, c_ref: (k, i, c_ref[0])),
                      pl.BlockSpec(blk, lambda k, i, c_ref: (k, i, 0))],
            out_specs=pl.BlockSpec(blk, lambda k, i, c_ref: (k, i, 0))),
        out_shape=jax.ShapeDtypeStruct(other.shape, BF16),
        compiler_params=_cparams(("parallel", "parallel")),
    )(jnp.reshape(c, (1,)).astype(jnp.int32), grads, other)


def _chip_copies(t_ref, b_ref, send_sems, recv_sems):
    x, y, c = _position()
    return [pltpu.make_async_remote_copy(src_ref=t_ref.at[2 * chip[0] + chip[1]], dst_ref=b_ref.at[j],
                                         send_sem=send_sems.at[j], recv_sem=recv_sems.at[j],
                                         device_id=(*chip, c), device_id_type=MESH)
            for j, chip in enumerate(_other_chips(x, y))]


def _send_chip_sums(sums, tag):
    def body(t_ref, b_ref, send_sems, recv_sems):
        copies = _chip_copies(t_ref, b_ref, send_sems, recv_sems)
        for cp in copies:
            cp.start()
        for cp in copies:
            cp.wait()

    return pl.pallas_call(
        body, name=tag + "_send_chips", in_specs=[HBM_SPEC], out_specs=HBM_SPEC,
        out_shape=jax.ShapeDtypeStruct((3,) + sums.shape[1:], sums.dtype),
        scratch_shapes=[pltpu.SemaphoreType.DMA((3,)), pltpu.SemaphoreType.DMA((3,))],
    )(sums)


def _send_chip_sums_start(sums, tag):
    land = (3,) + sums.shape[1:]

    def body(t_ref, b_ref, send_sems, recv_sems, t_thru, b_thru, token):
        for cp in _chip_copies(t_ref, b_ref, send_sems, recv_sems):
            cp.start()
        token[...] = jnp.zeros_like(token)

    return pl.pallas_call(
        body, name=tag + "_send_chips_start",
        out_shape=(pltpu.SemaphoreType.DMA((3,)), pltpu.SemaphoreType.DMA((3,)), pltpu.HBM(sums.shape, sums.dtype),
                   pltpu.HBM(land, sums.dtype), jax.ShapeDtypeStruct((8, LANES), F32)),
        in_specs=(HBM_SPEC, HBM_SPEC), out_specs=(SEM_SPEC, SEM_SPEC, HBM_SPEC, HBM_SPEC, VMEM_SPEC),
        input_output_aliases={0: 2, 1: 3},
        compiler_params=pltpu.CompilerParams(has_side_effects=DATAFLOW),
    )(_hbm(sums), _hbm(lax.empty(land, sums.dtype)))


def _send_chip_sums_wait(send_sems, recv_sems, t_thru, b_thru, after, tag):
    def body(t_ref, b_ref, send_sems, recv_sems, after_ref, t_dead, got_ref):
        for cp in _chip_copies(t_ref, b_ref, send_sems, recv_sems):
            cp.wait_send()
            cp.wait_recv()

    return pl.pallas_call(
        body, name=tag + "_send_chips_wait",
        out_shape=(pltpu.HBM(t_thru.shape, t_thru.dtype), pltpu.HBM(b_thru.shape, b_thru.dtype)),
        in_specs=(HBM_SPEC, HBM_SPEC, SEM_SPEC, SEM_SPEC, ANY_SPEC), out_specs=(HBM_SPEC, HBM_SPEC),
        input_output_aliases={0: 0, 1: 1},
        compiler_params=pltpu.CompilerParams(has_side_effects=DATAFLOW),
    )(t_thru, b_thru, send_sems, recv_sems, after)[1]


def _final_sum(grads, other, recv, k, c, tag):
    rows = other.shape[1]
    rb = _rs_block(rows)

    def body(k_ref, c_ref, g_ref, a_ref, b_ref, o_ref):
        own = g_ref[0] + a_ref[0]
        o_ref[...] = ((own + b_ref[0].astype(F32)) + b_ref[1].astype(F32)) + b_ref[2].astype(F32)

    return pl.pallas_call(
        body, name=tag + "_final_sum",
        grid_spec=pltpu.PrefetchScalarGridSpec(
            num_scalar_prefetch=2, grid=(rows // rb,),
            in_specs=[pl.BlockSpec((1, rb, HALF_W), lambda i, k_ref, c_ref: (k_ref[0], i, c_ref[0])),
                      pl.BlockSpec((1, rb, HALF_W), lambda i, k_ref, c_ref: (k_ref[0], i, 0)),
                      pl.BlockSpec((3, rb, HALF_W), lambda i, k_ref, c_ref: (0, i, 0))],
            out_specs=pl.BlockSpec((rb, HALF_W), lambda i, k_ref, c_ref: (i, 0))),
        out_shape=jax.ShapeDtypeStruct((rows, HALF_W), F32),
        compiler_params=_cparams(("parallel",)),
    )(jnp.reshape(k, (1,)).astype(jnp.int32), jnp.reshape(c, (1,)).astype(jnp.int32), grads, other, recv)


def _join_halves(half, core, tag):
    def body(h_ref, o_ref, send_sem, recv_sem):
        x, y, c = _position()
        cp = pltpu.make_async_remote_copy(src_ref=h_ref, dst_ref=o_ref, send_sem=send_sem, recv_sem=recv_sem,
                                          device_id=(x, y, 1 - c), device_id_type=MESH)
        cp.start()
        cp.wait()

    other = pl.pallas_call(
        body, name=tag + "_join_cores", in_specs=[HBM_SPEC], out_specs=HBM_SPEC,
        out_shape=jax.ShapeDtypeStruct(half.shape, half.dtype),
        scratch_shapes=[pltpu.SemaphoreType.DMA, pltpu.SemaphoreType.DMA],
    )(half)
    first = core == 0
    return jnp.concatenate([jnp.where(first, half, other), jnp.where(first, other, half)], axis=1)


def _all_reduce_packet(packet):
    rows = packet.shape[0]

    def body(p_ref, o_ref, buf, send_sems, recv_sems):
        x, y, c = _position()
        me = 4 * x + 2 * y + c
        buf[me] = p_ref[...]

        def flip(v, bit):
            return 1 - v if bit else v

        for p in range(1, 8):
            peer = (flip(x, p & 4), flip(y, p & 2), flip(c, p & 1))
            pltpu.make_async_remote_copy(src_ref=p_ref, dst_ref=buf.at[me], send_sem=send_sems.at[p - 1],
                                         recv_sem=recv_sems.at[p - 1], device_id=peer, device_id_type=MESH).start()
        for p in range(1, 8):
            peer = (flip(x, p & 4), flip(y, p & 2), flip(c, p & 1))
            slot = 4 * peer[0] + 2 * peer[1] + peer[2]
            cp = pltpu.make_async_remote_copy(src_ref=p_ref, dst_ref=buf.at[slot], send_sem=send_sems.at[p - 1],
                                              recv_sem=recv_sems.at[p - 1], device_id=peer, device_id_type=MESH)
            cp.wait_recv()
            cp.wait_send()
        acc = buf[0]
        for dev in range(1, 8):
            acc = acc + buf[dev]
        o_ref[...] = acc

    return pl.pallas_call(
        body, name="all_reduce_packet", in_specs=[VMEM_SPEC], out_specs=VMEM_SPEC,
        out_shape=jax.ShapeDtypeStruct(packet.shape, F32),
        scratch_shapes=[pltpu.VMEM((8, rows, LANES), F32), pltpu.SemaphoreType.DMA((7,)),
                        pltpu.SemaphoreType.DMA((7,))],
    )(packet)


def _stack_range(name, n_stack, pack):
    if name.startswith('gqa'):
        return (0, 0) if pack == 0 else (0, n_stack)
    return (0, 1) if pack == 0 else (1, n_stack)


def _pack_members(pack):
    out = []
    for n, shape, ax in BIG:
        lo, hi = _stack_range(n, shape[0], pack)
        if hi > lo:
            out.append((n, (hi - lo,) + shape[1:], ax, (lo, hi)))
    return out


PACK_ROW_MULTIPLE = 512


def _pad_rows(parts, dtype):
    rows = sum(p.shape[0] for p in parts)
    pad = -rows % PACK_ROW_MULTIPLE
    return jnp.concatenate(parts + ([jnp.zeros((pad, PACK_W), dtype)] if pad else []), axis=0)


def _pack_blocks(blocks, dtype, pack):
    return _pad_rows([blocks[n][lo:hi].astype(dtype).reshape(-1, PACK_W)
                      for n, _, _, (lo, hi) in _pack_members(pack)], dtype)


def _unpack_blocks(packed, pack):
    out, off = {}, 0
    for n, shape, _, _ in _pack_members(pack):
        r = math.prod(shape) // PACK_W
        out[n] = packed[off:off + r].reshape(shape)
        off += r
    return out


def _unpack_gathered(gathered, pack, own=None, chip=None):
    blocks = [gathered[k] if own is None else jnp.where(chip == k, own, gathered[k]) for k in range(4)]
    per_chip = [_unpack_blocks(blocks[k], pack) for k in range(4)]
    return {n: jnp.concatenate([per_chip[k][n] for k in range(4)], axis=ax) for n, _, ax, _ in _pack_members(pack)}


def _pack_full(full, dtype, pack):
    chips = []
    for k in range(4):
        parts = []
        for n, shape, ax, _ in _pack_members(pack):
            blk = lax.slice_in_dim(full[n], k * shape[ax], (k + 1) * shape[ax], axis=ax)
            parts.append(blk.astype(dtype).reshape(-1, PACK_W))
        chips.append(_pad_rows(parts, dtype))
    return jnp.stack(chips, axis=0)


def _pack_small(vals, loss_row):
    rows = [loss_row.reshape(1, LANES)]
    for n, shape in SMALL:
        v = vals.get(n)
        v = jnp.zeros(shape, F32) if v is None else v
        rows.append(v.astype(F32).reshape(-1, LANES))
    packet = jnp.concatenate(rows, axis=0)
    return jnp.pad(packet, ((0, PACKET_ROWS - packet.shape[0]), (0, 0)))


def _unpack_small(packet):
    out, off = {}, 1
    for n, shape in SMALL:
        r = math.prod(shape) // LANES
        out[n] = packet[off:off + r].reshape(shape)
        off += r
    return packet[0, 0], out


_MLA = dict(R=1, dk=MLA_DK, dv=MLA_V, hb=2, bq=512, bk=512)
_MLA_FWD_BQ = 1024
_BWD_UNROLL = 8
_GQA = dict(R=GQA_HEADS // GQA_KV_HEADS, dk=GQA_HEAD_DIM, dv=GQA_HEAD_DIM, hb=2, bq=256, bk=512)


def _layer_params(layer, full, gains):
    pack = 0 if layer == 0 else 1
    i = layer // 2

    def mat(name):
        lo, _ = _stack_range(name, 4 if name.startswith('ffn') else 2, pack)
        return full[name][(layer if name.startswith('ffn') else i) - lo]

    p = dict(ffn_norm=gains['ffn_norm'][layer][None], ffn_w_in=mat('ffn_w_in'), ffn_w_out=mat('ffn_w_out'))
    if layer % 2 == 0:
        w_in = mat('w_in_ab')
        zeros = jnp.zeros((D_MODEL, 32), w_in.dtype)
        p['w_a'] = jnp.concatenate([w_in[:, :640], zeros, zeros, w_in[:, 640:IN_A], zeros], axis=1)
        p['w_b'] = w_in[:, IN_A:]
        p['w_uq'] = jnp.pad(mat('mla_w_uq'), ((0, 0), (0, 0), (0, MLA_DK - 96))).reshape(MLA_Q_RANK, -1)
        ukv = mat('mla_w_ukv')
        p['w_uk'] = jnp.pad(ukv[:, :, :MLA_NOPE], ((0, 0), (0, 0), (0, MLA_DK - MLA_NOPE))).reshape(MLA_KV_RANK, -1)
        p['w_uv'] = ukv[:, :, MLA_NOPE:].reshape(MLA_KV_RANK, -1)
        p['w_out'] = mat('w_out_ab')
        p['mix_norm'] = gains['mix_norm_ab'][i][None]
        p['q_norm'] = gains['mla_q_norm'][i][None]
        p['kv_norm'] = gains['mla_kv_norm'][i][None]
    else:
        p['w_q'], p['w_kv'], p['w_o'] = mat('gqa_w_q'), mat('gqa_w_kv'), mat('gqa_w_o')
        p['mix_norm'] = gains['mix_norm_c'][i][None]
        p['q_norm'] = jnp.tile(gains['gqa_q_norm'][i][None], (1, 2))
        p['k_norm'] = jnp.tile(gains['gqa_k_norm'][i][None], (1, 2))
    return p


def _even_fwd(x, p, cs, swap, tag):
    xn = _rms_fwd(x, p['mix_norm'], tag + "_norm")
    za = _mm(xn, p['w_a'], name=tag + "_in_a")
    zb = _mm(xn, p['w_b'], out_dtype=BF16, name=tag + "_in_b")
    cq, ckv, krr = _mla_prep(za, p['q_norm'], p['kv_norm'], cs, swap, tag + "_mla_prep")
    q_raw = _mm(cq, p['w_uq'], name=tag + "_uq")
    k_pad = _mm(ckv, p['w_uk'], name=tag + "_uk")
    v = _mm(ckv, p['w_uv'], out_dtype=BF16, name=tag + "_uv")
    qh, kh = _mla_qk(q_raw, k_pad, krr, cs, swap, tag + "_mla_qk")
    o_a, lse_a = _flash_fwd(qh, kh, v, name=tag + "_mla_attn", **dict(_MLA, bq=_MLA_FWD_BQ))
    og, lg = [], []
    for grp in range(DIL_GROUPS):
        o, l = _dil_fwd(zb, grp, f"{tag}_dil{grp}")
        og.append(o)
        lg.append(l)
    o_b, lt = _dil_combine(og, lg, tag + "_dil_merge")
    ocat = jnp.concatenate([o_a, o_b], axis=1)
    x1 = _mm(ocat, p['w_out'], add=x, norm_gain=p['ffn_norm'], name=tag + "_out")
    saved = dict(x=x, xn=xn, za=za, zb=zb, cq=cq, ckv=ckv, qh=qh, kh=kh, v=v, lse_a=lse_a, og=og, lg=lg, lt=lt,
                 ocat=ocat)
    return x1, saved


def _even_bwd(dx1, p, sv, cs, swap, seg64, tag):
    docat = _mm(dx1, p['w_out'], mode="nt", name=tag + "_out_dx")
    d_w_out = _mm(sv['ocat'], dx1, mode="tn", name=tag + "_out_dw")
    n_a = MLA_HEADS * MLA_V
    do_a = docat[:, :n_a].astype(BF16)
    res = _dil_combine_bwd(docat[:, n_a:], sv['og'], sv['lg'], sv['lt'], seg64, tag + "_dil_merge_bwd")
    dqs, dks, dvs = [], [], []
    for grp in range(DIL_GROUPS):
        dq, dk, dv = _dil_bwd(sv['zb'], res[grp], sv['lg'][grp], res[3 + grp], grp, f"{tag}_dil{grp}_bwd")
        dqs.append(dq)
        dks.append(dk)
        dvs.append(dv)
    dzb = jnp.concatenate(dqs + dks + dvs, axis=1).astype(BF16)
    dqh, dkh, dv = _flash_bwd(sv['qh'], sv['kh'], sv['v'], sv['ocat'][:, :n_a], do_a, sv['lse_a'],
                              name=tag + "_mla_attn_bwd", unroll=_BWD_UNROLL, **_MLA)
    dq_raw, dkh, dkrr = _mla_qk_bwd(dqh, dkh, cs, swap, tag + "_mla_qk_bwd")
    dcq = _mm(dq_raw, p['w_uq'], mode="nt", name=tag + "_uq_dx")
    d_w_uq = _mm(sv['cq'], dq_raw, mode="tn", name=tag + "_uq_dw")
    dckv = _mm(dkh, p['w_uk'], mode="nt", name=tag + "_uk_dx")
    dckv = _mm(dv, p['w_uv'], mode="nt", add=dckv, name=tag + "_uv_dx")
    d_w_uk = _mm(sv['ckv'], dkh, mode="tn", name=tag + "_uk_dw")
    d_w_uv = _mm(sv['ckv'], dv, mode="tn", name=tag + "_uv_dw")
    dza, d_gq, d_gkv = _mla_prep_bwd(sv['za'], cs, dcq, dckv, dkrr, p['q_norm'], p['kv_norm'], swap,
                                     tag + "_mla_prep_bwd")
    dxn = _mm(dza, p['w_a'], mode="nt", name=tag + "_in_a_dx")
    dxn = _mm(dzb, p['w_b'], mode="nt", add=dxn, name=tag + "_in_b_dx")
    d_w_a = _mm(sv['xn'], dza, mode="tn", name=tag + "_in_a_dw")
    d_w_b = _mm(sv['xn'], dzb, mode="tn", name=tag + "_in_b_dw")
    dx, d_g = _rms_bwd(sv['x'], p['mix_norm'], dxn, dx1, tag + "_norm_bwd")
    d_w_in = jnp.concatenate([d_w_a[:, :640], d_w_a[:, 704:736], d_w_b], axis=1)
    d_uq = d_w_uq.reshape(MLA_Q_RANK, MLA_HEADS, MLA_DK)[:, :, :MLA_NOPE + MLA_ROPE]
    d_ukv = jnp.concatenate([d_w_uk.reshape(MLA_KV_RANK, MLA_HEADS, MLA_DK)[:, :, :MLA_NOPE],
                             d_w_uv.reshape(MLA_KV_RANK, MLA_HEADS, MLA_V)], axis=2)
    grads = dict(w_in_ab=d_w_in, mla_w_uq=d_uq, mla_w_ukv=d_ukv, w_out_ab=d_w_out, mix_norm_ab=d_g[0],
                 mla_q_norm=d_gq[0], mla_kv_norm=d_gkv[0])
    return dx, grads


def _odd_fwd(x, p, cs, seg64, swap, tag):
    xn = _rms_fwd(x, p['mix_norm'], tag + "_norm")
    q_raw = _mm(xn, p['w_q'], name=tag + "_q")
    kv_raw = _mm(xn, p['w_kv'], name=tag + "_kv")
    qh, kh, v = _gqa_prep(q_raw, kv_raw, cs, p['q_norm'], p['k_norm'], seg64, swap, tag + "_gqa_prep")
    o, lse = _flash_fwd(qh, kh, v, name=tag + "_gqa_attn", **_GQA)
    x1 = _mm(o, p['w_o'], add=x, norm_gain=p['ffn_norm'], name=tag + "_o")
    return x1, dict(x=x, xn=xn, q_raw=q_raw, kv_raw=kv_raw, qh=qh, kh=kh, v=v, o=o, lse=lse)


def _odd_bwd(dx1, p, sv, cs, seg64, swap, tag):
    do = _mm(dx1, p['w_o'], mode="nt", out_dtype=BF16, name=tag + "_o_dx")
    d_w_o = _mm(sv['o'], dx1, mode="tn", name=tag + "_o_dw")
    dqh, dkh, dv = _flash_bwd(sv['qh'], sv['kh'], sv['v'], sv['o'], do, sv['lse'], name=tag + "_gqa_attn_bwd",
                              unroll=_BWD_UNROLL, **_GQA)
    dq_raw, dkv_raw, d_gq, d_gk = _gqa_prep_bwd(sv['q_raw'], sv['kv_raw'], cs, dqh, dkh, dv, p['q_norm'],
                                                p['k_norm'], seg64, swap, tag + "_gqa_prep_bwd")
    dxn = _mm(dq_raw, p['w_q'], mode="nt", name=tag + "_q_dx")
    dxn = _mm(dkv_raw, p['w_kv'], mode="nt", add=dxn, name=tag + "_kv_dx")
    d_w_q = _mm(sv['xn'], dq_raw, mode="tn", name=tag + "_q_dw")
    d_w_kv = _mm(sv['xn'], dkv_raw, mode="tn", name=tag + "_kv_dw")
    dx, d_g = _rms_bwd(sv['x'], p['mix_norm'], dxn, dx1, tag + "_norm_bwd")
    grads = dict(gqa_w_q=d_w_q, gqa_w_kv=d_w_kv, gqa_w_o=d_w_o, mix_norm_c=d_g[0],
                 gqa_q_norm=d_gq[0, :GQA_HEAD_DIM] + d_gq[0, GQA_HEAD_DIM:],
                 gqa_k_norm=d_gk[0, :GQA_HEAD_DIM] + d_gk[0, GQA_HEAD_DIM:])
    return dx, grads


def _ffn_fwd(x_and_xn, p, tag):
    x, xn = x_and_xn
    h = _mm(xn, p['ffn_w_in'], out_dtype=BF16, name=tag + "_ffn_in")
    x2 = _mm(h, p['ffn_w_out'], gated=True, add=x, name=tag + "_ffn_out")
    return x2, dict(x=x, xn=xn, h=h)


def _ffn_bwd(dx2, p, sv, tag):
    d_gate, d_up = _mm(dx2, p['ffn_w_out'], mode="nt", gate_up=sv['h'], out_dtype=BF16, name=tag + "_ffn_out_dx")
    d_w_out = _mm(sv['h'], dx2, mode="tn", gated=True, name=tag + "_ffn_out_dw")
    d_w_in = jnp.concatenate([_mm(sv['xn'], d_gate, mode="tn", name=tag + "_ffn_in_dw_gate"),
                              _mm(sv['xn'], d_up, mode="tn", name=tag + "_ffn_in_dw_up")], axis=1)
    dxn = _mm(d_gate, p['ffn_w_in'], mode="nt", b_part=(0, 2), name=tag + "_ffn_in_dx_gate")
    dxn = _mm(d_up, p['ffn_w_in'], mode="nt", b_part=(1, 2), add=dxn, name=tag + "_ffn_in_dx_up")
    dx, d_g = _rms_bwd(sv['x'], p['ffn_norm'], dxn, dx2, tag + "_ffn_norm_bwd")
    return dx, d_w_in, d_w_out, d_g[0]


EVEN_MATS = ('w_in_ab', 'mla_w_uq', 'mla_w_ukv', 'w_out_ab')
ODD_MATS = ('gqa_w_q', 'gqa_w_kv', 'gqa_w_o')
FFN_MATS = ('ffn_w_in', 'ffn_w_out')


def _schedule(x, target, gains, full_of_pack, rest_grads_ready):
    s = x.shape[0]
    cs_mla, cs_gqa = _rope_tables(s)
    swap, seg64 = _swap_matrix(), _seg_matrix(GQA_HEAD_DIM)
    params, saved, full = [], [], None
    for layer in range(4):
        tag = f"l{layer}"
        if layer < 2:
            full = full_of_pack(layer, x)
        p = _layer_params(layer, full, gains)
        if layer % 2 == 0:
            x, sv = _even_fwd(x, p, cs_mla, swap, tag)
        else:
            x, sv = _odd_fwd(x, p, cs_gqa, seg64, swap, tag)
        x, sv_f = _ffn_fwd(x, p, tag)
        params.append(p)
        saved.append((sv, sv_f))
    dx, loss_row, d_final = _loss_head(x, target, gains['final_norm'][None], "loss_head")

    per_layer, rest = {}, None
    for layer in reversed(range(4)):
        p, (sv, sv_f), tag = params[layer], saved[layer], f"l{layer}"
        if layer == 0:
            rest = {n: per_layer[2][n][None] for n in EVEN_MATS}
            rest.update({n: jnp.stack([per_layer[1][n], per_layer[3][n]], axis=0) for n in ODD_MATS})
            rest.update({n: jnp.stack([per_layer[l][n] for l in (1, 2, 3)], axis=0) for n in FFN_MATS})
            token = rest_grads_ready(rest)
            if token is not None:
                p = dict(p, ffn_w_out=p['ffn_w_out'] + token[0, 0].astype(p['ffn_w_out'].dtype))
        dx, d_ffn_in, d_ffn_out, d_ffn_g = _ffn_bwd(dx, p, sv_f, tag)
        if layer % 2 == 0:
            dx, g = _even_bwd(dx, p, sv, cs_mla, swap, seg64, tag)
        else:
            dx, g = _odd_bwd(dx, p, sv, cs_gqa, seg64, swap, tag)
        g.update(ffn_w_in=d_ffn_in, ffn_w_out=d_ffn_out, ffn_norm=d_ffn_g)
        per_layer[layer] = g

    first = {n: per_layer[0][n][None] for n in EVEN_MATS + FFN_MATS}
    small = {'final_norm': d_final[0], 'ffn_norm': jnp.stack([per_layer[l]['ffn_norm'] for l in range(4)], axis=0)}
    for n in ('mix_norm_ab', 'mla_q_norm', 'mla_kv_norm'):
        small[n] = jnp.stack([per_layer[0][n], per_layer[2][n]], axis=0)
    for n in ('mix_norm_c', 'gqa_q_norm', 'gqa_k_norm'):
        small[n] = jnp.stack([per_layer[1][n], per_layer[3][n]], axis=0)
    return loss_row, dx, first, rest, small


def _core_sums(grads, pack, core, tag):
    packed = _pack_full(grads, F32, pack)
    other = _sibling_swap_halves(packed, tag)
    return packed, other, _chip_sum(packed, other, core, tag)


def _finish_reduce_scatter(packed, other, recv, pack, chip, core, tag):
    return _unpack_blocks(_join_halves(_final_sum(packed, other, recv, chip, core, tag), core, tag), pack)


def _step(x, target, w, m, v):
    big_names = [n for n, _, _ in BIG]
    chip = 2 * lax.axis_index("x") + lax.axis_index("y")
    core = lax.axis_index("c")

    gains = {n: w[n] for n, _ in SMALL if n != 'mix_norm_c'}
    c_cols = w['mix_norm_c'].shape[1]
    own_c = lax.dynamic_update_slice(jnp.zeros((2, 4 * c_cols), F32), w['mix_norm_c'], (0, chip * c_cols))
    gains['mix_norm_c'] = _unpack_small(_all_reduce_packet(_pack_small(
        {'mix_norm_c': own_c * 0.5}, jnp.zeros((LANES,), F32))))[1]['mix_norm_c']

    gathered0 = _all_gather_weights(_pack_blocks(w, BF16, 0))
    packed1, gathered0 = lax.optimization_barrier((_pack_blocks(w, BF16, 1), gathered0))
    ag_send, ag_recv, p_thru, g_thru, ag_token = _gather_start(packed1, "gather_rest")
    gains['mix_norm_ab'] = gains['mix_norm_ab'] + ag_token[0, 0]
    full0 = _unpack_gathered(gathered0, 0)

    def full_of_pack(pack, after):
        if pack == 0:
            return full0
        landed = _gather_wait(ag_send, ag_recv, p_thru, g_thru, after, "gather_rest")
        return _unpack_gathered(landed, 1, own=packed1, chip=chip)

    rs = {}

    def rest_grads_ready(rest):
        rs['packed'], rs['other'], sums = _core_sums(rest, 1, core, "grad_rest")
        rs['send'], rs['recv'], rs['t'], rs['b'], token = _send_chip_sums_start(sums, "grad_rest")
        return token

    loss_row, dx, first, rest, small = _schedule(x[0], target[0], gains, full_of_pack, rest_grads_ready)
    recv1 = _send_chip_sums_wait(rs['send'], rs['recv'], rs['t'], rs['b'], dx, "grad_rest")
    g_rest = _finish_reduce_scatter(rs['packed'], rs['other'], recv1, 1, chip, core, "grad_rest")
    packed0, other0, sums0 = _core_sums(first, 0, core, "grad_first")
    g_first = _finish_reduce_scatter(packed0, other0, _send_chip_sums(sums0, "grad_first"), 0, chip, core,
                                     "grad_first")
    g_blocks = {n: (jnp.concatenate([g_first[n], g_rest[n]], axis=0) if n in g_first else g_rest[n])
                for n in big_names}

    loss, g_small = _unpack_small(_all_reduce_packet(_pack_small(small, loss_row[0])))
    g_small['mix_norm_c'] = lax.dynamic_slice(g_small['mix_norm_c'], (0, chip * c_cols), (2, c_cols))

    out_g, out_d, out_m, out_v = {}, {}, {}, {}
    for n in big_names:
        shape = w[n].shape
        cols = shape[-1]
        d_, m_, v_ = _adamw(w[n].reshape(-1, cols), g_blocks[n].reshape(-1, cols), m[n].reshape(-1, cols),
                            v[n].reshape(-1, cols), "adamw_" + n)
        out_g[n], out_d[n], out_m[n], out_v[n] = g_blocks[n], d_.reshape(shape), m_.reshape(shape), v_.reshape(shape)
    for n, _ in SMALL:
        shape = w[n].shape
        as2d = (lambda t: t.reshape(1, -1)) if len(shape) == 1 else (lambda t: t)
        d_, m_, v_ = _adamw(as2d(w[n]), as2d(g_small[n]), as2d(m[n]), as2d(v[n]), "adamw_" + n)
        out_g[n], out_d[n], out_m[n], out_v[n] = g_small[n], d_.reshape(shape), m_.reshape(shape), v_.reshape(shape)
    return (loss, dx[None], *[out_g[n] for n in WEIGHTS], *[out_d[n] for n in WEIGHTS],
            *[out_m[n] for n in WEIGHTS], *[out_v[n] for n in WEIGHTS])


def kernel(x, mix_norm_ab, w_in_ab, mla_q_norm, mla_kv_norm, mla_w_uq, mla_w_ukv, w_out_ab, mix_norm_c, gqa_w_q, gqa_w_kv, gqa_q_norm, gqa_k_norm, gqa_w_o, ffn_norm, ffn_w_in, ffn_w_out, final_norm, loss_target, m_mix_norm_ab, m_w_in_ab, m_mla_q_norm, m_mla_kv_norm, m_mla_w_uq, m_mla_w_ukv, m_w_out_ab, m_mix_norm_c, m_gqa_w_q, m_gqa_w_kv, m_gqa_q_norm, m_gqa_k_norm, m_gqa_w_o, m_ffn_norm, m_ffn_w_in, m_ffn_w_out, m_final_norm, v_mix_norm_ab, v_w_in_ab, v_mla_q_norm, v_mla_kv_norm, v_mla_w_uq, v_mla_w_ukv, v_w_out_ab, v_mix_norm_c, v_gqa_w_q, v_gqa_w_kv, v_gqa_q_norm, v_gqa_k_norm, v_gqa_w_o, v_ffn_norm, v_ffn_w_in, v_ffn_w_out, v_final_norm):
    w = dict(zip(WEIGHTS, (mix_norm_ab, w_in_ab, mla_q_norm, mla_kv_norm, mla_w_uq, mla_w_ukv, w_out_ab, mix_norm_c,
                           gqa_w_q, gqa_w_kv, gqa_q_norm, gqa_k_norm, gqa_w_o, ffn_norm, ffn_w_in, ffn_w_out,
                           final_norm)))
    m = dict(zip(WEIGHTS, (m_mix_norm_ab, m_w_in_ab, m_mla_q_norm, m_mla_kv_norm, m_mla_w_uq, m_mla_w_ukv,
                           m_w_out_ab, m_mix_norm_c, m_gqa_w_q, m_gqa_w_kv, m_gqa_q_norm, m_gqa_k_norm, m_gqa_w_o,
                           m_ffn_norm, m_ffn_w_in, m_ffn_w_out, m_final_norm)))
    v = dict(zip(WEIGHTS, (v_mix_norm_ab, v_w_in_ab, v_mla_q_norm, v_mla_kv_norm, v_mla_w_uq, v_mla_w_ukv,
                           v_w_out_ab, v_mix_norm_c, v_gqa_w_q, v_gqa_w_kv, v_gqa_q_norm, v_gqa_k_norm, v_gqa_w_o,
                           v_ffn_norm, v_ffn_w_in, v_ffn_w_out, v_final_norm)))
    return _step(x, loss_target, w, m, v)
```
